```python
import jax, jax.numpy as jnp
from jax import lax
import numpy as np

D_MODEL = 1024
BATCH = 8
SEQ = 4096
DEPTH = 1

ATT_HEADS = 8
ATT_HEAD_DIM = 64
ATT_WIDTH = ATT_HEADS * ATT_HEAD_DIM
Q_BLOCK = 128
FORGET_BIAS_CENTER = 2.0
GM_GROUPS = 4
GM_GROUP_DIM = 128
GM_WIDTH = GM_GROUPS * GM_GROUP_DIM
GM_CHUNK = 128
PEER_HEADS = 8
PEER_KEY_DIM = 256
PEER_HALF = PEER_KEY_DIM // 2
N_KEYS = 128
N_EXPERTS = N_KEYS * N_KEYS
PEER_TOPK = 16
PEER_TOK_BLOCK = 128
IN_COLS = 3 * ATT_WIDTH + ATT_HEADS + 2 * GM_WIDTH + 2 * D_MODEL
SPLIT_POINTS = (ATT_WIDTH, 2 * ATT_WIDTH, 3 * ATT_WIDTH, 3 * ATT_WIDTH + ATT_HEADS,
                3 * ATT_WIDTH + ATT_HEADS + 2 * GM_WIDTH,
                3 * ATT_WIDTH + ATT_HEADS + 2 * GM_WIDTH + D_MODEL)
EPS = 1e-6

kernel_name = "fox_gmlp_peer_adaln_hybrid"


def rmsnorm(x, g):
    xf = x.astype(jnp.float32)
    y = xf * lax.rsqrt(jnp.mean(xf * xf, axis=-1, keepdims=True) + EPS)
    return (y * g.astype(jnp.float32)).astype(x.dtype)


def layernorm(x, g):
    xf = x.astype(jnp.float32)
    mu = jnp.mean(xf, axis=-1, keepdims=True)
    var = jnp.mean(jnp.square(xf - mu), axis=-1, keepdims=True)
    return ((xf - mu) * lax.rsqrt(var + EPS) * g.astype(jnp.float32)).astype(x.dtype)


def modulate(h, shift, scale):
    return h * (1.0 + scale[:, None, :]) + shift[:, None, :]


def fox_attention(q, k, v, log_f):
    B, S, H, Dh = q.shape
    nb = S // Q_BLOCK
    cum = jnp.cumsum(log_f.astype(jnp.float32), axis=1)
    cum_k = jnp.transpose(cum, (0, 2, 1))
    k_pos = jnp.arange(S)
    q_blocks = q.reshape(B, nb, Q_BLOCK, H, Dh).transpose(1, 0, 2, 3, 4)
    cq_blocks = cum_k.reshape(B, H, nb, Q_BLOCK).transpose(2, 0, 1, 3)
    starts = jnp.arange(nb) * Q_BLOCK
    scale = ATT_HEAD_DIM ** -0.5

    def one_block(args):
        q_blk, cq_blk, start = args
        s = jnp.einsum('bqhd,bkhd->bhqk', q_blk, k).astype(jnp.float32) * scale
        s = s + cq_blk[..., :, None] - cum_k[..., None, :]
        q_pos = start + jnp.arange(Q_BLOCK)
        causal = q_pos[:, None] >= k_pos[None, :]
        s = jnp.where(causal[None, None], s, -jnp.inf)
        p = jax.nn.softmax(s, axis=-1).astype(v.dtype)
        return jnp.einsum('bhqk,bkhd->bqhd', p, v)

    out = lax.map(one_block, (q_blocks, cq_blocks, starts))
    return out.transpose(1, 0, 2, 3, 4).reshape(B, S, H * Dh)


def gmlp_spatial_gating(z, ln_g, w_s, b_s):
    B, S, _ = z.shape
    u, v = jnp.split(z, 2, axis=-1)
    v = layernorm(v, ln_g)
    nc = S // GM_CHUNK
    v = v.reshape(B, nc, GM_CHUNK, GM_GROUPS, GM_GROUP_DIM)
    tril = jnp.tril(jnp.ones((GM_CHUNK, GM_CHUNK), dtype=bool))
    w = jnp.where(tril[None], w_s, jnp.zeros_like(w_s))
    mixed = jnp.einsum('gts,bcsgd->bctgd', w, v) + b_s.T[None, None, :, :, None]
    return u * mixed.reshape(B, S, GM_WIDTH)


def peer(h, w_query, sub_keys, expert_u, expert_v):
    B, S, D = h.shape
    n = B * S
    hf = h.reshape(n, D)
    q = (hf @ w_query).reshape(n, PEER_HEADS, 2, PEER_HALF)
    scores = jnp.einsum('nhpd,hpkd->nhpk', q, sub_keys).astype(jnp.float32)
    s_top, i_top = lax.top_k(scores, PEER_TOPK)
    cand_s = (s_top[:, :, 0, :, None] + s_top[:, :, 1, None, :]).reshape(n, PEER_HEADS, PEER_TOPK * PEER_TOPK)
    cand_i = (i_top[:, :, 0, :, None] * N_KEYS + i_top[:, :, 1, None, :]).reshape(n, PEER_HEADS, PEER_TOPK * PEER_TOPK)
    best_s, best_pos = lax.top_k(cand_s, PEER_TOPK)
    idx = jnp.take_along_axis(cand_i, best_pos, axis=-1)
    gates = jax.nn.softmax(best_s, axis=-1).astype(h.dtype)
    nblk = n // PEER_TOK_BLOCK

    def one_block(args):
        h_blk, idx_blk, g_blk = args
        u = jnp.take(expert_u, idx_blk, axis=0)
        act = jax.nn.gelu(jnp.einsum('thkd,td->thk', u, h_blk))
        vv = jnp.take(expert_v, idx_blk, axis=0)
        return jnp.einsum('thk,thkd->td', g_blk * act, vv)

    out = lax.map(one_block, (hf.reshape(nblk, PEER_TOK_BLOCK, D),
                              idx.reshape(nblk, PEER_TOK_BLOCK, PEER_HEADS, PEER_TOPK),
                              gates.reshape(nblk, PEER_TOK_BLOCK, PEER_HEADS, PEER_TOPK)))
    return out.reshape(B, S, D)


def setup_inputs(seed: int = 0) -> dict:
    key = jax.random.key(seed)
    ks = jax.random.split(key, 20)
    nrm = jax.random.normal
    L, D = DEPTH, D_MODEL
    return {
        "x": nrm(ks[0], (BATCH, SEQ, D), jnp.float32),
        "c": nrm(ks[1], (BATCH, D), jnp.float32),
        "w_mod": nrm(ks[2], (L, D, 6 * D), jnp.float32) * (0.5 * D ** -0.5),
        "b_mod": nrm(ks[3], (L, 6 * D), jnp.float32) * 0.02,
        "norm1_g": 1.0 + 0.05 * nrm(ks[4], (L, D), jnp.float32),
        "w_in": nrm(ks[5], (L, D, IN_COLS), jnp.float32) * D ** -0.5,
        "b_forget": FORGET_BIAS_CENTER + 0.5 * nrm(ks[6], (L, ATT_HEADS), jnp.float32),
        "ln_v_g": 1.0 + 0.05 * nrm(ks[7], (L, GM_WIDTH), jnp.float32),
        "w_spatial": nrm(ks[8], (L, GM_GROUPS, GM_CHUNK, GM_CHUNK), jnp.float32) * GM_CHUNK ** -0.5,
        "b_spatial": 1.0 + 0.1 * nrm(ks[9], (L, GM_GROUPS, GM_CHUNK), jnp.float32),
        "w_branch_a": nrm(ks[10], (L, ATT_WIDTH, D), jnp.float32) * ATT_WIDTH ** -0.5,
        "w_branch_b": nrm(ks[11], (L, GM_WIDTH, D), jnp.float32) * GM_WIDTH ** -0.5,
        "w_out": nrm(ks[12], (L, D, D), jnp.float32) * D ** -0.5,
        "norm2_g": 1.0 + 0.05 * nrm(ks[13], (L, D), jnp.float32),
        "w_query": nrm(ks[14], (L, D, PEER_HEADS * PEER_KEY_DIM), jnp.float32) * D ** -0.5,
        "sub_keys": nrm(ks[15], (L, PEER_HEADS, 2, N_KEYS, PEER_HALF), jnp.float32) * PEER_HALF ** -0.5,
        "expert_u": nrm(ks[16], (L, N_EXPERTS, D), jnp.float32) * D ** -0.5,
        "expert_v": nrm(ks[17], (L, N_EXPERTS, D), jnp.float32) * PEER_HEADS ** -0.5,
        "final_g": 1.0 + 0.05 * nrm(ks[18], (D,), jnp.float32),
    }


def reference(x, c, w_mod, b_mod, norm1_g, w_in, b_forget, ln_v_g, w_spatial, b_spatial,
              w_branch_a, w_branch_b, w_out, norm2_g, w_query, sub_keys, expert_u, expert_v,
              final_g):
    B, S, D = x.shape
    for l in range(DEPTH):
        mod = jax.nn.silu(c) @ w_mod[l] + b_mod[l]
        sh1, sc1, g1, sh2, sc2, g2 = jnp.split(mod, 6, axis=-1)

        h = modulate(rmsnorm(x, norm1_g[l]), sh1, sc1)
        proj = h @ w_in[l]
        q, k, v, f_logit, z, gate_a, gate_b = jnp.split(proj, SPLIT_POINTS, axis=-1)
        log_f = jax.nn.log_sigmoid(f_logit + b_forget[l])
        y_a = fox_attention(q.reshape(B, S, ATT_HEADS, ATT_HEAD_DIM),
                            k.reshape(B, S, ATT_HEADS, ATT_HEAD_DIM),
                            v.reshape(B, S, ATT_HEADS, ATT_HEAD_DIM), log_f)
        y_b = gmlp_spatial_gating(jax.nn.gelu(z), ln_v_g[l], w_spatial[l], b_spatial[l])
        merged = (jax.nn.sigmoid(gate_a) * (y_a @ w_branch_a[l])
                  + jax.nn.sigmoid(gate_b) * (y_b @ w_branch_b[l]))
        x = x + g1[:, None, :] * (merged @ w_out[l])

        h2 = modulate(rmsnorm(x, norm2_g[l]), sh2, sc2)
        x = x + g2[:, None, :] * peer(h2, w_query[l], sub_keys[l], expert_u[l], expert_v[l])
    return rmsnorm(x, final_g)
```

```python
import jax
import jax.numpy as jnp
from jax import lax
from jax.experimental import pallas as pl

D_MODEL = 1024
ATT_HEADS = 8
ATT_HEAD_DIM = 64
ATT_WIDTH = ATT_HEADS * ATT_HEAD_DIM
Q_BLOCK = 128
GM_GROUPS = 4
GM_GROUP_DIM = 128
GM_WIDTH = GM_GROUPS * GM_GROUP_DIM
GM_CHUNK = 128
PEER_HEADS = 8
PEER_KEY_DIM = 256
PEER_HALF = PEER_KEY_DIM // 2
N_KEYS = 128
PEER_TOPK = 16
PEER_TOK_BLOCK = 128
SPLIT_POINTS = (ATT_WIDTH, 2 * ATT_WIDTH, 3 * ATT_WIDTH, 3 * ATT_WIDTH + ATT_HEADS,
                3 * ATT_WIDTH + ATT_HEADS + 2 * GM_WIDTH,
                3 * ATT_WIDTH + ATT_HEADS + 2 * GM_WIDTH + D_MODEL)
EPS = 1e-6


def _rmsnorm(x, g):
    y = x * lax.rsqrt(jnp.mean(x * x, axis=-1, keepdims=True) + EPS)
    return y * g


def _layernorm(x, g):
    mu = jnp.mean(x, axis=-1, keepdims=True)
    var = jnp.mean(jnp.square(x - mu), axis=-1, keepdims=True)
    return (x - mu) * lax.rsqrt(var + EPS) * g


def _modulate(h, shift, scale):
    return h * (1.0 + scale[:, None, :]) + shift[:, None, :]


def _fox(q, k, v, log_f):
    B, S, H, Dh = q.shape
    nb = S // Q_BLOCK
    cum = jnp.cumsum(log_f, axis=1)
    cum_k = jnp.transpose(cum, (0, 2, 1))
    k_pos = jnp.arange(S)
    q_blocks = q.reshape(B, nb, Q_BLOCK, H, Dh).transpose(1, 0, 2, 3, 4)
    cq_blocks = cum_k.reshape(B, H, nb, Q_BLOCK).transpose(2, 0, 1, 3)
    starts = jnp.arange(nb) * Q_BLOCK
    scale = ATT_HEAD_DIM ** -0.5

    def one_block(args):
        q_blk, cq_blk, start = args
        s = jnp.einsum('bqhd,bkhd->bhqk', q_blk, k) * scale
        s = s + cq_blk[..., :, None] - cum_k[..., None, :]
        q_pos = start + jnp.arange(Q_BLOCK)
        causal = q_pos[:, None] >= k_pos[None, :]
        s = jnp.where(causal[None, None], s, -jnp.inf)
        p = jax.nn.softmax(s, axis=-1)
        return jnp.einsum('bhqk,bkhd->bqhd', p, v)

    out = lax.map(one_block, (q_blocks, cq_blocks, starts))
    return out.transpose(1, 0, 2, 3, 4).reshape(B, S, H * Dh)


def _gmlp(z, ln_g, w_s, b_s):
    B, S, _ = z.shape
    u, v = jnp.split(z, 2, axis=-1)
    v = _layernorm(v, ln_g)
    nc = S // GM_CHUNK
    v = v.reshape(B, nc, GM_CHUNK, GM_GROUPS, GM_GROUP_DIM)
    tril = jnp.tril(jnp.ones((GM_CHUNK, GM_CHUNK), dtype=bool))
    w = jnp.where(tril[None], w_s, jnp.zeros_like(w_s))
    mixed = jnp.einsum('gts,bcsgd->bctgd', w, v) + b_s.T[None, None, :, :, None]
    return u * mixed.reshape(B, S, GM_WIDTH)


def _peer(h, w_query, sub_keys, expert_u, expert_v):
    B, S, D = h.shape
    n = B * S
    hf = h.reshape(n, D)
    q = (hf @ w_query).reshape(n, PEER_HEADS, 2, PEER_HALF)
    scores = jnp.einsum('nhpd,hpkd->nhpk', q, sub_keys)
    s_top, i_top = lax.top_k(scores, PEER_TOPK)
    cand_s = (s_top[:, :, 0, :, None] + s_top[:, :, 1, None, :]).reshape(n, PEER_HEADS, PEER_TOPK * PEER_TOPK)
    cand_i = (i_top[:, :, 0, :, None] * N_KEYS + i_top[:, :, 1, None, :]).reshape(n, PEER_HEADS, PEER_TOPK * PEER_TOPK)
    best_s, best_pos = lax.top_k(cand_s, PEER_TOPK)
    idx = jnp.take_along_axis(cand_i, best_pos, axis=-1)
    gates = jax.nn.softmax(best_s, axis=-1)
    nblk = n // PEER_TOK_BLOCK

    def one_block(args):
        h_blk, idx_blk, g_blk = args
        u = jnp.take(expert_u, idx_blk, axis=0)
        act = jax.nn.gelu(jnp.einsum('thkd,td->thk', u, h_blk))
        vv = jnp.take(expert_v, idx_blk, axis=0)
        return jnp.einsum('thk,thkd->td', g_blk * act, vv)

    out = lax.map(one_block, (hf.reshape(nblk, PEER_TOK_BLOCK, D),
                              idx.reshape(nblk, PEER_TOK_BLOCK, PEER_HEADS, PEER_TOPK),
                              gates.reshape(nblk, PEER_TOK_BLOCK, PEER_HEADS, PEER_TOPK)))
    return out.reshape(B, S, D)


def _final_norm_kernel(x_ref, g_ref, o_ref):
    x = x_ref[...]
    o_ref[...] = x * lax.rsqrt(jnp.mean(x * x, axis=-1, keepdims=True) + EPS) * g_ref[...]


def _final_norm(x2d, g):
    n, d = x2d.shape
    tm = 512
    return pl.pallas_call(
        _final_norm_kernel,
        grid=(n // tm,),
        in_specs=[pl.BlockSpec((tm, d), lambda i: (i, 0)),
                  pl.BlockSpec((1, d), lambda i: (0, 0))],
        out_specs=pl.BlockSpec((tm, d), lambda i: (i, 0)),
        out_shape=jax.ShapeDtypeStruct((n, d), jnp.float32),
    )(x2d, g.reshape(1, d))


def kernel(x, c, w_mod, b_mod, norm1_g, w_in, b_forget, ln_v_g, w_spatial, b_spatial, w_branch_a, w_branch_b, w_out, norm2_g, w_query, sub_keys, expert_u, expert_v, final_g):
    B, S, D = x.shape
    l = 0
    mod = jax.nn.silu(c) @ w_mod[l] + b_mod[l]
    sh1, sc1, g1, sh2, sc2, g2 = jnp.split(mod, 6, axis=-1)
    h = _modulate(_rmsnorm(x, norm1_g[l]), sh1, sc1)
    proj = h @ w_in[l]
    q, k, v, f_logit, z, gate_a, gate_b = jnp.split(proj, SPLIT_POINTS, axis=-1)
    log_f = jax.nn.log_sigmoid(f_logit + b_forget[l])
    y_a = _fox(q.reshape(B, S, ATT_HEADS, ATT_HEAD_DIM),
               k.reshape(B, S, ATT_HEADS, ATT_HEAD_DIM),
               v.reshape(B, S, ATT_HEADS, ATT_HEAD_DIM), log_f)
    y_b = _gmlp(jax.nn.gelu(z), ln_v_g[l], w_spatial[l], b_spatial[l])
    merged = (jax.nn.sigmoid(gate_a) * (y_a @ w_branch_a[l])
              + jax.nn.sigmoid(gate_b) * (y_b @ w_branch_b[l]))
    x = x + g1[:, None, :] * (merged @ w_out[l])
    h2 = _modulate(_rmsnorm(x, norm2_g[l]), sh2, sc2)
    x = x + g2[:, None, :] * _peer(h2, w_query[l], sub_keys[l], expert_u[l], expert_v[l])
    return _final_norm(x.reshape(B * S, D), final_g).reshape(B, S, D)
```

```python
import functools

import jax
import jax.numpy as jnp
from jax import lax
from jax.experimental import pallas as pl
from jax.experimental.pallas import tpu as pltpu

D_MODEL = 1024
ATT_HEADS = 8
ATT_HEAD_DIM = 64
ATT_WIDTH = ATT_HEADS * ATT_HEAD_DIM
Q_BLOCK = 128
GM_GROUPS = 4
GM_GROUP_DIM = 128
GM_WIDTH = GM_GROUPS * GM_GROUP_DIM
GM_CHUNK = 128
PEER_HEADS = 8
PEER_KEY_DIM = 256
PEER_HALF = PEER_KEY_DIM // 2
N_KEYS = 128
PEER_TOPK = 16
PEER_TOK_BLOCK = 128
PEER_SLOTS = PEER_HEADS * PEER_TOPK
SPLIT_POINTS = (ATT_WIDTH, 2 * ATT_WIDTH, 3 * ATT_WIDTH, 3 * ATT_WIDTH + ATT_HEADS,
                3 * ATT_WIDTH + ATT_HEADS + 2 * GM_WIDTH,
                3 * ATT_WIDTH + ATT_HEADS + 2 * GM_WIDTH + D_MODEL)
EPS = 1e-6

LANES = 128
EXPERT_TOK_BLOCK = 128
EXPERT_SLOTS = 4


def _gelu(x):
    return 0.5 * x * (1.0 + jnp.tanh(0.7978845608028654 * (x + 0.044715 * (x * x * x))))


def _rmsnorm(x, g):
    y = x * lax.rsqrt(jnp.mean(x * x, axis=-1, keepdims=True) + EPS)
    return y * g


def _layernorm(x, g):
    mu = jnp.mean(x, axis=-1, keepdims=True)
    var = jnp.mean(jnp.square(x - mu), axis=-1, keepdims=True)
    return (x - mu) * lax.rsqrt(var + EPS) * g


def _modulate(h, shift, scale):
    return h * (1.0 + scale[:, None, :]) + shift[:, None, :]


def _fox(q, k, v, log_f):
    B, S, H, Dh = q.shape
    nb = S // Q_BLOCK
    cum = jnp.cumsum(log_f, axis=1)
    cum_k = jnp.transpose(cum, (0, 2, 1))
    k_pos = jnp.arange(S)
    q_blocks = q.reshape(B, nb, Q_BLOCK, H, Dh).transpose(1, 0, 2, 3, 4)
    cq_blocks = cum_k.reshape(B, H, nb, Q_BLOCK).transpose(2, 0, 1, 3)
    starts = jnp.arange(nb) * Q_BLOCK
    scale = ATT_HEAD_DIM ** -0.5

    def one_block(args):
        q_blk, cq_blk, start = args
        s = jnp.einsum('bqhd,bkhd->bhqk', q_blk, k) * scale
        s = s + cq_blk[..., :, None] - cum_k[..., None, :]
        q_pos = start + jnp.arange(Q_BLOCK)
        causal = q_pos[:, None] >= k_pos[None, :]
        s = jnp.where(causal[None, None], s, -jnp.inf)
        p = jax.nn.softmax(s, axis=-1)
        return jnp.einsum('bhqk,bkhd->bqhd', p, v)

    out = lax.map(one_block, (q_blocks, cq_blocks, starts))
    return out.transpose(1, 0, 2, 3, 4).reshape(B, S, H * Dh)


def _gmlp(z, ln_g, w_s, b_s):
    B, S, _ = z.shape
    u, v = jnp.split(z, 2, axis=-1)
    v = _layernorm(v, ln_g)
    nc = S // GM_CHUNK
    v = v.reshape(B, nc, GM_CHUNK, GM_GROUPS, GM_GROUP_DIM)
    tril = jnp.tril(jnp.ones((GM_CHUNK, GM_CHUNK), dtype=bool))
    w = jnp.where(tril[None], w_s, jnp.zeros_like(w_s))
    mixed = jnp.einsum('gts,bcsgd->bctgd', w, v) + b_s.T[None, None, :, :, None]
    return u * mixed.reshape(B, S, GM_WIDTH)


def _peer_route(hf, w_query, sub_keys):
    n = hf.shape[0]
    q = (hf @ w_query).reshape(n, PEER_HEADS, 2, PEER_HALF)
    scores = jnp.einsum('nhpd,hpkd->nhpk', q, sub_keys)
    s_top, i_top = lax.top_k(scores, PEER_TOPK)
    cand_s = (s_top[:, :, 0, :, None] + s_top[:, :, 1, None, :]).reshape(n, PEER_HEADS, PEER_TOPK * PEER_TOPK)
    cand_i = (i_top[:, :, 0, :, None] * N_KEYS + i_top[:, :, 1, None, :]).reshape(n, PEER_HEADS, PEER_TOPK * PEER_TOPK)
    best_s, best_pos = lax.top_k(cand_s, PEER_TOPK)
    idx = jnp.take_along_axis(cand_i, best_pos, axis=-1)
    gates = jax.nn.softmax(best_s, axis=-1)
    return idx.reshape(n, PEER_SLOTS), gates.reshape(n, PEER_SLOTS)


def _rowsum_bcast(p, ones_bf16):
    hi = p.astype(jnp.bfloat16)
    lo = (p - hi.astype(jnp.float32)).astype(jnp.bfloat16)
    return (jnp.dot(hi, ones_bf16, preferred_element_type=jnp.float32)
            + jnp.dot(lo, ones_bf16, preferred_element_type=jnp.float32))


def _expert_kernel(idx_ref, gates_ref, h_ref, x_ref, g2_ref, fg_ref, tab_ref, out_ref,
                   buf, peer, sem, *, tok_block, n_slots):
    d_model = h_ref.shape[-1]
    n_chunks = d_model // LANES
    prefetch = n_slots - 1

    def issue(t, slot):
        for r in range(PEER_SLOTS):
            e = idx_ref[t, r]
            pltpu.make_async_copy(tab_ref.at[pl.ds(e, 1), :],
                                  buf.at[slot, pl.ds(r, 1), :],
                                  sem.at[slot]).start()

    def wait(slot):
        pltpu.make_async_copy(tab_ref.at[pl.ds(0, PEER_SLOTS), :], buf.at[slot],
                              sem.at[slot]).wait()

    ones_bf16 = jnp.ones((LANES, LANES), jnp.bfloat16)
    eye = (lax.broadcasted_iota(jnp.int32, (PEER_SLOTS, LANES), 0)
           == lax.broadcasted_iota(jnp.int32, (PEER_SLOTS, LANES), 1))

    for t in range(prefetch):
        issue(t, t % n_slots)

    def body(t, carry):
        slot = lax.rem(t, n_slots)

        @pl.when(t + prefetch < tok_block)
        def _():
            issue(t + prefetch, lax.rem(t + prefetch, n_slots))

        wait(slot)
        w = buf[slot]
        u = lax.bitcast_convert_type(w & jnp.uint32(0xFFFF0000), jnp.float32)
        hrow = h_ref[pl.ds(t, 1), :]
        p = u * hrow
        psum = p[:, 0:LANES]
        for c in range(1, n_chunks):
            psum = psum + p[:, c * LANES:(c + 1) * LANES]
        act = _gelu(_rowsum_bcast(psum, ones_bf16))
        grow = gates_ref[pl.ds(t, 1), :]
        gcol = _rowsum_bcast(jnp.where(eye, grow, 0.0), ones_bf16)
        coef = gcol * act
        v = lax.bitcast_convert_type(w << 16, jnp.float32)
        outs = []
        for c in range(n_chunks):
            outs.append(jnp.sum(coef * v[:, c * LANES:(c + 1) * LANES], axis=0, keepdims=True))
        peer[pl.ds(t, 1), :] = jnp.concatenate(outs, axis=-1)
        return carry

    lax.fori_loop(0, tok_block, body, 0)

    y = x_ref[...] + g2_ref[0] * peer[...]
    out_ref[...] = y * lax.rsqrt(jnp.mean(y * y, axis=-1, keepdims=True) + EPS) * fg_ref[...]


def _pack_expert_table(expert_u, expert_v):
    ub = lax.bitcast_convert_type(expert_u.astype(jnp.bfloat16), jnp.uint16).astype(jnp.uint32)
    vb = lax.bitcast_convert_type(expert_v.astype(jnp.bfloat16), jnp.uint16).astype(jnp.uint32)
    return (ub << 16) | vb


def _peer_experts_final(idx, gates, h2, x1, g2, final_g, table, seq_len):
    n, d = h2.shape
    tb = EXPERT_TOK_BLOCK
    blocks_per_seq = seq_len // tb
    kern = functools.partial(_expert_kernel, tok_block=tb, n_slots=EXPERT_SLOTS)
    return pl.pallas_call(
        kern,
        grid=(n // tb,),
        in_specs=[
            pl.BlockSpec((tb, PEER_SLOTS), lambda i: (i, 0), memory_space=pltpu.SMEM),
            pl.BlockSpec((tb, PEER_SLOTS), lambda i: (i, 0)),
            pl.BlockSpec((tb, d), lambda i: (i, 0)),
            pl.BlockSpec((tb, d), lambda i: (i, 0)),
            pl.BlockSpec((1, 1, d), lambda i: (i // blocks_per_seq, 0, 0)),
            pl.BlockSpec((1, d), lambda i: (0, 0)),
            pl.BlockSpec(memory_space=pl.ANY),
        ],
        out_specs=pl.BlockSpec((tb, d), lambda i: (i, 0)),
        out_shape=jax.ShapeDtypeStruct((n, d), jnp.float32),
        scratch_shapes=[
            pltpu.VMEM((EXPERT_SLOTS, PEER_SLOTS, d), jnp.uint32),
            pltpu.VMEM((tb, d), jnp.float32),
            pltpu.SemaphoreType.DMA((EXPERT_SLOTS,)),
        ],
        compiler_params=pltpu.CompilerParams(dimension_semantics=("arbitrary",)),
        name="peer_experts",
    )(idx, gates, h2, x1, g2, final_g.reshape(1, d), table)


def kernel(x, c, w_mod, b_mod, norm1_g, w_in, b_forget, ln_v_g, w_spatial, b_spatial, w_branch_a, w_branch_b, w_out, norm2_g, w_query, sub_keys, expert_u, expert_v, final_g):
    B, S, D = x.shape
    l = 0
    mod = jax.nn.silu(c) @ w_mod[l] + b_mod[l]
    sh1, sc1, g1, sh2, sc2, g2 = jnp.split(mod, 6, axis=-1)
    h = _modulate(_rmsnorm(x, norm1_g[l]), sh1, sc1)
    proj = h @ w_in[l]
    q, k, v, f_logit, z, gate_a, gate_b = jnp.split(proj, SPLIT_POINTS, axis=-1)
    log_f = jax.nn.log_sigmoid(f_logit + b_forget[l])
    y_a = _fox(q.reshape(B, S, ATT_HEADS, ATT_HEAD_DIM),
               k.reshape(B, S, ATT_HEADS, ATT_HEAD_DIM),
               v.reshape(B, S, ATT_HEADS, ATT_HEAD_DIM), log_f)
    y_b = _gmlp(jax.nn.gelu(z), ln_v_g[l], w_spatial[l], b_spatial[l])
    merged = (jax.nn.sigmoid(gate_a) * (y_a @ w_branch_a[l])
              + jax.nn.sigmoid(gate_b) * (y_b @ w_branch_b[l]))
    x1 = x + g1[:, None, :] * (merged @ w_out[l])
    h2 = _modulate(_rmsnorm(x1, norm2_g[l]), sh2, sc2).reshape(B * S, D)
    idx, gates = _peer_route(h2, w_query[l], sub_keys[l])
    table = _pack_expert_table(expert_u[l], expert_v[l])
    out = _peer_experts_final(idx, gates, h2, x1.reshape(B * S, D), g2.reshape(B, 1, D),
                              final_g, table, S)
    return out.reshape(B, S, D)
```

```python
import functools

import jax
import jax.numpy as jnp
from jax import lax
from jax.experimental import pallas as pl
from jax.experimental.pallas import tpu as pltpu

D_MODEL = 1024
ATT_HEADS = 8
ATT_HEAD_DIM = 64
ATT_WIDTH = ATT_HEADS * ATT_HEAD_DIM
Q_BLOCK = 128
GM_GROUPS = 4
GM_GROUP_DIM = 128
GM_WIDTH = GM_GROUPS * GM_GROUP_DIM
GM_CHUNK = 128
PEER_HEADS = 8
PEER_KEY_DIM = 256
PEER_HALF = PEER_KEY_DIM // 2
N_KEYS = 128
PEER_TOPK = 16
PEER_TOK_BLOCK = 128
PEER_SLOTS = PEER_HEADS * PEER_TOPK
SPLIT_POINTS = (ATT_WIDTH, 2 * ATT_WIDTH, 3 * ATT_WIDTH, 3 * ATT_WIDTH + ATT_HEADS,
                3 * ATT_WIDTH + ATT_HEADS + 2 * GM_WIDTH,
                3 * ATT_WIDTH + ATT_HEADS + 2 * GM_WIDTH + D_MODEL)
EPS = 1e-6

LANES = 128
EXPERT_TOK_BLOCK = 128
EXPERT_SLOTS = 4


def _gelu(x):
    return 0.5 * x * (1.0 + jnp.tanh(0.7978845608028654 * (x + 0.044715 * (x * x * x))))


def _rmsnorm(x, g):
    y = x * lax.rsqrt(jnp.mean(x * x, axis=-1, keepdims=True) + EPS)
    return y * g


def _layernorm(x, g):
    mu = jnp.mean(x, axis=-1, keepdims=True)
    var = jnp.mean(jnp.square(x - mu), axis=-1, keepdims=True)
    return (x - mu) * lax.rsqrt(var + EPS) * g


def _modulate(h, shift, scale):
    return h * (1.0 + scale[:, None, :]) + shift[:, None, :]


def _fox(q, k, v, log_f):
    B, S, H, Dh = q.shape
    nb = S // Q_BLOCK
    cum = jnp.cumsum(log_f, axis=1)
    cum_k = jnp.transpose(cum, (0, 2, 1))
    k_pos = jnp.arange(S)
    q_blocks = q.reshape(B, nb, Q_BLOCK, H, Dh).transpose(1, 0, 2, 3, 4)
    cq_blocks = cum_k.reshape(B, H, nb, Q_BLOCK).transpose(2, 0, 1, 3)
    starts = jnp.arange(nb) * Q_BLOCK
    scale = ATT_HEAD_DIM ** -0.5

    def one_block(args):
        q_blk, cq_blk, start = args
        s = jnp.einsum('bqhd,bkhd->bhqk', q_blk, k) * scale
        s = s + cq_blk[..., :, None] - cum_k[..., None, :]
        q_pos = start + jnp.arange(Q_BLOCK)
        causal = q_pos[:, None] >= k_pos[None, :]
        s = jnp.where(causal[None, None], s, -jnp.inf)
        p = jax.nn.softmax(s, axis=-1)
        return jnp.einsum('bhqk,bkhd->bqhd', p, v)

    out = lax.map(one_block, (q_blocks, cq_blocks, starts))
    return out.transpose(1, 0, 2, 3, 4).reshape(B, S, H * Dh)


def _gmlp(z, ln_g, w_s, b_s):
    B, S, _ = z.shape
    u, v = jnp.split(z, 2, axis=-1)
    v = _layernorm(v, ln_g)
    nc = S // GM_CHUNK
    v = v.reshape(B, nc, GM_CHUNK, GM_GROUPS, GM_GROUP_DIM)
    tril = jnp.tril(jnp.ones((GM_CHUNK, GM_CHUNK), dtype=bool))
    w = jnp.where(tril[None], w_s, jnp.zeros_like(w_s))
    mixed = jnp.einsum('gts,bcsgd->bctgd', w, v) + b_s.T[None, None, :, :, None]
    return u * mixed.reshape(B, S, GM_WIDTH)


def _peer_route(hf, w_query, sub_keys):
    n = hf.shape[0]
    q = (hf @ w_query).reshape(n, PEER_HEADS, 2, PEER_HALF)
    scores = jnp.einsum('nhpd,hpkd->nhpk', q, sub_keys)
    s_top, i_top = lax.top_k(scores, PEER_TOPK)
    cand_s = (s_top[:, :, 0, :, None] + s_top[:, :, 1, None, :]).reshape(n, PEER_HEADS, PEER_TOPK * PEER_TOPK)
    cand_i = (i_top[:, :, 0, :, None] * N_KEYS + i_top[:, :, 1, None, :]).reshape(n, PEER_HEADS, PEER_TOPK * PEER_TOPK)
    best_s, best_pos = lax.top_k(cand_s, PEER_TOPK)
    idx = jnp.take_along_axis(cand_i, best_pos, axis=-1)
    gates = jax.nn.softmax(best_s, axis=-1)
    return idx.reshape(n, PEER_SLOTS), gates.reshape(n, PEER_SLOTS)


def _topk_rows(s, k):
    rows, t = s.shape
    iota = lax.broadcasted_iota(jnp.int32, (rows, t), 0)
    slot = lax.broadcasted_iota(jnp.int32, (k, t), 0)
    vals = jnp.zeros((k, t), jnp.float32)
    ids = jnp.zeros((k, t), jnp.int32)
    for j in range(k):
        m = jnp.max(s, axis=0, keepdims=True)
        am = jnp.min(jnp.where(s == m, iota, rows), axis=0, keepdims=True)
        vals = jnp.where(slot == j, m, vals)
        ids = jnp.where(slot == j, am, ids)
        s = jnp.where(iota == am, -jnp.inf, s)
    return vals, ids


def _select_rows(table, pos):
    out = jnp.zeros(pos.shape, table.dtype)
    for r in range(table.shape[0]):
        out = jnp.where(pos == r, table[r:r + 1, :], out)
    return out


def _post_kernel(ya_ref, sga_ref, gb_ref, x_ref, mod_ref, n2g_ref, wa_ref, wo_ref, wq_ref, keys_ref,
                 x1_ref, h2_ref, idx_ref, gates_ref, q_scr):
    f32 = jnp.float32
    a = jnp.dot(ya_ref[...], wa_ref[...], preferred_element_type=f32)
    merged = sga_ref[...].astype(f32) * a + gb_ref[...].astype(f32)
    o = jnp.dot(merged.astype(jnp.bfloat16), wo_ref[...], preferred_element_type=f32)
    g1 = mod_ref[0, 2:3, :]
    sh2 = mod_ref[0, 3:4, :]
    sc2 = mod_ref[0, 4:5, :]
    x1 = x_ref[...] + g1 * o
    x1_ref[...] = x1
    h2 = x1 * lax.rsqrt(jnp.mean(x1 * x1, axis=-1, keepdims=True) + EPS) * n2g_ref[...]
    h2 = h2 * (1.0 + sc2) + sh2
    h2_ref[...] = h2
    qp = jnp.dot(h2.astype(jnp.bfloat16), wq_ref[...], preferred_element_type=f32)
    for j in range(2 * PEER_HEADS):
        q_scr[j] = qp[:, j * PEER_HALF:(j + 1) * PEER_HALF]

    def head(h, carry):
        tops = []
        for p in range(2):
            q = q_scr[2 * h + p]
            keys = keys_ref[2 * h + p]
            sc = lax.dot_general(keys, q, (((1,), (1,)), ((), ())),
                                 precision=lax.Precision.HIGHEST,
                                 preferred_element_type=f32)
            tops.append(_topk_rows(sc, PEER_TOPK))
        (s1, i1), (s2, i2) = tops
        t = s1.shape[1]
        cand = (s1[:, None, :] + s2[None, :, :]).reshape(PEER_TOPK * PEER_TOPK, t)
        vals, pos = _topk_rows(cand, PEER_TOPK)
        eid = (_select_rows(i1, pos >> 4) * N_KEYS + _select_rows(i2, pos & (PEER_TOPK - 1)))
        e = jnp.exp(vals - vals[0:1, :])
        g = e / jnp.sum(e, axis=0, keepdims=True)
        row = pl.multiple_of(h * PEER_TOPK, PEER_TOPK)
        idx_ref[pl.ds(row, PEER_TOPK), :] = eid
        gates_ref[pl.ds(row, PEER_TOPK), :] = g
        return carry

    lax.fori_loop(0, PEER_HEADS, head, 0)


POST_TOK_BLOCK = 256


def _post_attention(ya, sga, gb, x, mod, norm2_g, w_a, w_out, w_query, sub_keys, seq_len):
    n, d = x.shape
    tm = POST_TOK_BLOCK
    blocks_per_seq = seq_len // tm
    aw = ya.shape[1]
    qw = w_query.shape[1]
    keys = sub_keys.reshape(2 * PEER_HEADS, N_KEYS, PEER_HALF)
    tok = lambda w: pl.BlockSpec((tm, w), lambda i: (i, 0))
    full = lambda a: pl.BlockSpec(a.shape, lambda i: (0,) * a.ndim)
    return pl.pallas_call(
        _post_kernel,
        grid=(n // tm,),
        in_specs=[tok(aw), tok(d), tok(d), tok(d),
                  pl.BlockSpec((1, 6, d), lambda i: (i // blocks_per_seq, 0, 0)),
                  full(norm2_g), full(w_a), full(w_out), full(w_query), full(keys)],
        out_specs=[tok(d), tok(d),
                   pl.BlockSpec((PEER_SLOTS, tm), lambda i: (0, i)),
                   pl.BlockSpec((PEER_SLOTS, tm), lambda i: (0, i))],
        out_shape=[jax.ShapeDtypeStruct((n, d), jnp.float32),
                   jax.ShapeDtypeStruct((n, d), jnp.float32),
                   jax.ShapeDtypeStruct((PEER_SLOTS, n), jnp.int32),
                   jax.ShapeDtypeStruct((PEER_SLOTS, n), jnp.float32)],
        scratch_shapes=[pltpu.VMEM((2 * PEER_HEADS, tm, PEER_HALF), jnp.float32)],
        compiler_params=pltpu.CompilerParams(dimension_semantics=("arbitrary",),
                                             vmem_limit_bytes=48 * 1024 * 1024),
        name="post_attention",
    )(ya, sga, gb, x, mod, norm2_g, w_a, w_out, w_query, keys)


def _rowsum_bcast(p, ones_bf16):
    hi = p.astype(jnp.bfloat16)
    lo = (p - hi.astype(jnp.float32)).astype(jnp.bfloat16)
    return (jnp.dot(hi, ones_bf16, preferred_element_type=jnp.float32)
            + jnp.dot(lo, ones_bf16, preferred_element_type=jnp.float32))


def _expert_kernel(idx_ref, gates_ref, h_ref, x_ref, g2_ref, fg_ref, tab_ref, out_ref,
                   buf, peer, sem, *, tok_block, n_slots):
    d_model = h_ref.shape[-1]
    n_chunks = d_model // LANES
    prefetch = n_slots - 1

    def issue(t, slot):
        for r in range(PEER_SLOTS):
            e = idx_ref[t, r]
            pltpu.make_async_copy(tab_ref.at[pl.ds(e, 1), :],
                                  buf.at[slot, pl.ds(r, 1), :],
                                  sem.at[slot]).start()

    def wait(slot):
        pltpu.make_async_copy(tab_ref.at[pl.ds(0, PEER_SLOTS), :], buf.at[slot],
                              sem.at[slot]).wait()

    ones_bf16 = jnp.ones((LANES, LANES), jnp.bfloat16)
    eye = (lax.broadcasted_iota(jnp.int32, (PEER_SLOTS, LANES), 0)
           == lax.broadcasted_iota(jnp.int32, (PEER_SLOTS, LANES), 1))

    for t in range(prefetch):
        issue(t, t % n_slots)

    def body(t, carry):
        slot = lax.rem(t, n_slots)

        @pl.when(t + prefetch < tok_block)
        def _():
            issue(t + prefetch, lax.rem(t + prefetch, n_slots))

        wait(slot)
        w = buf[slot]
        u = lax.bitcast_convert_type(w & jnp.uint32(0xFFFF0000), jnp.float32)
        hrow = h_ref[pl.ds(t, 1), :]
        p = u * hrow
        psum = p[:, 0:LANES]
        for c in range(1, n_chunks):
            psum = psum + p[:, c * LANES:(c + 1) * LANES]
        act = _gelu(_rowsum_bcast(psum, ones_bf16))
        grow = gates_ref[pl.ds(t, 1), :]
        gcol = _rowsum_bcast(jnp.where(eye, grow, 0.0), ones_bf16)
        coef = gcol * act
        v = lax.bitcast_convert_type(w << 16, jnp.float32)
        outs = []
        for c in range(n_chunks):
            outs.append(jnp.sum(coef * v[:, c * LANES:(c + 1) * LANES], axis=0, keepdims=True))
        peer[pl.ds(t, 1), :] = jnp.concatenate(outs, axis=-1)
        return carry

    lax.fori_loop(0, tok_block, body, 0)

    y = x_ref[...] + g2_ref[0] * peer[...]
    out_ref[...] = y * lax.rsqrt(jnp.mean(y * y, axis=-1, keepdims=True) + EPS) * fg_ref[...]


def _pack_expert_table(expert_u, expert_v):
    ub = lax.bitcast_convert_type(expert_u.astype(jnp.bfloat16), jnp.uint16).astype(jnp.uint32)
    vb = lax.bitcast_convert_type(expert_v.astype(jnp.bfloat16), jnp.uint16).astype(jnp.uint32)
    return (ub << 16) | vb


def _peer_experts_final(idx, gates, h2, x1, g2, final_g, table, seq_len):
    n, d = h2.shape
    tb = EXPERT_TOK_BLOCK
    blocks_per_seq = seq_len // tb
    kern = functools.partial(_expert_kernel, tok_block=tb, n_slots=EXPERT_SLOTS)
    return pl.pallas_call(
        kern,
        grid=(n // tb,),
        in_specs=[
            pl.BlockSpec((tb, PEER_SLOTS), lambda i: (i, 0), memory_space=pltpu.SMEM),
            pl.BlockSpec((tb, PEER_SLOTS), lambda i: (i, 0)),
            pl.BlockSpec((tb, d), lambda i: (i, 0)),
            pl.BlockSpec((tb, d), lambda i: (i, 0)),
            pl.BlockSpec((1, 1, d), lambda i: (i // blocks_per_seq, 0, 0)),
            pl.BlockSpec((1, d), lambda i: (0, 0)),
            pl.BlockSpec(memory_space=pl.ANY),
        ],
        out_specs=pl.BlockSpec((tb, d), lambda i: (i, 0)),
        out_shape=jax.ShapeDtypeStruct((n, d), jnp.float32),
        scratch_shapes=[
            pltpu.VMEM((EXPERT_SLOTS, PEER_SLOTS, d), jnp.uint32),
            pltpu.VMEM((tb, d), jnp.float32),
            pltpu.SemaphoreType.DMA((EXPERT_SLOTS,)),
        ],
        compiler_params=pltpu.CompilerParams(dimension_semantics=("arbitrary",)),
        name="peer_experts",
    )(idx, gates, h2, x1, g2, final_g.reshape(1, d), table)


def kernel(x, c, w_mod, b_mod, norm1_g, w_in, b_forget, ln_v_g, w_spatial, b_spatial, w_branch_a, w_branch_b, w_out, norm2_g, w_query, sub_keys, expert_u, expert_v, final_g):
    B, S, D = x.shape
    l = 0
    mod = jax.nn.silu(c) @ w_mod[l] + b_mod[l]
    sh1, sc1, g1, sh2, sc2, g2 = jnp.split(mod, 6, axis=-1)
    h = _modulate(_rmsnorm(x, norm1_g[l]), sh1, sc1)
    proj = h @ w_in[l]
    q, k, v, f_logit, z, gate_a, gate_b = jnp.split(proj, SPLIT_POINTS, axis=-1)
    log_f = jax.nn.log_sigmoid(f_logit + b_forget[l])
    y_a = _fox(q.reshape(B, S, ATT_HEADS, ATT_HEAD_DIM),
               k.reshape(B, S, ATT_HEADS, ATT_HEAD_DIM),
               v.reshape(B, S, ATT_HEADS, ATT_HEAD_DIM), log_f)
    y_b = _gmlp(jax.nn.gelu(z), ln_v_g[l], w_spatial[l], b_spatial[l])
    bf16 = jnp.bfloat16
    n = B * S
    sga = jax.nn.sigmoid(gate_a).reshape(n, D).astype(bf16)
    gb = (jax.nn.sigmoid(gate_b) * (y_b @ w_branch_b[l])).reshape(n, D).astype(bf16)
    x1, h2, idx_t, gates_t = _post_attention(
        y_a.reshape(n, ATT_WIDTH).astype(bf16), sga, gb, x.reshape(n, D), mod.reshape(B, 6, D),
        norm2_g[l].reshape(1, D), w_branch_a[l].astype(bf16), w_out[l].astype(bf16),
        w_query[l].astype(bf16), sub_keys[l], S)
    table = _pack_expert_table(expert_u[l], expert_v[l])
    out = _peer_experts_final(idx_t.T, gates_t.T, h2, x1, g2.reshape(B, 1, D), final_g, table, S)
    return out.reshape(B, S, D)
```

```python
import functools

import jax
import jax.numpy as jnp
from jax import lax
from jax.experimental import pallas as pl
from jax.experimental.pallas import tpu as pltpu

D_MODEL = 1024
ATT_HEADS = 8
ATT_HEAD_DIM = 64
ATT_WIDTH = ATT_HEADS * ATT_HEAD_DIM
Q_BLOCK = 128
GM_GROUPS = 4
GM_GROUP_DIM = 128
GM_WIDTH = GM_GROUPS * GM_GROUP_DIM
GM_CHUNK = 128
PEER_HEADS = 8
PEER_KEY_DIM = 256
PEER_HALF = PEER_KEY_DIM // 2
N_KEYS = 128
PEER_TOPK = 16
PEER_TOK_BLOCK = 128
PEER_SLOTS = PEER_HEADS * PEER_TOPK
SPLIT_POINTS = (ATT_WIDTH, 2 * ATT_WIDTH, 3 * ATT_WIDTH, 3 * ATT_WIDTH + ATT_HEADS,
                3 * ATT_WIDTH + ATT_HEADS + 2 * GM_WIDTH,
                3 * ATT_WIDTH + ATT_HEADS + 2 * GM_WIDTH + D_MODEL)
EPS = 1e-6

LANES = 128
EXPERT_TOK_BLOCK = 128
EXPERT_SLOTS = 4


def _gelu(x):
    return 0.5 * x * (1.0 + jnp.tanh(0.7978845608028654 * (x + 0.044715 * (x * x * x))))


def _rmsnorm(x, g):
    y = x * lax.rsqrt(jnp.mean(x * x, axis=-1, keepdims=True) + EPS)
    return y * g


def _layernorm(x, g):
    mu = jnp.mean(x, axis=-1, keepdims=True)
    var = jnp.mean(jnp.square(x - mu), axis=-1, keepdims=True)
    return (x - mu) * lax.rsqrt(var + EPS) * g


def _modulate(h, shift, scale):
    return h * (1.0 + scale[:, None, :]) + shift[:, None, :]


def _fox(q, k, v, log_f):
    B, S, H, Dh = q.shape
    nb = S // Q_BLOCK
    cum = jnp.cumsum(log_f, axis=1)
    cum_k = jnp.transpose(cum, (0, 2, 1))
    k_pos = jnp.arange(S)
    q_blocks = q.reshape(B, nb, Q_BLOCK, H, Dh).transpose(1, 0, 2, 3, 4)
    cq_blocks = cum_k.reshape(B, H, nb, Q_BLOCK).transpose(2, 0, 1, 3)
    starts = jnp.arange(nb) * Q_BLOCK
    scale = ATT_HEAD_DIM ** -0.5

    def one_block(args):
        q_blk, cq_blk, start = args
        s = jnp.einsum('bqhd,bkhd->bhqk', q_blk, k) * scale
        s = s + cq_blk[..., :, None] - cum_k[..., None, :]
        q_pos = start + jnp.arange(Q_BLOCK)
        causal = q_pos[:, None] >= k_pos[None, :]
        s = jnp.where(causal[None, None], s, -jnp.inf)
        p = jax.nn.softmax(s, axis=-1)
        return jnp.einsum('bhqk,bkhd->bqhd', p, v)

    out = lax.map(one_block, (q_blocks, cq_blocks, starts))
    return out.transpose(1, 0, 2, 3, 4).reshape(B, S, H * Dh)


def _gmlp(z, ln_g, w_s, b_s):
    B, S, _ = z.shape
    u, v = jnp.split(z, 2, axis=-1)
    v = _layernorm(v, ln_g)
    nc = S // GM_CHUNK
    v = v.reshape(B, nc, GM_CHUNK, GM_GROUPS, GM_GROUP_DIM)
    tril = jnp.tril(jnp.ones((GM_CHUNK, GM_CHUNK), dtype=bool))
    w = jnp.where(tril[None], w_s, jnp.zeros_like(w_s))
    mixed = jnp.einsum('gts,bcsgd->bctgd', w, v) + b_s.T[None, None, :, :, None]
    return u * mixed.reshape(B, S, GM_WIDTH)


def _peer_route(hf, w_query, sub_keys):
    n = hf.shape[0]
    q = (hf @ w_query).reshape(n, PEER_HEADS, 2, PEER_HALF)
    scores = jnp.einsum('nhpd,hpkd->nhpk', q, sub_keys)
    s_top, i_top = lax.top_k(scores, PEER_TOPK)
    cand_s = (s_top[:, :, 0, :, None] + s_top[:, :, 1, None, :]).reshape(n, PEER_HEADS, PEER_TOPK * PEER_TOPK)
    cand_i = (i_top[:, :, 0, :, None] * N_KEYS + i_top[:, :, 1, None, :]).reshape(n, PEER_HEADS, PEER_TOPK * PEER_TOPK)
    best_s, best_pos = lax.top_k(cand_s, PEER_TOPK)
    idx = jnp.take_along_axis(cand_i, best_pos, axis=-1)
    gates = jax.nn.softmax(best_s, axis=-1)
    return idx.reshape(n, PEER_SLOTS), gates.reshape(n, PEER_SLOTS)


def _topk_rows(s, k):
    rows, t = s.shape
    iota = lax.broadcasted_iota(jnp.int32, (rows, t), 0)
    slot = lax.broadcasted_iota(jnp.int32, (k, t), 0)
    vals = jnp.zeros((k, t), jnp.float32)
    ids = jnp.zeros((k, t), jnp.int32)
    for j in range(k):
        m = jnp.max(s, axis=0, keepdims=True)
        am = jnp.min(jnp.where(s == m, iota, rows), axis=0, keepdims=True)
        vals = jnp.where(slot == j, m, vals)
        ids = jnp.where(slot == j, am, ids)
        s = jnp.where(iota == am, -jnp.inf, s)
    return vals, ids


def _select_rows(table, pos):
    out = jnp.zeros(pos.shape, table.dtype)
    for r in range(table.shape[0]):
        out = jnp.where(pos == r, table[r:r + 1, :], out)
    return out


def _post_kernel(ya_ref, sga_ref, gb_ref, x_ref, mod_ref, n2g_ref, wa_ref, wo_ref, wq_ref, keys_ref,
                 x1_ref, h2_ref, idx_ref, gates_ref, q_scr):
    f32 = jnp.float32
    a = jnp.dot(ya_ref[...], wa_ref[...], preferred_element_type=f32)
    merged = sga_ref[...].astype(f32) * a + gb_ref[...].astype(f32)
    o = jnp.dot(merged.astype(jnp.bfloat16), wo_ref[...], preferred_element_type=f32)
    g1 = mod_ref[0, 2:3, :]
    sh2 = mod_ref[0, 3:4, :]
    sc2 = mod_ref[0, 4:5, :]
    x1 = x_ref[...] + g1 * o
    x1_ref[...] = x1
    h2 = x1 * lax.rsqrt(jnp.mean(x1 * x1, axis=-1, keepdims=True) + EPS) * n2g_ref[...]
    h2 = h2 * (1.0 + sc2) + sh2
    h2_ref[...] = h2
    qp = jnp.dot(h2.astype(jnp.bfloat16), wq_ref[...], preferred_element_type=f32)
    for j in range(2 * PEER_HEADS):
        q_scr[j] = qp[:, j * PEER_HALF:(j + 1) * PEER_HALF]

    def head(h, carry):
        tops = []
        for p in range(2):
            q = q_scr[2 * h + p]
            keys = keys_ref[2 * h + p]
            sc = lax.dot_general(keys, q, (((1,), (1,)), ((), ())),
                                 precision=lax.Precision.HIGHEST,
                                 preferred_element_type=f32)
            tops.append(_topk_rows(sc, PEER_TOPK))
        (s1, i1), (s2, i2) = tops
        t = s1.shape[1]
        cand = (s1[:, None, :] + s2[None, :, :]).reshape(PEER_TOPK * PEER_TOPK, t)
        vals, pos = _topk_rows(cand, PEER_TOPK)
        eid = (_select_rows(i1, pos >> 4) * N_KEYS + _select_rows(i2, pos & (PEER_TOPK - 1)))
        e = jnp.exp(vals - vals[0:1, :])
        g = e / jnp.sum(e, axis=0, keepdims=True)
        row = pl.multiple_of(h * PEER_TOPK, PEER_TOPK)
        idx_ref[pl.ds(row, PEER_TOPK), :] = eid
        gates_ref[pl.ds(row, PEER_TOPK), :] = g
        return carry

    lax.fori_loop(0, PEER_HEADS, head, 0)


POST_TOK_BLOCK = 256


def _post_attention(ya, sga, gb, x, mod, norm2_g, w_a, w_out, w_query, sub_keys, seq_len):
    n, d = x.shape
    tm = POST_TOK_BLOCK
    blocks_per_seq = seq_len // tm
    aw = ya.shape[1]
    qw = w_query.shape[1]
    keys = sub_keys.reshape(2 * PEER_HEADS, N_KEYS, PEER_HALF)
    tok = lambda w: pl.BlockSpec((tm, w), lambda i: (i, 0))
    full = lambda a: pl.BlockSpec(a.shape, lambda i: (0,) * a.ndim)
    return pl.pallas_call(
        _post_kernel,
        grid=(n // tm,),
        in_specs=[tok(aw), tok(d), tok(d), tok(d),
                  pl.BlockSpec((1, 6, d), lambda i: (i // blocks_per_seq, 0, 0)),
                  full(norm2_g), full(w_a), full(w_out), full(w_query), full(keys)],
        out_specs=[tok(d), tok(d),
                   pl.BlockSpec((PEER_SLOTS, tm), lambda i: (0, i)),
                   pl.BlockSpec((PEER_SLOTS, tm), lambda i: (0, i))],
        out_shape=[jax.ShapeDtypeStruct((n, d), jnp.float32),
                   jax.ShapeDtypeStruct((n, d), jnp.float32),
                   jax.ShapeDtypeStruct((PEER_SLOTS, n), jnp.int32),
                   jax.ShapeDtypeStruct((PEER_SLOTS, n), jnp.float32)],
        scratch_shapes=[pltpu.VMEM((2 * PEER_HEADS, tm, PEER_HALF), jnp.float32)],
        compiler_params=pltpu.CompilerParams(dimension_semantics=("arbitrary",),
                                             vmem_limit_bytes=48 * 1024 * 1024),
        name="post_attention",
    )(ya, sga, gb, x, mod, norm2_g, w_a, w_out, w_query, keys)


def _rowsum_bcast(p, ones_bf16):
    hi = p.astype(jnp.bfloat16)
    lo = (p - hi.astype(jnp.float32)).astype(jnp.bfloat16)
    return (jnp.dot(hi, ones_bf16, preferred_element_type=jnp.float32)
            + jnp.dot(lo, ones_bf16, preferred_element_type=jnp.float32))


def _expert_kernel(idx_ref, gates_ref, h_ref, x_ref, g2_ref, fg_ref, tab_ref, out_ref,
                   *scratch, tok_block, n_slots):
    bufs = scratch[:n_slots]
    peer, sem = scratch[n_slots], scratch[n_slots + 1]
    d_model = h_ref.shape[-1]
    n_chunks = d_model // LANES
    rows_per_tok = PEER_SLOTS * n_chunks

    def issue(t, s):
        for r in range(PEER_SLOTS):
            row = pl.multiple_of(idx_ref[t, r] * n_chunks, n_chunks)
            pltpu.make_async_copy(tab_ref.at[pl.ds(row, n_chunks), :],
                                  bufs[s].at[pl.ds(r * n_chunks, n_chunks), :],
                                  sem.at[s]).start()

    def wait(s):
        pltpu.make_async_copy(tab_ref.at[pl.ds(0, rows_per_tok), :], bufs[s], sem.at[s]).wait()

    ones_bf16 = jnp.ones((LANES, LANES), jnp.bfloat16)
    eye = (lax.broadcasted_iota(jnp.int32, (PEER_SLOTS, LANES), 0)
           == lax.broadcasted_iota(jnp.int32, (PEER_SLOTS, LANES), 1))

    def chunk(s, c):
        return bufs[s][pl.ds(c, PEER_SLOTS, stride=n_chunks), :]

    def compute(t, s):
        hrow = h_ref[pl.ds(t, 1), :]
        psum = None
        for c in range(n_chunks):
            u = lax.bitcast_convert_type(chunk(s, c) & jnp.uint32(0xFFFF0000), jnp.float32)
            p = u * hrow[:, c * LANES:(c + 1) * LANES]
            psum = p if psum is None else psum + p
        act = _gelu(_rowsum_bcast(psum, ones_bf16))
        grow = gates_ref[pl.ds(t, 1), :]
        gcol = _rowsum_bcast(jnp.where(eye, grow, 0.0), ones_bf16)
        coef = gcol * act
        outs = []
        for c in range(n_chunks):
            v = lax.bitcast_convert_type(chunk(s, c) << 16, jnp.float32)
            outs.append(jnp.sum(coef * v, axis=0, keepdims=True))
        peer[pl.ds(t, 1), :] = jnp.concatenate(outs, axis=-1)

    def step(t, s, prefetch):
        wait(s)
        if prefetch:
            issue(t + n_slots - 1, (s - 1) % n_slots)
        compute(t, s)

    for t in range(n_slots - 1):
        issue(t, t)

    n_groups = tok_block // n_slots

    def group(g, carry):
        for s in range(n_slots):
            step(g * n_slots + s, s, True)
        return carry

    lax.fori_loop(0, n_groups - 1, group, 0)
    for s in range(n_slots):
        t = (n_groups - 1) * n_slots + s
        step(t, s, t + n_slots - 1 < tok_block)

    y = x_ref[...] + g2_ref[0] * peer[...]
    out_ref[...] = y * lax.rsqrt(jnp.mean(y * y, axis=-1, keepdims=True) + EPS) * fg_ref[...]


def _pack_expert_table(expert_u, expert_v):
    ub = lax.bitcast_convert_type(expert_u.astype(jnp.bfloat16), jnp.uint16).astype(jnp.uint32)
    vb = lax.bitcast_convert_type(expert_v.astype(jnp.bfloat16), jnp.uint16).astype(jnp.uint32)
    packed = (ub << 16) | vb
    n_experts, d = packed.shape
    return packed.reshape(n_experts * (d // LANES), LANES)


def _peer_experts_final(idx, gates, h2, x1, g2, final_g, table, seq_len):
    n, d = h2.shape
    tb = EXPERT_TOK_BLOCK
    assert tb % EXPERT_SLOTS == 0 and seq_len % tb == 0
    blocks_per_seq = seq_len // tb
    kern = functools.partial(_expert_kernel, tok_block=tb, n_slots=EXPERT_SLOTS)
    return pl.pallas_call(
        kern,
        grid=(n // tb,),
        in_specs=[
            pl.BlockSpec((tb, PEER_SLOTS), lambda i: (i, 0), memory_space=pltpu.SMEM),
            pl.BlockSpec((tb, PEER_SLOTS), lambda i: (i, 0)),
            pl.BlockSpec((tb, d), lambda i: (i, 0)),
            pl.BlockSpec((tb, d), lambda i: (i, 0)),
            pl.BlockSpec((1, 1, d), lambda i: (i // blocks_per_seq, 0, 0)),
            pl.BlockSpec((1, d), lambda i: (0, 0)),
            pl.BlockSpec(memory_space=pl.ANY),
        ],
        out_specs=pl.BlockSpec((tb, d), lambda i: (i, 0)),
        out_shape=jax.ShapeDtypeStruct((n, d), jnp.float32),
        scratch_shapes=(
            [pltpu.VMEM((PEER_SLOTS * (d // LANES), LANES), jnp.uint32) for _ in range(EXPERT_SLOTS)]
            + [pltpu.VMEM((tb, d), jnp.float32), pltpu.SemaphoreType.DMA((EXPERT_SLOTS,))]),
        compiler_params=pltpu.CompilerParams(dimension_semantics=("arbitrary",)),
        name="peer_experts",
    )(idx, gates, h2, x1, g2, final_g.reshape(1, d), table)


def kernel(x, c, w_mod, b_mod, norm1_g, w_in, b_forget, ln_v_g, w_spatial, b_spatial, w_branch_a, w_branch_b, w_out, norm2_g, w_query, sub_keys, expert_u, expert_v, final_g):
    B, S, D = x.shape
    l = 0
    mod = jax.nn.silu(c) @ w_mod[l] + b_mod[l]
    sh1, sc1, g1, sh2, sc2, g2 = jnp.split(mod, 6, axis=-1)
    h = _modulate(_rmsnorm(x, norm1_g[l]), sh1, sc1)
    proj = h @ w_in[l]
    q, k, v, f_logit, z, gate_a, gate_b = jnp.split(proj, SPLIT_POINTS, axis=-1)
    log_f = jax.nn.log_sigmoid(f_logit + b_forget[l])
    y_a = _fox(q.reshape(B, S, ATT_HEADS, ATT_HEAD_DIM),
               k.reshape(B, S, ATT_HEADS, ATT_HEAD_DIM),
               v.reshape(B, S, ATT_HEADS, ATT_HEAD_DIM), log_f)
    y_b = _gmlp(jax.nn.gelu(z), ln_v_g[l], w_spatial[l], b_spatial[l])
    bf16 = jnp.bfloat16
    n = B * S
    sga = jax.nn.sigmoid(gate_a).reshape(n, D).astype(bf16)
    gb = (jax.nn.sigmoid(gate_b) * (y_b @ w_branch_b[l])).reshape(n, D).astype(bf16)
    x1, h2, idx_t, gates_t = _post_attention(
        y_a.reshape(n, ATT_WIDTH).astype(bf16), sga, gb, x.reshape(n, D), mod.reshape(B, 6, D),
        norm2_g[l].reshape(1, D), w_branch_a[l].astype(bf16), w_out[l].astype(bf16),
        w_query[l].astype(bf16), sub_keys[l], S)
    table = _pack_expert_table(expert_u[l], expert_v[l])
    out = _peer_experts_final(idx_t.T, gates_t.T, h2, x1, g2.reshape(B, 1, D), final_g, table, S)
    return out.reshape(B, S, D)
```

```python
import functools

import jax
import jax.numpy as jnp
from jax import lax
from jax.experimental import pallas as pl
from jax.experimental.pallas import tpu as pltpu

D_MODEL = 1024
ATT_HEADS = 8
ATT_HEAD_DIM = 64
ATT_WIDTH = ATT_HEADS * ATT_HEAD_DIM
Q_BLOCK = 128
GM_GROUPS = 4
GM_GROUP_DIM = 128
GM_WIDTH = GM_GROUPS * GM_GROUP_DIM
GM_CHUNK = 128
PEER_HEADS = 8
PEER_KEY_DIM = 256
PEER_HALF = PEER_KEY_DIM // 2
N_KEYS = 128
PEER_TOPK = 16
PEER_TOK_BLOCK = 128
PEER_SLOTS = PEER_HEADS * PEER_TOPK
SPLIT_POINTS = (ATT_WIDTH, 2 * ATT_WIDTH, 3 * ATT_WIDTH, 3 * ATT_WIDTH + ATT_HEADS,
                3 * ATT_WIDTH + ATT_HEADS + 2 * GM_WIDTH,
                3 * ATT_WIDTH + ATT_HEADS + 2 * GM_WIDTH + D_MODEL)
EPS = 1e-6

LANES = 128
EXPERT_TOK_BLOCK = 128
EXPERT_SLOTS = 4
DMA_THREADS = 2


def _gelu(x):
    return 0.5 * x * (1.0 + jnp.tanh(0.7978845608028654 * (x + 0.044715 * (x * x * x))))


def _rmsnorm(x, g):
    y = x * lax.rsqrt(jnp.mean(x * x, axis=-1, keepdims=True) + EPS)
    return y * g


def _layernorm(x, g):
    mu = jnp.mean(x, axis=-1, keepdims=True)
    var = jnp.mean(jnp.square(x - mu), axis=-1, keepdims=True)
    return (x - mu) * lax.rsqrt(var + EPS) * g


def _modulate(h, shift, scale):
    return h * (1.0 + scale[:, None, :]) + shift[:, None, :]


def _fox(q, k, v, log_f):
    B, S, H, Dh = q.shape
    nb = S // Q_BLOCK
    cum = jnp.cumsum(log_f, axis=1)
    cum_k = jnp.transpose(cum, (0, 2, 1))
    k_pos = jnp.arange(S)
    q_blocks = q.reshape(B, nb, Q_BLOCK, H, Dh).transpose(1, 0, 2, 3, 4)
    cq_blocks = cum_k.reshape(B, H, nb, Q_BLOCK).transpose(2, 0, 1, 3)
    starts = jnp.arange(nb) * Q_BLOCK
    scale = ATT_HEAD_DIM ** -0.5

    def one_block(args):
        q_blk, cq_blk, start = args
        s = jnp.einsum('bqhd,bkhd->bhqk', q_blk, k) * scale
        s = s + cq_blk[..., :, None] - cum_k[..., None, :]
        q_pos = start + jnp.arange(Q_BLOCK)
        causal = q_pos[:, None] >= k_pos[None, :]
        s = jnp.where(causal[None, None], s, -jnp.inf)
        p = jax.nn.softmax(s, axis=-1)
        return jnp.einsum('bhqk,bkhd->bqhd', p, v)

    out = lax.map(one_block, (q_blocks, cq_blocks, starts))
    return out.transpose(1, 0, 2, 3, 4).reshape(B, S, H * Dh)


def _gmlp(z, ln_g, w_s, b_s):
    B, S, _ = z.shape
    u, v = jnp.split(z, 2, axis=-1)
    v = _layernorm(v, ln_g)
    nc = S // GM_CHUNK
    v = v.reshape(B, nc, GM_CHUNK, GM_GROUPS, GM_GROUP_DIM)
    tril = jnp.tril(jnp.ones((GM_CHUNK, GM_CHUNK), dtype=bool))
    w = jnp.where(tril[None], w_s, jnp.zeros_like(w_s))
    mixed = jnp.einsum('gts,bcsgd->bctgd', w, v) + b_s.T[None, None, :, :, None]
    return u * mixed.reshape(B, S, GM_WIDTH)


def _peer_route(hf, w_query, sub_keys):
    n = hf.shape[0]
    q = (hf @ w_query).reshape(n, PEER_HEADS, 2, PEER_HALF)
    scores = jnp.einsum('nhpd,hpkd->nhpk', q, sub_keys)
    s_top, i_top = lax.top_k(scores, PEER_TOPK)
    cand_s = (s_top[:, :, 0, :, None] + s_top[:, :, 1, None, :]).reshape(n, PEER_HEADS, PEER_TOPK * PEER_TOPK)
    cand_i = (i_top[:, :, 0, :, None] * N_KEYS + i_top[:, :, 1, None, :]).reshape(n, PEER_HEADS, PEER_TOPK * PEER_TOPK)
    best_s, best_pos = lax.top_k(cand_s, PEER_TOPK)
    idx = jnp.take_along_axis(cand_i, best_pos, axis=-1)
    gates = jax.nn.softmax(best_s, axis=-1)
    return idx.reshape(n, PEER_SLOTS), gates.reshape(n, PEER_SLOTS)


def _topk_rows(s, k):
    rows, t = s.shape
    iota = lax.broadcasted_iota(jnp.int32, (rows, t), 0)
    slot = lax.broadcasted_iota(jnp.int32, (k, t), 0)
    vals = jnp.zeros((k, t), jnp.float32)
    ids = jnp.zeros((k, t), jnp.int32)
    for j in range(k):
        m = jnp.max(s, axis=0, keepdims=True)
        am = jnp.min(jnp.where(s == m, iota, rows), axis=0, keepdims=True)
        vals = jnp.where(slot == j, m, vals)
        ids = jnp.where(slot == j, am, ids)
        s = jnp.where(iota == am, -jnp.inf, s)
    return vals, ids


def _select_rows(table, pos):
    out = jnp.zeros(pos.shape, table.dtype)
    for r in range(table.shape[0]):
        out = jnp.where(pos == r, table[r:r + 1, :], out)
    return out


def _post_kernel(ya_ref, sga_ref, gb_ref, x_ref, mod_ref, n2g_ref, wa_ref, wo_ref, wq_ref, keys_ref,
                 x1_ref, h2_ref, idx_ref, gates_ref, q_scr):
    f32 = jnp.float32
    a = jnp.dot(ya_ref[...], wa_ref[...], preferred_element_type=f32)
    merged = sga_ref[...].astype(f32) * a + gb_ref[...].astype(f32)
    o = jnp.dot(merged.astype(jnp.bfloat16), wo_ref[...], preferred_element_type=f32)
    g1 = mod_ref[0, 2:3, :]
    sh2 = mod_ref[0, 3:4, :]
    sc2 = mod_ref[0, 4:5, :]
    x1 = x_ref[...] + g1 * o
    x1_ref[...] = x1
    h2 = x1 * lax.rsqrt(jnp.mean(x1 * x1, axis=-1, keepdims=True) + EPS) * n2g_ref[...]
    h2 = h2 * (1.0 + sc2) + sh2
    h2_ref[...] = h2
    qp = jnp.dot(h2.astype(jnp.bfloat16), wq_ref[...], preferred_element_type=f32)
    for j in range(2 * PEER_HEADS):
        q_scr[j] = qp[:, j * PEER_HALF:(j + 1) * PEER_HALF]

    def head(h, carry):
        tops = []
        for p in range(2):
            q = q_scr[2 * h + p]
            keys = keys_ref[2 * h + p]
            sc = lax.dot_general(keys, q, (((1,), (1,)), ((), ())),
                                 precision=lax.Precision.HIGHEST,
                                 preferred_element_type=f32)
            tops.append(_topk_rows(sc, PEER_TOPK))
        (s1, i1), (s2, i2) = tops
        t = s1.shape[1]
        cand = (s1[:, None, :] + s2[None, :, :]).reshape(PEER_TOPK * PEER_TOPK, t)
        vals, pos = _topk_rows(cand, PEER_TOPK)
        eid = (_select_rows(i1, pos >> 4) * N_KEYS + _select_rows(i2, pos & (PEER_TOPK - 1)))
        e = jnp.exp(vals - vals[0:1, :])
        g = e / jnp.sum(e, axis=0, keepdims=True)
        row = pl.multiple_of(h * PEER_TOPK, PEER_TOPK)
        idx_ref[pl.ds(row, PEER_TOPK), :] = eid
        gates_ref[pl.ds(row, PEER_TOPK), :] = g
        return carry

    lax.fori_loop(0, PEER_HEADS, head, 0)


POST_TOK_BLOCK = 256


def _post_attention(ya, sga, gb, x, mod, norm2_g, w_a, w_out, w_query, sub_keys, seq_len):
    n, d = x.shape
    tm = POST_TOK_BLOCK
    blocks_per_seq = seq_len // tm
    aw = ya.shape[1]
    qw = w_query.shape[1]
    keys = sub_keys.reshape(2 * PEER_HEADS, N_KEYS, PEER_HALF)
    tok = lambda w: pl.BlockSpec((tm, w), lambda i: (i, 0))
    full = lambda a: pl.BlockSpec(a.shape, lambda i: (0,) * a.ndim)
    return pl.pallas_call(
        _post_kernel,
        grid=(n // tm,),
        in_specs=[tok(aw), tok(d), tok(d), tok(d),
                  pl.BlockSpec((1, 6, d), lambda i: (i // blocks_per_seq, 0, 0)),
                  full(norm2_g), full(w_a), full(w_out), full(w_query), full(keys)],
        out_specs=[tok(d), tok(d),
                   pl.BlockSpec((PEER_SLOTS, tm), lambda i: (0, i)),
                   pl.BlockSpec((PEER_SLOTS, tm), lambda i: (0, i))],
        out_shape=[jax.ShapeDtypeStruct((n, d), jnp.float32),
                   jax.ShapeDtypeStruct((n, d), jnp.float32),
                   jax.ShapeDtypeStruct((PEER_SLOTS, n), jnp.int32),
                   jax.ShapeDtypeStruct((PEER_SLOTS, n), jnp.float32)],
        scratch_shapes=[pltpu.VMEM((2 * PEER_HEADS, tm, PEER_HALF), jnp.float32)],
        compiler_params=pltpu.CompilerParams(dimension_semantics=("arbitrary",),
                                             vmem_limit_bytes=48 * 1024 * 1024),
        name="post_attention",
    )(ya, sga, gb, x, mod, norm2_g, w_a, w_out, w_query, keys)


def _rowsum_bcast(p, ones_bf16):
    hi = p.astype(jnp.bfloat16)
    lo = (p - hi.astype(jnp.float32)).astype(jnp.bfloat16)
    return (jnp.dot(hi, ones_bf16, preferred_element_type=jnp.float32)
            + jnp.dot(lo, ones_bf16, preferred_element_type=jnp.float32))


def _expert_kernel(idx_ref, gates_ref, h_ref, x_ref, g2_ref, fg_ref, tab_ref, out_ref,
                   *scratch, tok_block, n_slots):
    bufs = scratch[:n_slots]
    peer, sem = scratch[n_slots], scratch[n_slots + 1]
    d_model = h_ref.shape[-1]
    n_chunks = d_model // LANES
    rows_per_tok = PEER_SLOTS * n_chunks

    def issue(t, s):
        for r in range(PEER_SLOTS):
            row = pl.multiple_of(idx_ref[t, r] * n_chunks, n_chunks)
            pltpu.make_async_copy(tab_ref.at[pl.ds(row, n_chunks), :],
                                  bufs[s].at[pl.ds(r * n_chunks, n_chunks), :],
                                  sem.at[s]).start(priority=r % DMA_THREADS)

    def wait(s):
        pltpu.make_async_copy(tab_ref.at[pl.ds(0, rows_per_tok), :], bufs[s], sem.at[s]).wait()

    ones_bf16 = jnp.ones((LANES, LANES), jnp.bfloat16)
    eye = (lax.broadcasted_iota(jnp.int32, (PEER_SLOTS, LANES), 0)
           == lax.broadcasted_iota(jnp.int32, (PEER_SLOTS, LANES), 1))

    def chunk(s, c):
        return bufs[s][pl.ds(c, PEER_SLOTS, stride=n_chunks), :]

    def compute(t, s):
        hrow = h_ref[pl.ds(t, 1), :]
        psum = None
        for c in range(n_chunks):
            u = lax.bitcast_convert_type(chunk(s, c) & jnp.uint32(0xFFFF0000), jnp.float32)
            p = u * hrow[:, c * LANES:(c + 1) * LANES]
            psum = p if psum is None else psum + p
        act = _gelu(_rowsum_bcast(psum, ones_bf16))
        grow = gates_ref[pl.ds(t, 1), :]
        gcol = _rowsum_bcast(jnp.where(eye, grow, 0.0), ones_bf16)
        coef = gcol * act
        outs = []
        for c in range(n_chunks):
            v = lax.bitcast_convert_type(chunk(s, c) << 16, jnp.float32)
            outs.append(jnp.sum(coef * v, axis=0, keepdims=True))
        peer[pl.ds(t, 1), :] = jnp.concatenate(outs, axis=-1)

    def step(t, s, prefetch):
        wait(s)
        if prefetch:
            issue(t + n_slots - 1, (s - 1) % n_slots)
        compute(t, s)

    for t in range(n_slots - 1):
        issue(t, t)

    n_groups = tok_block // n_slots

    def group(g, carry):
        for s in range(n_slots):
            step(g * n_slots + s, s, True)
        return carry

    lax.fori_loop(0, n_groups - 1, group, 0)
    for s in range(n_slots):
        t = (n_groups - 1) * n_slots + s
        step(t, s, t + n_slots - 1 < tok_block)

    y = x_ref[...] + g2_ref[0] * peer[...]
    out_ref[...] = y * lax.rsqrt(jnp.mean(y * y, axis=-1, keepdims=True) + EPS) * fg_ref[...]


def _pack_expert_table(expert_u, expert_v):
    ub = lax.bitcast_convert_type(expert_u.astype(jnp.bfloat16), jnp.uint16).astype(jnp.uint32)
    vb = lax.bitcast_convert_type(expert_v.astype(jnp.bfloat16), jnp.uint16).astype(jnp.uint32)
    packed = (ub << 16) | vb
    n_experts, d = packed.shape
    return packed.reshape(n_experts * (d // LANES), LANES)


def _peer_experts_final(idx, gates, h2, x1, g2, final_g, table, seq_len):
    n, d = h2.shape
    tb = EXPERT_TOK_BLOCK
    assert tb % EXPERT_SLOTS == 0 and seq_len % tb == 0
    blocks_per_seq = seq_len // tb
    kern = functools.partial(_expert_kernel, tok_block=tb, n_slots=EXPERT_SLOTS)
    return pl.pallas_call(
        kern,
        grid=(n // tb,),
        in_specs=[
            pl.BlockSpec((tb, PEER_SLOTS), lambda i: (i, 0), memory_space=pltpu.SMEM),
            pl.BlockSpec((tb, PEER_SLOTS), lambda i: (i, 0)),
            pl.BlockSpec((tb, d), lambda i: (i, 0)),
            pl.BlockSpec((tb, d), lambda i: (i, 0)),
            pl.BlockSpec((1, 1, d), lambda i: (i // blocks_per_seq, 0, 0)),
            pl.BlockSpec((1, d), lambda i: (0, 0)),
            pl.BlockSpec(memory_space=pl.ANY),
        ],
        out_specs=pl.BlockSpec((tb, d), lambda i: (i, 0)),
        out_shape=jax.ShapeDtypeStruct((n, d), jnp.float32),
        scratch_shapes=(
            [pltpu.VMEM((PEER_SLOTS * (d // LANES), LANES), jnp.uint32) for _ in range(EXPERT_SLOTS)]
            + [pltpu.VMEM((tb, d), jnp.float32), pltpu.SemaphoreType.DMA((EXPERT_SLOTS,))]),
        compiler_params=pltpu.CompilerParams(dimension_semantics=("arbitrary",)),
        name="peer_experts",
    )(idx, gates, h2, x1, g2, final_g.reshape(1, d), table)


def kernel(x, c, w_mod, b_mod, norm1_g, w_in, b_forget, ln_v_g, w_spatial, b_spatial, w_branch_a, w_branch_b, w_out, norm2_g, w_query, sub_keys, expert_u, expert_v, final_g):
    B, S, D = x.shape
    l = 0
    mod = jax.nn.silu(c) @ w_mod[l] + b_mod[l]
    sh1, sc1, g1, sh2, sc2, g2 = jnp.split(mod, 6, axis=-1)
    h = _modulate(_rmsnorm(x, norm1_g[l]), sh1, sc1)
    proj = h @ w_in[l]
    q, k, v, f_logit, z, gate_a, gate_b = jnp.split(proj, SPLIT_POINTS, axis=-1)
    log_f = jax.nn.log_sigmoid(f_logit + b_forget[l])
    y_a = _fox(q.reshape(B, S, ATT_HEADS, ATT_HEAD_DIM),
               k.reshape(B, S, ATT_HEADS, ATT_HEAD_DIM),
               v.reshape(B, S, ATT_HEADS, ATT_HEAD_DIM), log_f)
    y_b = _gmlp(jax.nn.gelu(z), ln_v_g[l], w_spatial[l], b_spatial[l])
    bf16 = jnp.bfloat16
    n = B * S
    sga = jax.nn.sigmoid(gate_a).reshape(n, D).astype(bf16)
    gb = (jax.nn.sigmoid(gate_b) * (y_b @ w_branch_b[l])).reshape(n, D).astype(bf16)
    x1, h2, idx_t, gates_t = _post_attention(
        y_a.reshape(n, ATT_WIDTH).astype(bf16), sga, gb, x.reshape(n, D), mod.reshape(B, 6, D),
        norm2_g[l].reshape(1, D), w_branch_a[l].astype(bf16), w_out[l].astype(bf16),
        w_query[l].astype(bf16), sub_keys[l], S)
    table = _pack_expert_table(expert_u[l], expert_v[l])
    out = _peer_experts_final(idx_t.T, gates_t.T, h2, x1, g2.reshape(B, 1, D), final_g, table, S)
    return out.reshape(B, S, D)
```

```python
import functools

import jax
import jax.numpy as jnp
from jax import lax
from jax.experimental import pallas as pl
from jax.experimental.pallas import tpu as pltpu

D_MODEL = 1024
ATT_HEADS = 8
ATT_HEAD_DIM = 64
ATT_WIDTH = ATT_HEADS * ATT_HEAD_DIM
Q_BLOCK = 128
GM_GROUPS = 4
GM_GROUP_DIM = 128
GM_WIDTH = GM_GROUPS * GM_GROUP_DIM
GM_CHUNK = 128
PEER_HEADS = 8
PEER_KEY_DIM = 256
PEER_HALF = PEER_KEY_DIM // 2
N_KEYS = 128
PEER_TOPK = 16
PEER_TOK_BLOCK = 128
PEER_SLOTS = PEER_HEADS * PEER_TOPK
SPLIT_POINTS = (ATT_WIDTH, 2 * ATT_WIDTH, 3 * ATT_WIDTH, 3 * ATT_WIDTH + ATT_HEADS,
                3 * ATT_WIDTH + ATT_HEADS + 2 * GM_WIDTH,
                3 * ATT_WIDTH + ATT_HEADS + 2 * GM_WIDTH + D_MODEL)
EPS = 1e-6

LANES = 128
EXPERT_TOK_BLOCK = 128
EXPERT_SLOTS = 4
DMA_THREADS = 2


def _gelu(x):
    return 0.5 * x * (1.0 + jnp.tanh(0.7978845608028654 * (x + 0.044715 * (x * x * x))))


def _rmsnorm(x, g):
    y = x * lax.rsqrt(jnp.mean(x * x, axis=-1, keepdims=True) + EPS)
    return y * g


def _layernorm(x, g):
    mu = jnp.mean(x, axis=-1, keepdims=True)
    var = jnp.mean(jnp.square(x - mu), axis=-1, keepdims=True)
    return (x - mu) * lax.rsqrt(var + EPS) * g


def _modulate(h, shift, scale):
    return h * (1.0 + scale[:, None, :]) + shift[:, None, :]


def _fox(q, k, v, log_f):
    B, S, H, Dh = q.shape
    nb = S // Q_BLOCK
    cum = jnp.cumsum(log_f, axis=1)
    cum_k = jnp.transpose(cum, (0, 2, 1))
    k_pos = jnp.arange(S)
    q_blocks = q.reshape(B, nb, Q_BLOCK, H, Dh).transpose(1, 0, 2, 3, 4)
    cq_blocks = cum_k.reshape(B, H, nb, Q_BLOCK).transpose(2, 0, 1, 3)
    starts = jnp.arange(nb) * Q_BLOCK
    scale = ATT_HEAD_DIM ** -0.5

    def one_block(args):
        q_blk, cq_blk, start = args
        s = jnp.einsum('bqhd,bkhd->bhqk', q_blk, k) * scale
        s = s + cq_blk[..., :, None] - cum_k[..., None, :]
        q_pos = start + jnp.arange(Q_BLOCK)
        causal = q_pos[:, None] >= k_pos[None, :]
        s = jnp.where(causal[None, None], s, -jnp.inf)
        p = jax.nn.softmax(s, axis=-1)
        return jnp.einsum('bhqk,bkhd->bqhd', p, v)

    out = lax.map(one_block, (q_blocks, cq_blocks, starts))
    return out.transpose(1, 0, 2, 3, 4).reshape(B, S, H * Dh)


def _gmlp(z, ln_g, w_s, b_s):
    B, S, _ = z.shape
    u, v = jnp.split(z, 2, axis=-1)
    v = _layernorm(v, ln_g)
    nc = S // GM_CHUNK
    v = v.reshape(B, nc, GM_CHUNK, GM_GROUPS, GM_GROUP_DIM)
    tril = jnp.tril(jnp.ones((GM_CHUNK, GM_CHUNK), dtype=bool))
    w = jnp.where(tril[None], w_s, jnp.zeros_like(w_s))
    mixed = jnp.einsum('gts,bcsgd->bctgd', w, v) + b_s.T[None, None, :, :, None]
    return u * mixed.reshape(B, S, GM_WIDTH)


def _peer_route(hf, w_query, sub_keys):
    n = hf.shape[0]
    q = (hf @ w_query).reshape(n, PEER_HEADS, 2, PEER_HALF)
    scores = jnp.einsum('nhpd,hpkd->nhpk', q, sub_keys)
    s_top, i_top = lax.top_k(scores, PEER_TOPK)
    cand_s = (s_top[:, :, 0, :, None] + s_top[:, :, 1, None, :]).reshape(n, PEER_HEADS, PEER_TOPK * PEER_TOPK)
    cand_i = (i_top[:, :, 0, :, None] * N_KEYS + i_top[:, :, 1, None, :]).reshape(n, PEER_HEADS, PEER_TOPK * PEER_TOPK)
    best_s, best_pos = lax.top_k(cand_s, PEER_TOPK)
    idx = jnp.take_along_axis(cand_i, best_pos, axis=-1)
    gates = jax.nn.softmax(best_s, axis=-1)
    return idx.reshape(n, PEER_SLOTS), gates.reshape(n, PEER_SLOTS)


def _mod_kernel(c_ref, w_ref, b_ref, o_ref):
    c = c_ref[...]
    sc = c * jax.nn.sigmoid(c)
    o_ref[...] = jnp.dot(sc, w_ref[...], precision=lax.Precision.HIGHEST,
                         preferred_element_type=jnp.float32) + b_ref[...]


def _modulation(c, w_mod, b_mod):
    b, d = c.shape
    cols = w_mod.shape[1]
    return pl.pallas_call(
        _mod_kernel,
        grid=(cols // d,),
        in_specs=[pl.BlockSpec((b, d), lambda j: (0, 0)),
                  pl.BlockSpec((d, d), lambda j: (0, j)),
                  pl.BlockSpec((1, d), lambda j: (0, j))],
        out_specs=pl.BlockSpec((b, d), lambda j: (0, j)),
        out_shape=jax.ShapeDtypeStruct((b, cols), jnp.float32),
        name="modulation",
    )(c, w_mod, b_mod.reshape(1, cols))


INPROJ_TOK_BLOCK = 256


def _inproj_kernel(x_ref, mod_ref, n1g_ref, wqkv_ref, wf_ref, bf_ref, wz_ref, wg_ref, lng_ref,
                   wsp_ref, bsp_ref, wb_ref, qkv_ref, cum_ref, sga_ref, gb_ref, carry):
    f32, bf16 = jnp.float32, jnp.bfloat16
    tm, d = x_ref.shape[1], x_ref.shape[2]
    x = x_ref[0]
    sh1 = mod_ref[0, 0:1, :]
    sc1 = mod_ref[0, 1:2, :]
    h = x * lax.rsqrt(jnp.mean(x * x, axis=-1, keepdims=True) + EPS) * n1g_ref[...]
    hb = (h * (1.0 + sc1) + sh1).astype(bf16)

    qkv = jnp.dot(hb, wqkv_ref[...], preferred_element_type=f32)
    qkv_ref[:, 0:ATT_WIDTH] = (qkv[:, 0:ATT_WIDTH] * (ATT_HEAD_DIM ** -0.5)).astype(bf16)
    qkv_ref[:, ATT_WIDTH:] = qkv[:, ATT_WIDTH:].astype(bf16)

    f = jnp.dot(hb, wf_ref[...], preferred_element_type=f32) + bf_ref[...]
    logf = jnp.minimum(f, 0.0) - jnp.log1p(jnp.exp(-jnp.abs(f)))

    @pl.when(pl.program_id(1) == 0)
    def _():
        carry[...] = jnp.zeros_like(carry)

    tri = (lax.broadcasted_iota(jnp.int32, (tm, tm), 0)
           >= lax.broadcasted_iota(jnp.int32, (tm, tm), 1)).astype(f32)
    cum = jnp.dot(tri, logf, precision=lax.Precision.HIGHEST, preferred_element_type=f32) + carry[...]
    cum_ref[...] = cum
    carry[...] = cum[tm - 1:tm, :]

    gz = _gelu(jnp.dot(hb, wz_ref[...], preferred_element_type=f32))
    u = gz[:, 0:GM_WIDTH]
    v = gz[:, GM_WIDTH:]
    mu = jnp.mean(v, axis=-1, keepdims=True)
    var = jnp.mean(jnp.square(v - mu), axis=-1, keepdims=True)
    vn = ((v - mu) * lax.rsqrt(var + EPS) * lng_ref[...]).astype(bf16)
    tril = (lax.broadcasted_iota(jnp.int32, (GM_CHUNK, GM_CHUNK), 0)
            >= lax.broadcasted_iota(jnp.int32, (GM_CHUNK, GM_CHUNK), 1))
    w_sp = [jnp.where(tril, wsp_ref[g], 0.0).astype(bf16) for g in range(GM_GROUPS)]
    rows = []
    for ck in range(tm // GM_CHUNK):
        r0 = ck * GM_CHUNK
        cols = []
        for g in range(GM_GROUPS):
            c0 = g * GM_GROUP_DIM
            mixed = jnp.dot(w_sp[g], vn[r0:r0 + GM_CHUNK, c0:c0 + GM_GROUP_DIM],
                            preferred_element_type=f32) + bsp_ref[g]
            cols.append(u[r0:r0 + GM_CHUNK, c0:c0 + GM_GROUP_DIM] * mixed)
        rows.append(jnp.concatenate(cols, axis=1))
    yb = jnp.concatenate(rows, axis=0).astype(bf16)
    ybp = jnp.dot(yb, wb_ref[...], preferred_element_type=f32)

    sg = jax.nn.sigmoid(jnp.dot(hb, wg_ref[...], preferred_element_type=f32))
    sga_ref[...] = sg[:, 0:d].astype(bf16)
    gb_ref[...] = (sg[:, d:] * ybp).astype(bf16)


def _input_projection(x, mod, norm1_g, w_in, b_forget, ln_v_g, w_spatial, b_spatial, w_branch_b):
    B, S, d = x.shape
    n = B * S
    tm = INPROJ_TOK_BLOCK
    bf16 = jnp.bfloat16
    p0, p1, p2, p3, p4, p5 = SPLIT_POINTS
    w_qkv = w_in[:, 0:p2].astype(bf16)
    w_f = jnp.pad(w_in[:, p2:p3], ((0, 0), (0, LANES - ATT_HEADS))).astype(bf16)
    b_f = jnp.pad(b_forget, (0, LANES - ATT_HEADS)).reshape(1, LANES)
    w_z = w_in[:, p3:p4].astype(bf16)
    w_g = w_in[:, p4:].astype(bf16)
    nt = S // tm
    tok = lambda w: pl.BlockSpec((tm, w), lambda b, i: (b * nt + i, 0))
    full = lambda a: pl.BlockSpec(a.shape, lambda b, i: (0,) * a.ndim)
    args = (x, mod, norm1_g.reshape(1, d), w_qkv, w_f, b_f, w_z, w_g, ln_v_g.reshape(1, GM_WIDTH),
            w_spatial, b_spatial.reshape(GM_GROUPS, GM_CHUNK, 1), w_branch_b.astype(bf16))
    return pl.pallas_call(
        _inproj_kernel,
        grid=(B, nt),
        in_specs=[pl.BlockSpec((1, tm, d), lambda b, i: (b, i, 0)),
                  pl.BlockSpec((1, 6, d), lambda b, i: (b, 0, 0))] + [full(a) for a in args[2:]],
        out_specs=[tok(3 * ATT_WIDTH), tok(LANES), tok(d), tok(d)],
        out_shape=[jax.ShapeDtypeStruct((n, 3 * ATT_WIDTH), bf16),
                   jax.ShapeDtypeStruct((n, LANES), jnp.float32),
                   jax.ShapeDtypeStruct((n, d), bf16),
                   jax.ShapeDtypeStruct((n, d), bf16)],
        scratch_shapes=[pltpu.VMEM((1, LANES), jnp.float32)],
        compiler_params=pltpu.CompilerParams(dimension_semantics=("arbitrary", "arbitrary"),
                                             vmem_limit_bytes=56 * 1024 * 1024),
        name="input_projection",
    )(*args)


ATT_BLOCK = 512


def _fox_kernel(q_ref, k_ref, v_ref, cq_ref, ck_ref, o_ref, *, blk):
    f32 = jnp.float32
    i = pl.program_id(2)
    q = q_ref[0, 0]
    cq = cq_ref[0, 0]

    def block(j, carry, masked):
        m, l, acc = carry
        off = pl.multiple_of(j * blk, blk)
        k = k_ref[0, 0, pl.ds(off, blk), :]
        v = v_ref[0, 0, pl.ds(off, blk), :]
        s = lax.dot_general(q, k, (((1,), (1,)), ((), ())), preferred_element_type=f32)
        s = s + (cq - ck_ref[0, 0, j])
        if masked:
            causal = (lax.broadcasted_iota(jnp.int32, (blk, blk), 0)
                      >= lax.broadcasted_iota(jnp.int32, (blk, blk), 1))
            s = jnp.where(causal, s, -jnp.inf)
        m_new = jnp.maximum(m, jnp.max(s, axis=1, keepdims=True))
        alpha = jnp.exp(m - m_new)
        p = jnp.exp(s - m_new)
        l = alpha * l + jnp.sum(p, axis=1, keepdims=True)
        acc = alpha * acc + jnp.dot(p.astype(v.dtype), v, preferred_element_type=f32)
        return m_new, l, acc

    init = (jnp.full((blk, 1), -1e30, f32), jnp.zeros((blk, 1), f32),
            jnp.zeros((blk, q.shape[1]), f32))
    carry = lax.fori_loop(0, i, lambda j, c: block(j, c, False), init)
    m, l, acc = block(i, carry, True)
    o_ref[0, 0] = (acc / l).astype(o_ref.dtype)


def _fox_attention(q, k, v, cum):
    B, H, S, dh = q.shape
    blk = min(ATT_BLOCK, S)
    nb = S // blk
    cq = cum.reshape(B, H, S, 1)
    ck = cum.reshape(B, H, nb, 1, blk)
    return pl.pallas_call(
        functools.partial(_fox_kernel, blk=blk),
        grid=(B, H, nb),
        in_specs=[pl.BlockSpec((1, 1, blk, dh), lambda b, h, i: (b, h, i, 0)),
                  pl.BlockSpec((1, 1, S, dh), lambda b, h, i: (b, h, 0, 0)),
                  pl.BlockSpec((1, 1, S, dh), lambda b, h, i: (b, h, 0, 0)),
                  pl.BlockSpec((1, 1, blk, 1), lambda b, h, i: (b, h, i, 0)),
                  pl.BlockSpec((1, 1, nb, 1, blk), lambda b, h, i: (b, h, 0, 0, 0))],
        out_specs=pl.BlockSpec((1, 1, blk, dh), lambda b, h, i: (b, h, i, 0)),
        out_shape=jax.ShapeDtypeStruct((B, H, S, dh), jnp.bfloat16),
        compiler_params=pltpu.CompilerParams(
            dimension_semantics=("arbitrary", "arbitrary", "arbitrary")),
        name="fox_attention",
    )(q, k, v, cq, ck)


def _topk_rows(s, k):
    rows, t = s.shape
    iota = lax.broadcasted_iota(jnp.int32, (rows, t), 0)
    slot = lax.broadcasted_iota(jnp.int32, (k, t), 0)
    vals = jnp.zeros((k, t), jnp.float32)
    ids = jnp.zeros((k, t), jnp.int32)
    for j in range(k):
        m = jnp.max(s, axis=0, keepdims=True)
        am = jnp.min(jnp.where(s == m, iota, rows), axis=0, keepdims=True)
        vals = jnp.where(slot == j, m, vals)
        ids = jnp.where(slot == j, am, ids)
        s = jnp.where(iota == am, -jnp.inf, s)
    return vals, ids


def _select_rows(table, pos):
    out = jnp.zeros(pos.shape, table.dtype)
    for r in range(table.shape[0]):
        out = jnp.where(pos == r, table[r:r + 1, :], out)
    return out


def _post_kernel(ya_ref, sga_ref, gb_ref, x_ref, mod_ref, n2g_ref, wa_ref, wo_ref, wq_ref, keys_ref,
                 x1_ref, h2_ref, idx_ref, gates_ref, q_scr):
    f32 = jnp.float32
    a = jnp.dot(ya_ref[...], wa_ref[...], preferred_element_type=f32)
    merged = sga_ref[...].astype(f32) * a + gb_ref[...].astype(f32)
    o = jnp.dot(merged.astype(jnp.bfloat16), wo_ref[...], preferred_element_type=f32)
    g1 = mod_ref[0, 2:3, :]
    sh2 = mod_ref[0, 3:4, :]
    sc2 = mod_ref[0, 4:5, :]
    x1 = x_ref[...] + g1 * o
    x1_ref[...] = x1
    h2 = x1 * lax.rsqrt(jnp.mean(x1 * x1, axis=-1, keepdims=True) + EPS) * n2g_ref[...]
    h2 = h2 * (1.0 + sc2) + sh2
    h2_ref[...] = h2
    qp = jnp.dot(h2.astype(jnp.bfloat16), wq_ref[...], preferred_element_type=f32)
    for j in range(2 * PEER_HEADS):
        q_scr[j] = qp[:, j * PEER_HALF:(j + 1) * PEER_HALF]

    def head(h, carry):
        tops = []
        for p in range(2):
            q = q_scr[2 * h + p]
            keys = keys_ref[2 * h + p]
            sc = lax.dot_general(keys, q, (((1,), (1,)), ((), ())),
                                 precision=lax.Precision.HIGHEST,
                                 preferred_element_type=f32)
            tops.append(_topk_rows(sc, PEER_TOPK))
        (s1, i1), (s2, i2) = tops
        t = s1.shape[1]
        cand = (s1[:, None, :] + s2[None, :, :]).reshape(PEER_TOPK * PEER_TOPK, t)
        vals, pos = _topk_rows(cand, PEER_TOPK)
        eid = (_select_rows(i1, pos >> 4) * N_KEYS + _select_rows(i2, pos & (PEER_TOPK - 1)))
        e = jnp.exp(vals - vals[0:1, :])
        g = e / jnp.sum(e, axis=0, keepdims=True)
        row = pl.multiple_of(h * PEER_TOPK, PEER_TOPK)
        idx_ref[pl.ds(row, PEER_TOPK), :] = eid
        gates_ref[pl.ds(row, PEER_TOPK), :] = g
        return carry

    lax.fori_loop(0, PEER_HEADS, head, 0)


POST_TOK_BLOCK = 256


def _post_attention(ya, sga, gb, x, mod, norm2_g, w_a, w_out, w_query, sub_keys, seq_len):
    n, d = x.shape
    tm = POST_TOK_BLOCK
    blocks_per_seq = seq_len // tm
    aw = ya.shape[1]
    qw = w_query.shape[1]
    keys = sub_keys.reshape(2 * PEER_HEADS, N_KEYS, PEER_HALF)
    tok = lambda w: pl.BlockSpec((tm, w), lambda i: (i, 0))
    full = lambda a: pl.BlockSpec(a.shape, lambda i: (0,) * a.ndim)
    return pl.pallas_call(
        _post_kernel,
        grid=(n // tm,),
        in_specs=[tok(aw), tok(d), tok(d), tok(d),
                  pl.BlockSpec((1, 6, d), lambda i: (i // blocks_per_seq, 0, 0)),
                  full(norm2_g), full(w_a), full(w_out), full(w_query), full(keys)],
        out_specs=[tok(d), tok(d),
                   pl.BlockSpec((PEER_SLOTS, tm), lambda i: (0, i)),
                   pl.BlockSpec((PEER_SLOTS, tm), lambda i: (0, i))],
        out_shape=[jax.ShapeDtypeStruct((n, d), jnp.float32),
                   jax.ShapeDtypeStruct((n, d), jnp.float32),
                   jax.ShapeDtypeStruct((PEER_SLOTS, n), jnp.int32),
                   jax.ShapeDtypeStruct((PEER_SLOTS, n), jnp.float32)],
        scratch_shapes=[pltpu.VMEM((2 * PEER_HEADS, tm, PEER_HALF), jnp.float32)],
        compiler_params=pltpu.CompilerParams(dimension_semantics=("arbitrary",),
                                             vmem_limit_bytes=48 * 1024 * 1024),
        name="post_attention",
    )(ya, sga, gb, x, mod, norm2_g, w_a, w_out, w_query, keys)


def _rowsum_bcast(p, ones_bf16):
    hi = p.astype(jnp.bfloat16)
    lo = (p - hi.astype(jnp.float32)).astype(jnp.bfloat16)
    return (jnp.dot(hi, ones_bf16, preferred_element_type=jnp.float32)
            + jnp.dot(lo, ones_bf16, preferred_element_type=jnp.float32))


def _expert_kernel(idx_ref, gates_ref, h_ref, x_ref, g2_ref, fg_ref, tab_ref, out_ref,
                   *scratch, tok_block, n_slots):
    bufs = scratch[:n_slots]
    peer, sem = scratch[n_slots], scratch[n_slots + 1]
    d_model = h_ref.shape[-1]
    n_chunks = d_model // LANES
    rows_per_tok = PEER_SLOTS * n_chunks

    def issue(t, s):
        for r in range(PEER_SLOTS):
            row = pl.multiple_of(idx_ref[t, r] * n_chunks, n_chunks)
            pltpu.make_async_copy(tab_ref.at[pl.ds(row, n_chunks), :],
                                  bufs[s].at[pl.ds(r * n_chunks, n_chunks), :],
                                  sem.at[s]).start(priority=r % DMA_THREADS)

    def wait(s):
        pltpu.make_async_copy(tab_ref.at[pl.ds(0, rows_per_tok), :], bufs[s], sem.at[s]).wait()

    ones_bf16 = jnp.ones((LANES, LANES), jnp.bfloat16)
    eye = (lax.broadcasted_iota(jnp.int32, (PEER_SLOTS, LANES), 0)
           == lax.broadcasted_iota(jnp.int32, (PEER_SLOTS, LANES), 1))

    def chunk(s, c):
        return bufs[s][pl.ds(c, PEER_SLOTS, stride=n_chunks), :]

    def compute(t, s):
        hrow = h_ref[pl.ds(t, 1), :]
        psum = None
        for c in range(n_chunks):
            u = lax.bitcast_convert_type(chunk(s, c) & jnp.uint32(0xFFFF0000), jnp.float32)
            p = u * hrow[:, c * LANES:(c + 1) * LANES]
            psum = p if psum is None else psum + p
        act = _gelu(_rowsum_bcast(psum, ones_bf16))
        grow = gates_ref[pl.ds(t, 1), :]
        gcol = _rowsum_bcast(jnp.where(eye, grow, 0.0), ones_bf16)
        coef = gcol * act
        outs = []
        for c in range(n_chunks):
            v = lax.bitcast_convert_type(chunk(s, c) << 16, jnp.float32)
            outs.append(jnp.sum(coef * v, axis=0, keepdims=True))
        peer[pl.ds(t, 1), :] = jnp.concatenate(outs, axis=-1)

    def step(t, s, prefetch):
        wait(s)
        if prefetch:
            issue(t + n_slots - 1, (s - 1) % n_slots)
        compute(t, s)

    for t in range(n_slots - 1):
        issue(t, t)

    n_groups = tok_block // n_slots

    def group(g, carry):
        for s in range(n_slots):
            step(g * n_slots + s, s, True)
        return carry

    lax.fori_loop(0, n_groups - 1, group, 0)
    for s in range(n_slots):
        t = (n_groups - 1) * n_slots + s
        step(t, s, t + n_slots - 1 < tok_block)

    y = x_ref[...] + g2_ref[0] * peer[...]
    out_ref[...] = y * lax.rsqrt(jnp.mean(y * y, axis=-1, keepdims=True) + EPS) * fg_ref[...]


def _pack_expert_table(expert_u, expert_v):
    ub = lax.bitcast_convert_type(expert_u.astype(jnp.bfloat16), jnp.uint16).astype(jnp.uint32)
    vb = lax.bitcast_convert_type(expert_v.astype(jnp.bfloat16), jnp.uint16).astype(jnp.uint32)
    packed = (ub << 16) | vb
    n_experts, d = packed.shape
    return packed.reshape(n_experts * (d // LANES), LANES)


def _peer_experts_final(idx, gates, h2, x1, g2, final_g, table, seq_len):
    n, d = h2.shape
    tb = EXPERT_TOK_BLOCK
    assert tb % EXPERT_SLOTS == 0 and seq_len % tb == 0
    blocks_per_seq = seq_len // tb
    kern = functools.partial(_expert_kernel, tok_block=tb, n_slots=EXPERT_SLOTS)
    return pl.pallas_call(
        kern,
        grid=(n // tb,),
        in_specs=[
            pl.BlockSpec((tb, PEER_SLOTS), lambda i: (i, 0), memory_space=pltpu.SMEM),
            pl.BlockSpec((tb, PEER_SLOTS), lambda i: (i, 0)),
            pl.BlockSpec((tb, d), lambda i: (i, 0)),
            pl.BlockSpec((tb, d), lambda i: (i, 0)),
            pl.BlockSpec((1, 1, d), lambda i: (i // blocks_per_seq, 0, 0)),
            pl.BlockSpec((1, d), lambda i: (0, 0)),
            pl.BlockSpec(memory_space=pl.ANY),
        ],
        out_specs=pl.BlockSpec((tb, d), lambda i: (i, 0)),
        out_shape=jax.ShapeDtypeStruct((n, d), jnp.float32),
        scratch_shapes=(
            [pltpu.VMEM((PEER_SLOTS * (d // LANES), LANES), jnp.uint32) for _ in range(EXPERT_SLOTS)]
            + [pltpu.VMEM((tb, d), jnp.float32), pltpu.SemaphoreType.DMA((EXPERT_SLOTS,))]),
        compiler_params=pltpu.CompilerParams(dimension_semantics=("arbitrary",)),
        name="peer_experts",
    )(idx, gates, h2, x1, g2, final_g.reshape(1, d), table)


def kernel(x, c, w_mod, b_mod, norm1_g, w_in, b_forget, ln_v_g, w_spatial, b_spatial, w_branch_a, w_branch_b, w_out, norm2_g, w_query, sub_keys, expert_u, expert_v, final_g):
    B, S, D = x.shape
    n = B * S
    bf16 = jnp.bfloat16
    assert w_mod.shape[0] == 1, "the final RMSNorm is fused into the single layer's expert kernel"
    l = 0
    mod = _modulation(c, w_mod[l], b_mod[l]).reshape(B, 6, D)

    qkv, cum, sga, gb = _input_projection(x, mod, norm1_g[l], w_in[l], b_forget[l], ln_v_g[l],
                                          w_spatial[l], b_spatial[l], w_branch_b[l])
    heads = lambda a: a.reshape(B, S, ATT_HEADS, ATT_HEAD_DIM).transpose(0, 2, 1, 3)
    q = heads(qkv[:, 0:ATT_WIDTH])
    k = heads(qkv[:, ATT_WIDTH:2 * ATT_WIDTH])
    v = heads(qkv[:, 2 * ATT_WIDTH:])
    cum_h = cum[:, 0:ATT_HEADS].reshape(B, S, ATT_HEADS).transpose(0, 2, 1)
    y_a = _fox_attention(q, k, v, cum_h).transpose(0, 2, 1, 3).reshape(n, ATT_WIDTH)

    x1, h2, idx_t, gates_t = _post_attention(
        y_a, sga, gb, x.reshape(n, D), mod, norm2_g[l].reshape(1, D),
        w_branch_a[l].astype(bf16), w_out[l].astype(bf16), w_query[l].astype(bf16),
        sub_keys[l], S)
    table = _pack_expert_table(expert_u[l], expert_v[l])
    out = _peer_experts_final(idx_t.T, gates_t.T, h2, x1, mod[:, 5:6, :], final_g, table, S)
    return out.reshape(B, S, D)
```

```python
import functools

import jax
import jax.numpy as jnp
from jax import lax
from jax.experimental import pallas as pl
from jax.experimental.pallas import tpu as pltpu

D_MODEL = 1024
ATT_HEADS = 8
ATT_HEAD_DIM = 64
ATT_WIDTH = ATT_HEADS * ATT_HEAD_DIM
Q_BLOCK = 128
GM_GROUPS = 4
GM_GROUP_DIM = 128
GM_WIDTH = GM_GROUPS * GM_GROUP_DIM
GM_CHUNK = 128
PEER_HEADS = 8
PEER_KEY_DIM = 256
PEER_HALF = PEER_KEY_DIM // 2
N_KEYS = 128
PEER_TOPK = 16
PEER_TOK_BLOCK = 128
PEER_SLOTS = PEER_HEADS * PEER_TOPK
SPLIT_POINTS = (ATT_WIDTH, 2 * ATT_WIDTH, 3 * ATT_WIDTH, 3 * ATT_WIDTH + ATT_HEADS,
                3 * ATT_WIDTH + ATT_HEADS + 2 * GM_WIDTH,
                3 * ATT_WIDTH + ATT_HEADS + 2 * GM_WIDTH + D_MODEL)
EPS = 1e-6

LANES = 128
EXPERT_TOK_BLOCK = 128
EXPERT_SLOTS = 4
DMA_THREADS = 2


def _gelu(x):
    return 0.5 * x * (1.0 + jnp.tanh(0.7978845608028654 * (x + 0.044715 * (x * x * x))))


def _rmsnorm(x, g):
    y = x * lax.rsqrt(jnp.mean(x * x, axis=-1, keepdims=True) + EPS)
    return y * g


def _layernorm(x, g):
    mu = jnp.mean(x, axis=-1, keepdims=True)
    var = jnp.mean(jnp.square(x - mu), axis=-1, keepdims=True)
    return (x - mu) * lax.rsqrt(var + EPS) * g


def _modulate(h, shift, scale):
    return h * (1.0 + scale[:, None, :]) + shift[:, None, :]


def _fox(q, k, v, log_f):
    B, S, H, Dh = q.shape
    nb = S // Q_BLOCK
    cum = jnp.cumsum(log_f, axis=1)
    cum_k = jnp.transpose(cum, (0, 2, 1))
    k_pos = jnp.arange(S)
    q_blocks = q.reshape(B, nb, Q_BLOCK, H, Dh).transpose(1, 0, 2, 3, 4)
    cq_blocks = cum_k.reshape(B, H, nb, Q_BLOCK).transpose(2, 0, 1, 3)
    starts = jnp.arange(nb) * Q_BLOCK
    scale = ATT_HEAD_DIM ** -0.5

    def one_block(args):
        q_blk, cq_blk, start = args
        s = jnp.einsum('bqhd,bkhd->bhqk', q_blk, k) * scale
        s = s + cq_blk[..., :, None] - cum_k[..., None, :]
        q_pos = start + jnp.arange(Q_BLOCK)
        causal = q_pos[:, None] >= k_pos[None, :]
        s = jnp.where(causal[None, None], s, -jnp.inf)
        p = jax.nn.softmax(s, axis=-1)
        return jnp.einsum('bhqk,bkhd->bqhd', p, v)

    out = lax.map(one_block, (q_blocks, cq_blocks, starts))
    return out.transpose(1, 0, 2, 3, 4).reshape(B, S, H * Dh)


def _gmlp(z, ln_g, w_s, b_s):
    B, S, _ = z.shape
    u, v = jnp.split(z, 2, axis=-1)
    v = _layernorm(v, ln_g)
    nc = S // GM_CHUNK
    v = v.reshape(B, nc, GM_CHUNK, GM_GROUPS, GM_GROUP_DIM)
    tril = jnp.tril(jnp.ones((GM_CHUNK, GM_CHUNK), dtype=bool))
    w = jnp.where(tril[None], w_s, jnp.zeros_like(w_s))
    mixed = jnp.einsum('gts,bcsgd->bctgd', w, v) + b_s.T[None, None, :, :, None]
    return u * mixed.reshape(B, S, GM_WIDTH)


def _peer_route(hf, w_query, sub_keys):
    n = hf.shape[0]
    q = (hf @ w_query).reshape(n, PEER_HEADS, 2, PEER_HALF)
    scores = jnp.einsum('nhpd,hpkd->nhpk', q, sub_keys)
    s_top, i_top = lax.top_k(scores, PEER_TOPK)
    cand_s = (s_top[:, :, 0, :, None] + s_top[:, :, 1, None, :]).reshape(n, PEER_HEADS, PEER_TOPK * PEER_TOPK)
    cand_i = (i_top[:, :, 0, :, None] * N_KEYS + i_top[:, :, 1, None, :]).reshape(n, PEER_HEADS, PEER_TOPK * PEER_TOPK)
    best_s, best_pos = lax.top_k(cand_s, PEER_TOPK)
    idx = jnp.take_along_axis(cand_i, best_pos, axis=-1)
    gates = jax.nn.softmax(best_s, axis=-1)
    return idx.reshape(n, PEER_SLOTS), gates.reshape(n, PEER_SLOTS)


def _mod_kernel(c_ref, w_ref, b_ref, o_ref):
    c = c_ref[...]
    sc = c * jax.nn.sigmoid(c)
    o_ref[...] = jnp.dot(sc, w_ref[...], precision=lax.Precision.HIGHEST,
                         preferred_element_type=jnp.float32) + b_ref[...]


def _modulation(c, w_mod, b_mod):
    b, d = c.shape
    cols = w_mod.shape[1]
    return pl.pallas_call(
        _mod_kernel,
        grid=(cols // d,),
        in_specs=[pl.BlockSpec((b, d), lambda j: (0, 0)),
                  pl.BlockSpec((d, d), lambda j: (0, j)),
                  pl.BlockSpec((1, d), lambda j: (0, j))],
        out_specs=pl.BlockSpec((b, d), lambda j: (0, j)),
        out_shape=jax.ShapeDtypeStruct((b, cols), jnp.float32),
        name="modulation",
    )(c, w_mod, b_mod.reshape(1, cols))


INPROJ_TOK_BLOCK = 256


def _inproj_kernel(x_ref, mod_ref, n1g_ref, wqkv_ref, wf_ref, bf_ref, wz_ref, wg_ref, lng_ref,
                   wsp_ref, bsp_ref, wb_ref, qkv_ref, cum_ref, sga_ref, gb_ref, carry):
    f32, bf16 = jnp.float32, jnp.bfloat16
    tm, d = x_ref.shape[1], x_ref.shape[2]
    x = x_ref[0]
    sh1 = mod_ref[0, 0:1, :]
    sc1 = mod_ref[0, 1:2, :]
    h = x * lax.rsqrt(jnp.mean(x * x, axis=-1, keepdims=True) + EPS) * n1g_ref[...]
    hb = (h * (1.0 + sc1) + sh1).astype(bf16)

    qkv = jnp.dot(hb, wqkv_ref[...], preferred_element_type=f32)
    qkv_ref[:, 0:ATT_WIDTH] = (qkv[:, 0:ATT_WIDTH] * (ATT_HEAD_DIM ** -0.5)).astype(bf16)
    qkv_ref[:, ATT_WIDTH:] = qkv[:, ATT_WIDTH:].astype(bf16)

    f = jnp.dot(hb, wf_ref[...], preferred_element_type=f32) + bf_ref[...]
    logf = jnp.minimum(f, 0.0) - jnp.log1p(jnp.exp(-jnp.abs(f)))

    @pl.when(pl.program_id(1) == 0)
    def _():
        carry[...] = jnp.zeros_like(carry)

    tri = (lax.broadcasted_iota(jnp.int32, (tm, tm), 0)
           >= lax.broadcasted_iota(jnp.int32, (tm, tm), 1)).astype(f32)
    cum = jnp.dot(tri, logf, precision=lax.Precision.HIGHEST, preferred_element_type=f32) + carry[...]
    cum_ref[...] = cum
    carry[...] = cum[tm - 1:tm, :]

    gz = _gelu(jnp.dot(hb, wz_ref[...], preferred_element_type=f32))
    u = gz[:, 0:GM_WIDTH]
    v = gz[:, GM_WIDTH:]
    mu = jnp.mean(v, axis=-1, keepdims=True)
    var = jnp.mean(jnp.square(v - mu), axis=-1, keepdims=True)
    vn = ((v - mu) * lax.rsqrt(var + EPS) * lng_ref[...]).astype(bf16)
    tril = (lax.broadcasted_iota(jnp.int32, (GM_CHUNK, GM_CHUNK), 0)
            >= lax.broadcasted_iota(jnp.int32, (GM_CHUNK, GM_CHUNK), 1))
    w_sp = [jnp.where(tril, wsp_ref[g], 0.0).astype(bf16) for g in range(GM_GROUPS)]
    rows = []
    for ck in range(tm // GM_CHUNK):
        r0 = ck * GM_CHUNK
        cols = []
        for g in range(GM_GROUPS):
            c0 = g * GM_GROUP_DIM
            mixed = jnp.dot(w_sp[g], vn[r0:r0 + GM_CHUNK, c0:c0 + GM_GROUP_DIM],
                            preferred_element_type=f32) + bsp_ref[g]
            cols.append(u[r0:r0 + GM_CHUNK, c0:c0 + GM_GROUP_DIM] * mixed)
        rows.append(jnp.concatenate(cols, axis=1))
    yb = jnp.concatenate(rows, axis=0).astype(bf16)
    ybp = jnp.dot(yb, wb_ref[...], preferred_element_type=f32)

    sg = jax.nn.sigmoid(jnp.dot(hb, wg_ref[...], preferred_element_type=f32))
    sga_ref[...] = sg[:, 0:d].astype(bf16)
    gb_ref[...] = (sg[:, d:] * ybp).astype(bf16)


def _input_projection(x, mod, norm1_g, w_in, b_forget, ln_v_g, w_spatial, b_spatial, w_branch_b):
    B, S, d = x.shape
    n = B * S
    tm = INPROJ_TOK_BLOCK
    bf16 = jnp.bfloat16
    p0, p1, p2, p3, p4, p5 = SPLIT_POINTS
    w_qkv = w_in[:, 0:p2].astype(bf16)
    w_f = jnp.pad(w_in[:, p2:p3], ((0, 0), (0, LANES - ATT_HEADS))).astype(bf16)
    b_f = jnp.pad(b_forget, (0, LANES - ATT_HEADS)).reshape(1, LANES)
    w_z = w_in[:, p3:p4].astype(bf16)
    w_g = w_in[:, p4:].astype(bf16)
    nt = S // tm
    tok = lambda w: pl.BlockSpec((tm, w), lambda b, i: (b * nt + i, 0))
    full = lambda a: pl.BlockSpec(a.shape, lambda b, i: (0,) * a.ndim)
    args = (x, mod, norm1_g.reshape(1, d), w_qkv, w_f, b_f, w_z, w_g, ln_v_g.reshape(1, GM_WIDTH),
            w_spatial, b_spatial.reshape(GM_GROUPS, GM_CHUNK, 1), w_branch_b.astype(bf16))
    return pl.pallas_call(
        _inproj_kernel,
        grid=(B, nt),
        in_specs=[pl.BlockSpec((1, tm, d), lambda b, i: (b, i, 0)),
                  pl.BlockSpec((1, 6, d), lambda b, i: (b, 0, 0))] + [full(a) for a in args[2:]],
        out_specs=[tok(3 * ATT_WIDTH), tok(LANES), tok(d), tok(d)],
        out_shape=[jax.ShapeDtypeStruct((n, 3 * ATT_WIDTH), bf16),
                   jax.ShapeDtypeStruct((n, LANES), jnp.float32),
                   jax.ShapeDtypeStruct((n, d), bf16),
                   jax.ShapeDtypeStruct((n, d), bf16)],
        scratch_shapes=[pltpu.VMEM((1, LANES), jnp.float32)],
        compiler_params=pltpu.CompilerParams(dimension_semantics=("arbitrary", "arbitrary"),
                                             vmem_limit_bytes=56 * 1024 * 1024),
        name="input_projection",
    )(*args)


ATT_BLOCK = 512


def _fox_kernel(q_ref, k_ref, v_ref, cq_ref, ck_ref, o_ref, *, blk):
    f32 = jnp.float32
    i = pl.program_id(2)
    q = q_ref[0, 0]
    cq = cq_ref[0, 0]

    def block(j, carry, masked):
        m, l, acc = carry
        off = pl.multiple_of(j * blk, blk)
        k = k_ref[0, 0, pl.ds(off, blk), :]
        v = v_ref[0, 0, pl.ds(off, blk), :]
        s = lax.dot_general(q, k, (((1,), (1,)), ((), ())), preferred_element_type=f32)
        s = s + (cq - ck_ref[0, 0, j])
        if masked:
            causal = (lax.broadcasted_iota(jnp.int32, (blk, blk), 0)
                      >= lax.broadcasted_iota(jnp.int32, (blk, blk), 1))
            s = jnp.where(causal, s, -jnp.inf)
        m_new = jnp.maximum(m, jnp.max(s, axis=1, keepdims=True))
        alpha = jnp.exp(m - m_new)
        p = jnp.exp(s - m_new)
        l = alpha * l + jnp.sum(p, axis=1, keepdims=True)
        acc = alpha * acc + jnp.dot(p.astype(v.dtype), v, preferred_element_type=f32)
        return m_new, l, acc

    init = (jnp.full((blk, 1), -1e30, f32), jnp.zeros((blk, 1), f32),
            jnp.zeros((blk, q.shape[1]), f32))
    carry = lax.fori_loop(0, i, lambda j, c: block(j, c, False), init)
    m, l, acc = block(i, carry, True)
    o_ref[0, 0] = (acc / l).astype(o_ref.dtype)


def _fox_attention(q, k, v, cum):
    B, H, S, dh = q.shape
    blk = min(ATT_BLOCK, S)
    nb = S // blk
    cq = cum.reshape(B, H, S, 1)
    ck = cum.reshape(B, H, nb, 1, blk)
    return pl.pallas_call(
        functools.partial(_fox_kernel, blk=blk),
        grid=(B, H, nb),
        in_specs=[pl.BlockSpec((1, 1, blk, dh), lambda b, h, i: (b, h, i, 0)),
                  pl.BlockSpec((1, 1, S, dh), lambda b, h, i: (b, h, 0, 0)),
                  pl.BlockSpec((1, 1, S, dh), lambda b, h, i: (b, h, 0, 0)),
                  pl.BlockSpec((1, 1, blk, 1), lambda b, h, i: (b, h, i, 0)),
                  pl.BlockSpec((1, 1, nb, 1, blk), lambda b, h, i: (b, h, 0, 0, 0))],
        out_specs=pl.BlockSpec((1, 1, blk, dh), lambda b, h, i: (b, h, i, 0)),
        out_shape=jax.ShapeDtypeStruct((B, H, S, dh), jnp.bfloat16),
        compiler_params=pltpu.CompilerParams(
            dimension_semantics=("arbitrary", "arbitrary", "arbitrary")),
        name="fox_attention",
    )(q, k, v, cq, ck)


def _topk_rows(s, k):
    rows, t = s.shape
    iota = lax.broadcasted_iota(jnp.int32, (rows, t), 0)
    slot = lax.broadcasted_iota(jnp.int32, (k, t), 0)
    vals = jnp.zeros((k, t), jnp.float32)
    ids = jnp.zeros((k, t), jnp.int32)
    for j in range(k):
        m = jnp.max(s, axis=0, keepdims=True)
        am = jnp.min(jnp.where(s == m, iota, rows), axis=0, keepdims=True)
        vals = jnp.where(slot == j, m, vals)
        ids = jnp.where(slot == j, am, ids)
        s = jnp.where(iota == am, -jnp.inf, s)
    return vals, ids


def _select_rows(table, pos):
    out = jnp.zeros(pos.shape, table.dtype)
    for r in range(table.shape[0]):
        out = jnp.where(pos == r, table[r:r + 1, :], out)
    return out


def _post_kernel(ya_ref, sga_ref, gb_ref, x_ref, mod_ref, n2g_ref, wa_ref, wo_ref, wq_ref, keys_ref,
                 x1_ref, h2_ref, idx_ref, gates_ref, q_scr):
    f32 = jnp.float32
    a = jnp.dot(ya_ref[...], wa_ref[...], preferred_element_type=f32)
    merged = sga_ref[...].astype(f32) * a + gb_ref[...].astype(f32)
    o = jnp.dot(merged.astype(jnp.bfloat16), wo_ref[...], preferred_element_type=f32)
    g1 = mod_ref[0, 2:3, :]
    sh2 = mod_ref[0, 3:4, :]
    sc2 = mod_ref[0, 4:5, :]
    x1 = x_ref[...] + g1 * o
    x1_ref[...] = x1
    h2 = x1 * lax.rsqrt(jnp.mean(x1 * x1, axis=-1, keepdims=True) + EPS) * n2g_ref[...]
    h2 = h2 * (1.0 + sc2) + sh2
    h2_ref[...] = h2
    qp = jnp.dot(h2.astype(jnp.bfloat16), wq_ref[...], preferred_element_type=f32)
    for j in range(2 * PEER_HEADS):
        q_scr[j] = qp[:, j * PEER_HALF:(j + 1) * PEER_HALF]

    def head(h, carry):
        tops = []
        for p in range(2):
            q = q_scr[2 * h + p]
            keys = keys_ref[2 * h + p]
            sc = lax.dot_general(keys, q, (((1,), (1,)), ((), ())),
                                 precision=lax.Precision.HIGHEST,
                                 preferred_element_type=f32)
            tops.append(_topk_rows(sc, PEER_TOPK))
        (s1, i1), (s2, i2) = tops
        half = PEER_TOPK // 2
        blocks = [s1[0:1, :] + s2]
        blocks += [s1[a:a + 1, :] + s2[0:half, :] for a in range(1, half)]
        blocks += [s1[half:, :] + s2[0:1, :]]
        vals, pos = _topk_rows(jnp.concatenate(blocks, axis=0), PEER_TOPK)
        mid = pos - PEER_TOPK
        tail0 = PEER_TOPK + (half - 1) * half
        ra = jnp.where(pos < PEER_TOPK, 0,
                       jnp.where(pos < tail0, 1 + (mid >> (half.bit_length() - 1)), pos - tail0 + half))
        rb = jnp.where(pos < PEER_TOPK, pos, jnp.where(pos < tail0, mid & (half - 1), 0))
        eid = _select_rows(i1, ra) * N_KEYS + _select_rows(i2, rb)
        e = jnp.exp(vals - vals[0:1, :])
        g = e / jnp.sum(e, axis=0, keepdims=True)
        row = pl.multiple_of(h * PEER_TOPK, PEER_TOPK)
        idx_ref[pl.ds(row, PEER_TOPK), :] = eid
        gates_ref[pl.ds(row, PEER_TOPK), :] = g
        return carry

    lax.fori_loop(0, PEER_HEADS, head, 0)


POST_TOK_BLOCK = 256


def _post_attention(ya, sga, gb, x, mod, norm2_g, w_a, w_out, w_query, sub_keys, seq_len):
    n, d = x.shape
    tm = POST_TOK_BLOCK
    blocks_per_seq = seq_len // tm
    aw = ya.shape[1]
    qw = w_query.shape[1]
    keys = sub_keys.reshape(2 * PEER_HEADS, N_KEYS, PEER_HALF)
    tok = lambda w: pl.BlockSpec((tm, w), lambda i: (i, 0))
    full = lambda a: pl.BlockSpec(a.shape, lambda i: (0,) * a.ndim)
    return pl.pallas_call(
        _post_kernel,
        grid=(n // tm,),
        in_specs=[tok(aw), tok(d), tok(d), tok(d),
                  pl.BlockSpec((1, 6, d), lambda i: (i // blocks_per_seq, 0, 0)),
                  full(norm2_g), full(w_a), full(w_out), full(w_query), full(keys)],
        out_specs=[tok(d), tok(d),
                   pl.BlockSpec((PEER_SLOTS, tm), lambda i: (0, i)),
                   pl.BlockSpec((PEER_SLOTS, tm), lambda i: (0, i))],
        out_shape=[jax.ShapeDtypeStruct((n, d), jnp.float32),
                   jax.ShapeDtypeStruct((n, d), jnp.float32),
                   jax.ShapeDtypeStruct((PEER_SLOTS, n), jnp.int32),
                   jax.ShapeDtypeStruct((PEER_SLOTS, n), jnp.float32)],
        scratch_shapes=[pltpu.VMEM((2 * PEER_HEADS, tm, PEER_HALF), jnp.float32)],
        compiler_params=pltpu.CompilerParams(dimension_semantics=("arbitrary",),
                                             vmem_limit_bytes=48 * 1024 * 1024),
        name="post_attention",
    )(ya, sga, gb, x, mod, norm2_g, w_a, w_out, w_query, keys)


def _rowsum_bcast(p, ones_bf16):
    hi = p.astype(jnp.bfloat16)
    lo = (p - hi.astype(jnp.float32)).astype(jnp.bfloat16)
    return (jnp.dot(hi, ones_bf16, preferred_element_type=jnp.float32)
            + jnp.dot(lo, ones_bf16, preferred_element_type=jnp.float32))


def _expert_kernel(idx_ref, gates_ref, h_ref, x_ref, g2_ref, fg_ref, tab_ref, out_ref,
                   *scratch, tok_block, n_slots):
    bufs = scratch[:n_slots]
    peer, sem = scratch[n_slots], scratch[n_slots + 1]
    d_model = h_ref.shape[-1]
    n_chunks = d_model // LANES
    rows_per_tok = PEER_SLOTS * n_chunks

    def issue(t, s):
        for r in range(PEER_SLOTS):
            row = pl.multiple_of(idx_ref[t, r] * n_chunks, n_chunks)
            pltpu.make_async_copy(tab_ref.at[pl.ds(row, n_chunks), :],
                                  bufs[s].at[pl.ds(r * n_chunks, n_chunks), :],
                                  sem.at[s]).start(priority=r % DMA_THREADS)

    def wait(s):
        pltpu.make_async_copy(tab_ref.at[pl.ds(0, rows_per_tok), :], bufs[s], sem.at[s]).wait()

    ones_bf16 = jnp.ones((LANES, LANES), jnp.bfloat16)
    eye = (lax.broadcasted_iota(jnp.int32, (PEER_SLOTS, LANES), 0)
           == lax.broadcasted_iota(jnp.int32, (PEER_SLOTS, LANES), 1))

    def chunk(s, c):
        return bufs[s][pl.ds(c, PEER_SLOTS, stride=n_chunks), :]

    def compute(t, s):
        hrow = h_ref[pl.ds(t, 1), :]
        psum = None
        for c in range(n_chunks):
            u = lax.bitcast_convert_type(chunk(s, c) & jnp.uint32(0xFFFF0000), jnp.float32)
            p = u * hrow[:, c * LANES:(c + 1) * LANES]
            psum = p if psum is None else psum + p
        act = _gelu(_rowsum_bcast(psum, ones_bf16))
        grow = gates_ref[pl.ds(t, 1), :]
        gcol = _rowsum_bcast(jnp.where(eye, grow, 0.0), ones_bf16)
        coef = gcol * act
        outs = []
        for c in range(n_chunks):
            v = lax.bitcast_convert_type(chunk(s, c) << 16, jnp.float32)
            outs.append(jnp.sum(coef * v, axis=0, keepdims=True))
        peer[pl.ds(t, 1), :] = jnp.concatenate(outs, axis=-1)

    def step(t, s, prefetch):
        wait(s)
        if prefetch:
            issue(t + n_slots - 1, (s - 1) % n_slots)
        compute(t, s)

    for t in range(n_slots - 1):
        issue(t, t)

    n_groups = tok_block // n_slots

    def group(g, carry):
        for s in range(n_slots):
            step(g * n_slots + s, s, True)
        return carry

    lax.fori_loop(0, n_groups - 1, group, 0)
    for s in range(n_slots):
        t = (n_groups - 1) * n_slots + s
        step(t, s, t + n_slots - 1 < tok_block)

    y = x_ref[...] + g2_ref[0] * peer[...]
    out_ref[...] = y * lax.rsqrt(jnp.mean(y * y, axis=-1, keepdims=True) + EPS) * fg_ref[...]


def _pack_expert_table(expert_u, expert_v):
    ub = lax.bitcast_convert_type(expert_u.astype(jnp.bfloat16), jnp.uint16).astype(jnp.uint32)
    vb = lax.bitcast_convert_type(expert_v.astype(jnp.bfloat16), jnp.uint16).astype(jnp.uint32)
    packed = (ub << 16) | vb
    n_experts, d = packed.shape
    return packed.reshape(n_experts * (d // LANES), LANES)


def _peer_experts_final(idx, gates, h2, x1, g2, final_g, table, seq_len):
    n, d = h2.shape
    tb = EXPERT_TOK_BLOCK
    assert tb % EXPERT_SLOTS == 0 and seq_len % tb == 0
    blocks_per_seq = seq_len // tb
    kern = functools.partial(_expert_kernel, tok_block=tb, n_slots=EXPERT_SLOTS)
    return pl.pallas_call(
        kern,
        grid=(n // tb,),
        in_specs=[
            pl.BlockSpec((tb, PEER_SLOTS), lambda i: (i, 0), memory_space=pltpu.SMEM),
            pl.BlockSpec((tb, PEER_SLOTS), lambda i: (i, 0)),
            pl.BlockSpec((tb, d), lambda i: (i, 0)),
            pl.BlockSpec((tb, d), lambda i: (i, 0)),
            pl.BlockSpec((1, 1, d), lambda i: (i // blocks_per_seq, 0, 0)),
            pl.BlockSpec((1, d), lambda i: (0, 0)),
            pl.BlockSpec(memory_space=pl.ANY),
        ],
        out_specs=pl.BlockSpec((tb, d), lambda i: (i, 0)),
        out_shape=jax.ShapeDtypeStruct((n, d), jnp.float32),
        scratch_shapes=(
            [pltpu.VMEM((PEER_SLOTS * (d // LANES), LANES), jnp.uint32) for _ in range(EXPERT_SLOTS)]
            + [pltpu.VMEM((tb, d), jnp.float32), pltpu.SemaphoreType.DMA((EXPERT_SLOTS,))]),
        compiler_params=pltpu.CompilerParams(dimension_semantics=("arbitrary",)),
        name="peer_experts",
    )(idx, gates, h2, x1, g2, final_g.reshape(1, d), table)


def kernel(x, c, w_mod, b_mod, norm1_g, w_in, b_forget, ln_v_g, w_spatial, b_spatial, w_branch_a, w_branch_b, w_out, norm2_g, w_query, sub_keys, expert_u, expert_v, final_g):
    B, S, D = x.shape
    n = B * S
    bf16 = jnp.bfloat16
    assert w_mod.shape[0] == 1, "the final RMSNorm is fused into the single layer's expert kernel"
    l = 0
    mod = _modulation(c, w_mod[l], b_mod[l]).reshape(B, 6, D)

    qkv, cum, sga, gb = _input_projection(x, mod, norm1_g[l], w_in[l], b_forget[l], ln_v_g[l],
                                          w_spatial[l], b_spatial[l], w_branch_b[l])
    heads = lambda a: a.reshape(B, S, ATT_HEADS, ATT_HEAD_DIM).transpose(0, 2, 1, 3)
    q = heads(qkv[:, 0:ATT_WIDTH])
    k = heads(qkv[:, ATT_WIDTH:2 * ATT_WIDTH])
    v = heads(qkv[:, 2 * ATT_WIDTH:])
    cum_h = cum[:, 0:ATT_HEADS].reshape(B, S, ATT_HEADS).transpose(0, 2, 1)
    y_a = _fox_attention(q, k, v, cum_h).transpose(0, 2, 1, 3).reshape(n, ATT_WIDTH)

    x1, h2, idx_t, gates_t = _post_attention(
        y_a, sga, gb, x.reshape(n, D), mod, norm2_g[l].reshape(1, D),
        w_branch_a[l].astype(bf16), w_out[l].astype(bf16), w_query[l].astype(bf16),
        sub_keys[l], S)
    table = _pack_expert_table(expert_u[l], expert_v[l])
    out = _peer_experts_final(idx_t.T, gates_t.T, h2, x1, mod[:, 5:6, :], final_g, table, S)
    return out.reshape(B, S, D)
```

```python
import functools

import jax
import jax.numpy as jnp
from jax import lax
from jax.experimental import pallas as pl
from jax.experimental.pallas import tpu as pltpu
from jax.experimental.pallas import tpu_sc as plsc

D_MODEL = 1024
ATT_HEADS = 8
ATT_HEAD_DIM = 64
ATT_WIDTH = ATT_HEADS * ATT_HEAD_DIM
Q_BLOCK = 128
GM_GROUPS = 4
GM_GROUP_DIM = 128
GM_WIDTH = GM_GROUPS * GM_GROUP_DIM
GM_CHUNK = 128
PEER_HEADS = 8
PEER_KEY_DIM = 256
PEER_HALF = PEER_KEY_DIM // 2
N_KEYS = 128
PEER_TOPK = 16
PEER_TOK_BLOCK = 128
PEER_SLOTS = PEER_HEADS * PEER_TOPK
SPLIT_POINTS = (ATT_WIDTH, 2 * ATT_WIDTH, 3 * ATT_WIDTH, 3 * ATT_WIDTH + ATT_HEADS,
                3 * ATT_WIDTH + ATT_HEADS + 2 * GM_WIDTH,
                3 * ATT_WIDTH + ATT_HEADS + 2 * GM_WIDTH + D_MODEL)
EPS = 1e-6

LANES = 128
EXPERT_TOK_BLOCK = 128
EXPERT_SLOTS = 4
DMA_THREADS = 2
STAGED_TOK_BLOCK = 8
SC_GATHER_ROWS = 32
SC_SHARE_PERCENT = 40
SC_SHARE_ALIGN = 1024


def _gelu(x):
    return 0.5 * x * (1.0 + jnp.tanh(0.7978845608028654 * (x + 0.044715 * (x * x * x))))


def _rmsnorm(x, g):
    y = x * lax.rsqrt(jnp.mean(x * x, axis=-1, keepdims=True) + EPS)
    return y * g


def _layernorm(x, g):
    mu = jnp.mean(x, axis=-1, keepdims=True)
    var = jnp.mean(jnp.square(x - mu), axis=-1, keepdims=True)
    return (x - mu) * lax.rsqrt(var + EPS) * g


def _modulate(h, shift, scale):
    return h * (1.0 + scale[:, None, :]) + shift[:, None, :]


def _fox(q, k, v, log_f):
    B, S, H, Dh = q.shape
    nb = S // Q_BLOCK
    cum = jnp.cumsum(log_f, axis=1)
    cum_k = jnp.transpose(cum, (0, 2, 1))
    k_pos = jnp.arange(S)
    q_blocks = q.reshape(B, nb, Q_BLOCK, H, Dh).transpose(1, 0, 2, 3, 4)
    cq_blocks = cum_k.reshape(B, H, nb, Q_BLOCK).transpose(2, 0, 1, 3)
    starts = jnp.arange(nb) * Q_BLOCK
    scale = ATT_HEAD_DIM ** -0.5

    def one_block(args):
        q_blk, cq_blk, start = args
        s = jnp.einsum('bqhd,bkhd->bhqk', q_blk, k) * scale
        s = s + cq_blk[..., :, None] - cum_k[..., None, :]
        q_pos = start + jnp.arange(Q_BLOCK)
        causal = q_pos[:, None] >= k_pos[None, :]
        s = jnp.where(causal[None, None], s, -jnp.inf)
        p = jax.nn.softmax(s, axis=-1)
        return jnp.einsum('bhqk,bkhd->bqhd', p, v)

    out = lax.map(one_block, (q_blocks, cq_blocks, starts))
    return out.transpose(1, 0, 2, 3, 4).reshape(B, S, H * Dh)


def _gmlp(z, ln_g, w_s, b_s):
    B, S, _ = z.shape
    u, v = jnp.split(z, 2, axis=-1)
    v = _layernorm(v, ln_g)
    nc = S // GM_CHUNK
    v = v.reshape(B, nc, GM_CHUNK, GM_GROUPS, GM_GROUP_DIM)
    tril = jnp.tril(jnp.ones((GM_CHUNK, GM_CHUNK), dtype=bool))
    w = jnp.where(tril[None], w_s, jnp.zeros_like(w_s))
    mixed = jnp.einsum('gts,bcsgd->bctgd', w, v) + b_s.T[None, None, :, :, None]
    return u * mixed.reshape(B, S, GM_WIDTH)


def _peer_route(hf, w_query, sub_keys):
    n = hf.shape[0]
    q = (hf @ w_query).reshape(n, PEER_HEADS, 2, PEER_HALF)
    scores = jnp.einsum('nhpd,hpkd->nhpk', q, sub_keys)
    s_top, i_top = lax.top_k(scores, PEER_TOPK)
    cand_s = (s_top[:, :, 0, :, None] + s_top[:, :, 1, None, :]).reshape(n, PEER_HEADS, PEER_TOPK * PEER_TOPK)
    cand_i = (i_top[:, :, 0, :, None] * N_KEYS + i_top[:, :, 1, None, :]).reshape(n, PEER_HEADS, PEER_TOPK * PEER_TOPK)
    best_s, best_pos = lax.top_k(cand_s, PEER_TOPK)
    idx = jnp.take_along_axis(cand_i, best_pos, axis=-1)
    gates = jax.nn.softmax(best_s, axis=-1)
    return idx.reshape(n, PEER_SLOTS), gates.reshape(n, PEER_SLOTS)


def _mod_kernel(c_ref, w_ref, b_ref, o_ref):
    c = c_ref[...]
    sc = c * jax.nn.sigmoid(c)
    o_ref[...] = jnp.dot(sc, w_ref[...], precision=lax.Precision.HIGHEST,
                         preferred_element_type=jnp.float32) + b_ref[...]


def _modulation(c, w_mod, b_mod):
    b, d = c.shape
    cols = w_mod.shape[1]
    return pl.pallas_call(
        _mod_kernel,
        grid=(cols // d,),
        in_specs=[pl.BlockSpec((b, d), lambda j: (0, 0)),
                  pl.BlockSpec((d, d), lambda j: (0, j)),
                  pl.BlockSpec((1, d), lambda j: (0, j))],
        out_specs=pl.BlockSpec((b, d), lambda j: (0, j)),
        out_shape=jax.ShapeDtypeStruct((b, cols), jnp.float32),
        name="modulation",
    )(c, w_mod, b_mod.reshape(1, cols))


INPROJ_TOK_BLOCK = 256


def _inproj_kernel(x_ref, mod_ref, n1g_ref, wqkv_ref, wf_ref, bf_ref, wz_ref, wg_ref, lng_ref,
                   wsp_ref, bsp_ref, wb_ref, qkv_ref, cum_ref, sga_ref, gb_ref, carry):
    f32, bf16 = jnp.float32, jnp.bfloat16
    tm, d = x_ref.shape[1], x_ref.shape[2]
    x = x_ref[0]
    sh1 = mod_ref[0, 0:1, :]
    sc1 = mod_ref[0, 1:2, :]
    h = x * lax.rsqrt(jnp.mean(x * x, axis=-1, keepdims=True) + EPS) * n1g_ref[...]
    hb = (h * (1.0 + sc1) + sh1).astype(bf16)

    qkv = jnp.dot(hb, wqkv_ref[...], preferred_element_type=f32)
    qkv_ref[:, 0:ATT_WIDTH] = (qkv[:, 0:ATT_WIDTH] * (ATT_HEAD_DIM ** -0.5)).astype(bf16)
    qkv_ref[:, ATT_WIDTH:] = qkv[:, ATT_WIDTH:].astype(bf16)

    f = jnp.dot(hb, wf_ref[...], preferred_element_type=f32) + bf_ref[...]
    logf = jnp.minimum(f, 0.0) - jnp.log1p(jnp.exp(-jnp.abs(f)))

    @pl.when(pl.program_id(1) == 0)
    def _():
        carry[...] = jnp.zeros_like(carry)

    tri = (lax.broadcasted_iota(jnp.int32, (tm, tm), 0)
           >= lax.broadcasted_iota(jnp.int32, (tm, tm), 1)).astype(f32)
    cum = jnp.dot(tri, logf, precision=lax.Precision.HIGHEST, preferred_element_type=f32) + carry[...]
    cum_ref[...] = cum
    carry[...] = cum[tm - 1:tm, :]

    gz = _gelu(jnp.dot(hb, wz_ref[...], preferred_element_type=f32))
    u = gz[:, 0:GM_WIDTH]
    v = gz[:, GM_WIDTH:]
    mu = jnp.mean(v, axis=-1, keepdims=True)
    var = jnp.mean(jnp.square(v - mu), axis=-1, keepdims=True)
    vn = ((v - mu) * lax.rsqrt(var + EPS) * lng_ref[...]).astype(bf16)
    tril = (lax.broadcasted_iota(jnp.int32, (GM_CHUNK, GM_CHUNK), 0)
            >= lax.broadcasted_iota(jnp.int32, (GM_CHUNK, GM_CHUNK), 1))
    w_sp = [jnp.where(tril, wsp_ref[g], 0.0).astype(bf16) for g in range(GM_GROUPS)]
    rows = []
    for ck in range(tm // GM_CHUNK):
        r0 = ck * GM_CHUNK
        cols = []
        for g in range(GM_GROUPS):
            c0 = g * GM_GROUP_DIM
            mixed = jnp.dot(w_sp[g], vn[r0:r0 + GM_CHUNK, c0:c0 + GM_GROUP_DIM],
                            preferred_element_type=f32) + bsp_ref[g]
            cols.append(u[r0:r0 + GM_CHUNK, c0:c0 + GM_GROUP_DIM] * mixed)
        rows.append(jnp.concatenate(cols, axis=1))
    yb = jnp.concatenate(rows, axis=0).astype(bf16)
    ybp = jnp.dot(yb, wb_ref[...], preferred_element_type=f32)

    sg = jax.nn.sigmoid(jnp.dot(hb, wg_ref[...], preferred_element_type=f32))
    sga_ref[...] = sg[:, 0:d].astype(bf16)
    gb_ref[...] = (sg[:, d:] * ybp).astype(bf16)


def _input_projection(x, mod, norm1_g, w_in, b_forget, ln_v_g, w_spatial, b_spatial, w_branch_b):
    B, S, d = x.shape
    n = B * S
    tm = INPROJ_TOK_BLOCK
    bf16 = jnp.bfloat16
    p0, p1, p2, p3, p4, p5 = SPLIT_POINTS
    w_qkv = w_in[:, 0:p2].astype(bf16)
    w_f = jnp.pad(w_in[:, p2:p3], ((0, 0), (0, LANES - ATT_HEADS))).astype(bf16)
    b_f = jnp.pad(b_forget, (0, LANES - ATT_HEADS)).reshape(1, LANES)
    w_z = w_in[:, p3:p4].astype(bf16)
    w_g = w_in[:, p4:].astype(bf16)
    nt = S // tm
    tok = lambda w: pl.BlockSpec((tm, w), lambda b, i: (b * nt + i, 0))
    full = lambda a: pl.BlockSpec(a.shape, lambda b, i: (0,) * a.ndim)
    args = (x, mod, norm1_g.reshape(1, d), w_qkv, w_f, b_f, w_z, w_g, ln_v_g.reshape(1, GM_WIDTH),
            w_spatial, b_spatial.reshape(GM_GROUPS, GM_CHUNK, 1), w_branch_b.astype(bf16))
    return pl.pallas_call(
        _inproj_kernel,
        grid=(B, nt),
        in_specs=[pl.BlockSpec((1, tm, d), lambda b, i: (b, i, 0)),
                  pl.BlockSpec((1, 6, d), lambda b, i: (b, 0, 0))] + [full(a) for a in args[2:]],
        out_specs=[tok(3 * ATT_WIDTH), tok(LANES), tok(d), tok(d)],
        out_shape=[jax.ShapeDtypeStruct((n, 3 * ATT_WIDTH), bf16),
                   jax.ShapeDtypeStruct((n, LANES), jnp.float32),
                   jax.ShapeDtypeStruct((n, d), bf16),
                   jax.ShapeDtypeStruct((n, d), bf16)],
        scratch_shapes=[pltpu.VMEM((1, LANES), jnp.float32)],
        compiler_params=pltpu.CompilerParams(dimension_semantics=("arbitrary", "arbitrary"),
                                             vmem_limit_bytes=56 * 1024 * 1024),
        name="input_projection",
    )(*args)


ATT_BLOCK = 512


def _fox_kernel(q_ref, k_ref, v_ref, cq_ref, ck_ref, o_ref, *, blk):
    f32 = jnp.float32
    i = pl.program_id(2)
    q = q_ref[0, 0]
    cq = cq_ref[0, 0]

    def block(j, carry, masked):
        m, l, acc = carry
        off = pl.multiple_of(j * blk, blk)
        k = k_ref[0, 0, pl.ds(off, blk), :]
        v = v_ref[0, 0, pl.ds(off, blk), :]
        s = lax.dot_general(q, k, (((1,), (1,)), ((), ())), preferred_element_type=f32)
        s = s + (cq - ck_ref[0, 0, j])
        if masked:
            causal = (lax.broadcasted_iota(jnp.int32, (blk, blk), 0)
                      >= lax.broadcasted_iota(jnp.int32, (blk, blk), 1))
            s = jnp.where(causal, s, -jnp.inf)
        m_new = jnp.maximum(m, jnp.max(s, axis=1, keepdims=True))
        alpha = jnp.exp(m - m_new)
        p = jnp.exp(s - m_new)
        l = alpha * l + jnp.sum(p, axis=1, keepdims=True)
        acc = alpha * acc + jnp.dot(p.astype(v.dtype), v, preferred_element_type=f32)
        return m_new, l, acc

    init = (jnp.full((blk, 1), -1e30, f32), jnp.zeros((blk, 1), f32),
            jnp.zeros((blk, q.shape[1]), f32))
    carry = lax.fori_loop(0, i, lambda j, c: block(j, c, False), init)
    m, l, acc = block(i, carry, True)
    o_ref[0, 0] = (acc / l).astype(o_ref.dtype)


def _fox_attention(q, k, v, cum):
    B, H, S, dh = q.shape
    blk = min(ATT_BLOCK, S)
    nb = S // blk
    cq = cum.reshape(B, H, S, 1)
    ck = cum.reshape(B, H, nb, 1, blk)
    return pl.pallas_call(
        functools.partial(_fox_kernel, blk=blk),
        grid=(B, H, nb),
        in_specs=[pl.BlockSpec((1, 1, blk, dh), lambda b, h, i: (b, h, i, 0)),
                  pl.BlockSpec((1, 1, S, dh), lambda b, h, i: (b, h, 0, 0)),
                  pl.BlockSpec((1, 1, S, dh), lambda b, h, i: (b, h, 0, 0)),
                  pl.BlockSpec((1, 1, blk, 1), lambda b, h, i: (b, h, i, 0)),
                  pl.BlockSpec((1, 1, nb, 1, blk), lambda b, h, i: (b, h, 0, 0, 0))],
        out_specs=pl.BlockSpec((1, 1, blk, dh), lambda b, h, i: (b, h, i, 0)),
        out_shape=jax.ShapeDtypeStruct((B, H, S, dh), jnp.bfloat16),
        compiler_params=pltpu.CompilerParams(
            dimension_semantics=("arbitrary", "arbitrary", "arbitrary")),
        name="fox_attention",
    )(q, k, v, cq, ck)


def _topk_rows(s, k):
    rows, t = s.shape
    iota = lax.broadcasted_iota(jnp.int32, (rows, t), 0)
    slot = lax.broadcasted_iota(jnp.int32, (k, t), 0)
    vals = jnp.zeros((k, t), jnp.float32)
    ids = jnp.zeros((k, t), jnp.int32)
    for j in range(k):
        m = jnp.max(s, axis=0, keepdims=True)
        am = jnp.min(jnp.where(s == m, iota, rows), axis=0, keepdims=True)
        vals = jnp.where(slot == j, m, vals)
        ids = jnp.where(slot == j, am, ids)
        s = jnp.where(iota == am, -jnp.inf, s)
    return vals, ids


def _select_rows(table, pos):
    out = jnp.zeros(pos.shape, table.dtype)
    for r in range(table.shape[0]):
        out = jnp.where(pos == r, table[r:r + 1, :], out)
    return out


def _post_kernel(ya_ref, sga_ref, gb_ref, x_ref, mod_ref, n2g_ref, wa_ref, wo_ref, wq_ref, keys_ref,
                 x1_ref, h2_ref, idx_ref, gates_ref, q_scr):
    f32 = jnp.float32
    a = jnp.dot(ya_ref[...], wa_ref[...], preferred_element_type=f32)
    merged = sga_ref[...].astype(f32) * a + gb_ref[...].astype(f32)
    o = jnp.dot(merged.astype(jnp.bfloat16), wo_ref[...], preferred_element_type=f32)
    g1 = mod_ref[0, 2:3, :]
    sh2 = mod_ref[0, 3:4, :]
    sc2 = mod_ref[0, 4:5, :]
    x1 = x_ref[...] + g1 * o
    x1_ref[...] = x1
    h2 = x1 * lax.rsqrt(jnp.mean(x1 * x1, axis=-1, keepdims=True) + EPS) * n2g_ref[...]
    h2 = h2 * (1.0 + sc2) + sh2
    h2_ref[...] = h2
    qp = jnp.dot(h2.astype(jnp.bfloat16), wq_ref[...], preferred_element_type=f32)
    for j in range(2 * PEER_HEADS):
        q_scr[j] = qp[:, j * PEER_HALF:(j + 1) * PEER_HALF]

    def head(h, carry):
        tops = []
        for p in range(2):
            q = q_scr[2 * h + p]
            keys = keys_ref[2 * h + p]
            sc = lax.dot_general(keys, q, (((1,), (1,)), ((), ())),
                                 precision=lax.Precision.HIGHEST,
                                 preferred_element_type=f32)
            tops.append(_topk_rows(sc, PEER_TOPK))
        (s1, i1), (s2, i2) = tops
        half = PEER_TOPK // 2
        blocks = [s1[0:1, :] + s2]
        blocks += [s1[a:a + 1, :] + s2[0:half, :] for a in range(1, half)]
        blocks += [s1[half:, :] + s2[0:1, :]]
        vals, pos = _topk_rows(jnp.concatenate(blocks, axis=0), PEER_TOPK)
        mid = pos - PEER_TOPK
        tail0 = PEER_TOPK + (half - 1) * half
        ra = jnp.where(pos < PEER_TOPK, 0,
                       jnp.where(pos < tail0, 1 + (mid >> (half.bit_length() - 1)), pos - tail0 + half))
        rb = jnp.where(pos < PEER_TOPK, pos, jnp.where(pos < tail0, mid & (half - 1), 0))
        eid = _select_rows(i1, ra) * N_KEYS + _select_rows(i2, rb)
        e = jnp.exp(vals - vals[0:1, :])
        g = e / jnp.sum(e, axis=0, keepdims=True)
        row = pl.multiple_of(h * PEER_TOPK, PEER_TOPK)
        idx_ref[pl.ds(row, PEER_TOPK), :] = eid
        gates_ref[pl.ds(row, PEER_TOPK), :] = g
        return carry

    lax.fori_loop(0, PEER_HEADS, head, 0)


POST_TOK_BLOCK = 256


def _post_attention(ya, sga, gb, x, mod, norm2_g, w_a, w_out, w_query, sub_keys, seq_len):
    n, d = x.shape
    tm = POST_TOK_BLOCK
    blocks_per_seq = seq_len // tm
    aw = ya.shape[1]
    qw = w_query.shape[1]
    keys = sub_keys.reshape(2 * PEER_HEADS, N_KEYS, PEER_HALF)
    tok = lambda w: pl.BlockSpec((tm, w), lambda i: (i, 0))
    full = lambda a: pl.BlockSpec(a.shape, lambda i: (0,) * a.ndim)
    return pl.pallas_call(
        _post_kernel,
        grid=(n // tm,),
        in_specs=[tok(aw), tok(d), tok(d), tok(d),
                  pl.BlockSpec((1, 6, d), lambda i: (i // blocks_per_seq, 0, 0)),
                  full(norm2_g), full(w_a), full(w_out), full(w_query), full(keys)],
        out_specs=[tok(d), tok(d),
                   pl.BlockSpec((PEER_SLOTS, tm), lambda i: (0, i)),
                   pl.BlockSpec((PEER_SLOTS, tm), lambda i: (0, i))],
        out_shape=[jax.ShapeDtypeStruct((n, d), jnp.float32),
                   jax.ShapeDtypeStruct((n, d), jnp.float32),
                   jax.ShapeDtypeStruct((PEER_SLOTS, n), jnp.int32),
                   jax.ShapeDtypeStruct((PEER_SLOTS, n), jnp.float32)],
        scratch_shapes=[pltpu.VMEM((2 * PEER_HEADS, tm, PEER_HALF), jnp.float32)],
        compiler_params=pltpu.CompilerParams(dimension_semantics=("arbitrary",),
                                             vmem_limit_bytes=48 * 1024 * 1024),
        name="post_attention",
    )(ya, sga, gb, x, mod, norm2_g, w_a, w_out, w_query, keys)


def _rowsum_bcast(p, ones_bf16):
    hi = p.astype(jnp.bfloat16)
    lo = (p - hi.astype(jnp.float32)).astype(jnp.bfloat16)
    return (jnp.dot(hi, ones_bf16, preferred_element_type=jnp.float32)
            + jnp.dot(lo, ones_bf16, preferred_element_type=jnp.float32))


def _eval_experts(chunk, hrow, grow):
    n_chunks = hrow.shape[1] // LANES
    ones_bf16 = jnp.ones((LANES, LANES), jnp.bfloat16)
    eye = (lax.broadcasted_iota(jnp.int32, (PEER_SLOTS, LANES), 0)
           == lax.broadcasted_iota(jnp.int32, (PEER_SLOTS, LANES), 1))
    psum = None
    for c in range(n_chunks):
        u = lax.bitcast_convert_type(chunk(c) & jnp.uint32(0xFFFF0000), jnp.float32)
        p = u * hrow[:, c * LANES:(c + 1) * LANES]
        psum = p if psum is None else psum + p
    act = _gelu(_rowsum_bcast(psum, ones_bf16))
    gcol = _rowsum_bcast(jnp.where(eye, grow, 0.0), ones_bf16)
    coef = gcol * act
    outs = []
    for c in range(n_chunks):
        v = lax.bitcast_convert_type(chunk(c) << 16, jnp.float32)
        outs.append(jnp.sum(coef * v, axis=0, keepdims=True))
    return jnp.concatenate(outs, axis=-1)


def _finish_block(x_ref, g2_ref, fg_ref, peer, out_ref):
    y = x_ref[...] + g2_ref[0] * peer[...]
    out_ref[...] = y * lax.rsqrt(jnp.mean(y * y, axis=-1, keepdims=True) + EPS) * fg_ref[...]


def _expert_kernel(idx_ref, gates_ref, h_ref, x_ref, g2_ref, fg_ref, tab_ref, out_ref,
                   *scratch, tok_block, n_slots):
    bufs = scratch[:n_slots]
    peer, sem = scratch[n_slots], scratch[n_slots + 1]
    d_model = h_ref.shape[-1]
    n_chunks = d_model // LANES
    rows_per_tok = PEER_SLOTS * n_chunks

    def issue(t, s):
        for r in range(PEER_SLOTS):
            row = pl.multiple_of(idx_ref[t, r] * n_chunks, n_chunks)
            pltpu.make_async_copy(tab_ref.at[pl.ds(row, n_chunks), :],
                                  bufs[s].at[pl.ds(r * n_chunks, n_chunks), :],
                                  sem.at[s]).start(priority=r % DMA_THREADS)

    def wait(s):
        pltpu.make_async_copy(tab_ref.at[pl.ds(0, rows_per_tok), :], bufs[s], sem.at[s]).wait()

    def compute(t, s):
        chunk = lambda c: bufs[s][pl.ds(c, PEER_SLOTS, stride=n_chunks), :]
        peer[pl.ds(t, 1), :] = _eval_experts(chunk, h_ref[pl.ds(t, 1), :], gates_ref[pl.ds(t, 1), :])

    def step(t, s, prefetch):
        wait(s)
        if prefetch:
            issue(t + n_slots - 1, (s - 1) % n_slots)
        compute(t, s)

    for t in range(n_slots - 1):
        issue(t, t)

    n_groups = tok_block // n_slots

    def group(g, carry):
        for s in range(n_slots):
            step(g * n_slots + s, s, True)
        return carry

    lax.fori_loop(0, n_groups - 1, group, 0)
    for s in range(n_slots):
        t = (n_groups - 1) * n_slots + s
        step(t, s, t + n_slots - 1 < tok_block)

    _finish_block(x_ref, g2_ref, fg_ref, peer, out_ref)


def _staged_expert_kernel(rows_ref, gates_ref, h_ref, x_ref, g2_ref, fg_ref, prev_ref, out_ref, peer,
                          *, tok_block):
    del prev_ref
    n_chunks = h_ref.shape[-1] // LANES
    rows_per_tok = PEER_SLOTS * n_chunks
    for t in range(tok_block):
        chunk = lambda c: rows_ref[pl.ds(t * rows_per_tok + c, PEER_SLOTS, stride=n_chunks), :]
        peer[pl.ds(t, 1), :] = _eval_experts(chunk, h_ref[pl.ds(t, 1), :], gates_ref[pl.ds(t, 1), :])
    _finish_block(x_ref, g2_ref, fg_ref, peer, out_ref)


def _sc_gather_experts(table3, ids):
    m = ids.shape[0]
    info = plsc.get_sparse_core_info()
    n_workers = info.num_cores * info.num_subcores
    per_worker = m // n_workers
    assert per_worker * n_workers == m and per_worker % SC_GATHER_ROWS == 0
    mesh = plsc.VectorSubcoreMesh(core_axis_name="c", subcore_axis_name="s")

    @functools.partial(
        pl.kernel, mesh=mesh,
        out_type=jax.ShapeDtypeStruct((m,) + table3.shape[1:], table3.dtype),
        scratch_types=[pltpu.VMEM((SC_GATHER_ROWS,), jnp.int32),
                       pltpu.VMEM((SC_GATHER_ROWS,) + table3.shape[1:], table3.dtype),
                       pltpu.SemaphoreType.DMA],
        name="sc_gather_experts")
    def gather(table_hbm, ids_hbm, out_hbm, ids_v, rows_v, sem):
        wid = lax.axis_index("s") * info.num_cores + lax.axis_index("c")
        base = wid * per_worker

        @pl.loop(0, per_worker // SC_GATHER_ROWS)
        def _(i):
            off = pl.multiple_of(base + i * SC_GATHER_ROWS, SC_GATHER_ROWS)
            pltpu.sync_copy(ids_hbm.at[pl.ds(off, SC_GATHER_ROWS)], ids_v)
            pltpu.async_copy(table_hbm.at[ids_v], rows_v, sem).wait()
            pltpu.sync_copy(rows_v, out_hbm.at[pl.ds(off, SC_GATHER_ROWS)])

    return gather(table3, ids)


def _pack_expert_table(expert_u, expert_v):
    ub = lax.bitcast_convert_type(expert_u.astype(jnp.bfloat16), jnp.uint16).astype(jnp.uint32)
    vb = lax.bitcast_convert_type(expert_v.astype(jnp.bfloat16), jnp.uint16).astype(jnp.uint32)
    packed = (ub << 16) | vb
    n_experts, d = packed.shape
    return packed.reshape(n_experts * (d // LANES), LANES)


def _peer_experts_final(idx, gates, h2, x1, g2, final_g, table, seq_len, n_sc):
    n, d = h2.shape
    n_chunks = d // LANES
    tb, ts = EXPERT_TOK_BLOCK, STAGED_TOK_BLOCK
    assert tb % EXPERT_SLOTS == 0 and seq_len % tb == 0 and seq_len % ts == 0
    assert n_sc % tb == 0 and n_sc % ts == 0 and 0 <= n_sc < n
    first = n_sc // tb
    fg = final_g.reshape(1, d)
    out_shape = jax.ShapeDtypeStruct((n, d), jnp.float32)

    out = pl.pallas_call(
        functools.partial(_expert_kernel, tok_block=tb, n_slots=EXPERT_SLOTS),
        grid=((n - n_sc) // tb,),
        in_specs=[
            pl.BlockSpec((tb, PEER_SLOTS), lambda i: (i + first, 0), memory_space=pltpu.SMEM),
            pl.BlockSpec((tb, PEER_SLOTS), lambda i: (i + first, 0)),
            pl.BlockSpec((tb, d), lambda i: (i + first, 0)),
            pl.BlockSpec((tb, d), lambda i: (i + first, 0)),
            pl.BlockSpec((1, 1, d), lambda i: ((i + first) * tb // seq_len, 0, 0)),
            pl.BlockSpec((1, d), lambda i: (0, 0)),
            pl.BlockSpec(memory_space=pl.ANY),
        ],
        out_specs=pl.BlockSpec((tb, d), lambda i: (i + first, 0)),
        out_shape=out_shape,
        scratch_shapes=(
            [pltpu.VMEM((PEER_SLOTS * n_chunks, LANES), jnp.uint32) for _ in range(EXPERT_SLOTS)]
            + [pltpu.VMEM((tb, d), jnp.float32), pltpu.SemaphoreType.DMA((EXPERT_SLOTS,))]),
        compiler_params=pltpu.CompilerParams(dimension_semantics=("arbitrary",)),
        name="peer_experts",
    )(idx, gates, h2, x1, g2, fg, table)
    if n_sc == 0:
        return out

    rows = _sc_gather_experts(table.reshape(-1, n_chunks, LANES), idx[:n_sc].reshape(-1))
    return _staged_experts(rows.reshape(-1, LANES), gates, h2, x1, g2, fg, out, seq_len, n_sc)


def _staged_experts(rows, gates, h2, x1, g2, fg, prev, seq_len, n_sc):
    n, d = h2.shape
    ts = STAGED_TOK_BLOCK
    rows_per_tok = PEER_SLOTS * (d // LANES)
    return pl.pallas_call(
        functools.partial(_staged_expert_kernel, tok_block=ts),
        grid=(n_sc // ts,),
        in_specs=[
            pl.BlockSpec((ts * rows_per_tok, LANES), lambda i: (i, 0)),
            pl.BlockSpec((ts, PEER_SLOTS), lambda i: (i, 0)),
            pl.BlockSpec((ts, d), lambda i: (i, 0)),
            pl.BlockSpec((ts, d), lambda i: (i, 0)),
            pl.BlockSpec((1, 1, d), lambda i: (i * ts // seq_len, 0, 0)),
            pl.BlockSpec((1, d), lambda i: (0, 0)),
            pl.BlockSpec(memory_space=pl.ANY),
        ],
        out_specs=pl.BlockSpec((ts, d), lambda i: (i, 0)),
        out_shape=jax.ShapeDtypeStruct((n, d), jnp.float32),
        scratch_shapes=[pltpu.VMEM((ts, d), jnp.float32)],
        input_output_aliases={6: 0},
        compiler_params=pltpu.CompilerParams(dimension_semantics=("arbitrary",)),
        name="staged_experts",
    )(rows, gates, h2, x1, g2, fg, prev)


def kernel(x, c, w_mod, b_mod, norm1_g, w_in, b_forget, ln_v_g, w_spatial, b_spatial, w_branch_a, w_branch_b, w_out, norm2_g, w_query, sub_keys, expert_u, expert_v, final_g):
    B, S, D = x.shape
    n = B * S
    bf16 = jnp.bfloat16
    assert w_mod.shape[0] == 1, "the final RMSNorm is fused into the single layer's expert kernel"
    l = 0
    mod = _modulation(c, w_mod[l], b_mod[l]).reshape(B, 6, D)

    qkv, cum, sga, gb = _input_projection(x, mod, norm1_g[l], w_in[l], b_forget[l], ln_v_g[l],
                                          w_spatial[l], b_spatial[l], w_branch_b[l])
    heads = lambda a: a.reshape(B, S, ATT_HEADS, ATT_HEAD_DIM).transpose(0, 2, 1, 3)
    q = heads(qkv[:, 0:ATT_WIDTH])
    k = heads(qkv[:, ATT_WIDTH:2 * ATT_WIDTH])
    v = heads(qkv[:, 2 * ATT_WIDTH:])
    cum_h = cum[:, 0:ATT_HEADS].reshape(B, S, ATT_HEADS).transpose(0, 2, 1)
    y_a = _fox_attention(q, k, v, cum_h).transpose(0, 2, 1, 3).reshape(n, ATT_WIDTH)

    x1, h2, idx_t, gates_t = _post_attention(
        y_a, sga, gb, x.reshape(n, D), mod, norm2_g[l].reshape(1, D),
        w_branch_a[l].astype(bf16), w_out[l].astype(bf16), w_query[l].astype(bf16),
        sub_keys[l], S)
    table = _pack_expert_table(expert_u[l], expert_v[l])
    n_sc = n * SC_SHARE_PERCENT // 100 // SC_SHARE_ALIGN * SC_SHARE_ALIGN
    out = _peer_experts_final(idx_t.T, gates_t.T, h2, x1, mod[:, 5:6, :], final_g, table, S, n_sc)
    return out.reshape(B, S, D)
```

```python
import functools

import jax
import jax.numpy as jnp
from jax import lax
from jax.experimental import pallas as pl
from jax.experimental.pallas import tpu as pltpu
from jax.experimental.pallas import tpu_sc as plsc

D_MODEL = 1024
ATT_HEADS = 8
ATT_HEAD_DIM = 64
ATT_WIDTH = ATT_HEADS * ATT_HEAD_DIM
Q_BLOCK = 128
GM_GROUPS = 4
GM_GROUP_DIM = 128
GM_WIDTH = GM_GROUPS * GM_GROUP_DIM
GM_CHUNK = 128
PEER_HEADS = 8
PEER_KEY_DIM = 256
PEER_HALF = PEER_KEY_DIM // 2
N_KEYS = 128
PEER_TOPK = 16
PEER_TOK_BLOCK = 128
PEER_SLOTS = PEER_HEADS * PEER_TOPK
SPLIT_POINTS = (ATT_WIDTH, 2 * ATT_WIDTH, 3 * ATT_WIDTH, 3 * ATT_WIDTH + ATT_HEADS,
                3 * ATT_WIDTH + ATT_HEADS + 2 * GM_WIDTH,
                3 * ATT_WIDTH + ATT_HEADS + 2 * GM_WIDTH + D_MODEL)
EPS = 1e-6

LANES = 128
EXPERT_TOK_BLOCK = 128
EXPERT_SLOTS = 4
DMA_THREADS = 2
FINISH_TOK_BLOCK = 512
SC_GATHER_ROWS = 32
SC_SHARE_PERCENT = 40
SC_SHARE_ALIGN = 1024


def _gelu(x):
    return 0.5 * x * (1.0 + jnp.tanh(0.7978845608028654 * (x + 0.044715 * (x * x * x))))


def _rmsnorm(x, g):
    y = x * lax.rsqrt(jnp.mean(x * x, axis=-1, keepdims=True) + EPS)
    return y * g


def _layernorm(x, g):
    mu = jnp.mean(x, axis=-1, keepdims=True)
    var = jnp.mean(jnp.square(x - mu), axis=-1, keepdims=True)
    return (x - mu) * lax.rsqrt(var + EPS) * g


def _modulate(h, shift, scale):
    return h * (1.0 + scale[:, None, :]) + shift[:, None, :]


def _fox(q, k, v, log_f):
    B, S, H, Dh = q.shape
    nb = S // Q_BLOCK
    cum = jnp.cumsum(log_f, axis=1)
    cum_k = jnp.transpose(cum, (0, 2, 1))
    k_pos = jnp.arange(S)
    q_blocks = q.reshape(B, nb, Q_BLOCK, H, Dh).transpose(1, 0, 2, 3, 4)
    cq_blocks = cum_k.reshape(B, H, nb, Q_BLOCK).transpose(2, 0, 1, 3)
    starts = jnp.arange(nb) * Q_BLOCK
    scale = ATT_HEAD_DIM ** -0.5

    def one_block(args):
        q_blk, cq_blk, start = args
        s = jnp.einsum('bqhd,bkhd->bhqk', q_blk, k) * scale
        s = s + cq_blk[..., :, None] - cum_k[..., None, :]
        q_pos = start + jnp.arange(Q_BLOCK)
        causal = q_pos[:, None] >= k_pos[None, :]
        s = jnp.where(causal[None, None], s, -jnp.inf)
        p = jax.nn.softmax(s, axis=-1)
        return jnp.einsum('bhqk,bkhd->bqhd', p, v)

    out = lax.map(one_block, (q_blocks, cq_blocks, starts))
    return out.transpose(1, 0, 2, 3, 4).reshape(B, S, H * Dh)


def _gmlp(z, ln_g, w_s, b_s):
    B, S, _ = z.shape
    u, v = jnp.split(z, 2, axis=-1)
    v = _layernorm(v, ln_g)
    nc = S // GM_CHUNK
    v = v.reshape(B, nc, GM_CHUNK, GM_GROUPS, GM_GROUP_DIM)
    tril = jnp.tril(jnp.ones((GM_CHUNK, GM_CHUNK), dtype=bool))
    w = jnp.where(tril[None], w_s, jnp.zeros_like(w_s))
    mixed = jnp.einsum('gts,bcsgd->bctgd', w, v) + b_s.T[None, None, :, :, None]
    return u * mixed.reshape(B, S, GM_WIDTH)


def _peer_route(hf, w_query, sub_keys):
    n = hf.shape[0]
    q = (hf @ w_query).reshape(n, PEER_HEADS, 2, PEER_HALF)
    scores = jnp.einsum('nhpd,hpkd->nhpk', q, sub_keys)
    s_top, i_top = lax.top_k(scores, PEER_TOPK)
    cand_s = (s_top[:, :, 0, :, None] + s_top[:, :, 1, None, :]).reshape(n, PEER_HEADS, PEER_TOPK * PEER_TOPK)
    cand_i = (i_top[:, :, 0, :, None] * N_KEYS + i_top[:, :, 1, None, :]).reshape(n, PEER_HEADS, PEER_TOPK * PEER_TOPK)
    best_s, best_pos = lax.top_k(cand_s, PEER_TOPK)
    idx = jnp.take_along_axis(cand_i, best_pos, axis=-1)
    gates = jax.nn.softmax(best_s, axis=-1)
    return idx.reshape(n, PEER_SLOTS), gates.reshape(n, PEER_SLOTS)


def _mod_kernel(c_ref, w_ref, b_ref, o_ref):
    c = c_ref[...]
    sc = c * jax.nn.sigmoid(c)
    o_ref[...] = jnp.dot(sc, w_ref[...], precision=lax.Precision.HIGHEST,
                         preferred_element_type=jnp.float32) + b_ref[...]


def _modulation(c, w_mod, b_mod):
    b, d = c.shape
    cols = w_mod.shape[1]
    return pl.pallas_call(
        _mod_kernel,
        grid=(cols // d,),
        in_specs=[pl.BlockSpec((b, d), lambda j: (0, 0)),
                  pl.BlockSpec((d, d), lambda j: (0, j)),
                  pl.BlockSpec((1, d), lambda j: (0, j))],
        out_specs=pl.BlockSpec((b, d), lambda j: (0, j)),
        out_shape=jax.ShapeDtypeStruct((b, cols), jnp.float32),
        name="modulation",
    )(c, w_mod, b_mod.reshape(1, cols))


INPROJ_TOK_BLOCK = 256


def _inproj_kernel(x_ref, mod_ref, n1g_ref, wqkv_ref, wf_ref, bf_ref, wz_ref, wg_ref, lng_ref,
                   wsp_ref, bsp_ref, wb_ref, qkv_ref, cum_ref, sga_ref, gb_ref, carry):
    f32, bf16 = jnp.float32, jnp.bfloat16
    tm, d = x_ref.shape[1], x_ref.shape[2]
    x = x_ref[0]
    sh1 = mod_ref[0, 0:1, :]
    sc1 = mod_ref[0, 1:2, :]
    h = x * lax.rsqrt(jnp.mean(x * x, axis=-1, keepdims=True) + EPS) * n1g_ref[...]
    hb = (h * (1.0 + sc1) + sh1).astype(bf16)

    qkv = jnp.dot(hb, wqkv_ref[...], preferred_element_type=f32)
    qkv_ref[:, 0:ATT_WIDTH] = (qkv[:, 0:ATT_WIDTH] * (ATT_HEAD_DIM ** -0.5)).astype(bf16)
    qkv_ref[:, ATT_WIDTH:] = qkv[:, ATT_WIDTH:].astype(bf16)

    f = jnp.dot(hb, wf_ref[...], preferred_element_type=f32) + bf_ref[...]
    logf = jnp.minimum(f, 0.0) - jnp.log1p(jnp.exp(-jnp.abs(f)))

    @pl.when(pl.program_id(1) == 0)
    def _():
        carry[...] = jnp.zeros_like(carry)

    tri = (lax.broadcasted_iota(jnp.int32, (tm, tm), 0)
           >= lax.broadcasted_iota(jnp.int32, (tm, tm), 1)).astype(f32)
    cum = jnp.dot(tri, logf, precision=lax.Precision.HIGHEST, preferred_element_type=f32) + carry[...]
    cum_ref[...] = cum
    carry[...] = cum[tm - 1:tm, :]

    gz = _gelu(jnp.dot(hb, wz_ref[...], preferred_element_type=f32))
    u = gz[:, 0:GM_WIDTH]
    v = gz[:, GM_WIDTH:]
    mu = jnp.mean(v, axis=-1, keepdims=True)
    var = jnp.mean(jnp.square(v - mu), axis=-1, keepdims=True)
    vn = ((v - mu) * lax.rsqrt(var + EPS) * lng_ref[...]).astype(bf16)
    tril = (lax.broadcasted_iota(jnp.int32, (GM_CHUNK, GM_CHUNK), 0)
            >= lax.broadcasted_iota(jnp.int32, (GM_CHUNK, GM_CHUNK), 1))
    w_sp = [jnp.where(tril, wsp_ref[g], 0.0).astype(bf16) for g in range(GM_GROUPS)]
    rows = []
    for ck in range(tm // GM_CHUNK):
        r0 = ck * GM_CHUNK
        cols = []
        for g in range(GM_GROUPS):
            c0 = g * GM_GROUP_DIM
            mixed = jnp.dot(w_sp[g], vn[r0:r0 + GM_CHUNK, c0:c0 + GM_GROUP_DIM],
                            preferred_element_type=f32) + bsp_ref[g]
            cols.append(u[r0:r0 + GM_CHUNK, c0:c0 + GM_GROUP_DIM] * mixed)
        rows.append(jnp.concatenate(cols, axis=1))
    yb = jnp.concatenate(rows, axis=0).astype(bf16)
    ybp = jnp.dot(yb, wb_ref[...], preferred_element_type=f32)

    sg = jax.nn.sigmoid(jnp.dot(hb, wg_ref[...], preferred_element_type=f32))
    sga_ref[...] = sg[:, 0:d].astype(bf16)
    gb_ref[...] = (sg[:, d:] * ybp).astype(bf16)


def _input_projection(x, mod, norm1_g, w_in, b_forget, ln_v_g, w_spatial, b_spatial, w_branch_b):
    B, S, d = x.shape
    n = B * S
    tm = INPROJ_TOK_BLOCK
    bf16 = jnp.bfloat16
    p0, p1, p2, p3, p4, p5 = SPLIT_POINTS
    w_qkv = w_in[:, 0:p2].astype(bf16)
    w_f = jnp.pad(w_in[:, p2:p3], ((0, 0), (0, LANES - ATT_HEADS))).astype(bf16)
    b_f = jnp.pad(b_forget, (0, LANES - ATT_HEADS)).reshape(1, LANES)
    w_z = w_in[:, p3:p4].astype(bf16)
    w_g = w_in[:, p4:].astype(bf16)
    nt = S // tm
    tok = lambda w: pl.BlockSpec((tm, w), lambda b, i: (b * nt + i, 0))
    full = lambda a: pl.BlockSpec(a.shape, lambda b, i: (0,) * a.ndim)
    args = (x, mod, norm1_g.reshape(1, d), w_qkv, w_f, b_f, w_z, w_g, ln_v_g.reshape(1, GM_WIDTH),
            w_spatial, b_spatial.reshape(GM_GROUPS, GM_CHUNK, 1), w_branch_b.astype(bf16))
    return pl.pallas_call(
        _inproj_kernel,
        grid=(B, nt),
        in_specs=[pl.BlockSpec((1, tm, d), lambda b, i: (b, i, 0)),
                  pl.BlockSpec((1, 6, d), lambda b, i: (b, 0, 0))] + [full(a) for a in args[2:]],
        out_specs=[tok(3 * ATT_WIDTH), tok(LANES), tok(d), tok(d)],
        out_shape=[jax.ShapeDtypeStruct((n, 3 * ATT_WIDTH), bf16),
                   jax.ShapeDtypeStruct((n, LANES), jnp.float32),
                   jax.ShapeDtypeStruct((n, d), bf16),
                   jax.ShapeDtypeStruct((n, d), bf16)],
        scratch_shapes=[pltpu.VMEM((1, LANES), jnp.float32)],
        compiler_params=pltpu.CompilerParams(dimension_semantics=("arbitrary", "arbitrary"),
                                             vmem_limit_bytes=56 * 1024 * 1024),
        name="input_projection",
    )(*args)


ATT_BLOCK = 512


def _fox_kernel(q_ref, k_ref, v_ref, cq_ref, ck_ref, o_ref, *, blk):
    f32 = jnp.float32
    i = pl.program_id(2)
    q = q_ref[0, 0]
    cq = cq_ref[0, 0]

    def block(j, carry, masked):
        m, l, acc = carry
        off = pl.multiple_of(j * blk, blk)
        k = k_ref[0, 0, pl.ds(off, blk), :]
        v = v_ref[0, 0, pl.ds(off, blk), :]
        s = lax.dot_general(q, k, (((1,), (1,)), ((), ())), preferred_element_type=f32)
        s = s + (cq - ck_ref[0, 0, j])
        if masked:
            causal = (lax.broadcasted_iota(jnp.int32, (blk, blk), 0)
                      >= lax.broadcasted_iota(jnp.int32, (blk, blk), 1))
            s = jnp.where(causal, s, -jnp.inf)
        m_new = jnp.maximum(m, jnp.max(s, axis=1, keepdims=True))
        alpha = jnp.exp(m - m_new)
        p = jnp.exp(s - m_new)
        l = alpha * l + jnp.sum(p, axis=1, keepdims=True)
        acc = alpha * acc + jnp.dot(p.astype(v.dtype), v, preferred_element_type=f32)
        return m_new, l, acc

    init = (jnp.full((blk, 1), -1e30, f32), jnp.zeros((blk, 1), f32),
            jnp.zeros((blk, q.shape[1]), f32))
    carry = lax.fori_loop(0, i, lambda j, c: block(j, c, False), init)
    m, l, acc = block(i, carry, True)
    o_ref[0, 0] = (acc / l).astype(o_ref.dtype)


def _fox_attention(q, k, v, cum):
    B, H, S, dh = q.shape
    blk = min(ATT_BLOCK, S)
    nb = S // blk
    cq = cum.reshape(B, H, S, 1)
    ck = cum.reshape(B, H, nb, 1, blk)
    return pl.pallas_call(
        functools.partial(_fox_kernel, blk=blk),
        grid=(B, H, nb),
        in_specs=[pl.BlockSpec((1, 1, blk, dh), lambda b, h, i: (b, h, i, 0)),
                  pl.BlockSpec((1, 1, S, dh), lambda b, h, i: (b, h, 0, 0)),
                  pl.BlockSpec((1, 1, S, dh), lambda b, h, i: (b, h, 0, 0)),
                  pl.BlockSpec((1, 1, blk, 1), lambda b, h, i: (b, h, i, 0)),
                  pl.BlockSpec((1, 1, nb, 1, blk), lambda b, h, i: (b, h, 0, 0, 0))],
        out_specs=pl.BlockSpec((1, 1, blk, dh), lambda b, h, i: (b, h, i, 0)),
        out_shape=jax.ShapeDtypeStruct((B, H, S, dh), jnp.bfloat16),
        compiler_params=pltpu.CompilerParams(
            dimension_semantics=("arbitrary", "arbitrary", "arbitrary")),
        name="fox_attention",
    )(q, k, v, cq, ck)


def _topk_rows(s, k):
    rows, t = s.shape
    iota = lax.broadcasted_iota(jnp.int32, (rows, t), 0)
    slot = lax.broadcasted_iota(jnp.int32, (k, t), 0)
    vals = jnp.zeros((k, t), jnp.float32)
    ids = jnp.zeros((k, t), jnp.int32)
    for j in range(k):
        m = jnp.max(s, axis=0, keepdims=True)
        am = jnp.min(jnp.where(s == m, iota, rows), axis=0, keepdims=True)
        vals = jnp.where(slot == j, m, vals)
        ids = jnp.where(slot == j, am, ids)
        s = jnp.where(iota == am, -jnp.inf, s)
    return vals, ids


def _select_rows(table, pos):
    out = jnp.zeros(pos.shape, table.dtype)
    for r in range(table.shape[0]):
        out = jnp.where(pos == r, table[r:r + 1, :], out)
    return out


def _post_kernel(ya_ref, sga_ref, gb_ref, x_ref, mod_ref, n2g_ref, wa_ref, wo_ref, wq_ref, keys_ref,
                 x1_ref, h2_ref, idx_ref, gates_ref, q_scr):
    f32 = jnp.float32
    a = jnp.dot(ya_ref[...], wa_ref[...], preferred_element_type=f32)
    merged = sga_ref[...].astype(f32) * a + gb_ref[...].astype(f32)
    o = jnp.dot(merged.astype(jnp.bfloat16), wo_ref[...], preferred_element_type=f32)
    g1 = mod_ref[0, 2:3, :]
    sh2 = mod_ref[0, 3:4, :]
    sc2 = mod_ref[0, 4:5, :]
    x1 = x_ref[...] + g1 * o
    x1_ref[...] = x1
    h2 = x1 * lax.rsqrt(jnp.mean(x1 * x1, axis=-1, keepdims=True) + EPS) * n2g_ref[...]
    h2 = h2 * (1.0 + sc2) + sh2
    h2_ref[...] = h2
    qp = jnp.dot(h2.astype(jnp.bfloat16), wq_ref[...], preferred_element_type=f32)
    for j in range(2 * PEER_HEADS):
        q_scr[j] = qp[:, j * PEER_HALF:(j + 1) * PEER_HALF]

    def head(h, carry):
        tops = []
        for p in range(2):
            q = q_scr[2 * h + p]
            keys = keys_ref[2 * h + p]
            sc = lax.dot_general(keys, q, (((1,), (1,)), ((), ())),
                                 precision=lax.Precision.HIGHEST,
                                 preferred_element_type=f32)
            tops.append(_topk_rows(sc, PEER_TOPK))
        (s1, i1), (s2, i2) = tops
        half = PEER_TOPK // 2
        blocks = [s1[0:1, :] + s2]
        blocks += [s1[a:a + 1, :] + s2[0:half, :] for a in range(1, half)]
        blocks += [s1[half:, :] + s2[0:1, :]]
        vals, pos = _topk_rows(jnp.concatenate(blocks, axis=0), PEER_TOPK)
        mid = pos - PEER_TOPK
        tail0 = PEER_TOPK + (half - 1) * half
        ra = jnp.where(pos < PEER_TOPK, 0,
                       jnp.where(pos < tail0, 1 + (mid >> (half.bit_length() - 1)), pos - tail0 + half))
        rb = jnp.where(pos < PEER_TOPK, pos, jnp.where(pos < tail0, mid & (half - 1), 0))
        eid = _select_rows(i1, ra) * N_KEYS + _select_rows(i2, rb)
        e = jnp.exp(vals - vals[0:1, :])
        g = e / jnp.sum(e, axis=0, keepdims=True)
        row = pl.multiple_of(h * PEER_TOPK, PEER_TOPK)
        idx_ref[pl.ds(row, PEER_TOPK), :] = eid
        gates_ref[pl.ds(row, PEER_TOPK), :] = g
        return carry

    lax.fori_loop(0, PEER_HEADS, head, 0)


POST_TOK_BLOCK = 256


def _post_attention(ya, sga, gb, x, mod, norm2_g, w_a, w_out, w_query, sub_keys, seq_len):
    n, d = x.shape
    tm = POST_TOK_BLOCK
    blocks_per_seq = seq_len // tm
    aw = ya.shape[1]
    qw = w_query.shape[1]
    keys = sub_keys.reshape(2 * PEER_HEADS, N_KEYS, PEER_HALF)
    tok = lambda w: pl.BlockSpec((tm, w), lambda i: (i, 0))
    full = lambda a: pl.BlockSpec(a.shape, lambda i: (0,) * a.ndim)
    return pl.pallas_call(
        _post_kernel,
        grid=(n // tm,),
        in_specs=[tok(aw), tok(d), tok(d), tok(d),
                  pl.BlockSpec((1, 6, d), lambda i: (i // blocks_per_seq, 0, 0)),
                  full(norm2_g), full(w_a), full(w_out), full(w_query), full(keys)],
        out_specs=[tok(d), tok(d),
                   pl.BlockSpec((PEER_SLOTS, tm), lambda i: (0, i)),
                   pl.BlockSpec((PEER_SLOTS, tm), lambda i: (0, i))],
        out_shape=[jax.ShapeDtypeStruct((n, d), jnp.float32),
                   jax.ShapeDtypeStruct((n, d), jnp.float32),
                   jax.ShapeDtypeStruct((PEER_SLOTS, n), jnp.int32),
                   jax.ShapeDtypeStruct((PEER_SLOTS, n), jnp.float32)],
        scratch_shapes=[pltpu.VMEM((2 * PEER_HEADS, tm, PEER_HALF), jnp.float32)],
        compiler_params=pltpu.CompilerParams(dimension_semantics=("arbitrary",),
                                             vmem_limit_bytes=48 * 1024 * 1024),
        name="post_attention",
    )(ya, sga, gb, x, mod, norm2_g, w_a, w_out, w_query, keys)


def _rowsum_bcast(p, ones_bf16):
    hi = p.astype(jnp.bfloat16)
    lo = (p - hi.astype(jnp.float32)).astype(jnp.bfloat16)
    return (jnp.dot(hi, ones_bf16, preferred_element_type=jnp.float32)
            + jnp.dot(lo, ones_bf16, preferred_element_type=jnp.float32))


def _eval_experts(chunk, hrow, grow):
    n_chunks = hrow.shape[1] // LANES
    ones_bf16 = jnp.ones((LANES, LANES), jnp.bfloat16)
    eye = (lax.broadcasted_iota(jnp.int32, (PEER_SLOTS, LANES), 0)
           == lax.broadcasted_iota(jnp.int32, (PEER_SLOTS, LANES), 1))
    psum = None
    for c in range(n_chunks):
        u = lax.bitcast_convert_type(chunk(c) & jnp.uint32(0xFFFF0000), jnp.float32)
        p = u * hrow[:, c * LANES:(c + 1) * LANES]
        psum = p if psum is None else psum + p
    act = _gelu(_rowsum_bcast(psum, ones_bf16))
    gcol = _rowsum_bcast(jnp.where(eye, grow, 0.0), ones_bf16)
    coef = gcol * act
    outs = []
    for c in range(n_chunks):
        v = lax.bitcast_convert_type(chunk(c) << 16, jnp.float32)
        outs.append(jnp.sum(coef * v, axis=0, keepdims=True))
    return jnp.concatenate(outs, axis=-1)


def _finish_block(x_ref, g2_ref, fg_ref, peer, out_ref):
    y = x_ref[...] + g2_ref[0] * peer[...]
    out_ref[...] = y * lax.rsqrt(jnp.mean(y * y, axis=-1, keepdims=True) + EPS) * fg_ref[...]


def _expert_kernel(idx_ref, gates_ref, h_ref, x_ref, g2_ref, fg_ref, tab_ref, out_ref,
                   *scratch, tok_block, n_slots):
    bufs = scratch[:n_slots]
    peer, sem = scratch[n_slots], scratch[n_slots + 1]
    d_model = h_ref.shape[-1]
    n_chunks = d_model // LANES
    rows_per_tok = PEER_SLOTS * n_chunks

    def issue(t, s):
        for r in range(PEER_SLOTS):
            row = pl.multiple_of(idx_ref[t, r] * n_chunks, n_chunks)
            pltpu.make_async_copy(tab_ref.at[pl.ds(row, n_chunks), :],
                                  bufs[s].at[pl.ds(r * n_chunks, n_chunks), :],
                                  sem.at[s]).start(priority=r % DMA_THREADS)

    def wait(s):
        pltpu.make_async_copy(tab_ref.at[pl.ds(0, rows_per_tok), :], bufs[s], sem.at[s]).wait()

    def compute(t, s):
        chunk = lambda c: bufs[s][pl.ds(c, PEER_SLOTS, stride=n_chunks), :]
        peer[pl.ds(t, 1), :] = _eval_experts(chunk, h_ref[pl.ds(t, 1), :], gates_ref[pl.ds(t, 1), :])

    def step(t, s, prefetch):
        wait(s)
        if prefetch:
            issue(t + n_slots - 1, (s - 1) % n_slots)
        compute(t, s)

    for t in range(n_slots - 1):
        issue(t, t)

    n_groups = tok_block // n_slots

    def group(g, carry):
        for s in range(n_slots):
            step(g * n_slots + s, s, True)
        return carry

    lax.fori_loop(0, n_groups - 1, group, 0)
    for s in range(n_slots):
        t = (n_groups - 1) * n_slots + s
        step(t, s, t + n_slots - 1 < tok_block)

    _finish_block(x_ref, g2_ref, fg_ref, peer, out_ref)


def _sc_tanh(y):
    return 1.0 - 2.0 / (jnp.exp(2.0 * y) + 1.0)


def _sc_peer_experts(table3, ids, gates, h3):
    n_tok, n_chunks, lanes = h3.shape
    info = plsc.get_sparse_core_info()
    sl = info.num_lanes
    n_workers = info.num_cores * info.num_subcores
    tok_per_worker = n_tok // n_workers
    assert tok_per_worker * n_workers == n_tok
    rows = SC_GATHER_ROWS
    n_sub = PEER_SLOTS // rows
    pieces = [(c, k * sl) for c in range(n_chunks) for k in range(lanes // sl)]
    mesh = plsc.VectorSubcoreMesh(core_axis_name="c", subcore_axis_name="s")
    buf = lambda dt: pltpu.VMEM((rows, n_chunks, lanes), dt)

    @functools.partial(
        pl.kernel, mesh=mesh,
        out_type=jax.ShapeDtypeStruct((n_tok, n_chunks, lanes), jnp.float32),
        scratch_types=[pltpu.VMEM((PEER_SLOTS,), jnp.int32), pltpu.VMEM((PEER_SLOTS,), jnp.float32),
                       pltpu.VMEM((n_chunks, lanes), jnp.float32), buf(jnp.uint32), buf(jnp.uint32),
                       pltpu.VMEM((n_chunks, lanes), jnp.float32), pltpu.SemaphoreType.DMA((2,))],
        compiler_params=pltpu.CompilerParams(needs_layout_passes=False),
        name="sc_peer_experts")
    def run(table_hbm, ids_hbm, gates_hbm, h_hbm, peer_hbm, ids_v, g_v, h_v, rows0, rows1, out_v, sem):
        bufs = (rows0, rows1)
        wid = lax.axis_index("s") * info.num_cores + lax.axis_index("c")

        def gather(s):
            return pltpu.make_async_copy(table_hbm.at[ids_v.at[pl.ds(s * rows, rows)]],
                                         bufs[s % 2], sem.at[s % 2])

        @pl.loop(0, tok_per_worker)
        def _(ti):
            t = wid * tok_per_worker + ti
            off = pl.multiple_of(t * PEER_SLOTS, PEER_SLOTS)
            pltpu.sync_copy(ids_hbm.at[pl.ds(off, PEER_SLOTS)], ids_v)
            pltpu.sync_copy(gates_hbm.at[pl.ds(off, PEER_SLOTS)], g_v)
            pltpu.sync_copy(h_hbm.at[t], h_v)
            gather(0).start()
            for c, k in pieces:
                out_v[c, pl.ds(k, sl)] = jnp.zeros((sl,), jnp.float32)
            for s in range(n_sub):
                gather(s).wait()
                if s + 1 < n_sub:
                    gather(s + 1).start()
                rbuf = bufs[s % 2]

                @pl.loop(0, rows)
                def _(r):
                    acc = jnp.zeros((sl,), jnp.float32)
                    for c, k in pieces:
                        w = rbuf[r, c, pl.ds(k, sl)]
                        u = lax.bitcast_convert_type(w & jnp.uint32(0xFFFF0000), jnp.float32)
                        acc = acc + u * h_v[c, pl.ds(k, sl)]
                    a = jnp.broadcast_to(jnp.sum(acc), (sl,))
                    act = 0.5 * a * (1.0 + _sc_tanh(0.7978845608028654 * (a + 0.044715 * (a * a * a))))
                    gate = plsc.load_gather(g_v, [jnp.broadcast_to(s * rows + r, (sl,))])
                    coef = gate * act
                    for c, k in pieces:
                        w = rbuf[r, c, pl.ds(k, sl)]
                        v = lax.bitcast_convert_type(w << 16, jnp.float32)
                        plsc.addupdate(out_v.at[c, pl.ds(k, sl)], coef * v)

            pltpu.sync_copy(out_v, peer_hbm.at[t])

    return run(table3, ids, gates, h3)


def _finish_kernel(peer_ref, x_ref, g2_ref, fg_ref, prev_ref, out_ref):
    del prev_ref
    _finish_block(x_ref, g2_ref, fg_ref, peer_ref, out_ref)


def _finish_tokens(peer, x1, g2, fg, prev, seq_len):
    n, d = x1.shape
    tm = FINISH_TOK_BLOCK
    return pl.pallas_call(
        _finish_kernel,
        grid=(peer.shape[0] // tm,),
        in_specs=[pl.BlockSpec((tm, d), lambda i: (i, 0)),
                  pl.BlockSpec((tm, d), lambda i: (i, 0)),
                  pl.BlockSpec((1, 1, d), lambda i: (i * tm // seq_len, 0, 0)),
                  pl.BlockSpec((1, d), lambda i: (0, 0)),
                  pl.BlockSpec(memory_space=pl.ANY)],
        out_specs=pl.BlockSpec((tm, d), lambda i: (i, 0)),
        out_shape=jax.ShapeDtypeStruct((n, d), jnp.float32),
        input_output_aliases={4: 0},
        compiler_params=pltpu.CompilerParams(dimension_semantics=("arbitrary",)),
        name="finish_tokens",
    )(peer, x1, g2, fg, prev)


def _pack_expert_table(expert_u, expert_v):
    ub = lax.bitcast_convert_type(expert_u.astype(jnp.bfloat16), jnp.uint16).astype(jnp.uint32)
    vb = lax.bitcast_convert_type(expert_v.astype(jnp.bfloat16), jnp.uint16).astype(jnp.uint32)
    packed = (ub << 16) | vb
    n_experts, d = packed.shape
    return packed.reshape(n_experts * (d // LANES), LANES)


def _peer_experts_final(idx, gates, h2, x1, g2, final_g, table, seq_len, n_sc):
    n, d = h2.shape
    n_chunks = d // LANES
    tb = EXPERT_TOK_BLOCK
    assert tb % EXPERT_SLOTS == 0 and seq_len % tb == 0 and seq_len % FINISH_TOK_BLOCK == 0
    assert n_sc % tb == 0 and n_sc % FINISH_TOK_BLOCK == 0 and 0 <= n_sc < n
    first = n_sc // tb
    fg = final_g.reshape(1, d)
    out_shape = jax.ShapeDtypeStruct((n, d), jnp.float32)

    out = pl.pallas_call(
        functools.partial(_expert_kernel, tok_block=tb, n_slots=EXPERT_SLOTS),
        grid=((n - n_sc) // tb,),
        in_specs=[
            pl.BlockSpec((tb, PEER_SLOTS), lambda i: (i + first, 0), memory_space=pltpu.SMEM),
            pl.BlockSpec((tb, PEER_SLOTS), lambda i: (i + first, 0)),
            pl.BlockSpec((tb, d), lambda i: (i + first, 0)),
            pl.BlockSpec((tb, d), lambda i: (i + first, 0)),
            pl.BlockSpec((1, 1, d), lambda i: ((i + first) * tb // seq_len, 0, 0)),
            pl.BlockSpec((1, d), lambda i: (0, 0)),
            pl.BlockSpec(memory_space=pl.ANY),
        ],
        out_specs=pl.BlockSpec((tb, d), lambda i: (i + first, 0)),
        out_shape=out_shape,
        scratch_shapes=(
            [pltpu.VMEM((PEER_SLOTS * n_chunks, LANES), jnp.uint32) for _ in range(EXPERT_SLOTS)]
            + [pltpu.VMEM((tb, d), jnp.float32), pltpu.SemaphoreType.DMA((EXPERT_SLOTS,))]),
        compiler_params=pltpu.CompilerParams(dimension_semantics=("arbitrary",)),
        name="peer_experts",
    )(idx, gates, h2, x1, g2, fg, table)
    if n_sc == 0:
        return out

    peer_sc = _sc_peer_experts(table.reshape(-1, n_chunks, LANES), idx[:n_sc].reshape(-1),
                               gates[:n_sc].reshape(-1), h2[:n_sc].reshape(n_sc, n_chunks, LANES))
    return _finish_tokens(peer_sc.reshape(n_sc, d), x1, g2, fg, out, seq_len)


def kernel(x, c, w_mod, b_mod, norm1_g, w_in, b_forget, ln_v_g, w_spatial, b_spatial, w_branch_a, w_branch_b, w_out, norm2_g, w_query, sub_keys, expert_u, expert_v, final_g):
    B, S, D = x.shape
    n = B * S
    bf16 = jnp.bfloat16
    assert w_mod.shape[0] == 1, "the final RMSNorm is fused into the single layer's expert kernel"
    l = 0
    mod = _modulation(c, w_mod[l], b_mod[l]).reshape(B, 6, D)

    qkv, cum, sga, gb = _input_projection(x, mod, norm1_g[l], w_in[l], b_forget[l], ln_v_g[l],
                                          w_spatial[l], b_spatial[l], w_branch_b[l])
    heads = lambda a: a.reshape(B, S, ATT_HEADS, ATT_HEAD_DIM).transpose(0, 2, 1, 3)
    q = heads(qkv[:, 0:ATT_WIDTH])
    k = heads(qkv[:, ATT_WIDTH:2 * ATT_WIDTH])
    v = heads(qkv[:, 2 * ATT_WIDTH:])
    cum_h = cum[:, 0:ATT_HEADS].reshape(B, S, ATT_HEADS).transpose(0, 2, 1)
    y_a = _fox_attention(q, k, v, cum_h).transpose(0, 2, 1, 3).reshape(n, ATT_WIDTH)

    x1, h2, idx_t, gates_t = _post_attention(
        y_a, sga, gb, x.reshape(n, D), mod, norm2_g[l].reshape(1, D),
        w_branch_a[l].astype(bf16), w_out[l].astype(bf16), w_query[l].astype(bf16),
        sub_keys[l], S)
    table = _pack_expert_table(expert_u[l], expert_v[l])
    n_sc = n * SC_SHARE_PERCENT // 100 // SC_SHARE_ALIGN * SC_SHARE_ALIGN
    out = _peer_experts_final(idx_t.T, gates_t.T, h2, x1, mod[:, 5:6, :], final_g, table, S, n_sc)
    return out.reshape(B, S, D)
```

```python
import functools

import jax
import jax.numpy as jnp
from jax import lax
from jax.experimental import pallas as pl
from jax.experimental.pallas import tpu as pltpu
from jax.experimental.pallas import tpu_sc as plsc

D_MODEL = 1024
ATT_HEADS = 8
ATT_HEAD_DIM = 64
ATT_WIDTH = ATT_HEADS * ATT_HEAD_DIM
Q_BLOCK = 128
GM_GROUPS = 4
GM_GROUP_DIM = 128
GM_WIDTH = GM_GROUPS * GM_GROUP_DIM
GM_CHUNK = 128
PEER_HEADS = 8
PEER_KEY_DIM = 256
PEER_HALF = PEER_KEY_DIM // 2
N_KEYS = 128
PEER_TOPK = 16
PEER_TOK_BLOCK = 128
PEER_SLOTS = PEER_HEADS * PEER_TOPK
SPLIT_POINTS = (ATT_WIDTH, 2 * ATT_WIDTH, 3 * ATT_WIDTH, 3 * ATT_WIDTH + ATT_HEADS,
                3 * ATT_WIDTH + ATT_HEADS + 2 * GM_WIDTH,
                3 * ATT_WIDTH + ATT_HEADS + 2 * GM_WIDTH + D_MODEL)
EPS = 1e-6

LANES = 128
EXPERT_TOK_BLOCK = 128
EXPERT_SLOTS = 4
DMA_THREADS = 2
FINISH_TOK_BLOCK = 512
SC_GATHER_ROWS = 32
SC_ROW_GROUP = 4
SC_STORE_BATCH = 8
SC_SHARE_PERCENT = 50


def _gelu(x):
    return 0.5 * x * (1.0 + jnp.tanh(0.7978845608028654 * (x + 0.044715 * (x * x * x))))


def _rmsnorm(x, g):
    y = x * lax.rsqrt(jnp.mean(x * x, axis=-1, keepdims=True) + EPS)
    return y * g


def _layernorm(x, g):
    mu = jnp.mean(x, axis=-1, keepdims=True)
    var = jnp.mean(jnp.square(x - mu), axis=-1, keepdims=True)
    return (x - mu) * lax.rsqrt(var + EPS) * g


def _modulate(h, shift, scale):
    return h * (1.0 + scale[:, None, :]) + shift[:, None, :]


def _fox(q, k, v, log_f):
    B, S, H, Dh = q.shape
    nb = S // Q_BLOCK
    cum = jnp.cumsum(log_f, axis=1)
    cum_k = jnp.transpose(cum, (0, 2, 1))
    k_pos = jnp.arange(S)
    q_blocks = q.reshape(B, nb, Q_BLOCK, H, Dh).transpose(1, 0, 2, 3, 4)
    cq_blocks = cum_k.reshape(B, H, nb, Q_BLOCK).transpose(2, 0, 1, 3)
    starts = jnp.arange(nb) * Q_BLOCK
    scale = ATT_HEAD_DIM ** -0.5

    def one_block(args):
        q_blk, cq_blk, start = args
        s = jnp.einsum('bqhd,bkhd->bhqk', q_blk, k) * scale
        s = s + cq_blk[..., :, None] - cum_k[..., None, :]
        q_pos = start + jnp.arange(Q_BLOCK)
        causal = q_pos[:, None] >= k_pos[None, :]
        s = jnp.where(causal[None, None], s, -jnp.inf)
        p = jax.nn.softmax(s, axis=-1)
        return jnp.einsum('bhqk,bkhd->bqhd', p, v)

    out = lax.map(one_block, (q_blocks, cq_blocks, starts))
    return out.transpose(1, 0, 2, 3, 4).reshape(B, S, H * Dh)


def _gmlp(z, ln_g, w_s, b_s):
    B, S, _ = z.shape
    u, v = jnp.split(z, 2, axis=-1)
    v = _layernorm(v, ln_g)
    nc = S // GM_CHUNK
    v = v.reshape(B, nc, GM_CHUNK, GM_GROUPS, GM_GROUP_DIM)
    tril = jnp.tril(jnp.ones((GM_CHUNK, GM_CHUNK), dtype=bool))
    w = jnp.where(tril[None], w_s, jnp.zeros_like(w_s))
    mixed = jnp.einsum('gts,bcsgd->bctgd', w, v) + b_s.T[None, None, :, :, None]
    return u * mixed.reshape(B, S, GM_WIDTH)


def _peer_route(hf, w_query, sub_keys):
    n = hf.shape[0]
    q = (hf @ w_query).reshape(n, PEER_HEADS, 2, PEER_HALF)
    scores = jnp.einsum('nhpd,hpkd->nhpk', q, sub_keys)
    s_top, i_top = lax.top_k(scores, PEER_TOPK)
    cand_s = (s_top[:, :, 0, :, None] + s_top[:, :, 1, None, :]).reshape(n, PEER_HEADS, PEER_TOPK * PEER_TOPK)
    cand_i = (i_top[:, :, 0, :, None] * N_KEYS + i_top[:, :, 1, None, :]).reshape(n, PEER_HEADS, PEER_TOPK * PEER_TOPK)
    best_s, best_pos = lax.top_k(cand_s, PEER_TOPK)
    idx = jnp.take_along_axis(cand_i, best_pos, axis=-1)
    gates = jax.nn.softmax(best_s, axis=-1)
    return idx.reshape(n, PEER_SLOTS), gates.reshape(n, PEER_SLOTS)


def _mod_kernel(c_ref, w_ref, b_ref, o_ref):
    c = c_ref[...]
    sc = c * jax.nn.sigmoid(c)
    o_ref[...] = jnp.dot(sc, w_ref[...], precision=lax.Precision.HIGHEST,
                         preferred_element_type=jnp.float32) + b_ref[...]


def _modulation(c, w_mod, b_mod):
    b, d = c.shape
    cols = w_mod.shape[1]
    return pl.pallas_call(
        _mod_kernel,
        grid=(cols // d,),
        in_specs=[pl.BlockSpec((b, d), lambda j: (0, 0)),
                  pl.BlockSpec((d, d), lambda j: (0, j)),
                  pl.BlockSpec((1, d), lambda j: (0, j))],
        out_specs=pl.BlockSpec((b, d), lambda j: (0, j)),
        out_shape=jax.ShapeDtypeStruct((b, cols), jnp.float32),
        name="modulation",
    )(c, w_mod, b_mod.reshape(1, cols))


INPROJ_TOK_BLOCK = 256


def _inproj_kernel(x_ref, mod_ref, n1g_ref, wqkv_ref, wf_ref, bf_ref, wz_ref, wg_ref, lng_ref,
                   wsp_ref, bsp_ref, wb_ref, qkv_ref, cum_ref, sga_ref, gb_ref, carry):
    f32, bf16 = jnp.float32, jnp.bfloat16
    tm, d = x_ref.shape[1], x_ref.shape[2]
    x = x_ref[0]
    sh1 = mod_ref[0, 0:1, :]
    sc1 = mod_ref[0, 1:2, :]
    h = x * lax.rsqrt(jnp.mean(x * x, axis=-1, keepdims=True) + EPS) * n1g_ref[...]
    hb = (h * (1.0 + sc1) + sh1).astype(bf16)

    qkv = jnp.dot(hb, wqkv_ref[...], preferred_element_type=f32)
    qkv_ref[:, 0:ATT_WIDTH] = (qkv[:, 0:ATT_WIDTH] * (ATT_HEAD_DIM ** -0.5)).astype(bf16)
    qkv_ref[:, ATT_WIDTH:] = qkv[:, ATT_WIDTH:].astype(bf16)

    f = jnp.dot(hb, wf_ref[...], preferred_element_type=f32) + bf_ref[...]
    logf = jnp.minimum(f, 0.0) - jnp.log1p(jnp.exp(-jnp.abs(f)))

    @pl.when(pl.program_id(1) == 0)
    def _():
        carry[...] = jnp.zeros_like(carry)

    tri = (lax.broadcasted_iota(jnp.int32, (tm, tm), 0)
           >= lax.broadcasted_iota(jnp.int32, (tm, tm), 1)).astype(f32)
    cum = jnp.dot(tri, logf, precision=lax.Precision.HIGHEST, preferred_element_type=f32) + carry[...]
    cum_ref[...] = cum
    carry[...] = cum[tm - 1:tm, :]

    gz = _gelu(jnp.dot(hb, wz_ref[...], preferred_element_type=f32))
    u = gz[:, 0:GM_WIDTH]
    v = gz[:, GM_WIDTH:]
    mu = jnp.mean(v, axis=-1, keepdims=True)
    var = jnp.mean(jnp.square(v - mu), axis=-1, keepdims=True)
    vn = ((v - mu) * lax.rsqrt(var + EPS) * lng_ref[...]).astype(bf16)
    tril = (lax.broadcasted_iota(jnp.int32, (GM_CHUNK, GM_CHUNK), 0)
            >= lax.broadcasted_iota(jnp.int32, (GM_CHUNK, GM_CHUNK), 1))
    w_sp = [jnp.where(tril, wsp_ref[g], 0.0).astype(bf16) for g in range(GM_GROUPS)]
    rows = []
    for ck in range(tm // GM_CHUNK):
        r0 = ck * GM_CHUNK
        cols = []
        for g in range(GM_GROUPS):
            c0 = g * GM_GROUP_DIM
            mixed = jnp.dot(w_sp[g], vn[r0:r0 + GM_CHUNK, c0:c0 + GM_GROUP_DIM],
                            preferred_element_type=f32) + bsp_ref[g]
            cols.append(u[r0:r0 + GM_CHUNK, c0:c0 + GM_GROUP_DIM] * mixed)
        rows.append(jnp.concatenate(cols, axis=1))
    yb = jnp.concatenate(rows, axis=0).astype(bf16)
    ybp = jnp.dot(yb, wb_ref[...], preferred_element_type=f32)

    sg = jax.nn.sigmoid(jnp.dot(hb, wg_ref[...], preferred_element_type=f32))
    sga_ref[...] = sg[:, 0:d].astype(bf16)
    gb_ref[...] = (sg[:, d:] * ybp).astype(bf16)


def _input_projection(x, mod, norm1_g, w_in, b_forget, ln_v_g, w_spatial, b_spatial, w_branch_b):
    B, S, d = x.shape
    n = B * S
    tm = INPROJ_TOK_BLOCK
    bf16 = jnp.bfloat16
    p0, p1, p2, p3, p4, p5 = SPLIT_POINTS
    w_qkv = w_in[:, 0:p2].astype(bf16)
    w_f = jnp.pad(w_in[:, p2:p3], ((0, 0), (0, LANES - ATT_HEADS))).astype(bf16)
    b_f = jnp.pad(b_forget, (0, LANES - ATT_HEADS)).reshape(1, LANES)
    w_z = w_in[:, p3:p4].astype(bf16)
    w_g = w_in[:, p4:].astype(bf16)
    nt = S // tm
    tok = lambda w: pl.BlockSpec((tm, w), lambda b, i: (b * nt + i, 0))
    full = lambda a: pl.BlockSpec(a.shape, lambda b, i: (0,) * a.ndim)
    args = (x, mod, norm1_g.reshape(1, d), w_qkv, w_f, b_f, w_z, w_g, ln_v_g.reshape(1, GM_WIDTH),
            w_spatial, b_spatial.reshape(GM_GROUPS, GM_CHUNK, 1), w_branch_b.astype(bf16))
    return pl.pallas_call(
        _inproj_kernel,
        grid=(B, nt),
        in_specs=[pl.BlockSpec((1, tm, d), lambda b, i: (b, i, 0)),
                  pl.BlockSpec((1, 6, d), lambda b, i: (b, 0, 0))] + [full(a) for a in args[2:]],
        out_specs=[tok(3 * ATT_WIDTH), tok(LANES), tok(d), tok(d)],
        out_shape=[jax.ShapeDtypeStruct((n, 3 * ATT_WIDTH), bf16),
                   jax.ShapeDtypeStruct((n, LANES), jnp.float32),
                   jax.ShapeDtypeStruct((n, d), bf16),
                   jax.ShapeDtypeStruct((n, d), bf16)],
        scratch_shapes=[pltpu.VMEM((1, LANES), jnp.float32)],
        compiler_params=pltpu.CompilerParams(dimension_semantics=("arbitrary", "arbitrary"),
                                             vmem_limit_bytes=56 * 1024 * 1024),
        name="input_projection",
    )(*args)


ATT_BLOCK = 512


def _fox_kernel(q_ref, k_ref, v_ref, cq_ref, ck_ref, o_ref, *, blk):
    f32 = jnp.float32
    i = pl.program_id(2)
    q = q_ref[0, 0]
    cq = cq_ref[0, 0]

    def block(j, carry, masked):
        m, l, acc = carry
        off = pl.multiple_of(j * blk, blk)
        k = k_ref[0, 0, pl.ds(off, blk), :]
        v = v_ref[0, 0, pl.ds(off, blk), :]
        s = lax.dot_general(q, k, (((1,), (1,)), ((), ())), preferred_element_type=f32)
        s = s + (cq - ck_ref[0, 0, j])
        if masked:
            causal = (lax.broadcasted_iota(jnp.int32, (blk, blk), 0)
                      >= lax.broadcasted_iota(jnp.int32, (blk, blk), 1))
            s = jnp.where(causal, s, -jnp.inf)
        m_new = jnp.maximum(m, jnp.max(s, axis=1, keepdims=True))
        alpha = jnp.exp(m - m_new)
        p = jnp.exp(s - m_new)
        l = alpha * l + jnp.sum(p, axis=1, keepdims=True)
        acc = alpha * acc + jnp.dot(p.astype(v.dtype), v, preferred_element_type=f32)
        return m_new, l, acc

    init = (jnp.full((blk, 1), -1e30, f32), jnp.zeros((blk, 1), f32),
            jnp.zeros((blk, q.shape[1]), f32))
    carry = lax.fori_loop(0, i, lambda j, c: block(j, c, False), init)
    m, l, acc = block(i, carry, True)
    o_ref[0, 0] = (acc / l).astype(o_ref.dtype)


def _fox_attention(q, k, v, cum):
    B, H, S, dh = q.shape
    blk = min(ATT_BLOCK, S)
    nb = S // blk
    cq = cum.reshape(B, H, S, 1)
    ck = cum.reshape(B, H, nb, 1, blk)
    return pl.pallas_call(
        functools.partial(_fox_kernel, blk=blk),
        grid=(B, H, nb),
        in_specs=[pl.BlockSpec((1, 1, blk, dh), lambda b, h, i: (b, h, i, 0)),
                  pl.BlockSpec((1, 1, S, dh), lambda b, h, i: (b, h, 0, 0)),
                  pl.BlockSpec((1, 1, S, dh), lambda b, h, i: (b, h, 0, 0)),
                  pl.BlockSpec((1, 1, blk, 1), lambda b, h, i: (b, h, i, 0)),
                  pl.BlockSpec((1, 1, nb, 1, blk), lambda b, h, i: (b, h, 0, 0, 0))],
        out_specs=pl.BlockSpec((1, 1, blk, dh), lambda b, h, i: (b, h, i, 0)),
        out_shape=jax.ShapeDtypeStruct((B, H, S, dh), jnp.bfloat16),
        compiler_params=pltpu.CompilerParams(
            dimension_semantics=("arbitrary", "arbitrary", "arbitrary")),
        name="fox_attention",
    )(q, k, v, cq, ck)


def _topk_rows(s, k):
    rows, t = s.shape
    iota = lax.broadcasted_iota(jnp.int32, (rows, t), 0)
    slot = lax.broadcasted_iota(jnp.int32, (k, t), 0)
    vals = jnp.zeros((k, t), jnp.float32)
    ids = jnp.zeros((k, t), jnp.int32)
    for j in range(k):
        m = jnp.max(s, axis=0, keepdims=True)
        am = jnp.min(jnp.where(s == m, iota, rows), axis=0, keepdims=True)
        vals = jnp.where(slot == j, m, vals)
        ids = jnp.where(slot == j, am, ids)
        s = jnp.where(iota == am, -jnp.inf, s)
    return vals, ids


def _select_rows(table, pos):
    out = jnp.zeros(pos.shape, table.dtype)
    for r in range(table.shape[0]):
        out = jnp.where(pos == r, table[r:r + 1, :], out)
    return out


def _post_kernel(ya_ref, sga_ref, gb_ref, x_ref, mod_ref, n2g_ref, wa_ref, wo_ref, wq_ref, keys_ref,
                 x1_ref, h2_ref, idx_ref, gates_ref, q_scr):
    f32 = jnp.float32
    a = jnp.dot(ya_ref[...], wa_ref[...], preferred_element_type=f32)
    merged = sga_ref[...].astype(f32) * a + gb_ref[...].astype(f32)
    o = jnp.dot(merged.astype(jnp.bfloat16), wo_ref[...], preferred_element_type=f32)
    g1 = mod_ref[0, 2:3, :]
    sh2 = mod_ref[0, 3:4, :]
    sc2 = mod_ref[0, 4:5, :]
    x1 = x_ref[...] + g1 * o
    x1_ref[...] = x1
    h2 = x1 * lax.rsqrt(jnp.mean(x1 * x1, axis=-1, keepdims=True) + EPS) * n2g_ref[...]
    h2 = h2 * (1.0 + sc2) + sh2
    h2_ref[...] = h2
    qp = jnp.dot(h2.astype(jnp.bfloat16), wq_ref[...], preferred_element_type=f32)
    for j in range(2 * PEER_HEADS):
        q_scr[j] = qp[:, j * PEER_HALF:(j + 1) * PEER_HALF]

    def head(h, carry):
        tops = []
        for p in range(2):
            q = q_scr[2 * h + p]
            keys = keys_ref[2 * h + p]
            sc = lax.dot_general(keys, q, (((1,), (1,)), ((), ())),
                                 precision=lax.Precision.HIGHEST,
                                 preferred_element_type=f32)
            tops.append(_topk_rows(sc, PEER_TOPK))
        (s1, i1), (s2, i2) = tops
        half = PEER_TOPK // 2
        blocks = [s1[0:1, :] + s2]
        blocks += [s1[a:a + 1, :] + s2[0:half, :] for a in range(1, half)]
        blocks += [s1[half:, :] + s2[0:1, :]]
        vals, pos = _topk_rows(jnp.concatenate(blocks, axis=0), PEER_TOPK)
        mid = pos - PEER_TOPK
        tail0 = PEER_TOPK + (half - 1) * half
        ra = jnp.where(pos < PEER_TOPK, 0,
                       jnp.where(pos < tail0, 1 + (mid >> (half.bit_length() - 1)), pos - tail0 + half))
        rb = jnp.where(pos < PEER_TOPK, pos, jnp.where(pos < tail0, mid & (half - 1), 0))
        eid = _select_rows(i1, ra) * N_KEYS + _select_rows(i2, rb)
        e = jnp.exp(vals - vals[0:1, :])
        g = e / jnp.sum(e, axis=0, keepdims=True)
        row = pl.multiple_of(h * PEER_TOPK, PEER_TOPK)
        idx_ref[pl.ds(row, PEER_TOPK), :] = eid
        gates_ref[pl.ds(row, PEER_TOPK), :] = g
        return carry

    lax.fori_loop(0, PEER_HEADS, head, 0)


POST_TOK_BLOCK = 256


def _post_attention(ya, sga, gb, x, mod, norm2_g, w_a, w_out, w_query, sub_keys, seq_len):
    n, d = x.shape
    tm = POST_TOK_BLOCK
    blocks_per_seq = seq_len // tm
    aw = ya.shape[1]
    qw = w_query.shape[1]
    keys = sub_keys.reshape(2 * PEER_HEADS, N_KEYS, PEER_HALF)
    tok = lambda w: pl.BlockSpec((tm, w), lambda i: (i, 0))
    full = lambda a: pl.BlockSpec(a.shape, lambda i: (0,) * a.ndim)
    return pl.pallas_call(
        _post_kernel,
        grid=(n // tm,),
        in_specs=[tok(aw), tok(d), tok(d), tok(d),
                  pl.BlockSpec((1, 6, d), lambda i: (i // blocks_per_seq, 0, 0)),
                  full(norm2_g), full(w_a), full(w_out), full(w_query), full(keys)],
        out_specs=[tok(d), tok(d),
                   pl.BlockSpec((PEER_SLOTS, tm), lambda i: (0, i)),
                   pl.BlockSpec((PEER_SLOTS, tm), lambda i: (0, i))],
        out_shape=[jax.ShapeDtypeStruct((n, d), jnp.float32),
                   jax.ShapeDtypeStruct((n, d), jnp.float32),
                   jax.ShapeDtypeStruct((PEER_SLOTS, n), jnp.int32),
                   jax.ShapeDtypeStruct((PEER_SLOTS, n), jnp.float32)],
        scratch_shapes=[pltpu.VMEM((2 * PEER_HEADS, tm, PEER_HALF), jnp.float32)],
        compiler_params=pltpu.CompilerParams(dimension_semantics=("arbitrary",),
                                             vmem_limit_bytes=48 * 1024 * 1024),
        name="post_attention",
    )(ya, sga, gb, x, mod, norm2_g, w_a, w_out, w_query, keys)


def _rowsum_bcast(p, ones_bf16):
    hi = p.astype(jnp.bfloat16)
    lo = (p - hi.astype(jnp.float32)).astype(jnp.bfloat16)
    return (jnp.dot(hi, ones_bf16, preferred_element_type=jnp.float32)
            + jnp.dot(lo, ones_bf16, preferred_element_type=jnp.float32))


def _eval_experts(chunk, hrow, grow):
    n_chunks = hrow.shape[1] // LANES
    ones_bf16 = jnp.ones((LANES, LANES), jnp.bfloat16)
    eye = (lax.broadcasted_iota(jnp.int32, (PEER_SLOTS, LANES), 0)
           == lax.broadcasted_iota(jnp.int32, (PEER_SLOTS, LANES), 1))
    psum = None
    for c in range(n_chunks):
        u = lax.bitcast_convert_type(chunk(c) & jnp.uint32(0xFFFF0000), jnp.float32)
        p = u * hrow[:, c * LANES:(c + 1) * LANES]
        psum = p if psum is None else psum + p
    act = _gelu(_rowsum_bcast(psum, ones_bf16))
    gcol = _rowsum_bcast(jnp.where(eye, grow, 0.0), ones_bf16)
    coef = gcol * act
    outs = []
    for c in range(n_chunks):
        v = lax.bitcast_convert_type(chunk(c) << 16, jnp.float32)
        outs.append(jnp.sum(coef * v, axis=0, keepdims=True))
    return jnp.concatenate(outs, axis=-1)


def _finish_block(x_ref, g2_ref, fg_ref, peer, out_ref):
    y = x_ref[...] + g2_ref[0] * peer[...]
    out_ref[...] = y * lax.rsqrt(jnp.mean(y * y, axis=-1, keepdims=True) + EPS) * fg_ref[...]


def _expert_kernel(idx_ref, gates_ref, h_ref, x_ref, g2_ref, fg_ref, tab_ref, out_ref,
                   *scratch, tok_block, n_slots):
    bufs = scratch[:n_slots]
    peer, sem = scratch[n_slots], scratch[n_slots + 1]
    d_model = h_ref.shape[-1]
    n_chunks = d_model // LANES
    rows_per_tok = PEER_SLOTS * n_chunks

    def issue(t, s):
        for r in range(PEER_SLOTS):
            row = pl.multiple_of(idx_ref[t, r] * n_chunks, n_chunks)
            pltpu.make_async_copy(tab_ref.at[pl.ds(row, n_chunks), :],
                                  bufs[s].at[pl.ds(r * n_chunks, n_chunks), :],
                                  sem.at[s]).start(priority=r % DMA_THREADS)

    def wait(s):
        pltpu.make_async_copy(tab_ref.at[pl.ds(0, rows_per_tok), :], bufs[s], sem.at[s]).wait()

    def compute(t, s):
        chunk = lambda c: bufs[s][pl.ds(c, PEER_SLOTS, stride=n_chunks), :]
        peer[pl.ds(t, 1), :] = _eval_experts(chunk, h_ref[pl.ds(t, 1), :], gates_ref[pl.ds(t, 1), :])

    def step(t, s, prefetch):
        wait(s)
        if prefetch:
            issue(t + n_slots - 1, (s - 1) % n_slots)
        compute(t, s)

    for t in range(n_slots - 1):
        issue(t, t)

    n_groups = tok_block // n_slots

    def group(g, carry):
        for s in range(n_slots):
            step(g * n_slots + s, s, True)
        return carry

    lax.fori_loop(0, n_groups - 1, group, 0)
    for s in range(n_slots):
        t = (n_groups - 1) * n_slots + s
        step(t, s, t + n_slots - 1 < tok_block)

    _finish_block(x_ref, g2_ref, fg_ref, peer, out_ref)


def _sc_tanh(y):
    return 1.0 - 2.0 / (jnp.exp(2.0 * y) + 1.0)


def _sc_peer_experts(table3, ids, gates, h3):
    n_tok, n_chunks, lanes = h3.shape
    info = plsc.get_sparse_core_info()
    sl = info.num_lanes
    n_workers = info.num_cores * info.num_subcores
    tok_per_worker = n_tok // n_workers
    assert tok_per_worker * n_workers == n_tok
    rows = SC_GATHER_ROWS
    n_sub = PEER_SLOTS // rows
    pieces = [(c, k * sl) for c in range(n_chunks) for k in range(lanes // sl)]
    mesh = plsc.VectorSubcoreMesh(core_axis_name="c", subcore_axis_name="s")
    buf = lambda dt: pltpu.VMEM((rows, n_chunks, lanes), dt)

    @functools.partial(
        pl.kernel, mesh=mesh,
        out_type=jax.ShapeDtypeStruct((n_tok, n_chunks, lanes), jnp.float32),
        scratch_types=[pltpu.VMEM((PEER_SLOTS,), jnp.int32), pltpu.VMEM((PEER_SLOTS,), jnp.float32),
                       pltpu.VMEM((n_chunks, lanes), jnp.float32), buf(jnp.uint32), buf(jnp.uint32),
                       pltpu.VMEM((n_chunks, lanes), jnp.float32), pltpu.SemaphoreType.DMA((2,))],
        compiler_params=pltpu.CompilerParams(needs_layout_passes=False),
        name="sc_peer_experts")
    def run(table_hbm, ids_hbm, gates_hbm, h_hbm, peer_hbm, ids_v, g_v, h_v, rows0, rows1, out_v, sem):
        bufs = (rows0, rows1)
        wid = lax.axis_index("s") * info.num_cores + lax.axis_index("c")

        def gather(s):
            return pltpu.make_async_copy(table_hbm.at[ids_v.at[pl.ds(s * rows, rows)]],
                                         bufs[s % 2], sem.at[s % 2])

        @pl.loop(0, tok_per_worker)
        def _(ti):
            t = wid * tok_per_worker + ti
            off = pl.multiple_of(t * PEER_SLOTS, PEER_SLOTS)
            pltpu.sync_copy(ids_hbm.at[pl.ds(off, PEER_SLOTS)], ids_v)
            pltpu.sync_copy(gates_hbm.at[pl.ds(off, PEER_SLOTS)], g_v)
            pltpu.sync_copy(h_hbm.at[t], h_v)
            gather(0).start()
            for c, k in pieces:
                out_v[c, pl.ds(k, sl)] = jnp.zeros((sl,), jnp.float32)
            for s in range(n_sub):
                gather(s).wait()
                if s + 1 < n_sub:
                    gather(s + 1).start()
                rbuf = bufs[s % 2]

                @pl.loop(0, rows, step=SC_ROW_GROUP)
                def _(r0):
                    accs = [jnp.zeros((sl,), jnp.float32) for _ in range(SC_ROW_GROUP)]
                    for c, k in pieces:
                        hv = h_v[c, pl.ds(k, sl)]
                        for j in range(SC_ROW_GROUP):
                            w = rbuf[r0 + j, c, pl.ds(k, sl)]
                            u = lax.bitcast_convert_type(w & jnp.uint32(0xFFFF0000), jnp.float32)
                            accs[j] = accs[j] + u * hv
                    coefs = []
                    for j in range(SC_ROW_GROUP):
                        a = jnp.broadcast_to(jnp.sum(accs[j]), (sl,))
                        act = 0.5 * a * (1.0 + _sc_tanh(0.7978845608028654 * (a + 0.044715 * (a * a * a))))
                        gate = plsc.load_gather(g_v, [jnp.broadcast_to(s * rows + r0 + j, (sl,))])
                        coefs.append(gate * act)
                    for b0 in range(0, len(pieces), SC_STORE_BATCH):
                        tots = []
                        for c, k in pieces[b0:b0 + SC_STORE_BATCH]:
                            tot = None
                            for j in range(SC_ROW_GROUP):
                                w = rbuf[r0 + j, c, pl.ds(k, sl)]
                                term = coefs[j] * lax.bitcast_convert_type(w << 16, jnp.float32)
                                tot = term if tot is None else tot + term
                            tots.append(tot)
                        for (c, k), tot in zip(pieces[b0:b0 + SC_STORE_BATCH], tots):
                            plsc.addupdate(out_v.at[c, pl.ds(k, sl)], tot)

            pltpu.sync_copy(out_v, peer_hbm.at[t])

    return run(table3, ids, gates, h3)


def _finish_kernel(peer_ref, x_ref, g2_ref, fg_ref, prev_ref, out_ref):
    del prev_ref
    _finish_block(x_ref, g2_ref, fg_ref, peer_ref, out_ref)


def _finish_tokens(peer, x1, g2, fg, prev, seq_len):
    d = x1.shape[1]
    tm = FINISH_TOK_BLOCK
    return pl.pallas_call(
        _finish_kernel,
        grid=(peer.shape[0] // tm,),
        in_specs=[pl.BlockSpec((tm, d), lambda i: (i, 0)),
                  pl.BlockSpec((tm, d), lambda i: (i, 0)),
                  pl.BlockSpec((1, 1, d), lambda i: (i * tm // seq_len, 0, 0)),
                  pl.BlockSpec((1, d), lambda i: (0, 0)),
                  pl.BlockSpec(memory_space=pl.ANY)],
        out_specs=pl.BlockSpec((tm, d), lambda i: (i, 0)),
        out_shape=jax.ShapeDtypeStruct(prev.shape, prev.dtype),
        input_output_aliases={4: 0},
        compiler_params=pltpu.CompilerParams(dimension_semantics=("arbitrary",)),
        name="finish_tokens",
    )(peer, x1, g2, fg, prev)


def _pack_expert_table(expert_u, expert_v):
    ub = lax.bitcast_convert_type(expert_u.astype(jnp.bfloat16), jnp.uint16).astype(jnp.uint32)
    vb = lax.bitcast_convert_type(expert_v.astype(jnp.bfloat16), jnp.uint16).astype(jnp.uint32)
    packed = (ub << 16) | vb
    n_experts, d = packed.shape
    return packed.reshape(n_experts * (d // LANES), LANES)


def _tc_peer_experts(idx, gates, h2, x1, g2, final_g, table, seq_len, n_out):
    m, d = h2.shape
    n_chunks = d // LANES
    tb = EXPERT_TOK_BLOCK
    assert tb % EXPERT_SLOTS == 0 and seq_len % tb == 0 and (n_out - m) % tb == 0
    first = (n_out - m) // tb
    return pl.pallas_call(
        functools.partial(_expert_kernel, tok_block=tb, n_slots=EXPERT_SLOTS),
        grid=(m // tb,),
        in_specs=[
            pl.BlockSpec((tb, PEER_SLOTS), lambda i: (i, 0), memory_space=pltpu.SMEM),
            pl.BlockSpec((tb, PEER_SLOTS), lambda i: (i, 0)),
            pl.BlockSpec((tb, d), lambda i: (i, 0)),
            pl.BlockSpec((tb, d), lambda i: (i, 0)),
            pl.BlockSpec((1, 1, d), lambda i: (i * tb // seq_len, 0, 0)),
            pl.BlockSpec((1, d), lambda i: (0, 0)),
            pl.BlockSpec(memory_space=pl.ANY),
        ],
        out_specs=pl.BlockSpec((tb, d), lambda i: (i + first, 0)),
        out_shape=jax.ShapeDtypeStruct((n_out, d), jnp.float32),
        scratch_shapes=(
            [pltpu.VMEM((PEER_SLOTS * n_chunks, LANES), jnp.uint32) for _ in range(EXPERT_SLOTS)]
            + [pltpu.VMEM((tb, d), jnp.float32), pltpu.SemaphoreType.DMA((EXPERT_SLOTS,))]),
        compiler_params=pltpu.CompilerParams(dimension_semantics=("arbitrary",)),
        name="peer_experts",
    )(idx, gates, h2, x1, g2, final_g.reshape(1, d), table)


def _mix_and_route(x, mod, norm1_g, w_in, b_forget, ln_v_g, w_spatial, b_spatial, w_branch_a,
                   w_branch_b, w_out, norm2_g, w_query, sub_keys):
    Bp, S, D = x.shape
    m = Bp * S
    bf16 = jnp.bfloat16
    qkv, cum, sga, gb = _input_projection(x, mod, norm1_g, w_in, b_forget, ln_v_g,
                                          w_spatial, b_spatial, w_branch_b)
    heads = lambda a: a.reshape(Bp, S, ATT_HEADS, ATT_HEAD_DIM).transpose(0, 2, 1, 3)
    q = heads(qkv[:, 0:ATT_WIDTH])
    k = heads(qkv[:, ATT_WIDTH:2 * ATT_WIDTH])
    v = heads(qkv[:, 2 * ATT_WIDTH:])
    cum_h = cum[:, 0:ATT_HEADS].reshape(Bp, S, ATT_HEADS).transpose(0, 2, 1)
    y_a = _fox_attention(q, k, v, cum_h).transpose(0, 2, 1, 3).reshape(m, ATT_WIDTH)
    x1, h2, idx_t, gates_t = _post_attention(
        y_a, sga, gb, x.reshape(m, D), mod, norm2_g.reshape(1, D),
        w_branch_a.astype(bf16), w_out.astype(bf16), w_query.astype(bf16), sub_keys, S)
    return x1, h2, idx_t.T, gates_t.T


def kernel(x, c, w_mod, b_mod, norm1_g, w_in, b_forget, ln_v_g, w_spatial, b_spatial, w_branch_a, w_branch_b, w_out, norm2_g, w_query, sub_keys, expert_u, expert_v, final_g):
    B, S, D = x.shape
    n = B * S
    assert w_mod.shape[0] == 1, "the final RMSNorm is fused into the single layer's expert kernels"
    l = 0
    mod = _modulation(c, w_mod[l], b_mod[l]).reshape(B, 6, D)
    table = _pack_expert_table(expert_u[l], expert_v[l])
    n_chunks = D // LANES
    weights = (norm1_g[l], w_in[l], b_forget[l], ln_v_g[l], w_spatial[l], b_spatial[l],
               w_branch_a[l], w_branch_b[l], w_out[l], norm2_g[l], w_query[l], sub_keys[l])

    b_sc = B * SC_SHARE_PERCENT // 100
    peer_sc = None
    if b_sc > 0:
        x1_a, h2_a, idx_a, gates_a = _mix_and_route(x[:b_sc], mod[:b_sc], *weights)
        m_a = b_sc * S
        peer_sc = _sc_peer_experts(table.reshape(-1, n_chunks, LANES), idx_a.reshape(-1),
                                   gates_a.reshape(-1), h2_a.reshape(m_a, n_chunks, LANES))
    x1_b, h2_b, idx_b, gates_b = _mix_and_route(x[b_sc:], mod[b_sc:], *weights)
    out = _tc_peer_experts(idx_b, gates_b, h2_b, x1_b, mod[b_sc:, 5:6, :], final_g, table, S, n)
    if b_sc > 0:
        out = _finish_tokens(peer_sc.reshape(m_a, D), x1_a, mod[:b_sc, 5:6, :],
                             final_g.reshape(1, D), out, S)
    return out.reshape(B, S, D)
```

```python
import functools

import jax
import jax.numpy as jnp
from jax import lax
from jax.experimental import pallas as pl
from jax.experimental.pallas import tpu as pltpu
from jax.experimental.pallas import tpu_sc as plsc

D_MODEL = 1024
ATT_HEADS = 8
ATT_HEAD_DIM = 64
ATT_WIDTH = ATT_HEADS * ATT_HEAD_DIM
Q_BLOCK = 128
GM_GROUPS = 4
GM_GROUP_DIM = 128
GM_WIDTH = GM_GROUPS * GM_GROUP_DIM
GM_CHUNK = 128
PEER_HEADS = 8
PEER_KEY_DIM = 256
PEER_HALF = PEER_KEY_DIM // 2
N_KEYS = 128
PEER_TOPK = 16
PEER_TOK_BLOCK = 128
PEER_SLOTS = PEER_HEADS * PEER_TOPK
SPLIT_POINTS = (ATT_WIDTH, 2 * ATT_WIDTH, 3 * ATT_WIDTH, 3 * ATT_WIDTH + ATT_HEADS,
                3 * ATT_WIDTH + ATT_HEADS + 2 * GM_WIDTH,
                3 * ATT_WIDTH + ATT_HEADS + 2 * GM_WIDTH + D_MODEL)
EPS = 1e-6

LANES = 128
EXPERT_TOK_BLOCK = 128
EXPERT_SLOTS = 4
DMA_THREADS = 2
FINISH_TOK_BLOCK = 512
SC_GATHER_ROWS = 32
SC_ROW_GROUP = 4
SC_STORE_BATCH = 8
SC_SHARE_PERCENT = 50


def _gelu(x):
    return 0.5 * x * (1.0 + jnp.tanh(0.7978845608028654 * (x + 0.044715 * (x * x * x))))


def _rmsnorm(x, g):
    y = x * lax.rsqrt(jnp.mean(x * x, axis=-1, keepdims=True) + EPS)
    return y * g


def _layernorm(x, g):
    mu = jnp.mean(x, axis=-1, keepdims=True)
    var = jnp.mean(jnp.square(x - mu), axis=-1, keepdims=True)
    return (x - mu) * lax.rsqrt(var + EPS) * g


def _modulate(h, shift, scale):
    return h * (1.0 + scale[:, None, :]) + shift[:, None, :]


def _fox(q, k, v, log_f):
    B, S, H, Dh = q.shape
    nb = S // Q_BLOCK
    cum = jnp.cumsum(log_f, axis=1)
    cum_k = jnp.transpose(cum, (0, 2, 1))
    k_pos = jnp.arange(S)
    q_blocks = q.reshape(B, nb, Q_BLOCK, H, Dh).transpose(1, 0, 2, 3, 4)
    cq_blocks = cum_k.reshape(B, H, nb, Q_BLOCK).transpose(2, 0, 1, 3)
    starts = jnp.arange(nb) * Q_BLOCK
    scale = ATT_HEAD_DIM ** -0.5

    def one_block(args):
        q_blk, cq_blk, start = args
        s = jnp.einsum('bqhd,bkhd->bhqk', q_blk, k) * scale
        s = s + cq_blk[..., :, None] - cum_k[..., None, :]
        q_pos = start + jnp.arange(Q_BLOCK)
        causal = q_pos[:, None] >= k_pos[None, :]
        s = jnp.where(causal[None, None], s, -jnp.inf)
        p = jax.nn.softmax(s, axis=-1)
        return jnp.einsum('bhqk,bkhd->bqhd', p, v)

    out = lax.map(one_block, (q_blocks, cq_blocks, starts))
    return out.transpose(1, 0, 2, 3, 4).reshape(B, S, H * Dh)


def _gmlp(z, ln_g, w_s, b_s):
    B, S, _ = z.shape
    u, v = jnp.split(z, 2, axis=-1)
    v = _layernorm(v, ln_g)
    nc = S // GM_CHUNK
    v = v.reshape(B, nc, GM_CHUNK, GM_GROUPS, GM_GROUP_DIM)
    tril = jnp.tril(jnp.ones((GM_CHUNK, GM_CHUNK), dtype=bool))
    w = jnp.where(tril[None], w_s, jnp.zeros_like(w_s))
    mixed = jnp.einsum('gts,bcsgd->bctgd', w, v) + b_s.T[None, None, :, :, None]
    return u * mixed.reshape(B, S, GM_WIDTH)


def _peer_route(hf, w_query, sub_keys):
    n = hf.shape[0]
    q = (hf @ w_query).reshape(n, PEER_HEADS, 2, PEER_HALF)
    scores = jnp.einsum('nhpd,hpkd->nhpk', q, sub_keys)
    s_top, i_top = lax.top_k(scores, PEER_TOPK)
    cand_s = (s_top[:, :, 0, :, None] + s_top[:, :, 1, None, :]).reshape(n, PEER_HEADS, PEER_TOPK * PEER_TOPK)
    cand_i = (i_top[:, :, 0, :, None] * N_KEYS + i_top[:, :, 1, None, :]).reshape(n, PEER_HEADS, PEER_TOPK * PEER_TOPK)
    best_s, best_pos = lax.top_k(cand_s, PEER_TOPK)
    idx = jnp.take_along_axis(cand_i, best_pos, axis=-1)
    gates = jax.nn.softmax(best_s, axis=-1)
    return idx.reshape(n, PEER_SLOTS), gates.reshape(n, PEER_SLOTS)


def _mod_kernel(c_ref, w_ref, b_ref, o_ref):
    c = c_ref[...]
    sc = c * jax.nn.sigmoid(c)
    o_ref[...] = jnp.dot(sc, w_ref[...], precision=lax.Precision.HIGHEST,
                         preferred_element_type=jnp.float32) + b_ref[...]


def _modulation(c, w_mod, b_mod):
    b, d = c.shape
    cols = w_mod.shape[1]
    return pl.pallas_call(
        _mod_kernel,
        grid=(cols // d,),
        in_specs=[pl.BlockSpec((b, d), lambda j: (0, 0)),
                  pl.BlockSpec((d, d), lambda j: (0, j)),
                  pl.BlockSpec((1, d), lambda j: (0, j))],
        out_specs=pl.BlockSpec((b, d), lambda j: (0, j)),
        out_shape=jax.ShapeDtypeStruct((b, cols), jnp.float32),
        name="modulation",
    )(c, w_mod, b_mod.reshape(1, cols))


INPROJ_TOK_BLOCK = 256


def _inproj_kernel(x_ref, mod_ref, n1g_ref, wqkv_ref, wf_ref, bf_ref, wz_ref, wg_ref, lng_ref,
                   wsp_ref, bsp_ref, wb_ref, qkv_ref, cum_ref, sga_ref, gb_ref, carry):
    f32, bf16 = jnp.float32, jnp.bfloat16
    tm, d = x_ref.shape[1], x_ref.shape[2]
    x = x_ref[0]
    sh1 = mod_ref[0, 0:1, :]
    sc1 = mod_ref[0, 1:2, :]
    h = x * lax.rsqrt(jnp.mean(x * x, axis=-1, keepdims=True) + EPS) * n1g_ref[...]
    hb = (h * (1.0 + sc1) + sh1).astype(bf16)

    qkv = jnp.dot(hb, wqkv_ref[...], preferred_element_type=f32)
    qkv_ref[:, 0:ATT_WIDTH] = (qkv[:, 0:ATT_WIDTH] * (ATT_HEAD_DIM ** -0.5)).astype(bf16)
    qkv_ref[:, ATT_WIDTH:] = qkv[:, ATT_WIDTH:].astype(bf16)

    f = jnp.dot(hb, wf_ref[...], preferred_element_type=f32) + bf_ref[...]
    logf = jnp.minimum(f, 0.0) - jnp.log1p(jnp.exp(-jnp.abs(f)))

    @pl.when(pl.program_id(1) == 0)
    def _():
        carry[...] = jnp.zeros_like(carry)

    tri = (lax.broadcasted_iota(jnp.int32, (tm, tm), 0)
           >= lax.broadcasted_iota(jnp.int32, (tm, tm), 1)).astype(f32)
    cum = jnp.dot(tri, logf, precision=lax.Precision.HIGHEST, preferred_element_type=f32) + carry[...]
    cum_ref[...] = cum
    carry[...] = cum[tm - 1:tm, :]

    gz = _gelu(jnp.dot(hb, wz_ref[...], preferred_element_type=f32))
    u = gz[:, 0:GM_WIDTH]
    v = gz[:, GM_WIDTH:]
    mu = jnp.mean(v, axis=-1, keepdims=True)
    var = jnp.mean(jnp.square(v - mu), axis=-1, keepdims=True)
    vn = ((v - mu) * lax.rsqrt(var + EPS) * lng_ref[...]).astype(bf16)
    tril = (lax.broadcasted_iota(jnp.int32, (GM_CHUNK, GM_CHUNK), 0)
            >= lax.broadcasted_iota(jnp.int32, (GM_CHUNK, GM_CHUNK), 1))
    w_sp = [jnp.where(tril, wsp_ref[g], 0.0).astype(bf16) for g in range(GM_GROUPS)]
    rows = []
    for ck in range(tm // GM_CHUNK):
        r0 = ck * GM_CHUNK
        cols = []
        for g in range(GM_GROUPS):
            c0 = g * GM_GROUP_DIM
            mixed = jnp.dot(w_sp[g], vn[r0:r0 + GM_CHUNK, c0:c0 + GM_GROUP_DIM],
                            preferred_element_type=f32) + bsp_ref[g]
            cols.append(u[r0:r0 + GM_CHUNK, c0:c0 + GM_GROUP_DIM] * mixed)
        rows.append(jnp.concatenate(cols, axis=1))
    yb = jnp.concatenate(rows, axis=0).astype(bf16)
    ybp = jnp.dot(yb, wb_ref[...], preferred_element_type=f32)

    sg = jax.nn.sigmoid(jnp.dot(hb, wg_ref[...], preferred_element_type=f32))
    sga_ref[...] = sg[:, 0:d].astype(bf16)
    gb_ref[...] = (sg[:, d:] * ybp).astype(bf16)


def _input_projection(x, mod, norm1_g, w_in, b_forget, ln_v_g, w_spatial, b_spatial, w_branch_b):
    B, S, d = x.shape
    n = B * S
    tm = INPROJ_TOK_BLOCK
    bf16 = jnp.bfloat16
    p0, p1, p2, p3, p4, p5 = SPLIT_POINTS
    w_qkv = w_in[:, 0:p2].astype(bf16)
    w_f = jnp.pad(w_in[:, p2:p3], ((0, 0), (0, LANES - ATT_HEADS))).astype(bf16)
    b_f = jnp.pad(b_forget, (0, LANES - ATT_HEADS)).reshape(1, LANES)
    w_z = w_in[:, p3:p4].astype(bf16)
    w_g = w_in[:, p4:].astype(bf16)
    nt = S // tm
    tok = lambda w: pl.BlockSpec((tm, w), lambda b, i: (b * nt + i, 0))
    full = lambda a: pl.BlockSpec(a.shape, lambda b, i: (0,) * a.ndim)
    args = (x, mod, norm1_g.reshape(1, d), w_qkv, w_f, b_f, w_z, w_g, ln_v_g.reshape(1, GM_WIDTH),
            w_spatial, b_spatial.reshape(GM_GROUPS, GM_CHUNK, 1), w_branch_b.astype(bf16))
    return pl.pallas_call(
        _inproj_kernel,
        grid=(B, nt),
        in_specs=[pl.BlockSpec((1, tm, d), lambda b, i: (b, i, 0)),
                  pl.BlockSpec((1, 6, d), lambda b, i: (b, 0, 0))] + [full(a) for a in args[2:]],
        out_specs=[tok(3 * ATT_WIDTH), tok(LANES), tok(d), tok(d)],
        out_shape=[jax.ShapeDtypeStruct((n, 3 * ATT_WIDTH), bf16),
                   jax.ShapeDtypeStruct((n, LANES), jnp.float32),
                   jax.ShapeDtypeStruct((n, d), bf16),
                   jax.ShapeDtypeStruct((n, d), bf16)],
        scratch_shapes=[pltpu.VMEM((1, LANES), jnp.float32)],
        compiler_params=pltpu.CompilerParams(dimension_semantics=("arbitrary", "arbitrary"),
                                             vmem_limit_bytes=56 * 1024 * 1024),
        name="input_projection",
    )(*args)


ATT_BLOCK = 512


def _fox_kernel(q_ref, k_ref, v_ref, cq_ref, ck_ref, o_ref, *, blk):
    f32 = jnp.float32
    i = pl.program_id(2)
    q = q_ref[0, 0]
    cq = cq_ref[0, 0]

    def block(j, carry, masked):
        m, l, acc = carry
        off = pl.multiple_of(j * blk, blk)
        k = k_ref[0, 0, pl.ds(off, blk), :]
        v = v_ref[0, 0, pl.ds(off, blk), :]
        s = lax.dot_general(q, k, (((1,), (1,)), ((), ())), preferred_element_type=f32)
        s = s + (cq - ck_ref[0, 0, j])
        if masked:
            causal = (lax.broadcasted_iota(jnp.int32, (blk, blk), 0)
                      >= lax.broadcasted_iota(jnp.int32, (blk, blk), 1))
            s = jnp.where(causal, s, -jnp.inf)
        m_new = jnp.maximum(m, jnp.max(s, axis=1, keepdims=True))
        alpha = jnp.exp(m - m_new)
        p = jnp.exp(s - m_new)
        l = alpha * l + jnp.sum(p, axis=1, keepdims=True)
        acc = alpha * acc + jnp.dot(p.astype(v.dtype), v, preferred_element_type=f32)
        return m_new, l, acc

    init = (jnp.full((blk, 1), -1e30, f32), jnp.zeros((blk, 1), f32),
            jnp.zeros((blk, q.shape[1]), f32))
    carry = lax.fori_loop(0, i, lambda j, c: block(j, c, False), init)
    m, l, acc = block(i, carry, True)
    o_ref[0, 0] = (acc / l).astype(o_ref.dtype)


def _fox_attention(q, k, v, cum):
    B, H, S, dh = q.shape
    blk = min(ATT_BLOCK, S)
    nb = S // blk
    cq = cum.reshape(B, H, S, 1)
    ck = cum.reshape(B, H, nb, 1, blk)
    return pl.pallas_call(
        functools.partial(_fox_kernel, blk=blk),
        grid=(B, H, nb),
        in_specs=[pl.BlockSpec((1, 1, blk, dh), lambda b, h, i: (b, h, i, 0)),
                  pl.BlockSpec((1, 1, S, dh), lambda b, h, i: (b, h, 0, 0)),
                  pl.BlockSpec((1, 1, S, dh), lambda b, h, i: (b, h, 0, 0)),
                  pl.BlockSpec((1, 1, blk, 1), lambda b, h, i: (b, h, i, 0)),
                  pl.BlockSpec((1, 1, nb, 1, blk), lambda b, h, i: (b, h, 0, 0, 0))],
        out_specs=pl.BlockSpec((1, 1, blk, dh), lambda b, h, i: (b, h, i, 0)),
        out_shape=jax.ShapeDtypeStruct((B, H, S, dh), jnp.bfloat16),
        compiler_params=pltpu.CompilerParams(
            dimension_semantics=("arbitrary", "arbitrary", "arbitrary")),
        name="fox_attention",
    )(q, k, v, cq, ck)


def _topk_rows(s, k):
    rows, t = s.shape
    iota = lax.broadcasted_iota(jnp.int32, (rows, t), 0)
    slot = lax.broadcasted_iota(jnp.int32, (k, t), 0)
    vals = jnp.zeros((k, t), jnp.float32)
    ids = jnp.zeros((k, t), jnp.int32)
    for j in range(k):
        m = jnp.max(s, axis=0, keepdims=True)
        am = jnp.min(jnp.where(s == m, iota, rows), axis=0, keepdims=True)
        vals = jnp.where(slot == j, m, vals)
        ids = jnp.where(slot == j, am, ids)
        s = jnp.where(iota == am, -jnp.inf, s)
    return vals, ids


def _select_rows(table, pos):
    out = jnp.zeros(pos.shape, table.dtype)
    for r in range(table.shape[0]):
        out = jnp.where(pos == r, table[r:r + 1, :], out)
    return out


def _post_kernel(ya_ref, sga_ref, gb_ref, x_ref, mod_ref, n2g_ref, wa_ref, wo_ref, wq_ref, keys_ref,
                 x1_ref, h2_ref, idx_ref, gates_ref, q_scr):
    f32 = jnp.float32
    a = jnp.dot(ya_ref[...], wa_ref[...], preferred_element_type=f32)
    merged = sga_ref[...].astype(f32) * a + gb_ref[...].astype(f32)
    o = jnp.dot(merged.astype(jnp.bfloat16), wo_ref[...], preferred_element_type=f32)
    g1 = mod_ref[0, 2:3, :]
    sh2 = mod_ref[0, 3:4, :]
    sc2 = mod_ref[0, 4:5, :]
    x1 = x_ref[...] + g1 * o
    x1_ref[...] = x1
    h2 = x1 * lax.rsqrt(jnp.mean(x1 * x1, axis=-1, keepdims=True) + EPS) * n2g_ref[...]
    h2 = h2 * (1.0 + sc2) + sh2
    h2_ref[...] = h2
    qp = jnp.dot(h2.astype(jnp.bfloat16), wq_ref[...], preferred_element_type=f32)
    for j in range(2 * PEER_HEADS):
        q_scr[j] = qp[:, j * PEER_HALF:(j + 1) * PEER_HALF]

    def head(h, carry):
        tops = []
        for p in range(2):
            q = q_scr[2 * h + p]
            keys = keys_ref[2 * h + p]
            sc = lax.dot_general(keys, q, (((1,), (1,)), ((), ())),
                                 precision=lax.Precision.HIGHEST,
                                 preferred_element_type=f32)
            tops.append(_topk_rows(sc, PEER_TOPK))
        (s1, i1), (s2, i2) = tops
        half = PEER_TOPK // 2
        blocks = [s1[0:1, :] + s2]
        blocks += [s1[a:a + 1, :] + s2[0:half, :] for a in range(1, half)]
        blocks += [s1[half:, :] + s2[0:1, :]]
        vals, pos = _topk_rows(jnp.concatenate(blocks, axis=0), PEER_TOPK)
        mid = pos - PEER_TOPK
        tail0 = PEER_TOPK + (half - 1) * half
        ra = jnp.where(pos < PEER_TOPK, 0,
                       jnp.where(pos < tail0, 1 + (mid >> (half.bit_length() - 1)), pos - tail0 + half))
        rb = jnp.where(pos < PEER_TOPK, pos, jnp.where(pos < tail0, mid & (half - 1), 0))
        eid = _select_rows(i1, ra) * N_KEYS + _select_rows(i2, rb)
        e = jnp.exp(vals - vals[0:1, :])
        g = e / jnp.sum(e, axis=0, keepdims=True)
        row = pl.multiple_of(h * PEER_TOPK, PEER_TOPK)
        idx_ref[pl.ds(row, PEER_TOPK), :] = eid
        gates_ref[pl.ds(row, PEER_TOPK), :] = g
        return carry

    lax.fori_loop(0, PEER_HEADS, head, 0)


POST_TOK_BLOCK = 256


def _post_attention(ya, sga, gb, x, mod, norm2_g, w_a, w_out, w_query, sub_keys, seq_len):
    n, d = x.shape
    tm = POST_TOK_BLOCK
    blocks_per_seq = seq_len // tm
    aw = ya.shape[1]
    qw = w_query.shape[1]
    keys = sub_keys.reshape(2 * PEER_HEADS, N_KEYS, PEER_HALF)
    tok = lambda w: pl.BlockSpec((tm, w), lambda i: (i, 0))
    full = lambda a: pl.BlockSpec(a.shape, lambda i: (0,) * a.ndim)
    return pl.pallas_call(
        _post_kernel,
        grid=(n // tm,),
        in_specs=[tok(aw), tok(d), tok(d), tok(d),
                  pl.BlockSpec((1, 6, d), lambda i: (i // blocks_per_seq, 0, 0)),
                  full(norm2_g), full(w_a), full(w_out), full(w_query), full(keys)],
        out_specs=[tok(d), tok(d),
                   pl.BlockSpec((PEER_SLOTS, tm), lambda i: (0, i)),
                   pl.BlockSpec((PEER_SLOTS, tm), lambda i: (0, i))],
        out_shape=[jax.ShapeDtypeStruct((n, d), jnp.float32),
                   jax.ShapeDtypeStruct((n, d), jnp.float32),
                   jax.ShapeDtypeStruct((PEER_SLOTS, n), jnp.int32),
                   jax.ShapeDtypeStruct((PEER_SLOTS, n), jnp.float32)],
        scratch_shapes=[pltpu.VMEM((2 * PEER_HEADS, tm, PEER_HALF), jnp.float32)],
        compiler_params=pltpu.CompilerParams(dimension_semantics=("arbitrary",),
                                             vmem_limit_bytes=48 * 1024 * 1024),
        name="post_attention",
    )(ya, sga, gb, x, mod, norm2_g, w_a, w_out, w_query, keys)


def _rowsum_bcast(p, ones_bf16):
    hi = p.astype(jnp.bfloat16)
    lo = (p - hi.astype(jnp.float32)).astype(jnp.bfloat16)
    return (jnp.dot(hi, ones_bf16, preferred_element_type=jnp.float32)
            + jnp.dot(lo, ones_bf16, preferred_element_type=jnp.float32))


def _eval_experts(chunk, hrow, grow):
    n_chunks = hrow.shape[1] // LANES
    ones_bf16 = jnp.ones((LANES, LANES), jnp.bfloat16)
    eye = (lax.broadcasted_iota(jnp.int32, (PEER_SLOTS, LANES), 0)
           == lax.broadcasted_iota(jnp.int32, (PEER_SLOTS, LANES), 1))
    psum = None
    for c in range(n_chunks):
        u = lax.bitcast_convert_type(chunk(c) & jnp.uint32(0xFFFF0000), jnp.float32)
        p = u * hrow[:, c * LANES:(c + 1) * LANES]
        psum = p if psum is None else psum + p
    act = _gelu(_rowsum_bcast(psum, ones_bf16))
    gcol = _rowsum_bcast(jnp.where(eye, grow, 0.0), ones_bf16)
    coef = gcol * act
    outs = []
    for c in range(n_chunks):
        v = lax.bitcast_convert_type(chunk(c) << 16, jnp.float32)
        outs.append(jnp.sum(coef * v, axis=0, keepdims=True))
    return jnp.concatenate(outs, axis=-1)


def _finish_block(x_ref, g2_ref, fg_ref, peer, out_ref):
    y = x_ref[...] + g2_ref[0] * peer[...]
    out_ref[...] = y * lax.rsqrt(jnp.mean(y * y, axis=-1, keepdims=True) + EPS) * fg_ref[...]


def _expert_kernel(idx_ref, gates_ref, h_ref, x_ref, g2_ref, fg_ref, tab_ref, out_ref,
                   *scratch, tok_block, n_slots):
    bufs = scratch[:n_slots]
    peer, sem = scratch[n_slots], scratch[n_slots + 1]
    d_model = h_ref.shape[-1]
    n_chunks = d_model // LANES
    rows_per_tok = PEER_SLOTS * n_chunks

    def issue(t, s):
        for r in range(PEER_SLOTS):
            row = pl.multiple_of(idx_ref[t, r] * n_chunks, n_chunks)
            pltpu.make_async_copy(tab_ref.at[pl.ds(row, n_chunks), :],
                                  bufs[s].at[pl.ds(r * n_chunks, n_chunks), :],
                                  sem.at[s]).start(priority=r % DMA_THREADS)

    def wait(s):
        pltpu.make_async_copy(tab_ref.at[pl.ds(0, rows_per_tok), :], bufs[s], sem.at[s]).wait()

    def compute(t, s):
        chunk = lambda c: bufs[s][pl.ds(c, PEER_SLOTS, stride=n_chunks), :]
        peer[pl.ds(t, 1), :] = _eval_experts(chunk, h_ref[pl.ds(t, 1), :], gates_ref[pl.ds(t, 1), :])

    def step(t, s, prefetch):
        wait(s)
        if prefetch:
            issue(t + n_slots - 1, (s - 1) % n_slots)
        compute(t, s)

    for t in range(n_slots - 1):
        issue(t, t)

    n_groups = tok_block // n_slots

    def group(g, carry):
        for s in range(n_slots):
            step(g * n_slots + s, s, True)
        return carry

    lax.fori_loop(0, n_groups - 1, group, 0)
    for s in range(n_slots):
        t = (n_groups - 1) * n_slots + s
        step(t, s, t + n_slots - 1 < tok_block)

    _finish_block(x_ref, g2_ref, fg_ref, peer, out_ref)


def _expert_cost(n_tok, d):
    pairs = n_tok * PEER_SLOTS
    return pl.CostEstimate(flops=4 * pairs * d, transcendentals=pairs,
                           bytes_accessed=4 * pairs * d + 12 * n_tok * d + 8 * pairs)


def _sc_tanh(y):
    return 1.0 - 2.0 / (jnp.exp(2.0 * y) + 1.0)


def _sc_peer_experts(table3, ids, gates, h3):
    n_tok, n_chunks, lanes = h3.shape
    info = plsc.get_sparse_core_info()
    sl = info.num_lanes
    n_workers = info.num_cores * info.num_subcores
    tok_per_worker = n_tok // n_workers
    assert tok_per_worker * n_workers == n_tok
    rows = SC_GATHER_ROWS
    n_sub = PEER_SLOTS // rows
    pieces = [(c, k * sl) for c in range(n_chunks) for k in range(lanes // sl)]
    mesh = plsc.VectorSubcoreMesh(core_axis_name="c", subcore_axis_name="s")
    buf = lambda dt: pltpu.VMEM((rows, n_chunks, lanes), dt)

    @functools.partial(
        pl.kernel, mesh=mesh,
        out_type=jax.ShapeDtypeStruct((n_tok, n_chunks, lanes), jnp.float32),
        scratch_types=[pltpu.VMEM((PEER_SLOTS,), jnp.int32), pltpu.VMEM((PEER_SLOTS,), jnp.float32),
                       pltpu.VMEM((n_chunks, lanes), jnp.float32), buf(jnp.uint32), buf(jnp.uint32),
                       pltpu.VMEM((n_chunks, lanes), jnp.float32), pltpu.SemaphoreType.DMA((2,))],
        compiler_params=pltpu.CompilerParams(needs_layout_passes=False),
        cost_estimate=_expert_cost(n_tok, n_chunks * lanes),
        name="sc_peer_experts")
    def run(table_hbm, ids_hbm, gates_hbm, h_hbm, peer_hbm, ids_v, g_v, h_v, rows0, rows1, out_v, sem):
        bufs = (rows0, rows1)
        wid = lax.axis_index("s") * info.num_cores + lax.axis_index("c")

        def gather(s):
            return pltpu.make_async_copy(table_hbm.at[ids_v.at[pl.ds(s * rows, rows)]],
                                         bufs[s % 2], sem.at[s % 2])

        @pl.loop(0, tok_per_worker)
        def _(ti):
            t = wid * tok_per_worker + ti
            off = pl.multiple_of(t * PEER_SLOTS, PEER_SLOTS)
            pltpu.sync_copy(ids_hbm.at[pl.ds(off, PEER_SLOTS)], ids_v)
            pltpu.sync_copy(gates_hbm.at[pl.ds(off, PEER_SLOTS)], g_v)
            pltpu.sync_copy(h_hbm.at[t], h_v)
            gather(0).start()
            for c, k in pieces:
                out_v[c, pl.ds(k, sl)] = jnp.zeros((sl,), jnp.float32)
            for s in range(n_sub):
                gather(s).wait()
                if s + 1 < n_sub:
                    gather(s + 1).start()
                rbuf = bufs[s % 2]

                @pl.loop(0, rows, step=SC_ROW_GROUP)
                def _(r0):
                    accs = [jnp.zeros((sl,), jnp.float32) for _ in range(SC_ROW_GROUP)]
                    for c, k in pieces:
                        hv = h_v[c, pl.ds(k, sl)]
                        for j in range(SC_ROW_GROUP):
                            w = rbuf[r0 + j, c, pl.ds(k, sl)]
                            u = lax.bitcast_convert_type(w & jnp.uint32(0xFFFF0000), jnp.float32)
                            accs[j] = accs[j] + u * hv
                    coefs = []
                    for j in range(SC_ROW_GROUP):
                        a = jnp.broadcast_to(jnp.sum(accs[j]), (sl,))
                        act = 0.5 * a * (1.0 + _sc_tanh(0.7978845608028654 * (a + 0.044715 * (a * a * a))))
                        gate = plsc.load_gather(g_v, [jnp.broadcast_to(s * rows + r0 + j, (sl,))])
                        coefs.append(gate * act)
                    for b0 in range(0, len(pieces), SC_STORE_BATCH):
                        tots = []
                        for c, k in pieces[b0:b0 + SC_STORE_BATCH]:
                            tot = None
                            for j in range(SC_ROW_GROUP):
                                w = rbuf[r0 + j, c, pl.ds(k, sl)]
                                term = coefs[j] * lax.bitcast_convert_type(w << 16, jnp.float32)
                                tot = term if tot is None else tot + term
                            tots.append(tot)
                        for (c, k), tot in zip(pieces[b0:b0 + SC_STORE_BATCH], tots):
                            plsc.addupdate(out_v.at[c, pl.ds(k, sl)], tot)

            pltpu.sync_copy(out_v, peer_hbm.at[t])

    return run(table3, ids, gates, h3)


def _finish_kernel(peer_ref, x_ref, g2_ref, fg_ref, prev_ref, out_ref):
    del prev_ref
    _finish_block(x_ref, g2_ref, fg_ref, peer_ref, out_ref)


def _finish_tokens(peer, x1, g2, fg, prev, seq_len):
    d = x1.shape[1]
    tm = FINISH_TOK_BLOCK
    return pl.pallas_call(
        _finish_kernel,
        grid=(peer.shape[0] // tm,),
        in_specs=[pl.BlockSpec((tm, d), lambda i: (i, 0)),
                  pl.BlockSpec((tm, d), lambda i: (i, 0)),
                  pl.BlockSpec((1, 1, d), lambda i: (i * tm // seq_len, 0, 0)),
                  pl.BlockSpec((1, d), lambda i: (0, 0)),
                  pl.BlockSpec(memory_space=pl.ANY)],
        out_specs=pl.BlockSpec((tm, d), lambda i: (i, 0)),
        out_shape=jax.ShapeDtypeStruct(prev.shape, prev.dtype),
        input_output_aliases={4: 0},
        compiler_params=pltpu.CompilerParams(dimension_semantics=("arbitrary",)),
        name="finish_tokens",
    )(peer, x1, g2, fg, prev)


def _pack_expert_table(expert_u, expert_v):
    ub = lax.bitcast_convert_type(expert_u.astype(jnp.bfloat16), jnp.uint16).astype(jnp.uint32)
    vb = lax.bitcast_convert_type(expert_v.astype(jnp.bfloat16), jnp.uint16).astype(jnp.uint32)
    packed = (ub << 16) | vb
    n_experts, d = packed.shape
    return packed.reshape(n_experts * (d // LANES), LANES)


def _tc_peer_experts(idx, gates, h2, x1, g2, final_g, table, seq_len, n_out):
    m, d = h2.shape
    n_chunks = d // LANES
    tb = EXPERT_TOK_BLOCK
    assert tb % EXPERT_SLOTS == 0 and seq_len % tb == 0 and (n_out - m) % tb == 0
    first = (n_out - m) // tb
    return pl.pallas_call(
        functools.partial(_expert_kernel, tok_block=tb, n_slots=EXPERT_SLOTS),
        grid=(m // tb,),
        in_specs=[
            pl.BlockSpec((tb, PEER_SLOTS), lambda i: (i, 0), memory_space=pltpu.SMEM),
            pl.BlockSpec((tb, PEER_SLOTS), lambda i: (i, 0)),
            pl.BlockSpec((tb, d), lambda i: (i, 0)),
            pl.BlockSpec((tb, d), lambda i: (i, 0)),
            pl.BlockSpec((1, 1, d), lambda i: (i * tb // seq_len, 0, 0)),
            pl.BlockSpec((1, d), lambda i: (0, 0)),
            pl.BlockSpec(memory_space=pl.ANY),
        ],
        out_specs=pl.BlockSpec((tb, d), lambda i: (i + first, 0)),
        out_shape=jax.ShapeDtypeStruct((n_out, d), jnp.float32),
        scratch_shapes=(
            [pltpu.VMEM((PEER_SLOTS * n_chunks, LANES), jnp.uint32) for _ in range(EXPERT_SLOTS)]
            + [pltpu.VMEM((tb, d), jnp.float32), pltpu.SemaphoreType.DMA((EXPERT_SLOTS,))]),
        compiler_params=pltpu.CompilerParams(dimension_semantics=("arbitrary",)),
        cost_estimate=_expert_cost(m, d),
        name="peer_experts",
    )(idx, gates, h2, x1, g2, final_g.reshape(1, d), table)


def _mix_and_route(x, mod, norm1_g, w_in, b_forget, ln_v_g, w_spatial, b_spatial, w_branch_a,
                   w_branch_b, w_out, norm2_g, w_query, sub_keys):
    Bp, S, D = x.shape
    m = Bp * S
    bf16 = jnp.bfloat16
    qkv, cum, sga, gb = _input_projection(x, mod, norm1_g, w_in, b_forget, ln_v_g,
                                          w_spatial, b_spatial, w_branch_b)
    heads = lambda a: a.reshape(Bp, S, ATT_HEADS, ATT_HEAD_DIM).transpose(0, 2, 1, 3)
    q = heads(qkv[:, 0:ATT_WIDTH])
    k = heads(qkv[:, ATT_WIDTH:2 * ATT_WIDTH])
    v = heads(qkv[:, 2 * ATT_WIDTH:])
    cum_h = cum[:, 0:ATT_HEADS].reshape(Bp, S, ATT_HEADS).transpose(0, 2, 1)
    y_a = _fox_attention(q, k, v, cum_h).transpose(0, 2, 1, 3).reshape(m, ATT_WIDTH)
    x1, h2, idx_t, gates_t = _post_attention(
        y_a, sga, gb, x.reshape(m, D), mod, norm2_g.reshape(1, D),
        w_branch_a.astype(bf16), w_out.astype(bf16), w_query.astype(bf16), sub_keys, S)
    return x1, h2, idx_t.T, gates_t.T


def kernel(x, c, w_mod, b_mod, norm1_g, w_in, b_forget, ln_v_g, w_spatial, b_spatial, w_branch_a, w_branch_b, w_out, norm2_g, w_query, sub_keys, expert_u, expert_v, final_g):
    B, S, D = x.shape
    n = B * S
    assert w_mod.shape[0] == 1, "the final RMSNorm is fused into the single layer's expert kernels"
    l = 0
    mod = _modulation(c, w_mod[l], b_mod[l]).reshape(B, 6, D)
    table = _pack_expert_table(expert_u[l], expert_v[l])
    n_chunks = D // LANES
    weights = (norm1_g[l], w_in[l], b_forget[l], ln_v_g[l], w_spatial[l], b_spatial[l],
               w_branch_a[l], w_branch_b[l], w_out[l], norm2_g[l], w_query[l], sub_keys[l])

    b_sc = B * SC_SHARE_PERCENT // 100
    peer_sc = None
    if b_sc > 0:
        x1_a, h2_a, idx_a, gates_a = _mix_and_route(x[:b_sc], mod[:b_sc], *weights)
        m_a = b_sc * S
        peer_sc = _sc_peer_experts(table.reshape(-1, n_chunks, LANES), idx_a.reshape(-1),
                                   gates_a.reshape(-1), h2_a.reshape(m_a, n_chunks, LANES))
    x1_b, h2_b, idx_b, gates_b = _mix_and_route(x[b_sc:], mod[b_sc:], *weights)
    out = _tc_peer_experts(idx_b, gates_b, h2_b, x1_b, mod[b_sc:, 5:6, :], final_g, table, S, n)
    if b_sc > 0:
        out = _finish_tokens(peer_sc.reshape(m_a, D), x1_a, mod[:b_sc, 5:6, :],
                             final_g.reshape(1, D), out, S)
    return out.reshape(B, S, D)
```

```python
import functools

import jax
import jax.numpy as jnp
from jax import lax
from jax.experimental import pallas as pl
from jax.experimental.pallas import tpu as pltpu
from jax.experimental.pallas import tpu_sc as plsc

D_MODEL = 1024
ATT_HEADS = 8
ATT_HEAD_DIM = 64
ATT_WIDTH = ATT_HEADS * ATT_HEAD_DIM
Q_BLOCK = 128
GM_GROUPS = 4
GM_GROUP_DIM = 128
GM_WIDTH = GM_GROUPS * GM_GROUP_DIM
GM_CHUNK = 128
PEER_HEADS = 8
PEER_KEY_DIM = 256
PEER_HALF = PEER_KEY_DIM // 2
N_KEYS = 128
PEER_TOPK = 16
PEER_TOK_BLOCK = 128
PEER_SLOTS = PEER_HEADS * PEER_TOPK
SPLIT_POINTS = (ATT_WIDTH, 2 * ATT_WIDTH, 3 * ATT_WIDTH, 3 * ATT_WIDTH + ATT_HEADS,
                3 * ATT_WIDTH + ATT_HEADS + 2 * GM_WIDTH,
                3 * ATT_WIDTH + ATT_HEADS + 2 * GM_WIDTH + D_MODEL)
EPS = 1e-6

LANES = 128
EXPERT_TOK_BLOCK = 128
EXPERT_SLOTS = 4
DMA_THREADS = 2
FINISH_TOK_BLOCK = 512
SC_GATHER_ROWS = 32
SC_ROW_GROUP = 4
SC_STORE_BATCH = 8
SC_SHARE_PERCENT = 50


def _gelu(x):
    return 0.5 * x * (1.0 + jnp.tanh(0.7978845608028654 * (x + 0.044715 * (x * x * x))))


def _rmsnorm(x, g):
    y = x * lax.rsqrt(jnp.mean(x * x, axis=-1, keepdims=True) + EPS)
    return y * g


def _layernorm(x, g):
    mu = jnp.mean(x, axis=-1, keepdims=True)
    var = jnp.mean(jnp.square(x - mu), axis=-1, keepdims=True)
    return (x - mu) * lax.rsqrt(var + EPS) * g


def _modulate(h, shift, scale):
    return h * (1.0 + scale[:, None, :]) + shift[:, None, :]


def _fox(q, k, v, log_f):
    B, S, H, Dh = q.shape
    nb = S // Q_BLOCK
    cum = jnp.cumsum(log_f, axis=1)
    cum_k = jnp.transpose(cum, (0, 2, 1))
    k_pos = jnp.arange(S)
    q_blocks = q.reshape(B, nb, Q_BLOCK, H, Dh).transpose(1, 0, 2, 3, 4)
    cq_blocks = cum_k.reshape(B, H, nb, Q_BLOCK).transpose(2, 0, 1, 3)
    starts = jnp.arange(nb) * Q_BLOCK
    scale = ATT_HEAD_DIM ** -0.5

    def one_block(args):
        q_blk, cq_blk, start = args
        s = jnp.einsum('bqhd,bkhd->bhqk', q_blk, k) * scale
        s = s + cq_blk[..., :, None] - cum_k[..., None, :]
        q_pos = start + jnp.arange(Q_BLOCK)
        causal = q_pos[:, None] >= k_pos[None, :]
        s = jnp.where(causal[None, None], s, -jnp.inf)
        p = jax.nn.softmax(s, axis=-1)
        return jnp.einsum('bhqk,bkhd->bqhd', p, v)

    out = lax.map(one_block, (q_blocks, cq_blocks, starts))
    return out.transpose(1, 0, 2, 3, 4).reshape(B, S, H * Dh)


def _gmlp(z, ln_g, w_s, b_s):
    B, S, _ = z.shape
    u, v = jnp.split(z, 2, axis=-1)
    v = _layernorm(v, ln_g)
    nc = S // GM_CHUNK
    v = v.reshape(B, nc, GM_CHUNK, GM_GROUPS, GM_GROUP_DIM)
    tril = jnp.tril(jnp.ones((GM_CHUNK, GM_CHUNK), dtype=bool))
    w = jnp.where(tril[None], w_s, jnp.zeros_like(w_s))
    mixed = jnp.einsum('gts,bcsgd->bctgd', w, v) + b_s.T[None, None, :, :, None]
    return u * mixed.reshape(B, S, GM_WIDTH)


def _peer_route(hf, w_query, sub_keys):
    n = hf.shape[0]
    q = (hf @ w_query).reshape(n, PEER_HEADS, 2, PEER_HALF)
    scores = jnp.einsum('nhpd,hpkd->nhpk', q, sub_keys)
    s_top, i_top = lax.top_k(scores, PEER_TOPK)
    cand_s = (s_top[:, :, 0, :, None] + s_top[:, :, 1, None, :]).reshape(n, PEER_HEADS, PEER_TOPK * PEER_TOPK)
    cand_i = (i_top[:, :, 0, :, None] * N_KEYS + i_top[:, :, 1, None, :]).reshape(n, PEER_HEADS, PEER_TOPK * PEER_TOPK)
    best_s, best_pos = lax.top_k(cand_s, PEER_TOPK)
    idx = jnp.take_along_axis(cand_i, best_pos, axis=-1)
    gates = jax.nn.softmax(best_s, axis=-1)
    return idx.reshape(n, PEER_SLOTS), gates.reshape(n, PEER_SLOTS)


def _mod_kernel(c_ref, w_ref, b_ref, o_ref):
    c = c_ref[...]
    sc = c * jax.nn.sigmoid(c)
    o_ref[...] = jnp.dot(sc, w_ref[...], precision=lax.Precision.HIGHEST,
                         preferred_element_type=jnp.float32) + b_ref[...]


def _modulation(c, w_mod, b_mod):
    b, d = c.shape
    cols = w_mod.shape[1]
    return pl.pallas_call(
        _mod_kernel,
        grid=(cols // d,),
        in_specs=[pl.BlockSpec((b, d), lambda j: (0, 0)),
                  pl.BlockSpec((d, d), lambda j: (0, j)),
                  pl.BlockSpec((1, d), lambda j: (0, j))],
        out_specs=pl.BlockSpec((b, d), lambda j: (0, j)),
        out_shape=jax.ShapeDtypeStruct((b, cols), jnp.float32),
        name="modulation",
    )(c, w_mod, b_mod.reshape(1, cols))


INPROJ_TOK_BLOCK = 256


def _inproj_kernel(x_ref, mod_ref, n1g_ref, wqkv_ref, wf_ref, bf_ref, wz_ref, wg_ref, lng_ref,
                   wsp_ref, bsp_ref, wb_ref, qkv_ref, cum_ref, sga_ref, gb_ref, carry):
    f32, bf16 = jnp.float32, jnp.bfloat16
    tm, d = x_ref.shape[1], x_ref.shape[2]
    x = x_ref[0]
    sh1 = mod_ref[0, 0:1, :]
    sc1 = mod_ref[0, 1:2, :]
    h = x * lax.rsqrt(jnp.mean(x * x, axis=-1, keepdims=True) + EPS) * n1g_ref[...]
    hb = (h * (1.0 + sc1) + sh1).astype(bf16)

    qkv = jnp.dot(hb, wqkv_ref[...], preferred_element_type=f32)
    qkv_ref[:, 0:ATT_WIDTH] = (qkv[:, 0:ATT_WIDTH] * (ATT_HEAD_DIM ** -0.5)).astype(bf16)
    qkv_ref[:, ATT_WIDTH:] = qkv[:, ATT_WIDTH:].astype(bf16)

    f = jnp.dot(hb, wf_ref[...], preferred_element_type=f32) + bf_ref[...]
    logf = jnp.minimum(f, 0.0) - jnp.log1p(jnp.exp(-jnp.abs(f)))

    @pl.when(pl.program_id(1) == 0)
    def _():
        carry[...] = jnp.zeros_like(carry)

    tri = (lax.broadcasted_iota(jnp.int32, (tm, tm), 0)
           >= lax.broadcasted_iota(jnp.int32, (tm, tm), 1)).astype(f32)
    cum = jnp.dot(tri, logf, precision=lax.Precision.HIGHEST, preferred_element_type=f32) + carry[...]
    cum_ref[...] = cum
    carry[...] = cum[tm - 1:tm, :]

    gz = _gelu(jnp.dot(hb, wz_ref[...], preferred_element_type=f32))
    u = gz[:, 0:GM_WIDTH]
    v = gz[:, GM_WIDTH:]
    mu = jnp.mean(v, axis=-1, keepdims=True)
    var = jnp.mean(jnp.square(v - mu), axis=-1, keepdims=True)
    vn = ((v - mu) * lax.rsqrt(var + EPS) * lng_ref[...]).astype(bf16)
    tril = (lax.broadcasted_iota(jnp.int32, (GM_CHUNK, GM_CHUNK), 0)
            >= lax.broadcasted_iota(jnp.int32, (GM_CHUNK, GM_CHUNK), 1))
    w_sp = [jnp.where(tril, wsp_ref[g], 0.0).astype(bf16) for g in range(GM_GROUPS)]
    rows = []
    for ck in range(tm // GM_CHUNK):
        r0 = ck * GM_CHUNK
        cols = []
        for g in range(GM_GROUPS):
            c0 = g * GM_GROUP_DIM
            mixed = jnp.dot(w_sp[g], vn[r0:r0 + GM_CHUNK, c0:c0 + GM_GROUP_DIM],
                            preferred_element_type=f32) + bsp_ref[g]
            cols.append(u[r0:r0 + GM_CHUNK, c0:c0 + GM_GROUP_DIM] * mixed)
        rows.append(jnp.concatenate(cols, axis=1))
    yb = jnp.concatenate(rows, axis=0).astype(bf16)
    ybp = jnp.dot(yb, wb_ref[...], preferred_element_type=f32)

    sg = jax.nn.sigmoid(jnp.dot(hb, wg_ref[...], preferred_element_type=f32))
    sga_ref[...] = sg[:, 0:d].astype(bf16)
    gb_ref[...] = (sg[:, d:] * ybp).astype(bf16)


def _input_projection(x, mod, norm1_g, w_in, b_forget, ln_v_g, w_spatial, b_spatial, w_branch_b):
    B, S, d = x.shape
    n = B * S
    tm = INPROJ_TOK_BLOCK
    bf16 = jnp.bfloat16
    p0, p1, p2, p3, p4, p5 = SPLIT_POINTS
    w_qkv = w_in[:, 0:p2].astype(bf16)
    w_f = jnp.pad(w_in[:, p2:p3], ((0, 0), (0, LANES - ATT_HEADS))).astype(bf16)
    b_f = jnp.pad(b_forget, (0, LANES - ATT_HEADS)).reshape(1, LANES)
    w_z = w_in[:, p3:p4].astype(bf16)
    w_g = w_in[:, p4:].astype(bf16)
    nt = S // tm
    tok = lambda w: pl.BlockSpec((tm, w), lambda b, i: (b * nt + i, 0))
    full = lambda a: pl.BlockSpec(a.shape, lambda b, i: (0,) * a.ndim)
    args = (x, mod, norm1_g.reshape(1, d), w_qkv, w_f, b_f, w_z, w_g, ln_v_g.reshape(1, GM_WIDTH),
            w_spatial, b_spatial.reshape(GM_GROUPS, GM_CHUNK, 1), w_branch_b.astype(bf16))
    return pl.pallas_call(
        _inproj_kernel,
        grid=(B, nt),
        in_specs=[pl.BlockSpec((1, tm, d), lambda b, i: (b, i, 0)),
                  pl.BlockSpec((1, 6, d), lambda b, i: (b, 0, 0))] + [full(a) for a in args[2:]],
        out_specs=[tok(3 * ATT_WIDTH), tok(LANES), tok(d), tok(d)],
        out_shape=[jax.ShapeDtypeStruct((n, 3 * ATT_WIDTH), bf16),
                   jax.ShapeDtypeStruct((n, LANES), jnp.float32),
                   jax.ShapeDtypeStruct((n, d), bf16),
                   jax.ShapeDtypeStruct((n, d), bf16)],
        scratch_shapes=[pltpu.VMEM((1, LANES), jnp.float32)],
        compiler_params=pltpu.CompilerParams(dimension_semantics=("arbitrary", "arbitrary"),
                                             vmem_limit_bytes=56 * 1024 * 1024),
        name="input_projection",
    )(*args)


ATT_BLOCK = 512


def _fox_kernel(q_ref, k_ref, v_ref, cq_ref, ck_ref, o_ref, *, blk):
    f32 = jnp.float32
    i = pl.program_id(2)
    q = q_ref[0, 0]
    cq = cq_ref[0, 0]

    def block(j, carry, masked):
        m, l, acc = carry
        off = pl.multiple_of(j * blk, blk)
        k = k_ref[0, 0, pl.ds(off, blk), :]
        v = v_ref[0, 0, pl.ds(off, blk), :]
        s = lax.dot_general(q, k, (((1,), (1,)), ((), ())), preferred_element_type=f32)
        s = s + (cq - ck_ref[0, 0, j])
        if masked:
            causal = (lax.broadcasted_iota(jnp.int32, (blk, blk), 0)
                      >= lax.broadcasted_iota(jnp.int32, (blk, blk), 1))
            s = jnp.where(causal, s, -jnp.inf)
        m_new = jnp.maximum(m, jnp.max(s, axis=1, keepdims=True))
        alpha = jnp.exp(m - m_new)
        p = jnp.exp(s - m_new)
        l = alpha * l + jnp.sum(p, axis=1, keepdims=True)
        acc = alpha * acc + jnp.dot(p.astype(v.dtype), v, preferred_element_type=f32)
        return m_new, l, acc

    init = (jnp.full((blk, 1), -1e30, f32), jnp.zeros((blk, 1), f32),
            jnp.zeros((blk, q.shape[1]), f32))
    carry = lax.fori_loop(0, i, lambda j, c: block(j, c, False), init)
    m, l, acc = block(i, carry, True)
    o_ref[0, 0] = (acc / l).astype(o_ref.dtype)


def _fox_attention(q, k, v, cum):
    B, H, S, dh = q.shape
    blk = min(ATT_BLOCK, S)
    nb = S // blk
    cq = cum.reshape(B, H, S, 1)
    ck = cum.reshape(B, H, nb, 1, blk)
    return pl.pallas_call(
        functools.partial(_fox_kernel, blk=blk),
        grid=(B, H, nb),
        in_specs=[pl.BlockSpec((1, 1, blk, dh), lambda b, h, i: (b, h, i, 0)),
                  pl.BlockSpec((1, 1, S, dh), lambda b, h, i: (b, h, 0, 0)),
                  pl.BlockSpec((1, 1, S, dh), lambda b, h, i: (b, h, 0, 0)),
                  pl.BlockSpec((1, 1, blk, 1), lambda b, h, i: (b, h, i, 0)),
                  pl.BlockSpec((1, 1, nb, 1, blk), lambda b, h, i: (b, h, 0, 0, 0))],
        out_specs=pl.BlockSpec((1, 1, blk, dh), lambda b, h, i: (b, h, i, 0)),
        out_shape=jax.ShapeDtypeStruct((B, H, S, dh), jnp.bfloat16),
        compiler_params=pltpu.CompilerParams(
            dimension_semantics=("arbitrary", "arbitrary", "arbitrary")),
        name="fox_attention",
    )(q, k, v, cq, ck)


def _topk_rows(s, k):
    rows, t = s.shape
    iota = lax.broadcasted_iota(jnp.int32, (rows, t), 0)
    slot = lax.broadcasted_iota(jnp.int32, (k, t), 0)
    vals = jnp.zeros((k, t), jnp.float32)
    ids = jnp.zeros((k, t), jnp.int32)
    for j in range(k):
        m = jnp.max(s, axis=0, keepdims=True)
        am = jnp.min(jnp.where(s == m, iota, rows), axis=0, keepdims=True)
        vals = jnp.where(slot == j, m, vals)
        ids = jnp.where(slot == j, am, ids)
        s = jnp.where(iota == am, -jnp.inf, s)
    return vals, ids


def _select_rows(table, pos):
    out = jnp.zeros(pos.shape, table.dtype)
    for r in range(table.shape[0]):
        out = jnp.where(pos == r, table[r:r + 1, :], out)
    return out


def _post_kernel(ya_ref, sga_ref, gb_ref, x_ref, mod_ref, n2g_ref, wa_ref, wo_ref, wq_ref, keys_ref,
                 x1_ref, h2_ref, idx_ref, gates_ref, q_scr):
    f32 = jnp.float32
    a = jnp.dot(ya_ref[...], wa_ref[...], preferred_element_type=f32)
    merged = sga_ref[...].astype(f32) * a + gb_ref[...].astype(f32)
    o = jnp.dot(merged.astype(jnp.bfloat16), wo_ref[...], preferred_element_type=f32)
    g1 = mod_ref[0, 2:3, :]
    sh2 = mod_ref[0, 3:4, :]
    sc2 = mod_ref[0, 4:5, :]
    x1 = x_ref[...] + g1 * o
    x1_ref[...] = x1
    h2 = x1 * lax.rsqrt(jnp.mean(x1 * x1, axis=-1, keepdims=True) + EPS) * n2g_ref[...]
    h2 = h2 * (1.0 + sc2) + sh2
    h2_ref[...] = h2
    qp = jnp.dot(h2.astype(jnp.bfloat16), wq_ref[...], preferred_element_type=f32)
    for j in range(2 * PEER_HEADS):
        q_scr[j] = qp[:, j * PEER_HALF:(j + 1) * PEER_HALF]

    def head(h, carry):
        tops = []
        for p in range(2):
            q = q_scr[2 * h + p]
            keys = keys_ref[2 * h + p]
            sc = lax.dot_general(keys, q, (((1,), (1,)), ((), ())),
                                 precision=lax.Precision.HIGHEST,
                                 preferred_element_type=f32)
            tops.append(_topk_rows(sc, PEER_TOPK))
        (s1, i1), (s2, i2) = tops
        half = PEER_TOPK // 2
        blocks = [s1[0:1, :] + s2]
        blocks += [s1[a:a + 1, :] + s2[0:half, :] for a in range(1, half)]
        blocks += [s1[half:, :] + s2[0:1, :]]
        vals, pos = _topk_rows(jnp.concatenate(blocks, axis=0), PEER_TOPK)
        mid = pos - PEER_TOPK
        tail0 = PEER_TOPK + (half - 1) * half
        ra = jnp.where(pos < PEER_TOPK, 0,
                       jnp.where(pos < tail0, 1 + (mid >> (half.bit_length() - 1)), pos - tail0 + half))
        rb = jnp.where(pos < PEER_TOPK, pos, jnp.where(pos < tail0, mid & (half - 1), 0))
        eid = _select_rows(i1, ra) * N_KEYS + _select_rows(i2, rb)
        e = jnp.exp(vals - vals[0:1, :])
        g = e / jnp.sum(e, axis=0, keepdims=True)
        row = pl.multiple_of(h * PEER_TOPK, PEER_TOPK)
        idx_ref[pl.ds(row, PEER_TOPK), :] = eid
        gates_ref[pl.ds(row, PEER_TOPK), :] = g
        return carry

    lax.fori_loop(0, PEER_HEADS, head, 0)


POST_TOK_BLOCK = 256


def _post_attention(ya, sga, gb, x, mod, norm2_g, w_a, w_out, w_query, sub_keys, seq_len):
    n, d = x.shape
    tm = POST_TOK_BLOCK
    blocks_per_seq = seq_len // tm
    aw = ya.shape[1]
    qw = w_query.shape[1]
    keys = sub_keys.reshape(2 * PEER_HEADS, N_KEYS, PEER_HALF)
    tok = lambda w: pl.BlockSpec((tm, w), lambda i: (i, 0))
    full = lambda a: pl.BlockSpec(a.shape, lambda i: (0,) * a.ndim)
    return pl.pallas_call(
        _post_kernel,
        grid=(n // tm,),
        in_specs=[tok(aw), tok(d), tok(d), tok(d),
                  pl.BlockSpec((1, 6, d), lambda i: (i // blocks_per_seq, 0, 0)),
                  full(norm2_g), full(w_a), full(w_out), full(w_query), full(keys)],
        out_specs=[tok(d), tok(d),
                   pl.BlockSpec((PEER_SLOTS, tm), lambda i: (0, i)),
                   pl.BlockSpec((PEER_SLOTS, tm), lambda i: (0, i))],
        out_shape=[jax.ShapeDtypeStruct((n, d), jnp.float32),
                   jax.ShapeDtypeStruct((n, d), jnp.float32),
                   jax.ShapeDtypeStruct((PEER_SLOTS, n), jnp.int32),
                   jax.ShapeDtypeStruct((PEER_SLOTS, n), jnp.float32)],
        scratch_shapes=[pltpu.VMEM((2 * PEER_HEADS, tm, PEER_HALF), jnp.float32)],
        compiler_params=pltpu.CompilerParams(dimension_semantics=("arbitrary",),
                                             vmem_limit_bytes=48 * 1024 * 1024),
        name="post_attention",
    )(ya, sga, gb, x, mod, norm2_g, w_a, w_out, w_query, keys)


def _rowsum_bcast(p, ones_bf16):
    hi = p.astype(jnp.bfloat16)
    lo = (p - hi.astype(jnp.float32)).astype(jnp.bfloat16)
    return (jnp.dot(hi, ones_bf16, preferred_element_type=jnp.float32)
            + jnp.dot(lo, ones_bf16, preferred_element_type=jnp.float32))


def _eval_experts(chunk, hrow, grow):
    n_chunks = hrow.shape[1] // LANES
    ones_bf16 = jnp.ones((LANES, LANES), jnp.bfloat16)
    eye = (lax.broadcasted_iota(jnp.int32, (PEER_SLOTS, LANES), 0)
           == lax.broadcasted_iota(jnp.int32, (PEER_SLOTS, LANES), 1))
    psum = None
    for c in range(n_chunks):
        u = lax.bitcast_convert_type(chunk(c) & jnp.uint32(0xFFFF0000), jnp.float32)
        p = u * hrow[:, c * LANES:(c + 1) * LANES]
        psum = p if psum is None else psum + p
    act = _gelu(_rowsum_bcast(psum, ones_bf16))
    gcol = _rowsum_bcast(jnp.where(eye, grow, 0.0), ones_bf16)
    coef = gcol * act
    outs = []
    for c in range(n_chunks):
        v = lax.bitcast_convert_type(chunk(c) << 16, jnp.float32)
        outs.append(jnp.sum(coef * v, axis=0, keepdims=True))
    return jnp.concatenate(outs, axis=-1)


def _finish_block(x_ref, g2_ref, fg_ref, peer, out_ref):
    y = x_ref[...] + g2_ref[0] * peer[...]
    out_ref[...] = y * lax.rsqrt(jnp.mean(y * y, axis=-1, keepdims=True) + EPS) * fg_ref[...]


def _expert_kernel(idx_ref, gates_ref, h_ref, x_ref, g2_ref, fg_ref, tab_ref, out_ref,
                   *scratch, tok_block, n_slots):
    bufs = scratch[:n_slots]
    peer, sem = scratch[n_slots], scratch[n_slots + 1]
    d_model = h_ref.shape[-1]
    n_chunks = d_model // LANES
    rows_per_tok = PEER_SLOTS * n_chunks

    def issue(t, s):
        for r in range(PEER_SLOTS):
            row = pl.multiple_of(idx_ref[t, r] * n_chunks, n_chunks)
            pltpu.make_async_copy(tab_ref.at[pl.ds(row, n_chunks), :],
                                  bufs[s].at[pl.ds(r * n_chunks, n_chunks), :],
                                  sem.at[s]).start(priority=r % DMA_THREADS)

    def wait(s):
        pltpu.make_async_copy(tab_ref.at[pl.ds(0, rows_per_tok), :], bufs[s], sem.at[s]).wait()

    def compute(t, s):
        chunk = lambda c: bufs[s][pl.ds(c, PEER_SLOTS, stride=n_chunks), :]
        peer[pl.ds(t, 1), :] = _eval_experts(chunk, h_ref[pl.ds(t, 1), :], gates_ref[pl.ds(t, 1), :])

    def step(t, s, prefetch):
        wait(s)
        if prefetch:
            issue(t + n_slots - 1, (s - 1) % n_slots)
        compute(t, s)

    for t in range(n_slots - 1):
        issue(t, t)

    n_groups = tok_block // n_slots

    def group(g, carry):
        for s in range(n_slots):
            step(g * n_slots + s, s, True)
        return carry

    lax.fori_loop(0, n_groups - 1, group, 0)
    for s in range(n_slots):
        t = (n_groups - 1) * n_slots + s
        step(t, s, t + n_slots - 1 < tok_block)

    _finish_block(x_ref, g2_ref, fg_ref, peer, out_ref)


def _expert_cost(n_tok, d):
    pairs = n_tok * PEER_SLOTS
    return pl.CostEstimate(flops=4 * pairs * d, transcendentals=pairs,
                           bytes_accessed=4 * pairs * d + 12 * n_tok * d + 8 * pairs)


def _sc_tanh(y):
    return 1.0 - 2.0 / (jnp.exp(2.0 * y) + 1.0)


def _sc_peer_experts(table3, ids, gates, h3):
    n_tok, n_chunks, lanes = h3.shape
    info = plsc.get_sparse_core_info()
    sl = info.num_lanes
    n_workers = info.num_cores * info.num_subcores
    tok_per_worker = n_tok // n_workers
    assert tok_per_worker * n_workers == n_tok
    rows = SC_GATHER_ROWS
    n_sub = PEER_SLOTS // rows
    pieces = [(c, k * sl) for c in range(n_chunks) for k in range(lanes // sl)]
    mesh = plsc.VectorSubcoreMesh(core_axis_name="c", subcore_axis_name="s")
    buf = lambda dt: pltpu.VMEM((rows, n_chunks, lanes), dt)

    @functools.partial(
        pl.kernel, mesh=mesh,
        out_type=jax.ShapeDtypeStruct((n_tok, n_chunks, lanes), jnp.float32),
        scratch_types=[pltpu.VMEM((PEER_SLOTS,), jnp.int32), pltpu.VMEM((PEER_SLOTS,), jnp.float32),
                       pltpu.VMEM((n_chunks, lanes), jnp.float32), buf(jnp.uint32), buf(jnp.uint32),
                       pltpu.VMEM((n_chunks, lanes), jnp.float32), pltpu.SemaphoreType.DMA((2,))],
        compiler_params=pltpu.CompilerParams(needs_layout_passes=False),
        cost_estimate=_expert_cost(n_tok, n_chunks * lanes),
        name="sc_peer_experts")
    def run(table_hbm, ids_hbm, gates_hbm, h_hbm, peer_hbm, ids_v, g_v, h_v, rows0, rows1, out_v, sem):
        bufs = (rows0, rows1)
        wid = lax.axis_index("s") * info.num_cores + lax.axis_index("c")

        def gather(s):
            return pltpu.make_async_copy(table_hbm.at[ids_v.at[pl.ds(s * rows, rows)]],
                                         bufs[s % 2], sem.at[s % 2])

        @pl.loop(0, tok_per_worker)
        def _(ti):
            t = wid * tok_per_worker + ti
            off = pl.multiple_of(t * PEER_SLOTS, PEER_SLOTS)
            pltpu.sync_copy(ids_hbm.at[pl.ds(off, PEER_SLOTS)], ids_v)
            pltpu.sync_copy(gates_hbm.at[pl.ds(off, PEER_SLOTS)], g_v)
            pltpu.sync_copy(h_hbm.at[t], h_v)
            gather(0).start()
            for c, k in pieces:
                out_v[c, pl.ds(k, sl)] = jnp.zeros((sl,), jnp.float32)
            for s in range(n_sub):
                gather(s).wait()
                if s + 1 < n_sub:
                    gather(s + 1).start()
                rbuf = bufs[s % 2]

                @pl.loop(0, rows, step=SC_ROW_GROUP)
                def _(r0):
                    accs = [jnp.zeros((sl,), jnp.float32) for _ in range(SC_ROW_GROUP)]
                    for c, k in pieces:
                        hv = h_v[c, pl.ds(k, sl)]
                        for j in range(SC_ROW_GROUP):
                            w = rbuf[r0 + j, c, pl.ds(k, sl)]
                            u = lax.bitcast_convert_type(w & jnp.uint32(0xFFFF0000), jnp.float32)
                            accs[j] = accs[j] + u * hv
                    coefs = []
                    for j in range(SC_ROW_GROUP):
                        a = jnp.broadcast_to(jnp.sum(accs[j]), (sl,))
                        act = 0.5 * a * (1.0 + _sc_tanh(0.7978845608028654 * (a + 0.044715 * (a * a * a))))
                        gate = plsc.load_gather(g_v, [jnp.broadcast_to(s * rows + r0 + j, (sl,))])
                        coefs.append(gate * act)
                    for b0 in range(0, len(pieces), SC_STORE_BATCH):
                        tots = []
                        for c, k in pieces[b0:b0 + SC_STORE_BATCH]:
                            tot = None
                            for j in range(SC_ROW_GROUP):
                                w = rbuf[r0 + j, c, pl.ds(k, sl)]
                                term = coefs[j] * lax.bitcast_convert_type(w << 16, jnp.float32)
                                tot = term if tot is None else tot + term
                            tots.append(tot)
                        for (c, k), tot in zip(pieces[b0:b0 + SC_STORE_BATCH], tots):
                            plsc.addupdate(out_v.at[c, pl.ds(k, sl)], tot)

            pltpu.sync_copy(out_v, peer_hbm.at[t])

    return run(table3, ids, gates, h3)


def _finish_kernel(peer_ref, x_ref, g2_ref, fg_ref, prev_ref, out_ref):
    del prev_ref
    _finish_block(x_ref, g2_ref, fg_ref, peer_ref, out_ref)


def _finish_tokens(peer, x1, g2, fg, prev, seq_len):
    d = x1.shape[1]
    tm = FINISH_TOK_BLOCK
    return pl.pallas_call(
        _finish_kernel,
        grid=(peer.shape[0] // tm,),
        in_specs=[pl.BlockSpec((tm, d), lambda i: (i, 0)),
                  pl.BlockSpec((tm, d), lambda i: (i, 0)),
                  pl.BlockSpec((1, 1, d), lambda i: (i * tm // seq_len, 0, 0)),
                  pl.BlockSpec((1, d), lambda i: (0, 0)),
                  pl.BlockSpec(memory_space=pl.ANY)],
        out_specs=pl.BlockSpec((tm, d), lambda i: (i, 0)),
        out_shape=jax.ShapeDtypeStruct(prev.shape, prev.dtype),
        input_output_aliases={4: 0},
        compiler_params=pltpu.CompilerParams(dimension_semantics=("arbitrary",)),
        name="finish_tokens",
    )(peer, x1, g2, fg, prev)


def _pack_expert_table(expert_u, expert_v):
    ub = lax.bitcast_convert_type(expert_u.astype(jnp.bfloat16), jnp.uint16).astype(jnp.uint32)
    vb = lax.bitcast_convert_type(expert_v.astype(jnp.bfloat16), jnp.uint16).astype(jnp.uint32)
    packed = (ub << 16) | vb
    n_experts, d = packed.shape
    return packed.reshape(n_experts * (d // LANES), LANES)


def _tc_peer_experts(idx, gates, h2, x1, g2, final_g, table, seq_len, n_out):
    m, d = h2.shape
    n_chunks = d // LANES
    tb = EXPERT_TOK_BLOCK
    assert tb % EXPERT_SLOTS == 0 and seq_len % tb == 0 and (n_out - m) % tb == 0
    first = (n_out - m) // tb
    return pl.pallas_call(
        functools.partial(_expert_kernel, tok_block=tb, n_slots=EXPERT_SLOTS),
        grid=(m // tb,),
        in_specs=[
            pl.BlockSpec((tb, PEER_SLOTS), lambda i: (i, 0), memory_space=pltpu.SMEM),
            pl.BlockSpec((tb, PEER_SLOTS), lambda i: (i, 0)),
            pl.BlockSpec((tb, d), lambda i: (i, 0)),
            pl.BlockSpec((tb, d), lambda i: (i, 0)),
            pl.BlockSpec((1, 1, d), lambda i: (i * tb // seq_len, 0, 0)),
            pl.BlockSpec((1, d), lambda i: (0, 0)),
            pl.BlockSpec(memory_space=pl.ANY),
        ],
        out_specs=pl.BlockSpec((tb, d), lambda i: (i + first, 0)),
        out_shape=jax.ShapeDtypeStruct((n_out, d), jnp.float32),
        scratch_shapes=(
            [pltpu.VMEM((PEER_SLOTS * n_chunks, LANES), jnp.uint32) for _ in range(EXPERT_SLOTS)]
            + [pltpu.VMEM((tb, d), jnp.float32), pltpu.SemaphoreType.DMA((EXPERT_SLOTS,))]),
        compiler_params=pltpu.CompilerParams(dimension_semantics=("arbitrary",)),
        cost_estimate=_expert_cost(m, d),
        name="peer_experts",
    )(idx, gates, h2, x1, g2, final_g.reshape(1, d), table)


def _mix_and_route(x, mod, norm1_g, w_in, b_forget, ln_v_g, w_spatial, b_spatial, w_branch_a,
                   w_branch_b, w_out, norm2_g, w_query, sub_keys):
    Bp, S, D = x.shape
    m = Bp * S
    bf16 = jnp.bfloat16
    qkv, cum, sga, gb = _input_projection(x, mod, norm1_g, w_in, b_forget, ln_v_g,
                                          w_spatial, b_spatial, w_branch_b)
    heads = lambda a: a.reshape(Bp, S, ATT_HEADS, ATT_HEAD_DIM).transpose(0, 2, 1, 3)
    q = heads(qkv[:, 0:ATT_WIDTH])
    k = heads(qkv[:, ATT_WIDTH:2 * ATT_WIDTH])
    v = heads(qkv[:, 2 * ATT_WIDTH:])
    cum_h = cum[:, 0:ATT_HEADS].reshape(Bp, S, ATT_HEADS).transpose(0, 2, 1)
    y_a = _fox_attention(q, k, v, cum_h).transpose(0, 2, 1, 3).reshape(m, ATT_WIDTH)
    x1, h2, idx_t, gates_t = _post_attention(
        y_a, sga, gb, x.reshape(m, D), mod, norm2_g.reshape(1, D),
        w_branch_a.astype(bf16), w_out.astype(bf16), w_query.astype(bf16), sub_keys, S)
    return x1, h2, idx_t.T, gates_t.T


def kernel(x, c, w_mod, b_mod, norm1_g, w_in, b_forget, ln_v_g, w_spatial, b_spatial, w_branch_a, w_branch_b, w_out, norm2_g, w_query, sub_keys, expert_u, expert_v, final_g):
    B, S, D = x.shape
    n = B * S
    assert w_mod.shape[0] == 1, "the final RMSNorm is fused into the single layer's expert kernels"
    l = 0
    mod = _modulation(c, w_mod[l], b_mod[l]).reshape(B, 6, D)
    table = _pack_expert_table(expert_u[l], expert_v[l])
    n_chunks = D // LANES
    weights = (norm1_g[l], w_in[l], b_forget[l], ln_v_g[l], w_spatial[l], b_spatial[l],
               w_branch_a[l], w_branch_b[l], w_out[l], norm2_g[l], w_query[l], sub_keys[l])

    b_sc = B * SC_SHARE_PERCENT // 100
    peer_sc = None
    x_b = x[b_sc:]
    if b_sc > 0:
        x1_a, h2_a, idx_a, gates_a = _mix_and_route(x[:b_sc], mod[:b_sc], *weights)
        m_a = b_sc * S
        sc_args = (idx_a.reshape(-1), gates_a.reshape(-1), h2_a.reshape(m_a, n_chunks, LANES))
        sc_args, x_b = lax.optimization_barrier((sc_args, x_b))
        peer_sc = _sc_peer_experts(table.reshape(-1, n_chunks, LANES), *sc_args)
    x1_b, h2_b, idx_b, gates_b = _mix_and_route(x_b, mod[b_sc:], *weights)
    out = _tc_peer_experts(idx_b, gates_b, h2_b, x1_b, mod[b_sc:, 5:6, :], final_g, table, S, n)
    if b_sc > 0:
        out = _finish_tokens(peer_sc.reshape(m_a, D), x1_a, mod[:b_sc, 5:6, :],
                             final_g.reshape(1, D), out, S)
    return out.reshape(B, S, D)
```

```python
import functools

import jax
import jax.numpy as jnp
from jax import lax
from jax.experimental import pallas as pl
from jax.experimental.pallas import tpu as pltpu
from jax.experimental.pallas import tpu_sc as plsc

D_MODEL = 1024
ATT_HEADS = 8
ATT_HEAD_DIM = 64
ATT_WIDTH = ATT_HEADS * ATT_HEAD_DIM
Q_BLOCK = 128
GM_GROUPS = 4
GM_GROUP_DIM = 128
GM_WIDTH = GM_GROUPS * GM_GROUP_DIM
GM_CHUNK = 128
PEER_HEADS = 8
PEER_KEY_DIM = 256
PEER_HALF = PEER_KEY_DIM // 2
N_KEYS = 128
PEER_TOPK = 16
PEER_TOK_BLOCK = 128
PEER_SLOTS = PEER_HEADS * PEER_TOPK
SPLIT_POINTS = (ATT_WIDTH, 2 * ATT_WIDTH, 3 * ATT_WIDTH, 3 * ATT_WIDTH + ATT_HEADS,
                3 * ATT_WIDTH + ATT_HEADS + 2 * GM_WIDTH,
                3 * ATT_WIDTH + ATT_HEADS + 2 * GM_WIDTH + D_MODEL)
EPS = 1e-6

LANES = 128
EXPERT_TOK_BLOCK = 128
EXPERT_SLOTS = 4
DMA_THREADS = 2
FINISH_TOK_BLOCK = 512
SC_GATHER_ROWS = 32
SC_ROW_GROUP = 4
SC_STORE_BATCH = 8
SC_SHARE_PERCENT = 44
SC_SHARE_ALIGN = 1024


def _gelu(x):
    return 0.5 * x * (1.0 + jnp.tanh(0.7978845608028654 * (x + 0.044715 * (x * x * x))))


def _rmsnorm(x, g):
    y = x * lax.rsqrt(jnp.mean(x * x, axis=-1, keepdims=True) + EPS)
    return y * g


def _layernorm(x, g):
    mu = jnp.mean(x, axis=-1, keepdims=True)
    var = jnp.mean(jnp.square(x - mu), axis=-1, keepdims=True)
    return (x - mu) * lax.rsqrt(var + EPS) * g


def _modulate(h, shift, scale):
    return h * (1.0 + scale[:, None, :]) + shift[:, None, :]


def _fox(q, k, v, log_f):
    B, S, H, Dh = q.shape
    nb = S // Q_BLOCK
    cum = jnp.cumsum(log_f, axis=1)
    cum_k = jnp.transpose(cum, (0, 2, 1))
    k_pos = jnp.arange(S)
    q_blocks = q.reshape(B, nb, Q_BLOCK, H, Dh).transpose(1, 0, 2, 3, 4)
    cq_blocks = cum_k.reshape(B, H, nb, Q_BLOCK).transpose(2, 0, 1, 3)
    starts = jnp.arange(nb) * Q_BLOCK
    scale = ATT_HEAD_DIM ** -0.5

    def one_block(args):
        q_blk, cq_blk, start = args
        s = jnp.einsum('bqhd,bkhd->bhqk', q_blk, k) * scale
        s = s + cq_blk[..., :, None] - cum_k[..., None, :]
        q_pos = start + jnp.arange(Q_BLOCK)
        causal = q_pos[:, None] >= k_pos[None, :]
        s = jnp.where(causal[None, None], s, -jnp.inf)
        p = jax.nn.softmax(s, axis=-1)
        return jnp.einsum('bhqk,bkhd->bqhd', p, v)

    out = lax.map(one_block, (q_blocks, cq_blocks, starts))
    return out.transpose(1, 0, 2, 3, 4).reshape(B, S, H * Dh)


def _gmlp(z, ln_g, w_s, b_s):
    B, S, _ = z.shape
    u, v = jnp.split(z, 2, axis=-1)
    v = _layernorm(v, ln_g)
    nc = S // GM_CHUNK
    v = v.reshape(B, nc, GM_CHUNK, GM_GROUPS, GM_GROUP_DIM)
    tril = jnp.tril(jnp.ones((GM_CHUNK, GM_CHUNK), dtype=bool))
    w = jnp.where(tril[None], w_s, jnp.zeros_like(w_s))
    mixed = jnp.einsum('gts,bcsgd->bctgd', w, v) + b_s.T[None, None, :, :, None]
    return u * mixed.reshape(B, S, GM_WIDTH)


def _peer_route(hf, w_query, sub_keys):
    n = hf.shape[0]
    q = (hf @ w_query).reshape(n, PEER_HEADS, 2, PEER_HALF)
    scores = jnp.einsum('nhpd,hpkd->nhpk', q, sub_keys)
    s_top, i_top = lax.top_k(scores, PEER_TOPK)
    cand_s = (s_top[:, :, 0, :, None] + s_top[:, :, 1, None, :]).reshape(n, PEER_HEADS, PEER_TOPK * PEER_TOPK)
    cand_i = (i_top[:, :, 0, :, None] * N_KEYS + i_top[:, :, 1, None, :]).reshape(n, PEER_HEADS, PEER_TOPK * PEER_TOPK)
    best_s, best_pos = lax.top_k(cand_s, PEER_TOPK)
    idx = jnp.take_along_axis(cand_i, best_pos, axis=-1)
    gates = jax.nn.softmax(best_s, axis=-1)
    return idx.reshape(n, PEER_SLOTS), gates.reshape(n, PEER_SLOTS)


def _mod_kernel(c_ref, w_ref, b_ref, o_ref):
    c = c_ref[...]
    sc = c * jax.nn.sigmoid(c)
    o_ref[...] = jnp.dot(sc, w_ref[...], precision=lax.Precision.HIGHEST,
                         preferred_element_type=jnp.float32) + b_ref[...]


def _modulation(c, w_mod, b_mod):
    b, d = c.shape
    cols = w_mod.shape[1]
    return pl.pallas_call(
        _mod_kernel,
        grid=(cols // d,),
        in_specs=[pl.BlockSpec((b, d), lambda j: (0, 0)),
                  pl.BlockSpec((d, d), lambda j: (0, j)),
                  pl.BlockSpec((1, d), lambda j: (0, j))],
        out_specs=pl.BlockSpec((b, d), lambda j: (0, j)),
        out_shape=jax.ShapeDtypeStruct((b, cols), jnp.float32),
        name="modulation",
    )(c, w_mod, b_mod.reshape(1, cols))


INPROJ_TOK_BLOCK = 256


def _inproj_kernel(x_ref, mod_ref, n1g_ref, wqkv_ref, wf_ref, bf_ref, wz_ref, wg_ref, lng_ref,
                   wsp_ref, bsp_ref, wb_ref, qkv_ref, cum_ref, sga_ref, gb_ref, carry):
    f32, bf16 = jnp.float32, jnp.bfloat16
    tm, d = x_ref.shape[1], x_ref.shape[2]
    x = x_ref[0]
    sh1 = mod_ref[0, 0:1, :]
    sc1 = mod_ref[0, 1:2, :]
    h = x * lax.rsqrt(jnp.mean(x * x, axis=-1, keepdims=True) + EPS) * n1g_ref[...]
    hb = (h * (1.0 + sc1) + sh1).astype(bf16)

    qkv = jnp.dot(hb, wqkv_ref[...], preferred_element_type=f32)
    qkv_ref[:, 0:ATT_WIDTH] = (qkv[:, 0:ATT_WIDTH] * (ATT_HEAD_DIM ** -0.5)).astype(bf16)
    qkv_ref[:, ATT_WIDTH:] = qkv[:, ATT_WIDTH:].astype(bf16)

    f = jnp.dot(hb, wf_ref[...], preferred_element_type=f32) + bf_ref[...]
    logf = jnp.minimum(f, 0.0) - jnp.log1p(jnp.exp(-jnp.abs(f)))

    @pl.when(pl.program_id(1) == 0)
    def _():
        carry[...] = jnp.zeros_like(carry)

    tri = (lax.broadcasted_iota(jnp.int32, (tm, tm), 0)
           >= lax.broadcasted_iota(jnp.int32, (tm, tm), 1)).astype(f32)
    cum = jnp.dot(tri, logf, precision=lax.Precision.HIGHEST, preferred_element_type=f32) + carry[...]
    cum_ref[...] = cum
    carry[...] = cum[tm - 1:tm, :]

    gz = _gelu(jnp.dot(hb, wz_ref[...], preferred_element_type=f32))
    u = gz[:, 0:GM_WIDTH]
    v = gz[:, GM_WIDTH:]
    mu = jnp.mean(v, axis=-1, keepdims=True)
    var = jnp.mean(jnp.square(v - mu), axis=-1, keepdims=True)
    vn = ((v - mu) * lax.rsqrt(var + EPS) * lng_ref[...]).astype(bf16)
    tril = (lax.broadcasted_iota(jnp.int32, (GM_CHUNK, GM_CHUNK), 0)
            >= lax.broadcasted_iota(jnp.int32, (GM_CHUNK, GM_CHUNK), 1))
    w_sp = [jnp.where(tril, wsp_ref[g], 0.0).astype(bf16) for g in range(GM_GROUPS)]
    rows = []
    for ck in range(tm // GM_CHUNK):
        r0 = ck * GM_CHUNK
        cols = []
        for g in range(GM_GROUPS):
            c0 = g * GM_GROUP_DIM
            mixed = jnp.dot(w_sp[g], vn[r0:r0 + GM_CHUNK, c0:c0 + GM_GROUP_DIM],
                            preferred_element_type=f32) + bsp_ref[g]
            cols.append(u[r0:r0 + GM_CHUNK, c0:c0 + GM_GROUP_DIM] * mixed)
        rows.append(jnp.concatenate(cols, axis=1))
    yb = jnp.concatenate(rows, axis=0).astype(bf16)
    ybp = jnp.dot(yb, wb_ref[...], preferred_element_type=f32)

    sg = jax.nn.sigmoid(jnp.dot(hb, wg_ref[...], preferred_element_type=f32))
    sga_ref[...] = sg[:, 0:d].astype(bf16)
    gb_ref[...] = (sg[:, d:] * ybp).astype(bf16)


def _input_projection(x, mod, norm1_g, w_in, b_forget, ln_v_g, w_spatial, b_spatial, w_branch_b):
    B, S, d = x.shape
    n = B * S
    tm = INPROJ_TOK_BLOCK
    bf16 = jnp.bfloat16
    p0, p1, p2, p3, p4, p5 = SPLIT_POINTS
    w_qkv = w_in[:, 0:p2].astype(bf16)
    w_f = jnp.pad(w_in[:, p2:p3], ((0, 0), (0, LANES - ATT_HEADS))).astype(bf16)
    b_f = jnp.pad(b_forget, (0, LANES - ATT_HEADS)).reshape(1, LANES)
    w_z = w_in[:, p3:p4].astype(bf16)
    w_g = w_in[:, p4:].astype(bf16)
    nt = S // tm
    tok = lambda w: pl.BlockSpec((tm, w), lambda b, i: (b * nt + i, 0))
    full = lambda a: pl.BlockSpec(a.shape, lambda b, i: (0,) * a.ndim)
    args = (x, mod, norm1_g.reshape(1, d), w_qkv, w_f, b_f, w_z, w_g, ln_v_g.reshape(1, GM_WIDTH),
            w_spatial, b_spatial.reshape(GM_GROUPS, GM_CHUNK, 1), w_branch_b.astype(bf16))
    return pl.pallas_call(
        _inproj_kernel,
        grid=(B, nt),
        in_specs=[pl.BlockSpec((1, tm, d), lambda b, i: (b, i, 0)),
                  pl.BlockSpec((1, 6, d), lambda b, i: (b, 0, 0))] + [full(a) for a in args[2:]],
        out_specs=[tok(3 * ATT_WIDTH), tok(LANES), tok(d), tok(d)],
        out_shape=[jax.ShapeDtypeStruct((n, 3 * ATT_WIDTH), bf16),
                   jax.ShapeDtypeStruct((n, LANES), jnp.float32),
                   jax.ShapeDtypeStruct((n, d), bf16),
                   jax.ShapeDtypeStruct((n, d), bf16)],
        scratch_shapes=[pltpu.VMEM((1, LANES), jnp.float32)],
        compiler_params=pltpu.CompilerParams(dimension_semantics=("arbitrary", "arbitrary"),
                                             vmem_limit_bytes=56 * 1024 * 1024),
        name="input_projection",
    )(*args)


ATT_BLOCK = 512


def _fox_kernel(q_ref, k_ref, v_ref, cq_ref, ck_ref, o_ref, *, blk):
    f32 = jnp.float32
    i = pl.program_id(2)
    q = q_ref[0, 0]
    cq = cq_ref[0, 0]

    def block(j, carry, masked):
        m, l, acc = carry
        off = pl.multiple_of(j * blk, blk)
        k = k_ref[0, 0, pl.ds(off, blk), :]
        v = v_ref[0, 0, pl.ds(off, blk), :]
        s = lax.dot_general(q, k, (((1,), (1,)), ((), ())), preferred_element_type=f32)
        s = s + (cq - ck_ref[0, 0, j])
        if masked:
            causal = (lax.broadcasted_iota(jnp.int32, (blk, blk), 0)
                      >= lax.broadcasted_iota(jnp.int32, (blk, blk), 1))
            s = jnp.where(causal, s, -jnp.inf)
        m_new = jnp.maximum(m, jnp.max(s, axis=1, keepdims=True))
        alpha = jnp.exp(m - m_new)
        p = jnp.exp(s - m_new)
        l = alpha * l + jnp.sum(p, axis=1, keepdims=True)
        acc = alpha * acc + jnp.dot(p.astype(v.dtype), v, preferred_element_type=f32)
        return m_new, l, acc

    init = (jnp.full((blk, 1), -1e30, f32), jnp.zeros((blk, 1), f32),
            jnp.zeros((blk, q.shape[1]), f32))
    carry = lax.fori_loop(0, i, lambda j, c: block(j, c, False), init)
    m, l, acc = block(i, carry, True)
    o_ref[0, 0] = (acc / l).astype(o_ref.dtype)


def _fox_attention(q, k, v, cum):
    B, H, S, dh = q.shape
    blk = min(ATT_BLOCK, S)
    nb = S // blk
    cq = cum.reshape(B, H, S, 1)
    ck = cum.reshape(B, H, nb, 1, blk)
    return pl.pallas_call(
        functools.partial(_fox_kernel, blk=blk),
        grid=(B, H, nb),
        in_specs=[pl.BlockSpec((1, 1, blk, dh), lambda b, h, i: (b, h, i, 0)),
                  pl.BlockSpec((1, 1, S, dh), lambda b, h, i: (b, h, 0, 0)),
                  pl.BlockSpec((1, 1, S, dh), lambda b, h, i: (b, h, 0, 0)),
                  pl.BlockSpec((1, 1, blk, 1), lambda b, h, i: (b, h, i, 0)),
                  pl.BlockSpec((1, 1, nb, 1, blk), lambda b, h, i: (b, h, 0, 0, 0))],
        out_specs=pl.BlockSpec((1, 1, blk, dh), lambda b, h, i: (b, h, i, 0)),
        out_shape=jax.ShapeDtypeStruct((B, H, S, dh), jnp.bfloat16),
        compiler_params=pltpu.CompilerParams(
            dimension_semantics=("arbitrary", "arbitrary", "arbitrary")),
        name="fox_attention",
    )(q, k, v, cq, ck)


def _topk_rows(s, k):
    rows, t = s.shape
    iota = lax.broadcasted_iota(jnp.int32, (rows, t), 0)
    slot = lax.broadcasted_iota(jnp.int32, (k, t), 0)
    vals = jnp.zeros((k, t), jnp.float32)
    ids = jnp.zeros((k, t), jnp.int32)
    for j in range(k):
        m = jnp.max(s, axis=0, keepdims=True)
        am = jnp.min(jnp.where(s == m, iota, rows), axis=0, keepdims=True)
        vals = jnp.where(slot == j, m, vals)
        ids = jnp.where(slot == j, am, ids)
        s = jnp.where(iota == am, -jnp.inf, s)
    return vals, ids


def _select_rows(table, pos):
    out = jnp.zeros(pos.shape, table.dtype)
    for r in range(table.shape[0]):
        out = jnp.where(pos == r, table[r:r + 1, :], out)
    return out


def _post_kernel(ya_ref, sga_ref, gb_ref, x_ref, mod_ref, n2g_ref, wa_ref, wo_ref, wq_ref, keys_ref,
                 x1_ref, h2_ref, idx_ref, gates_ref, q_scr):
    f32 = jnp.float32
    a = jnp.dot(ya_ref[...], wa_ref[...], preferred_element_type=f32)
    merged = sga_ref[...].astype(f32) * a + gb_ref[...].astype(f32)
    o = jnp.dot(merged.astype(jnp.bfloat16), wo_ref[...], preferred_element_type=f32)
    g1 = mod_ref[0, 2:3, :]
    sh2 = mod_ref[0, 3:4, :]
    sc2 = mod_ref[0, 4:5, :]
    x1 = x_ref[...] + g1 * o
    x1_ref[...] = x1
    h2 = x1 * lax.rsqrt(jnp.mean(x1 * x1, axis=-1, keepdims=True) + EPS) * n2g_ref[...]
    h2 = h2 * (1.0 + sc2) + sh2
    h2_ref[...] = h2
    qp = jnp.dot(h2.astype(jnp.bfloat16), wq_ref[...], preferred_element_type=f32)
    for j in range(2 * PEER_HEADS):
        q_scr[j] = qp[:, j * PEER_HALF:(j + 1) * PEER_HALF]

    def head(h, carry):
        tops = []
        for p in range(2):
            q = q_scr[2 * h + p]
            keys = keys_ref[2 * h + p]
            sc = lax.dot_general(keys, q, (((1,), (1,)), ((), ())),
                                 precision=lax.Precision.HIGHEST,
                                 preferred_element_type=f32)
            tops.append(_topk_rows(sc, PEER_TOPK))
        (s1, i1), (s2, i2) = tops
        half = PEER_TOPK // 2
        blocks = [s1[0:1, :] + s2]
        blocks += [s1[a:a + 1, :] + s2[0:half, :] for a in range(1, half)]
        blocks += [s1[half:, :] + s2[0:1, :]]
        vals, pos = _topk_rows(jnp.concatenate(blocks, axis=0), PEER_TOPK)
        mid = pos - PEER_TOPK
        tail0 = PEER_TOPK + (half - 1) * half
        ra = jnp.where(pos < PEER_TOPK, 0,
                       jnp.where(pos < tail0, 1 + (mid >> (half.bit_length() - 1)), pos - tail0 + half))
        rb = jnp.where(pos < PEER_TOPK, pos, jnp.where(pos < tail0, mid & (half - 1), 0))
        eid = _select_rows(i1, ra) * N_KEYS + _select_rows(i2, rb)
        e = jnp.exp(vals - vals[0:1, :])
        g = e / jnp.sum(e, axis=0, keepdims=True)
        row = pl.multiple_of(h * PEER_TOPK, PEER_TOPK)
        idx_ref[pl.ds(row, PEER_TOPK), :] = eid
        gates_ref[pl.ds(row, PEER_TOPK), :] = g
        return carry

    lax.fori_loop(0, PEER_HEADS, head, 0)


POST_TOK_BLOCK = 256


def _post_attention(ya, sga, gb, x, mod, norm2_g, w_a, w_out, w_query, sub_keys, seq_len):
    n, d = x.shape
    tm = POST_TOK_BLOCK
    blocks_per_seq = seq_len // tm
    aw = ya.shape[1]
    qw = w_query.shape[1]
    keys = sub_keys.reshape(2 * PEER_HEADS, N_KEYS, PEER_HALF)
    tok = lambda w: pl.BlockSpec((tm, w), lambda i: (i, 0))
    full = lambda a: pl.BlockSpec(a.shape, lambda i: (0,) * a.ndim)
    return pl.pallas_call(
        _post_kernel,
        grid=(n // tm,),
        in_specs=[tok(aw), tok(d), tok(d), tok(d),
                  pl.BlockSpec((1, 6, d), lambda i: (i // blocks_per_seq, 0, 0)),
                  full(norm2_g), full(w_a), full(w_out), full(w_query), full(keys)],
        out_specs=[tok(d), tok(d),
                   pl.BlockSpec((PEER_SLOTS, tm), lambda i: (0, i)),
                   pl.BlockSpec((PEER_SLOTS, tm), lambda i: (0, i))],
        out_shape=[jax.ShapeDtypeStruct((n, d), jnp.float32),
                   jax.ShapeDtypeStruct((n, d), jnp.float32),
                   jax.ShapeDtypeStruct((PEER_SLOTS, n), jnp.int32),
                   jax.ShapeDtypeStruct((PEER_SLOTS, n), jnp.float32)],
        scratch_shapes=[pltpu.VMEM((2 * PEER_HEADS, tm, PEER_HALF), jnp.float32)],
        compiler_params=pltpu.CompilerParams(dimension_semantics=("arbitrary",),
                                             vmem_limit_bytes=48 * 1024 * 1024),
        name="post_attention",
    )(ya, sga, gb, x, mod, norm2_g, w_a, w_out, w_query, keys)


def _rowsum_bcast(p, ones_bf16):
    hi = p.astype(jnp.bfloat16)
    lo = (p - hi.astype(jnp.float32)).astype(jnp.bfloat16)
    return (jnp.dot(hi, ones_bf16, preferred_element_type=jnp.float32)
            + jnp.dot(lo, ones_bf16, preferred_element_type=jnp.float32))


def _eval_experts(chunk, hrow, grow):
    n_chunks = hrow.shape[1] // LANES
    ones_bf16 = jnp.ones((LANES, LANES), jnp.bfloat16)
    eye = (lax.broadcasted_iota(jnp.int32, (PEER_SLOTS, LANES), 0)
           == lax.broadcasted_iota(jnp.int32, (PEER_SLOTS, LANES), 1))
    psum = None
    for c in range(n_chunks):
        u = lax.bitcast_convert_type(chunk(c) & jnp.uint32(0xFFFF0000), jnp.float32)
        p = u * hrow[:, c * LANES:(c + 1) * LANES]
        psum = p if psum is None else psum + p
    act = _gelu(_rowsum_bcast(psum, ones_bf16))
    gcol = _rowsum_bcast(jnp.where(eye, grow, 0.0), ones_bf16)
    coef = gcol * act
    outs = []
    for c in range(n_chunks):
        v = lax.bitcast_convert_type(chunk(c) << 16, jnp.float32)
        outs.append(jnp.sum(coef * v, axis=0, keepdims=True))
    return jnp.concatenate(outs, axis=-1)


def _finish_block(x_ref, g2_ref, fg_ref, peer, out_ref):
    y = x_ref[...] + g2_ref[0] * peer[...]
    out_ref[...] = y * lax.rsqrt(jnp.mean(y * y, axis=-1, keepdims=True) + EPS) * fg_ref[...]


def _expert_kernel(idx_ref, gates_ref, h_ref, x_ref, g2_ref, fg_ref, tab_ref, out_ref,
                   *scratch, tok_block, n_slots):
    bufs = scratch[:n_slots]
    peer, sem = scratch[n_slots], scratch[n_slots + 1]
    d_model = h_ref.shape[-1]
    n_chunks = d_model // LANES
    rows_per_tok = PEER_SLOTS * n_chunks

    def issue(t, s):
        for r in range(PEER_SLOTS):
            row = pl.multiple_of(idx_ref[t, r] * n_chunks, n_chunks)
            pltpu.make_async_copy(tab_ref.at[pl.ds(row, n_chunks), :],
                                  bufs[s].at[pl.ds(r * n_chunks, n_chunks), :],
                                  sem.at[s]).start(priority=r % DMA_THREADS)

    def wait(s):
        pltpu.make_async_copy(tab_ref.at[pl.ds(0, rows_per_tok), :], bufs[s], sem.at[s]).wait()

    def compute(t, s):
        chunk = lambda c: bufs[s][pl.ds(c, PEER_SLOTS, stride=n_chunks), :]
        peer[pl.ds(t, 1), :] = _eval_experts(chunk, h_ref[pl.ds(t, 1), :], gates_ref[pl.ds(t, 1), :])

    def step(t, s, prefetch):
        wait(s)
        if prefetch:
            issue(t + n_slots - 1, (s - 1) % n_slots)
        compute(t, s)

    for t in range(n_slots - 1):
        issue(t, t)

    n_groups = tok_block // n_slots

    def group(g, carry):
        for s in range(n_slots):
            step(g * n_slots + s, s, True)
        return carry

    lax.fori_loop(0, n_groups - 1, group, 0)
    for s in range(n_slots):
        t = (n_groups - 1) * n_slots + s
        step(t, s, t + n_slots - 1 < tok_block)

    _finish_block(x_ref, g2_ref, fg_ref, peer, out_ref)


def _expert_cost(n_tok, d):
    pairs = n_tok * PEER_SLOTS
    return pl.CostEstimate(flops=4 * pairs * d, transcendentals=pairs,
                           bytes_accessed=4 * pairs * d + 12 * n_tok * d + 8 * pairs)


def _sc_tanh(y):
    return 1.0 - 2.0 / (jnp.exp(2.0 * y) + 1.0)


def _sc_peer_experts(table3, ids, gates, h3):
    n_tok, n_chunks, lanes = h3.shape
    info = plsc.get_sparse_core_info()
    sl = info.num_lanes
    n_workers = info.num_cores * info.num_subcores
    tok_per_worker = n_tok // n_workers
    assert tok_per_worker * n_workers == n_tok and tok_per_worker % 2 == 0
    rows = SC_GATHER_ROWS
    n_sub = PEER_SLOTS // rows
    assert n_sub % 2 == 0
    pieces = [(c, k * sl) for c in range(n_chunks) for k in range(lanes // sl)]
    mesh = plsc.VectorSubcoreMesh(core_axis_name="c", subcore_axis_name="s")
    buf = lambda dt: pltpu.VMEM((rows, n_chunks, lanes), dt)

    @functools.partial(
        pl.kernel, mesh=mesh,
        out_type=jax.ShapeDtypeStruct((n_tok, n_chunks, lanes), jnp.float32),
        scratch_types=[pltpu.VMEM((PEER_SLOTS,), jnp.int32), pltpu.VMEM((PEER_SLOTS,), jnp.int32),
                       pltpu.VMEM((PEER_SLOTS,), jnp.float32), pltpu.VMEM((PEER_SLOTS,), jnp.float32),
                       pltpu.VMEM((n_chunks, lanes), jnp.float32),
                       pltpu.VMEM((n_chunks, lanes), jnp.float32),
                       buf(jnp.uint32), buf(jnp.uint32), pltpu.VMEM((n_chunks, lanes), jnp.float32),
                       pltpu.SemaphoreType.DMA((2,)), pltpu.SemaphoreType.DMA((2,))],
        compiler_params=pltpu.CompilerParams(needs_layout_passes=False),
        cost_estimate=_expert_cost(n_tok, n_chunks * lanes),
        name="sc_peer_experts")
    def run(table_hbm, ids_hbm, gates_hbm, h_hbm, peer_hbm, ids0, ids1, g0, g1, h0, h1, rows0, rows1,
            out_v, gsem, isem):
        bufs = (rows0, rows1)
        ids_p, g_p, h_p = (ids0, ids1), (g0, g1), (h0, h1)
        t_first = (lax.axis_index("s") * info.num_cores + lax.axis_index("c")) * tok_per_worker

        def token_inputs(t, p):
            off = pl.multiple_of(t * PEER_SLOTS, PEER_SLOTS)
            return (pltpu.make_async_copy(ids_hbm.at[pl.ds(off, PEER_SLOTS)], ids_p[p], isem.at[p]),
                    pltpu.make_async_copy(gates_hbm.at[pl.ds(off, PEER_SLOTS)], g_p[p], isem.at[p]),
                    pltpu.make_async_copy(h_hbm.at[t], h_p[p], isem.at[p]))

        def gather(p, s):
            return pltpu.make_async_copy(table_hbm.at[ids_p[p].at[pl.ds(s * rows, rows)]],
                                         bufs[s % 2], gsem.at[s % 2])

        for cp in token_inputs(t_first, 0):
            cp.start()
        for cp in token_inputs(t_first, 0):
            cp.wait()
        gather(0, 0).start()

        def one_token(ti, p):
            t = t_first + ti
            g_v, h_v = g_p[p], h_p[p]
            has_next = ti + 1 < tok_per_worker

            @pl.when(has_next)
            def _():
                for cp in token_inputs(t + 1, 1 - p):
                    cp.start()

            for c, k in pieces:
                out_v[c, pl.ds(k, sl)] = jnp.zeros((sl,), jnp.float32)
            for s in range(n_sub):
                gather(p, s).wait()
                if s + 1 < n_sub:
                    gather(p, s + 1).start()
                else:
                    @pl.when(has_next)
                    def _():
                        for cp in token_inputs(t + 1, 1 - p):
                            cp.wait()
                        gather(1 - p, 0).start()
                rbuf = bufs[s % 2]

                @pl.loop(0, rows, step=SC_ROW_GROUP)
                def _(r0):
                    accs = [jnp.zeros((sl,), jnp.float32) for _ in range(SC_ROW_GROUP)]
                    for c, k in pieces:
                        hv = h_v[c, pl.ds(k, sl)]
                        for j in range(SC_ROW_GROUP):
                            w = rbuf[r0 + j, c, pl.ds(k, sl)]
                            u = lax.bitcast_convert_type(w & jnp.uint32(0xFFFF0000), jnp.float32)
                            accs[j] = accs[j] + u * hv
                    coefs = []
                    for j in range(SC_ROW_GROUP):
                        a = jnp.broadcast_to(jnp.sum(accs[j]), (sl,))
                        act = 0.5 * a * (1.0 + _sc_tanh(0.7978845608028654 * (a + 0.044715 * (a * a * a))))
                        gate = plsc.load_gather(g_v, [jnp.broadcast_to(s * rows + r0 + j, (sl,))])
                        coefs.append(gate * act)
                    for b0 in range(0, len(pieces), SC_STORE_BATCH):
                        tots = []
                        for c, k in pieces[b0:b0 + SC_STORE_BATCH]:
                            tot = None
                            for j in range(SC_ROW_GROUP):
                                w = rbuf[r0 + j, c, pl.ds(k, sl)]
                                term = coefs[j] * lax.bitcast_convert_type(w << 16, jnp.float32)
                                tot = term if tot is None else tot + term
                            tots.append(tot)
                        for (c, k), tot in zip(pieces[b0:b0 + SC_STORE_BATCH], tots):
                            plsc.addupdate(out_v.at[c, pl.ds(k, sl)], tot)

            pltpu.sync_copy(out_v, peer_hbm.at[t])

        @pl.loop(0, tok_per_worker, step=2)
        def _(ti):
            one_token(ti, 0)
            one_token(ti + 1, 1)

    return run(table3, ids, gates, h3)


def _finish_kernel(peer_ref, x_ref, g2_ref, fg_ref, prev_ref, out_ref):
    del prev_ref
    _finish_block(x_ref, g2_ref, fg_ref, peer_ref, out_ref)


def _finish_tokens(peer, x1, g2, fg, prev, seq_len):
    d = x1.shape[1]
    tm = FINISH_TOK_BLOCK
    return pl.pallas_call(
        _finish_kernel,
        grid=(peer.shape[0] // tm,),
        in_specs=[pl.BlockSpec((tm, d), lambda i: (i, 0)),
                  pl.BlockSpec((tm, d), lambda i: (i, 0)),
                  pl.BlockSpec((1, 1, d), lambda i: (i * tm // seq_len, 0, 0)),
                  pl.BlockSpec((1, d), lambda i: (0, 0)),
                  pl.BlockSpec(memory_space=pl.ANY)],
        out_specs=pl.BlockSpec((tm, d), lambda i: (i, 0)),
        out_shape=jax.ShapeDtypeStruct(prev.shape, prev.dtype),
        input_output_aliases={4: 0},
        compiler_params=pltpu.CompilerParams(dimension_semantics=("arbitrary",)),
        name="finish_tokens",
    )(peer, x1, g2, fg, prev)


def _pack_expert_table(expert_u, expert_v):
    ub = lax.bitcast_convert_type(expert_u.astype(jnp.bfloat16), jnp.uint16).astype(jnp.uint32)
    vb = lax.bitcast_convert_type(expert_v.astype(jnp.bfloat16), jnp.uint16).astype(jnp.uint32)
    packed = (ub << 16) | vb
    n_experts, d = packed.shape
    return packed.reshape(n_experts * (d // LANES), LANES)


def _tc_peer_experts(idx, gates, h2, x1, g2, final_g, table, seq_len, first_tok):
    n, d = h2.shape
    m = n - first_tok
    n_chunks = d // LANES
    tb = EXPERT_TOK_BLOCK
    assert tb % EXPERT_SLOTS == 0 and seq_len % tb == 0 and first_tok % tb == 0
    first = first_tok // tb
    return pl.pallas_call(
        functools.partial(_expert_kernel, tok_block=tb, n_slots=EXPERT_SLOTS),
        grid=(m // tb,),
        in_specs=[
            pl.BlockSpec((tb, PEER_SLOTS), lambda i: (i + first, 0), memory_space=pltpu.SMEM),
            pl.BlockSpec((tb, PEER_SLOTS), lambda i: (i + first, 0)),
            pl.BlockSpec((tb, d), lambda i: (i + first, 0)),
            pl.BlockSpec((tb, d), lambda i: (i + first, 0)),
            pl.BlockSpec((1, 1, d), lambda i: ((i + first) * tb // seq_len, 0, 0)),
            pl.BlockSpec((1, d), lambda i: (0, 0)),
            pl.BlockSpec(memory_space=pl.ANY),
        ],
        out_specs=pl.BlockSpec((tb, d), lambda i: (i + first, 0)),
        out_shape=jax.ShapeDtypeStruct((n, d), jnp.float32),
        scratch_shapes=(
            [pltpu.VMEM((PEER_SLOTS * n_chunks, LANES), jnp.uint32) for _ in range(EXPERT_SLOTS)]
            + [pltpu.VMEM((tb, d), jnp.float32), pltpu.SemaphoreType.DMA((EXPERT_SLOTS,))]),
        compiler_params=pltpu.CompilerParams(dimension_semantics=("arbitrary",)),
        cost_estimate=_expert_cost(m, d),
        name="peer_experts",
    )(idx, gates, h2, x1, g2, final_g.reshape(1, d), table)


def _mix_and_route(x, mod, norm1_g, w_in, b_forget, ln_v_g, w_spatial, b_spatial, w_branch_a,
                   w_branch_b, w_out, norm2_g, w_query, sub_keys):
    Bp, S, D = x.shape
    m = Bp * S
    bf16 = jnp.bfloat16
    qkv, cum, sga, gb = _input_projection(x, mod, norm1_g, w_in, b_forget, ln_v_g,
                                          w_spatial, b_spatial, w_branch_b)
    heads = lambda a: a.reshape(Bp, S, ATT_HEADS, ATT_HEAD_DIM).transpose(0, 2, 1, 3)
    q = heads(qkv[:, 0:ATT_WIDTH])
    k = heads(qkv[:, ATT_WIDTH:2 * ATT_WIDTH])
    v = heads(qkv[:, 2 * ATT_WIDTH:])
    cum_h = cum[:, 0:ATT_HEADS].reshape(Bp, S, ATT_HEADS).transpose(0, 2, 1)
    y_a = _fox_attention(q, k, v, cum_h).transpose(0, 2, 1, 3).reshape(m, ATT_WIDTH)
    x1, h2, idx_t, gates_t = _post_attention(
        y_a, sga, gb, x.reshape(m, D), mod, norm2_g.reshape(1, D),
        w_branch_a.astype(bf16), w_out.astype(bf16), w_query.astype(bf16), sub_keys, S)
    return x1, h2, idx_t.T, gates_t.T


def kernel(x, c, w_mod, b_mod, norm1_g, w_in, b_forget, ln_v_g, w_spatial, b_spatial, w_branch_a, w_branch_b, w_out, norm2_g, w_query, sub_keys, expert_u, expert_v, final_g):
    B, S, D = x.shape
    n = B * S
    assert w_mod.shape[0] == 1, "the final RMSNorm is fused into the single layer's expert kernels"
    l = 0
    mod = _modulation(c, w_mod[l], b_mod[l]).reshape(B, 6, D)
    table = _pack_expert_table(expert_u[l], expert_v[l])
    n_chunks = D // LANES
    weights = (norm1_g[l], w_in[l], b_forget[l], ln_v_g[l], w_spatial[l], b_spatial[l],
               w_branch_a[l], w_branch_b[l], w_out[l], norm2_g[l], w_query[l], sub_keys[l])

    x1, h2, idx, gates = _mix_and_route(x, mod, *weights)
    g2 = mod[:, 5:6, :]

    n_sc = n * SC_SHARE_PERCENT // 100 // SC_SHARE_ALIGN * SC_SHARE_ALIGN
    out = _tc_peer_experts(idx, gates, h2, x1, g2, final_g, table, S, n_sc)
    if n_sc > 0:
        peer_sc = _sc_peer_experts(table.reshape(-1, n_chunks, LANES), idx[:n_sc].reshape(-1),
                                   gates[:n_sc].reshape(-1), h2[:n_sc].reshape(n_sc, n_chunks, LANES))
        out = _finish_tokens(peer_sc.reshape(n_sc, D), x1, g2, final_g.reshape(1, D), out, S)
    return out.reshape(B, S, D)
```

```python
import functools

import jax
import jax.numpy as jnp
from jax import lax
from jax.experimental import pallas as pl
from jax.experimental.pallas import tpu as pltpu
from jax.experimental.pallas import tpu_sc as plsc

D_MODEL = 1024
ATT_HEADS = 8
ATT_HEAD_DIM = 64
ATT_WIDTH = ATT_HEADS * ATT_HEAD_DIM
Q_BLOCK = 128
GM_GROUPS = 4
GM_GROUP_DIM = 128
GM_WIDTH = GM_GROUPS * GM_GROUP_DIM
GM_CHUNK = 128
PEER_HEADS = 8
PEER_KEY_DIM = 256
PEER_HALF = PEER_KEY_DIM // 2
N_KEYS = 128
PEER_TOPK = 16
PEER_TOK_BLOCK = 128
PEER_SLOTS = PEER_HEADS * PEER_TOPK
SPLIT_POINTS = (ATT_WIDTH, 2 * ATT_WIDTH, 3 * ATT_WIDTH, 3 * ATT_WIDTH + ATT_HEADS,
                3 * ATT_WIDTH + ATT_HEADS + 2 * GM_WIDTH,
                3 * ATT_WIDTH + ATT_HEADS + 2 * GM_WIDTH + D_MODEL)
EPS = 1e-6

LANES = 128
EXPERT_TOK_BLOCK = 128
EXPERT_SLOTS = 4
DMA_THREADS = 2
FINISH_TOK_BLOCK = 512
SC_GATHER_ROWS = 32
SC_ROW_GROUP = 4
SC_STORE_BATCH = 8
SC_SHARE_PERCENT = 50
SC_SHARE_ALIGN = 1024


def _gelu(x):
    return 0.5 * x * (1.0 + jnp.tanh(0.7978845608028654 * (x + 0.044715 * (x * x * x))))


def _rmsnorm(x, g):
    y = x * lax.rsqrt(jnp.mean(x * x, axis=-1, keepdims=True) + EPS)
    return y * g


def _layernorm(x, g):
    mu = jnp.mean(x, axis=-1, keepdims=True)
    var = jnp.mean(jnp.square(x - mu), axis=-1, keepdims=True)
    return (x - mu) * lax.rsqrt(var + EPS) * g


def _modulate(h, shift, scale):
    return h * (1.0 + scale[:, None, :]) + shift[:, None, :]


def _fox(q, k, v, log_f):
    B, S, H, Dh = q.shape
    nb = S // Q_BLOCK
    cum = jnp.cumsum(log_f, axis=1)
    cum_k = jnp.transpose(cum, (0, 2, 1))
    k_pos = jnp.arange(S)
    q_blocks = q.reshape(B, nb, Q_BLOCK, H, Dh).transpose(1, 0, 2, 3, 4)
    cq_blocks = cum_k.reshape(B, H, nb, Q_BLOCK).transpose(2, 0, 1, 3)
    starts = jnp.arange(nb) * Q_BLOCK
    scale = ATT_HEAD_DIM ** -0.5

    def one_block(args):
        q_blk, cq_blk, start = args
        s = jnp.einsum('bqhd,bkhd->bhqk', q_blk, k) * scale
        s = s + cq_blk[..., :, None] - cum_k[..., None, :]
        q_pos = start + jnp.arange(Q_BLOCK)
        causal = q_pos[:, None] >= k_pos[None, :]
        s = jnp.where(causal[None, None], s, -jnp.inf)
        p = jax.nn.softmax(s, axis=-1)
        return jnp.einsum('bhqk,bkhd->bqhd', p, v)

    out = lax.map(one_block, (q_blocks, cq_blocks, starts))
    return out.transpose(1, 0, 2, 3, 4).reshape(B, S, H * Dh)


def _gmlp(z, ln_g, w_s, b_s):
    B, S, _ = z.shape
    u, v = jnp.split(z, 2, axis=-1)
    v = _layernorm(v, ln_g)
    nc = S // GM_CHUNK
    v = v.reshape(B, nc, GM_CHUNK, GM_GROUPS, GM_GROUP_DIM)
    tril = jnp.tril(jnp.ones((GM_CHUNK, GM_CHUNK), dtype=bool))
    w = jnp.where(tril[None], w_s, jnp.zeros_like(w_s))
    mixed = jnp.einsum('gts,bcsgd->bctgd', w, v) + b_s.T[None, None, :, :, None]
    return u * mixed.reshape(B, S, GM_WIDTH)


def _peer_route(hf, w_query, sub_keys):
    n = hf.shape[0]
    q = (hf @ w_query).reshape(n, PEER_HEADS, 2, PEER_HALF)
    scores = jnp.einsum('nhpd,hpkd->nhpk', q, sub_keys)
    s_top, i_top = lax.top_k(scores, PEER_TOPK)
    cand_s = (s_top[:, :, 0, :, None] + s_top[:, :, 1, None, :]).reshape(n, PEER_HEADS, PEER_TOPK * PEER_TOPK)
    cand_i = (i_top[:, :, 0, :, None] * N_KEYS + i_top[:, :, 1, None, :]).reshape(n, PEER_HEADS, PEER_TOPK * PEER_TOPK)
    best_s, best_pos = lax.top_k(cand_s, PEER_TOPK)
    idx = jnp.take_along_axis(cand_i, best_pos, axis=-1)
    gates = jax.nn.softmax(best_s, axis=-1)
    return idx.reshape(n, PEER_SLOTS), gates.reshape(n, PEER_SLOTS)


def _mod_kernel(c_ref, w_ref, b_ref, o_ref):
    c = c_ref[...]
    sc = c * jax.nn.sigmoid(c)
    o_ref[...] = jnp.dot(sc, w_ref[...], precision=lax.Precision.HIGHEST,
                         preferred_element_type=jnp.float32) + b_ref[...]


def _modulation(c, w_mod, b_mod):
    b, d = c.shape
    cols = w_mod.shape[1]
    return pl.pallas_call(
        _mod_kernel,
        grid=(cols // d,),
        in_specs=[pl.BlockSpec((b, d), lambda j: (0, 0)),
                  pl.BlockSpec((d, d), lambda j: (0, j)),
                  pl.BlockSpec((1, d), lambda j: (0, j))],
        out_specs=pl.BlockSpec((b, d), lambda j: (0, j)),
        out_shape=jax.ShapeDtypeStruct((b, cols), jnp.float32),
        name="modulation",
    )(c, w_mod, b_mod.reshape(1, cols))


INPROJ_TOK_BLOCK = 256


def _inproj_kernel(x_ref, mod_ref, n1g_ref, wqkv_ref, wf_ref, bf_ref, wz_ref, wg_ref, lng_ref,
                   wsp_ref, bsp_ref, wb_ref, qkv_ref, cum_ref, sga_ref, gb_ref, carry):
    f32, bf16 = jnp.float32, jnp.bfloat16
    tm, d = x_ref.shape[1], x_ref.shape[2]
    x = x_ref[0]
    sh1 = mod_ref[0, 0:1, :]
    sc1 = mod_ref[0, 1:2, :]
    h = x * lax.rsqrt(jnp.mean(x * x, axis=-1, keepdims=True) + EPS) * n1g_ref[...]
    hb = (h * (1.0 + sc1) + sh1).astype(bf16)

    qkv = jnp.dot(hb, wqkv_ref[...], preferred_element_type=f32)
    qkv_ref[:, 0:ATT_WIDTH] = (qkv[:, 0:ATT_WIDTH] * (ATT_HEAD_DIM ** -0.5)).astype(bf16)
    qkv_ref[:, ATT_WIDTH:] = qkv[:, ATT_WIDTH:].astype(bf16)

    f = jnp.dot(hb, wf_ref[...], preferred_element_type=f32) + bf_ref[...]
    logf = jnp.minimum(f, 0.0) - jnp.log1p(jnp.exp(-jnp.abs(f)))

    @pl.when(pl.program_id(1) == 0)
    def _():
        carry[...] = jnp.zeros_like(carry)

    tri = (lax.broadcasted_iota(jnp.int32, (tm, tm), 0)
           >= lax.broadcasted_iota(jnp.int32, (tm, tm), 1)).astype(f32)
    cum = jnp.dot(tri, logf, precision=lax.Precision.HIGHEST, preferred_element_type=f32) + carry[...]
    cum_ref[...] = cum
    carry[...] = cum[tm - 1:tm, :]

    gz = _gelu(jnp.dot(hb, wz_ref[...], preferred_element_type=f32))
    u = gz[:, 0:GM_WIDTH]
    v = gz[:, GM_WIDTH:]
    mu = jnp.mean(v, axis=-1, keepdims=True)
    var = jnp.mean(jnp.square(v - mu), axis=-1, keepdims=True)
    vn = ((v - mu) * lax.rsqrt(var + EPS) * lng_ref[...]).astype(bf16)
    tril = (lax.broadcasted_iota(jnp.int32, (GM_CHUNK, GM_CHUNK), 0)
            >= lax.broadcasted_iota(jnp.int32, (GM_CHUNK, GM_CHUNK), 1))
    w_sp = [jnp.where(tril, wsp_ref[g], 0.0).astype(bf16) for g in range(GM_GROUPS)]
    rows = []
    for ck in range(tm // GM_CHUNK):
        r0 = ck * GM_CHUNK
        cols = []
        for g in range(GM_GROUPS):
            c0 = g * GM_GROUP_DIM
            mixed = jnp.dot(w_sp[g], vn[r0:r0 + GM_CHUNK, c0:c0 + GM_GROUP_DIM],
                            preferred_element_type=f32) + bsp_ref[g]
            cols.append(u[r0:r0 + GM_CHUNK, c0:c0 + GM_GROUP_DIM] * mixed)
        rows.append(jnp.concatenate(cols, axis=1))
    yb = jnp.concatenate(rows, axis=0).astype(bf16)
    ybp = jnp.dot(yb, wb_ref[...], preferred_element_type=f32)

    sg = jax.nn.sigmoid(jnp.dot(hb, wg_ref[...], preferred_element_type=f32))
    sga_ref[...] = sg[:, 0:d].astype(bf16)
    gb_ref[...] = (sg[:, d:] * ybp).astype(bf16)


def _input_projection(x, mod, norm1_g, w_in, b_forget, ln_v_g, w_spatial, b_spatial, w_branch_b):
    B, S, d = x.shape
    n = B * S
    tm = INPROJ_TOK_BLOCK
    bf16 = jnp.bfloat16
    p0, p1, p2, p3, p4, p5 = SPLIT_POINTS
    w_qkv = w_in[:, 0:p2].astype(bf16)
    w_f = jnp.pad(w_in[:, p2:p3], ((0, 0), (0, LANES - ATT_HEADS))).astype(bf16)
    b_f = jnp.pad(b_forget, (0, LANES - ATT_HEADS)).reshape(1, LANES)
    w_z = w_in[:, p3:p4].astype(bf16)
    w_g = w_in[:, p4:].astype(bf16)
    nt = S // tm
    tok = lambda w: pl.BlockSpec((tm, w), lambda b, i: (b * nt + i, 0))
    full = lambda a: pl.BlockSpec(a.shape, lambda b, i: (0,) * a.ndim)
    args = (x, mod, norm1_g.reshape(1, d), w_qkv, w_f, b_f, w_z, w_g, ln_v_g.reshape(1, GM_WIDTH),
            w_spatial, b_spatial.reshape(GM_GROUPS, GM_CHUNK, 1), w_branch_b.astype(bf16))
    return pl.pallas_call(
        _inproj_kernel,
        grid=(B, nt),
        in_specs=[pl.BlockSpec((1, tm, d), lambda b, i: (b, i, 0)),
                  pl.BlockSpec((1, 6, d), lambda b, i: (b, 0, 0))] + [full(a) for a in args[2:]],
        out_specs=[tok(3 * ATT_WIDTH), tok(LANES), tok(d), tok(d)],
        out_shape=[jax.ShapeDtypeStruct((n, 3 * ATT_WIDTH), bf16),
                   jax.ShapeDtypeStruct((n, LANES), jnp.float32),
                   jax.ShapeDtypeStruct((n, d), bf16),
                   jax.ShapeDtypeStruct((n, d), bf16)],
        scratch_shapes=[pltpu.VMEM((1, LANES), jnp.float32)],
        compiler_params=pltpu.CompilerParams(dimension_semantics=("arbitrary", "arbitrary"),
                                             vmem_limit_bytes=56 * 1024 * 1024),
        name="input_projection",
    )(*args)


ATT_BLOCK = 512


def _fox_kernel(q_ref, k_ref, v_ref, cq_ref, ck_ref, o_ref, *, blk):
    f32 = jnp.float32
    i = pl.program_id(2)
    q = q_ref[0, 0]
    cq = cq_ref[0, 0]

    def block(j, carry, masked):
        m, l, acc = carry
        off = pl.multiple_of(j * blk, blk)
        k = k_ref[0, 0, pl.ds(off, blk), :]
        v = v_ref[0, 0, pl.ds(off, blk), :]
        s = lax.dot_general(q, k, (((1,), (1,)), ((), ())), preferred_element_type=f32)
        s = s + (cq - ck_ref[0, 0, j])
        if masked:
            causal = (lax.broadcasted_iota(jnp.int32, (blk, blk), 0)
                      >= lax.broadcasted_iota(jnp.int32, (blk, blk), 1))
            s = jnp.where(causal, s, -jnp.inf)
        m_new = jnp.maximum(m, jnp.max(s, axis=1, keepdims=True))
        alpha = jnp.exp(m - m_new)
        p = jnp.exp(s - m_new)
        l = alpha * l + jnp.sum(p, axis=1, keepdims=True)
        acc = alpha * acc + jnp.dot(p.astype(v.dtype), v, preferred_element_type=f32)
        return m_new, l, acc

    init = (jnp.full((blk, 1), -1e30, f32), jnp.zeros((blk, 1), f32),
            jnp.zeros((blk, q.shape[1]), f32))
    carry = lax.fori_loop(0, i, lambda j, c: block(j, c, False), init)
    m, l, acc = block(i, carry, True)
    o_ref[0, 0] = (acc / l).astype(o_ref.dtype)


def _fox_attention(q, k, v, cum):
    B, H, S, dh = q.shape
    blk = min(ATT_BLOCK, S)
    nb = S // blk
    cq = cum.reshape(B, H, S, 1)
    ck = cum.reshape(B, H, nb, 1, blk)
    return pl.pallas_call(
        functools.partial(_fox_kernel, blk=blk),
        grid=(B, H, nb),
        in_specs=[pl.BlockSpec((1, 1, blk, dh), lambda b, h, i: (b, h, i, 0)),
                  pl.BlockSpec((1, 1, S, dh), lambda b, h, i: (b, h, 0, 0)),
                  pl.BlockSpec((1, 1, S, dh), lambda b, h, i: (b, h, 0, 0)),
                  pl.BlockSpec((1, 1, blk, 1), lambda b, h, i: (b, h, i, 0)),
                  pl.BlockSpec((1, 1, nb, 1, blk), lambda b, h, i: (b, h, 0, 0, 0))],
        out_specs=pl.BlockSpec((1, 1, blk, dh), lambda b, h, i: (b, h, i, 0)),
        out_shape=jax.ShapeDtypeStruct((B, H, S, dh), jnp.bfloat16),
        compiler_params=pltpu.CompilerParams(
            dimension_semantics=("arbitrary", "arbitrary", "arbitrary")),
        name="fox_attention",
    )(q, k, v, cq, ck)


def _topk_rows(s, k):
    rows, t = s.shape
    iota = lax.broadcasted_iota(jnp.int32, (rows, t), 0)
    slot = lax.broadcasted_iota(jnp.int32, (k, t), 0)
    vals = jnp.zeros((k, t), jnp.float32)
    ids = jnp.zeros((k, t), jnp.int32)
    for j in range(k):
        m = jnp.max(s, axis=0, keepdims=True)
        am = jnp.min(jnp.where(s == m, iota, rows), axis=0, keepdims=True)
        vals = jnp.where(slot == j, m, vals)
        ids = jnp.where(slot == j, am, ids)
        s = jnp.where(iota == am, -jnp.inf, s)
    return vals, ids


def _select_rows(table, pos):
    out = jnp.zeros(pos.shape, table.dtype)
    for r in range(table.shape[0]):
        out = jnp.where(pos == r, table[r:r + 1, :], out)
    return out


def _post_kernel(ya_ref, sga_ref, gb_ref, x_ref, mod_ref, n2g_ref, wa_ref, wo_ref, wq_ref, keys_ref,
                 x1_ref, h2_ref, idx_ref, gates_ref, q_scr):
    f32 = jnp.float32
    a = jnp.dot(ya_ref[...], wa_ref[...], preferred_element_type=f32)
    merged = sga_ref[...].astype(f32) * a + gb_ref[...].astype(f32)
    o = jnp.dot(merged.astype(jnp.bfloat16), wo_ref[...], preferred_element_type=f32)
    g1 = mod_ref[0, 2:3, :]
    sh2 = mod_ref[0, 3:4, :]
    sc2 = mod_ref[0, 4:5, :]
    x1 = x_ref[...] + g1 * o
    x1_ref[...] = x1
    h2 = x1 * lax.rsqrt(jnp.mean(x1 * x1, axis=-1, keepdims=True) + EPS) * n2g_ref[...]
    h2 = h2 * (1.0 + sc2) + sh2
    h2_ref[...] = h2
    qp = jnp.dot(h2.astype(jnp.bfloat16), wq_ref[...], preferred_element_type=f32)
    for j in range(2 * PEER_HEADS):
        q_scr[j] = qp[:, j * PEER_HALF:(j + 1) * PEER_HALF]

    def head(h, carry):
        tops = []
        for p in range(2):
            q = q_scr[2 * h + p]
            keys = keys_ref[2 * h + p]
            sc = lax.dot_general(keys, q, (((1,), (1,)), ((), ())),
                                 precision=lax.Precision.HIGHEST,
                                 preferred_element_type=f32)
            tops.append(_topk_rows(sc, PEER_TOPK))
        (s1, i1), (s2, i2) = tops
        half = PEER_TOPK // 2
        blocks = [s1[0:1, :] + s2]
        blocks += [s1[a:a + 1, :] + s2[0:half, :] for a in range(1, half)]
        blocks += [s1[half:, :] + s2[0:1, :]]
        vals, pos = _topk_rows(jnp.concatenate(blocks, axis=0), PEER_TOPK)
        mid = pos - PEER_TOPK
        tail0 = PEER_TOPK + (half - 1) * half
        ra = jnp.where(pos < PEER_TOPK, 0,
                       jnp.where(pos < tail0, 1 + (mid >> (half.bit_length() - 1)), pos - tail0 + half))
        rb = jnp.where(pos < PEER_TOPK, pos, jnp.where(pos < tail0, mid & (half - 1), 0))
        eid = _select_rows(i1, ra) * N_KEYS + _select_rows(i2, rb)
        e = jnp.exp(vals - vals[0:1, :])
        g = e / jnp.sum(e, axis=0, keepdims=True)
        row = pl.multiple_of(h * PEER_TOPK, PEER_TOPK)
        idx_ref[pl.ds(row, PEER_TOPK), :] = eid
        gates_ref[pl.ds(row, PEER_TOPK), :] = g
        return carry

    lax.fori_loop(0, PEER_HEADS, head, 0)


POST_TOK_BLOCK = 256


def _post_attention(ya, sga, gb, x, mod, norm2_g, w_a, w_out, w_query, sub_keys, seq_len):
    n, d = x.shape
    tm = POST_TOK_BLOCK
    blocks_per_seq = seq_len // tm
    aw = ya.shape[1]
    qw = w_query.shape[1]
    keys = sub_keys.reshape(2 * PEER_HEADS, N_KEYS, PEER_HALF)
    tok = lambda w: pl.BlockSpec((tm, w), lambda i: (i, 0))
    full = lambda a: pl.BlockSpec(a.shape, lambda i: (0,) * a.ndim)
    return pl.pallas_call(
        _post_kernel,
        grid=(n // tm,),
        in_specs=[tok(aw), tok(d), tok(d), tok(d),
                  pl.BlockSpec((1, 6, d), lambda i: (i // blocks_per_seq, 0, 0)),
                  full(norm2_g), full(w_a), full(w_out), full(w_query), full(keys)],
        out_specs=[tok(d), tok(d),
                   pl.BlockSpec((PEER_SLOTS, tm), lambda i: (0, i)),
                   pl.BlockSpec((PEER_SLOTS, tm), lambda i: (0, i))],
        out_shape=[jax.ShapeDtypeStruct((n, d), jnp.float32),
                   jax.ShapeDtypeStruct((n, d), jnp.float32),
                   jax.ShapeDtypeStruct((PEER_SLOTS, n), jnp.int32),
                   jax.ShapeDtypeStruct((PEER_SLOTS, n), jnp.float32)],
        scratch_shapes=[pltpu.VMEM((2 * PEER_HEADS, tm, PEER_HALF), jnp.float32)],
        compiler_params=pltpu.CompilerParams(dimension_semantics=("arbitrary",),
                                             vmem_limit_bytes=48 * 1024 * 1024),
        name="post_attention",
    )(ya, sga, gb, x, mod, norm2_g, w_a, w_out, w_query, keys)


def _rowsum_bcast(p, ones_bf16):
    hi = p.astype(jnp.bfloat16)
    lo = (p - hi.astype(jnp.float32)).astype(jnp.bfloat16)
    return (jnp.dot(hi, ones_bf16, preferred_element_type=jnp.float32)
            + jnp.dot(lo, ones_bf16, preferred_element_type=jnp.float32))


def _eval_experts(chunk, hrow, grow):
    n_chunks = hrow.shape[1] // LANES
    ones_bf16 = jnp.ones((LANES, LANES), jnp.bfloat16)
    eye = (lax.broadcasted_iota(jnp.int32, (PEER_SLOTS, LANES), 0)
           == lax.broadcasted_iota(jnp.int32, (PEER_SLOTS, LANES), 1))
    psum = None
    for c in range(n_chunks):
        u = lax.bitcast_convert_type(chunk(c) & jnp.uint32(0xFFFF0000), jnp.float32)
        p = u * hrow[:, c * LANES:(c + 1) * LANES]
        psum = p if psum is None else psum + p
    act = _gelu(_rowsum_bcast(psum, ones_bf16))
    gcol = _rowsum_bcast(jnp.where(eye, grow, 0.0), ones_bf16)
    coef = gcol * act
    outs = []
    for c in range(n_chunks):
        v = lax.bitcast_convert_type(chunk(c) << 16, jnp.float32)
        outs.append(jnp.sum(coef * v, axis=0, keepdims=True))
    return jnp.concatenate(outs, axis=-1)


def _finish_block(x_ref, g2_ref, fg_ref, peer, out_ref):
    y = x_ref[...] + g2_ref[0] * peer[...]
    out_ref[...] = y * lax.rsqrt(jnp.mean(y * y, axis=-1, keepdims=True) + EPS) * fg_ref[...]


def _expert_kernel(idx_ref, gates_ref, h_ref, x_ref, g2_ref, fg_ref, tab_ref, out_ref,
                   *scratch, tok_block, n_slots):
    bufs = scratch[:n_slots]
    peer, sem = scratch[n_slots], scratch[n_slots + 1]
    d_model = h_ref.shape[-1]
    n_chunks = d_model // LANES
    rows_per_tok = PEER_SLOTS * n_chunks

    def issue(t, s):
        for r in range(PEER_SLOTS):
            row = pl.multiple_of(idx_ref[t, r] * n_chunks, n_chunks)
            pltpu.make_async_copy(tab_ref.at[pl.ds(row, n_chunks), :],
                                  bufs[s].at[pl.ds(r * n_chunks, n_chunks), :],
                                  sem.at[s]).start(priority=r % DMA_THREADS)

    def wait(s):
        pltpu.make_async_copy(tab_ref.at[pl.ds(0, rows_per_tok), :], bufs[s], sem.at[s]).wait()

    def compute(t, s):
        chunk = lambda c: bufs[s][pl.ds(c, PEER_SLOTS, stride=n_chunks), :]
        peer[pl.ds(t, 1), :] = _eval_experts(chunk, h_ref[pl.ds(t, 1), :], gates_ref[pl.ds(t, 1), :])

    def step(t, s, prefetch):
        wait(s)
        if prefetch:
            issue(t + n_slots - 1, (s - 1) % n_slots)
        compute(t, s)

    for t in range(n_slots - 1):
        issue(t, t)

    n_groups = tok_block // n_slots

    def group(g, carry):
        for s in range(n_slots):
            step(g * n_slots + s, s, True)
        return carry

    lax.fori_loop(0, n_groups - 1, group, 0)
    for s in range(n_slots):
        t = (n_groups - 1) * n_slots + s
        step(t, s, t + n_slots - 1 < tok_block)

    _finish_block(x_ref, g2_ref, fg_ref, peer, out_ref)


def _expert_cost(n_tok, d):
    pairs = n_tok * PEER_SLOTS
    return pl.CostEstimate(flops=4 * pairs * d, transcendentals=pairs,
                           bytes_accessed=4 * pairs * d + 12 * n_tok * d + 8 * pairs)


def _sc_tanh(y):
    return 1.0 - 2.0 / (jnp.exp(2.0 * y) + 1.0)


def _sc_peer_experts(table3, ids, gates, h3):
    n_tok, n_chunks, lanes = h3.shape
    info = plsc.get_sparse_core_info()
    sl = info.num_lanes
    n_workers = info.num_cores * info.num_subcores
    tok_per_worker = n_tok // n_workers
    assert tok_per_worker * n_workers == n_tok and tok_per_worker % 2 == 0
    rows = SC_GATHER_ROWS
    n_sub = PEER_SLOTS // rows
    assert n_sub % 2 == 0
    half = n_chunks // 2
    pieces = [(c, k * sl) for c in range(half) for k in range(lanes // sl)]
    mesh = plsc.VectorSubcoreMesh(core_axis_name="c", subcore_axis_name="s")

    def halves(w):
        return (lax.bitcast_convert_type(w & jnp.uint32(0xFFFF0000), jnp.float32),
                lax.bitcast_convert_type(w << 16, jnp.float32))

    buf = lambda dt: pltpu.VMEM((rows, n_chunks, lanes), dt)

    @functools.partial(
        pl.kernel, mesh=mesh,
        out_type=jax.ShapeDtypeStruct((n_tok, n_chunks, lanes), jnp.float32),
        scratch_types=[pltpu.VMEM((PEER_SLOTS,), jnp.int32), pltpu.VMEM((PEER_SLOTS,), jnp.int32),
                       pltpu.VMEM((PEER_SLOTS,), jnp.float32), pltpu.VMEM((PEER_SLOTS,), jnp.float32),
                       pltpu.VMEM((n_chunks, lanes), jnp.float32),
                       pltpu.VMEM((n_chunks, lanes), jnp.float32),
                       buf(jnp.uint32), buf(jnp.uint32), pltpu.VMEM((n_chunks, lanes), jnp.float32),
                       pltpu.SemaphoreType.DMA((2,)), pltpu.SemaphoreType.DMA((2,))],
        compiler_params=pltpu.CompilerParams(needs_layout_passes=False),
        cost_estimate=_expert_cost(n_tok, n_chunks * lanes),
        name="sc_peer_experts")
    def run(table_hbm, ids_hbm, gates_hbm, h_hbm, peer_hbm, ids0, ids1, g0, g1, h0, h1, rows0, rows1,
            out_v, gsem, isem):
        bufs = (rows0, rows1)
        ids_p, g_p, h_p = (ids0, ids1), (g0, g1), (h0, h1)
        t_first = (lax.axis_index("s") * info.num_cores + lax.axis_index("c")) * tok_per_worker

        def token_inputs(t, p):
            off = pl.multiple_of(t * PEER_SLOTS, PEER_SLOTS)
            return (pltpu.make_async_copy(ids_hbm.at[pl.ds(off, PEER_SLOTS)], ids_p[p], isem.at[p]),
                    pltpu.make_async_copy(gates_hbm.at[pl.ds(off, PEER_SLOTS)], g_p[p], isem.at[p]),
                    pltpu.make_async_copy(h_hbm.at[t], h_p[p], isem.at[p]))

        def gather(p, s):
            return pltpu.make_async_copy(table_hbm.at[ids_p[p].at[pl.ds(s * rows, rows)]],
                                         bufs[s % 2], gsem.at[s % 2])

        for cp in token_inputs(t_first, 0):
            cp.start()
        for cp in token_inputs(t_first, 0):
            cp.wait()
        gather(0, 0).start()

        def one_token(ti, p):
            t = t_first + ti
            g_v, h_v = g_p[p], h_p[p]
            has_next = ti + 1 < tok_per_worker

            @pl.when(has_next)
            def _():
                for cp in token_inputs(t + 1, 1 - p):
                    cp.start()

            for c, k in pieces:
                out_v[c, pl.ds(k, sl)] = jnp.zeros((sl,), jnp.float32)
                out_v[c + half, pl.ds(k, sl)] = jnp.zeros((sl,), jnp.float32)
            for s in range(n_sub):
                gather(p, s).wait()
                if s + 1 < n_sub:
                    gather(p, s + 1).start()
                else:
                    @pl.when(has_next)
                    def _():
                        for cp in token_inputs(t + 1, 1 - p):
                            cp.wait()
                        gather(1 - p, 0).start()
                rbuf = bufs[s % 2]

                @pl.loop(0, rows, step=SC_ROW_GROUP)
                def _(r0):
                    accs = [jnp.zeros((sl,), jnp.float32) for _ in range(SC_ROW_GROUP)]
                    for c, k in pieces:
                        h_hi, h_lo = h_v[c, pl.ds(k, sl)], h_v[c + half, pl.ds(k, sl)]
                        for j in range(SC_ROW_GROUP):
                            hi, lo = halves(rbuf[r0 + j, c, pl.ds(k, sl)])
                            accs[j] = accs[j] + hi * h_hi + lo * h_lo
                    coefs = []
                    for j in range(SC_ROW_GROUP):
                        a = jnp.broadcast_to(jnp.sum(accs[j]), (sl,))
                        act = 0.5 * a * (1.0 + _sc_tanh(0.7978845608028654 * (a + 0.044715 * (a * a * a))))
                        gate = plsc.load_gather(g_v, [jnp.broadcast_to(s * rows + r0 + j, (sl,))])
                        coefs.append(gate * act)
                    for b0 in range(0, len(pieces), SC_STORE_BATCH):
                        tots = []
                        for c, k in pieces[b0:b0 + SC_STORE_BATCH]:
                            t_hi = t_lo = None
                            for j in range(SC_ROW_GROUP):
                                hi, lo = halves(rbuf[r0 + j, c + half, pl.ds(k, sl)])
                                t_hi = coefs[j] * hi if t_hi is None else t_hi + coefs[j] * hi
                                t_lo = coefs[j] * lo if t_lo is None else t_lo + coefs[j] * lo
                            tots.append((t_hi, t_lo))
                        for (c, k), (t_hi, t_lo) in zip(pieces[b0:b0 + SC_STORE_BATCH], tots):
                            plsc.addupdate(out_v.at[c, pl.ds(k, sl)], t_hi)
                            plsc.addupdate(out_v.at[c + half, pl.ds(k, sl)], t_lo)

            pltpu.sync_copy(out_v, peer_hbm.at[t])

        @pl.loop(0, tok_per_worker, step=2)
        def _(ti):
            one_token(ti, 0)
            one_token(ti + 1, 1)

    return run(table3, ids, gates, h3)


def _finish_kernel(peer_ref, x_ref, g2_ref, fg_ref, prev_ref, out_ref):
    del prev_ref
    _finish_block(x_ref, g2_ref, fg_ref, peer_ref, out_ref)


def _finish_tokens(peer, x1, g2, fg, prev, seq_len):
    d = x1.shape[1]
    tm = FINISH_TOK_BLOCK
    return pl.pallas_call(
        _finish_kernel,
        grid=(peer.shape[0] // tm,),
        in_specs=[pl.BlockSpec((tm, d), lambda i: (i, 0)),
                  pl.BlockSpec((tm, d), lambda i: (i, 0)),
                  pl.BlockSpec((1, 1, d), lambda i: (i * tm // seq_len, 0, 0)),
                  pl.BlockSpec((1, d), lambda i: (0, 0)),
                  pl.BlockSpec(memory_space=pl.ANY)],
        out_specs=pl.BlockSpec((tm, d), lambda i: (i, 0)),
        out_shape=jax.ShapeDtypeStruct(prev.shape, prev.dtype),
        input_output_aliases={4: 0},
        compiler_params=pltpu.CompilerParams(dimension_semantics=("arbitrary",)),
        name="finish_tokens",
    )(peer, x1, g2, fg, prev)


def _pack_expert_table_sc(expert_u, expert_v):
    n_experts, d = expert_u.shape
    bits = lambda a: lax.bitcast_convert_type(a.astype(jnp.bfloat16), jnp.uint16).astype(jnp.uint32)
    pair = lambda b: (b[:, :d // 2] << 16) | b[:, d // 2:]
    words = jnp.concatenate([pair(bits(expert_u)), pair(bits(expert_v))], axis=1)
    return words.reshape(n_experts, d // LANES, LANES)


def _pack_expert_table(expert_u, expert_v):
    ub = lax.bitcast_convert_type(expert_u.astype(jnp.bfloat16), jnp.uint16).astype(jnp.uint32)
    vb = lax.bitcast_convert_type(expert_v.astype(jnp.bfloat16), jnp.uint16).astype(jnp.uint32)
    packed = (ub << 16) | vb
    n_experts, d = packed.shape
    return packed.reshape(n_experts * (d // LANES), LANES)


def _tc_peer_experts(idx, gates, h2, x1, g2, final_g, table, seq_len, first_tok):
    n, d = h2.shape
    m = n - first_tok
    n_chunks = d // LANES
    tb = EXPERT_TOK_BLOCK
    assert tb % EXPERT_SLOTS == 0 and seq_len % tb == 0 and first_tok % tb == 0
    first = first_tok // tb
    return pl.pallas_call(
        functools.partial(_expert_kernel, tok_block=tb, n_slots=EXPERT_SLOTS),
        grid=(m // tb,),
        in_specs=[
            pl.BlockSpec((tb, PEER_SLOTS), lambda i: (i + first, 0), memory_space=pltpu.SMEM),
            pl.BlockSpec((tb, PEER_SLOTS), lambda i: (i + first, 0)),
            pl.BlockSpec((tb, d), lambda i: (i + first, 0)),
            pl.BlockSpec((tb, d), lambda i: (i + first, 0)),
            pl.BlockSpec((1, 1, d), lambda i: ((i + first) * tb // seq_len, 0, 0)),
            pl.BlockSpec((1, d), lambda i: (0, 0)),
            pl.BlockSpec(memory_space=pl.ANY),
        ],
        out_specs=pl.BlockSpec((tb, d), lambda i: (i + first, 0)),
        out_shape=jax.ShapeDtypeStruct((n, d), jnp.float32),
        scratch_shapes=(
            [pltpu.VMEM((PEER_SLOTS * n_chunks, LANES), jnp.uint32) for _ in range(EXPERT_SLOTS)]
            + [pltpu.VMEM((tb, d), jnp.float32), pltpu.SemaphoreType.DMA((EXPERT_SLOTS,))]),
        compiler_params=pltpu.CompilerParams(dimension_semantics=("arbitrary",)),
        cost_estimate=_expert_cost(m, d),
        name="peer_experts",
    )(idx, gates, h2, x1, g2, final_g.reshape(1, d), table)


def _mix_and_route(x, mod, norm1_g, w_in, b_forget, ln_v_g, w_spatial, b_spatial, w_branch_a,
                   w_branch_b, w_out, norm2_g, w_query, sub_keys):
    Bp, S, D = x.shape
    m = Bp * S
    bf16 = jnp.bfloat16
    qkv, cum, sga, gb = _input_projection(x, mod, norm1_g, w_in, b_forget, ln_v_g,
                                          w_spatial, b_spatial, w_branch_b)
    heads = lambda a: a.reshape(Bp, S, ATT_HEADS, ATT_HEAD_DIM).transpose(0, 2, 1, 3)
    q = heads(qkv[:, 0:ATT_WIDTH])
    k = heads(qkv[:, ATT_WIDTH:2 * ATT_WIDTH])
    v = heads(qkv[:, 2 * ATT_WIDTH:])
    cum_h = cum[:, 0:ATT_HEADS].reshape(Bp, S, ATT_HEADS).transpose(0, 2, 1)
    y_a = _fox_attention(q, k, v, cum_h).transpose(0, 2, 1, 3).reshape(m, ATT_WIDTH)
    x1, h2, idx_t, gates_t = _post_attention(
        y_a, sga, gb, x.reshape(m, D), mod, norm2_g.reshape(1, D),
        w_branch_a.astype(bf16), w_out.astype(bf16), w_query.astype(bf16), sub_keys, S)
    return x1, h2, idx_t.T, gates_t.T


def kernel(x, c, w_mod, b_mod, norm1_g, w_in, b_forget, ln_v_g, w_spatial, b_spatial, w_branch_a, w_branch_b, w_out, norm2_g, w_query, sub_keys, expert_u, expert_v, final_g):
    B, S, D = x.shape
    n = B * S
    assert w_mod.shape[0] == 1, "the final RMSNorm is fused into the single layer's expert kernels"
    l = 0
    mod = _modulation(c, w_mod[l], b_mod[l]).reshape(B, 6, D)
    table = _pack_expert_table(expert_u[l], expert_v[l])
    n_chunks = D // LANES
    weights = (norm1_g[l], w_in[l], b_forget[l], ln_v_g[l], w_spatial[l], b_spatial[l],
               w_branch_a[l], w_branch_b[l], w_out[l], norm2_g[l], w_query[l], sub_keys[l])

    x1, h2, idx, gates = _mix_and_route(x, mod, *weights)
    g2 = mod[:, 5:6, :]

    n_sc = n * SC_SHARE_PERCENT // 100 // SC_SHARE_ALIGN * SC_SHARE_ALIGN
    out = _tc_peer_experts(idx, gates, h2, x1, g2, final_g, table, S, n_sc)
    if n_sc > 0:
        peer_sc = _sc_peer_experts(_pack_expert_table_sc(expert_u[l], expert_v[l]),
                                   idx[:n_sc].reshape(-1), gates[:n_sc].reshape(-1),
                                   h2[:n_sc].reshape(n_sc, n_chunks, LANES))
        out = _finish_tokens(peer_sc.reshape(n_sc, D), x1, g2, final_g.reshape(1, D), out, S)
    return out.reshape(B, S, D)
```

```python
import functools

import jax
import jax.numpy as jnp
from jax import lax
from jax.experimental import pallas as pl
from jax.experimental.pallas import tpu as pltpu
from jax.experimental.pallas import tpu_sc as plsc

D_MODEL = 1024
ATT_HEADS = 8
ATT_HEAD_DIM = 64
ATT_WIDTH = ATT_HEADS * ATT_HEAD_DIM
Q_BLOCK = 128
GM_GROUPS = 4
GM_GROUP_DIM = 128
GM_WIDTH = GM_GROUPS * GM_GROUP_DIM
GM_CHUNK = 128
PEER_HEADS = 8
PEER_KEY_DIM = 256
PEER_HALF = PEER_KEY_DIM // 2
N_KEYS = 128
PEER_TOPK = 16
PEER_TOK_BLOCK = 128
PEER_SLOTS = PEER_HEADS * PEER_TOPK
SPLIT_POINTS = (ATT_WIDTH, 2 * ATT_WIDTH, 3 * ATT_WIDTH, 3 * ATT_WIDTH + ATT_HEADS,
                3 * ATT_WIDTH + ATT_HEADS + 2 * GM_WIDTH,
                3 * ATT_WIDTH + ATT_HEADS + 2 * GM_WIDTH + D_MODEL)
EPS = 1e-6

LANES = 128
EXPERT_TOK_BLOCK = 128
EXPERT_SLOTS = 4
DMA_THREADS = 2
FINISH_TOK_BLOCK = 512
SC_GATHER_ROWS = 32
SC_ROW_GROUP = 4
SC_STORE_BATCH = 8
SC_SHARE_PERCENT = 58
SC_SHARE_ALIGN = 512


def _gelu(x):
    return 0.5 * x * (1.0 + jnp.tanh(0.7978845608028654 * (x + 0.044715 * (x * x * x))))


def _rmsnorm(x, g):
    y = x * lax.rsqrt(jnp.mean(x * x, axis=-1, keepdims=True) + EPS)
    return y * g


def _layernorm(x, g):
    mu = jnp.mean(x, axis=-1, keepdims=True)
    var = jnp.mean(jnp.square(x - mu), axis=-1, keepdims=True)
    return (x - mu) * lax.rsqrt(var + EPS) * g


def _modulate(h, shift, scale):
    return h * (1.0 + scale[:, None, :]) + shift[:, None, :]


def _fox(q, k, v, log_f):
    B, S, H, Dh = q.shape
    nb = S // Q_BLOCK
    cum = jnp.cumsum(log_f, axis=1)
    cum_k = jnp.transpose(cum, (0, 2, 1))
    k_pos = jnp.arange(S)
    q_blocks = q.reshape(B, nb, Q_BLOCK, H, Dh).transpose(1, 0, 2, 3, 4)
    cq_blocks = cum_k.reshape(B, H, nb, Q_BLOCK).transpose(2, 0, 1, 3)
    starts = jnp.arange(nb) * Q_BLOCK
    scale = ATT_HEAD_DIM ** -0.5

    def one_block(args):
        q_blk, cq_blk, start = args
        s = jnp.einsum('bqhd,bkhd->bhqk', q_blk, k) * scale
        s = s + cq_blk[..., :, None] - cum_k[..., None, :]
        q_pos = start + jnp.arange(Q_BLOCK)
        causal = q_pos[:, None] >= k_pos[None, :]
        s = jnp.where(causal[None, None], s, -jnp.inf)
        p = jax.nn.softmax(s, axis=-1)
        return jnp.einsum('bhqk,bkhd->bqhd', p, v)

    out = lax.map(one_block, (q_blocks, cq_blocks, starts))
    return out.transpose(1, 0, 2, 3, 4).reshape(B, S, H * Dh)


def _gmlp(z, ln_g, w_s, b_s):
    B, S, _ = z.shape
    u, v = jnp.split(z, 2, axis=-1)
    v = _layernorm(v, ln_g)
    nc = S // GM_CHUNK
    v = v.reshape(B, nc, GM_CHUNK, GM_GROUPS, GM_GROUP_DIM)
    tril = jnp.tril(jnp.ones((GM_CHUNK, GM_CHUNK), dtype=bool))
    w = jnp.where(tril[None], w_s, jnp.zeros_like(w_s))
    mixed = jnp.einsum('gts,bcsgd->bctgd', w, v) + b_s.T[None, None, :, :, None]
    return u * mixed.reshape(B, S, GM_WIDTH)


def _peer_route(hf, w_query, sub_keys):
    n = hf.shape[0]
    q = (hf @ w_query).reshape(n, PEER_HEADS, 2, PEER_HALF)
    scores = jnp.einsum('nhpd,hpkd->nhpk', q, sub_keys)
    s_top, i_top = lax.top_k(scores, PEER_TOPK)
    cand_s = (s_top[:, :, 0, :, None] + s_top[:, :, 1, None, :]).reshape(n, PEER_HEADS, PEER_TOPK * PEER_TOPK)
    cand_i = (i_top[:, :, 0, :, None] * N_KEYS + i_top[:, :, 1, None, :]).reshape(n, PEER_HEADS, PEER_TOPK * PEER_TOPK)
    best_s, best_pos = lax.top_k(cand_s, PEER_TOPK)
    idx = jnp.take_along_axis(cand_i, best_pos, axis=-1)
    gates = jax.nn.softmax(best_s, axis=-1)
    return idx.reshape(n, PEER_SLOTS), gates.reshape(n, PEER_SLOTS)


def _mod_kernel(c_ref, w_ref, b_ref, o_ref):
    c = c_ref[...]
    sc = c * jax.nn.sigmoid(c)
    o_ref[...] = jnp.dot(sc, w_ref[...], precision=lax.Precision.HIGHEST,
                         preferred_element_type=jnp.float32) + b_ref[...]


def _modulation(c, w_mod, b_mod):
    b, d = c.shape
    cols = w_mod.shape[1]
    return pl.pallas_call(
        _mod_kernel,
        grid=(cols // d,),
        in_specs=[pl.BlockSpec((b, d), lambda j: (0, 0)),
                  pl.BlockSpec((d, d), lambda j: (0, j)),
                  pl.BlockSpec((1, d), lambda j: (0, j))],
        out_specs=pl.BlockSpec((b, d), lambda j: (0, j)),
        out_shape=jax.ShapeDtypeStruct((b, cols), jnp.float32),
        name="modulation",
    )(c, w_mod, b_mod.reshape(1, cols))


INPROJ_TOK_BLOCK = 256


def _inproj_kernel(x_ref, mod_ref, n1g_ref, wqkv_ref, wf_ref, bf_ref, wz_ref, wg_ref, lng_ref,
                   wsp_ref, bsp_ref, wb_ref, qkv_ref, cum_ref, sga_ref, gb_ref, carry):
    f32, bf16 = jnp.float32, jnp.bfloat16
    tm, d = x_ref.shape[1], x_ref.shape[2]
    x = x_ref[0]
    sh1 = mod_ref[0, 0:1, :]
    sc1 = mod_ref[0, 1:2, :]
    h = x * lax.rsqrt(jnp.mean(x * x, axis=-1, keepdims=True) + EPS) * n1g_ref[...]
    hb = (h * (1.0 + sc1) + sh1).astype(bf16)

    qkv = jnp.dot(hb, wqkv_ref[...], preferred_element_type=f32)
    qkv_ref[:, 0:ATT_WIDTH] = (qkv[:, 0:ATT_WIDTH] * (ATT_HEAD_DIM ** -0.5)).astype(bf16)
    qkv_ref[:, ATT_WIDTH:] = qkv[:, ATT_WIDTH:].astype(bf16)

    f = jnp.dot(hb, wf_ref[...], preferred_element_type=f32) + bf_ref[...]
    logf = jnp.minimum(f, 0.0) - jnp.log1p(jnp.exp(-jnp.abs(f)))

    @pl.when(pl.program_id(1) == 0)
    def _():
        carry[...] = jnp.zeros_like(carry)

    tri = (lax.broadcasted_iota(jnp.int32, (tm, tm), 0)
           >= lax.broadcasted_iota(jnp.int32, (tm, tm), 1)).astype(f32)
    cum = jnp.dot(tri, logf, precision=lax.Precision.HIGHEST, preferred_element_type=f32) + carry[...]
    cum_ref[...] = cum
    carry[...] = cum[tm - 1:tm, :]

    gz = _gelu(jnp.dot(hb, wz_ref[...], preferred_element_type=f32))
    u = gz[:, 0:GM_WIDTH]
    v = gz[:, GM_WIDTH:]
    mu = jnp.mean(v, axis=-1, keepdims=True)
    var = jnp.mean(jnp.square(v - mu), axis=-1, keepdims=True)
    vn = ((v - mu) * lax.rsqrt(var + EPS) * lng_ref[...]).astype(bf16)
    tril = (lax.broadcasted_iota(jnp.int32, (GM_CHUNK, GM_CHUNK), 0)
            >= lax.broadcasted_iota(jnp.int32, (GM_CHUNK, GM_CHUNK), 1))
    w_sp = [jnp.where(tril, wsp_ref[g], 0.0).astype(bf16) for g in range(GM_GROUPS)]
    rows = []
    for ck in range(tm // GM_CHUNK):
        r0 = ck * GM_CHUNK
        cols = []
        for g in range(GM_GROUPS):
            c0 = g * GM_GROUP_DIM
            mixed = jnp.dot(w_sp[g], vn[r0:r0 + GM_CHUNK, c0:c0 + GM_GROUP_DIM],
                            preferred_element_type=f32) + bsp_ref[g]
            cols.append(u[r0:r0 + GM_CHUNK, c0:c0 + GM_GROUP_DIM] * mixed)
        rows.append(jnp.concatenate(cols, axis=1))
    yb = jnp.concatenate(rows, axis=0).astype(bf16)
    ybp = jnp.dot(yb, wb_ref[...], preferred_element_type=f32)

    sg = jax.nn.sigmoid(jnp.dot(hb, wg_ref[...], preferred_element_type=f32))
    sga_ref[...] = sg[:, 0:d].astype(bf16)
    gb_ref[...] = (sg[:, d:] * ybp).astype(bf16)


def _input_projection(x, mod, norm1_g, w_in, b_forget, ln_v_g, w_spatial, b_spatial, w_branch_b):
    B, S, d = x.shape
    n = B * S
    tm = INPROJ_TOK_BLOCK
    bf16 = jnp.bfloat16
    p0, p1, p2, p3, p4, p5 = SPLIT_POINTS
    w_qkv = w_in[:, 0:p2].astype(bf16)
    w_f = jnp.pad(w_in[:, p2:p3], ((0, 0), (0, LANES - ATT_HEADS))).astype(bf16)
    b_f = jnp.pad(b_forget, (0, LANES - ATT_HEADS)).reshape(1, LANES)
    w_z = w_in[:, p3:p4].astype(bf16)
    w_g = w_in[:, p4:].astype(bf16)
    nt = S // tm
    tok = lambda w: pl.BlockSpec((tm, w), lambda b, i: (b * nt + i, 0))
    full = lambda a: pl.BlockSpec(a.shape, lambda b, i: (0,) * a.ndim)
    args = (x, mod, norm1_g.reshape(1, d), w_qkv, w_f, b_f, w_z, w_g, ln_v_g.reshape(1, GM_WIDTH),
            w_spatial, b_spatial.reshape(GM_GROUPS, GM_CHUNK, 1), w_branch_b.astype(bf16))
    return pl.pallas_call(
        _inproj_kernel,
        grid=(B, nt),
        in_specs=[pl.BlockSpec((1, tm, d), lambda b, i: (b, i, 0)),
                  pl.BlockSpec((1, 6, d), lambda b, i: (b, 0, 0))] + [full(a) for a in args[2:]],
        out_specs=[tok(3 * ATT_WIDTH), tok(LANES), tok(d), tok(d)],
        out_shape=[jax.ShapeDtypeStruct((n, 3 * ATT_WIDTH), bf16),
                   jax.ShapeDtypeStruct((n, LANES), jnp.float32),
                   jax.ShapeDtypeStruct((n, d), bf16),
                   jax.ShapeDtypeStruct((n, d), bf16)],
        scratch_shapes=[pltpu.VMEM((1, LANES), jnp.float32)],
        compiler_params=pltpu.CompilerParams(dimension_semantics=("arbitrary", "arbitrary"),
                                             vmem_limit_bytes=56 * 1024 * 1024),
        name="input_projection",
    )(*args)


ATT_BLOCK = 512


def _fox_kernel(q_ref, k_ref, v_ref, cq_ref, ck_ref, o_ref, *, blk):
    f32 = jnp.float32
    i = pl.program_id(2)
    q = q_ref[0, 0]
    cq = cq_ref[0, 0]

    def block(j, carry, masked):
        m, l, acc = carry
        off = pl.multiple_of(j * blk, blk)
        k = k_ref[0, 0, pl.ds(off, blk), :]
        v = v_ref[0, 0, pl.ds(off, blk), :]
        s = lax.dot_general(q, k, (((1,), (1,)), ((), ())), preferred_element_type=f32)
        s = s + (cq - ck_ref[0, 0, j])
        if masked:
            causal = (lax.broadcasted_iota(jnp.int32, (blk, blk), 0)
                      >= lax.broadcasted_iota(jnp.int32, (blk, blk), 1))
            s = jnp.where(causal, s, -jnp.inf)
        m_new = jnp.maximum(m, jnp.max(s, axis=1, keepdims=True))
        alpha = jnp.exp(m - m_new)
        p = jnp.exp(s - m_new)
        l = alpha * l + jnp.sum(p, axis=1, keepdims=True)
        acc = alpha * acc + jnp.dot(p.astype(v.dtype), v, preferred_element_type=f32)
        return m_new, l, acc

    init = (jnp.full((blk, 1), -1e30, f32), jnp.zeros((blk, 1), f32),
            jnp.zeros((blk, q.shape[1]), f32))
    carry = lax.fori_loop(0, i, lambda j, c: block(j, c, False), init)
    m, l, acc = block(i, carry, True)
    o_ref[0, 0] = (acc / l).astype(o_ref.dtype)


def _fox_attention(q, k, v, cum):
    B, H, S, dh = q.shape
    blk = min(ATT_BLOCK, S)
    nb = S // blk
    cq = cum.reshape(B, H, S, 1)
    ck = cum.reshape(B, H, nb, 1, blk)
    return pl.pallas_call(
        functools.partial(_fox_kernel, blk=blk),
        grid=(B, H, nb),
        in_specs=[pl.BlockSpec((1, 1, blk, dh), lambda b, h, i: (b, h, i, 0)),
                  pl.BlockSpec((1, 1, S, dh), lambda b, h, i: (b, h, 0, 0)),
                  pl.BlockSpec((1, 1, S, dh), lambda b, h, i: (b, h, 0, 0)),
                  pl.BlockSpec((1, 1, blk, 1), lambda b, h, i: (b, h, i, 0)),
                  pl.BlockSpec((1, 1, nb, 1, blk), lambda b, h, i: (b, h, 0, 0, 0))],
        out_specs=pl.BlockSpec((1, 1, blk, dh), lambda b, h, i: (b, h, i, 0)),
        out_shape=jax.ShapeDtypeStruct((B, H, S, dh), jnp.bfloat16),
        compiler_params=pltpu.CompilerParams(
            dimension_semantics=("arbitrary", "arbitrary", "arbitrary")),
        name="fox_attention",
    )(q, k, v, cq, ck)


def _topk_rows(s, k):
    rows, t = s.shape
    iota = lax.broadcasted_iota(jnp.int32, (rows, t), 0)
    slot = lax.broadcasted_iota(jnp.int32, (k, t), 0)
    vals = jnp.zeros((k, t), jnp.float32)
    ids = jnp.zeros((k, t), jnp.int32)
    for j in range(k):
        m = jnp.max(s, axis=0, keepdims=True)
        am = jnp.min(jnp.where(s == m, iota, rows), axis=0, keepdims=True)
        vals = jnp.where(slot == j, m, vals)
        ids = jnp.where(slot == j, am, ids)
        s = jnp.where(iota == am, -jnp.inf, s)
    return vals, ids


def _select_rows(table, pos):
    out = jnp.zeros(pos.shape, table.dtype)
    for r in range(table.shape[0]):
        out = jnp.where(pos == r, table[r:r + 1, :], out)
    return out


def _post_kernel(ya_ref, sga_ref, gb_ref, x_ref, mod_ref, n2g_ref, wa_ref, wo_ref, wq_ref, keys_ref,
                 x1_ref, h2_ref, idx_ref, gates_ref, q_scr):
    f32 = jnp.float32
    a = jnp.dot(ya_ref[...], wa_ref[...], preferred_element_type=f32)
    merged = sga_ref[...].astype(f32) * a + gb_ref[...].astype(f32)
    o = jnp.dot(merged.astype(jnp.bfloat16), wo_ref[...], preferred_element_type=f32)
    g1 = mod_ref[0, 2:3, :]
    sh2 = mod_ref[0, 3:4, :]
    sc2 = mod_ref[0, 4:5, :]
    x1 = x_ref[...] + g1 * o
    x1_ref[...] = x1
    h2 = x1 * lax.rsqrt(jnp.mean(x1 * x1, axis=-1, keepdims=True) + EPS) * n2g_ref[...]
    h2 = h2 * (1.0 + sc2) + sh2
    h2_ref[...] = h2
    qp = jnp.dot(h2.astype(jnp.bfloat16), wq_ref[...], preferred_element_type=f32)
    for j in range(2 * PEER_HEADS):
        q_scr[j] = qp[:, j * PEER_HALF:(j + 1) * PEER_HALF]

    def head(h, carry):
        tops = []
        for p in range(2):
            q = q_scr[2 * h + p]
            keys = keys_ref[2 * h + p]
            sc = lax.dot_general(keys, q, (((1,), (1,)), ((), ())),
                                 precision=lax.Precision.HIGHEST,
                                 preferred_element_type=f32)
            tops.append(_topk_rows(sc, PEER_TOPK))
        (s1, i1), (s2, i2) = tops
        half = PEER_TOPK // 2
        blocks = [s1[0:1, :] + s2]
        blocks += [s1[a:a + 1, :] + s2[0:half, :] for a in range(1, half)]
        blocks += [s1[half:, :] + s2[0:1, :]]
        vals, pos = _topk_rows(jnp.concatenate(blocks, axis=0), PEER_TOPK)
        mid = pos - PEER_TOPK
        tail0 = PEER_TOPK + (half - 1) * half
        ra = jnp.where(pos < PEER_TOPK, 0,
                       jnp.where(pos < tail0, 1 + (mid >> (half.bit_length() - 1)), pos - tail0 + half))
        rb = jnp.where(pos < PEER_TOPK, pos, jnp.where(pos < tail0, mid & (half - 1), 0))
        eid = _select_rows(i1, ra) * N_KEYS + _select_rows(i2, rb)
        e = jnp.exp(vals - vals[0:1, :])
        g = e / jnp.sum(e, axis=0, keepdims=True)
        row = pl.multiple_of(h * PEER_TOPK, PEER_TOPK)
        idx_ref[pl.ds(row, PEER_TOPK), :] = eid
        gates_ref[pl.ds(row, PEER_TOPK), :] = g
        return carry

    lax.fori_loop(0, PEER_HEADS, head, 0)


POST_TOK_BLOCK = 256


def _post_attention(ya, sga, gb, x, mod, norm2_g, w_a, w_out, w_query, sub_keys, seq_len):
    n, d = x.shape
    tm = POST_TOK_BLOCK
    blocks_per_seq = seq_len // tm
    aw = ya.shape[1]
    qw = w_query.shape[1]
    keys = sub_keys.reshape(2 * PEER_HEADS, N_KEYS, PEER_HALF)
    tok = lambda w: pl.BlockSpec((tm, w), lambda i: (i, 0))
    full = lambda a: pl.BlockSpec(a.shape, lambda i: (0,) * a.ndim)
    return pl.pallas_call(
        _post_kernel,
        grid=(n // tm,),
        in_specs=[tok(aw), tok(d), tok(d), tok(d),
                  pl.BlockSpec((1, 6, d), lambda i: (i // blocks_per_seq, 0, 0)),
                  full(norm2_g), full(w_a), full(w_out), full(w_query), full(keys)],
        out_specs=[tok(d), tok(d),
                   pl.BlockSpec((PEER_SLOTS, tm), lambda i: (0, i)),
                   pl.BlockSpec((PEER_SLOTS, tm), lambda i: (0, i))],
        out_shape=[jax.ShapeDtypeStruct((n, d), jnp.float32),
                   jax.ShapeDtypeStruct((n, d), jnp.float32),
                   jax.ShapeDtypeStruct((PEER_SLOTS, n), jnp.int32),
                   jax.ShapeDtypeStruct((PEER_SLOTS, n), jnp.float32)],
        scratch_shapes=[pltpu.VMEM((2 * PEER_HEADS, tm, PEER_HALF), jnp.float32)],
        compiler_params=pltpu.CompilerParams(dimension_semantics=("arbitrary",),
                                             vmem_limit_bytes=48 * 1024 * 1024),
        name="post_attention",
    )(ya, sga, gb, x, mod, norm2_g, w_a, w_out, w_query, keys)


def _rowsum_bcast(p, ones_bf16):
    hi = p.astype(jnp.bfloat16)
    lo = (p - hi.astype(jnp.float32)).astype(jnp.bfloat16)
    return (jnp.dot(hi, ones_bf16, preferred_element_type=jnp.float32)
            + jnp.dot(lo, ones_bf16, preferred_element_type=jnp.float32))


def _word_halves(w):
    return (lax.bitcast_convert_type(w & jnp.uint32(0xFFFF0000), jnp.float32),
            lax.bitcast_convert_type(w << 16, jnp.float32))


def _eval_experts(chunk, hrow, grow):
    half = hrow.shape[1] // LANES // 2
    ones_bf16 = jnp.ones((LANES, LANES), jnp.bfloat16)
    eye = (lax.broadcasted_iota(jnp.int32, (PEER_SLOTS, LANES), 0)
           == lax.broadcasted_iota(jnp.int32, (PEER_SLOTS, LANES), 1))
    hpart = lambda c: hrow[:, c * LANES:(c + 1) * LANES]
    psum = None
    for c in range(half):
        hi, lo = _word_halves(chunk(c))
        p = hi * hpart(c) + lo * hpart(c + half)
        psum = p if psum is None else psum + p
    act = _gelu(_rowsum_bcast(psum, ones_bf16))
    gcol = _rowsum_bcast(jnp.where(eye, grow, 0.0), ones_bf16)
    coef = gcol * act
    outs_hi, outs_lo = [], []
    for c in range(half):
        hi, lo = _word_halves(chunk(c + half))
        outs_hi.append(jnp.sum(coef * hi, axis=0, keepdims=True))
        outs_lo.append(jnp.sum(coef * lo, axis=0, keepdims=True))
    return jnp.concatenate(outs_hi + outs_lo, axis=-1)


def _finish_block(x_ref, g2_ref, fg_ref, peer, out_ref):
    y = x_ref[...] + g2_ref[0] * peer[...]
    out_ref[...] = y * lax.rsqrt(jnp.mean(y * y, axis=-1, keepdims=True) + EPS) * fg_ref[...]


def _expert_kernel(idx_ref, gates_ref, h_ref, x_ref, g2_ref, fg_ref, tab_ref, out_ref,
                   *scratch, tok_block, n_slots):
    bufs = scratch[:n_slots]
    peer, sem = scratch[n_slots], scratch[n_slots + 1]
    d_model = h_ref.shape[-1]
    n_chunks = d_model // LANES
    rows_per_tok = PEER_SLOTS * n_chunks

    def issue(t, s):
        for r in range(PEER_SLOTS):
            row = pl.multiple_of(idx_ref[t, r] * n_chunks, n_chunks)
            pltpu.make_async_copy(tab_ref.at[pl.ds(row, n_chunks), :],
                                  bufs[s].at[pl.ds(r * n_chunks, n_chunks), :],
                                  sem.at[s]).start(priority=r % DMA_THREADS)

    def wait(s):
        pltpu.make_async_copy(tab_ref.at[pl.ds(0, rows_per_tok), :], bufs[s], sem.at[s]).wait()

    def compute(t, s):
        chunk = lambda c: bufs[s][pl.ds(c, PEER_SLOTS, stride=n_chunks), :]
        peer[pl.ds(t, 1), :] = _eval_experts(chunk, h_ref[pl.ds(t, 1), :], gates_ref[pl.ds(t, 1), :])

    def step(t, s, prefetch):
        wait(s)
        if prefetch:
            issue(t + n_slots - 1, (s - 1) % n_slots)
        compute(t, s)

    for t in range(n_slots - 1):
        issue(t, t)

    n_groups = tok_block // n_slots

    def group(g, carry):
        for s in range(n_slots):
            step(g * n_slots + s, s, True)
        return carry

    lax.fori_loop(0, n_groups - 1, group, 0)
    for s in range(n_slots):
        t = (n_groups - 1) * n_slots + s
        step(t, s, t + n_slots - 1 < tok_block)

    _finish_block(x_ref, g2_ref, fg_ref, peer, out_ref)


def _expert_cost(n_tok, d):
    pairs = n_tok * PEER_SLOTS
    return pl.CostEstimate(flops=4 * pairs * d, transcendentals=pairs,
                           bytes_accessed=4 * pairs * d + 12 * n_tok * d + 8 * pairs)


def _sc_tanh(y):
    return 1.0 - 2.0 / (jnp.exp(2.0 * y) + 1.0)


def _sc_peer_experts(table3, ids, gates, h3):
    n_tok, n_chunks, lanes = h3.shape
    info = plsc.get_sparse_core_info()
    sl = info.num_lanes
    n_workers = info.num_cores * info.num_subcores
    tok_per_worker = n_tok // n_workers
    assert tok_per_worker * n_workers == n_tok and tok_per_worker % 2 == 0
    rows = SC_GATHER_ROWS
    n_sub = PEER_SLOTS // rows
    assert n_sub % 2 == 0
    half = n_chunks // 2
    pieces = [(c, k * sl) for c in range(half) for k in range(lanes // sl)]
    mesh = plsc.VectorSubcoreMesh(core_axis_name="c", subcore_axis_name="s")

    buf = lambda dt: pltpu.VMEM((rows, n_chunks, lanes), dt)

    @functools.partial(
        pl.kernel, mesh=mesh,
        out_type=jax.ShapeDtypeStruct((n_tok, n_chunks, lanes), jnp.float32),
        scratch_types=[pltpu.VMEM((PEER_SLOTS,), jnp.int32), pltpu.VMEM((PEER_SLOTS,), jnp.int32),
                       pltpu.VMEM((PEER_SLOTS,), jnp.float32), pltpu.VMEM((PEER_SLOTS,), jnp.float32),
                       pltpu.VMEM((n_chunks, lanes), jnp.float32),
                       pltpu.VMEM((n_chunks, lanes), jnp.float32),
                       buf(jnp.uint32), buf(jnp.uint32), pltpu.VMEM((n_chunks, lanes), jnp.float32),
                       pltpu.SemaphoreType.DMA((2,)), pltpu.SemaphoreType.DMA((2,))],
        compiler_params=pltpu.CompilerParams(needs_layout_passes=False),
        cost_estimate=_expert_cost(n_tok, n_chunks * lanes),
        name="sc_peer_experts")
    def run(table_hbm, ids_hbm, gates_hbm, h_hbm, peer_hbm, ids0, ids1, g0, g1, h0, h1, rows0, rows1,
            out_v, gsem, isem):
        bufs = (rows0, rows1)
        ids_p, g_p, h_p = (ids0, ids1), (g0, g1), (h0, h1)
        t_first = (lax.axis_index("s") * info.num_cores + lax.axis_index("c")) * tok_per_worker

        def token_inputs(t, p):
            off = pl.multiple_of(t * PEER_SLOTS, PEER_SLOTS)
            return (pltpu.make_async_copy(ids_hbm.at[pl.ds(off, PEER_SLOTS)], ids_p[p], isem.at[p]),
                    pltpu.make_async_copy(gates_hbm.at[pl.ds(off, PEER_SLOTS)], g_p[p], isem.at[p]),
                    pltpu.make_async_copy(h_hbm.at[t], h_p[p], isem.at[p]))

        def gather(p, s):
            return pltpu.make_async_copy(table_hbm.at[ids_p[p].at[pl.ds(s * rows, rows)]],
                                         bufs[s % 2], gsem.at[s % 2])

        for cp in token_inputs(t_first, 0):
            cp.start()
        for cp in token_inputs(t_first, 0):
            cp.wait()
        gather(0, 0).start()

        def one_token(ti, p):
            t = t_first + ti
            g_v, h_v = g_p[p], h_p[p]
            has_next = ti + 1 < tok_per_worker

            @pl.when(has_next)
            def _():
                for cp in token_inputs(t + 1, 1 - p):
                    cp.start()

            for c, k in pieces:
                out_v[c, pl.ds(k, sl)] = jnp.zeros((sl,), jnp.float32)
                out_v[c + half, pl.ds(k, sl)] = jnp.zeros((sl,), jnp.float32)
            for s in range(n_sub):
                gather(p, s).wait()
                if s + 1 < n_sub:
                    gather(p, s + 1).start()
                else:
                    @pl.when(has_next)
                    def _():
                        for cp in token_inputs(t + 1, 1 - p):
                            cp.wait()
                        gather(1 - p, 0).start()
                rbuf = bufs[s % 2]

                @pl.loop(0, rows, step=SC_ROW_GROUP)
                def _(r0):
                    accs = [jnp.zeros((sl,), jnp.float32) for _ in range(SC_ROW_GROUP)]
                    for c, k in pieces:
                        h_hi, h_lo = h_v[c, pl.ds(k, sl)], h_v[c + half, pl.ds(k, sl)]
                        for j in range(SC_ROW_GROUP):
                            hi, lo = _word_halves(rbuf[r0 + j, c, pl.ds(k, sl)])
                            accs[j] = accs[j] + hi * h_hi + lo * h_lo
                    coefs = []
                    for j in range(SC_ROW_GROUP):
                        a = jnp.broadcast_to(jnp.sum(accs[j]), (sl,))
                        act = 0.5 * a * (1.0 + _sc_tanh(0.7978845608028654 * (a + 0.044715 * (a * a * a))))
                        gate = plsc.load_gather(g_v, [jnp.broadcast_to(s * rows + r0 + j, (sl,))])
                        coefs.append(gate * act)
                    for b0 in range(0, len(pieces), SC_STORE_BATCH):
                        tots = []
                        for c, k in pieces[b0:b0 + SC_STORE_BATCH]:
                            t_hi = t_lo = None
                            for j in range(SC_ROW_GROUP):
                                hi, lo = _word_halves(rbuf[r0 + j, c + half, pl.ds(k, sl)])
                                t_hi = coefs[j] * hi if t_hi is None else t_hi + coefs[j] * hi
                                t_lo = coefs[j] * lo if t_lo is None else t_lo + coefs[j] * lo
                            tots.append((t_hi, t_lo))
                        for (c, k), (t_hi, t_lo) in zip(pieces[b0:b0 + SC_STORE_BATCH], tots):
                            plsc.addupdate(out_v.at[c, pl.ds(k, sl)], t_hi)
                            plsc.addupdate(out_v.at[c + half, pl.ds(k, sl)], t_lo)

            pltpu.sync_copy(out_v, peer_hbm.at[t])

        @pl.loop(0, tok_per_worker, step=2)
        def _(ti):
            one_token(ti, 0)
            one_token(ti + 1, 1)

    return run(table3, ids, gates, h3)


def _finish_kernel(peer_ref, x_ref, g2_ref, fg_ref, prev_ref, out_ref):
    del prev_ref
    _finish_block(x_ref, g2_ref, fg_ref, peer_ref, out_ref)


def _finish_tokens(peer, x1, g2, fg, prev, seq_len):
    d = x1.shape[1]
    tm = FINISH_TOK_BLOCK
    return pl.pallas_call(
        _finish_kernel,
        grid=(peer.shape[0] // tm,),
        in_specs=[pl.BlockSpec((tm, d), lambda i: (i, 0)),
                  pl.BlockSpec((tm, d), lambda i: (i, 0)),
                  pl.BlockSpec((1, 1, d), lambda i: (i * tm // seq_len, 0, 0)),
                  pl.BlockSpec((1, d), lambda i: (0, 0)),
                  pl.BlockSpec(memory_space=pl.ANY)],
        out_specs=pl.BlockSpec((tm, d), lambda i: (i, 0)),
        out_shape=jax.ShapeDtypeStruct(prev.shape, prev.dtype),
        input_output_aliases={4: 0},
        compiler_params=pltpu.CompilerParams(dimension_semantics=("arbitrary",)),
        name="finish_tokens",
    )(peer, x1, g2, fg, prev)


def _pack_expert_table(expert_u, expert_v):
    n_experts, d = expert_u.shape
    bits = lambda a: lax.bitcast_convert_type(a.astype(jnp.bfloat16), jnp.uint16).astype(jnp.uint32)
    pair = lambda b: (b[:, :d // 2] << 16) | b[:, d // 2:]
    words = jnp.concatenate([pair(bits(expert_u)), pair(bits(expert_v))], axis=1)
    return words.reshape(n_experts, d // LANES, LANES)


def _tc_peer_experts(idx, gates, h2, x1, g2, final_g, table, seq_len, first_tok):
    n, d = h2.shape
    m = n - first_tok
    n_chunks = d // LANES
    tb = EXPERT_TOK_BLOCK
    assert tb % EXPERT_SLOTS == 0 and seq_len % tb == 0 and first_tok % tb == 0
    first = first_tok // tb
    return pl.pallas_call(
        functools.partial(_expert_kernel, tok_block=tb, n_slots=EXPERT_SLOTS),
        grid=(m // tb,),
        in_specs=[
            pl.BlockSpec((tb, PEER_SLOTS), lambda i: (i + first, 0), memory_space=pltpu.SMEM),
            pl.BlockSpec((tb, PEER_SLOTS), lambda i: (i + first, 0)),
            pl.BlockSpec((tb, d), lambda i: (i + first, 0)),
            pl.BlockSpec((tb, d), lambda i: (i + first, 0)),
            pl.BlockSpec((1, 1, d), lambda i: ((i + first) * tb // seq_len, 0, 0)),
            pl.BlockSpec((1, d), lambda i: (0, 0)),
            pl.BlockSpec(memory_space=pl.ANY),
        ],
        out_specs=pl.BlockSpec((tb, d), lambda i: (i + first, 0)),
        out_shape=jax.ShapeDtypeStruct((n, d), jnp.float32),
        scratch_shapes=(
            [pltpu.VMEM((PEER_SLOTS * n_chunks, LANES), jnp.uint32) for _ in range(EXPERT_SLOTS)]
            + [pltpu.VMEM((tb, d), jnp.float32), pltpu.SemaphoreType.DMA((EXPERT_SLOTS,))]),
        compiler_params=pltpu.CompilerParams(dimension_semantics=("arbitrary",)),
        cost_estimate=_expert_cost(m, d),
        name="peer_experts",
    )(idx, gates, h2, x1, g2, final_g.reshape(1, d), table.reshape(-1, LANES))


def _mix_and_route(x, mod, norm1_g, w_in, b_forget, ln_v_g, w_spatial, b_spatial, w_branch_a,
                   w_branch_b, w_out, norm2_g, w_query, sub_keys):
    Bp, S, D = x.shape
    m = Bp * S
    bf16 = jnp.bfloat16
    qkv, cum, sga, gb = _input_projection(x, mod, norm1_g, w_in, b_forget, ln_v_g,
                                          w_spatial, b_spatial, w_branch_b)
    heads = lambda a: a.reshape(Bp, S, ATT_HEADS, ATT_HEAD_DIM).transpose(0, 2, 1, 3)
    q = heads(qkv[:, 0:ATT_WIDTH])
    k = heads(qkv[:, ATT_WIDTH:2 * ATT_WIDTH])
    v = heads(qkv[:, 2 * ATT_WIDTH:])
    cum_h = cum[:, 0:ATT_HEADS].reshape(Bp, S, ATT_HEADS).transpose(0, 2, 1)
    y_a = _fox_attention(q, k, v, cum_h).transpose(0, 2, 1, 3).reshape(m, ATT_WIDTH)
    x1, h2, idx_t, gates_t = _post_attention(
        y_a, sga, gb, x.reshape(m, D), mod, norm2_g.reshape(1, D),
        w_branch_a.astype(bf16), w_out.astype(bf16), w_query.astype(bf16), sub_keys, S)
    return x1, h2, idx_t.T, gates_t.T


def kernel(x, c, w_mod, b_mod, norm1_g, w_in, b_forget, ln_v_g, w_spatial, b_spatial, w_branch_a, w_branch_b, w_out, norm2_g, w_query, sub_keys, expert_u, expert_v, final_g):
    B, S, D = x.shape
    n = B * S
    assert w_mod.shape[0] == 1, "the final RMSNorm is fused into the single layer's expert kernels"
    l = 0
    mod = _modulation(c, w_mod[l], b_mod[l]).reshape(B, 6, D)
    table = _pack_expert_table(expert_u[l], expert_v[l])
    n_chunks = D // LANES
    weights = (norm1_g[l], w_in[l], b_forget[l], ln_v_g[l], w_spatial[l], b_spatial[l],
               w_branch_a[l], w_branch_b[l], w_out[l], norm2_g[l], w_query[l], sub_keys[l])

    x1, h2, idx, gates = _mix_and_route(x, mod, *weights)
    g2 = mod[:, 5:6, :]

    n_sc = n * SC_SHARE_PERCENT // 100 // SC_SHARE_ALIGN * SC_SHARE_ALIGN
    out = _tc_peer_experts(idx, gates, h2, x1, g2, final_g, table, S, n_sc)
    if n_sc > 0:
        peer_sc = _sc_peer_experts(table, idx[:n_sc].reshape(-1), gates[:n_sc].reshape(-1),
                                   h2[:n_sc].reshape(n_sc, n_chunks, LANES))
        out = _finish_tokens(peer_sc.reshape(n_sc, D), x1, g2, final_g.reshape(1, D), out, S)
    return out.reshape(B, S, D)
```

```python
import functools

import jax
import jax.numpy as jnp
from jax import lax
from jax.experimental import pallas as pl
from jax.experimental.pallas import tpu as pltpu
from jax.experimental.pallas import tpu_sc as plsc

D_MODEL = 1024
ATT_HEADS = 8
ATT_HEAD_DIM = 64
ATT_WIDTH = ATT_HEADS * ATT_HEAD_DIM
Q_BLOCK = 128
GM_GROUPS = 4
GM_GROUP_DIM = 128
GM_WIDTH = GM_GROUPS * GM_GROUP_DIM
GM_CHUNK = 128
PEER_HEADS = 8
PEER_KEY_DIM = 256
PEER_HALF = PEER_KEY_DIM // 2
N_KEYS = 128
PEER_TOPK = 16
PEER_TOK_BLOCK = 128
PEER_SLOTS = PEER_HEADS * PEER_TOPK
SPLIT_POINTS = (ATT_WIDTH, 2 * ATT_WIDTH, 3 * ATT_WIDTH, 3 * ATT_WIDTH + ATT_HEADS,
                3 * ATT_WIDTH + ATT_HEADS + 2 * GM_WIDTH,
                3 * ATT_WIDTH + ATT_HEADS + 2 * GM_WIDTH + D_MODEL)
EPS = 1e-6

LANES = 128
EXPERT_TOK_BLOCK = 128
EXPERT_SLOTS = 4
DMA_THREADS = 2
FINISH_TOK_BLOCK = 512
SC_GATHER_ROWS = 32
SC_ROW_GROUP = 4
SC_STORE_BATCH = 8
SC_SHARE_PERCENT = 62


def _gelu(x):
    return 0.5 * x * (1.0 + jnp.tanh(0.7978845608028654 * (x + 0.044715 * (x * x * x))))


def _rmsnorm(x, g):
    y = x * lax.rsqrt(jnp.mean(x * x, axis=-1, keepdims=True) + EPS)
    return y * g


def _layernorm(x, g):
    mu = jnp.mean(x, axis=-1, keepdims=True)
    var = jnp.mean(jnp.square(x - mu), axis=-1, keepdims=True)
    return (x - mu) * lax.rsqrt(var + EPS) * g


def _modulate(h, shift, scale):
    return h * (1.0 + scale[:, None, :]) + shift[:, None, :]


def _fox(q, k, v, log_f):
    B, S, H, Dh = q.shape
    nb = S // Q_BLOCK
    cum = jnp.cumsum(log_f, axis=1)
    cum_k = jnp.transpose(cum, (0, 2, 1))
    k_pos = jnp.arange(S)
    q_blocks = q.reshape(B, nb, Q_BLOCK, H, Dh).transpose(1, 0, 2, 3, 4)
    cq_blocks = cum_k.reshape(B, H, nb, Q_BLOCK).transpose(2, 0, 1, 3)
    starts = jnp.arange(nb) * Q_BLOCK
    scale = ATT_HEAD_DIM ** -0.5

    def one_block(args):
        q_blk, cq_blk, start = args
        s = jnp.einsum('bqhd,bkhd->bhqk', q_blk, k) * scale
        s = s + cq_blk[..., :, None] - cum_k[..., None, :]
        q_pos = start + jnp.arange(Q_BLOCK)
        causal = q_pos[:, None] >= k_pos[None, :]
        s = jnp.where(causal[None, None], s, -jnp.inf)
        p = jax.nn.softmax(s, axis=-1)
        return jnp.einsum('bhqk,bkhd->bqhd', p, v)

    out = lax.map(one_block, (q_blocks, cq_blocks, starts))
    return out.transpose(1, 0, 2, 3, 4).reshape(B, S, H * Dh)


def _gmlp(z, ln_g, w_s, b_s):
    B, S, _ = z.shape
    u, v = jnp.split(z, 2, axis=-1)
    v = _layernorm(v, ln_g)
    nc = S // GM_CHUNK
    v = v.reshape(B, nc, GM_CHUNK, GM_GROUPS, GM_GROUP_DIM)
    tril = jnp.tril(jnp.ones((GM_CHUNK, GM_CHUNK), dtype=bool))
    w = jnp.where(tril[None], w_s, jnp.zeros_like(w_s))
    mixed = jnp.einsum('gts,bcsgd->bctgd', w, v) + b_s.T[None, None, :, :, None]
    return u * mixed.reshape(B, S, GM_WIDTH)


def _peer_route(hf, w_query, sub_keys):
    n = hf.shape[0]
    q = (hf @ w_query).reshape(n, PEER_HEADS, 2, PEER_HALF)
    scores = jnp.einsum('nhpd,hpkd->nhpk', q, sub_keys)
    s_top, i_top = lax.top_k(scores, PEER_TOPK)
    cand_s = (s_top[:, :, 0, :, None] + s_top[:, :, 1, None, :]).reshape(n, PEER_HEADS, PEER_TOPK * PEER_TOPK)
    cand_i = (i_top[:, :, 0, :, None] * N_KEYS + i_top[:, :, 1, None, :]).reshape(n, PEER_HEADS, PEER_TOPK * PEER_TOPK)
    best_s, best_pos = lax.top_k(cand_s, PEER_TOPK)
    idx = jnp.take_along_axis(cand_i, best_pos, axis=-1)
    gates = jax.nn.softmax(best_s, axis=-1)
    return idx.reshape(n, PEER_SLOTS), gates.reshape(n, PEER_SLOTS)


def _mod_kernel(c_ref, w_ref, b_ref, o_ref):
    c = c_ref[...]
    sc = c * jax.nn.sigmoid(c)
    o_ref[...] = jnp.dot(sc, w_ref[...], precision=lax.Precision.HIGHEST,
                         preferred_element_type=jnp.float32) + b_ref[...]


def _modulation(c, w_mod, b_mod):
    b, d = c.shape
    cols = w_mod.shape[1]
    return pl.pallas_call(
        _mod_kernel,
        grid=(cols // d,),
        in_specs=[pl.BlockSpec((b, d), lambda j: (0, 0)),
                  pl.BlockSpec((d, d), lambda j: (0, j)),
                  pl.BlockSpec((1, d), lambda j: (0, j))],
        out_specs=pl.BlockSpec((b, d), lambda j: (0, j)),
        out_shape=jax.ShapeDtypeStruct((b, cols), jnp.float32),
        name="modulation",
    )(c, w_mod, b_mod.reshape(1, cols))


INPROJ_TOK_BLOCK = 256


def _inproj_kernel(x_ref, mod_ref, n1g_ref, wqkv_ref, wf_ref, bf_ref, wz_ref, wg_ref, lng_ref,
                   wsp_ref, bsp_ref, wb_ref, qkv_ref, cum_ref, sga_ref, gb_ref, carry):
    f32, bf16 = jnp.float32, jnp.bfloat16
    tm, d = x_ref.shape[1], x_ref.shape[2]
    x = x_ref[0]
    sh1 = mod_ref[0, 0:1, :]
    sc1 = mod_ref[0, 1:2, :]
    h = x * lax.rsqrt(jnp.mean(x * x, axis=-1, keepdims=True) + EPS) * n1g_ref[...]
    hb = (h * (1.0 + sc1) + sh1).astype(bf16)

    qkv = jnp.dot(hb, wqkv_ref[...], preferred_element_type=f32)
    qkv_ref[:, 0:ATT_WIDTH] = (qkv[:, 0:ATT_WIDTH] * (ATT_HEAD_DIM ** -0.5)).astype(bf16)
    qkv_ref[:, ATT_WIDTH:] = qkv[:, ATT_WIDTH:].astype(bf16)

    f = jnp.dot(hb, wf_ref[...], preferred_element_type=f32) + bf_ref[...]
    logf = jnp.minimum(f, 0.0) - jnp.log1p(jnp.exp(-jnp.abs(f)))

    @pl.when(pl.program_id(1) == 0)
    def _():
        carry[...] = jnp.zeros_like(carry)

    tri = (lax.broadcasted_iota(jnp.int32, (tm, tm), 0)
           >= lax.broadcasted_iota(jnp.int32, (tm, tm), 1)).astype(f32)
    cum = jnp.dot(tri, logf, precision=lax.Precision.HIGHEST, preferred_element_type=f32) + carry[...]
    cum_ref[...] = cum
    carry[...] = cum[tm - 1:tm, :]

    gz = _gelu(jnp.dot(hb, wz_ref[...], preferred_element_type=f32))
    u = gz[:, 0:GM_WIDTH]
    v = gz[:, GM_WIDTH:]
    mu = jnp.mean(v, axis=-1, keepdims=True)
    var = jnp.mean(jnp.square(v - mu), axis=-1, keepdims=True)
    vn = ((v - mu) * lax.rsqrt(var + EPS) * lng_ref[...]).astype(bf16)
    tril = (lax.broadcasted_iota(jnp.int32, (GM_CHUNK, GM_CHUNK), 0)
            >= lax.broadcasted_iota(jnp.int32, (GM_CHUNK, GM_CHUNK), 1))
    w_sp = [jnp.where(tril, wsp_ref[g], 0.0).astype(bf16) for g in range(GM_GROUPS)]
    rows = []
    for ck in range(tm // GM_CHUNK):
        r0 = ck * GM_CHUNK
        cols = []
        for g in range(GM_GROUPS):
            c0 = g * GM_GROUP_DIM
            mixed = jnp.dot(w_sp[g], vn[r0:r0 + GM_CHUNK, c0:c0 + GM_GROUP_DIM],
                            preferred_element_type=f32) + bsp_ref[g]
            cols.append(u[r0:r0 + GM_CHUNK, c0:c0 + GM_GROUP_DIM] * mixed)
        rows.append(jnp.concatenate(cols, axis=1))
    yb = jnp.concatenate(rows, axis=0).astype(bf16)
    ybp = jnp.dot(yb, wb_ref[...], preferred_element_type=f32)

    sg = jax.nn.sigmoid(jnp.dot(hb, wg_ref[...], preferred_element_type=f32))
    sga_ref[...] = sg[:, 0:d].astype(bf16)
    gb_ref[...] = (sg[:, d:] * ybp).astype(bf16)


def _input_projection(x, mod, norm1_g, w_in, b_forget, ln_v_g, w_spatial, b_spatial, w_branch_b):
    B, S, d = x.shape
    n = B * S
    tm = INPROJ_TOK_BLOCK
    bf16 = jnp.bfloat16
    p0, p1, p2, p3, p4, p5 = SPLIT_POINTS
    w_qkv = w_in[:, 0:p2].astype(bf16)
    w_f = jnp.pad(w_in[:, p2:p3], ((0, 0), (0, LANES - ATT_HEADS))).astype(bf16)
    b_f = jnp.pad(b_forget, (0, LANES - ATT_HEADS)).reshape(1, LANES)
    w_z = w_in[:, p3:p4].astype(bf16)
    w_g = w_in[:, p4:].astype(bf16)
    nt = S // tm
    tok = lambda w: pl.BlockSpec((tm, w), lambda b, i: (b * nt + i, 0))
    full = lambda a: pl.BlockSpec(a.shape, lambda b, i: (0,) * a.ndim)
    args = (x, mod, norm1_g.reshape(1, d), w_qkv, w_f, b_f, w_z, w_g, ln_v_g.reshape(1, GM_WIDTH),
            w_spatial, b_spatial.reshape(GM_GROUPS, GM_CHUNK, 1), w_branch_b.astype(bf16))
    return pl.pallas_call(
        _inproj_kernel,
        grid=(B, nt),
        in_specs=[pl.BlockSpec((1, tm, d), lambda b, i: (b, i, 0)),
                  pl.BlockSpec((1, 6, d), lambda b, i: (b, 0, 0))] + [full(a) for a in args[2:]],
        out_specs=[tok(3 * ATT_WIDTH), tok(LANES), tok(d), tok(d)],
        out_shape=[jax.ShapeDtypeStruct((n, 3 * ATT_WIDTH), bf16),
                   jax.ShapeDtypeStruct((n, LANES), jnp.float32),
                   jax.ShapeDtypeStruct((n, d), bf16),
                   jax.ShapeDtypeStruct((n, d), bf16)],
        scratch_shapes=[pltpu.VMEM((1, LANES), jnp.float32)],
        compiler_params=pltpu.CompilerParams(dimension_semantics=("arbitrary", "arbitrary"),
                                             vmem_limit_bytes=56 * 1024 * 1024),
        cost_estimate=pl.CostEstimate(
            flops=2 * n * d * (w_in.shape[1] + LANES - ATT_HEADS) + 2 * n * GM_WIDTH * (d + GM_CHUNK)
            + 2 * n * tm * LANES,
            transcendentals=n * (4 * d + 2 * LANES),
            bytes_accessed=4 * n * d + 2 * n * (3 * ATT_WIDTH + 2 * d) + 4 * n * LANES
            + 2 * d * (w_in.shape[1] + LANES) + 2 * GM_WIDTH * d),
        name="input_projection",
    )(*args)


ATT_BLOCK = 512


def _fox_kernel(q_ref, k_ref, v_ref, cq_ref, ck_ref, o_ref, *, blk):
    f32 = jnp.float32
    i = pl.program_id(2)
    q = q_ref[0, 0]
    cq = cq_ref[0, 0]

    def block(j, carry, masked):
        m, l, acc = carry
        off = pl.multiple_of(j * blk, blk)
        k = k_ref[0, 0, pl.ds(off, blk), :]
        v = v_ref[0, 0, pl.ds(off, blk), :]
        s = lax.dot_general(q, k, (((1,), (1,)), ((), ())), preferred_element_type=f32)
        s = s + (cq - ck_ref[0, 0, j])
        if masked:
            causal = (lax.broadcasted_iota(jnp.int32, (blk, blk), 0)
                      >= lax.broadcasted_iota(jnp.int32, (blk, blk), 1))
            s = jnp.where(causal, s, -jnp.inf)
        m_new = jnp.maximum(m, jnp.max(s, axis=1, keepdims=True))
        alpha = jnp.exp(m - m_new)
        p = jnp.exp(s - m_new)
        l = alpha * l + jnp.sum(p, axis=1, keepdims=True)
        acc = alpha * acc + jnp.dot(p.astype(v.dtype), v, preferred_element_type=f32)
        return m_new, l, acc

    init = (jnp.full((blk, 1), -1e30, f32), jnp.zeros((blk, 1), f32),
            jnp.zeros((blk, q.shape[1]), f32))
    carry = lax.fori_loop(0, i, lambda j, c: block(j, c, False), init)
    m, l, acc = block(i, carry, True)
    o_ref[0, 0] = (acc / l).astype(o_ref.dtype)


def _fox_attention(q, k, v, cum):
    B, H, S, dh = q.shape
    blk = min(ATT_BLOCK, S)
    nb = S // blk
    cq = cum.reshape(B, H, S, 1)
    ck = cum.reshape(B, H, nb, 1, blk)
    return pl.pallas_call(
        functools.partial(_fox_kernel, blk=blk),
        grid=(B, H, nb),
        in_specs=[pl.BlockSpec((1, 1, blk, dh), lambda b, h, i: (b, h, i, 0)),
                  pl.BlockSpec((1, 1, S, dh), lambda b, h, i: (b, h, 0, 0)),
                  pl.BlockSpec((1, 1, S, dh), lambda b, h, i: (b, h, 0, 0)),
                  pl.BlockSpec((1, 1, blk, 1), lambda b, h, i: (b, h, i, 0)),
                  pl.BlockSpec((1, 1, nb, 1, blk), lambda b, h, i: (b, h, 0, 0, 0))],
        out_specs=pl.BlockSpec((1, 1, blk, dh), lambda b, h, i: (b, h, i, 0)),
        out_shape=jax.ShapeDtypeStruct((B, H, S, dh), jnp.bfloat16),
        compiler_params=pltpu.CompilerParams(
            dimension_semantics=("arbitrary", "arbitrary", "arbitrary")),
        cost_estimate=pl.CostEstimate(flops=2 * B * H * S * (S + blk) * dh,
                                      transcendentals=B * H * S * (S + blk) // 2,
                                      bytes_accessed=8 * B * H * S * dh + 8 * B * H * S),
        name="fox_attention",
    )(q, k, v, cq, ck)


def _topk_rows(s, k):
    rows, t = s.shape
    iota = lax.broadcasted_iota(jnp.int32, (rows, t), 0)
    slot = lax.broadcasted_iota(jnp.int32, (k, t), 0)
    vals = jnp.zeros((k, t), jnp.float32)
    ids = jnp.zeros((k, t), jnp.int32)
    for j in range(k):
        m = jnp.max(s, axis=0, keepdims=True)
        am = jnp.min(jnp.where(s == m, iota, rows), axis=0, keepdims=True)
        vals = jnp.where(slot == j, m, vals)
        ids = jnp.where(slot == j, am, ids)
        s = jnp.where(iota == am, -jnp.inf, s)
    return vals, ids


def _select_rows(table, pos):
    out = jnp.zeros(pos.shape, table.dtype)
    for r in range(table.shape[0]):
        out = jnp.where(pos == r, table[r:r + 1, :], out)
    return out


def _post_kernel(ya_ref, sga_ref, gb_ref, x_ref, mod_ref, n2g_ref, wa_ref, wo_ref, wq_ref, keys_ref,
                 x1_ref, h2_ref, idx_ref, gates_ref, q_scr):
    f32 = jnp.float32
    a = jnp.dot(ya_ref[...], wa_ref[...], preferred_element_type=f32)
    merged = sga_ref[...].astype(f32) * a + gb_ref[...].astype(f32)
    o = jnp.dot(merged.astype(jnp.bfloat16), wo_ref[...], preferred_element_type=f32)
    g1 = mod_ref[0, 2:3, :]
    sh2 = mod_ref[0, 3:4, :]
    sc2 = mod_ref[0, 4:5, :]
    x1 = x_ref[...] + g1 * o
    x1_ref[...] = x1
    h2 = x1 * lax.rsqrt(jnp.mean(x1 * x1, axis=-1, keepdims=True) + EPS) * n2g_ref[...]
    h2 = h2 * (1.0 + sc2) + sh2
    h2_ref[...] = h2
    qp = jnp.dot(h2.astype(jnp.bfloat16), wq_ref[...], preferred_element_type=f32)
    for j in range(2 * PEER_HEADS):
        q_scr[j] = qp[:, j * PEER_HALF:(j + 1) * PEER_HALF]

    def head(h, carry):
        tops = []
        for p in range(2):
            q = q_scr[2 * h + p]
            keys = keys_ref[2 * h + p]
            sc = lax.dot_general(keys, q, (((1,), (1,)), ((), ())),
                                 precision=lax.Precision.HIGHEST,
                                 preferred_element_type=f32)
            tops.append(_topk_rows(sc, PEER_TOPK))
        (s1, i1), (s2, i2) = tops
        half = PEER_TOPK // 2
        blocks = [s1[0:1, :] + s2]
        blocks += [s1[a:a + 1, :] + s2[0:half, :] for a in range(1, half)]
        blocks += [s1[half:, :] + s2[0:1, :]]
        vals, pos = _topk_rows(jnp.concatenate(blocks, axis=0), PEER_TOPK)
        mid = pos - PEER_TOPK
        tail0 = PEER_TOPK + (half - 1) * half
        ra = jnp.where(pos < PEER_TOPK, 0,
                       jnp.where(pos < tail0, 1 + (mid >> (half.bit_length() - 1)), pos - tail0 + half))
        rb = jnp.where(pos < PEER_TOPK, pos, jnp.where(pos < tail0, mid & (half - 1), 0))
        eid = _select_rows(i1, ra) * N_KEYS + _select_rows(i2, rb)
        e = jnp.exp(vals - vals[0:1, :])
        g = e / jnp.sum(e, axis=0, keepdims=True)
        row = pl.multiple_of(h * PEER_TOPK, PEER_TOPK)
        idx_ref[pl.ds(row, PEER_TOPK), :] = eid
        gates_ref[pl.ds(row, PEER_TOPK), :] = g
        return carry

    lax.fori_loop(0, PEER_HEADS, head, 0)


POST_TOK_BLOCK = 256


def _post_attention(ya, sga, gb, x, mod, norm2_g, w_a, w_out, w_query, sub_keys, seq_len):
    n, d = x.shape
    tm = POST_TOK_BLOCK
    blocks_per_seq = seq_len // tm
    aw = ya.shape[1]
    qw = w_query.shape[1]
    keys = sub_keys.reshape(2 * PEER_HEADS, N_KEYS, PEER_HALF)
    tok = lambda w: pl.BlockSpec((tm, w), lambda i: (i, 0))
    full = lambda a: pl.BlockSpec(a.shape, lambda i: (0,) * a.ndim)
    return pl.pallas_call(
        _post_kernel,
        grid=(n // tm,),
        in_specs=[tok(aw), tok(d), tok(d), tok(d),
                  pl.BlockSpec((1, 6, d), lambda i: (i // blocks_per_seq, 0, 0)),
                  full(norm2_g), full(w_a), full(w_out), full(w_query), full(keys)],
        out_specs=[tok(d), tok(d),
                   pl.BlockSpec((PEER_SLOTS, tm), lambda i: (0, i)),
                   pl.BlockSpec((PEER_SLOTS, tm), lambda i: (0, i))],
        out_shape=[jax.ShapeDtypeStruct((n, d), jnp.float32),
                   jax.ShapeDtypeStruct((n, d), jnp.float32),
                   jax.ShapeDtypeStruct((PEER_SLOTS, n), jnp.int32),
                   jax.ShapeDtypeStruct((PEER_SLOTS, n), jnp.float32)],
        scratch_shapes=[pltpu.VMEM((2 * PEER_HEADS, tm, PEER_HALF), jnp.float32)],
        compiler_params=pltpu.CompilerParams(dimension_semantics=("arbitrary",),
                                             vmem_limit_bytes=48 * 1024 * 1024),
        cost_estimate=pl.CostEstimate(
            flops=2 * n * d * (aw + d + qw) + 2 * n * qw * N_KEYS,
            transcendentals=n * PEER_SLOTS,
            bytes_accessed=n * (2 * aw + 4 * d + 12 * d + 8 * PEER_SLOTS) + 2 * d * (aw + d + qw)),
        name="post_attention",
    )(ya, sga, gb, x, mod, norm2_g, w_a, w_out, w_query, keys)


def _rowsum_bcast(p, ones_bf16):
    hi = p.astype(jnp.bfloat16)
    lo = (p - hi.astype(jnp.float32)).astype(jnp.bfloat16)
    return (jnp.dot(hi, ones_bf16, preferred_element_type=jnp.float32)
            + jnp.dot(lo, ones_bf16, preferred_element_type=jnp.float32))


def _word_halves(w):
    return (lax.bitcast_convert_type(w & jnp.uint32(0xFFFF0000), jnp.float32),
            lax.bitcast_convert_type(w << 16, jnp.float32))


def _eval_experts(chunk, hrow, grow):
    half = hrow.shape[1] // LANES // 2
    ones_bf16 = jnp.ones((LANES, LANES), jnp.bfloat16)
    eye = (lax.broadcasted_iota(jnp.int32, (PEER_SLOTS, LANES), 0)
           == lax.broadcasted_iota(jnp.int32, (PEER_SLOTS, LANES), 1))
    hpart = lambda c: hrow[:, c * LANES:(c + 1) * LANES]
    psum = None
    for c in range(half):
        hi, lo = _word_halves(chunk(c))
        p = hi * hpart(c) + lo * hpart(c + half)
        psum = p if psum is None else psum + p
    act = _gelu(_rowsum_bcast(psum, ones_bf16))
    gcol = _rowsum_bcast(jnp.where(eye, grow, 0.0), ones_bf16)
    coef = gcol * act
    outs_hi, outs_lo = [], []
    for c in range(half):
        hi, lo = _word_halves(chunk(c + half))
        outs_hi.append(jnp.sum(coef * hi, axis=0, keepdims=True))
        outs_lo.append(jnp.sum(coef * lo, axis=0, keepdims=True))
    return jnp.concatenate(outs_hi + outs_lo, axis=-1)


def _finish_block(x_ref, g2_ref, fg_ref, peer, out_ref):
    y = x_ref[...] + g2_ref[0] * peer[...]
    out_ref[...] = y * lax.rsqrt(jnp.mean(y * y, axis=-1, keepdims=True) + EPS) * fg_ref[...]


def _expert_kernel(idx_ref, gates_ref, h_ref, x_ref, g2_ref, fg_ref, tab_ref, out_ref,
                   *scratch, tok_block, n_slots):
    bufs = scratch[:n_slots]
    peer, sem = scratch[n_slots], scratch[n_slots + 1]
    d_model = h_ref.shape[-1]
    n_chunks = d_model // LANES
    rows_per_tok = PEER_SLOTS * n_chunks

    def issue(t, s):
        for r in range(PEER_SLOTS):
            row = pl.multiple_of(idx_ref[t, r] * n_chunks, n_chunks)
            pltpu.make_async_copy(tab_ref.at[pl.ds(row, n_chunks), :],
                                  bufs[s].at[pl.ds(r * n_chunks, n_chunks), :],
                                  sem.at[s]).start(priority=r % DMA_THREADS)

    def wait(s):
        pltpu.make_async_copy(tab_ref.at[pl.ds(0, rows_per_tok), :], bufs[s], sem.at[s]).wait()

    def compute(t, s):
        chunk = lambda c: bufs[s][pl.ds(c, PEER_SLOTS, stride=n_chunks), :]
        peer[pl.ds(t, 1), :] = _eval_experts(chunk, h_ref[pl.ds(t, 1), :], gates_ref[pl.ds(t, 1), :])

    def step(t, s, prefetch):
        wait(s)
        if prefetch:
            issue(t + n_slots - 1, (s - 1) % n_slots)
        compute(t, s)

    for t in range(n_slots - 1):
        issue(t, t)

    n_groups = tok_block // n_slots

    def group(g, carry):
        for s in range(n_slots):
            step(g * n_slots + s, s, True)
        return carry

    lax.fori_loop(0, n_groups - 1, group, 0)
    for s in range(n_slots):
        t = (n_groups - 1) * n_slots + s
        step(t, s, t + n_slots - 1 < tok_block)

    _finish_block(x_ref, g2_ref, fg_ref, peer, out_ref)


def _expert_cost(n_tok, d):
    pairs = n_tok * PEER_SLOTS
    return pl.CostEstimate(flops=4 * pairs * d, transcendentals=pairs,
                           bytes_accessed=4 * pairs * d + 12 * n_tok * d + 8 * pairs)


def _sc_tanh(y):
    return 1.0 - 2.0 / (jnp.exp(2.0 * y) + 1.0)


def _sc_peer_experts(table3, ids, gates, h3):
    n_tok, n_chunks, lanes = h3.shape
    info = plsc.get_sparse_core_info()
    sl = info.num_lanes
    n_workers = info.num_cores * info.num_subcores
    tok_per_worker = n_tok // n_workers
    assert tok_per_worker * n_workers == n_tok and tok_per_worker % 2 == 0
    rows = SC_GATHER_ROWS
    n_sub = PEER_SLOTS // rows
    assert n_sub % 2 == 0
    half = n_chunks // 2
    pieces = [(c, k * sl) for c in range(half) for k in range(lanes // sl)]
    mesh = plsc.VectorSubcoreMesh(core_axis_name="c", subcore_axis_name="s")

    buf = lambda dt: pltpu.VMEM((rows, n_chunks, lanes), dt)

    @functools.partial(
        pl.kernel, mesh=mesh,
        out_type=jax.ShapeDtypeStruct((n_tok, n_chunks, lanes), jnp.float32),
        scratch_types=[pltpu.VMEM((PEER_SLOTS,), jnp.int32), pltpu.VMEM((PEER_SLOTS,), jnp.int32),
                       pltpu.VMEM((PEER_SLOTS,), jnp.float32), pltpu.VMEM((PEER_SLOTS,), jnp.float32),
                       pltpu.VMEM((n_chunks, lanes), jnp.float32),
                       pltpu.VMEM((n_chunks, lanes), jnp.float32),
                       buf(jnp.uint32), buf(jnp.uint32), pltpu.VMEM((n_chunks, lanes), jnp.float32),
                       pltpu.SemaphoreType.DMA((2,)), pltpu.SemaphoreType.DMA((2,))],
        compiler_params=pltpu.CompilerParams(needs_layout_passes=False),
        cost_estimate=_expert_cost(n_tok, n_chunks * lanes),
        name="sc_peer_experts")
    def run(table_hbm, ids_hbm, gates_hbm, h_hbm, peer_hbm, ids0, ids1, g0, g1, h0, h1, rows0, rows1,
            out_v, gsem, isem):
        bufs = (rows0, rows1)
        ids_p, g_p, h_p = (ids0, ids1), (g0, g1), (h0, h1)
        t_first = (lax.axis_index("s") * info.num_cores + lax.axis_index("c")) * tok_per_worker

        def token_inputs(t, p):
            off = pl.multiple_of(t * PEER_SLOTS, PEER_SLOTS)
            return (pltpu.make_async_copy(ids_hbm.at[pl.ds(off, PEER_SLOTS)], ids_p[p], isem.at[p]),
                    pltpu.make_async_copy(gates_hbm.at[pl.ds(off, PEER_SLOTS)], g_p[p], isem.at[p]),
                    pltpu.make_async_copy(h_hbm.at[t], h_p[p], isem.at[p]))

        def gather(p, s):
            return pltpu.make_async_copy(table_hbm.at[ids_p[p].at[pl.ds(s * rows, rows)]],
                                         bufs[s % 2], gsem.at[s % 2])

        for cp in token_inputs(t_first, 0):
            cp.start()
        for cp in token_inputs(t_first, 0):
            cp.wait()
        gather(0, 0).start()

        def one_token(ti, p):
            t = t_first + ti
            g_v, h_v = g_p[p], h_p[p]
            has_next = ti + 1 < tok_per_worker

            @pl.when(has_next)
            def _():
                for cp in token_inputs(t + 1, 1 - p):
                    cp.start()

            for c, k in pieces:
                out_v[c, pl.ds(k, sl)] = jnp.zeros((sl,), jnp.float32)
                out_v[c + half, pl.ds(k, sl)] = jnp.zeros((sl,), jnp.float32)
            for s in range(n_sub):
                gather(p, s).wait()
                if s + 1 < n_sub:
                    gather(p, s + 1).start()
                else:
                    @pl.when(has_next)
                    def _():
                        for cp in token_inputs(t + 1, 1 - p):
                            cp.wait()
                        gather(1 - p, 0).start()
                rbuf = bufs[s % 2]

                @pl.loop(0, rows, step=SC_ROW_GROUP)
                def _(r0):
                    accs = [jnp.zeros((sl,), jnp.float32) for _ in range(SC_ROW_GROUP)]
                    for c, k in pieces:
                        h_hi, h_lo = h_v[c, pl.ds(k, sl)], h_v[c + half, pl.ds(k, sl)]
                        for j in range(SC_ROW_GROUP):
                            hi, lo = _word_halves(rbuf[r0 + j, c, pl.ds(k, sl)])
                            accs[j] = accs[j] + hi * h_hi + lo * h_lo
                    coefs = []
                    for j in range(SC_ROW_GROUP):
                        a = jnp.broadcast_to(jnp.sum(accs[j]), (sl,))
                        act = 0.5 * a * (1.0 + _sc_tanh(0.7978845608028654 * (a + 0.044715 * (a * a * a))))
                        gate = plsc.load_gather(g_v, [jnp.broadcast_to(s * rows + r0 + j, (sl,))])
                        coefs.append(gate * act)
                    for b0 in range(0, len(pieces), SC_STORE_BATCH):
                        tots = []
                        for c, k in pieces[b0:b0 + SC_STORE_BATCH]:
                            t_hi = t_lo = None
                            for j in range(SC_ROW_GROUP):
                                hi, lo = _word_halves(rbuf[r0 + j, c + half, pl.ds(k, sl)])
                                t_hi = coefs[j] * hi if t_hi is None else t_hi + coefs[j] * hi
                                t_lo = coefs[j] * lo if t_lo is None else t_lo + coefs[j] * lo
                            tots.append((t_hi, t_lo))
                        for (c, k), (t_hi, t_lo) in zip(pieces[b0:b0 + SC_STORE_BATCH], tots):
                            plsc.addupdate(out_v.at[c, pl.ds(k, sl)], t_hi)
                            plsc.addupdate(out_v.at[c + half, pl.ds(k, sl)], t_lo)

            pltpu.sync_copy(out_v, peer_hbm.at[t])

        @pl.loop(0, tok_per_worker, step=2)
        def _(ti):
            one_token(ti, 0)
            one_token(ti + 1, 1)

    return run(table3, ids, gates, h3)


def _finish_kernel(peer_ref, x_ref, g2_ref, fg_ref, prev_ref, out_ref):
    del prev_ref
    _finish_block(x_ref, g2_ref, fg_ref, peer_ref, out_ref)


def _finish_tokens(peer, x1, g2, fg, prev, seq_len):
    d = x1.shape[1]
    tm = FINISH_TOK_BLOCK
    return pl.pallas_call(
        _finish_kernel,
        grid=(peer.shape[0] // tm,),
        in_specs=[pl.BlockSpec((tm, d), lambda i: (i, 0)),
                  pl.BlockSpec((tm, d), lambda i: (i, 0)),
                  pl.BlockSpec((1, 1, d), lambda i: (i * tm // seq_len, 0, 0)),
                  pl.BlockSpec((1, d), lambda i: (0, 0)),
                  pl.BlockSpec(memory_space=pl.ANY)],
        out_specs=pl.BlockSpec((tm, d), lambda i: (i, 0)),
        out_shape=jax.ShapeDtypeStruct(prev.shape, prev.dtype),
        input_output_aliases={4: 0},
        compiler_params=pltpu.CompilerParams(dimension_semantics=("arbitrary",)),
        name="finish_tokens",
    )(peer, x1, g2, fg, prev)


def _pack_expert_table(expert_u, expert_v):
    n_experts, d = expert_u.shape
    bits = lambda a: lax.bitcast_convert_type(a.astype(jnp.bfloat16), jnp.uint16).astype(jnp.uint32)
    pair = lambda b: (b[:, :d // 2] << 16) | b[:, d // 2:]
    words = jnp.concatenate([pair(bits(expert_u)), pair(bits(expert_v))], axis=1)
    return words.reshape(n_experts, d // LANES, LANES)


def _tc_peer_experts(idx, gates, h2, x1, g2, final_g, table, seq_len, n_out):
    m, d = h2.shape
    n_chunks = d // LANES
    tb = EXPERT_TOK_BLOCK
    assert tb % EXPERT_SLOTS == 0 and seq_len % tb == 0 and (n_out - m) % tb == 0
    first = (n_out - m) // tb
    return pl.pallas_call(
        functools.partial(_expert_kernel, tok_block=tb, n_slots=EXPERT_SLOTS),
        grid=(m // tb,),
        in_specs=[
            pl.BlockSpec((tb, PEER_SLOTS), lambda i: (i, 0), memory_space=pltpu.SMEM),
            pl.BlockSpec((tb, PEER_SLOTS), lambda i: (i, 0)),
            pl.BlockSpec((tb, d), lambda i: (i, 0)),
            pl.BlockSpec((tb, d), lambda i: (i, 0)),
            pl.BlockSpec((1, 1, d), lambda i: (i * tb // seq_len, 0, 0)),
            pl.BlockSpec((1, d), lambda i: (0, 0)),
            pl.BlockSpec(memory_space=pl.ANY),
        ],
        out_specs=pl.BlockSpec((tb, d), lambda i: (i + first, 0)),
        out_shape=jax.ShapeDtypeStruct((n_out, d), jnp.float32),
        scratch_shapes=(
            [pltpu.VMEM((PEER_SLOTS * n_chunks, LANES), jnp.uint32) for _ in range(EXPERT_SLOTS)]
            + [pltpu.VMEM((tb, d), jnp.float32), pltpu.SemaphoreType.DMA((EXPERT_SLOTS,))]),
        compiler_params=pltpu.CompilerParams(dimension_semantics=("arbitrary",)),
        cost_estimate=_expert_cost(m, d),
        name="peer_experts",
    )(idx, gates, h2, x1, g2, final_g.reshape(1, d), table.reshape(-1, LANES))


def _mix_and_route(x, mod, norm1_g, w_in, b_forget, ln_v_g, w_spatial, b_spatial, w_branch_a,
                   w_branch_b, w_out, norm2_g, w_query, sub_keys):
    Bp, S, D = x.shape
    m = Bp * S
    bf16 = jnp.bfloat16
    qkv, cum, sga, gb = _input_projection(x, mod, norm1_g, w_in, b_forget, ln_v_g,
                                          w_spatial, b_spatial, w_branch_b)
    heads = lambda a: a.reshape(Bp, S, ATT_HEADS, ATT_HEAD_DIM).transpose(0, 2, 1, 3)
    q = heads(qkv[:, 0:ATT_WIDTH])
    k = heads(qkv[:, ATT_WIDTH:2 * ATT_WIDTH])
    v = heads(qkv[:, 2 * ATT_WIDTH:])
    cum_h = cum[:, 0:ATT_HEADS].reshape(Bp, S, ATT_HEADS).transpose(0, 2, 1)
    y_a = _fox_attention(q, k, v, cum_h).transpose(0, 2, 1, 3).reshape(m, ATT_WIDTH)
    x1, h2, idx_t, gates_t = _post_attention(
        y_a, sga, gb, x.reshape(m, D), mod, norm2_g.reshape(1, D),
        w_branch_a.astype(bf16), w_out.astype(bf16), w_query.astype(bf16), sub_keys, S)
    return x1, h2, idx_t.T, gates_t.T


def kernel(x, c, w_mod, b_mod, norm1_g, w_in, b_forget, ln_v_g, w_spatial, b_spatial, w_branch_a, w_branch_b, w_out, norm2_g, w_query, sub_keys, expert_u, expert_v, final_g):
    B, S, D = x.shape
    n = B * S
    assert w_mod.shape[0] == 1, "the final RMSNorm is fused into the single layer's expert kernels"
    l = 0
    mod = _modulation(c, w_mod[l], b_mod[l]).reshape(B, 6, D)
    table = _pack_expert_table(expert_u[l], expert_v[l])
    n_chunks = D // LANES
    weights = (norm1_g[l], w_in[l], b_forget[l], ln_v_g[l], w_spatial[l], b_spatial[l],
               w_branch_a[l], w_branch_b[l], w_out[l], norm2_g[l], w_query[l], sub_keys[l])

    b_sc = min((B * SC_SHARE_PERCENT + 50) // 100, B - 1)
    if b_sc > 0:
        m_a = b_sc * S
        x1_a, h2_a, idx_a, gates_a = _mix_and_route(x[:b_sc], mod[:b_sc], *weights)
        peer_sc = _sc_peer_experts(table, idx_a.reshape(-1), gates_a.reshape(-1),
                                   h2_a.reshape(m_a, n_chunks, LANES))
    x1_b, h2_b, idx_b, gates_b = _mix_and_route(x[b_sc:], mod[b_sc:], *weights)
    out = _tc_peer_experts(idx_b, gates_b, h2_b, x1_b, mod[b_sc:, 5:6, :], final_g, table, S, n)
    if b_sc > 0:
        out = _finish_tokens(peer_sc.reshape(m_a, D), x1_a, mod[:b_sc, 5:6, :],
                             final_g.reshape(1, D), out, S)
    return out.reshape(B, S, D)
```

```python
import functools

import jax
import jax.numpy as jnp
from jax import lax
from jax.experimental import pallas as pl
from jax.experimental.pallas import tpu as pltpu
from jax.experimental.pallas import tpu_sc as plsc

D_MODEL = 1024
ATT_HEADS = 8
ATT_HEAD_DIM = 64
ATT_WIDTH = ATT_HEADS * ATT_HEAD_DIM
Q_BLOCK = 128
GM_GROUPS = 4
GM_GROUP_DIM = 128
GM_WIDTH = GM_GROUPS * GM_GROUP_DIM
GM_CHUNK = 128
PEER_HEADS = 8
PEER_KEY_DIM = 256
PEER_HALF = PEER_KEY_DIM // 2
N_KEYS = 128
PEER_TOPK = 16
PEER_TOK_BLOCK = 128
PEER_SLOTS = PEER_HEADS * PEER_TOPK
SPLIT_POINTS = (ATT_WIDTH, 2 * ATT_WIDTH, 3 * ATT_WIDTH, 3 * ATT_WIDTH + ATT_HEADS,
                3 * ATT_WIDTH + ATT_HEADS + 2 * GM_WIDTH,
                3 * ATT_WIDTH + ATT_HEADS + 2 * GM_WIDTH + D_MODEL)
EPS = 1e-6

LANES = 128
EXPERT_TOK_BLOCK = 128
EXPERT_SLOTS = 4
DMA_THREADS = 2
FINISH_TOK_BLOCK = 512
SC_GATHER_ROWS = 32
SC_ROW_GROUP = 4
SC_STORE_BATCH = 8
SC_SHARE_PERCENT = 62
SC_EARLY_SHARES = (10, 15)
SC_CUT_ALIGN = 1024


def _gelu(x):
    return 0.5 * x * (1.0 + jnp.tanh(0.7978845608028654 * (x + 0.044715 * (x * x * x))))


def _rmsnorm(x, g):
    y = x * lax.rsqrt(jnp.mean(x * x, axis=-1, keepdims=True) + EPS)
    return y * g


def _layernorm(x, g):
    mu = jnp.mean(x, axis=-1, keepdims=True)
    var = jnp.mean(jnp.square(x - mu), axis=-1, keepdims=True)
    return (x - mu) * lax.rsqrt(var + EPS) * g


def _modulate(h, shift, scale):
    return h * (1.0 + scale[:, None, :]) + shift[:, None, :]


def _fox(q, k, v, log_f):
    B, S, H, Dh = q.shape
    nb = S // Q_BLOCK
    cum = jnp.cumsum(log_f, axis=1)
    cum_k = jnp.transpose(cum, (0, 2, 1))
    k_pos = jnp.arange(S)
    q_blocks = q.reshape(B, nb, Q_BLOCK, H, Dh).transpose(1, 0, 2, 3, 4)
    cq_blocks = cum_k.reshape(B, H, nb, Q_BLOCK).transpose(2, 0, 1, 3)
    starts = jnp.arange(nb) * Q_BLOCK
    scale = ATT_HEAD_DIM ** -0.5

    def one_block(args):
        q_blk, cq_blk, start = args
        s = jnp.einsum('bqhd,bkhd->bhqk', q_blk, k) * scale
        s = s + cq_blk[..., :, None] - cum_k[..., None, :]
        q_pos = start + jnp.arange(Q_BLOCK)
        causal = q_pos[:, None] >= k_pos[None, :]
        s = jnp.where(causal[None, None], s, -jnp.inf)
        p = jax.nn.softmax(s, axis=-1)
        return jnp.einsum('bhqk,bkhd->bqhd', p, v)

    out = lax.map(one_block, (q_blocks, cq_blocks, starts))
    return out.transpose(1, 0, 2, 3, 4).reshape(B, S, H * Dh)


def _gmlp(z, ln_g, w_s, b_s):
    B, S, _ = z.shape
    u, v = jnp.split(z, 2, axis=-1)
    v = _layernorm(v, ln_g)
    nc = S // GM_CHUNK
    v = v.reshape(B, nc, GM_CHUNK, GM_GROUPS, GM_GROUP_DIM)
    tril = jnp.tril(jnp.ones((GM_CHUNK, GM_CHUNK), dtype=bool))
    w = jnp.where(tril[None], w_s, jnp.zeros_like(w_s))
    mixed = jnp.einsum('gts,bcsgd->bctgd', w, v) + b_s.T[None, None, :, :, None]
    return u * mixed.reshape(B, S, GM_WIDTH)


def _peer_route(hf, w_query, sub_keys):
    n = hf.shape[0]
    q = (hf @ w_query).reshape(n, PEER_HEADS, 2, PEER_HALF)
    scores = jnp.einsum('nhpd,hpkd->nhpk', q, sub_keys)
    s_top, i_top = lax.top_k(scores, PEER_TOPK)
    cand_s = (s_top[:, :, 0, :, None] + s_top[:, :, 1, None, :]).reshape(n, PEER_HEADS, PEER_TOPK * PEER_TOPK)
    cand_i = (i_top[:, :, 0, :, None] * N_KEYS + i_top[:, :, 1, None, :]).reshape(n, PEER_HEADS, PEER_TOPK * PEER_TOPK)
    best_s, best_pos = lax.top_k(cand_s, PEER_TOPK)
    idx = jnp.take_along_axis(cand_i, best_pos, axis=-1)
    gates = jax.nn.softmax(best_s, axis=-1)
    return idx.reshape(n, PEER_SLOTS), gates.reshape(n, PEER_SLOTS)


def _mod_kernel(c_ref, w_ref, b_ref, o_ref):
    c = c_ref[...]
    sc = c * jax.nn.sigmoid(c)
    o_ref[...] = jnp.dot(sc, w_ref[...], precision=lax.Precision.HIGHEST,
                         preferred_element_type=jnp.float32) + b_ref[...]


def _modulation(c, w_mod, b_mod):
    b, d = c.shape
    cols = w_mod.shape[1]
    return pl.pallas_call(
        _mod_kernel,
        grid=(cols // d,),
        in_specs=[pl.BlockSpec((b, d), lambda j: (0, 0)),
                  pl.BlockSpec((d, d), lambda j: (0, j)),
                  pl.BlockSpec((1, d), lambda j: (0, j))],
        out_specs=pl.BlockSpec((b, d), lambda j: (0, j)),
        out_shape=jax.ShapeDtypeStruct((b, cols), jnp.float32),
        name="modulation",
    )(c, w_mod, b_mod.reshape(1, cols))


INPROJ_TOK_BLOCK = 256


def _inproj_kernel(x_ref, mod_ref, n1g_ref, wqkv_ref, wf_ref, bf_ref, wz_ref, wg_ref, lng_ref,
                   wsp_ref, bsp_ref, wb_ref, qkv_ref, cum_ref, sga_ref, gb_ref, carry):
    f32, bf16 = jnp.float32, jnp.bfloat16
    tm, d = x_ref.shape[1], x_ref.shape[2]
    x = x_ref[0]
    sh1 = mod_ref[0, 0:1, :]
    sc1 = mod_ref[0, 1:2, :]
    h = x * lax.rsqrt(jnp.mean(x * x, axis=-1, keepdims=True) + EPS) * n1g_ref[...]
    hb = (h * (1.0 + sc1) + sh1).astype(bf16)

    qkv = jnp.dot(hb, wqkv_ref[...], preferred_element_type=f32)
    qkv_ref[:, 0:ATT_WIDTH] = (qkv[:, 0:ATT_WIDTH] * (ATT_HEAD_DIM ** -0.5)).astype(bf16)
    qkv_ref[:, ATT_WIDTH:] = qkv[:, ATT_WIDTH:].astype(bf16)

    f = jnp.dot(hb, wf_ref[...], preferred_element_type=f32) + bf_ref[...]
    logf = jnp.minimum(f, 0.0) - jnp.log1p(jnp.exp(-jnp.abs(f)))

    @pl.when(pl.program_id(1) == 0)
    def _():
        carry[...] = jnp.zeros_like(carry)

    tri = (lax.broadcasted_iota(jnp.int32, (tm, tm), 0)
           >= lax.broadcasted_iota(jnp.int32, (tm, tm), 1)).astype(f32)
    cum = jnp.dot(tri, logf, precision=lax.Precision.HIGHEST, preferred_element_type=f32) + carry[...]
    cum_ref[...] = cum
    carry[...] = cum[tm - 1:tm, :]

    gz = _gelu(jnp.dot(hb, wz_ref[...], preferred_element_type=f32))
    u = gz[:, 0:GM_WIDTH]
    v = gz[:, GM_WIDTH:]
    mu = jnp.mean(v, axis=-1, keepdims=True)
    var = jnp.mean(jnp.square(v - mu), axis=-1, keepdims=True)
    vn = ((v - mu) * lax.rsqrt(var + EPS) * lng_ref[...]).astype(bf16)
    tril = (lax.broadcasted_iota(jnp.int32, (GM_CHUNK, GM_CHUNK), 0)
            >= lax.broadcasted_iota(jnp.int32, (GM_CHUNK, GM_CHUNK), 1))
    w_sp = [jnp.where(tril, wsp_ref[g], 0.0).astype(bf16) for g in range(GM_GROUPS)]
    rows = []
    for ck in range(tm // GM_CHUNK):
        r0 = ck * GM_CHUNK
        cols = []
        for g in range(GM_GROUPS):
            c0 = g * GM_GROUP_DIM
            mixed = jnp.dot(w_sp[g], vn[r0:r0 + GM_CHUNK, c0:c0 + GM_GROUP_DIM],
                            preferred_element_type=f32) + bsp_ref[g]
            cols.append(u[r0:r0 + GM_CHUNK, c0:c0 + GM_GROUP_DIM] * mixed)
        rows.append(jnp.concatenate(cols, axis=1))
    yb = jnp.concatenate(rows, axis=0).astype(bf16)
    ybp = jnp.dot(yb, wb_ref[...], preferred_element_type=f32)

    sg = jax.nn.sigmoid(jnp.dot(hb, wg_ref[...], preferred_element_type=f32))
    sga_ref[...] = sg[:, 0:d].astype(bf16)
    gb_ref[...] = (sg[:, d:] * ybp).astype(bf16)


def _input_projection(x, mod, norm1_g, w_in, b_forget, ln_v_g, w_spatial, b_spatial, w_branch_b):
    B, S, d = x.shape
    n = B * S
    tm = INPROJ_TOK_BLOCK
    bf16 = jnp.bfloat16
    p0, p1, p2, p3, p4, p5 = SPLIT_POINTS
    w_qkv = w_in[:, 0:p2].astype(bf16)
    w_f = jnp.pad(w_in[:, p2:p3], ((0, 0), (0, LANES - ATT_HEADS))).astype(bf16)
    b_f = jnp.pad(b_forget, (0, LANES - ATT_HEADS)).reshape(1, LANES)
    w_z = w_in[:, p3:p4].astype(bf16)
    w_g = w_in[:, p4:].astype(bf16)
    nt = S // tm
    tok = lambda w: pl.BlockSpec((tm, w), lambda b, i: (b * nt + i, 0))
    full = lambda a: pl.BlockSpec(a.shape, lambda b, i: (0,) * a.ndim)
    args = (x, mod, norm1_g.reshape(1, d), w_qkv, w_f, b_f, w_z, w_g, ln_v_g.reshape(1, GM_WIDTH),
            w_spatial, b_spatial.reshape(GM_GROUPS, GM_CHUNK, 1), w_branch_b.astype(bf16))
    return pl.pallas_call(
        _inproj_kernel,
        grid=(B, nt),
        in_specs=[pl.BlockSpec((1, tm, d), lambda b, i: (b, i, 0)),
                  pl.BlockSpec((1, 6, d), lambda b, i: (b, 0, 0))] + [full(a) for a in args[2:]],
        out_specs=[tok(3 * ATT_WIDTH), tok(LANES), tok(d), tok(d)],
        out_shape=[jax.ShapeDtypeStruct((n, 3 * ATT_WIDTH), bf16),
                   jax.ShapeDtypeStruct((n, LANES), jnp.float32),
                   jax.ShapeDtypeStruct((n, d), bf16),
                   jax.ShapeDtypeStruct((n, d), bf16)],
        scratch_shapes=[pltpu.VMEM((1, LANES), jnp.float32)],
        compiler_params=pltpu.CompilerParams(dimension_semantics=("arbitrary", "arbitrary"),
                                             vmem_limit_bytes=56 * 1024 * 1024),
        cost_estimate=pl.CostEstimate(
            flops=2 * n * d * (w_in.shape[1] + LANES - ATT_HEADS) + 2 * n * GM_WIDTH * (d + GM_CHUNK)
            + 2 * n * tm * LANES,
            transcendentals=n * (4 * d + 2 * LANES),
            bytes_accessed=4 * n * d + 2 * n * (3 * ATT_WIDTH + 2 * d) + 4 * n * LANES
            + 2 * d * (w_in.shape[1] + LANES) + 2 * GM_WIDTH * d),
        name="input_projection",
    )(*args)


ATT_BLOCK = 512


def _fox_kernel(q_ref, k_ref, v_ref, cq_ref, ck_ref, o_ref, *, blk):
    f32 = jnp.float32
    i = pl.program_id(2)
    q = q_ref[0, 0]
    cq = cq_ref[0, 0]

    def block(j, carry, masked):
        m, l, acc = carry
        off = pl.multiple_of(j * blk, blk)
        k = k_ref[0, 0, pl.ds(off, blk), :]
        v = v_ref[0, 0, pl.ds(off, blk), :]
        s = lax.dot_general(q, k, (((1,), (1,)), ((), ())), preferred_element_type=f32)
        s = s + (cq - ck_ref[0, 0, j])
        if masked:
            causal = (lax.broadcasted_iota(jnp.int32, (blk, blk), 0)
                      >= lax.broadcasted_iota(jnp.int32, (blk, blk), 1))
            s = jnp.where(causal, s, -jnp.inf)
        m_new = jnp.maximum(m, jnp.max(s, axis=1, keepdims=True))
        alpha = jnp.exp(m - m_new)
        p = jnp.exp(s - m_new)
        l = alpha * l + jnp.sum(p, axis=1, keepdims=True)
        acc = alpha * acc + jnp.dot(p.astype(v.dtype), v, preferred_element_type=f32)
        return m_new, l, acc

    init = (jnp.full((blk, 1), -1e30, f32), jnp.zeros((blk, 1), f32),
            jnp.zeros((blk, q.shape[1]), f32))
    carry = lax.fori_loop(0, i, lambda j, c: block(j, c, False), init)
    m, l, acc = block(i, carry, True)
    o_ref[0, 0] = (acc / l).astype(o_ref.dtype)


def _fox_attention(q, k, v, cum):
    B, H, S, dh = q.shape
    blk = min(ATT_BLOCK, S)
    nb = S // blk
    cq = cum.reshape(B, H, S, 1)
    ck = cum.reshape(B, H, nb, 1, blk)
    return pl.pallas_call(
        functools.partial(_fox_kernel, blk=blk),
        grid=(B, H, nb),
        in_specs=[pl.BlockSpec((1, 1, blk, dh), lambda b, h, i: (b, h, i, 0)),
                  pl.BlockSpec((1, 1, S, dh), lambda b, h, i: (b, h, 0, 0)),
                  pl.BlockSpec((1, 1, S, dh), lambda b, h, i: (b, h, 0, 0)),
                  pl.BlockSpec((1, 1, blk, 1), lambda b, h, i: (b, h, i, 0)),
                  pl.BlockSpec((1, 1, nb, 1, blk), lambda b, h, i: (b, h, 0, 0, 0))],
        out_specs=pl.BlockSpec((1, 1, blk, dh), lambda b, h, i: (b, h, i, 0)),
        out_shape=jax.ShapeDtypeStruct((B, H, S, dh), jnp.bfloat16),
        compiler_params=pltpu.CompilerParams(
            dimension_semantics=("arbitrary", "arbitrary", "arbitrary")),
        cost_estimate=pl.CostEstimate(flops=2 * B * H * S * (S + blk) * dh,
                                      transcendentals=B * H * S * (S + blk) // 2,
                                      bytes_accessed=8 * B * H * S * dh + 8 * B * H * S),
        name="fox_attention",
    )(q, k, v, cq, ck)


def _ordered_after(kernel_fn, pos):
    def wrapped(*refs):
        return kernel_fn(*refs[:pos], *refs[pos + 1:])
    return wrapped


def _topk_rows(s, k):
    rows, t = s.shape
    iota = lax.broadcasted_iota(jnp.int32, (rows, t), 0)
    slot = lax.broadcasted_iota(jnp.int32, (k, t), 0)
    vals = jnp.zeros((k, t), jnp.float32)
    ids = jnp.zeros((k, t), jnp.int32)
    for j in range(k):
        m = jnp.max(s, axis=0, keepdims=True)
        am = jnp.min(jnp.where(s == m, iota, rows), axis=0, keepdims=True)
        vals = jnp.where(slot == j, m, vals)
        ids = jnp.where(slot == j, am, ids)
        s = jnp.where(iota == am, -jnp.inf, s)
    return vals, ids


def _select_rows(table, pos):
    out = jnp.zeros(pos.shape, table.dtype)
    for r in range(table.shape[0]):
        out = jnp.where(pos == r, table[r:r + 1, :], out)
    return out


def _post_kernel(ya_ref, sga_ref, gb_ref, x_ref, mod_ref, n2g_ref, wa_ref, wo_ref, wq_ref, keys_ref,
                 x1_ref, h2_ref, idx_ref, gates_ref, q_scr):
    f32 = jnp.float32
    a = jnp.dot(ya_ref[...], wa_ref[...], preferred_element_type=f32)
    merged = sga_ref[...].astype(f32) * a + gb_ref[...].astype(f32)
    o = jnp.dot(merged.astype(jnp.bfloat16), wo_ref[...], preferred_element_type=f32)
    g1 = mod_ref[0, 2:3, :]
    sh2 = mod_ref[0, 3:4, :]
    sc2 = mod_ref[0, 4:5, :]
    x1 = x_ref[...] + g1 * o
    x1_ref[...] = x1
    h2 = x1 * lax.rsqrt(jnp.mean(x1 * x1, axis=-1, keepdims=True) + EPS) * n2g_ref[...]
    h2 = h2 * (1.0 + sc2) + sh2
    h2_ref[...] = h2
    qp = jnp.dot(h2.astype(jnp.bfloat16), wq_ref[...], preferred_element_type=f32)
    for j in range(2 * PEER_HEADS):
        q_scr[j] = qp[:, j * PEER_HALF:(j + 1) * PEER_HALF]

    def head(h, carry):
        tops = []
        for p in range(2):
            q = q_scr[2 * h + p]
            keys = keys_ref[2 * h + p]
            sc = lax.dot_general(keys, q, (((1,), (1,)), ((), ())),
                                 precision=lax.Precision.HIGHEST,
                                 preferred_element_type=f32)
            tops.append(_topk_rows(sc, PEER_TOPK))
        (s1, i1), (s2, i2) = tops
        half = PEER_TOPK // 2
        blocks = [s1[0:1, :] + s2]
        blocks += [s1[a:a + 1, :] + s2[0:half, :] for a in range(1, half)]
        blocks += [s1[half:, :] + s2[0:1, :]]
        vals, pos = _topk_rows(jnp.concatenate(blocks, axis=0), PEER_TOPK)
        mid = pos - PEER_TOPK
        tail0 = PEER_TOPK + (half - 1) * half
        ra = jnp.where(pos < PEER_TOPK, 0,
                       jnp.where(pos < tail0, 1 + (mid >> (half.bit_length() - 1)), pos - tail0 + half))
        rb = jnp.where(pos < PEER_TOPK, pos, jnp.where(pos < tail0, mid & (half - 1), 0))
        eid = _select_rows(i1, ra) * N_KEYS + _select_rows(i2, rb)
        e = jnp.exp(vals - vals[0:1, :])
        g = e / jnp.sum(e, axis=0, keepdims=True)
        row = pl.multiple_of(h * PEER_TOPK, PEER_TOPK)
        idx_ref[pl.ds(row, PEER_TOPK), :] = eid
        gates_ref[pl.ds(row, PEER_TOPK), :] = g
        return carry

    lax.fori_loop(0, PEER_HEADS, head, 0)


POST_TOK_BLOCK = 256


def _post_attention(ya, sga, gb, x, mod, norm2_g, w_a, w_out, w_query, sub_keys, seq_len, after=None):
    n, d = x.shape
    tm = POST_TOK_BLOCK
    blocks_per_seq = seq_len // tm
    aw = ya.shape[1]
    qw = w_query.shape[1]
    keys = sub_keys.reshape(2 * PEER_HEADS, N_KEYS, PEER_HALF)
    tok = lambda w: pl.BlockSpec((tm, w), lambda i: (i, 0))
    full = lambda a: pl.BlockSpec(a.shape, lambda i: (0,) * a.ndim)
    extra = [] if after is None else [after]
    n_in = 10
    return pl.pallas_call(
        _post_kernel if after is None else _ordered_after(_post_kernel, n_in),
        grid=(n // tm,),
        in_specs=[tok(aw), tok(d), tok(d), tok(d),
                  pl.BlockSpec((1, 6, d), lambda i: (i // blocks_per_seq, 0, 0)),
                  full(norm2_g), full(w_a), full(w_out), full(w_query), full(keys)]
        + [pl.BlockSpec(memory_space=pl.ANY) for _ in extra],
        out_specs=[tok(d), tok(d),
                   pl.BlockSpec((PEER_SLOTS, tm), lambda i: (0, i)),
                   pl.BlockSpec((PEER_SLOTS, tm), lambda i: (0, i))],
        out_shape=[jax.ShapeDtypeStruct((n, d), jnp.float32),
                   jax.ShapeDtypeStruct((n, d), jnp.float32),
                   jax.ShapeDtypeStruct((PEER_SLOTS, n), jnp.int32),
                   jax.ShapeDtypeStruct((PEER_SLOTS, n), jnp.float32)],
        scratch_shapes=[pltpu.VMEM((2 * PEER_HEADS, tm, PEER_HALF), jnp.float32)],
        compiler_params=pltpu.CompilerParams(dimension_semantics=("arbitrary",),
                                             vmem_limit_bytes=48 * 1024 * 1024),
        cost_estimate=pl.CostEstimate(
            flops=2 * n * d * (aw + d + qw) + 2 * n * qw * N_KEYS,
            transcendentals=n * PEER_SLOTS,
            bytes_accessed=n * (2 * aw + 4 * d + 12 * d + 8 * PEER_SLOTS) + 2 * d * (aw + d + qw)),
        name="post_attention",
    )(ya, sga, gb, x, mod, norm2_g, w_a, w_out, w_query, keys, *extra)


def _rowsum_bcast(p, ones_bf16):
    hi = p.astype(jnp.bfloat16)
    lo = (p - hi.astype(jnp.float32)).astype(jnp.bfloat16)
    return (jnp.dot(hi, ones_bf16, preferred_element_type=jnp.float32)
            + jnp.dot(lo, ones_bf16, preferred_element_type=jnp.float32))


def _word_halves(w):
    return (lax.bitcast_convert_type(w & jnp.uint32(0xFFFF0000), jnp.float32),
            lax.bitcast_convert_type(w << 16, jnp.float32))


def _eval_experts(chunk, hrow, grow):
    half = hrow.shape[1] // LANES // 2
    ones_bf16 = jnp.ones((LANES, LANES), jnp.bfloat16)
    eye = (lax.broadcasted_iota(jnp.int32, (PEER_SLOTS, LANES), 0)
           == lax.broadcasted_iota(jnp.int32, (PEER_SLOTS, LANES), 1))
    hpart = lambda c: hrow[:, c * LANES:(c + 1) * LANES]
    psum = None
    for c in range(half):
        hi, lo = _word_halves(chunk(c))
        p = hi * hpart(c) + lo * hpart(c + half)
        psum = p if psum is None else psum + p
    act = _gelu(_rowsum_bcast(psum, ones_bf16))
    gcol = _rowsum_bcast(jnp.where(eye, grow, 0.0), ones_bf16)
    coef = gcol * act
    outs_hi, outs_lo = [], []
    for c in range(half):
        hi, lo = _word_halves(chunk(c + half))
        outs_hi.append(jnp.sum(coef * hi, axis=0, keepdims=True))
        outs_lo.append(jnp.sum(coef * lo, axis=0, keepdims=True))
    return jnp.concatenate(outs_hi + outs_lo, axis=-1)


def _finish_block(x_ref, g2_ref, fg_ref, peer, out_ref):
    y = x_ref[...] + g2_ref[0] * peer[...]
    out_ref[...] = y * lax.rsqrt(jnp.mean(y * y, axis=-1, keepdims=True) + EPS) * fg_ref[...]


def _expert_kernel(idx_ref, gates_ref, h_ref, x_ref, g2_ref, fg_ref, tab_ref, out_ref,
                   *scratch, tok_block, n_slots):
    bufs = scratch[:n_slots]
    peer, sem = scratch[n_slots], scratch[n_slots + 1]
    d_model = h_ref.shape[-1]
    n_chunks = d_model // LANES
    rows_per_tok = PEER_SLOTS * n_chunks

    def issue(t, s):
        for r in range(PEER_SLOTS):
            row = pl.multiple_of(idx_ref[t, r] * n_chunks, n_chunks)
            pltpu.make_async_copy(tab_ref.at[pl.ds(row, n_chunks), :],
                                  bufs[s].at[pl.ds(r * n_chunks, n_chunks), :],
                                  sem.at[s]).start(priority=r % DMA_THREADS)

    def wait(s):
        pltpu.make_async_copy(tab_ref.at[pl.ds(0, rows_per_tok), :], bufs[s], sem.at[s]).wait()

    def compute(t, s):
        chunk = lambda c: bufs[s][pl.ds(c, PEER_SLOTS, stride=n_chunks), :]
        peer[pl.ds(t, 1), :] = _eval_experts(chunk, h_ref[pl.ds(t, 1), :], gates_ref[pl.ds(t, 1), :])

    def step(t, s, prefetch):
        wait(s)
        if prefetch:
            issue(t + n_slots - 1, (s - 1) % n_slots)
        compute(t, s)

    for t in range(n_slots - 1):
        issue(t, t)

    n_groups = tok_block // n_slots

    def group(g, carry):
        for s in range(n_slots):
            step(g * n_slots + s, s, True)
        return carry

    lax.fori_loop(0, n_groups - 1, group, 0)
    for s in range(n_slots):
        t = (n_groups - 1) * n_slots + s
        step(t, s, t + n_slots - 1 < tok_block)

    _finish_block(x_ref, g2_ref, fg_ref, peer, out_ref)


def _expert_cost(n_tok, d):
    pairs = n_tok * PEER_SLOTS
    return pl.CostEstimate(flops=4 * pairs * d, transcendentals=pairs,
                           bytes_accessed=4 * pairs * d + 12 * n_tok * d + 8 * pairs)


def _sc_tanh(y):
    return 1.0 - 2.0 / (jnp.exp(2.0 * y) + 1.0)


def _sc_peer_experts(table3, ids, gates, h3):
    n_tok, n_chunks, lanes = h3.shape
    info = plsc.get_sparse_core_info()
    sl = info.num_lanes
    n_workers = info.num_cores * info.num_subcores
    tok_per_worker = n_tok // n_workers
    assert tok_per_worker * n_workers == n_tok and tok_per_worker % 2 == 0
    rows = SC_GATHER_ROWS
    n_sub = PEER_SLOTS // rows
    assert n_sub % 2 == 0
    half = n_chunks // 2
    pieces = [(c, k * sl) for c in range(half) for k in range(lanes // sl)]
    mesh = plsc.VectorSubcoreMesh(core_axis_name="c", subcore_axis_name="s")

    buf = lambda dt: pltpu.VMEM((rows, n_chunks, lanes), dt)

    @functools.partial(
        pl.kernel, mesh=mesh,
        out_type=jax.ShapeDtypeStruct((n_tok, n_chunks, lanes), jnp.float32),
        scratch_types=[pltpu.VMEM((PEER_SLOTS,), jnp.int32), pltpu.VMEM((PEER_SLOTS,), jnp.int32),
                       pltpu.VMEM((PEER_SLOTS,), jnp.float32), pltpu.VMEM((PEER_SLOTS,), jnp.float32),
                       pltpu.VMEM((n_chunks, lanes), jnp.float32),
                       pltpu.VMEM((n_chunks, lanes), jnp.float32),
                       buf(jnp.uint32), buf(jnp.uint32), pltpu.VMEM((n_chunks, lanes), jnp.float32),
                       pltpu.SemaphoreType.DMA((2,)), pltpu.SemaphoreType.DMA((2,))],
        compiler_params=pltpu.CompilerParams(needs_layout_passes=False),
        cost_estimate=_expert_cost(n_tok, n_chunks * lanes),
        name="sc_peer_experts")
    def run(table_hbm, ids_hbm, gates_hbm, h_hbm, peer_hbm, ids0, ids1, g0, g1, h0, h1, rows0, rows1,
            out_v, gsem, isem):
        bufs = (rows0, rows1)
        ids_p, g_p, h_p = (ids0, ids1), (g0, g1), (h0, h1)
        t_first = (lax.axis_index("s") * info.num_cores + lax.axis_index("c")) * tok_per_worker

        def token_inputs(t, p):
            off = pl.multiple_of(t * PEER_SLOTS, PEER_SLOTS)
            return (pltpu.make_async_copy(ids_hbm.at[pl.ds(off, PEER_SLOTS)], ids_p[p], isem.at[p]),
                    pltpu.make_async_copy(gates_hbm.at[pl.ds(off, PEER_SLOTS)], g_p[p], isem.at[p]),
                    pltpu.make_async_copy(h_hbm.at[t], h_p[p], isem.at[p]))

        def gather(p, s):
            return pltpu.make_async_copy(table_hbm.at[ids_p[p].at[pl.ds(s * rows, rows)]],
                                         bufs[s % 2], gsem.at[s % 2])

        for cp in token_inputs(t_first, 0):
            cp.start()
        for cp in token_inputs(t_first, 0):
            cp.wait()
        gather(0, 0).start()

        def one_token(ti, p):
            t = t_first + ti
            g_v, h_v = g_p[p], h_p[p]
            has_next = ti + 1 < tok_per_worker

            @pl.when(has_next)
            def _():
                for cp in token_inputs(t + 1, 1 - p):
                    cp.start()

            for c, k in pieces:
                out_v[c, pl.ds(k, sl)] = jnp.zeros((sl,), jnp.float32)
                out_v[c + half, pl.ds(k, sl)] = jnp.zeros((sl,), jnp.float32)
            for s in range(n_sub):
                gather(p, s).wait()
                if s + 1 < n_sub:
                    gather(p, s + 1).start()
                else:
                    @pl.when(has_next)
                    def _():
                        for cp in token_inputs(t + 1, 1 - p):
                            cp.wait()
                        gather(1 - p, 0).start()
                rbuf = bufs[s % 2]

                @pl.loop(0, rows, step=SC_ROW_GROUP)
                def _(r0):
                    accs = [jnp.zeros((sl,), jnp.float32) for _ in range(SC_ROW_GROUP)]
                    for c, k in pieces:
                        h_hi, h_lo = h_v[c, pl.ds(k, sl)], h_v[c + half, pl.ds(k, sl)]
                        for j in range(SC_ROW_GROUP):
                            hi, lo = _word_halves(rbuf[r0 + j, c, pl.ds(k, sl)])
                            accs[j] = accs[j] + hi * h_hi + lo * h_lo
                    coefs = []
                    for j in range(SC_ROW_GROUP):
                        a = jnp.broadcast_to(jnp.sum(accs[j]), (sl,))
                        act = 0.5 * a * (1.0 + _sc_tanh(0.7978845608028654 * (a + 0.044715 * (a * a * a))))
                        gate = plsc.load_gather(g_v, [jnp.broadcast_to(s * rows + r0 + j, (sl,))])
                        coefs.append(gate * act)
                    for b0 in range(0, len(pieces), SC_STORE_BATCH):
                        tots = []
                        for c, k in pieces[b0:b0 + SC_STORE_BATCH]:
                            t_hi = t_lo = None
                            for j in range(SC_ROW_GROUP):
                                hi, lo = _word_halves(rbuf[r0 + j, c + half, pl.ds(k, sl)])
                                t_hi = coefs[j] * hi if t_hi is None else t_hi + coefs[j] * hi
                                t_lo = coefs[j] * lo if t_lo is None else t_lo + coefs[j] * lo
                            tots.append((t_hi, t_lo))
                        for (c, k), (t_hi, t_lo) in zip(pieces[b0:b0 + SC_STORE_BATCH], tots):
                            plsc.addupdate(out_v.at[c, pl.ds(k, sl)], t_hi)
                            plsc.addupdate(out_v.at[c + half, pl.ds(k, sl)], t_lo)

            pltpu.sync_copy(out_v, peer_hbm.at[t])

        @pl.loop(0, tok_per_worker, step=2)
        def _(ti):
            one_token(ti, 0)
            one_token(ti + 1, 1)

    return run(table3, ids, gates, h3)


def _finish_kernel(peer_ref, x_ref, g2_ref, fg_ref, prev_ref, out_ref):
    del prev_ref
    _finish_block(x_ref, g2_ref, fg_ref, peer_ref, out_ref)


def _finish_tokens(peer, x1, g2, fg, prev, seq_len):
    d = x1.shape[1]
    tm = FINISH_TOK_BLOCK
    return pl.pallas_call(
        _finish_kernel,
        grid=(peer.shape[0] // tm,),
        in_specs=[pl.BlockSpec((tm, d), lambda i: (i, 0)),
                  pl.BlockSpec((tm, d), lambda i: (i, 0)),
                  pl.BlockSpec((1, 1, d), lambda i: (i * tm // seq_len, 0, 0)),
                  pl.BlockSpec((1, d), lambda i: (0, 0)),
                  pl.BlockSpec(memory_space=pl.ANY)],
        out_specs=pl.BlockSpec((tm, d), lambda i: (i, 0)),
        out_shape=jax.ShapeDtypeStruct(prev.shape, prev.dtype),
        input_output_aliases={4: 0},
        compiler_params=pltpu.CompilerParams(dimension_semantics=("arbitrary",)),
        name="finish_tokens",
    )(peer, x1, g2, fg, prev)


def _pack_expert_table(expert_u, expert_v):
    n_experts, d = expert_u.shape
    bits = lambda a: lax.bitcast_convert_type(a.astype(jnp.bfloat16), jnp.uint16).astype(jnp.uint32)
    pair = lambda b: (b[:, :d // 2] << 16) | b[:, d // 2:]
    words = jnp.concatenate([pair(bits(expert_u)), pair(bits(expert_v))], axis=1)
    return words.reshape(n_experts, d // LANES, LANES)


def _tc_peer_experts(idx, gates, h2, x1, g2, final_g, table, seq_len, n_out, after=None):
    m, d = h2.shape
    n_chunks = d // LANES
    tb = EXPERT_TOK_BLOCK
    assert tb % EXPERT_SLOTS == 0 and seq_len % tb == 0 and (n_out - m) % tb == 0
    first = (n_out - m) // tb
    extra = [] if after is None else [after]
    kern = functools.partial(_expert_kernel, tok_block=tb, n_slots=EXPERT_SLOTS)
    return pl.pallas_call(
        kern if after is None else _ordered_after(kern, 7),
        grid=(m // tb,),
        in_specs=[
            pl.BlockSpec((tb, PEER_SLOTS), lambda i: (i, 0), memory_space=pltpu.SMEM),
            pl.BlockSpec((tb, PEER_SLOTS), lambda i: (i, 0)),
            pl.BlockSpec((tb, d), lambda i: (i, 0)),
            pl.BlockSpec((tb, d), lambda i: (i, 0)),
            pl.BlockSpec((1, 1, d), lambda i: (i * tb // seq_len, 0, 0)),
            pl.BlockSpec((1, d), lambda i: (0, 0)),
            pl.BlockSpec(memory_space=pl.ANY),
        ] + [pl.BlockSpec(memory_space=pl.ANY) for _ in extra],
        out_specs=pl.BlockSpec((tb, d), lambda i: (i + first, 0)),
        out_shape=jax.ShapeDtypeStruct((n_out, d), jnp.float32),
        scratch_shapes=(
            [pltpu.VMEM((PEER_SLOTS * n_chunks, LANES), jnp.uint32) for _ in range(EXPERT_SLOTS)]
            + [pltpu.VMEM((tb, d), jnp.float32), pltpu.SemaphoreType.DMA((EXPERT_SLOTS,))]),
        compiler_params=pltpu.CompilerParams(dimension_semantics=("arbitrary",)),
        cost_estimate=_expert_cost(m, d),
        name="peer_experts",
    )(idx, gates, h2, x1, g2, final_g.reshape(1, d), table.reshape(-1, LANES), *extra)


def _mix_and_route(x, mod, norm1_g, w_in, b_forget, ln_v_g, w_spatial, b_spatial, w_branch_a,
                   w_branch_b, w_out, norm2_g, w_query, sub_keys, after_routing=None):
    Bp, S, D = x.shape
    m = Bp * S
    bf16 = jnp.bfloat16
    qkv, cum, sga, gb = _input_projection(x, mod, norm1_g, w_in, b_forget, ln_v_g,
                                          w_spatial, b_spatial, w_branch_b)
    heads = lambda a: a.reshape(Bp, S, ATT_HEADS, ATT_HEAD_DIM).transpose(0, 2, 1, 3)
    q = heads(qkv[:, 0:ATT_WIDTH])
    k = heads(qkv[:, ATT_WIDTH:2 * ATT_WIDTH])
    v = heads(qkv[:, 2 * ATT_WIDTH:])
    cum_h = cum[:, 0:ATT_HEADS].reshape(Bp, S, ATT_HEADS).transpose(0, 2, 1)
    y_a = _fox_attention(q, k, v, cum_h).transpose(0, 2, 1, 3).reshape(m, ATT_WIDTH)
    x1, h2, idx_t, gates_t = _post_attention(
        y_a, sga, gb, x.reshape(m, D), mod, norm2_g.reshape(1, D),
        w_branch_a.astype(bf16), w_out.astype(bf16), w_query.astype(bf16), sub_keys, S,
        after=after_routing)
    return x1, h2, idx_t.T, gates_t.T


def kernel(x, c, w_mod, b_mod, norm1_g, w_in, b_forget, ln_v_g, w_spatial, b_spatial, w_branch_a, w_branch_b, w_out, norm2_g, w_query, sub_keys, expert_u, expert_v, final_g):
    B, S, D = x.shape
    n = B * S
    assert w_mod.shape[0] == 1, "the final RMSNorm is fused into the single layer's expert kernels"
    l = 0
    mod = _modulation(c, w_mod[l], b_mod[l]).reshape(B, 6, D)
    table = _pack_expert_table(expert_u[l], expert_v[l])
    n_chunks = D // LANES
    weights = (norm1_g[l], w_in[l], b_forget[l], ln_v_g[l], w_spatial[l], b_spatial[l],
               w_branch_a[l], w_branch_b[l], w_out[l], norm2_g[l], w_query[l], sub_keys[l])

    b_sc = min((B * SC_SHARE_PERCENT + 50) // 100, B - 1)
    peers = []
    if b_sc > 0:
        m_a = b_sc * S
        x1_a, h2_a, idx_a, gates_a = _mix_and_route(x[:b_sc], mod[:b_sc], *weights)
        h3_a = h2_a.reshape(m_a, n_chunks, LANES)
        cuts = [0, m_a * SC_EARLY_SHARES[0] // 100, m_a * (SC_EARLY_SHARES[0] + SC_EARLY_SHARES[1]) // 100, m_a]
        cuts = [c // SC_CUT_ALIGN * SC_CUT_ALIGN for c in cuts]
        for lo, hi in zip(cuts[:-1], cuts[1:]):
            peers.append(_sc_peer_experts(table, idx_a[lo:hi].reshape(-1), gates_a[lo:hi].reshape(-1),
                                          h3_a[lo:hi]))
    after = peers[:2] if peers else [None, None]
    x1_b, h2_b, idx_b, gates_b = _mix_and_route(x[b_sc:], mod[b_sc:], *weights, after_routing=after[0])
    out = _tc_peer_experts(idx_b, gates_b, h2_b, x1_b, mod[b_sc:, 5:6, :], final_g, table, S, n,
                           after=after[1])
    if b_sc > 0:
        out = _finish_tokens(jnp.concatenate(peers, axis=0).reshape(m_a, D), x1_a, mod[:b_sc, 5:6, :],
                             final_g.reshape(1, D), out, S)
    return out.reshape(B, S, D)
```

```python
import functools

import jax
import jax.numpy as jnp
from jax import lax
from jax.experimental import pallas as pl
from jax.experimental.pallas import tpu as pltpu
from jax.experimental.pallas import tpu_sc as plsc

D_MODEL = 1024
ATT_HEADS = 8
ATT_HEAD_DIM = 64
ATT_WIDTH = ATT_HEADS * ATT_HEAD_DIM
GM_GROUPS = 4
GM_GROUP_DIM = 128
GM_WIDTH = GM_GROUPS * GM_GROUP_DIM
GM_CHUNK = 128
PEER_HEADS = 8
PEER_KEY_DIM = 256
PEER_HALF = PEER_KEY_DIM // 2
N_KEYS = 128
PEER_TOPK = 16
PEER_SLOTS = PEER_HEADS * PEER_TOPK
SPLIT_POINTS = (ATT_WIDTH, 2 * ATT_WIDTH, 3 * ATT_WIDTH, 3 * ATT_WIDTH + ATT_HEADS,
                3 * ATT_WIDTH + ATT_HEADS + 2 * GM_WIDTH,
                3 * ATT_WIDTH + ATT_HEADS + 2 * GM_WIDTH + D_MODEL)
EPS = 1e-6

LANES = 128
EXPERT_TOK_BLOCK = 128
EXPERT_SLOTS = 4
DMA_THREADS = 2
FINISH_TOK_BLOCK = 512
SC_GATHER_ROWS = 32
SC_ROW_GROUP = 4
SC_STORE_BATCH = 8
SC_SHARE_PERCENT = 58
SC_SHARE_ALIGN = 512


def _gelu(x):
    return 0.5 * x * (1.0 + jnp.tanh(0.7978845608028654 * (x + 0.044715 * (x * x * x))))


def _mod_kernel(c_ref, w_ref, b_ref, o_ref):
    c = c_ref[...]
    sc = c * jax.nn.sigmoid(c)
    o_ref[...] = jnp.dot(sc, w_ref[...], precision=lax.Precision.HIGHEST,
                         preferred_element_type=jnp.float32) + b_ref[...]


def _modulation(c, w_mod, b_mod):
    b, d = c.shape
    cols = w_mod.shape[1]
    return pl.pallas_call(
        _mod_kernel,
        grid=(cols // d,),
        in_specs=[pl.BlockSpec((b, d), lambda j: (0, 0)),
                  pl.BlockSpec((d, d), lambda j: (0, j)),
                  pl.BlockSpec((1, d), lambda j: (0, j))],
        out_specs=pl.BlockSpec((b, d), lambda j: (0, j)),
        out_shape=jax.ShapeDtypeStruct((b, cols), jnp.float32),
        name="modulation",
    )(c, w_mod, b_mod.reshape(1, cols))


INPROJ_TOK_BLOCK = 256


def _inproj_kernel(x_ref, mod_ref, n1g_ref, wqkv_ref, wf_ref, bf_ref, wz_ref, wg_ref, lng_ref,
                   wsp_ref, bsp_ref, wb_ref, qkv_ref, cum_ref, cumt_ref, sga_ref, gb_ref, carry):
    f32, bf16 = jnp.float32, jnp.bfloat16
    tm, d = x_ref.shape[1], x_ref.shape[2]
    x = x_ref[0]
    sh1 = mod_ref[0, 0:1, :]
    sc1 = mod_ref[0, 1:2, :]
    h = x * lax.rsqrt(jnp.mean(x * x, axis=-1, keepdims=True) + EPS) * n1g_ref[...]
    hb = (h * (1.0 + sc1) + sh1).astype(bf16)

    qkv = jnp.dot(hb, wqkv_ref[...], preferred_element_type=f32)
    qkv_ref[:, 0:ATT_WIDTH] = (qkv[:, 0:ATT_WIDTH] * (ATT_HEAD_DIM ** -0.5)).astype(bf16)
    qkv_ref[:, ATT_WIDTH:] = qkv[:, ATT_WIDTH:].astype(bf16)

    f = jnp.dot(hb, wf_ref[...], preferred_element_type=f32) + bf_ref[...]
    logf = jnp.minimum(f, 0.0) - jnp.log1p(jnp.exp(-jnp.abs(f)))

    @pl.when(pl.program_id(1) == 0)
    def _():
        carry[...] = jnp.zeros_like(carry)

    tri = (lax.broadcasted_iota(jnp.int32, (tm, tm), 0)
           >= lax.broadcasted_iota(jnp.int32, (tm, tm), 1)).astype(f32)
    cum = jnp.dot(tri, logf, precision=lax.Precision.HIGHEST, preferred_element_type=f32) + carry[...]
    cum_ref[...] = cum
    cumt_ref[0, 0] = jnp.transpose(cum)[0:ATT_HEADS, :]
    carry[...] = cum[tm - 1:tm, :]

    gz = _gelu(jnp.dot(hb, wz_ref[...], preferred_element_type=f32))
    u = gz[:, 0:GM_WIDTH]
    v = gz[:, GM_WIDTH:]
    mu = jnp.mean(v, axis=-1, keepdims=True)
    var = jnp.mean(jnp.square(v - mu), axis=-1, keepdims=True)
    vn = ((v - mu) * lax.rsqrt(var + EPS) * lng_ref[...]).astype(bf16)
    tril = (lax.broadcasted_iota(jnp.int32, (GM_CHUNK, GM_CHUNK), 0)
            >= lax.broadcasted_iota(jnp.int32, (GM_CHUNK, GM_CHUNK), 1))
    w_sp = [jnp.where(tril, wsp_ref[g], 0.0).astype(bf16) for g in range(GM_GROUPS)]
    rows = []
    for ck in range(tm // GM_CHUNK):
        r0 = ck * GM_CHUNK
        cols = []
        for g in range(GM_GROUPS):
            c0 = g * GM_GROUP_DIM
            mixed = jnp.dot(w_sp[g], vn[r0:r0 + GM_CHUNK, c0:c0 + GM_GROUP_DIM],
                            preferred_element_type=f32) + bsp_ref[g]
            cols.append(u[r0:r0 + GM_CHUNK, c0:c0 + GM_GROUP_DIM] * mixed)
        rows.append(jnp.concatenate(cols, axis=1))
    yb = jnp.concatenate(rows, axis=0).astype(bf16)
    ybp = jnp.dot(yb, wb_ref[...], preferred_element_type=f32)

    sg = jax.nn.sigmoid(jnp.dot(hb, wg_ref[...], preferred_element_type=f32))
    sga_ref[...] = sg[:, 0:d].astype(bf16)
    gb_ref[...] = (sg[:, d:] * ybp).astype(bf16)


def _input_projection(x, mod, norm1_g, w_in, b_forget, ln_v_g, w_spatial, b_spatial, w_branch_b):
    B, S, d = x.shape
    n = B * S
    tm = INPROJ_TOK_BLOCK
    bf16 = jnp.bfloat16
    p0, p1, p2, p3, p4, p5 = SPLIT_POINTS
    w_qkv = w_in[:, 0:p2].astype(bf16)
    w_f = jnp.pad(w_in[:, p2:p3], ((0, 0), (0, LANES - ATT_HEADS))).astype(bf16)
    b_f = jnp.pad(b_forget, (0, LANES - ATT_HEADS)).reshape(1, LANES)
    w_z = w_in[:, p3:p4].astype(bf16)
    w_g = w_in[:, p4:].astype(bf16)
    nt = S // tm
    tok = lambda w: pl.BlockSpec((tm, w), lambda b, i: (b * nt + i, 0))
    full = lambda a: pl.BlockSpec(a.shape, lambda b, i: (0,) * a.ndim)
    args = (x, mod, norm1_g.reshape(1, d), w_qkv, w_f, b_f, w_z, w_g, ln_v_g.reshape(1, GM_WIDTH),
            w_spatial, b_spatial.reshape(GM_GROUPS, GM_CHUNK, 1), w_branch_b.astype(bf16))
    return pl.pallas_call(
        _inproj_kernel,
        grid=(B, nt),
        in_specs=[pl.BlockSpec((1, tm, d), lambda b, i: (b, i, 0)),
                  pl.BlockSpec((1, 6, d), lambda b, i: (b, 0, 0))] + [full(a) for a in args[2:]],
        out_specs=[tok(3 * ATT_WIDTH), tok(LANES),
                   pl.BlockSpec((1, 1, ATT_HEADS, tm), lambda b, i: (b, i, 0, 0)), tok(d), tok(d)],
        out_shape=[jax.ShapeDtypeStruct((n, 3 * ATT_WIDTH), bf16),
                   jax.ShapeDtypeStruct((n, LANES), jnp.float32),
                   jax.ShapeDtypeStruct((B, nt, ATT_HEADS, tm), jnp.float32),
                   jax.ShapeDtypeStruct((n, d), bf16),
                   jax.ShapeDtypeStruct((n, d), bf16)],
        scratch_shapes=[pltpu.VMEM((1, LANES), jnp.float32)],
        compiler_params=pltpu.CompilerParams(dimension_semantics=("arbitrary", "arbitrary"),
                                             vmem_limit_bytes=56 * 1024 * 1024),
        name="input_projection",
    )(*args)


ATT_BLOCK = 512


def _fox_kernel(q_ref, k_ref, v_ref, cum_ref, cumt_ref, o_ref, *, blk, ratio):
    f32 = jnp.float32
    hp = pl.program_id(1)
    i = pl.program_id(2)
    dh = ATT_HEAD_DIM
    q2 = q_ref[...]
    first = lax.broadcasted_iota(jnp.int32, (1, 2 * dh), 1) < dh
    qs = (jnp.where(first, q2, jnp.zeros_like(q2)), jnp.where(first, jnp.zeros_like(q2), q2))
    cum_blk = cum_ref[...]
    lane = lax.broadcasted_iota(jnp.int32, cum_blk.shape, 1)
    cqs = [jnp.sum(jnp.where(lane == 2 * hp + e, cum_blk, 0.0), axis=1, keepdims=True)
           for e in range(2)]

    def block(j, carry, masked):
        off = pl.multiple_of(j * blk, blk)
        k2 = k_ref[0, pl.ds(off, blk), :]
        v2 = v_ref[0, pl.ds(off, blk), :]
        out = []
        for e in range(2):
            m, l, acc = carry[e]
            ck = jnp.concatenate([cumt_ref[0, j * ratio + a, pl.ds(2 * hp + e, 1), :]
                                  for a in range(ratio)], axis=1)
            s = lax.dot_general(qs[e], k2, (((1,), (1,)), ((), ())), preferred_element_type=f32)
            s = s + (cqs[e] - ck)
            if masked:
                causal = (lax.broadcasted_iota(jnp.int32, (blk, blk), 0)
                          >= lax.broadcasted_iota(jnp.int32, (blk, blk), 1))
                s = jnp.where(causal, s, -jnp.inf)
            m_new = jnp.maximum(m, jnp.max(s, axis=1, keepdims=True))
            alpha = jnp.exp(m - m_new)
            p = jnp.exp(s - m_new)
            l = alpha * l + jnp.sum(p, axis=1, keepdims=True)
            acc = alpha * acc + jnp.dot(p.astype(v2.dtype), v2, preferred_element_type=f32)
            out.append((m_new, l, acc))
        return tuple(out)

    one = (jnp.full((blk, 1), -1e30, f32), jnp.zeros((blk, 1), f32), jnp.zeros((blk, 2 * dh), f32))
    carry = lax.fori_loop(0, i, lambda j, c: block(j, c, False), (one, one))
    (_, l0, acc0), (_, l1, acc1) = block(i, carry, True)
    o_ref[...] = jnp.where(first, acc0 / l0, acc1 / l1).astype(o_ref.dtype)


def _fox_attention(qkv, cum, cumt, batch, seq_len):
    n = qkv.shape[0]
    tm = cumt.shape[-1]
    blk = max(min(ATT_BLOCK, seq_len), tm)
    nb = seq_len // blk
    pair = 2 * ATT_HEAD_DIM
    n_pairs = ATT_WIDTH // pair
    qkv3 = qkv.reshape(batch, seq_len, 3 * ATT_WIDTH)
    return pl.pallas_call(
        functools.partial(_fox_kernel, blk=blk, ratio=blk // tm),
        grid=(batch, n_pairs, nb),
        in_specs=[pl.BlockSpec((blk, pair), lambda b, hp, i: (b * nb + i, hp)),
                  pl.BlockSpec((1, seq_len, pair), lambda b, hp, i: (b, 0, n_pairs + hp)),
                  pl.BlockSpec((1, seq_len, pair), lambda b, hp, i: (b, 0, 2 * n_pairs + hp)),
                  pl.BlockSpec((blk, LANES), lambda b, hp, i: (b * nb + i, 0)),
                  pl.BlockSpec((1,) + cumt.shape[1:], lambda b, hp, i: (b, 0, 0, 0))],
        out_specs=pl.BlockSpec((blk, pair), lambda b, hp, i: (b * nb + i, hp)),
        out_shape=jax.ShapeDtypeStruct((n, ATT_WIDTH), jnp.bfloat16),
        compiler_params=pltpu.CompilerParams(
            dimension_semantics=("arbitrary", "arbitrary", "arbitrary")),
        name="fox_attention",
    )(qkv, qkv3, qkv3, cum, cumt)


def _topk_rows(s, k):
    rows, t = s.shape
    iota = lax.broadcasted_iota(jnp.int32, (rows, t), 0)
    slot = lax.broadcasted_iota(jnp.int32, (k, t), 0)
    vals = jnp.zeros((k, t), jnp.float32)
    ids = jnp.zeros((k, t), jnp.int32)
    for j in range(k):
        m = jnp.max(s, axis=0, keepdims=True)
        am = jnp.min(jnp.where(s == m, iota, rows), axis=0, keepdims=True)
        vals = jnp.where(slot == j, m, vals)
        ids = jnp.where(slot == j, am, ids)
        s = jnp.where(iota == am, -jnp.inf, s)
    return vals, ids


def _select_rows(table, pos):
    out = jnp.zeros(pos.shape, table.dtype)
    for r in range(table.shape[0]):
        out = jnp.where(pos == r, table[r:r + 1, :], out)
    return out


def _post_kernel(ya_ref, sga_ref, gb_ref, x_ref, mod_ref, n2g_ref, wa_ref, wo_ref, wq_ref, keys_ref,
                 x1_ref, h2_ref, idx_ref, gates_ref, q_scr):
    f32 = jnp.float32
    a = jnp.dot(ya_ref[...], wa_ref[...], preferred_element_type=f32)
    merged = sga_ref[...].astype(f32) * a + gb_ref[...].astype(f32)
    o = jnp.dot(merged.astype(jnp.bfloat16), wo_ref[...], preferred_element_type=f32)
    g1 = mod_ref[0, 2:3, :]
    sh2 = mod_ref[0, 3:4, :]
    sc2 = mod_ref[0, 4:5, :]
    x1 = x_ref[...] + g1 * o
    x1_ref[...] = x1
    h2 = x1 * lax.rsqrt(jnp.mean(x1 * x1, axis=-1, keepdims=True) + EPS) * n2g_ref[...]
    h2 = h2 * (1.0 + sc2) + sh2
    h2_ref[...] = h2
    qp = jnp.dot(h2.astype(jnp.bfloat16), wq_ref[...], preferred_element_type=f32)
    for j in range(2 * PEER_HEADS):
        q_scr[j] = qp[:, j * PEER_HALF:(j + 1) * PEER_HALF]

    def head(h, carry):
        tops = []
        for p in range(2):
            q = q_scr[2 * h + p]
            keys = keys_ref[2 * h + p]
            sc = lax.dot_general(keys, q, (((1,), (1,)), ((), ())),
                                 precision=lax.Precision.HIGHEST,
                                 preferred_element_type=f32)
            tops.append(_topk_rows(sc, PEER_TOPK))
        (s1, i1), (s2, i2) = tops
        half = PEER_TOPK // 2
        blocks = [s1[0:1, :] + s2]
        blocks += [s1[a:a + 1, :] + s2[0:half, :] for a in range(1, half)]
        blocks += [s1[half:, :] + s2[0:1, :]]
        vals, pos = _topk_rows(jnp.concatenate(blocks, axis=0), PEER_TOPK)
        mid = pos - PEER_TOPK
        tail0 = PEER_TOPK + (half - 1) * half
        ra = jnp.where(pos < PEER_TOPK, 0,
                       jnp.where(pos < tail0, 1 + (mid >> (half.bit_length() - 1)), pos - tail0 + half))
        rb = jnp.where(pos < PEER_TOPK, pos, jnp.where(pos < tail0, mid & (half - 1), 0))
        eid = _select_rows(i1, ra) * N_KEYS + _select_rows(i2, rb)
        e = jnp.exp(vals - vals[0:1, :])
        g = e / jnp.sum(e, axis=0, keepdims=True)
        row = pl.multiple_of(h * PEER_TOPK, PEER_TOPK)
        idx_ref[pl.ds(row, PEER_TOPK), :] = eid
        gates_ref[pl.ds(row, PEER_TOPK), :] = g
        return carry

    lax.fori_loop(0, PEER_HEADS, head, 0)


POST_TOK_BLOCK = 256


def _post_attention(ya, sga, gb, x, mod, norm2_g, w_a, w_out, w_query, sub_keys, seq_len):
    n, d = x.shape
    tm = POST_TOK_BLOCK
    blocks_per_seq = seq_len // tm
    aw = ya.shape[1]
    keys = sub_keys.reshape(2 * PEER_HEADS, N_KEYS, PEER_HALF)
    tok = lambda w: pl.BlockSpec((tm, w), lambda i: (i, 0))
    full = lambda a: pl.BlockSpec(a.shape, lambda i: (0,) * a.ndim)
    return pl.pallas_call(
        _post_kernel,
        grid=(n // tm,),
        in_specs=[tok(aw), tok(d), tok(d), tok(d),
                  pl.BlockSpec((1, 6, d), lambda i: (i // blocks_per_seq, 0, 0)),
                  full(norm2_g), full(w_a), full(w_out), full(w_query), full(keys)],
        out_specs=[tok(d), tok(d),
                   pl.BlockSpec((PEER_SLOTS, tm), lambda i: (0, i)),
                   pl.BlockSpec((PEER_SLOTS, tm), lambda i: (0, i))],
        out_shape=[jax.ShapeDtypeStruct((n, d), jnp.float32),
                   jax.ShapeDtypeStruct((n, d), jnp.float32),
                   jax.ShapeDtypeStruct((PEER_SLOTS, n), jnp.int32),
                   jax.ShapeDtypeStruct((PEER_SLOTS, n), jnp.float32)],
        scratch_shapes=[pltpu.VMEM((2 * PEER_HEADS, tm, PEER_HALF), jnp.float32)],
        compiler_params=pltpu.CompilerParams(dimension_semantics=("arbitrary",),
                                             vmem_limit_bytes=48 * 1024 * 1024),
        name="post_attention",
    )(ya, sga, gb, x, mod, norm2_g, w_a, w_out, w_query, keys)


def _rowsum_bcast(p, ones_bf16):
    hi = p.astype(jnp.bfloat16)
    lo = (p - hi.astype(jnp.float32)).astype(jnp.bfloat16)
    return (jnp.dot(hi, ones_bf16, preferred_element_type=jnp.float32)
            + jnp.dot(lo, ones_bf16, preferred_element_type=jnp.float32))


def _word_halves(w):
    return (lax.bitcast_convert_type(w & jnp.uint32(0xFFFF0000), jnp.float32),
            lax.bitcast_convert_type(w << 16, jnp.float32))


def _eval_experts(chunk, hrow, grow):
    half = hrow.shape[1] // LANES // 2
    ones_bf16 = jnp.ones((LANES, LANES), jnp.bfloat16)
    eye = (lax.broadcasted_iota(jnp.int32, (PEER_SLOTS, LANES), 0)
           == lax.broadcasted_iota(jnp.int32, (PEER_SLOTS, LANES), 1))
    hpart = lambda c: hrow[:, c * LANES:(c + 1) * LANES]
    psum = None
    for c in range(half):
        hi, lo = _word_halves(chunk(c))
        p = hi * hpart(c) + lo * hpart(c + half)
        psum = p if psum is None else psum + p
    act = _gelu(_rowsum_bcast(psum, ones_bf16))
    gcol = _rowsum_bcast(jnp.where(eye, grow, 0.0), ones_bf16)
    coef = gcol * act
    outs_hi, outs_lo = [], []
    for c in range(half):
        hi, lo = _word_halves(chunk(c + half))
        outs_hi.append(jnp.sum(coef * hi, axis=0, keepdims=True))
        outs_lo.append(jnp.sum(coef * lo, axis=0, keepdims=True))
    return jnp.concatenate(outs_hi + outs_lo, axis=-1)


def _finish_block(x_ref, g2_ref, fg_ref, peer, out_ref):
    y = x_ref[...] + g2_ref[0] * peer[...]
    out_ref[...] = y * lax.rsqrt(jnp.mean(y * y, axis=-1, keepdims=True) + EPS) * fg_ref[...]


def _expert_kernel(idx_ref, gates_ref, h_ref, x_ref, g2_ref, fg_ref, tab_ref, out_ref,
                   *scratch, tok_block, n_slots):
    bufs = scratch[:n_slots]
    peer, sem = scratch[n_slots], scratch[n_slots + 1]
    d_model = h_ref.shape[-1]
    n_chunks = d_model // LANES
    rows_per_tok = PEER_SLOTS * n_chunks

    def issue(t, s):
        for r in range(PEER_SLOTS):
            row = pl.multiple_of(idx_ref[t, r] * n_chunks, n_chunks)
            pltpu.make_async_copy(tab_ref.at[pl.ds(row, n_chunks), :],
                                  bufs[s].at[pl.ds(r * n_chunks, n_chunks), :],
                                  sem.at[s]).start(priority=r % DMA_THREADS)

    def wait(s):
        pltpu.make_async_copy(tab_ref.at[pl.ds(0, rows_per_tok), :], bufs[s], sem.at[s]).wait()

    def compute(t, s):
        chunk = lambda c: bufs[s][pl.ds(c, PEER_SLOTS, stride=n_chunks), :]
        peer[pl.ds(t, 1), :] = _eval_experts(chunk, h_ref[pl.ds(t, 1), :], gates_ref[pl.ds(t, 1), :])

    def step(t, s, prefetch):
        wait(s)
        if prefetch:
            issue(t + n_slots - 1, (s - 1) % n_slots)
        compute(t, s)

    for t in range(n_slots - 1):
        issue(t, t)

    n_groups = tok_block // n_slots

    def group(g, carry):
        for s in range(n_slots):
            step(g * n_slots + s, s, True)
        return carry

    lax.fori_loop(0, n_groups - 1, group, 0)
    for s in range(n_slots):
        t = (n_groups - 1) * n_slots + s
        step(t, s, t + n_slots - 1 < tok_block)

    _finish_block(x_ref, g2_ref, fg_ref, peer, out_ref)


def _expert_cost(n_tok, d):
    pairs = n_tok * PEER_SLOTS
    return pl.CostEstimate(flops=4 * pairs * d, transcendentals=pairs,
                           bytes_accessed=4 * pairs * d + 12 * n_tok * d + 8 * pairs)


def _sc_tanh(y):
    return 1.0 - 2.0 / (jnp.exp(2.0 * y) + 1.0)


def _sc_peer_experts(table3, ids, gates, h3):
    n_tok, n_chunks, lanes = h3.shape
    info = plsc.get_sparse_core_info()
    sl = info.num_lanes
    n_workers = info.num_cores * info.num_subcores
    tok_per_worker = n_tok // n_workers
    assert tok_per_worker * n_workers == n_tok and tok_per_worker % 2 == 0
    rows = SC_GATHER_ROWS
    n_sub = PEER_SLOTS // rows
    assert n_sub % 2 == 0
    half = n_chunks // 2
    pieces = [(c, k * sl) for c in range(half) for k in range(lanes // sl)]
    mesh = plsc.VectorSubcoreMesh(core_axis_name="c", subcore_axis_name="s")
    buf = lambda dt: pltpu.VMEM((rows, n_chunks, lanes), dt)

    @functools.partial(
        pl.kernel, mesh=mesh,
        out_type=jax.ShapeDtypeStruct((n_tok, n_chunks, lanes), jnp.float32),
        scratch_types=[pltpu.VMEM((PEER_SLOTS,), jnp.int32), pltpu.VMEM((PEER_SLOTS,), jnp.int32),
                       pltpu.VMEM((PEER_SLOTS,), jnp.float32), pltpu.VMEM((PEER_SLOTS,), jnp.float32),
                       pltpu.VMEM((n_chunks, lanes), jnp.float32),
                       pltpu.VMEM((n_chunks, lanes), jnp.float32),
                       buf(jnp.uint32), buf(jnp.uint32), pltpu.VMEM((n_chunks, lanes), jnp.float32),
                       pltpu.SemaphoreType.DMA((2,)), pltpu.SemaphoreType.DMA((2,))],
        compiler_params=pltpu.CompilerParams(needs_layout_passes=False),
        cost_estimate=_expert_cost(n_tok, n_chunks * lanes),
        name="sc_peer_experts")
    def run(table_hbm, ids_hbm, gates_hbm, h_hbm, peer_hbm, ids0, ids1, g0, g1, h0, h1, rows0, rows1,
            out_v, gsem, isem):
        bufs = (rows0, rows1)
        ids_p, g_p, h_p = (ids0, ids1), (g0, g1), (h0, h1)
        t_first = (lax.axis_index("s") * info.num_cores + lax.axis_index("c")) * tok_per_worker

        def token_inputs(t, p):
            off = pl.multiple_of(t * PEER_SLOTS, PEER_SLOTS)
            return (pltpu.make_async_copy(ids_hbm.at[pl.ds(off, PEER_SLOTS)], ids_p[p], isem.at[p]),
                    pltpu.make_async_copy(gates_hbm.at[pl.ds(off, PEER_SLOTS)], g_p[p], isem.at[p]),
                    pltpu.make_async_copy(h_hbm.at[t], h_p[p], isem.at[p]))

        def gather(p, s):
            return pltpu.make_async_copy(table_hbm.at[ids_p[p].at[pl.ds(s * rows, rows)]],
                                         bufs[s % 2], gsem.at[s % 2])

        for cp in token_inputs(t_first, 0):
            cp.start()
        for cp in token_inputs(t_first, 0):
            cp.wait()
        gather(0, 0).start()

        def one_token(ti, p):
            t = t_first + ti
            g_v, h_v = g_p[p], h_p[p]
            has_next = ti + 1 < tok_per_worker

            @pl.when(has_next)
            def _():
                for cp in token_inputs(t + 1, 1 - p):
                    cp.start()

            for c, k in pieces:
                out_v[c, pl.ds(k, sl)] = jnp.zeros((sl,), jnp.float32)
                out_v[c + half, pl.ds(k, sl)] = jnp.zeros((sl,), jnp.float32)
            for s in range(n_sub):
                gather(p, s).wait()
                if s + 1 < n_sub:
                    gather(p, s + 1).start()
                else:
                    @pl.when(has_next)
                    def _():
                        for cp in token_inputs(t + 1, 1 - p):
                            cp.wait()
                        gather(1 - p, 0).start()
                rbuf = bufs[s % 2]

                @pl.loop(0, rows, step=SC_ROW_GROUP)
                def _(r0):
                    accs = [jnp.zeros((sl,), jnp.float32) for _ in range(SC_ROW_GROUP)]
                    for c, k in pieces:
                        h_hi, h_lo = h_v[c, pl.ds(k, sl)], h_v[c + half, pl.ds(k, sl)]
                        for j in range(SC_ROW_GROUP):
                            hi, lo = _word_halves(rbuf[r0 + j, c, pl.ds(k, sl)])
                            accs[j] = accs[j] + hi * h_hi + lo * h_lo
                    coefs = []
                    for j in range(SC_ROW_GROUP):
                        a = jnp.broadcast_to(jnp.sum(accs[j]), (sl,))
                        act = 0.5 * a * (1.0 + _sc_tanh(0.7978845608028654 * (a + 0.044715 * (a * a * a))))
                        gate = plsc.load_gather(g_v, [jnp.broadcast_to(s * rows + r0 + j, (sl,))])
                        coefs.append(gate * act)
                    for b0 in range(0, len(pieces), SC_STORE_BATCH):
                        tots = []
                        for c, k in pieces[b0:b0 + SC_STORE_BATCH]:
                            t_hi = t_lo = None
                            for j in range(SC_ROW_GROUP):
                                hi, lo = _word_halves(rbuf[r0 + j, c + half, pl.ds(k, sl)])
                                t_hi = coefs[j] * hi if t_hi is None else t_hi + coefs[j] * hi
                                t_lo = coefs[j] * lo if t_lo is None else t_lo + coefs[j] * lo
                            tots.append((t_hi, t_lo))
                        for (c, k), (t_hi, t_lo) in zip(pieces[b0:b0 + SC_STORE_BATCH], tots):
                            plsc.addupdate(out_v.at[c, pl.ds(k, sl)], t_hi)
                            plsc.addupdate(out_v.at[c + half, pl.ds(k, sl)], t_lo)

            pltpu.sync_copy(out_v, peer_hbm.at[t])

        @pl.loop(0, tok_per_worker, step=2)
        def _(ti):
            one_token(ti, 0)
            one_token(ti + 1, 1)

    return run(table3, ids, gates, h3)


def _finish_kernel(peer_ref, x_ref, g2_ref, fg_ref, prev_ref, out_ref):
    del prev_ref
    _finish_block(x_ref, g2_ref, fg_ref, peer_ref, out_ref)


def _finish_tokens(peer, x1, g2, fg, prev, seq_len):
    d = x1.shape[1]
    tm = FINISH_TOK_BLOCK
    return pl.pallas_call(
        _finish_kernel,
        grid=(peer.shape[0] // tm,),
        in_specs=[pl.BlockSpec((tm, d), lambda i: (i, 0)),
                  pl.BlockSpec((tm, d), lambda i: (i, 0)),
                  pl.BlockSpec((1, 1, d), lambda i: (i * tm // seq_len, 0, 0)),
                  pl.BlockSpec((1, d), lambda i: (0, 0)),
                  pl.BlockSpec(memory_space=pl.ANY)],
        out_specs=pl.BlockSpec((tm, d), lambda i: (i, 0)),
        out_shape=jax.ShapeDtypeStruct(prev.shape, prev.dtype),
        input_output_aliases={4: 0},
        compiler_params=pltpu.CompilerParams(dimension_semantics=("arbitrary",)),
        name="finish_tokens",
    )(peer, x1, g2, fg, prev)


def _pack_expert_table(expert_u, expert_v):
    n_experts, d = expert_u.shape
    bits = lambda a: lax.bitcast_convert_type(a.astype(jnp.bfloat16), jnp.uint16).astype(jnp.uint32)
    pair = lambda b: (b[:, :d // 2] << 16) | b[:, d // 2:]
    words = jnp.concatenate([pair(bits(expert_u)), pair(bits(expert_v))], axis=1)
    return words.reshape(n_experts, d // LANES, LANES)


def _tc_peer_experts(idx, gates, h2, x1, g2, final_g, table, seq_len, first_tok):
    n, d = h2.shape
    m = n - first_tok
    n_chunks = d // LANES
    tb = EXPERT_TOK_BLOCK
    assert tb % EXPERT_SLOTS == 0 and seq_len % tb == 0 and first_tok % tb == 0
    first = first_tok // tb
    return pl.pallas_call(
        functools.partial(_expert_kernel, tok_block=tb, n_slots=EXPERT_SLOTS),
        grid=(m // tb,),
        in_specs=[
            pl.BlockSpec((tb, PEER_SLOTS), lambda i: (i + first, 0), memory_space=pltpu.SMEM),
            pl.BlockSpec((tb, PEER_SLOTS), lambda i: (i + first, 0)),
            pl.BlockSpec((tb, d), lambda i: (i + first, 0)),
            pl.BlockSpec((tb, d), lambda i: (i + first, 0)),
            pl.BlockSpec((1, 1, d), lambda i: ((i + first) * tb // seq_len, 0, 0)),
            pl.BlockSpec((1, d), lambda i: (0, 0)),
            pl.BlockSpec(memory_space=pl.ANY),
        ],
        out_specs=pl.BlockSpec((tb, d), lambda i: (i + first, 0)),
        out_shape=jax.ShapeDtypeStruct((n, d), jnp.float32),
        scratch_shapes=(
            [pltpu.VMEM((PEER_SLOTS * n_chunks, LANES), jnp.uint32) for _ in range(EXPERT_SLOTS)]
            + [pltpu.VMEM((tb, d), jnp.float32), pltpu.SemaphoreType.DMA((EXPERT_SLOTS,))]),
        compiler_params=pltpu.CompilerParams(dimension_semantics=("arbitrary",)),
        cost_estimate=_expert_cost(m, d),
        name="peer_experts",
    )(idx, gates, h2, x1, g2, final_g.reshape(1, d), table.reshape(-1, LANES))


def _mix_and_route(x, mod, norm1_g, w_in, b_forget, ln_v_g, w_spatial, b_spatial, w_branch_a,
                   w_branch_b, w_out, norm2_g, w_query, sub_keys):
    B, S, D = x.shape
    n = B * S
    bf16 = jnp.bfloat16
    qkv, cum, cumt, sga, gb = _input_projection(x, mod, norm1_g, w_in, b_forget, ln_v_g,
                                                w_spatial, b_spatial, w_branch_b)
    y_a = _fox_attention(qkv, cum, cumt, B, S)
    x1, h2, idx_t, gates_t = _post_attention(
        y_a, sga, gb, x.reshape(n, D), mod, norm2_g.reshape(1, D),
        w_branch_a.astype(bf16), w_out.astype(bf16), w_query.astype(bf16), sub_keys, S)
    return x1, h2, idx_t.T, gates_t.T


def kernel(x, c, w_mod, b_mod, norm1_g, w_in, b_forget, ln_v_g, w_spatial, b_spatial, w_branch_a, w_branch_b, w_out, norm2_g, w_query, sub_keys, expert_u, expert_v, final_g):
    B, S, D = x.shape
    n = B * S
    assert w_mod.shape[0] == 1, "the final RMSNorm is fused into the single layer's expert kernels"
    l = 0
    mod = _modulation(c, w_mod[l], b_mod[l]).reshape(B, 6, D)
    table = _pack_expert_table(expert_u[l], expert_v[l])
    n_chunks = D // LANES
    weights = (norm1_g[l], w_in[l], b_forget[l], ln_v_g[l], w_spatial[l], b_spatial[l],
               w_branch_a[l], w_branch_b[l], w_out[l], norm2_g[l], w_query[l], sub_keys[l])

    x1, h2, idx, gates = _mix_and_route(x, mod, *weights)
    g2 = mod[:, 5:6, :]

    n_sc = n * SC_SHARE_PERCENT // 100 // SC_SHARE_ALIGN * SC_SHARE_ALIGN
    out = _tc_peer_experts(idx, gates, h2, x1, g2, final_g, table, S, n_sc)
    if n_sc > 0:
        peer_sc = _sc_peer_experts(table, idx[:n_sc].reshape(-1), gates[:n_sc].reshape(-1),
                                   h2[:n_sc].reshape(n_sc, n_chunks, LANES))
        out = _finish_tokens(peer_sc.reshape(n_sc, D), x1, g2, final_g.reshape(1, D), out, S)
    return out.reshape(B, S, D)
```

```python
import functools

import jax
import jax.numpy as jnp
from jax import lax
from jax.experimental import pallas as pl
from jax.experimental.pallas import tpu as pltpu
from jax.experimental.pallas import tpu_sc as plsc

D_MODEL = 1024
ATT_HEADS = 8
ATT_HEAD_DIM = 64
ATT_WIDTH = ATT_HEADS * ATT_HEAD_DIM
GM_GROUPS = 4
GM_GROUP_DIM = 128
GM_WIDTH = GM_GROUPS * GM_GROUP_DIM
GM_CHUNK = 128
PEER_HEADS = 8
PEER_KEY_DIM = 256
PEER_HALF = PEER_KEY_DIM // 2
N_KEYS = 128
PEER_TOPK = 16
PEER_SLOTS = PEER_HEADS * PEER_TOPK
SPLIT_POINTS = (ATT_WIDTH, 2 * ATT_WIDTH, 3 * ATT_WIDTH, 3 * ATT_WIDTH + ATT_HEADS,
                3 * ATT_WIDTH + ATT_HEADS + 2 * GM_WIDTH,
                3 * ATT_WIDTH + ATT_HEADS + 2 * GM_WIDTH + D_MODEL)
EPS = 1e-6

LANES = 128
EXPERT_TOK_BLOCK = 128
EXPERT_SLOTS = 8
DMA_THREADS = 2
FINISH_TOK_BLOCK = 512
SC_GATHER_ROWS = 32
SC_ROW_GROUP = 4
SC_STORE_BATCH = 8
SC_SHARE_PERCENT = 58
SC_SHARE_ALIGN = 512


def _gelu(x):
    return 0.5 * x * (1.0 + jnp.tanh(0.7978845608028654 * (x + 0.044715 * (x * x * x))))


def _mod_kernel(c_ref, w_ref, b_ref, o_ref):
    c = c_ref[...]
    sc = c * jax.nn.sigmoid(c)
    o_ref[...] = jnp.dot(sc, w_ref[...], precision=lax.Precision.HIGHEST,
                         preferred_element_type=jnp.float32) + b_ref[...]


def _modulation(c, w_mod, b_mod):
    b, d = c.shape
    cols = w_mod.shape[1]
    return pl.pallas_call(
        _mod_kernel,
        grid=(cols // d,),
        in_specs=[pl.BlockSpec((b, d), lambda j: (0, 0)),
                  pl.BlockSpec((d, d), lambda j: (0, j)),
                  pl.BlockSpec((1, d), lambda j: (0, j))],
        out_specs=pl.BlockSpec((b, d), lambda j: (0, j)),
        out_shape=jax.ShapeDtypeStruct((b, cols), jnp.float32),
        name="modulation",
    )(c, w_mod, b_mod.reshape(1, cols))


INPROJ_TOK_BLOCK = 256


def _inproj_kernel(x_ref, mod_ref, n1g_ref, wqkv_ref, wf_ref, bf_ref, wz_ref, wg_ref, lng_ref,
                   wsp_ref, bsp_ref, wb_ref, qkv_ref, cum_ref, cumt_ref, sga_ref, gb_ref, carry):
    f32, bf16 = jnp.float32, jnp.bfloat16
    tm, d = x_ref.shape[1], x_ref.shape[2]
    x = x_ref[0]
    sh1 = mod_ref[0, 0:1, :]
    sc1 = mod_ref[0, 1:2, :]
    h = x * lax.rsqrt(jnp.mean(x * x, axis=-1, keepdims=True) + EPS) * n1g_ref[...]
    hb = (h * (1.0 + sc1) + sh1).astype(bf16)

    qkv = jnp.dot(hb, wqkv_ref[...], preferred_element_type=f32)
    qkv_ref[:, 0:ATT_WIDTH] = (qkv[:, 0:ATT_WIDTH] * (ATT_HEAD_DIM ** -0.5)).astype(bf16)
    qkv_ref[:, ATT_WIDTH:] = qkv[:, ATT_WIDTH:].astype(bf16)

    f = jnp.dot(hb, wf_ref[...], preferred_element_type=f32) + bf_ref[...]
    logf = jnp.minimum(f, 0.0) - jnp.log1p(jnp.exp(-jnp.abs(f)))

    @pl.when(pl.program_id(1) == 0)
    def _():
        carry[...] = jnp.zeros_like(carry)

    tri = (lax.broadcasted_iota(jnp.int32, (tm, tm), 0)
           >= lax.broadcasted_iota(jnp.int32, (tm, tm), 1)).astype(f32)
    cum = jnp.dot(tri, logf, precision=lax.Precision.HIGHEST, preferred_element_type=f32) + carry[...]
    cum_ref[...] = cum
    cumt_ref[0, 0] = jnp.transpose(cum)[0:ATT_HEADS, :]
    carry[...] = cum[tm - 1:tm, :]

    gz = _gelu(jnp.dot(hb, wz_ref[...], preferred_element_type=f32))
    u = gz[:, 0:GM_WIDTH]
    v = gz[:, GM_WIDTH:]
    mu = jnp.mean(v, axis=-1, keepdims=True)
    var = jnp.mean(jnp.square(v - mu), axis=-1, keepdims=True)
    vn = ((v - mu) * lax.rsqrt(var + EPS) * lng_ref[...]).astype(bf16)
    tril = (lax.broadcasted_iota(jnp.int32, (GM_CHUNK, GM_CHUNK), 0)
            >= lax.broadcasted_iota(jnp.int32, (GM_CHUNK, GM_CHUNK), 1))
    w_sp = [jnp.where(tril, wsp_ref[g], 0.0).astype(bf16) for g in range(GM_GROUPS)]
    rows = []
    for ck in range(tm // GM_CHUNK):
        r0 = ck * GM_CHUNK
        cols = []
        for g in range(GM_GROUPS):
            c0 = g * GM_GROUP_DIM
            mixed = jnp.dot(w_sp[g], vn[r0:r0 + GM_CHUNK, c0:c0 + GM_GROUP_DIM],
                            preferred_element_type=f32) + bsp_ref[g]
            cols.append(u[r0:r0 + GM_CHUNK, c0:c0 + GM_GROUP_DIM] * mixed)
        rows.append(jnp.concatenate(cols, axis=1))
    yb = jnp.concatenate(rows, axis=0).astype(bf16)
    ybp = jnp.dot(yb, wb_ref[...], preferred_element_type=f32)

    sg = jax.nn.sigmoid(jnp.dot(hb, wg_ref[...], preferred_element_type=f32))
    sga_ref[...] = sg[:, 0:d].astype(bf16)
    gb_ref[...] = (sg[:, d:] * ybp).astype(bf16)


def _input_projection(x, mod, norm1_g, w_in, b_forget, ln_v_g, w_spatial, b_spatial, w_branch_b):
    B, S, d = x.shape
    n = B * S
    tm = INPROJ_TOK_BLOCK
    bf16 = jnp.bfloat16
    p0, p1, p2, p3, p4, p5 = SPLIT_POINTS
    w_qkv = w_in[:, 0:p2].astype(bf16)
    w_f = jnp.pad(w_in[:, p2:p3], ((0, 0), (0, LANES - ATT_HEADS))).astype(bf16)
    b_f = jnp.pad(b_forget, (0, LANES - ATT_HEADS)).reshape(1, LANES)
    w_z = w_in[:, p3:p4].astype(bf16)
    w_g = w_in[:, p4:].astype(bf16)
    nt = S // tm
    tok = lambda w: pl.BlockSpec((tm, w), lambda b, i: (b * nt + i, 0))
    full = lambda a: pl.BlockSpec(a.shape, lambda b, i: (0,) * a.ndim)
    args = (x, mod, norm1_g.reshape(1, d), w_qkv, w_f, b_f, w_z, w_g, ln_v_g.reshape(1, GM_WIDTH),
            w_spatial, b_spatial.reshape(GM_GROUPS, GM_CHUNK, 1), w_branch_b.astype(bf16))
    return pl.pallas_call(
        _inproj_kernel,
        grid=(B, nt),
        in_specs=[pl.BlockSpec((1, tm, d), lambda b, i: (b, i, 0)),
                  pl.BlockSpec((1, 6, d), lambda b, i: (b, 0, 0))] + [full(a) for a in args[2:]],
        out_specs=[tok(3 * ATT_WIDTH), tok(LANES),
                   pl.BlockSpec((1, 1, ATT_HEADS, tm), lambda b, i: (b, i, 0, 0)), tok(d), tok(d)],
        out_shape=[jax.ShapeDtypeStruct((n, 3 * ATT_WIDTH), bf16),
                   jax.ShapeDtypeStruct((n, LANES), jnp.float32),
                   jax.ShapeDtypeStruct((B, nt, ATT_HEADS, tm), jnp.float32),
                   jax.ShapeDtypeStruct((n, d), bf16),
                   jax.ShapeDtypeStruct((n, d), bf16)],
        scratch_shapes=[pltpu.VMEM((1, LANES), jnp.float32)],
        compiler_params=pltpu.CompilerParams(dimension_semantics=("arbitrary", "arbitrary"),
                                             vmem_limit_bytes=56 * 1024 * 1024),
        name="input_projection",
    )(*args)


ATT_BLOCK = 512


def _fox_kernel(q_ref, k_ref, v_ref, cum_ref, cumt_ref, o_ref, *, blk, ratio):
    f32 = jnp.float32
    hp = pl.program_id(1)
    i = pl.program_id(2)
    dh = ATT_HEAD_DIM
    q2 = q_ref[...]
    first = lax.broadcasted_iota(jnp.int32, (1, 2 * dh), 1) < dh
    qs = (jnp.where(first, q2, jnp.zeros_like(q2)), jnp.where(first, jnp.zeros_like(q2), q2))
    cum_blk = cum_ref[...]
    lane = lax.broadcasted_iota(jnp.int32, cum_blk.shape, 1)
    cqs = [jnp.sum(jnp.where(lane == 2 * hp + e, cum_blk, 0.0), axis=1, keepdims=True)
           for e in range(2)]

    def block(j, carry, masked):
        off = pl.multiple_of(j * blk, blk)
        k2 = k_ref[0, pl.ds(off, blk), :]
        v2 = v_ref[0, pl.ds(off, blk), :]
        out = []
        for e in range(2):
            m, l, acc = carry[e]
            ck = jnp.concatenate([cumt_ref[0, j * ratio + a, pl.ds(2 * hp + e, 1), :]
                                  for a in range(ratio)], axis=1)
            s = lax.dot_general(qs[e], k2, (((1,), (1,)), ((), ())), preferred_element_type=f32)
            s = s + (cqs[e] - ck)
            if masked:
                causal = (lax.broadcasted_iota(jnp.int32, (blk, blk), 0)
                          >= lax.broadcasted_iota(jnp.int32, (blk, blk), 1))
                s = jnp.where(causal, s, -jnp.inf)
            m_new = jnp.maximum(m, jnp.max(s, axis=1, keepdims=True))
            alpha = jnp.exp(m - m_new)
            p = jnp.exp(s - m_new)
            l = alpha * l + jnp.sum(p, axis=1, keepdims=True)
            acc = alpha * acc + jnp.dot(p.astype(v2.dtype), v2, preferred_element_type=f32)
            out.append((m_new, l, acc))
        return tuple(out)

    one = (jnp.full((blk, 1), -1e30, f32), jnp.zeros((blk, 1), f32), jnp.zeros((blk, 2 * dh), f32))
    carry = lax.fori_loop(0, i, lambda j, c: block(j, c, False), (one, one))
    (_, l0, acc0), (_, l1, acc1) = block(i, carry, True)
    o_ref[...] = jnp.where(first, acc0 / l0, acc1 / l1).astype(o_ref.dtype)


def _fox_attention(qkv, cum, cumt, batch, seq_len):
    n = qkv.shape[0]
    tm = cumt.shape[-1]
    blk = max(min(ATT_BLOCK, seq_len), tm)
    nb = seq_len // blk
    pair = 2 * ATT_HEAD_DIM
    n_pairs = ATT_WIDTH // pair
    qkv3 = qkv.reshape(batch, seq_len, 3 * ATT_WIDTH)
    return pl.pallas_call(
        functools.partial(_fox_kernel, blk=blk, ratio=blk // tm),
        grid=(batch, n_pairs, nb),
        in_specs=[pl.BlockSpec((blk, pair), lambda b, hp, i: (b * nb + i, hp)),
                  pl.BlockSpec((1, seq_len, pair), lambda b, hp, i: (b, 0, n_pairs + hp)),
                  pl.BlockSpec((1, seq_len, pair), lambda b, hp, i: (b, 0, 2 * n_pairs + hp)),
                  pl.BlockSpec((blk, LANES), lambda b, hp, i: (b * nb + i, 0)),
                  pl.BlockSpec((1,) + cumt.shape[1:], lambda b, hp, i: (b, 0, 0, 0))],
        out_specs=pl.BlockSpec((blk, pair), lambda b, hp, i: (b * nb + i, hp)),
        out_shape=jax.ShapeDtypeStruct((n, ATT_WIDTH), jnp.bfloat16),
        compiler_params=pltpu.CompilerParams(
            dimension_semantics=("arbitrary", "arbitrary", "arbitrary")),
        name="fox_attention",
    )(qkv, qkv3, qkv3, cum, cumt)


def _topk_rows(s, k):
    rows, t = s.shape
    iota = lax.broadcasted_iota(jnp.int32, (rows, t), 0)
    slot = lax.broadcasted_iota(jnp.int32, (k, t), 0)
    vals = jnp.zeros((k, t), jnp.float32)
    ids = jnp.zeros((k, t), jnp.int32)
    for j in range(k):
        m = jnp.max(s, axis=0, keepdims=True)
        am = jnp.min(jnp.where(s == m, iota, rows), axis=0, keepdims=True)
        vals = jnp.where(slot == j, m, vals)
        ids = jnp.where(slot == j, am, ids)
        s = jnp.where(iota == am, -jnp.inf, s)
    return vals, ids


def _select_rows(table, pos):
    out = jnp.zeros(pos.shape, table.dtype)
    for r in range(table.shape[0]):
        out = jnp.where(pos == r, table[r:r + 1, :], out)
    return out


def _post_kernel(ya_ref, sga_ref, gb_ref, x_ref, mod_ref, n2g_ref, wa_ref, wo_ref, wq_ref, keys_ref,
                 x1_ref, h2_ref, idx_ref, gates_ref, q_scr):
    f32 = jnp.float32
    a = jnp.dot(ya_ref[...], wa_ref[...], preferred_element_type=f32)
    merged = sga_ref[...].astype(f32) * a + gb_ref[...].astype(f32)
    o = jnp.dot(merged.astype(jnp.bfloat16), wo_ref[...], preferred_element_type=f32)
    g1 = mod_ref[0, 2:3, :]
    sh2 = mod_ref[0, 3:4, :]
    sc2 = mod_ref[0, 4:5, :]
    x1 = x_ref[...] + g1 * o
    x1_ref[...] = x1
    h2 = x1 * lax.rsqrt(jnp.mean(x1 * x1, axis=-1, keepdims=True) + EPS) * n2g_ref[...]
    h2 = h2 * (1.0 + sc2) + sh2
    h2_ref[...] = h2
    qp = jnp.dot(h2.astype(jnp.bfloat16), wq_ref[...], preferred_element_type=f32)
    for j in range(2 * PEER_HEADS):
        q_scr[j] = qp[:, j * PEER_HALF:(j + 1) * PEER_HALF]

    def head(h, carry):
        tops = []
        for p in range(2):
            q = q_scr[2 * h + p]
            keys = keys_ref[2 * h + p]
            sc = lax.dot_general(keys, q, (((1,), (1,)), ((), ())),
                                 precision=lax.Precision.HIGHEST,
                                 preferred_element_type=f32)
            tops.append(_topk_rows(sc, PEER_TOPK))
        (s1, i1), (s2, i2) = tops
        half = PEER_TOPK // 2
        blocks = [s1[0:1, :] + s2]
        blocks += [s1[a:a + 1, :] + s2[0:half, :] for a in range(1, half)]
        blocks += [s1[half:, :] + s2[0:1, :]]
        vals, pos = _topk_rows(jnp.concatenate(blocks, axis=0), PEER_TOPK)
        mid = pos - PEER_TOPK
        tail0 = PEER_TOPK + (half - 1) * half
        ra = jnp.where(pos < PEER_TOPK, 0,
                       jnp.where(pos < tail0, 1 + (mid >> (half.bit_length() - 1)), pos - tail0 + half))
        rb = jnp.where(pos < PEER_TOPK, pos, jnp.where(pos < tail0, mid & (half - 1), 0))
        eid = _select_rows(i1, ra) * N_KEYS + _select_rows(i2, rb)
        e = jnp.exp(vals - vals[0:1, :])
        g = e / jnp.sum(e, axis=0, keepdims=True)
        row = pl.multiple_of(h * PEER_TOPK, PEER_TOPK)
        idx_ref[pl.ds(row, PEER_TOPK), :] = eid
        gates_ref[pl.ds(row, PEER_TOPK), :] = g
        return carry

    lax.fori_loop(0, PEER_HEADS, head, 0)


POST_TOK_BLOCK = 256


def _post_attention(ya, sga, gb, x, mod, norm2_g, w_a, w_out, w_query, sub_keys, seq_len):
    n, d = x.shape
    tm = POST_TOK_BLOCK
    blocks_per_seq = seq_len // tm
    aw = ya.shape[1]
    keys = sub_keys.reshape(2 * PEER_HEADS, N_KEYS, PEER_HALF)
    tok = lambda w: pl.BlockSpec((tm, w), lambda i: (i, 0))
    full = lambda a: pl.BlockSpec(a.shape, lambda i: (0,) * a.ndim)
    return pl.pallas_call(
        _post_kernel,
        grid=(n // tm,),
        in_specs=[tok(aw), tok(d), tok(d), tok(d),
                  pl.BlockSpec((1, 6, d), lambda i: (i // blocks_per_seq, 0, 0)),
                  full(norm2_g), full(w_a), full(w_out), full(w_query), full(keys)],
        out_specs=[tok(d), tok(d),
                   pl.BlockSpec((PEER_SLOTS, tm), lambda i: (0, i)),
                   pl.BlockSpec((PEER_SLOTS, tm), lambda i: (0, i))],
        out_shape=[jax.ShapeDtypeStruct((n, d), jnp.float32),
                   jax.ShapeDtypeStruct((n, d), jnp.float32),
                   jax.ShapeDtypeStruct((PEER_SLOTS, n), jnp.int32),
                   jax.ShapeDtypeStruct((PEER_SLOTS, n), jnp.float32)],
        scratch_shapes=[pltpu.VMEM((2 * PEER_HEADS, tm, PEER_HALF), jnp.float32)],
        compiler_params=pltpu.CompilerParams(dimension_semantics=("arbitrary",),
                                             vmem_limit_bytes=48 * 1024 * 1024),
        name="post_attention",
    )(ya, sga, gb, x, mod, norm2_g, w_a, w_out, w_query, keys)


def _rowsum_bcast(p, ones_bf16):
    hi = p.astype(jnp.bfloat16)
    lo = (p - hi.astype(jnp.float32)).astype(jnp.bfloat16)
    return (jnp.dot(hi, ones_bf16, preferred_element_type=jnp.float32)
            + jnp.dot(lo, ones_bf16, preferred_element_type=jnp.float32))


def _word_halves(w):
    return (lax.bitcast_convert_type(w & jnp.uint32(0xFFFF0000), jnp.float32),
            lax.bitcast_convert_type(w << 16, jnp.float32))


def _eval_experts(chunk, hrow, grow):
    half = hrow.shape[1] // LANES // 2
    ones_bf16 = jnp.ones((LANES, LANES), jnp.bfloat16)
    eye = (lax.broadcasted_iota(jnp.int32, (PEER_SLOTS, LANES), 0)
           == lax.broadcasted_iota(jnp.int32, (PEER_SLOTS, LANES), 1))
    hpart = lambda c: hrow[:, c * LANES:(c + 1) * LANES]
    psum = None
    for c in range(half):
        hi, lo = _word_halves(chunk(c))
        p = hi * hpart(c) + lo * hpart(c + half)
        psum = p if psum is None else psum + p
    act = _gelu(_rowsum_bcast(psum, ones_bf16))
    gcol = _rowsum_bcast(jnp.where(eye, grow, 0.0), ones_bf16)
    coef = gcol * act
    outs_hi, outs_lo = [], []
    for c in range(half):
        hi, lo = _word_halves(chunk(c + half))
        outs_hi.append(jnp.sum(coef * hi, axis=0, keepdims=True))
        outs_lo.append(jnp.sum(coef * lo, axis=0, keepdims=True))
    return jnp.concatenate(outs_hi + outs_lo, axis=-1)


def _finish_block(x_ref, g2_ref, fg_ref, peer, out_ref):
    y = x_ref[...] + g2_ref[0] * peer[...]
    out_ref[...] = y * lax.rsqrt(jnp.mean(y * y, axis=-1, keepdims=True) + EPS) * fg_ref[...]


def _expert_kernel(idx_ref, gates_ref, h_ref, x_ref, g2_ref, fg_ref, tab_ref, out_ref,
                   *scratch, tok_block, n_slots):
    bufs = scratch[:n_slots]
    peer, sem = scratch[n_slots], scratch[n_slots + 1]
    d_model = h_ref.shape[-1]
    n_chunks = d_model // LANES
    rows_per_tok = PEER_SLOTS * n_chunks

    def issue(t, s):
        for r in range(PEER_SLOTS):
            row = pl.multiple_of(idx_ref[t, r] * n_chunks, n_chunks)
            pltpu.make_async_copy(tab_ref.at[pl.ds(row, n_chunks), :],
                                  bufs[s].at[pl.ds(r * n_chunks, n_chunks), :],
                                  sem.at[s]).start(priority=r % DMA_THREADS)

    def wait(s):
        pltpu.make_async_copy(tab_ref.at[pl.ds(0, rows_per_tok), :], bufs[s], sem.at[s]).wait()

    def compute(t, s):
        chunk = lambda c: bufs[s][pl.ds(c, PEER_SLOTS, stride=n_chunks), :]
        peer[pl.ds(t, 1), :] = _eval_experts(chunk, h_ref[pl.ds(t, 1), :], gates_ref[pl.ds(t, 1), :])

    def step(t, s, prefetch):
        wait(s)
        if prefetch:
            issue(t + n_slots - 1, (s - 1) % n_slots)
        compute(t, s)

    for t in range(n_slots - 1):
        issue(t, t)

    n_groups = tok_block // n_slots

    def group(g, carry):
        for s in range(n_slots):
            step(g * n_slots + s, s, True)
        return carry

    lax.fori_loop(0, n_groups - 1, group, 0)
    for s in range(n_slots):
        t = (n_groups - 1) * n_slots + s
        step(t, s, t + n_slots - 1 < tok_block)

    _finish_block(x_ref, g2_ref, fg_ref, peer, out_ref)


def _expert_cost(n_tok, d):
    pairs = n_tok * PEER_SLOTS
    return pl.CostEstimate(flops=4 * pairs * d, transcendentals=pairs,
                           bytes_accessed=4 * pairs * d + 12 * n_tok * d + 8 * pairs)


def _sc_tanh(y):
    return 1.0 - 2.0 / (jnp.exp(2.0 * y) + 1.0)


def _sc_peer_experts(table3, ids, gates, h3):
    n_tok, n_chunks, lanes = h3.shape
    info = plsc.get_sparse_core_info()
    sl = info.num_lanes
    n_workers = info.num_cores * info.num_subcores
    tok_per_worker = n_tok // n_workers
    assert tok_per_worker * n_workers == n_tok and tok_per_worker % 2 == 0
    rows = SC_GATHER_ROWS
    n_sub = PEER_SLOTS // rows
    assert n_sub % 2 == 0
    half = n_chunks // 2
    pieces = [(c, k * sl) for c in range(half) for k in range(lanes // sl)]
    mesh = plsc.VectorSubcoreMesh(core_axis_name="c", subcore_axis_name="s")
    buf = lambda dt: pltpu.VMEM((rows, n_chunks, lanes), dt)

    @functools.partial(
        pl.kernel, mesh=mesh,
        out_type=jax.ShapeDtypeStruct((n_tok, n_chunks, lanes), jnp.float32),
        scratch_types=[pltpu.VMEM((PEER_SLOTS,), jnp.int32), pltpu.VMEM((PEER_SLOTS,), jnp.int32),
                       pltpu.VMEM((PEER_SLOTS,), jnp.float32), pltpu.VMEM((PEER_SLOTS,), jnp.float32),
                       pltpu.VMEM((n_chunks, lanes), jnp.float32),
                       pltpu.VMEM((n_chunks, lanes), jnp.float32),
                       buf(jnp.uint32), buf(jnp.uint32), pltpu.VMEM((n_chunks, lanes), jnp.float32),
                       pltpu.SemaphoreType.DMA((2,)), pltpu.SemaphoreType.DMA((2,))],
        compiler_params=pltpu.CompilerParams(needs_layout_passes=False),
        cost_estimate=_expert_cost(n_tok, n_chunks * lanes),
        name="sc_peer_experts")
    def run(table_hbm, ids_hbm, gates_hbm, h_hbm, peer_hbm, ids0, ids1, g0, g1, h0, h1, rows0, rows1,
            out_v, gsem, isem):
        bufs = (rows0, rows1)
        ids_p, g_p, h_p = (ids0, ids1), (g0, g1), (h0, h1)
        t_first = (lax.axis_index("s") * info.num_cores + lax.axis_index("c")) * tok_per_worker

        def token_inputs(t, p):
            off = pl.multiple_of(t * PEER_SLOTS, PEER_SLOTS)
            return (pltpu.make_async_copy(ids_hbm.at[pl.ds(off, PEER_SLOTS)], ids_p[p], isem.at[p]),
                    pltpu.make_async_copy(gates_hbm.at[pl.ds(off, PEER_SLOTS)], g_p[p], isem.at[p]),
                    pltpu.make_async_copy(h_hbm.at[t], h_p[p], isem.at[p]))

        def gather(p, s):
            return pltpu.make_async_copy(table_hbm.at[ids_p[p].at[pl.ds(s * rows, rows)]],
                                         bufs[s % 2], gsem.at[s % 2])

        for cp in token_inputs(t_first, 0):
            cp.start()
        for cp in token_inputs(t_first, 0):
            cp.wait()
        gather(0, 0).start()

        def one_token(ti, p):
            t = t_first + ti
            g_v, h_v = g_p[p], h_p[p]
            has_next = ti + 1 < tok_per_worker

            @pl.when(has_next)
            def _():
                for cp in token_inputs(t + 1, 1 - p):
                    cp.start()

            for c, k in pieces:
                out_v[c, pl.ds(k, sl)] = jnp.zeros((sl,), jnp.float32)
                out_v[c + half, pl.ds(k, sl)] = jnp.zeros((sl,), jnp.float32)
            for s in range(n_sub):
                gather(p, s).wait()
                if s + 1 < n_sub:
                    gather(p, s + 1).start()
                else:
                    @pl.when(has_next)
                    def _():
                        for cp in token_inputs(t + 1, 1 - p):
                            cp.wait()
                        gather(1 - p, 0).start()
                rbuf = bufs[s % 2]

                @pl.loop(0, rows, step=SC_ROW_GROUP)
                def _(r0):
                    accs = [jnp.zeros((sl,), jnp.float32) for _ in range(SC_ROW_GROUP)]
                    for c, k in pieces:
                        h_hi, h_lo = h_v[c, pl.ds(k, sl)], h_v[c + half, pl.ds(k, sl)]
                        for j in range(SC_ROW_GROUP):
                            hi, lo = _word_halves(rbuf[r0 + j, c, pl.ds(k, sl)])
                            accs[j] = accs[j] + hi * h_hi + lo * h_lo
                    coefs = []
                    for j in range(SC_ROW_GROUP):
                        a = jnp.broadcast_to(jnp.sum(accs[j]), (sl,))
                        act = 0.5 * a * (1.0 + _sc_tanh(0.7978845608028654 * (a + 0.044715 * (a * a * a))))
                        gate = plsc.load_gather(g_v, [jnp.broadcast_to(s * rows + r0 + j, (sl,))])
                        coefs.append(gate * act)
                    for b0 in range(0, len(pieces), SC_STORE_BATCH):
                        tots = []
                        for c, k in pieces[b0:b0 + SC_STORE_BATCH]:
                            t_hi = t_lo = None
                            for j in range(SC_ROW_GROUP):
                                hi, lo = _word_halves(rbuf[r0 + j, c + half, pl.ds(k, sl)])
                                t_hi = coefs[j] * hi if t_hi is None else t_hi + coefs[j] * hi
                                t_lo = coefs[j] * lo if t_lo is None else t_lo + coefs[j] * lo
                            tots.append((t_hi, t_lo))
                        for (c, k), (t_hi, t_lo) in zip(pieces[b0:b0 + SC_STORE_BATCH], tots):
                            plsc.addupdate(out_v.at[c, pl.ds(k, sl)], t_hi)
                            plsc.addupdate(out_v.at[c + half, pl.ds(k, sl)], t_lo)

            pltpu.sync_copy(out_v, peer_hbm.at[t])

        @pl.loop(0, tok_per_worker, step=2)
        def _(ti):
            one_token(ti, 0)
            one_token(ti + 1, 1)

    return run(table3, ids, gates, h3)


def _finish_kernel(peer_ref, x_ref, g2_ref, fg_ref, prev_ref, out_ref):
    del prev_ref
    _finish_block(x_ref, g2_ref, fg_ref, peer_ref, out_ref)


def _finish_tokens(peer, x1, g2, fg, prev, seq_len):
    d = x1.shape[1]
    tm = FINISH_TOK_BLOCK
    return pl.pallas_call(
        _finish_kernel,
        grid=(peer.shape[0] // tm,),
        in_specs=[pl.BlockSpec((tm, d), lambda i: (i, 0)),
                  pl.BlockSpec((tm, d), lambda i: (i, 0)),
                  pl.BlockSpec((1, 1, d), lambda i: (i * tm // seq_len, 0, 0)),
                  pl.BlockSpec((1, d), lambda i: (0, 0)),
                  pl.BlockSpec(memory_space=pl.ANY)],
        out_specs=pl.BlockSpec((tm, d), lambda i: (i, 0)),
        out_shape=jax.ShapeDtypeStruct(prev.shape, prev.dtype),
        input_output_aliases={4: 0},
        compiler_params=pltpu.CompilerParams(dimension_semantics=("arbitrary",)),
        name="finish_tokens",
    )(peer, x1, g2, fg, prev)


def _pack_expert_table(expert_u, expert_v):
    n_experts, d = expert_u.shape
    bits = lambda a: lax.bitcast_convert_type(a.astype(jnp.bfloat16), jnp.uint16).astype(jnp.uint32)
    pair = lambda b: (b[:, :d // 2] << 16) | b[:, d // 2:]
    words = jnp.concatenate([pair(bits(expert_u)), pair(bits(expert_v))], axis=1)
    return words.reshape(n_experts, d // LANES, LANES)


def _tc_peer_experts(idx, gates, h2, x1, g2, final_g, table, seq_len, first_tok):
    n, d = h2.shape
    m = n - first_tok
    n_chunks = d // LANES
    tb = EXPERT_TOK_BLOCK
    assert tb % EXPERT_SLOTS == 0 and seq_len % tb == 0 and first_tok % tb == 0
    first = first_tok // tb
    return pl.pallas_call(
        functools.partial(_expert_kernel, tok_block=tb, n_slots=EXPERT_SLOTS),
        grid=(m // tb,),
        in_specs=[
            pl.BlockSpec((tb, PEER_SLOTS), lambda i: (i + first, 0), memory_space=pltpu.SMEM),
            pl.BlockSpec((tb, PEER_SLOTS), lambda i: (i + first, 0)),
            pl.BlockSpec((tb, d), lambda i: (i + first, 0)),
            pl.BlockSpec((tb, d), lambda i: (i + first, 0)),
            pl.BlockSpec((1, 1, d), lambda i: ((i + first) * tb // seq_len, 0, 0)),
            pl.BlockSpec((1, d), lambda i: (0, 0)),
            pl.BlockSpec(memory_space=pl.ANY),
        ],
        out_specs=pl.BlockSpec((tb, d), lambda i: (i + first, 0)),
        out_shape=jax.ShapeDtypeStruct((n, d), jnp.float32),
        scratch_shapes=(
            [pltpu.VMEM((PEER_SLOTS * n_chunks, LANES), jnp.uint32) for _ in range(EXPERT_SLOTS)]
            + [pltpu.VMEM((tb, d), jnp.float32), pltpu.SemaphoreType.DMA((EXPERT_SLOTS,))]),
        compiler_params=pltpu.CompilerParams(dimension_semantics=("arbitrary",)),
        cost_estimate=_expert_cost(m, d),
        name="peer_experts",
    )(idx, gates, h2, x1, g2, final_g.reshape(1, d), table.reshape(-1, LANES))


def _mix_and_route(x, mod, norm1_g, w_in, b_forget, ln_v_g, w_spatial, b_spatial, w_branch_a,
                   w_branch_b, w_out, norm2_g, w_query, sub_keys):
    B, S, D = x.shape
    n = B * S
    bf16 = jnp.bfloat16
    qkv, cum, cumt, sga, gb = _input_projection(x, mod, norm1_g, w_in, b_forget, ln_v_g,
                                                w_spatial, b_spatial, w_branch_b)
    y_a = _fox_attention(qkv, cum, cumt, B, S)
    x1, h2, idx_t, gates_t = _post_attention(
        y_a, sga, gb, x.reshape(n, D), mod, norm2_g.reshape(1, D),
        w_branch_a.astype(bf16), w_out.astype(bf16), w_query.astype(bf16), sub_keys, S)
    return x1, h2, idx_t.T, gates_t.T


def kernel(x, c, w_mod, b_mod, norm1_g, w_in, b_forget, ln_v_g, w_spatial, b_spatial, w_branch_a, w_branch_b, w_out, norm2_g, w_query, sub_keys, expert_u, expert_v, final_g):
    B, S, D = x.shape
    n = B * S
    assert w_mod.shape[0] == 1, "the final RMSNorm is fused into the single layer's expert kernels"
    l = 0
    mod = _modulation(c, w_mod[l], b_mod[l]).reshape(B, 6, D)
    table = _pack_expert_table(expert_u[l], expert_v[l])
    n_chunks = D // LANES
    weights = (norm1_g[l], w_in[l], b_forget[l], ln_v_g[l], w_spatial[l], b_spatial[l],
               w_branch_a[l], w_branch_b[l], w_out[l], norm2_g[l], w_query[l], sub_keys[l])

    x1, h2, idx, gates = _mix_and_route(x, mod, *weights)
    g2 = mod[:, 5:6, :]

    n_sc = n * SC_SHARE_PERCENT // 100 // SC_SHARE_ALIGN * SC_SHARE_ALIGN
    out = _tc_peer_experts(idx, gates, h2, x1, g2, final_g, table, S, n_sc)
    if n_sc > 0:
        peer_sc = _sc_peer_experts(table, idx[:n_sc].reshape(-1), gates[:n_sc].reshape(-1),
                                   h2[:n_sc].reshape(n_sc, n_chunks, LANES))
        out = _finish_tokens(peer_sc.reshape(n_sc, D), x1, g2, final_g.reshape(1, D), out, S)
    return out.reshape(B, S, D)
```

```python
import functools

import jax
import jax.numpy as jnp
from jax import lax
from jax.experimental import pallas as pl
from jax.experimental.pallas import tpu as pltpu
from jax.experimental.pallas import tpu_sc as plsc

D_MODEL = 1024
ATT_HEADS = 8
ATT_HEAD_DIM = 64
ATT_WIDTH = ATT_HEADS * ATT_HEAD_DIM
GM_GROUPS = 4
GM_GROUP_DIM = 128
GM_WIDTH = GM_GROUPS * GM_GROUP_DIM
GM_CHUNK = 128
PEER_HEADS = 8
PEER_KEY_DIM = 256
PEER_HALF = PEER_KEY_DIM // 2
N_KEYS = 128
PEER_TOPK = 16
PEER_SLOTS = PEER_HEADS * PEER_TOPK
SPLIT_POINTS = (ATT_WIDTH, 2 * ATT_WIDTH, 3 * ATT_WIDTH, 3 * ATT_WIDTH + ATT_HEADS,
                3 * ATT_WIDTH + ATT_HEADS + 2 * GM_WIDTH,
                3 * ATT_WIDTH + ATT_HEADS + 2 * GM_WIDTH + D_MODEL)
EPS = 1e-6

LANES = 128
EXPERT_TOK_BLOCK = 128
EXPERT_SLOTS = 8
DMA_THREADS = 2
FINISH_TOK_BLOCK = 512
SC_GATHER_ROWS = 32
SC_ROW_GROUP = 4
SC_STORE_BATCH = 8
SC_SHARE_PERCENT = 57
SC_EARLY_PERCENT = 19
SC_SHARE_ALIGN = 512


def _gelu(x):
    return 0.5 * x * (1.0 + jnp.tanh(0.7978845608028654 * (x + 0.044715 * (x * x * x))))


def _mod_kernel(c_ref, w_ref, b_ref, o_ref):
    c = c_ref[...]
    sc = c * jax.nn.sigmoid(c)
    o_ref[...] = jnp.dot(sc, w_ref[...], precision=lax.Precision.HIGHEST,
                         preferred_element_type=jnp.float32) + b_ref[...]


def _modulation(c, w_mod, b_mod):
    b, d = c.shape
    cols = w_mod.shape[1]
    return pl.pallas_call(
        _mod_kernel,
        grid=(cols // d,),
        in_specs=[pl.BlockSpec((b, d), lambda j: (0, 0)),
                  pl.BlockSpec((d, d), lambda j: (0, j)),
                  pl.BlockSpec((1, d), lambda j: (0, j))],
        out_specs=pl.BlockSpec((b, d), lambda j: (0, j)),
        out_shape=jax.ShapeDtypeStruct((b, cols), jnp.float32),
        name="modulation",
    )(c, w_mod, b_mod.reshape(1, cols))


INPROJ_TOK_BLOCK = 256


def _inproj_kernel(x_ref, mod_ref, n1g_ref, wqkv_ref, wf_ref, bf_ref, wz_ref, wg_ref, lng_ref,
                   wsp_ref, bsp_ref, wb_ref, qkv_ref, cum_ref, cumt_ref, sga_ref, gb_ref, carry):
    f32, bf16 = jnp.float32, jnp.bfloat16
    tm, d = x_ref.shape[1], x_ref.shape[2]
    x = x_ref[0]
    sh1 = mod_ref[0, 0:1, :]
    sc1 = mod_ref[0, 1:2, :]
    h = x * lax.rsqrt(jnp.mean(x * x, axis=-1, keepdims=True) + EPS) * n1g_ref[...]
    hb = (h * (1.0 + sc1) + sh1).astype(bf16)

    qkv = jnp.dot(hb, wqkv_ref[...], preferred_element_type=f32)
    qkv_ref[:, 0:ATT_WIDTH] = (qkv[:, 0:ATT_WIDTH] * (ATT_HEAD_DIM ** -0.5)).astype(bf16)
    qkv_ref[:, ATT_WIDTH:] = qkv[:, ATT_WIDTH:].astype(bf16)

    f = jnp.dot(hb, wf_ref[...], preferred_element_type=f32) + bf_ref[...]
    logf = jnp.minimum(f, 0.0) - jnp.log1p(jnp.exp(-jnp.abs(f)))

    @pl.when(pl.program_id(1) == 0)
    def _():
        carry[...] = jnp.zeros_like(carry)

    tri = (lax.broadcasted_iota(jnp.int32, (tm, tm), 0)
           >= lax.broadcasted_iota(jnp.int32, (tm, tm), 1)).astype(f32)
    cum = jnp.dot(tri, logf, precision=lax.Precision.HIGHEST, preferred_element_type=f32) + carry[...]
    cum_ref[...] = cum
    cumt_ref[0, 0] = jnp.transpose(cum)[0:ATT_HEADS, :]
    carry[...] = cum[tm - 1:tm, :]

    gz = _gelu(jnp.dot(hb, wz_ref[...], preferred_element_type=f32))
    u = gz[:, 0:GM_WIDTH]
    v = gz[:, GM_WIDTH:]
    mu = jnp.mean(v, axis=-1, keepdims=True)
    var = jnp.mean(jnp.square(v - mu), axis=-1, keepdims=True)
    vn = ((v - mu) * lax.rsqrt(var + EPS) * lng_ref[...]).astype(bf16)
    tril = (lax.broadcasted_iota(jnp.int32, (GM_CHUNK, GM_CHUNK), 0)
            >= lax.broadcasted_iota(jnp.int32, (GM_CHUNK, GM_CHUNK), 1))
    w_sp = [jnp.where(tril, wsp_ref[g], 0.0).astype(bf16) for g in range(GM_GROUPS)]
    rows = []
    for ck in range(tm // GM_CHUNK):
        r0 = ck * GM_CHUNK
        cols = []
        for g in range(GM_GROUPS):
            c0 = g * GM_GROUP_DIM
            mixed = jnp.dot(w_sp[g], vn[r0:r0 + GM_CHUNK, c0:c0 + GM_GROUP_DIM],
                            preferred_element_type=f32) + bsp_ref[g]
            cols.append(u[r0:r0 + GM_CHUNK, c0:c0 + GM_GROUP_DIM] * mixed)
        rows.append(jnp.concatenate(cols, axis=1))
    yb = jnp.concatenate(rows, axis=0).astype(bf16)
    ybp = jnp.dot(yb, wb_ref[...], preferred_element_type=f32)

    sg = jax.nn.sigmoid(jnp.dot(hb, wg_ref[...], preferred_element_type=f32))
    sga_ref[...] = sg[:, 0:d].astype(bf16)
    gb_ref[...] = (sg[:, d:] * ybp).astype(bf16)


def _input_projection(x, mod, norm1_g, w_in, b_forget, ln_v_g, w_spatial, b_spatial, w_branch_b):
    B, S, d = x.shape
    n = B * S
    tm = INPROJ_TOK_BLOCK
    bf16 = jnp.bfloat16
    p0, p1, p2, p3, p4, p5 = SPLIT_POINTS
    w_qkv = w_in[:, 0:p2].astype(bf16)
    w_f = jnp.pad(w_in[:, p2:p3], ((0, 0), (0, LANES - ATT_HEADS))).astype(bf16)
    b_f = jnp.pad(b_forget, (0, LANES - ATT_HEADS)).reshape(1, LANES)
    w_z = w_in[:, p3:p4].astype(bf16)
    w_g = w_in[:, p4:].astype(bf16)
    nt = S // tm
    tok = lambda w: pl.BlockSpec((tm, w), lambda b, i: (b * nt + i, 0))
    full = lambda a: pl.BlockSpec(a.shape, lambda b, i: (0,) * a.ndim)
    args = (x, mod, norm1_g.reshape(1, d), w_qkv, w_f, b_f, w_z, w_g, ln_v_g.reshape(1, GM_WIDTH),
            w_spatial, b_spatial.reshape(GM_GROUPS, GM_CHUNK, 1), w_branch_b.astype(bf16))
    return pl.pallas_call(
        _inproj_kernel,
        grid=(B, nt),
        in_specs=[pl.BlockSpec((1, tm, d), lambda b, i: (b, i, 0)),
                  pl.BlockSpec((1, 6, d), lambda b, i: (b, 0, 0))] + [full(a) for a in args[2:]],
        out_specs=[tok(3 * ATT_WIDTH), tok(LANES),
                   pl.BlockSpec((1, 1, ATT_HEADS, tm), lambda b, i: (b, i, 0, 0)), tok(d), tok(d)],
        out_shape=[jax.ShapeDtypeStruct((n, 3 * ATT_WIDTH), bf16),
                   jax.ShapeDtypeStruct((n, LANES), jnp.float32),
                   jax.ShapeDtypeStruct((B, nt, ATT_HEADS, tm), jnp.float32),
                   jax.ShapeDtypeStruct((n, d), bf16),
                   jax.ShapeDtypeStruct((n, d), bf16)],
        scratch_shapes=[pltpu.VMEM((1, LANES), jnp.float32)],
        compiler_params=pltpu.CompilerParams(dimension_semantics=("arbitrary", "arbitrary"),
                                             vmem_limit_bytes=56 * 1024 * 1024),
        name="input_projection",
    )(*args)


ATT_BLOCK = 512


def _fox_kernel(q_ref, k_ref, v_ref, cum_ref, cumt_ref, o_ref, *, blk, ratio):
    f32 = jnp.float32
    hp = pl.program_id(1)
    i = pl.program_id(2)
    dh = ATT_HEAD_DIM
    q2 = q_ref[...]
    first = lax.broadcasted_iota(jnp.int32, (1, 2 * dh), 1) < dh
    qs = (jnp.where(first, q2, jnp.zeros_like(q2)), jnp.where(first, jnp.zeros_like(q2), q2))
    cum_blk = cum_ref[...]
    lane = lax.broadcasted_iota(jnp.int32, cum_blk.shape, 1)
    cqs = [jnp.sum(jnp.where(lane == 2 * hp + e, cum_blk, 0.0), axis=1, keepdims=True)
           for e in range(2)]

    def block(j, carry, masked):
        off = pl.multiple_of(j * blk, blk)
        k2 = k_ref[0, pl.ds(off, blk), :]
        v2 = v_ref[0, pl.ds(off, blk), :]
        out = []
        for e in range(2):
            m, l, acc = carry[e]
            ck = jnp.concatenate([cumt_ref[0, j * ratio + a, pl.ds(2 * hp + e, 1), :]
                                  for a in range(ratio)], axis=1)
            s = lax.dot_general(qs[e], k2, (((1,), (1,)), ((), ())), preferred_element_type=f32)
            s = s + (cqs[e] - ck)
            if masked:
                causal = (lax.broadcasted_iota(jnp.int32, (blk, blk), 0)
                          >= lax.broadcasted_iota(jnp.int32, (blk, blk), 1))
                s = jnp.where(causal, s, -jnp.inf)
            m_new = jnp.maximum(m, jnp.max(s, axis=1, keepdims=True))
            alpha = jnp.exp(m - m_new)
            p = jnp.exp(s - m_new)
            l = alpha * l + jnp.sum(p, axis=1, keepdims=True)
            acc = alpha * acc + jnp.dot(p.astype(v2.dtype), v2, preferred_element_type=f32)
            out.append((m_new, l, acc))
        return tuple(out)

    one = (jnp.full((blk, 1), -1e30, f32), jnp.zeros((blk, 1), f32), jnp.zeros((blk, 2 * dh), f32))
    carry = lax.fori_loop(0, i, lambda j, c: block(j, c, False), (one, one))
    (_, l0, acc0), (_, l1, acc1) = block(i, carry, True)
    o_ref[...] = jnp.where(first, acc0 / l0, acc1 / l1).astype(o_ref.dtype)


def _fox_attention(qkv, cum, cumt, batch, seq_len):
    n = qkv.shape[0]
    tm = cumt.shape[-1]
    blk = max(min(ATT_BLOCK, seq_len), tm)
    nb = seq_len // blk
    pair = 2 * ATT_HEAD_DIM
    n_pairs = ATT_WIDTH // pair
    qkv3 = qkv.reshape(batch, seq_len, 3 * ATT_WIDTH)
    return pl.pallas_call(
        functools.partial(_fox_kernel, blk=blk, ratio=blk // tm),
        grid=(batch, n_pairs, nb),
        in_specs=[pl.BlockSpec((blk, pair), lambda b, hp, i: (b * nb + i, hp)),
                  pl.BlockSpec((1, seq_len, pair), lambda b, hp, i: (b, 0, n_pairs + hp)),
                  pl.BlockSpec((1, seq_len, pair), lambda b, hp, i: (b, 0, 2 * n_pairs + hp)),
                  pl.BlockSpec((blk, LANES), lambda b, hp, i: (b * nb + i, 0)),
                  pl.BlockSpec((1,) + cumt.shape[1:], lambda b, hp, i: (b, 0, 0, 0))],
        out_specs=pl.BlockSpec((blk, pair), lambda b, hp, i: (b * nb + i, hp)),
        out_shape=jax.ShapeDtypeStruct((n, ATT_WIDTH), jnp.bfloat16),
        compiler_params=pltpu.CompilerParams(
            dimension_semantics=("arbitrary", "arbitrary", "arbitrary")),
        name="fox_attention",
    )(qkv, qkv3, qkv3, cum, cumt)


def _topk_rows(s, k):
    rows, t = s.shape
    iota = lax.broadcasted_iota(jnp.int32, (rows, t), 0)
    slot = lax.broadcasted_iota(jnp.int32, (k, t), 0)
    vals = jnp.zeros((k, t), jnp.float32)
    ids = jnp.zeros((k, t), jnp.int32)
    for j in range(k):
        m = jnp.max(s, axis=0, keepdims=True)
        am = jnp.min(jnp.where(s == m, iota, rows), axis=0, keepdims=True)
        vals = jnp.where(slot == j, m, vals)
        ids = jnp.where(slot == j, am, ids)
        s = jnp.where(iota == am, -jnp.inf, s)
    return vals, ids


def _select_rows(table, pos):
    out = jnp.zeros(pos.shape, table.dtype)
    for r in range(table.shape[0]):
        out = jnp.where(pos == r, table[r:r + 1, :], out)
    return out


def _post_kernel(ya_ref, sga_ref, gb_ref, x_ref, mod_ref, n2g_ref, wa_ref, wo_ref, wq_ref, keys_ref,
                 x1_ref, h2_ref, idx_ref, gates_ref, q_scr):
    f32 = jnp.float32
    a = jnp.dot(ya_ref[...], wa_ref[...], preferred_element_type=f32)
    merged = sga_ref[...].astype(f32) * a + gb_ref[...].astype(f32)
    o = jnp.dot(merged.astype(jnp.bfloat16), wo_ref[...], preferred_element_type=f32)
    g1 = mod_ref[0, 2:3, :]
    sh2 = mod_ref[0, 3:4, :]
    sc2 = mod_ref[0, 4:5, :]
    x1 = x_ref[...] + g1 * o
    x1_ref[...] = x1
    h2 = x1 * lax.rsqrt(jnp.mean(x1 * x1, axis=-1, keepdims=True) + EPS) * n2g_ref[...]
    h2 = h2 * (1.0 + sc2) + sh2
    h2_ref[...] = h2
    qp = jnp.dot(h2.astype(jnp.bfloat16), wq_ref[...], preferred_element_type=f32)
    for j in range(2 * PEER_HEADS):
        q_scr[j] = qp[:, j * PEER_HALF:(j + 1) * PEER_HALF]

    def head(h, carry):
        tops = []
        for p in range(2):
            q = q_scr[2 * h + p]
            keys = keys_ref[2 * h + p]
            sc = lax.dot_general(keys, q, (((1,), (1,)), ((), ())),
                                 precision=lax.Precision.HIGHEST,
                                 preferred_element_type=f32)
            tops.append(_topk_rows(sc, PEER_TOPK))
        (s1, i1), (s2, i2) = tops
        half = PEER_TOPK // 2
        blocks = [s1[0:1, :] + s2]
        blocks += [s1[a:a + 1, :] + s2[0:half, :] for a in range(1, half)]
        blocks += [s1[half:, :] + s2[0:1, :]]
        vals, pos = _topk_rows(jnp.concatenate(blocks, axis=0), PEER_TOPK)
        mid = pos - PEER_TOPK
        tail0 = PEER_TOPK + (half - 1) * half
        ra = jnp.where(pos < PEER_TOPK, 0,
                       jnp.where(pos < tail0, 1 + (mid >> (half.bit_length() - 1)), pos - tail0 + half))
        rb = jnp.where(pos < PEER_TOPK, pos, jnp.where(pos < tail0, mid & (half - 1), 0))
        eid = _select_rows(i1, ra) * N_KEYS + _select_rows(i2, rb)
        e = jnp.exp(vals - vals[0:1, :])
        g = e / jnp.sum(e, axis=0, keepdims=True)
        row = pl.multiple_of(h * PEER_TOPK, PEER_TOPK)
        idx_ref[pl.ds(row, PEER_TOPK), :] = eid
        gates_ref[pl.ds(row, PEER_TOPK), :] = g
        return carry

    lax.fori_loop(0, PEER_HEADS, head, 0)


POST_TOK_BLOCK = 256


def _post_attention(ya, sga, gb, x, mod, norm2_g, w_a, w_out, w_query, sub_keys, seq_len, tok0, n_tok):
    d = x.shape[1]
    tm = POST_TOK_BLOCK
    assert tok0 % tm == 0 and n_tok % tm == 0 and seq_len % tm == 0
    first = tok0 // tm
    blocks_per_seq = seq_len // tm
    aw = ya.shape[1]
    keys = sub_keys.reshape(2 * PEER_HEADS, N_KEYS, PEER_HALF)
    tok_in = lambda w: pl.BlockSpec((tm, w), lambda i: (i + first, 0))
    tok_out = lambda w: pl.BlockSpec((tm, w), lambda i: (i, 0))
    full = lambda a: pl.BlockSpec(a.shape, lambda i: (0,) * a.ndim)
    return pl.pallas_call(
        _post_kernel,
        grid=(n_tok // tm,),
        in_specs=[tok_in(aw), tok_in(d), tok_in(d), tok_in(d),
                  pl.BlockSpec((1, 6, d), lambda i: ((i + first) // blocks_per_seq, 0, 0)),
                  full(norm2_g), full(w_a), full(w_out), full(w_query), full(keys)],
        out_specs=[tok_out(d), tok_out(d),
                   pl.BlockSpec((PEER_SLOTS, tm), lambda i: (0, i)),
                   pl.BlockSpec((PEER_SLOTS, tm), lambda i: (0, i))],
        out_shape=[jax.ShapeDtypeStruct((n_tok, d), jnp.float32),
                   jax.ShapeDtypeStruct((n_tok, d), jnp.float32),
                   jax.ShapeDtypeStruct((PEER_SLOTS, n_tok), jnp.int32),
                   jax.ShapeDtypeStruct((PEER_SLOTS, n_tok), jnp.float32)],
        scratch_shapes=[pltpu.VMEM((2 * PEER_HEADS, tm, PEER_HALF), jnp.float32)],
        compiler_params=pltpu.CompilerParams(dimension_semantics=("arbitrary",),
                                             vmem_limit_bytes=48 * 1024 * 1024),
        name="post_attention",
    )(ya, sga, gb, x, mod, norm2_g, w_a, w_out, w_query, keys)


def _rowsum_bcast(p, ones_bf16):
    hi = p.astype(jnp.bfloat16)
    lo = (p - hi.astype(jnp.float32)).astype(jnp.bfloat16)
    return (jnp.dot(hi, ones_bf16, preferred_element_type=jnp.float32)
            + jnp.dot(lo, ones_bf16, preferred_element_type=jnp.float32))


def _word_halves(w):
    return (lax.bitcast_convert_type(w & jnp.uint32(0xFFFF0000), jnp.float32),
            lax.bitcast_convert_type(w << 16, jnp.float32))


def _eval_experts(chunk, hrow, grow):
    half = hrow.shape[1] // LANES // 2
    ones_bf16 = jnp.ones((LANES, LANES), jnp.bfloat16)
    eye = (lax.broadcasted_iota(jnp.int32, (PEER_SLOTS, LANES), 0)
           == lax.broadcasted_iota(jnp.int32, (PEER_SLOTS, LANES), 1))
    hpart = lambda c: hrow[:, c * LANES:(c + 1) * LANES]
    psum = None
    for c in range(half):
        hi, lo = _word_halves(chunk(c))
        p = hi * hpart(c) + lo * hpart(c + half)
        psum = p if psum is None else psum + p
    act = _gelu(_rowsum_bcast(psum, ones_bf16))
    gcol = _rowsum_bcast(jnp.where(eye, grow, 0.0), ones_bf16)
    coef = gcol * act
    outs_hi, outs_lo = [], []
    for c in range(half):
        hi, lo = _word_halves(chunk(c + half))
        outs_hi.append(jnp.sum(coef * hi, axis=0, keepdims=True))
        outs_lo.append(jnp.sum(coef * lo, axis=0, keepdims=True))
    return jnp.concatenate(outs_hi + outs_lo, axis=-1)


def _finish_block(x_ref, g2_ref, fg_ref, peer, out_ref):
    y = x_ref[...] + g2_ref[0] * peer[...]
    out_ref[...] = y * lax.rsqrt(jnp.mean(y * y, axis=-1, keepdims=True) + EPS) * fg_ref[...]


def _expert_kernel(idx_ref, gates_ref, h_ref, x_ref, g2_ref, fg_ref, tab_ref, out_ref,
                   *scratch, tok_block, n_slots):
    bufs = scratch[:n_slots]
    peer, sem = scratch[n_slots], scratch[n_slots + 1]
    d_model = h_ref.shape[-1]
    n_chunks = d_model // LANES
    rows_per_tok = PEER_SLOTS * n_chunks

    def issue(t, s):
        for r in range(PEER_SLOTS):
            row = pl.multiple_of(idx_ref[t, r] * n_chunks, n_chunks)
            pltpu.make_async_copy(tab_ref.at[pl.ds(row, n_chunks), :],
                                  bufs[s].at[pl.ds(r * n_chunks, n_chunks), :],
                                  sem.at[s]).start(priority=r % DMA_THREADS)

    def wait(s):
        pltpu.make_async_copy(tab_ref.at[pl.ds(0, rows_per_tok), :], bufs[s], sem.at[s]).wait()

    def compute(t, s):
        chunk = lambda c: bufs[s][pl.ds(c, PEER_SLOTS, stride=n_chunks), :]
        peer[pl.ds(t, 1), :] = _eval_experts(chunk, h_ref[pl.ds(t, 1), :], gates_ref[pl.ds(t, 1), :])

    def step(t, s, prefetch):
        wait(s)
        if prefetch:
            issue(t + n_slots - 1, (s - 1) % n_slots)
        compute(t, s)

    for t in range(n_slots - 1):
        issue(t, t)

    n_groups = tok_block // n_slots

    def group(g, carry):
        for s in range(n_slots):
            step(g * n_slots + s, s, True)
        return carry

    lax.fori_loop(0, n_groups - 1, group, 0)
    for s in range(n_slots):
        t = (n_groups - 1) * n_slots + s
        step(t, s, t + n_slots - 1 < tok_block)

    _finish_block(x_ref, g2_ref, fg_ref, peer, out_ref)


def _expert_cost(n_tok, d):
    pairs = n_tok * PEER_SLOTS
    return pl.CostEstimate(flops=4 * pairs * d, transcendentals=pairs,
                           bytes_accessed=4 * pairs * d + 12 * n_tok * d + 8 * pairs)


def _sc_tanh(y):
    return 1.0 - 2.0 / (jnp.exp(2.0 * y) + 1.0)


def _sc_peer_experts(table3, ids, gates, h3):
    n_tok, n_chunks, lanes = h3.shape
    info = plsc.get_sparse_core_info()
    sl = info.num_lanes
    n_workers = info.num_cores * info.num_subcores
    tok_per_worker = n_tok // n_workers
    assert tok_per_worker * n_workers == n_tok and tok_per_worker % 2 == 0
    rows = SC_GATHER_ROWS
    n_sub = PEER_SLOTS // rows
    assert n_sub % 2 == 0
    half = n_chunks // 2
    pieces = [(c, k * sl) for c in range(half) for k in range(lanes // sl)]
    mesh = plsc.VectorSubcoreMesh(core_axis_name="c", subcore_axis_name="s")
    buf = lambda dt: pltpu.VMEM((rows, n_chunks, lanes), dt)

    @functools.partial(
        pl.kernel, mesh=mesh,
        out_type=jax.ShapeDtypeStruct((n_tok, n_chunks, lanes), jnp.float32),
        scratch_types=[pltpu.VMEM((PEER_SLOTS,), jnp.int32), pltpu.VMEM((PEER_SLOTS,), jnp.int32),
                       pltpu.VMEM((PEER_SLOTS,), jnp.float32), pltpu.VMEM((PEER_SLOTS,), jnp.float32),
                       pltpu.VMEM((n_chunks, lanes), jnp.float32),
                       pltpu.VMEM((n_chunks, lanes), jnp.float32),
                       buf(jnp.uint32), buf(jnp.uint32), pltpu.VMEM((n_chunks, lanes), jnp.float32),
                       pltpu.SemaphoreType.DMA((2,)), pltpu.SemaphoreType.DMA((2,))],
        compiler_params=pltpu.CompilerParams(needs_layout_passes=False),
        cost_estimate=_expert_cost(n_tok, n_chunks * lanes),
        name="sc_peer_experts")
    def run(table_hbm, ids_hbm, gates_hbm, h_hbm, peer_hbm, ids0, ids1, g0, g1, h0, h1, rows0, rows1,
            out_v, gsem, isem):
        bufs = (rows0, rows1)
        ids_p, g_p, h_p = (ids0, ids1), (g0, g1), (h0, h1)
        t_first = (lax.axis_index("s") * info.num_cores + lax.axis_index("c")) * tok_per_worker

        def token_inputs(t, p):
            off = pl.multiple_of(t * PEER_SLOTS, PEER_SLOTS)
            return (pltpu.make_async_copy(ids_hbm.at[pl.ds(off, PEER_SLOTS)], ids_p[p], isem.at[p]),
                    pltpu.make_async_copy(gates_hbm.at[pl.ds(off, PEER_SLOTS)], g_p[p], isem.at[p]),
                    pltpu.make_async_copy(h_hbm.at[t], h_p[p], isem.at[p]))

        def gather(p, s):
            return pltpu.make_async_copy(table_hbm.at[ids_p[p].at[pl.ds(s * rows, rows)]],
                                         bufs[s % 2], gsem.at[s % 2])

        for cp in token_inputs(t_first, 0):
            cp.start()
        for cp in token_inputs(t_first, 0):
            cp.wait()
        gather(0, 0).start()

        def one_token(ti, p):
            t = t_first + ti
            g_v, h_v = g_p[p], h_p[p]
            has_next = ti + 1 < tok_per_worker

            @pl.when(has_next)
            def _():
                for cp in token_inputs(t + 1, 1 - p):
                    cp.start()

            for c, k in pieces:
                out_v[c, pl.ds(k, sl)] = jnp.zeros((sl,), jnp.float32)
                out_v[c + half, pl.ds(k, sl)] = jnp.zeros((sl,), jnp.float32)
            for s in range(n_sub):
                gather(p, s).wait()
                if s + 1 < n_sub:
                    gather(p, s + 1).start()
                else:
                    @pl.when(has_next)
                    def _():
                        for cp in token_inputs(t + 1, 1 - p):
                            cp.wait()
                        gather(1 - p, 0).start()
                rbuf = bufs[s % 2]

                @pl.loop(0, rows, step=SC_ROW_GROUP)
                def _(r0):
                    accs = [jnp.zeros((sl,), jnp.float32) for _ in range(SC_ROW_GROUP)]
                    for c, k in pieces:
                        h_hi, h_lo = h_v[c, pl.ds(k, sl)], h_v[c + half, pl.ds(k, sl)]
                        for j in range(SC_ROW_GROUP):
                            hi, lo = _word_halves(rbuf[r0 + j, c, pl.ds(k, sl)])
                            accs[j] = accs[j] + hi * h_hi + lo * h_lo
                    coefs = []
                    for j in range(SC_ROW_GROUP):
                        a = jnp.broadcast_to(jnp.sum(accs[j]), (sl,))
                        act = 0.5 * a * (1.0 + _sc_tanh(0.7978845608028654 * (a + 0.044715 * (a * a * a))))
                        gate = plsc.load_gather(g_v, [jnp.broadcast_to(s * rows + r0 + j, (sl,))])
                        coefs.append(gate * act)
                    for b0 in range(0, len(pieces), SC_STORE_BATCH):
                        tots = []
                        for c, k in pieces[b0:b0 + SC_STORE_BATCH]:
                            t_hi = t_lo = None
                            for j in range(SC_ROW_GROUP):
                                hi, lo = _word_halves(rbuf[r0 + j, c + half, pl.ds(k, sl)])
                                t_hi = coefs[j] * hi if t_hi is None else t_hi + coefs[j] * hi
                                t_lo = coefs[j] * lo if t_lo is None else t_lo + coefs[j] * lo
                            tots.append((t_hi, t_lo))
                        for (c, k), (t_hi, t_lo) in zip(pieces[b0:b0 + SC_STORE_BATCH], tots):
                            plsc.addupdate(out_v.at[c, pl.ds(k, sl)], t_hi)
                            plsc.addupdate(out_v.at[c + half, pl.ds(k, sl)], t_lo)

            pltpu.sync_copy(out_v, peer_hbm.at[t])

        @pl.loop(0, tok_per_worker, step=2)
        def _(ti):
            one_token(ti, 0)
            one_token(ti + 1, 1)

    return run(table3, ids, gates, h3)


def _finish_kernel(peer_ref, x_ref, g2_ref, fg_ref, *rest):
    _finish_block(x_ref, g2_ref, fg_ref, peer_ref, rest[-1])


def _finish_tokens(peer, x1, g2, fg, seq_len, x_tok0, out_tok0, n_out, prev=None):
    m, d = peer.shape
    tm = FINISH_TOK_BLOCK
    assert m % tm == 0 and x_tok0 % tm == 0 and out_tok0 % tm == 0 and seq_len % tm == 0
    xf, of = x_tok0 // tm, out_tok0 // tm
    extra = [] if prev is None else [prev]
    return pl.pallas_call(
        _finish_kernel,
        grid=(m // tm,),
        in_specs=[pl.BlockSpec((tm, d), lambda i: (i, 0)),
                  pl.BlockSpec((tm, d), lambda i: (i + xf, 0)),
                  pl.BlockSpec((1, 1, d), lambda i: ((i + of) * tm // seq_len, 0, 0)),
                  pl.BlockSpec((1, d), lambda i: (0, 0))]
        + [pl.BlockSpec(memory_space=pl.ANY) for _ in extra],
        out_specs=pl.BlockSpec((tm, d), lambda i: (i + of, 0)),
        out_shape=jax.ShapeDtypeStruct((n_out, d), jnp.float32),
        input_output_aliases={4: 0} if extra else {},
        compiler_params=pltpu.CompilerParams(dimension_semantics=("arbitrary",)),
        name="finish_tokens",
    )(peer, x1, g2, fg, *extra)


def _pack_expert_table(expert_u, expert_v):
    n_experts, d = expert_u.shape
    bits = lambda a: lax.bitcast_convert_type(a.astype(jnp.bfloat16), jnp.uint16).astype(jnp.uint32)
    pair = lambda b: (b[:, :d // 2] << 16) | b[:, d // 2:]
    words = jnp.concatenate([pair(bits(expert_u)), pair(bits(expert_v))], axis=1)
    return words.reshape(n_experts, d // LANES, LANES)


def _tc_peer_experts(idx, gates, h2, x1, g2, final_g, table, seq_len, first_tok, out_tok0, prev):
    m, d = h2.shape
    n_chunks = d // LANES
    tb = EXPERT_TOK_BLOCK
    assert tb % EXPERT_SLOTS == 0 and seq_len % tb == 0 and first_tok % tb == 0 and out_tok0 % tb == 0
    first, ofirst = first_tok // tb, out_tok0 // tb
    kern = functools.partial(_expert_kernel, tok_block=tb, n_slots=EXPERT_SLOTS)
    return pl.pallas_call(
        lambda *refs: kern(*refs[:7], *refs[8:]),
        grid=((m - first_tok) // tb,),
        in_specs=[
            pl.BlockSpec((tb, PEER_SLOTS), lambda i: (i + first, 0), memory_space=pltpu.SMEM),
            pl.BlockSpec((tb, PEER_SLOTS), lambda i: (i + first, 0)),
            pl.BlockSpec((tb, d), lambda i: (i + first, 0)),
            pl.BlockSpec((tb, d), lambda i: (i + first, 0)),
            pl.BlockSpec((1, 1, d), lambda i: ((i + ofirst) * tb // seq_len, 0, 0)),
            pl.BlockSpec((1, d), lambda i: (0, 0)),
            pl.BlockSpec(memory_space=pl.ANY),
            pl.BlockSpec(memory_space=pl.ANY),
        ],
        out_specs=pl.BlockSpec((tb, d), lambda i: (i + ofirst, 0)),
        out_shape=jax.ShapeDtypeStruct(prev.shape, prev.dtype),
        scratch_shapes=(
            [pltpu.VMEM((PEER_SLOTS * n_chunks, LANES), jnp.uint32) for _ in range(EXPERT_SLOTS)]
            + [pltpu.VMEM((tb, d), jnp.float32), pltpu.SemaphoreType.DMA((EXPERT_SLOTS,))]),
        input_output_aliases={7: 0},
        compiler_params=pltpu.CompilerParams(dimension_semantics=("arbitrary",)),
        cost_estimate=_expert_cost(m - first_tok, d),
        name="peer_experts",
    )(idx, gates, h2, x1, g2, final_g.reshape(1, d), table.reshape(-1, LANES), prev)


def kernel(x, c, w_mod, b_mod, norm1_g, w_in, b_forget, ln_v_g, w_spatial, b_spatial, w_branch_a, w_branch_b, w_out, norm2_g, w_query, sub_keys, expert_u, expert_v, final_g):
    B, S, D = x.shape
    n = B * S
    bf16 = jnp.bfloat16
    assert w_mod.shape[0] == 1, "the final RMSNorm is fused into the single layer's expert kernels"
    l = 0
    mod = _modulation(c, w_mod[l], b_mod[l]).reshape(B, 6, D)
    g2 = mod[:, 5:6, :]
    fg = final_g.reshape(1, D)
    table = _pack_expert_table(expert_u[l], expert_v[l])
    n_chunks = D // LANES

    qkv, cum, cumt, sga, gb = _input_projection(x, mod, norm1_g[l], w_in[l], b_forget[l], ln_v_g[l],
                                                w_spatial[l], b_spatial[l], w_branch_b[l])
    y_a = _fox_attention(qkv, cum, cumt, B, S)

    def route(tok0, n_tok):
        x1, h2, idx_t, gates_t = _post_attention(
            y_a, sga, gb, x.reshape(n, D), mod, norm2_g[l].reshape(1, D), w_branch_a[l].astype(bf16),
            w_out[l].astype(bf16), w_query[l].astype(bf16), sub_keys[l], S, tok0, n_tok)
        return x1, h2, idx_t.T, gates_t.T

    def sc_experts(h2, idx, gates, lo, hi):
        return _sc_peer_experts(table, idx[lo:hi].reshape(-1), gates[lo:hi].reshape(-1),
                                h2[lo:hi].reshape(hi - lo, n_chunks, LANES)).reshape(hi - lo, D)

    n_sc = n * SC_SHARE_PERCENT // 100 // SC_SHARE_ALIGN * SC_SHARE_ALIGN
    n_a = n * SC_EARLY_PERCENT // 100 // SC_SHARE_ALIGN * SC_SHARE_ALIGN if n_sc > 0 else 0
    n_b = n_sc - n_a
    out = None
    if n_a > 0:
        x1_a, h2_a, idx_a, gates_a = route(0, n_a)
        peer_a = sc_experts(h2_a, idx_a, gates_a, 0, n_a)
    x1_r, h2_r, idx_r, gates_r = route(n_a, n - n_a)
    if n_a > 0:
        out = _finish_tokens(peer_a, x1_a, g2, fg, S, 0, 0, n)
    if n_b > 0:
        peer_b = sc_experts(h2_r, idx_r, gates_r, 0, n_b)
    if out is None:
        out = jnp.zeros((n, D), jnp.float32)
    out = _tc_peer_experts(idx_r, gates_r, h2_r, x1_r, g2, final_g, table, S, n_b, n_a + n_b, out)
    if n_b > 0:
        out = _finish_tokens(peer_b, x1_r, g2, fg, S, 0, n_a, n, prev=out)
    return out.reshape(B, S, D)
```

```python
import functools

import jax
import jax.numpy as jnp
from jax import lax
from jax.experimental import pallas as pl
from jax.experimental.pallas import tpu as pltpu
from jax.experimental.pallas import tpu_sc as plsc

D_MODEL = 1024
ATT_HEADS = 8
ATT_HEAD_DIM = 64
ATT_WIDTH = ATT_HEADS * ATT_HEAD_DIM
GM_GROUPS = 4
GM_GROUP_DIM = 128
GM_WIDTH = GM_GROUPS * GM_GROUP_DIM
GM_CHUNK = 128
PEER_HEADS = 8
PEER_KEY_DIM = 256
PEER_HALF = PEER_KEY_DIM // 2
N_KEYS = 128
PEER_TOPK = 16
PEER_SLOTS = PEER_HEADS * PEER_TOPK
SPLIT_POINTS = (ATT_WIDTH, 2 * ATT_WIDTH, 3 * ATT_WIDTH, 3 * ATT_WIDTH + ATT_HEADS,
                3 * ATT_WIDTH + ATT_HEADS + 2 * GM_WIDTH,
                3 * ATT_WIDTH + ATT_HEADS + 2 * GM_WIDTH + D_MODEL)
EPS = 1e-6

LANES = 128
EXPERT_TOK_BLOCK = 256
EXPERT_SLOTS = 8
DMA_THREADS = 2
FINISH_TOK_BLOCK = 512
SC_GATHER_ROWS = 32
SC_ROW_GROUP = 4
SC_STORE_BATCH = 8
SC_SHARE_PERCENT = 48
SC_SHARE_ALIGN = 512


def _gelu(x):
    return 0.5 * x * (1.0 + jnp.tanh(0.7978845608028654 * (x + 0.044715 * (x * x * x))))


def _mod_kernel(c_ref, w_ref, b_ref, o_ref):
    c = c_ref[...]
    sc = c * jax.nn.sigmoid(c)
    o_ref[...] = jnp.dot(sc, w_ref[...], precision=lax.Precision.HIGHEST,
                         preferred_element_type=jnp.float32) + b_ref[...]


def _modulation(c, w_mod, b_mod):
    b, d = c.shape
    cols = w_mod.shape[1]
    return pl.pallas_call(
        _mod_kernel,
        grid=(cols // d,),
        in_specs=[pl.BlockSpec((b, d), lambda j: (0, 0)),
                  pl.BlockSpec((d, d), lambda j: (0, j)),
                  pl.BlockSpec((1, d), lambda j: (0, j))],
        out_specs=pl.BlockSpec((b, d), lambda j: (0, j)),
        out_shape=jax.ShapeDtypeStruct((b, cols), jnp.float32),
        name="modulation",
    )(c, w_mod, b_mod.reshape(1, cols))


INPROJ_TOK_BLOCK = 256


def _inproj_kernel(x_ref, mod_ref, n1g_ref, wqkv_ref, wf_ref, bf_ref, wz_ref, wg_ref, lng_ref,
                   wsp_ref, bsp_ref, wb_ref, qkv_ref, cum_ref, cumt_ref, sga_ref, gb_ref, carry):
    f32, bf16 = jnp.float32, jnp.bfloat16
    tm, d = x_ref.shape[1], x_ref.shape[2]
    x = x_ref[0]
    sh1 = mod_ref[0, 0:1, :]
    sc1 = mod_ref[0, 1:2, :]
    h = x * lax.rsqrt(jnp.mean(x * x, axis=-1, keepdims=True) + EPS) * n1g_ref[...]
    hb = (h * (1.0 + sc1) + sh1).astype(bf16)

    qkv = jnp.dot(hb, wqkv_ref[...], preferred_element_type=f32)
    qkv_ref[:, 0:ATT_WIDTH] = (qkv[:, 0:ATT_WIDTH] * (ATT_HEAD_DIM ** -0.5)).astype(bf16)
    qkv_ref[:, ATT_WIDTH:] = qkv[:, ATT_WIDTH:].astype(bf16)

    f = jnp.dot(hb, wf_ref[...], preferred_element_type=f32) + bf_ref[...]
    logf = jnp.minimum(f, 0.0) - jnp.log1p(jnp.exp(-jnp.abs(f)))

    @pl.when(pl.program_id(1) == 0)
    def _():
        carry[...] = jnp.zeros_like(carry)

    tri = (lax.broadcasted_iota(jnp.int32, (tm, tm), 0)
           >= lax.broadcasted_iota(jnp.int32, (tm, tm), 1)).astype(f32)
    cum = jnp.dot(tri, logf, precision=lax.Precision.HIGHEST, preferred_element_type=f32) + carry[...]
    cum_ref[...] = cum
    cumt_ref[0, 0] = jnp.transpose(cum)[0:ATT_HEADS, :]
    carry[...] = cum[tm - 1:tm, :]

    gz = _gelu(jnp.dot(hb, wz_ref[...], preferred_element_type=f32))
    u = gz[:, 0:GM_WIDTH]
    v = gz[:, GM_WIDTH:]
    mu = jnp.mean(v, axis=-1, keepdims=True)
    var = jnp.mean(jnp.square(v - mu), axis=-1, keepdims=True)
    vn = ((v - mu) * lax.rsqrt(var + EPS) * lng_ref[...]).astype(bf16)
    tril = (lax.broadcasted_iota(jnp.int32, (GM_CHUNK, GM_CHUNK), 0)
            >= lax.broadcasted_iota(jnp.int32, (GM_CHUNK, GM_CHUNK), 1))
    w_sp = [jnp.where(tril, wsp_ref[g], 0.0).astype(bf16) for g in range(GM_GROUPS)]
    rows = []
    for ck in range(tm // GM_CHUNK):
        r0 = ck * GM_CHUNK
        cols = []
        for g in range(GM_GROUPS):
            c0 = g * GM_GROUP_DIM
            mixed = jnp.dot(w_sp[g], vn[r0:r0 + GM_CHUNK, c0:c0 + GM_GROUP_DIM],
                            preferred_element_type=f32) + bsp_ref[g]
            cols.append(u[r0:r0 + GM_CHUNK, c0:c0 + GM_GROUP_DIM] * mixed)
        rows.append(jnp.concatenate(cols, axis=1))
    yb = jnp.concatenate(rows, axis=0).astype(bf16)
    ybp = jnp.dot(yb, wb_ref[...], preferred_element_type=f32)

    sg = jax.nn.sigmoid(jnp.dot(hb, wg_ref[...], preferred_element_type=f32))
    sga_ref[...] = sg[:, 0:d].astype(bf16)
    gb_ref[...] = (sg[:, d:] * ybp).astype(bf16)


def _input_projection(x, mod, norm1_g, w_in, b_forget, ln_v_g, w_spatial, b_spatial, w_branch_b):
    B, S, d = x.shape
    n = B * S
    tm = INPROJ_TOK_BLOCK
    bf16 = jnp.bfloat16
    p0, p1, p2, p3, p4, p5 = SPLIT_POINTS
    w_qkv = w_in[:, 0:p2].astype(bf16)
    w_f = jnp.pad(w_in[:, p2:p3], ((0, 0), (0, LANES - ATT_HEADS))).astype(bf16)
    b_f = jnp.pad(b_forget, (0, LANES - ATT_HEADS)).reshape(1, LANES)
    w_z = w_in[:, p3:p4].astype(bf16)
    w_g = w_in[:, p4:].astype(bf16)
    nt = S // tm
    tok = lambda w: pl.BlockSpec((tm, w), lambda b, i: (b * nt + i, 0))
    full = lambda a: pl.BlockSpec(a.shape, lambda b, i: (0,) * a.ndim)
    args = (x, mod, norm1_g.reshape(1, d), w_qkv, w_f, b_f, w_z, w_g, ln_v_g.reshape(1, GM_WIDTH),
            w_spatial, b_spatial.reshape(GM_GROUPS, GM_CHUNK, 1), w_branch_b.astype(bf16))
    return pl.pallas_call(
        _inproj_kernel,
        grid=(B, nt),
        in_specs=[pl.BlockSpec((1, tm, d), lambda b, i: (b, i, 0)),
                  pl.BlockSpec((1, 6, d), lambda b, i: (b, 0, 0))] + [full(a) for a in args[2:]],
        out_specs=[tok(3 * ATT_WIDTH), tok(LANES),
                   pl.BlockSpec((1, 1, ATT_HEADS, tm), lambda b, i: (b, i, 0, 0)), tok(d), tok(d)],
        out_shape=[jax.ShapeDtypeStruct((n, 3 * ATT_WIDTH), bf16),
                   jax.ShapeDtypeStruct((n, LANES), jnp.float32),
                   jax.ShapeDtypeStruct((B, nt, ATT_HEADS, tm), jnp.float32),
                   jax.ShapeDtypeStruct((n, d), bf16),
                   jax.ShapeDtypeStruct((n, d), bf16)],
        scratch_shapes=[pltpu.VMEM((1, LANES), jnp.float32)],
        compiler_params=pltpu.CompilerParams(dimension_semantics=("arbitrary", "arbitrary"),
                                             vmem_limit_bytes=56 * 1024 * 1024),
        name="input_projection",
    )(*args)


ATT_BLOCK = 512


def _fox_kernel(q_ref, k_ref, v_ref, cum_ref, cumt_ref, o_ref, *, blk, ratio):
    f32 = jnp.float32
    hp = pl.program_id(1)
    i = pl.program_id(2)
    dh = ATT_HEAD_DIM
    q2 = q_ref[...]
    first = lax.broadcasted_iota(jnp.int32, (1, 2 * dh), 1) < dh
    qs = (jnp.where(first, q2, jnp.zeros_like(q2)), jnp.where(first, jnp.zeros_like(q2), q2))
    cum_blk = cum_ref[...]
    lane = lax.broadcasted_iota(jnp.int32, cum_blk.shape, 1)
    cqs = [jnp.sum(jnp.where(lane == 2 * hp + e, cum_blk, 0.0), axis=1, keepdims=True)
           for e in range(2)]

    def block(j, carry, masked):
        off = pl.multiple_of(j * blk, blk)
        k2 = k_ref[0, pl.ds(off, blk), :]
        v2 = v_ref[0, pl.ds(off, blk), :]
        out = []
        for e in range(2):
            m, l, acc = carry[e]
            ck = jnp.concatenate([cumt_ref[0, j * ratio + a, pl.ds(2 * hp + e, 1), :]
                                  for a in range(ratio)], axis=1)
            s = lax.dot_general(qs[e], k2, (((1,), (1,)), ((), ())), preferred_element_type=f32)
            s = s + (cqs[e] - ck)
            if masked:
                causal = (lax.broadcasted_iota(jnp.int32, (blk, blk), 0)
                          >= lax.broadcasted_iota(jnp.int32, (blk, blk), 1))
                s = jnp.where(causal, s, -jnp.inf)
            m_new = jnp.maximum(m, jnp.max(s, axis=1, keepdims=True))
            alpha = jnp.exp(m - m_new)
            p = jnp.exp(s - m_new)
            l = alpha * l + jnp.sum(p, axis=1, keepdims=True)
            acc = alpha * acc + jnp.dot(p.astype(v2.dtype), v2, preferred_element_type=f32)
            out.append((m_new, l, acc))
        return tuple(out)

    one = (jnp.full((blk, 1), -1e30, f32), jnp.zeros((blk, 1), f32), jnp.zeros((blk, 2 * dh), f32))
    carry = lax.fori_loop(0, i, lambda j, c: block(j, c, False), (one, one))
    (_, l0, acc0), (_, l1, acc1) = block(i, carry, True)
    o_ref[...] = jnp.where(first, acc0 / l0, acc1 / l1).astype(o_ref.dtype)


def _fox_attention(qkv, cum, cumt, batch, seq_len):
    n = qkv.shape[0]
    tm = cumt.shape[-1]
    blk = max(min(ATT_BLOCK, seq_len), tm)
    nb = seq_len // blk
    pair = 2 * ATT_HEAD_DIM
    n_pairs = ATT_WIDTH // pair
    qkv3 = qkv.reshape(batch, seq_len, 3 * ATT_WIDTH)
    return pl.pallas_call(
        functools.partial(_fox_kernel, blk=blk, ratio=blk // tm),
        grid=(batch, n_pairs, nb),
        in_specs=[pl.BlockSpec((blk, pair), lambda b, hp, i: (b * nb + i, hp)),
                  pl.BlockSpec((1, seq_len, pair), lambda b, hp, i: (b, 0, n_pairs + hp)),
                  pl.BlockSpec((1, seq_len, pair), lambda b, hp, i: (b, 0, 2 * n_pairs + hp)),
                  pl.BlockSpec((blk, LANES), lambda b, hp, i: (b * nb + i, 0)),
                  pl.BlockSpec((1,) + cumt.shape[1:], lambda b, hp, i: (b, 0, 0, 0))],
        out_specs=pl.BlockSpec((blk, pair), lambda b, hp, i: (b * nb + i, hp)),
        out_shape=jax.ShapeDtypeStruct((n, ATT_WIDTH), jnp.bfloat16),
        compiler_params=pltpu.CompilerParams(
            dimension_semantics=("arbitrary", "arbitrary", "arbitrary")),
        name="fox_attention",
    )(qkv, qkv3, qkv3, cum, cumt)


def _topk_rows(s, k):
    rows, t = s.shape
    iota = lax.broadcasted_iota(jnp.int32, (rows, t), 0)
    slot = lax.broadcasted_iota(jnp.int32, (k, t), 0)
    vals = jnp.zeros((k, t), jnp.float32)
    ids = jnp.zeros((k, t), jnp.int32)
    for j in range(k):
        m = jnp.max(s, axis=0, keepdims=True)
        am = jnp.min(jnp.where(s == m, iota, rows), axis=0, keepdims=True)
        vals = jnp.where(slot == j, m, vals)
        ids = jnp.where(slot == j, am, ids)
        s = jnp.where(iota == am, -jnp.inf, s)
    return vals, ids


def _select_rows(table, pos):
    out = jnp.zeros(pos.shape, table.dtype)
    for r in range(table.shape[0]):
        out = jnp.where(pos == r, table[r:r + 1, :], out)
    return out


def _post_kernel(ya_ref, sga_ref, gb_ref, x_ref, mod_ref, n2g_ref, wa_ref, wo_ref, wq_ref, keys_ref,
                 x1_ref, h2_ref, idx_ref, gates_ref, q_scr):
    f32 = jnp.float32
    a = jnp.dot(ya_ref[...], wa_ref[...], preferred_element_type=f32)
    merged = sga_ref[...].astype(f32) * a + gb_ref[...].astype(f32)
    o = jnp.dot(merged.astype(jnp.bfloat16), wo_ref[...], preferred_element_type=f32)
    g1 = mod_ref[0, 2:3, :]
    sh2 = mod_ref[0, 3:4, :]
    sc2 = mod_ref[0, 4:5, :]
    x1 = x_ref[...] + g1 * o
    x1_ref[...] = x1
    h2 = x1 * lax.rsqrt(jnp.mean(x1 * x1, axis=-1, keepdims=True) + EPS) * n2g_ref[...]
    h2 = h2 * (1.0 + sc2) + sh2
    h2_ref[...] = h2
    qp = jnp.dot(h2.astype(jnp.bfloat16), wq_ref[...], preferred_element_type=f32)
    for j in range(2 * PEER_HEADS):
        q_scr[j] = qp[:, j * PEER_HALF:(j + 1) * PEER_HALF]

    def head(h, carry):
        tops = []
        for p in range(2):
            q = q_scr[2 * h + p]
            keys = keys_ref[2 * h + p]
            sc = lax.dot_general(keys, q, (((1,), (1,)), ((), ())),
                                 precision=lax.Precision.HIGHEST,
                                 preferred_element_type=f32)
            tops.append(_topk_rows(sc, PEER_TOPK))
        (s1, i1), (s2, i2) = tops
        half = PEER_TOPK // 2
        blocks = [s1[0:1, :] + s2]
        blocks += [s1[a:a + 1, :] + s2[0:half, :] for a in range(1, half)]
        blocks += [s1[half:, :] + s2[0:1, :]]
        vals, pos = _topk_rows(jnp.concatenate(blocks, axis=0), PEER_TOPK)
        mid = pos - PEER_TOPK
        tail0 = PEER_TOPK + (half - 1) * half
        ra = jnp.where(pos < PEER_TOPK, 0,
                       jnp.where(pos < tail0, 1 + (mid >> (half.bit_length() - 1)), pos - tail0 + half))
        rb = jnp.where(pos < PEER_TOPK, pos, jnp.where(pos < tail0, mid & (half - 1), 0))
        eid = _select_rows(i1, ra) * N_KEYS + _select_rows(i2, rb)
        e = jnp.exp(vals - vals[0:1, :])
        g = e / jnp.sum(e, axis=0, keepdims=True)
        row = pl.multiple_of(h * PEER_TOPK, PEER_TOPK)
        idx_ref[pl.ds(row, PEER_TOPK), :] = eid
        gates_ref[pl.ds(row, PEER_TOPK), :] = g
        return carry

    lax.fori_loop(0, PEER_HEADS, head, 0)


POST_TOK_BLOCK = 256


def _post_attention(ya, sga, gb, x, mod, norm2_g, w_a, w_out, w_query, sub_keys, seq_len):
    n, d = x.shape
    tm = POST_TOK_BLOCK
    blocks_per_seq = seq_len // tm
    aw = ya.shape[1]
    keys = sub_keys.reshape(2 * PEER_HEADS, N_KEYS, PEER_HALF)
    tok = lambda w: pl.BlockSpec((tm, w), lambda i: (i, 0))
    full = lambda a: pl.BlockSpec(a.shape, lambda i: (0,) * a.ndim)
    return pl.pallas_call(
        _post_kernel,
        grid=(n // tm,),
        in_specs=[tok(aw), tok(d), tok(d), tok(d),
                  pl.BlockSpec((1, 6, d), lambda i: (i // blocks_per_seq, 0, 0)),
                  full(norm2_g), full(w_a), full(w_out), full(w_query), full(keys)],
        out_specs=[tok(d), tok(d),
                   pl.BlockSpec((PEER_SLOTS, tm), lambda i: (0, i)),
                   pl.BlockSpec((PEER_SLOTS, tm), lambda i: (0, i))],
        out_shape=[jax.ShapeDtypeStruct((n, d), jnp.float32),
                   jax.ShapeDtypeStruct((n, d), jnp.float32),
                   jax.ShapeDtypeStruct((PEER_SLOTS, n), jnp.int32),
                   jax.ShapeDtypeStruct((PEER_SLOTS, n), jnp.float32)],
        scratch_shapes=[pltpu.VMEM((2 * PEER_HEADS, tm, PEER_HALF), jnp.float32)],
        compiler_params=pltpu.CompilerParams(dimension_semantics=("arbitrary",),
                                             vmem_limit_bytes=48 * 1024 * 1024),
        name="post_attention",
    )(ya, sga, gb, x, mod, norm2_g, w_a, w_out, w_query, keys)


def _rowsum_bcast(p, ones_bf16):
    hi = p.astype(jnp.bfloat16)
    lo = (p - hi.astype(jnp.float32)).astype(jnp.bfloat16)
    return (jnp.dot(hi, ones_bf16, preferred_element_type=jnp.float32)
            + jnp.dot(lo, ones_bf16, preferred_element_type=jnp.float32))


def _word_halves(w):
    return (lax.bitcast_convert_type(w & jnp.uint32(0xFFFF0000), jnp.float32),
            lax.bitcast_convert_type(w << 16, jnp.float32))


def _eval_experts(chunk, hrow, grow):
    half = hrow.shape[1] // LANES // 2
    ones_bf16 = jnp.ones((LANES, LANES), jnp.bfloat16)
    eye = (lax.broadcasted_iota(jnp.int32, (PEER_SLOTS, LANES), 0)
           == lax.broadcasted_iota(jnp.int32, (PEER_SLOTS, LANES), 1))
    hpart = lambda c: hrow[:, c * LANES:(c + 1) * LANES]
    psum = None
    for c in range(half):
        hi, lo = _word_halves(chunk(c))
        p = hi * hpart(c) + lo * hpart(c + half)
        psum = p if psum is None else psum + p
    act = _gelu(_rowsum_bcast(psum, ones_bf16))
    gcol = _rowsum_bcast(jnp.where(eye, grow, 0.0), ones_bf16)
    coef = gcol * act
    outs_hi, outs_lo = [], []
    for c in range(half):
        hi, lo = _word_halves(chunk(c + half))
        outs_hi.append(jnp.sum(coef * hi, axis=0, keepdims=True))
        outs_lo.append(jnp.sum(coef * lo, axis=0, keepdims=True))
    return jnp.concatenate(outs_hi + outs_lo, axis=-1)


def _finish_block(x_ref, g2_ref, fg_ref, peer, out_ref):
    y = x_ref[...] + g2_ref[0] * peer[...]
    out_ref[...] = y * lax.rsqrt(jnp.mean(y * y, axis=-1, keepdims=True) + EPS) * fg_ref[...]


def _expert_kernel(idx_ref, gates_ref, h_ref, x_ref, g2_ref, fg_ref, tab_ref, out_ref,
                   *scratch, tok_block, n_slots):
    bufs = scratch[:n_slots]
    peer, sem = scratch[n_slots], scratch[n_slots + 1]
    d_model = h_ref.shape[-1]
    n_chunks = d_model // LANES
    rows_per_tok = PEER_SLOTS * n_chunks

    def issue(t, s):
        for r in range(PEER_SLOTS):
            row = pl.multiple_of(idx_ref[t, r] * n_chunks, n_chunks)
            pltpu.make_async_copy(tab_ref.at[pl.ds(row, n_chunks), :],
                                  bufs[s].at[pl.ds(r * n_chunks, n_chunks), :],
                                  sem.at[s]).start(priority=r % DMA_THREADS)

    def wait(s):
        pltpu.make_async_copy(tab_ref.at[pl.ds(0, rows_per_tok), :], bufs[s], sem.at[s]).wait()

    def compute(t, s):
        chunk = lambda c: bufs[s][pl.ds(c, PEER_SLOTS, stride=n_chunks), :]
        peer[pl.ds(t, 1), :] = _eval_experts(chunk, h_ref[pl.ds(t, 1), :], gates_ref[pl.ds(t, 1), :])

    def step(t, s, prefetch):
        wait(s)
        if prefetch:
            issue(t + n_slots - 1, (s - 1) % n_slots)
        compute(t, s)

    for t in range(n_slots - 1):
        issue(t, t)

    n_groups = tok_block // n_slots

    def group(g, carry):
        for s in range(n_slots):
            step(g * n_slots + s, s, True)
        return carry

    lax.fori_loop(0, n_groups - 1, group, 0)
    for s in range(n_slots):
        t = (n_groups - 1) * n_slots + s
        step(t, s, t + n_slots - 1 < tok_block)

    _finish_block(x_ref, g2_ref, fg_ref, peer, out_ref)


def _expert_cost(n_tok, d):
    pairs = n_tok * PEER_SLOTS
    return pl.CostEstimate(flops=4 * pairs * d, transcendentals=pairs,
                           bytes_accessed=4 * pairs * d + 12 * n_tok * d + 8 * pairs)


def _sc_tanh(y):
    return 1.0 - 2.0 / (jnp.exp(2.0 * y) + 1.0)


def _sc_peer_experts(table3, ids, gates, h3):
    n_tok, n_chunks, lanes = h3.shape
    info = plsc.get_sparse_core_info()
    sl = info.num_lanes
    n_workers = info.num_cores * info.num_subcores
    tok_per_worker = n_tok // n_workers
    assert tok_per_worker * n_workers == n_tok and tok_per_worker % 2 == 0
    rows = SC_GATHER_ROWS
    n_sub = PEER_SLOTS // rows
    assert n_sub % 2 == 0
    half = n_chunks // 2
    pieces = [(c, k * sl) for c in range(half) for k in range(lanes // sl)]
    mesh = plsc.VectorSubcoreMesh(core_axis_name="c", subcore_axis_name="s")
    buf = lambda dt: pltpu.VMEM((rows, n_chunks, lanes), dt)

    @functools.partial(
        pl.kernel, mesh=mesh,
        out_type=jax.ShapeDtypeStruct((n_tok, n_chunks, lanes), jnp.float32),
        scratch_types=[pltpu.VMEM((PEER_SLOTS,), jnp.int32), pltpu.VMEM((PEER_SLOTS,), jnp.int32),
                       pltpu.VMEM((PEER_SLOTS,), jnp.float32), pltpu.VMEM((PEER_SLOTS,), jnp.float32),
                       pltpu.VMEM((n_chunks, lanes), jnp.float32),
                       pltpu.VMEM((n_chunks, lanes), jnp.float32),
                       buf(jnp.uint32), buf(jnp.uint32), pltpu.VMEM((n_chunks, lanes), jnp.float32),
                       pltpu.SemaphoreType.DMA((2,)), pltpu.SemaphoreType.DMA((2,))],
        compiler_params=pltpu.CompilerParams(needs_layout_passes=False),
        cost_estimate=_expert_cost(n_tok, n_chunks * lanes),
        name="sc_peer_experts")
    def run(table_hbm, ids_hbm, gates_hbm, h_hbm, peer_hbm, ids0, ids1, g0, g1, h0, h1, rows0, rows1,
            out_v, gsem, isem):
        bufs = (rows0, rows1)
        ids_p, g_p, h_p = (ids0, ids1), (g0, g1), (h0, h1)
        t_first = (lax.axis_index("s") * info.num_cores + lax.axis_index("c")) * tok_per_worker

        def token_inputs(t, p):
            off = pl.multiple_of(t * PEER_SLOTS, PEER_SLOTS)
            return (pltpu.make_async_copy(ids_hbm.at[pl.ds(off, PEER_SLOTS)], ids_p[p], isem.at[p]),
                    pltpu.make_async_copy(gates_hbm.at[pl.ds(off, PEER_SLOTS)], g_p[p], isem.at[p]),
                    pltpu.make_async_copy(h_hbm.at[t], h_p[p], isem.at[p]))

        def gather(p, s):
            return pltpu.make_async_copy(table_hbm.at[ids_p[p].at[pl.ds(s * rows, rows)]],
                                         bufs[s % 2], gsem.at[s % 2])

        for cp in token_inputs(t_first, 0):
            cp.start()
        for cp in token_inputs(t_first, 0):
            cp.wait()
        gather(0, 0).start()

        def one_token(ti, p):
            t = t_first + ti
            g_v, h_v = g_p[p], h_p[p]
            has_next = ti + 1 < tok_per_worker

            @pl.when(has_next)
            def _():
                for cp in token_inputs(t + 1, 1 - p):
                    cp.start()

            for c, k in pieces:
                out_v[c, pl.ds(k, sl)] = jnp.zeros((sl,), jnp.float32)
                out_v[c + half, pl.ds(k, sl)] = jnp.zeros((sl,), jnp.float32)
            for s in range(n_sub):
                gather(p, s).wait()
                if s + 1 < n_sub:
                    gather(p, s + 1).start()
                else:
                    @pl.when(has_next)
                    def _():
                        for cp in token_inputs(t + 1, 1 - p):
                            cp.wait()
                        gather(1 - p, 0).start()
                rbuf = bufs[s % 2]

                @pl.loop(0, rows, step=SC_ROW_GROUP)
                def _(r0):
                    accs = [jnp.zeros((sl,), jnp.float32) for _ in range(SC_ROW_GROUP)]
                    for c, k in pieces:
                        h_hi, h_lo = h_v[c, pl.ds(k, sl)], h_v[c + half, pl.ds(k, sl)]
                        for j in range(SC_ROW_GROUP):
                            hi, lo = _word_halves(rbuf[r0 + j, c, pl.ds(k, sl)])
                            accs[j] = accs[j] + hi * h_hi + lo * h_lo
                    coefs = []
                    for j in range(SC_ROW_GROUP):
                        a = jnp.broadcast_to(jnp.sum(accs[j]), (sl,))
                        act = 0.5 * a * (1.0 + _sc_tanh(0.7978845608028654 * (a + 0.044715 * (a * a * a))))
                        gate = plsc.load_gather(g_v, [jnp.broadcast_to(s * rows + r0 + j, (sl,))])
                        coefs.append(gate * act)
                    for b0 in range(0, len(pieces), SC_STORE_BATCH):
                        tots = []
                        for c, k in pieces[b0:b0 + SC_STORE_BATCH]:
                            t_hi = t_lo = None
                            for j in range(SC_ROW_GROUP):
                                hi, lo = _word_halves(rbuf[r0 + j, c + half, pl.ds(k, sl)])
                                t_hi = coefs[j] * hi if t_hi is None else t_hi + coefs[j] * hi
                                t_lo = coefs[j] * lo if t_lo is None else t_lo + coefs[j] * lo
                            tots.append((t_hi, t_lo))
                        for (c, k), (t_hi, t_lo) in zip(pieces[b0:b0 + SC_STORE_BATCH], tots):
                            plsc.addupdate(out_v.at[c, pl.ds(k, sl)], t_hi)
                            plsc.addupdate(out_v.at[c + half, pl.ds(k, sl)], t_lo)

            pltpu.sync_copy(out_v, peer_hbm.at[t])

        @pl.loop(0, tok_per_worker, step=2)
        def _(ti):
            one_token(ti, 0)
            one_token(ti + 1, 1)

    return run(table3, ids, gates, h3)


def _finish_kernel(peer_ref, x_ref, g2_ref, fg_ref, prev_ref, out_ref):
    del prev_ref
    _finish_block(x_ref, g2_ref, fg_ref, peer_ref, out_ref)


def _finish_tokens(peer, x1, g2, fg, prev, seq_len):
    d = x1.shape[1]
    tm = FINISH_TOK_BLOCK
    return pl.pallas_call(
        _finish_kernel,
        grid=(peer.shape[0] // tm,),
        in_specs=[pl.BlockSpec((tm, d), lambda i: (i, 0)),
                  pl.BlockSpec((tm, d), lambda i: (i, 0)),
                  pl.BlockSpec((1, 1, d), lambda i: (i * tm // seq_len, 0, 0)),
                  pl.BlockSpec((1, d), lambda i: (0, 0)),
                  pl.BlockSpec(memory_space=pl.ANY)],
        out_specs=pl.BlockSpec((tm, d), lambda i: (i, 0)),
        out_shape=jax.ShapeDtypeStruct(prev.shape, prev.dtype),
        input_output_aliases={4: 0},
        compiler_params=pltpu.CompilerParams(dimension_semantics=("arbitrary",)),
        name="finish_tokens",
    )(peer, x1, g2, fg, prev)


def _pack_expert_table(expert_u, expert_v):
    n_experts, d = expert_u.shape
    bits = lambda a: lax.bitcast_convert_type(a.astype(jnp.bfloat16), jnp.uint16).astype(jnp.uint32)
    pair = lambda b: (b[:, :d // 2] << 16) | b[:, d // 2:]
    words = jnp.concatenate([pair(bits(expert_u)), pair(bits(expert_v))], axis=1)
    return words.reshape(n_experts, d // LANES, LANES)


def _tc_peer_experts(idx, gates, h2, x1, g2, final_g, table, seq_len, first_tok):
    n, d = h2.shape
    m = n - first_tok
    n_chunks = d // LANES
    tb = EXPERT_TOK_BLOCK
    assert tb % EXPERT_SLOTS == 0 and seq_len % tb == 0 and first_tok % tb == 0
    first = first_tok // tb
    return pl.pallas_call(
        functools.partial(_expert_kernel, tok_block=tb, n_slots=EXPERT_SLOTS),
        grid=(m // tb,),
        in_specs=[
            pl.BlockSpec((tb, PEER_SLOTS), lambda i: (i + first, 0), memory_space=pltpu.SMEM),
            pl.BlockSpec((tb, PEER_SLOTS), lambda i: (i + first, 0)),
            pl.BlockSpec((tb, d), lambda i: (i + first, 0)),
            pl.BlockSpec((tb, d), lambda i: (i + first, 0)),
            pl.BlockSpec((1, 1, d), lambda i: ((i + first) * tb // seq_len, 0, 0)),
            pl.BlockSpec((1, d), lambda i: (0, 0)),
            pl.BlockSpec(memory_space=pl.ANY),
        ],
        out_specs=pl.BlockSpec((tb, d), lambda i: (i + first, 0)),
        out_shape=jax.ShapeDtypeStruct((n, d), jnp.float32),
        scratch_shapes=(
            [pltpu.VMEM((PEER_SLOTS * n_chunks, LANES), jnp.uint32) for _ in range(EXPERT_SLOTS)]
            + [pltpu.VMEM((tb, d), jnp.float32), pltpu.SemaphoreType.DMA((EXPERT_SLOTS,))]),
        compiler_params=pltpu.CompilerParams(dimension_semantics=("arbitrary",)),
        cost_estimate=_expert_cost(m, d),
        name="peer_experts",
    )(idx, gates, h2, x1, g2, final_g.reshape(1, d), table.reshape(-1, LANES))


def _mix_and_route(x, mod, norm1_g, w_in, b_forget, ln_v_g, w_spatial, b_spatial, w_branch_a,
                   w_branch_b, w_out, norm2_g, w_query, sub_keys):
    B, S, D = x.shape
    n = B * S
    bf16 = jnp.bfloat16
    qkv, cum, cumt, sga, gb = _input_projection(x, mod, norm1_g, w_in, b_forget, ln_v_g,
                                                w_spatial, b_spatial, w_branch_b)
    y_a = _fox_attention(qkv, cum, cumt, B, S)
    x1, h2, idx_t, gates_t = _post_attention(
        y_a, sga, gb, x.reshape(n, D), mod, norm2_g.reshape(1, D),
        w_branch_a.astype(bf16), w_out.astype(bf16), w_query.astype(bf16), sub_keys, S)
    return x1, h2, idx_t.T, gates_t.T


def kernel(x, c, w_mod, b_mod, norm1_g, w_in, b_forget, ln_v_g, w_spatial, b_spatial, w_branch_a, w_branch_b, w_out, norm2_g, w_query, sub_keys, expert_u, expert_v, final_g):
    B, S, D = x.shape
    n = B * S
    assert w_mod.shape[0] == 1, "the final RMSNorm is fused into the single layer's expert kernels"
    l = 0
    mod = _modulation(c, w_mod[l], b_mod[l]).reshape(B, 6, D)
    table = _pack_expert_table(expert_u[l], expert_v[l])
    n_chunks = D // LANES
    weights = (norm1_g[l], w_in[l], b_forget[l], ln_v_g[l], w_spatial[l], b_spatial[l],
               w_branch_a[l], w_branch_b[l], w_out[l], norm2_g[l], w_query[l], sub_keys[l])

    x1, h2, idx, gates = _mix_and_route(x, mod, *weights)
    g2 = mod[:, 5:6, :]

    n_sc = n * SC_SHARE_PERCENT // 100 // SC_SHARE_ALIGN * SC_SHARE_ALIGN
    out = _tc_peer_experts(idx, gates, h2, x1, g2, final_g, table, S, n_sc)
    if n_sc > 0:
        peer_sc = _sc_peer_experts(table, idx[:n_sc].reshape(-1), gates[:n_sc].reshape(-1),
                                   h2[:n_sc].reshape(n_sc, n_chunks, LANES))
        out = _finish_tokens(peer_sc.reshape(n_sc, D), x1, g2, final_g.reshape(1, D), out, S)
    return out.reshape(B, S, D)
```

```python
import functools

import jax
import jax.numpy as jnp
from jax import lax
from jax.experimental import pallas as pl
from jax.experimental.pallas import tpu as pltpu
from jax.experimental.pallas import tpu_sc as plsc

D_MODEL = 1024
ATT_HEADS = 8
ATT_HEAD_DIM = 64
ATT_WIDTH = ATT_HEADS * ATT_HEAD_DIM
GM_GROUPS = 4
GM_GROUP_DIM = 128
GM_WIDTH = GM_GROUPS * GM_GROUP_DIM
GM_CHUNK = 128
PEER_HEADS = 8
PEER_KEY_DIM = 256
PEER_HALF = PEER_KEY_DIM // 2
N_KEYS = 128
PEER_TOPK = 16
PEER_SLOTS = PEER_HEADS * PEER_TOPK
SPLIT_POINTS = (ATT_WIDTH, 2 * ATT_WIDTH, 3 * ATT_WIDTH, 3 * ATT_WIDTH + ATT_HEADS,
                3 * ATT_WIDTH + ATT_HEADS + 2 * GM_WIDTH,
                3 * ATT_WIDTH + ATT_HEADS + 2 * GM_WIDTH + D_MODEL)
EPS = 1e-6

LANES = 128
EXPERT_TOK_BLOCK = 256
EXPERT_SLOTS = 8
DMA_THREADS = 2
FINISH_TOK_BLOCK = 512
SC_GATHER_ROWS = 32
SC_ROW_GROUP = 4
SC_STORE_BATCH = 8
SC_SHARE_PERCENT = 48
SC_SHARE_ALIGN = 512


def _gelu(x):
    return 0.5 * x * (1.0 + jnp.tanh(0.7978845608028654 * (x + 0.044715 * (x * x * x))))


def _mod_kernel(c_ref, w_ref, b_ref, o_ref):
    c = c_ref[...]
    sc = c * jax.nn.sigmoid(c)
    o_ref[...] = jnp.dot(sc, w_ref[...], precision=lax.Precision.HIGHEST,
                         preferred_element_type=jnp.float32) + b_ref[...]


def _modulation(c, w_mod, b_mod):
    b, d = c.shape
    cols = w_mod.shape[1]
    return pl.pallas_call(
        _mod_kernel,
        grid=(cols // d,),
        in_specs=[pl.BlockSpec((b, d), lambda j: (0, 0)),
                  pl.BlockSpec((d, d), lambda j: (0, j)),
                  pl.BlockSpec((1, d), lambda j: (0, j))],
        out_specs=pl.BlockSpec((b, d), lambda j: (0, j)),
        out_shape=jax.ShapeDtypeStruct((b, cols), jnp.float32),
        name="modulation",
    )(c, w_mod, b_mod.reshape(1, cols))


INPROJ_TOK_BLOCK = 512


def _inproj_kernel(x_ref, mod_ref, n1g_ref, wqkv_ref, wf_ref, bf_ref, wz_ref, wg_ref, lng_ref,
                   wsp_ref, bsp_ref, wb_ref, qkv_ref, cum_ref, cumt_ref, sga_ref, gb_ref, carry):
    f32, bf16 = jnp.float32, jnp.bfloat16
    tm, d = x_ref.shape[1], x_ref.shape[2]
    x = x_ref[0]
    sh1 = mod_ref[0, 0:1, :]
    sc1 = mod_ref[0, 1:2, :]
    h = x * lax.rsqrt(jnp.mean(x * x, axis=-1, keepdims=True) + EPS) * n1g_ref[...]
    hb = (h * (1.0 + sc1) + sh1).astype(bf16)

    qkv = jnp.dot(hb, wqkv_ref[...], preferred_element_type=f32)
    qkv_ref[:, 0:ATT_WIDTH] = (qkv[:, 0:ATT_WIDTH] * (ATT_HEAD_DIM ** -0.5)).astype(bf16)
    qkv_ref[:, ATT_WIDTH:] = qkv[:, ATT_WIDTH:].astype(bf16)

    f = jnp.dot(hb, wf_ref[...], preferred_element_type=f32) + bf_ref[...]
    logf = jnp.minimum(f, 0.0) - jnp.log1p(jnp.exp(-jnp.abs(f)))

    @pl.when(pl.program_id(1) == 0)
    def _():
        carry[...] = jnp.zeros_like(carry)

    tri = (lax.broadcasted_iota(jnp.int32, (tm, tm), 0)
           >= lax.broadcasted_iota(jnp.int32, (tm, tm), 1)).astype(f32)
    cum = jnp.dot(tri, logf, precision=lax.Precision.HIGHEST, preferred_element_type=f32) + carry[...]
    cum_ref[...] = cum
    cumt_ref[0, 0] = jnp.transpose(cum)[0:ATT_HEADS, :]
    carry[...] = cum[tm - 1:tm, :]

    gz = _gelu(jnp.dot(hb, wz_ref[...], preferred_element_type=f32))
    u = gz[:, 0:GM_WIDTH]
    v = gz[:, GM_WIDTH:]
    mu = jnp.mean(v, axis=-1, keepdims=True)
    var = jnp.mean(jnp.square(v - mu), axis=-1, keepdims=True)
    vn = ((v - mu) * lax.rsqrt(var + EPS) * lng_ref[...]).astype(bf16)
    tril = (lax.broadcasted_iota(jnp.int32, (GM_CHUNK, GM_CHUNK), 0)
            >= lax.broadcasted_iota(jnp.int32, (GM_CHUNK, GM_CHUNK), 1))
    w_sp = [jnp.where(tril, wsp_ref[g], 0.0).astype(bf16) for g in range(GM_GROUPS)]
    rows = []
    for ck in range(tm // GM_CHUNK):
        r0 = ck * GM_CHUNK
        cols = []
        for g in range(GM_GROUPS):
            c0 = g * GM_GROUP_DIM
            mixed = jnp.dot(w_sp[g], vn[r0:r0 + GM_CHUNK, c0:c0 + GM_GROUP_DIM],
                            preferred_element_type=f32) + bsp_ref[g]
            cols.append(u[r0:r0 + GM_CHUNK, c0:c0 + GM_GROUP_DIM] * mixed)
        rows.append(jnp.concatenate(cols, axis=1))
    yb = jnp.concatenate(rows, axis=0).astype(bf16)
    ybp = jnp.dot(yb, wb_ref[...], preferred_element_type=f32)

    sg = jax.nn.sigmoid(jnp.dot(hb, wg_ref[...], preferred_element_type=f32))
    sga_ref[...] = sg[:, 0:d].astype(bf16)
    gb_ref[...] = (sg[:, d:] * ybp).astype(bf16)


def _input_projection(x, mod, norm1_g, w_in, b_forget, ln_v_g, w_spatial, b_spatial, w_branch_b):
    B, S, d = x.shape
    n = B * S
    tm = INPROJ_TOK_BLOCK
    bf16 = jnp.bfloat16
    p0, p1, p2, p3, p4, p5 = SPLIT_POINTS
    w_qkv = w_in[:, 0:p2].astype(bf16)
    w_f = jnp.pad(w_in[:, p2:p3], ((0, 0), (0, LANES - ATT_HEADS))).astype(bf16)
    b_f = jnp.pad(b_forget, (0, LANES - ATT_HEADS)).reshape(1, LANES)
    w_z = w_in[:, p3:p4].astype(bf16)
    w_g = w_in[:, p4:].astype(bf16)
    nt = S // tm
    tok = lambda w: pl.BlockSpec((tm, w), lambda b, i: (b * nt + i, 0))
    full = lambda a: pl.BlockSpec(a.shape, lambda b, i: (0,) * a.ndim)
    args = (x, mod, norm1_g.reshape(1, d), w_qkv, w_f, b_f, w_z, w_g, ln_v_g.reshape(1, GM_WIDTH),
            w_spatial, b_spatial.reshape(GM_GROUPS, GM_CHUNK, 1), w_branch_b.astype(bf16))
    return pl.pallas_call(
        _inproj_kernel,
        grid=(B, nt),
        in_specs=[pl.BlockSpec((1, tm, d), lambda b, i: (b, i, 0)),
                  pl.BlockSpec((1, 6, d), lambda b, i: (b, 0, 0))] + [full(a) for a in args[2:]],
        out_specs=[tok(3 * ATT_WIDTH), tok(LANES),
                   pl.BlockSpec((1, 1, ATT_HEADS, tm), lambda b, i: (b, i, 0, 0)), tok(d), tok(d)],
        out_shape=[jax.ShapeDtypeStruct((n, 3 * ATT_WIDTH), bf16),
                   jax.ShapeDtypeStruct((n, LANES), jnp.float32),
                   jax.ShapeDtypeStruct((B, nt, ATT_HEADS, tm), jnp.float32),
                   jax.ShapeDtypeStruct((n, d), bf16),
                   jax.ShapeDtypeStruct((n, d), bf16)],
        scratch_shapes=[pltpu.VMEM((1, LANES), jnp.float32)],
        compiler_params=pltpu.CompilerParams(dimension_semantics=("arbitrary", "arbitrary"),
                                             vmem_limit_bytes=56 * 1024 * 1024),
        name="input_projection",
    )(*args)


ATT_BLOCK = 512


def _fox_kernel(q_ref, k_ref, v_ref, cum_ref, cumt_ref, o_ref, *, blk, ratio):
    f32 = jnp.float32
    hp = pl.program_id(1)
    i = pl.program_id(2)
    dh = ATT_HEAD_DIM
    q2 = q_ref[...]
    first = lax.broadcasted_iota(jnp.int32, (1, 2 * dh), 1) < dh
    qs = (jnp.where(first, q2, jnp.zeros_like(q2)), jnp.where(first, jnp.zeros_like(q2), q2))
    cum_blk = cum_ref[...]
    lane = lax.broadcasted_iota(jnp.int32, cum_blk.shape, 1)
    cqs = [jnp.sum(jnp.where(lane == 2 * hp + e, cum_blk, 0.0), axis=1, keepdims=True)
           for e in range(2)]

    def block(j, carry, masked):
        off = pl.multiple_of(j * blk, blk)
        k2 = k_ref[0, pl.ds(off, blk), :]
        v2 = v_ref[0, pl.ds(off, blk), :]
        out = []
        for e in range(2):
            m, l, acc = carry[e]
            ck = jnp.concatenate([cumt_ref[0, j * ratio + a, pl.ds(2 * hp + e, 1), :]
                                  for a in range(ratio)], axis=1)
            s = lax.dot_general(qs[e], k2, (((1,), (1,)), ((), ())), preferred_element_type=f32)
            s = s + (cqs[e] - ck)
            if masked:
                causal = (lax.broadcasted_iota(jnp.int32, (blk, blk), 0)
                          >= lax.broadcasted_iota(jnp.int32, (blk, blk), 1))
                s = jnp.where(causal, s, -jnp.inf)
            m_new = jnp.maximum(m, jnp.max(s, axis=1, keepdims=True))
            alpha = jnp.exp(m - m_new)
            p = jnp.exp(s - m_new)
            l = alpha * l + jnp.sum(p, axis=1, keepdims=True)
            acc = alpha * acc + jnp.dot(p.astype(v2.dtype), v2, preferred_element_type=f32)
            out.append((m_new, l, acc))
        return tuple(out)

    one = (jnp.full((blk, 1), -1e30, f32), jnp.zeros((blk, 1), f32), jnp.zeros((blk, 2 * dh), f32))
    carry = lax.fori_loop(0, i, lambda j, c: block(j, c, False), (one, one))
    (_, l0, acc0), (_, l1, acc1) = block(i, carry, True)
    o_ref[...] = jnp.where(first, acc0 / l0, acc1 / l1).astype(o_ref.dtype)


def _fox_attention(qkv, cum, cumt, batch, seq_len):
    n = qkv.shape[0]
    tm = cumt.shape[-1]
    blk = max(min(ATT_BLOCK, seq_len), tm)
    nb = seq_len // blk
    pair = 2 * ATT_HEAD_DIM
    n_pairs = ATT_WIDTH // pair
    qkv3 = qkv.reshape(batch, seq_len, 3 * ATT_WIDTH)
    return pl.pallas_call(
        functools.partial(_fox_kernel, blk=blk, ratio=blk // tm),
        grid=(batch, n_pairs, nb),
        in_specs=[pl.BlockSpec((blk, pair), lambda b, hp, i: (b * nb + i, hp)),
                  pl.BlockSpec((1, seq_len, pair), lambda b, hp, i: (b, 0, n_pairs + hp)),
                  pl.BlockSpec((1, seq_len, pair), lambda b, hp, i: (b, 0, 2 * n_pairs + hp)),
                  pl.BlockSpec((blk, LANES), lambda b, hp, i: (b * nb + i, 0)),
                  pl.BlockSpec((1,) + cumt.shape[1:], lambda b, hp, i: (b, 0, 0, 0))],
        out_specs=pl.BlockSpec((blk, pair), lambda b, hp, i: (b * nb + i, hp)),
        out_shape=jax.ShapeDtypeStruct((n, ATT_WIDTH), jnp.bfloat16),
        compiler_params=pltpu.CompilerParams(
            dimension_semantics=("arbitrary", "arbitrary", "arbitrary")),
        name="fox_attention",
    )(qkv, qkv3, qkv3, cum, cumt)


def _topk_rows(s, k):
    rows, t = s.shape
    iota = lax.broadcasted_iota(jnp.int32, (rows, t), 0)
    slot = lax.broadcasted_iota(jnp.int32, (k, t), 0)
    vals = jnp.zeros((k, t), jnp.float32)
    ids = jnp.zeros((k, t), jnp.int32)
    for j in range(k):
        m = jnp.max(s, axis=0, keepdims=True)
        am = jnp.min(jnp.where(s == m, iota, rows), axis=0, keepdims=True)
        vals = jnp.where(slot == j, m, vals)
        ids = jnp.where(slot == j, am, ids)
        s = jnp.where(iota == am, -jnp.inf, s)
    return vals, ids


def _select_rows(table, pos):
    out = jnp.zeros(pos.shape, table.dtype)
    for r in range(table.shape[0]):
        out = jnp.where(pos == r, table[r:r + 1, :], out)
    return out


def _post_kernel(ya_ref, sga_ref, gb_ref, x_ref, mod_ref, n2g_ref, wa_ref, wo_ref, wq_ref, keys_ref,
                 x1_ref, h2_ref, idx_ref, gates_ref, q_scr):
    f32 = jnp.float32
    a = jnp.dot(ya_ref[...], wa_ref[...], preferred_element_type=f32)
    merged = sga_ref[...].astype(f32) * a + gb_ref[...].astype(f32)
    o = jnp.dot(merged.astype(jnp.bfloat16), wo_ref[...], preferred_element_type=f32)
    g1 = mod_ref[0, 2:3, :]
    sh2 = mod_ref[0, 3:4, :]
    sc2 = mod_ref[0, 4:5, :]
    x1 = x_ref[...] + g1 * o
    x1_ref[...] = x1
    h2 = x1 * lax.rsqrt(jnp.mean(x1 * x1, axis=-1, keepdims=True) + EPS) * n2g_ref[...]
    h2 = h2 * (1.0 + sc2) + sh2
    h2_ref[...] = h2
    qp = jnp.dot(h2.astype(jnp.bfloat16), wq_ref[...], preferred_element_type=f32)
    for j in range(2 * PEER_HEADS):
        q_scr[j] = qp[:, j * PEER_HALF:(j + 1) * PEER_HALF]

    def head(h, carry):
        tops = []
        for p in range(2):
            q = q_scr[2 * h + p]
            keys = keys_ref[2 * h + p]
            sc = lax.dot_general(keys, q, (((1,), (1,)), ((), ())),
                                 precision=lax.Precision.HIGHEST,
                                 preferred_element_type=f32)
            tops.append(_topk_rows(sc, PEER_TOPK))
        (s1, i1), (s2, i2) = tops
        half = PEER_TOPK // 2
        blocks = [s1[0:1, :] + s2]
        blocks += [s1[a:a + 1, :] + s2[0:half, :] for a in range(1, half)]
        blocks += [s1[half:, :] + s2[0:1, :]]
        vals, pos = _topk_rows(jnp.concatenate(blocks, axis=0), PEER_TOPK)
        mid = pos - PEER_TOPK
        tail0 = PEER_TOPK + (half - 1) * half
        ra = jnp.where(pos < PEER_TOPK, 0,
                       jnp.where(pos < tail0, 1 + (mid >> (half.bit_length() - 1)), pos - tail0 + half))
        rb = jnp.where(pos < PEER_TOPK, pos, jnp.where(pos < tail0, mid & (half - 1), 0))
        eid = _select_rows(i1, ra) * N_KEYS + _select_rows(i2, rb)
        e = jnp.exp(vals - vals[0:1, :])
        g = e / jnp.sum(e, axis=0, keepdims=True)
        row = pl.multiple_of(h * PEER_TOPK, PEER_TOPK)
        idx_ref[pl.ds(row, PEER_TOPK), :] = eid
        gates_ref[pl.ds(row, PEER_TOPK), :] = g
        return carry

    lax.fori_loop(0, PEER_HEADS, head, 0)


POST_TOK_BLOCK = 256


def _post_attention(ya, sga, gb, x, mod, norm2_g, w_a, w_out, w_query, sub_keys, seq_len):
    n, d = x.shape
    tm = POST_TOK_BLOCK
    blocks_per_seq = seq_len // tm
    aw = ya.shape[1]
    keys = sub_keys.reshape(2 * PEER_HEADS, N_KEYS, PEER_HALF)
    tok = lambda w: pl.BlockSpec((tm, w), lambda i: (i, 0))
    full = lambda a: pl.BlockSpec(a.shape, lambda i: (0,) * a.ndim)
    return pl.pallas_call(
        _post_kernel,
        grid=(n // tm,),
        in_specs=[tok(aw), tok(d), tok(d), tok(d),
                  pl.BlockSpec((1, 6, d), lambda i: (i // blocks_per_seq, 0, 0)),
                  full(norm2_g), full(w_a), full(w_out), full(w_query), full(keys)],
        out_specs=[tok(d), tok(d),
                   pl.BlockSpec((PEER_SLOTS, tm), lambda i: (0, i)),
                   pl.BlockSpec((PEER_SLOTS, tm), lambda i: (0, i))],
        out_shape=[jax.ShapeDtypeStruct((n, d), jnp.float32),
                   jax.ShapeDtypeStruct((n, d), jnp.float32),
                   jax.ShapeDtypeStruct((PEER_SLOTS, n), jnp.int32),
                   jax.ShapeDtypeStruct((PEER_SLOTS, n), jnp.float32)],
        scratch_shapes=[pltpu.VMEM((2 * PEER_HEADS, tm, PEER_HALF), jnp.float32)],
        compiler_params=pltpu.CompilerParams(dimension_semantics=("arbitrary",),
                                             vmem_limit_bytes=48 * 1024 * 1024),
        name="post_attention",
    )(ya, sga, gb, x, mod, norm2_g, w_a, w_out, w_query, keys)


def _rowsum_bcast(p, ones_bf16):
    hi = p.astype(jnp.bfloat16)
    lo = (p - hi.astype(jnp.float32)).astype(jnp.bfloat16)
    return (jnp.dot(hi, ones_bf16, preferred_element_type=jnp.float32)
            + jnp.dot(lo, ones_bf16, preferred_element_type=jnp.float32))


def _word_halves(w):
    return (lax.bitcast_convert_type(w & jnp.uint32(0xFFFF0000), jnp.float32),
            lax.bitcast_convert_type(w << 16, jnp.float32))


def _eval_experts(chunk, hrow, grow):
    half = hrow.shape[1] // LANES // 2
    ones_bf16 = jnp.ones((LANES, LANES), jnp.bfloat16)
    eye = (lax.broadcasted_iota(jnp.int32, (PEER_SLOTS, LANES), 0)
           == lax.broadcasted_iota(jnp.int32, (PEER_SLOTS, LANES), 1))
    hpart = lambda c: hrow[:, c * LANES:(c + 1) * LANES]
    psum = None
    for c in range(half):
        hi, lo = _word_halves(chunk(c))
        p = hi * hpart(c) + lo * hpart(c + half)
        psum = p if psum is None else psum + p
    act = _gelu(_rowsum_bcast(psum, ones_bf16))
    gcol = _rowsum_bcast(jnp.where(eye, grow, 0.0), ones_bf16)
    coef = gcol * act
    outs_hi, outs_lo = [], []
    for c in range(half):
        hi, lo = _word_halves(chunk(c + half))
        outs_hi.append(jnp.sum(coef * hi, axis=0, keepdims=True))
        outs_lo.append(jnp.sum(coef * lo, axis=0, keepdims=True))
    return jnp.concatenate(outs_hi + outs_lo, axis=-1)


def _finish_block(x_ref, g2_ref, fg_ref, peer, out_ref):
    y = x_ref[...] + g2_ref[0] * peer[...]
    out_ref[...] = y * lax.rsqrt(jnp.mean(y * y, axis=-1, keepdims=True) + EPS) * fg_ref[...]


def _expert_kernel(idx_ref, gates_ref, h_ref, x_ref, g2_ref, fg_ref, tab_ref, out_ref,
                   *scratch, tok_block, n_slots):
    bufs = scratch[:n_slots]
    peer, sem = scratch[n_slots], scratch[n_slots + 1]
    d_model = h_ref.shape[-1]
    n_chunks = d_model // LANES
    rows_per_tok = PEER_SLOTS * n_chunks

    def issue(t, s):
        for r in range(PEER_SLOTS):
            row = pl.multiple_of(idx_ref[t, r] * n_chunks, n_chunks)
            pltpu.make_async_copy(tab_ref.at[pl.ds(row, n_chunks), :],
                                  bufs[s].at[pl.ds(r * n_chunks, n_chunks), :],
                                  sem.at[s]).start(priority=r % DMA_THREADS)

    def wait(s):
        pltpu.make_async_copy(tab_ref.at[pl.ds(0, rows_per_tok), :], bufs[s], sem.at[s]).wait()

    def compute(t, s):
        chunk = lambda c: bufs[s][pl.ds(c, PEER_SLOTS, stride=n_chunks), :]
        peer[pl.ds(t, 1), :] = _eval_experts(chunk, h_ref[pl.ds(t, 1), :], gates_ref[pl.ds(t, 1), :])

    def step(t, s, prefetch):
        wait(s)
        if prefetch:
            issue(t + n_slots - 1, (s - 1) % n_slots)
        compute(t, s)

    for t in range(n_slots - 1):
        issue(t, t)

    n_groups = tok_block // n_slots

    def group(g, carry):
        for s in range(n_slots):
            step(g * n_slots + s, s, True)
        return carry

    lax.fori_loop(0, n_groups - 1, group, 0)
    for s in range(n_slots):
        t = (n_groups - 1) * n_slots + s
        step(t, s, t + n_slots - 1 < tok_block)

    _finish_block(x_ref, g2_ref, fg_ref, peer, out_ref)


def _expert_cost(n_tok, d):
    pairs = n_tok * PEER_SLOTS
    return pl.CostEstimate(flops=4 * pairs * d, transcendentals=pairs,
                           bytes_accessed=4 * pairs * d + 12 * n_tok * d + 8 * pairs)


def _sc_tanh(y):
    return 1.0 - 2.0 / (jnp.exp(2.0 * y) + 1.0)


def _sc_peer_experts(table3, ids, gates, h3):
    n_tok, n_chunks, lanes = h3.shape
    info = plsc.get_sparse_core_info()
    sl = info.num_lanes
    n_workers = info.num_cores * info.num_subcores
    tok_per_worker = n_tok // n_workers
    assert tok_per_worker * n_workers == n_tok and tok_per_worker % 2 == 0
    rows = SC_GATHER_ROWS
    n_sub = PEER_SLOTS // rows
    assert n_sub % 2 == 0
    half = n_chunks // 2
    pieces = [(c, k * sl) for c in range(half) for k in range(lanes // sl)]
    mesh = plsc.VectorSubcoreMesh(core_axis_name="c", subcore_axis_name="s")
    buf = lambda dt: pltpu.VMEM((rows, n_chunks, lanes), dt)

    @functools.partial(
        pl.kernel, mesh=mesh,
        out_type=jax.ShapeDtypeStruct((n_tok, n_chunks, lanes), jnp.float32),
        scratch_types=[pltpu.VMEM((PEER_SLOTS,), jnp.int32), pltpu.VMEM((PEER_SLOTS,), jnp.int32),
                       pltpu.VMEM((PEER_SLOTS,), jnp.float32), pltpu.VMEM((PEER_SLOTS,), jnp.float32),
                       pltpu.VMEM((n_chunks, lanes), jnp.float32),
                       pltpu.VMEM((n_chunks, lanes), jnp.float32),
                       buf(jnp.uint32), buf(jnp.uint32), pltpu.VMEM((n_chunks, lanes), jnp.float32),
                       pltpu.SemaphoreType.DMA((2,)), pltpu.SemaphoreType.DMA((2,))],
        compiler_params=pltpu.CompilerParams(needs_layout_passes=False),
        cost_estimate=_expert_cost(n_tok, n_chunks * lanes),
        name="sc_peer_experts")
    def run(table_hbm, ids_hbm, gates_hbm, h_hbm, peer_hbm, ids0, ids1, g0, g1, h0, h1, rows0, rows1,
            out_v, gsem, isem):
        bufs = (rows0, rows1)
        ids_p, g_p, h_p = (ids0, ids1), (g0, g1), (h0, h1)
        t_first = (lax.axis_index("s") * info.num_cores + lax.axis_index("c")) * tok_per_worker

        def token_inputs(t, p):
            off = pl.multiple_of(t * PEER_SLOTS, PEER_SLOTS)
            return (pltpu.make_async_copy(ids_hbm.at[pl.ds(off, PEER_SLOTS)], ids_p[p], isem.at[p]),
                    pltpu.make_async_copy(gates_hbm.at[pl.ds(off, PEER_SLOTS)], g_p[p], isem.at[p]),
                    pltpu.make_async_copy(h_hbm.at[t], h_p[p], isem.at[p]))

        def gather(p, s):
            return pltpu.make_async_copy(table_hbm.at[ids_p[p].at[pl.ds(s * rows, rows)]],
                                         bufs[s % 2], gsem.at[s % 2])

        for cp in token_inputs(t_first, 0):
            cp.start()
        for cp in token_inputs(t_first, 0):
            cp.wait()
        gather(0, 0).start()

        def one_token(ti, p):
            t = t_first + ti
            g_v, h_v = g_p[p], h_p[p]
            has_next = ti + 1 < tok_per_worker

            @pl.when(has_next)
            def _():
                for cp in token_inputs(t + 1, 1 - p):
                    cp.start()

            for c, k in pieces:
                out_v[c, pl.ds(k, sl)] = jnp.zeros((sl,), jnp.float32)
                out_v[c + half, pl.ds(k, sl)] = jnp.zeros((sl,), jnp.float32)
            for s in range(n_sub):
                gather(p, s).wait()
                if s + 1 < n_sub:
                    gather(p, s + 1).start()
                else:
                    @pl.when(has_next)
                    def _():
                        for cp in token_inputs(t + 1, 1 - p):
                            cp.wait()
                        gather(1 - p, 0).start()
                rbuf = bufs[s % 2]

                @pl.loop(0, rows, step=SC_ROW_GROUP)
                def _(r0):
                    accs = [jnp.zeros((sl,), jnp.float32) for _ in range(SC_ROW_GROUP)]
                    for c, k in pieces:
                        h_hi, h_lo = h_v[c, pl.ds(k, sl)], h_v[c + half, pl.ds(k, sl)]
                        for j in range(SC_ROW_GROUP):
                            hi, lo = _word_halves(rbuf[r0 + j, c, pl.ds(k, sl)])
                            accs[j] = accs[j] + hi * h_hi + lo * h_lo
                    coefs = []
                    for j in range(SC_ROW_GROUP):
                        a = jnp.broadcast_to(jnp.sum(accs[j]), (sl,))
                        act = 0.5 * a * (1.0 + _sc_tanh(0.7978845608028654 * (a + 0.044715 * (a * a * a))))
                        gate = plsc.load_gather(g_v, [jnp.broadcast_to(s * rows + r0 + j, (sl,))])
                        coefs.append(gate * act)
                    for b0 in range(0, len(pieces), SC_STORE_BATCH):
                        tots = []
                        for c, k in pieces[b0:b0 + SC_STORE_BATCH]:
                            t_hi = t_lo = None
                            for j in range(SC_ROW_GROUP):
                                hi, lo = _word_halves(rbuf[r0 + j, c + half, pl.ds(k, sl)])
                                t_hi = coefs[j] * hi if t_hi is None else t_hi + coefs[j] * hi
                                t_lo = coefs[j] * lo if t_lo is None else t_lo + coefs[j] * lo
                            tots.append((t_hi, t_lo))
                        for (c, k), (t_hi, t_lo) in zip(pieces[b0:b0 + SC_STORE_BATCH], tots):
                            plsc.addupdate(out_v.at[c, pl.ds(k, sl)], t_hi)
                            plsc.addupdate(out_v.at[c + half, pl.ds(k, sl)], t_lo)

            pltpu.sync_copy(out_v, peer_hbm.at[t])

        @pl.loop(0, tok_per_worker, step=2)
        def _(ti):
            one_token(ti, 0)
            one_token(ti + 1, 1)

    return run(table3, ids, gates, h3)


def _finish_kernel(peer_ref, x_ref, g2_ref, fg_ref, prev_ref, out_ref):
    del prev_ref
    _finish_block(x_ref, g2_ref, fg_ref, peer_ref, out_ref)


def _finish_tokens(peer, x1, g2, fg, prev, seq_len):
    d = x1.shape[1]
    tm = FINISH_TOK_BLOCK
    return pl.pallas_call(
        _finish_kernel,
        grid=(peer.shape[0] // tm,),
        in_specs=[pl.BlockSpec((tm, d), lambda i: (i, 0)),
                  pl.BlockSpec((tm, d), lambda i: (i, 0)),
                  pl.BlockSpec((1, 1, d), lambda i: (i * tm // seq_len, 0, 0)),
                  pl.BlockSpec((1, d), lambda i: (0, 0)),
                  pl.BlockSpec(memory_space=pl.ANY)],
        out_specs=pl.BlockSpec((tm, d), lambda i: (i, 0)),
        out_shape=jax.ShapeDtypeStruct(prev.shape, prev.dtype),
        input_output_aliases={4: 0},
        compiler_params=pltpu.CompilerParams(dimension_semantics=("arbitrary",)),
        name="finish_tokens",
    )(peer, x1, g2, fg, prev)


def _pack_expert_table(expert_u, expert_v):
    n_experts, d = expert_u.shape
    bits = lambda a: lax.bitcast_convert_type(a.astype(jnp.bfloat16), jnp.uint16).astype(jnp.uint32)
    pair = lambda b: (b[:, :d // 2] << 16) | b[:, d // 2:]
    words = jnp.concatenate([pair(bits(expert_u)), pair(bits(expert_v))], axis=1)
    return words.reshape(n_experts, d // LANES, LANES)


def _tc_peer_experts(idx, gates, h2, x1, g2, final_g, table, seq_len, first_tok):
    n, d = h2.shape
    m = n - first_tok
    n_chunks = d // LANES
    tb = EXPERT_TOK_BLOCK
    assert tb % EXPERT_SLOTS == 0 and seq_len % tb == 0 and first_tok % tb == 0
    first = first_tok // tb
    return pl.pallas_call(
        functools.partial(_expert_kernel, tok_block=tb, n_slots=EXPERT_SLOTS),
        grid=(m // tb,),
        in_specs=[
            pl.BlockSpec((tb, PEER_SLOTS), lambda i: (i + first, 0), memory_space=pltpu.SMEM),
            pl.BlockSpec((tb, PEER_SLOTS), lambda i: (i + first, 0)),
            pl.BlockSpec((tb, d), lambda i: (i + first, 0)),
            pl.BlockSpec((tb, d), lambda i: (i + first, 0)),
            pl.BlockSpec((1, 1, d), lambda i: ((i + first) * tb // seq_len, 0, 0)),
            pl.BlockSpec((1, d), lambda i: (0, 0)),
            pl.BlockSpec(memory_space=pl.ANY),
        ],
        out_specs=pl.BlockSpec((tb, d), lambda i: (i + first, 0)),
        out_shape=jax.ShapeDtypeStruct((n, d), jnp.float32),
        scratch_shapes=(
            [pltpu.VMEM((PEER_SLOTS * n_chunks, LANES), jnp.uint32) for _ in range(EXPERT_SLOTS)]
            + [pltpu.VMEM((tb, d), jnp.float32), pltpu.SemaphoreType.DMA((EXPERT_SLOTS,))]),
        compiler_params=pltpu.CompilerParams(dimension_semantics=("arbitrary",)),
        cost_estimate=_expert_cost(m, d),
        name="peer_experts",
    )(idx, gates, h2, x1, g2, final_g.reshape(1, d), table.reshape(-1, LANES))


def _mix_and_route(x, mod, norm1_g, w_in, b_forget, ln_v_g, w_spatial, b_spatial, w_branch_a,
                   w_branch_b, w_out, norm2_g, w_query, sub_keys):
    B, S, D = x.shape
    n = B * S
    bf16 = jnp.bfloat16
    qkv, cum, cumt, sga, gb = _input_projection(x, mod, norm1_g, w_in, b_forget, ln_v_g,
                                                w_spatial, b_spatial, w_branch_b)
    y_a = _fox_attention(qkv, cum, cumt, B, S)
    x1, h2, idx_t, gates_t = _post_attention(
        y_a, sga, gb, x.reshape(n, D), mod, norm2_g.reshape(1, D),
        w_branch_a.astype(bf16), w_out.astype(bf16), w_query.astype(bf16), sub_keys, S)
    return x1, h2, idx_t.T, gates_t.T


def kernel(x, c, w_mod, b_mod, norm1_g, w_in, b_forget, ln_v_g, w_spatial, b_spatial, w_branch_a, w_branch_b, w_out, norm2_g, w_query, sub_keys, expert_u, expert_v, final_g):
    B, S, D = x.shape
    n = B * S
    assert w_mod.shape[0] == 1, "the final RMSNorm is fused into the single layer's expert kernels"
    l = 0
    mod = _modulation(c, w_mod[l], b_mod[l]).reshape(B, 6, D)
    table = _pack_expert_table(expert_u[l], expert_v[l])
    n_chunks = D // LANES
    weights = (norm1_g[l], w_in[l], b_forget[l], ln_v_g[l], w_spatial[l], b_spatial[l],
               w_branch_a[l], w_branch_b[l], w_out[l], norm2_g[l], w_query[l], sub_keys[l])

    x1, h2, idx, gates = _mix_and_route(x, mod, *weights)
    g2 = mod[:, 5:6, :]

    n_sc = n * SC_SHARE_PERCENT // 100 // SC_SHARE_ALIGN * SC_SHARE_ALIGN
    out = _tc_peer_experts(idx, gates, h2, x1, g2, final_g, table, S, n_sc)
    if n_sc > 0:
        peer_sc = _sc_peer_experts(table, idx[:n_sc].reshape(-1), gates[:n_sc].reshape(-1),
                                   h2[:n_sc].reshape(n_sc, n_chunks, LANES))
        out = _finish_tokens(peer_sc.reshape(n_sc, D), x1, g2, final_g.reshape(1, D), out, S)
    return out.reshape(B, S, D)
```

```python
import functools

import jax
import jax.numpy as jnp
from jax import lax
from jax.experimental import pallas as pl
from jax.experimental.pallas import tpu as pltpu
from jax.experimental.pallas import tpu_sc as plsc

D_MODEL = 1024
ATT_HEADS = 8
ATT_HEAD_DIM = 64
ATT_WIDTH = ATT_HEADS * ATT_HEAD_DIM
GM_GROUPS = 4
GM_GROUP_DIM = 128
GM_WIDTH = GM_GROUPS * GM_GROUP_DIM
GM_CHUNK = 128
PEER_HEADS = 8
PEER_KEY_DIM = 256
PEER_HALF = PEER_KEY_DIM // 2
N_KEYS = 128
PEER_TOPK = 16
PEER_SLOTS = PEER_HEADS * PEER_TOPK
SPLIT_POINTS = (ATT_WIDTH, 2 * ATT_WIDTH, 3 * ATT_WIDTH, 3 * ATT_WIDTH + ATT_HEADS,
                3 * ATT_WIDTH + ATT_HEADS + 2 * GM_WIDTH,
                3 * ATT_WIDTH + ATT_HEADS + 2 * GM_WIDTH + D_MODEL)
EPS = 1e-6

LANES = 128
EXPERT_TOK_BLOCK = 256
EXPERT_SLOTS = 8
DMA_THREADS = 2
FINISH_TOK_BLOCK = 512
SC_GATHER_ROWS = 32
SC_ROW_GROUP = 4
SC_STORE_BATCH = 8
SC_SHARE_PERCENT = 48
SC_SHARE_ALIGN = 512


def _gelu(x):
    return 0.5 * x * (1.0 + jnp.tanh(0.7978845608028654 * (x + 0.044715 * (x * x * x))))


def _mod_kernel(c_ref, w_ref, b_ref, o_ref):
    c = c_ref[...]
    sc = c * jax.nn.sigmoid(c)
    o_ref[...] = jnp.dot(sc, w_ref[...], precision=lax.Precision.HIGHEST,
                         preferred_element_type=jnp.float32) + b_ref[...]


def _modulation(c, w_mod, b_mod):
    b, d = c.shape
    cols = w_mod.shape[1]
    return pl.pallas_call(
        _mod_kernel,
        grid=(cols // d,),
        in_specs=[pl.BlockSpec((b, d), lambda j: (0, 0)),
                  pl.BlockSpec((d, d), lambda j: (0, j)),
                  pl.BlockSpec((1, d), lambda j: (0, j))],
        out_specs=pl.BlockSpec((b, d), lambda j: (0, j)),
        out_shape=jax.ShapeDtypeStruct((b, cols), jnp.float32),
        name="modulation",
    )(c, w_mod, b_mod.reshape(1, cols))


INPROJ_TOK_BLOCK = 256


def _inproj_kernel(x_ref, mod_ref, n1g_ref, wqkv_ref, wf_ref, bf_ref, wz_ref, wg_ref, lng_ref,
                   wsp_ref, bsp_ref, wb_ref, qkv_ref, cum_ref, cumt_ref, sga_ref, gb_ref, carry):
    f32, bf16 = jnp.float32, jnp.bfloat16
    tm, d = x_ref.shape[1], x_ref.shape[2]
    x = x_ref[0]
    sh1 = mod_ref[0, 0:1, :]
    sc1 = mod_ref[0, 1:2, :]
    h = x * lax.rsqrt(jnp.mean(x * x, axis=-1, keepdims=True) + EPS) * n1g_ref[...]
    hb = (h * (1.0 + sc1) + sh1).astype(bf16)

    qkv = jnp.dot(hb, wqkv_ref[...], preferred_element_type=f32)
    qkv_ref[:, 0:ATT_WIDTH] = (qkv[:, 0:ATT_WIDTH] * (ATT_HEAD_DIM ** -0.5)).astype(bf16)
    qkv_ref[:, ATT_WIDTH:] = qkv[:, ATT_WIDTH:].astype(bf16)

    f = jnp.dot(hb, wf_ref[...], preferred_element_type=f32) + bf_ref[...]
    logf = jnp.minimum(f, 0.0) - jnp.log1p(jnp.exp(-jnp.abs(f)))

    @pl.when(pl.program_id(1) == 0)
    def _():
        carry[...] = jnp.zeros_like(carry)

    tri = (lax.broadcasted_iota(jnp.int32, (tm, tm), 0)
           >= lax.broadcasted_iota(jnp.int32, (tm, tm), 1)).astype(f32)
    cum = jnp.dot(tri, logf, precision=lax.Precision.HIGHEST, preferred_element_type=f32) + carry[...]
    cum_ref[...] = cum
    cumt_ref[0, 0] = jnp.transpose(cum)[0:ATT_HEADS, :]
    carry[...] = cum[tm - 1:tm, :]

    gz = _gelu(jnp.dot(hb, wz_ref[...], preferred_element_type=f32))
    u = gz[:, 0:GM_WIDTH]
    v = gz[:, GM_WIDTH:]
    mu = jnp.mean(v, axis=-1, keepdims=True)
    var = jnp.mean(jnp.square(v - mu), axis=-1, keepdims=True)
    vn = ((v - mu) * lax.rsqrt(var + EPS) * lng_ref[...]).astype(bf16)
    tril = (lax.broadcasted_iota(jnp.int32, (GM_CHUNK, GM_CHUNK), 0)
            >= lax.broadcasted_iota(jnp.int32, (GM_CHUNK, GM_CHUNK), 1))
    w_sp = [jnp.where(tril, wsp_ref[g], 0.0).astype(bf16) for g in range(GM_GROUPS)]
    rows = []
    for ck in range(tm // GM_CHUNK):
        r0 = ck * GM_CHUNK
        cols = []
        for g in range(GM_GROUPS):
            c0 = g * GM_GROUP_DIM
            mixed = jnp.dot(w_sp[g], vn[r0:r0 + GM_CHUNK, c0:c0 + GM_GROUP_DIM],
                            preferred_element_type=f32) + bsp_ref[g]
            cols.append(u[r0:r0 + GM_CHUNK, c0:c0 + GM_GROUP_DIM] * mixed)
        rows.append(jnp.concatenate(cols, axis=1))
    yb = jnp.concatenate(rows, axis=0).astype(bf16)
    ybp = jnp.dot(yb, wb_ref[...], preferred_element_type=f32)

    sg = jax.nn.sigmoid(jnp.dot(hb, wg_ref[...], preferred_element_type=f32))
    sga_ref[...] = sg[:, 0:d].astype(bf16)
    gb_ref[...] = (sg[:, d:] * ybp).astype(bf16)


def _input_projection(x, mod, norm1_g, w_in, b_forget, ln_v_g, w_spatial, b_spatial, w_branch_b):
    B, S, d = x.shape
    n = B * S
    tm = INPROJ_TOK_BLOCK
    bf16 = jnp.bfloat16
    p0, p1, p2, p3, p4, p5 = SPLIT_POINTS
    w_qkv = w_in[:, 0:p2].astype(bf16)
    w_f = jnp.pad(w_in[:, p2:p3], ((0, 0), (0, LANES - ATT_HEADS))).astype(bf16)
    b_f = jnp.pad(b_forget, (0, LANES - ATT_HEADS)).reshape(1, LANES)
    w_z = w_in[:, p3:p4].astype(bf16)
    w_g = w_in[:, p4:].astype(bf16)
    nt = S // tm
    tok = lambda w: pl.BlockSpec((tm, w), lambda b, i: (b * nt + i, 0))
    full = lambda a: pl.BlockSpec(a.shape, lambda b, i: (0,) * a.ndim)
    args = (x, mod, norm1_g.reshape(1, d), w_qkv, w_f, b_f, w_z, w_g, ln_v_g.reshape(1, GM_WIDTH),
            w_spatial, b_spatial.reshape(GM_GROUPS, GM_CHUNK, 1), w_branch_b.astype(bf16))
    return pl.pallas_call(
        _inproj_kernel,
        grid=(B, nt),
        in_specs=[pl.BlockSpec((1, tm, d), lambda b, i: (b, i, 0)),
                  pl.BlockSpec((1, 6, d), lambda b, i: (b, 0, 0))] + [full(a) for a in args[2:]],
        out_specs=[tok(3 * ATT_WIDTH), tok(LANES),
                   pl.BlockSpec((1, 1, ATT_HEADS, tm), lambda b, i: (b, i, 0, 0)), tok(d), tok(d)],
        out_shape=[jax.ShapeDtypeStruct((n, 3 * ATT_WIDTH), bf16),
                   jax.ShapeDtypeStruct((n, LANES), jnp.float32),
                   jax.ShapeDtypeStruct((B, nt, ATT_HEADS, tm), jnp.float32),
                   jax.ShapeDtypeStruct((n, d), bf16),
                   jax.ShapeDtypeStruct((n, d), bf16)],
        scratch_shapes=[pltpu.VMEM((1, LANES), jnp.float32)],
        compiler_params=pltpu.CompilerParams(dimension_semantics=("arbitrary", "arbitrary"),
                                             vmem_limit_bytes=56 * 1024 * 1024),
        name="input_projection",
    )(*args)


ATT_BLOCK = 512


def _fox_kernel(q_ref, k_ref, v_ref, cum_ref, cumt_ref, o_ref, *, blk, ratio):
    f32 = jnp.float32
    hp = pl.program_id(1)
    i = pl.program_id(2)
    dh = ATT_HEAD_DIM
    q2 = q_ref[...]
    first = lax.broadcasted_iota(jnp.int32, (1, 2 * dh), 1) < dh
    qs = (jnp.where(first, q2, jnp.zeros_like(q2)), jnp.where(first, jnp.zeros_like(q2), q2))
    cum_blk = cum_ref[...]
    lane = lax.broadcasted_iota(jnp.int32, cum_blk.shape, 1)
    cqs = [jnp.sum(jnp.where(lane == 2 * hp + e, cum_blk, 0.0), axis=1, keepdims=True)
           for e in range(2)]

    def block(j, carry, masked):
        off = pl.multiple_of(j * blk, blk)
        k2 = k_ref[0, pl.ds(off, blk), :]
        v2 = v_ref[0, pl.ds(off, blk), :]
        out = []
        for e in range(2):
            m, l, acc = carry[e]
            ck = jnp.concatenate([cumt_ref[0, j * ratio + a, pl.ds(2 * hp + e, 1), :]
                                  for a in range(ratio)], axis=1)
            s = lax.dot_general(qs[e], k2, (((1,), (1,)), ((), ())), preferred_element_type=f32)
            s = s + (cqs[e] - ck)
            if masked:
                causal = (lax.broadcasted_iota(jnp.int32, (blk, blk), 0)
                          >= lax.broadcasted_iota(jnp.int32, (blk, blk), 1))
                s = jnp.where(causal, s, -jnp.inf)
            m_new = jnp.maximum(m, jnp.max(s, axis=1, keepdims=True))
            alpha = jnp.exp(m - m_new)
            p = jnp.exp(s - m_new)
            l = alpha * l + jnp.sum(p, axis=1, keepdims=True)
            acc = alpha * acc + jnp.dot(p.astype(v2.dtype), v2, preferred_element_type=f32)
            out.append((m_new, l, acc))
        return tuple(out)

    one = (jnp.full((blk, 1), -1e30, f32), jnp.zeros((blk, 1), f32), jnp.zeros((blk, 2 * dh), f32))
    carry = lax.fori_loop(0, i, lambda j, c: block(j, c, False), (one, one))
    (_, l0, acc0), (_, l1, acc1) = block(i, carry, True)
    o_ref[...] = jnp.where(first, acc0 / l0, acc1 / l1).astype(o_ref.dtype)


def _fox_attention(qkv, cum, cumt, batch, seq_len):
    n = qkv.shape[0]
    tm = cumt.shape[-1]
    blk = max(min(ATT_BLOCK, seq_len), tm)
    nb = seq_len // blk
    pair = 2 * ATT_HEAD_DIM
    n_pairs = ATT_WIDTH // pair
    qkv3 = qkv.reshape(batch, seq_len, 3 * ATT_WIDTH)
    return pl.pallas_call(
        functools.partial(_fox_kernel, blk=blk, ratio=blk // tm),
        grid=(batch, n_pairs, nb),
        in_specs=[pl.BlockSpec((blk, pair), lambda b, hp, i: (b * nb + i, hp)),
                  pl.BlockSpec((1, seq_len, pair), lambda b, hp, i: (b, 0, n_pairs + hp)),
                  pl.BlockSpec((1, seq_len, pair), lambda b, hp, i: (b, 0, 2 * n_pairs + hp)),
                  pl.BlockSpec((blk, LANES), lambda b, hp, i: (b * nb + i, 0)),
                  pl.BlockSpec((1,) + cumt.shape[1:], lambda b, hp, i: (b, 0, 0, 0))],
        out_specs=pl.BlockSpec((blk, pair), lambda b, hp, i: (b * nb + i, hp)),
        out_shape=jax.ShapeDtypeStruct((n, ATT_WIDTH), jnp.bfloat16),
        compiler_params=pltpu.CompilerParams(
            dimension_semantics=("arbitrary", "arbitrary", "arbitrary")),
        name="fox_attention",
    )(qkv, qkv3, qkv3, cum, cumt)


def _topk_rows(s, k):
    rows, t = s.shape
    iota = lax.broadcasted_iota(jnp.int32, (rows, t), 0).astype(jnp.float32)
    slot = lax.broadcasted_iota(jnp.int32, (k, t), 0)
    vals = jnp.zeros((k, t), jnp.float32)
    ids = jnp.zeros((k, t), jnp.float32)
    for j in range(k):
        m = jnp.max(s, axis=0, keepdims=True)
        am = jnp.min(jnp.where(s == m, iota, float(rows)), axis=0, keepdims=True)
        vals = jnp.where(slot == j, m, vals)
        ids = jnp.where(slot == j, am, ids)
        s = jnp.where(iota == am, -jnp.inf, s)
    return vals, ids.astype(jnp.int32)


def _select_rows(table, pos):
    out = jnp.zeros(pos.shape, table.dtype)
    for r in range(table.shape[0]):
        out = jnp.where(pos == r, table[r:r + 1, :], out)
    return out


def _post_kernel(ya_ref, sga_ref, gb_ref, x_ref, mod_ref, n2g_ref, wa_ref, wo_ref, wq_ref, keys_ref,
                 x1_ref, h2_ref, idx_ref, gates_ref, q_scr):
    f32 = jnp.float32
    a = jnp.dot(ya_ref[...], wa_ref[...], preferred_element_type=f32)
    merged = sga_ref[...].astype(f32) * a + gb_ref[...].astype(f32)
    o = jnp.dot(merged.astype(jnp.bfloat16), wo_ref[...], preferred_element_type=f32)
    g1 = mod_ref[0, 2:3, :]
    sh2 = mod_ref[0, 3:4, :]
    sc2 = mod_ref[0, 4:5, :]
    x1 = x_ref[...] + g1 * o
    x1_ref[...] = x1
    h2 = x1 * lax.rsqrt(jnp.mean(x1 * x1, axis=-1, keepdims=True) + EPS) * n2g_ref[...]
    h2 = h2 * (1.0 + sc2) + sh2
    h2_ref[...] = h2
    qp = jnp.dot(h2.astype(jnp.bfloat16), wq_ref[...], preferred_element_type=f32)
    for j in range(2 * PEER_HEADS):
        q_scr[j] = qp[:, j * PEER_HALF:(j + 1) * PEER_HALF]

    def head(h, carry):
        tops = []
        for p in range(2):
            q = q_scr[2 * h + p]
            keys = keys_ref[2 * h + p]
            sc = lax.dot_general(keys, q, (((1,), (1,)), ((), ())),
                                 precision=lax.Precision.HIGHEST,
                                 preferred_element_type=f32)
            tops.append(_topk_rows(sc, PEER_TOPK))
        (s1, i1), (s2, i2) = tops
        half = PEER_TOPK // 2
        blocks = [s1[0:1, :] + s2]
        blocks += [s1[a:a + 1, :] + s2[0:half, :] for a in range(1, half)]
        blocks += [s1[half:, :] + s2[0:1, :]]
        vals, pos = _topk_rows(jnp.concatenate(blocks, axis=0), PEER_TOPK)
        mid = pos - PEER_TOPK
        tail0 = PEER_TOPK + (half - 1) * half
        ra = jnp.where(pos < PEER_TOPK, 0,
                       jnp.where(pos < tail0, 1 + (mid >> (half.bit_length() - 1)), pos - tail0 + half))
        rb = jnp.where(pos < PEER_TOPK, pos, jnp.where(pos < tail0, mid & (half - 1), 0))
        eid = _select_rows(i1, ra) * N_KEYS + _select_rows(i2, rb)
        e = jnp.exp(vals - vals[0:1, :])
        g = e / jnp.sum(e, axis=0, keepdims=True)
        row = pl.multiple_of(h * PEER_TOPK, PEER_TOPK)
        idx_ref[pl.ds(row, PEER_TOPK), :] = eid
        gates_ref[pl.ds(row, PEER_TOPK), :] = g
        return carry

    lax.fori_loop(0, PEER_HEADS, head, 0)


POST_TOK_BLOCK = 256


def _post_attention(ya, sga, gb, x, mod, norm2_g, w_a, w_out, w_query, sub_keys, seq_len):
    n, d = x.shape
    tm = POST_TOK_BLOCK
    blocks_per_seq = seq_len // tm
    aw = ya.shape[1]
    keys = sub_keys.reshape(2 * PEER_HEADS, N_KEYS, PEER_HALF)
    tok = lambda w: pl.BlockSpec((tm, w), lambda i: (i, 0))
    full = lambda a: pl.BlockSpec(a.shape, lambda i: (0,) * a.ndim)
    return pl.pallas_call(
        _post_kernel,
        grid=(n // tm,),
        in_specs=[tok(aw), tok(d), tok(d), tok(d),
                  pl.BlockSpec((1, 6, d), lambda i: (i // blocks_per_seq, 0, 0)),
                  full(norm2_g), full(w_a), full(w_out), full(w_query), full(keys)],
        out_specs=[tok(d), tok(d),
                   pl.BlockSpec((PEER_SLOTS, tm), lambda i: (0, i)),
                   pl.BlockSpec((PEER_SLOTS, tm), lambda i: (0, i))],
        out_shape=[jax.ShapeDtypeStruct((n, d), jnp.float32),
                   jax.ShapeDtypeStruct((n, d), jnp.float32),
                   jax.ShapeDtypeStruct((PEER_SLOTS, n), jnp.int32),
                   jax.ShapeDtypeStruct((PEER_SLOTS, n), jnp.float32)],
        scratch_shapes=[pltpu.VMEM((2 * PEER_HEADS, tm, PEER_HALF), jnp.float32)],
        compiler_params=pltpu.CompilerParams(dimension_semantics=("arbitrary",),
                                             vmem_limit_bytes=48 * 1024 * 1024),
        name="post_attention",
    )(ya, sga, gb, x, mod, norm2_g, w_a, w_out, w_query, keys)


def _rowsum_bcast(p, ones_bf16):
    hi = p.astype(jnp.bfloat16)
    lo = (p - hi.astype(jnp.float32)).astype(jnp.bfloat16)
    return (jnp.dot(hi, ones_bf16, preferred_element_type=jnp.float32)
            + jnp.dot(lo, ones_bf16, preferred_element_type=jnp.float32))


def _word_halves(w):
    return (lax.bitcast_convert_type(w & jnp.uint32(0xFFFF0000), jnp.float32),
            lax.bitcast_convert_type(w << 16, jnp.float32))


def _eval_experts(chunk, hrow, grow):
    half = hrow.shape[1] // LANES // 2
    ones_bf16 = jnp.ones((LANES, LANES), jnp.bfloat16)
    eye = (lax.broadcasted_iota(jnp.int32, (PEER_SLOTS, LANES), 0)
           == lax.broadcasted_iota(jnp.int32, (PEER_SLOTS, LANES), 1))
    hpart = lambda c: hrow[:, c * LANES:(c + 1) * LANES]
    psum = None
    for c in range(half):
        hi, lo = _word_halves(chunk(c))
        p = hi * hpart(c) + lo * hpart(c + half)
        psum = p if psum is None else psum + p
    act = _gelu(_rowsum_bcast(psum, ones_bf16))
    gcol = _rowsum_bcast(jnp.where(eye, grow, 0.0), ones_bf16)
    coef = gcol * act
    outs_hi, outs_lo = [], []
    for c in range(half):
        hi, lo = _word_halves(chunk(c + half))
        outs_hi.append(jnp.sum(coef * hi, axis=0, keepdims=True))
        outs_lo.append(jnp.sum(coef * lo, axis=0, keepdims=True))
    return jnp.concatenate(outs_hi + outs_lo, axis=-1)


def _finish_block(x_ref, g2_ref, fg_ref, peer, out_ref):
    y = x_ref[...] + g2_ref[0] * peer[...]
    out_ref[...] = y * lax.rsqrt(jnp.mean(y * y, axis=-1, keepdims=True) + EPS) * fg_ref[...]


def _expert_kernel(idx_ref, gates_ref, h_ref, x_ref, g2_ref, fg_ref, tab_ref, out_ref,
                   *scratch, tok_block, n_slots):
    bufs = scratch[:n_slots]
    peer, sem = scratch[n_slots], scratch[n_slots + 1]
    d_model = h_ref.shape[-1]
    n_chunks = d_model // LANES
    rows_per_tok = PEER_SLOTS * n_chunks

    def issue(t, s):
        for r in range(PEER_SLOTS):
            row = pl.multiple_of(idx_ref[t, r] * n_chunks, n_chunks)
            pltpu.make_async_copy(tab_ref.at[pl.ds(row, n_chunks), :],
                                  bufs[s].at[pl.ds(r * n_chunks, n_chunks), :],
                                  sem.at[s]).start(priority=r % DMA_THREADS)

    def wait(s):
        pltpu.make_async_copy(tab_ref.at[pl.ds(0, rows_per_tok), :], bufs[s], sem.at[s]).wait()

    def compute(t, s):
        chunk = lambda c: bufs[s][pl.ds(c, PEER_SLOTS, stride=n_chunks), :]
        peer[pl.ds(t, 1), :] = _eval_experts(chunk, h_ref[pl.ds(t, 1), :], gates_ref[pl.ds(t, 1), :])

    def step(t, s, prefetch):
        wait(s)
        if prefetch:
            issue(t + n_slots - 1, (s - 1) % n_slots)
        compute(t, s)

    for t in range(n_slots - 1):
        issue(t, t)

    n_groups = tok_block // n_slots

    def group(g, carry):
        for s in range(n_slots):
            step(g * n_slots + s, s, True)
        return carry

    lax.fori_loop(0, n_groups - 1, group, 0)
    for s in range(n_slots):
        t = (n_groups - 1) * n_slots + s
        step(t, s, t + n_slots - 1 < tok_block)

    _finish_block(x_ref, g2_ref, fg_ref, peer, out_ref)


def _expert_cost(n_tok, d):
    pairs = n_tok * PEER_SLOTS
    return pl.CostEstimate(flops=4 * pairs * d, transcendentals=pairs,
                           bytes_accessed=4 * pairs * d + 12 * n_tok * d + 8 * pairs)


def _sc_tanh(y):
    return 1.0 - 2.0 / (jnp.exp(2.0 * y) + 1.0)


def _sc_peer_experts(table3, ids, gates, h3):
    n_tok, n_chunks, lanes = h3.shape
    info = plsc.get_sparse_core_info()
    sl = info.num_lanes
    n_workers = info.num_cores * info.num_subcores
    tok_per_worker = n_tok // n_workers
    assert tok_per_worker * n_workers == n_tok and tok_per_worker % 2 == 0
    rows = SC_GATHER_ROWS
    n_sub = PEER_SLOTS // rows
    assert n_sub % 2 == 0
    half = n_chunks // 2
    pieces = [(c, k * sl) for c in range(half) for k in range(lanes // sl)]
    mesh = plsc.VectorSubcoreMesh(core_axis_name="c", subcore_axis_name="s")
    buf = lambda dt: pltpu.VMEM((rows, n_chunks, lanes), dt)

    @functools.partial(
        pl.kernel, mesh=mesh,
        out_type=jax.ShapeDtypeStruct((n_tok, n_chunks, lanes), jnp.float32),
        scratch_types=[pltpu.VMEM((PEER_SLOTS,), jnp.int32), pltpu.VMEM((PEER_SLOTS,), jnp.int32),
                       pltpu.VMEM((PEER_SLOTS,), jnp.float32), pltpu.VMEM((PEER_SLOTS,), jnp.float32),
                       pltpu.VMEM((n_chunks, lanes), jnp.float32),
                       pltpu.VMEM((n_chunks, lanes), jnp.float32),
                       buf(jnp.uint32), buf(jnp.uint32), pltpu.VMEM((n_chunks, lanes), jnp.float32),
                       pltpu.SemaphoreType.DMA((2,)), pltpu.SemaphoreType.DMA((2,))],
        compiler_params=pltpu.CompilerParams(needs_layout_passes=False),
        cost_estimate=_expert_cost(n_tok, n_chunks * lanes),
        name="sc_peer_experts")
    def run(table_hbm, ids_hbm, gates_hbm, h_hbm, peer_hbm, ids0, ids1, g0, g1, h0, h1, rows0, rows1,
            out_v, gsem, isem):
        bufs = (rows0, rows1)
        ids_p, g_p, h_p = (ids0, ids1), (g0, g1), (h0, h1)
        t_first = (lax.axis_index("s") * info.num_cores + lax.axis_index("c")) * tok_per_worker

        def token_inputs(t, p):
            off = pl.multiple_of(t * PEER_SLOTS, PEER_SLOTS)
            return (pltpu.make_async_copy(ids_hbm.at[pl.ds(off, PEER_SLOTS)], ids_p[p], isem.at[p]),
                    pltpu.make_async_copy(gates_hbm.at[pl.ds(off, PEER_SLOTS)], g_p[p], isem.at[p]),
                    pltpu.make_async_copy(h_hbm.at[t], h_p[p], isem.at[p]))

        def gather(p, s):
            return pltpu.make_async_copy(table_hbm.at[ids_p[p].at[pl.ds(s * rows, rows)]],
                                         bufs[s % 2], gsem.at[s % 2])

        for cp in token_inputs(t_first, 0):
            cp.start()
        for cp in token_inputs(t_first, 0):
            cp.wait()
        gather(0, 0).start()

        def one_token(ti, p):
            t = t_first + ti
            g_v, h_v = g_p[p], h_p[p]
            has_next = ti + 1 < tok_per_worker

            @pl.when(has_next)
            def _():
                for cp in token_inputs(t + 1, 1 - p):
                    cp.start()

            for c, k in pieces:
                out_v[c, pl.ds(k, sl)] = jnp.zeros((sl,), jnp.float32)
                out_v[c + half, pl.ds(k, sl)] = jnp.zeros((sl,), jnp.float32)
            for s in range(n_sub):
                gather(p, s).wait()
                if s + 1 < n_sub:
                    gather(p, s + 1).start()
                else:
                    @pl.when(has_next)
                    def _():
                        for cp in token_inputs(t + 1, 1 - p):
                            cp.wait()
                        gather(1 - p, 0).start()
                rbuf = bufs[s % 2]

                @pl.loop(0, rows, step=SC_ROW_GROUP)
                def _(r0):
                    accs = [jnp.zeros((sl,), jnp.float32) for _ in range(SC_ROW_GROUP)]
                    for c, k in pieces:
                        h_hi, h_lo = h_v[c, pl.ds(k, sl)], h_v[c + half, pl.ds(k, sl)]
                        for j in range(SC_ROW_GROUP):
                            hi, lo = _word_halves(rbuf[r0 + j, c, pl.ds(k, sl)])
                            accs[j] = accs[j] + hi * h_hi + lo * h_lo
                    coefs = []
                    for j in range(SC_ROW_GROUP):
                        a = jnp.broadcast_to(jnp.sum(accs[j]), (sl,))
                        act = 0.5 * a * (1.0 + _sc_tanh(0.7978845608028654 * (a + 0.044715 * (a * a * a))))
                        gate = plsc.load_gather(g_v, [jnp.broadcast_to(s * rows + r0 + j, (sl,))])
                        coefs.append(gate * act)
                    for b0 in range(0, len(pieces), SC_STORE_BATCH):
                        tots = []
                        for c, k in pieces[b0:b0 + SC_STORE_BATCH]:
                            t_hi = t_lo = None
                            for j in range(SC_ROW_GROUP):
                                hi, lo = _word_halves(rbuf[r0 + j, c + half, pl.ds(k, sl)])
                                t_hi = coefs[j] * hi if t_hi is None else t_hi + coefs[j] * hi
                                t_lo = coefs[j] * lo if t_lo is None else t_lo + coefs[j] * lo
                            tots.append((t_hi, t_lo))
                        for (c, k), (t_hi, t_lo) in zip(pieces[b0:b0 + SC_STORE_BATCH], tots):
                            plsc.addupdate(out_v.at[c, pl.ds(k, sl)], t_hi)
                            plsc.addupdate(out_v.at[c + half, pl.ds(k, sl)], t_lo)

            pltpu.sync_copy(out_v, peer_hbm.at[t])

        @pl.loop(0, tok_per_worker, step=2)
        def _(ti):
            one_token(ti, 0)
            one_token(ti + 1, 1)

    return run(table3, ids, gates, h3)


def _finish_kernel(peer_ref, x_ref, g2_ref, fg_ref, prev_ref, out_ref):
    del prev_ref
    _finish_block(x_ref, g2_ref, fg_ref, peer_ref, out_ref)


def _finish_tokens(peer, x1, g2, fg, prev, seq_len):
    d = x1.shape[1]
    tm = FINISH_TOK_BLOCK
    return pl.pallas_call(
        _finish_kernel,
        grid=(peer.shape[0] // tm,),
        in_specs=[pl.BlockSpec((tm, d), lambda i: (i, 0)),
                  pl.BlockSpec((tm, d), lambda i: (i, 0)),
                  pl.BlockSpec((1, 1, d), lambda i: (i * tm // seq_len, 0, 0)),
                  pl.BlockSpec((1, d), lambda i: (0, 0)),
                  pl.BlockSpec(memory_space=pl.ANY)],
        out_specs=pl.BlockSpec((tm, d), lambda i: (i, 0)),
        out_shape=jax.ShapeDtypeStruct(prev.shape, prev.dtype),
        input_output_aliases={4: 0},
        compiler_params=pltpu.CompilerParams(dimension_semantics=("arbitrary",)),
        name="finish_tokens",
    )(peer, x1, g2, fg, prev)


def _pack_expert_table(expert_u, expert_v):
    n_experts, d = expert_u.shape
    bits = lambda a: lax.bitcast_convert_type(a.astype(jnp.bfloat16), jnp.uint16).astype(jnp.uint32)
    pair = lambda b: (b[:, :d // 2] << 16) | b[:, d // 2:]
    words = jnp.concatenate([pair(bits(expert_u)), pair(bits(expert_v))], axis=1)
    return words.reshape(n_experts, d // LANES, LANES)


def _tc_peer_experts(idx, gates, h2, x1, g2, final_g, table, seq_len, first_tok):
    n, d = h2.shape
    m = n - first_tok
    n_chunks = d // LANES
    tb = EXPERT_TOK_BLOCK
    assert tb % EXPERT_SLOTS == 0 and seq_len % tb == 0 and first_tok % tb == 0
    first = first_tok // tb
    return pl.pallas_call(
        functools.partial(_expert_kernel, tok_block=tb, n_slots=EXPERT_SLOTS),
        grid=(m // tb,),
        in_specs=[
            pl.BlockSpec((tb, PEER_SLOTS), lambda i: (i + first, 0), memory_space=pltpu.SMEM),
            pl.BlockSpec((tb, PEER_SLOTS), lambda i: (i + first, 0)),
            pl.BlockSpec((tb, d), lambda i: (i + first, 0)),
            pl.BlockSpec((tb, d), lambda i: (i + first, 0)),
            pl.BlockSpec((1, 1, d), lambda i: ((i + first) * tb // seq_len, 0, 0)),
            pl.BlockSpec((1, d), lambda i: (0, 0)),
            pl.BlockSpec(memory_space=pl.ANY),
        ],
        out_specs=pl.BlockSpec((tb, d), lambda i: (i + first, 0)),
        out_shape=jax.ShapeDtypeStruct((n, d), jnp.float32),
        scratch_shapes=(
            [pltpu.VMEM((PEER_SLOTS * n_chunks, LANES), jnp.uint32) for _ in range(EXPERT_SLOTS)]
            + [pltpu.VMEM((tb, d), jnp.float32), pltpu.SemaphoreType.DMA((EXPERT_SLOTS,))]),
        compiler_params=pltpu.CompilerParams(dimension_semantics=("arbitrary",)),
        cost_estimate=_expert_cost(m, d),
        name="peer_experts",
    )(idx, gates, h2, x1, g2, final_g.reshape(1, d), table.reshape(-1, LANES))


def _mix_and_route(x, mod, norm1_g, w_in, b_forget, ln_v_g, w_spatial, b_spatial, w_branch_a,
                   w_branch_b, w_out, norm2_g, w_query, sub_keys):
    B, S, D = x.shape
    n = B * S
    bf16 = jnp.bfloat16
    qkv, cum, cumt, sga, gb = _input_projection(x, mod, norm1_g, w_in, b_forget, ln_v_g,
                                                w_spatial, b_spatial, w_branch_b)
    y_a = _fox_attention(qkv, cum, cumt, B, S)
    x1, h2, idx_t, gates_t = _post_attention(
        y_a, sga, gb, x.reshape(n, D), mod, norm2_g.reshape(1, D),
        w_branch_a.astype(bf16), w_out.astype(bf16), w_query.astype(bf16), sub_keys, S)
    return x1, h2, idx_t.T, gates_t.T


def kernel(x, c, w_mod, b_mod, norm1_g, w_in, b_forget, ln_v_g, w_spatial, b_spatial, w_branch_a, w_branch_b, w_out, norm2_g, w_query, sub_keys, expert_u, expert_v, final_g):
    B, S, D = x.shape
    n = B * S
    assert w_mod.shape[0] == 1, "the final RMSNorm is fused into the single layer's expert kernels"
    l = 0
    mod = _modulation(c, w_mod[l], b_mod[l]).reshape(B, 6, D)
    table = _pack_expert_table(expert_u[l], expert_v[l])
    n_chunks = D // LANES
    weights = (norm1_g[l], w_in[l], b_forget[l], ln_v_g[l], w_spatial[l], b_spatial[l],
               w_branch_a[l], w_branch_b[l], w_out[l], norm2_g[l], w_query[l], sub_keys[l])

    x1, h2, idx, gates = _mix_and_route(x, mod, *weights)
    g2 = mod[:, 5:6, :]

    n_sc = n * SC_SHARE_PERCENT // 100 // SC_SHARE_ALIGN * SC_SHARE_ALIGN
    out = _tc_peer_experts(idx, gates, h2, x1, g2, final_g, table, S, n_sc)
    if n_sc > 0:
        peer_sc = _sc_peer_experts(table, idx[:n_sc].reshape(-1), gates[:n_sc].reshape(-1),
                                   h2[:n_sc].reshape(n_sc, n_chunks, LANES))
        out = _finish_tokens(peer_sc.reshape(n_sc, D), x1, g2, final_g.reshape(1, D), out, S)
    return out.reshape(B, S, D)
```

```python
import functools

import jax
import jax.numpy as jnp
from jax import lax
from jax.experimental import pallas as pl
from jax.experimental.pallas import tpu as pltpu
from jax.experimental.pallas import tpu_sc as plsc

D_MODEL = 1024
ATT_HEADS = 8
ATT_HEAD_DIM = 64
ATT_WIDTH = ATT_HEADS * ATT_HEAD_DIM
GM_GROUPS = 4
GM_GROUP_DIM = 128
GM_WIDTH = GM_GROUPS * GM_GROUP_DIM
GM_CHUNK = 128
PEER_HEADS = 8
PEER_KEY_DIM = 256
PEER_HALF = PEER_KEY_DIM // 2
N_KEYS = 128
PEER_TOPK = 16
PEER_SLOTS = PEER_HEADS * PEER_TOPK
SPLIT_POINTS = (ATT_WIDTH, 2 * ATT_WIDTH, 3 * ATT_WIDTH, 3 * ATT_WIDTH + ATT_HEADS,
                3 * ATT_WIDTH + ATT_HEADS + 2 * GM_WIDTH,
                3 * ATT_WIDTH + ATT_HEADS + 2 * GM_WIDTH + D_MODEL)
EPS = 1e-6

LANES = 128
EXPERT_TOK_BLOCK = 256
EXPERT_SLOTS = 8
DMA_THREADS = 2
FINISH_TOK_BLOCK = 512
SC_GATHER_ROWS = 32
SC_ROW_GROUP = 4
SC_STORE_BATCH = 8
SC_SHARE_PERCENT = 48
SC_SHARE_ALIGN = 512


def _gelu(x):
    return 0.5 * x * (1.0 + jnp.tanh(0.7978845608028654 * (x + 0.044715 * (x * x * x))))


def _mod_kernel(c_ref, w_ref, b_ref, o_ref):
    c = c_ref[...]
    sc = c * jax.nn.sigmoid(c)
    o_ref[...] = jnp.dot(sc, w_ref[...], precision=lax.Precision.HIGHEST,
                         preferred_element_type=jnp.float32) + b_ref[...]


def _modulation(c, w_mod, b_mod):
    b, d = c.shape
    cols = w_mod.shape[1]
    return pl.pallas_call(
        _mod_kernel,
        grid=(cols // d,),
        in_specs=[pl.BlockSpec((b, d), lambda j: (0, 0)),
                  pl.BlockSpec((d, d), lambda j: (0, j)),
                  pl.BlockSpec((1, d), lambda j: (0, j))],
        out_specs=pl.BlockSpec((b, d), lambda j: (0, j)),
        out_shape=jax.ShapeDtypeStruct((b, cols), jnp.float32),
        name="modulation",
    )(c, w_mod, b_mod.reshape(1, cols))


INPROJ_TOK_BLOCK = 256


def _inproj_kernel(x_ref, mod_ref, n1g_ref, wqkv_ref, wf_ref, bf_ref, wz_ref, wg_ref, lng_ref,
                   wsp_ref, bsp_ref, wb_ref, qkv_ref, cum_ref, cumt_ref, sga_ref, gb_ref, carry):
    f32, bf16 = jnp.float32, jnp.bfloat16
    tm, d = x_ref.shape[1], x_ref.shape[2]
    x = x_ref[0]
    sh1 = mod_ref[0, 0:1, :]
    sc1 = mod_ref[0, 1:2, :]
    h = x * lax.rsqrt(jnp.mean(x * x, axis=-1, keepdims=True) + EPS) * n1g_ref[...]
    hb = (h * (1.0 + sc1) + sh1).astype(bf16)

    qkv = jnp.dot(hb, wqkv_ref[...], preferred_element_type=f32)
    qkv_ref[:, 0:ATT_WIDTH] = (qkv[:, 0:ATT_WIDTH] * (ATT_HEAD_DIM ** -0.5)).astype(bf16)
    qkv_ref[:, ATT_WIDTH:] = qkv[:, ATT_WIDTH:].astype(bf16)

    f = jnp.dot(hb, wf_ref[...], preferred_element_type=f32) + bf_ref[...]
    logf = jnp.minimum(f, 0.0) - jnp.log1p(jnp.exp(-jnp.abs(f)))

    @pl.when(pl.program_id(1) == 0)
    def _():
        carry[...] = jnp.zeros_like(carry)

    tri = (lax.broadcasted_iota(jnp.int32, (tm, tm), 0)
           >= lax.broadcasted_iota(jnp.int32, (tm, tm), 1)).astype(f32)
    cum = jnp.dot(tri, logf, precision=lax.Precision.HIGHEST, preferred_element_type=f32) + carry[...]
    cum_ref[...] = cum
    cumt_ref[0, 0] = jnp.transpose(cum)[0:ATT_HEADS, :]
    carry[...] = cum[tm - 1:tm, :]

    gz = _gelu(jnp.dot(hb, wz_ref[...], preferred_element_type=f32))
    u = gz[:, 0:GM_WIDTH]
    v = gz[:, GM_WIDTH:]
    mu = jnp.mean(v, axis=-1, keepdims=True)
    var = jnp.mean(jnp.square(v - mu), axis=-1, keepdims=True)
    vn = ((v - mu) * lax.rsqrt(var + EPS) * lng_ref[...]).astype(bf16)
    tril = (lax.broadcasted_iota(jnp.int32, (GM_CHUNK, GM_CHUNK), 0)
            >= lax.broadcasted_iota(jnp.int32, (GM_CHUNK, GM_CHUNK), 1))
    w_sp = [jnp.where(tril, wsp_ref[g], 0.0).astype(bf16) for g in range(GM_GROUPS)]
    rows = []
    for ck in range(tm // GM_CHUNK):
        r0 = ck * GM_CHUNK
        cols = []
        for g in range(GM_GROUPS):
            c0 = g * GM_GROUP_DIM
            mixed = jnp.dot(w_sp[g], vn[r0:r0 + GM_CHUNK, c0:c0 + GM_GROUP_DIM],
                            preferred_element_type=f32) + bsp_ref[g]
            cols.append(u[r0:r0 + GM_CHUNK, c0:c0 + GM_GROUP_DIM] * mixed)
        rows.append(jnp.concatenate(cols, axis=1))
    yb = jnp.concatenate(rows, axis=0).astype(bf16)
    ybp = jnp.dot(yb, wb_ref[...], preferred_element_type=f32)

    sg = jax.nn.sigmoid(jnp.dot(hb, wg_ref[...], preferred_element_type=f32))
    sga_ref[...] = sg[:, 0:d].astype(bf16)
    gb_ref[...] = (sg[:, d:] * ybp).astype(bf16)


def _input_projection(x, mod, norm1_g, w_in, b_forget, ln_v_g, w_spatial, b_spatial, w_branch_b):
    B, S, d = x.shape
    n = B * S
    tm = INPROJ_TOK_BLOCK
    bf16 = jnp.bfloat16
    p0, p1, p2, p3, p4, p5 = SPLIT_POINTS
    w_qkv = w_in[:, 0:p2].astype(bf16)
    w_f = jnp.pad(w_in[:, p2:p3], ((0, 0), (0, LANES - ATT_HEADS))).astype(bf16)
    b_f = jnp.pad(b_forget, (0, LANES - ATT_HEADS)).reshape(1, LANES)
    w_z = w_in[:, p3:p4].astype(bf16)
    w_g = w_in[:, p4:].astype(bf16)
    nt = S // tm
    tok = lambda w: pl.BlockSpec((tm, w), lambda b, i: (b * nt + i, 0))
    full = lambda a: pl.BlockSpec(a.shape, lambda b, i: (0,) * a.ndim)
    args = (x, mod, norm1_g.reshape(1, d), w_qkv, w_f, b_f, w_z, w_g, ln_v_g.reshape(1, GM_WIDTH),
            w_spatial, b_spatial.reshape(GM_GROUPS, GM_CHUNK, 1), w_branch_b.astype(bf16))
    return pl.pallas_call(
        _inproj_kernel,
        grid=(B, nt),
        in_specs=[pl.BlockSpec((1, tm, d), lambda b, i: (b, i, 0)),
                  pl.BlockSpec((1, 6, d), lambda b, i: (b, 0, 0))] + [full(a) for a in args[2:]],
        out_specs=[tok(3 * ATT_WIDTH), tok(LANES),
                   pl.BlockSpec((1, 1, ATT_HEADS, tm), lambda b, i: (b, i, 0, 0)), tok(d), tok(d)],
        out_shape=[jax.ShapeDtypeStruct((n, 3 * ATT_WIDTH), bf16),
                   jax.ShapeDtypeStruct((n, LANES), jnp.float32),
                   jax.ShapeDtypeStruct((B, nt, ATT_HEADS, tm), jnp.float32),
                   jax.ShapeDtypeStruct((n, d), bf16),
                   jax.ShapeDtypeStruct((n, d), bf16)],
        scratch_shapes=[pltpu.VMEM((1, LANES), jnp.float32)],
        compiler_params=pltpu.CompilerParams(dimension_semantics=("arbitrary", "arbitrary"),
                                             vmem_limit_bytes=56 * 1024 * 1024),
        name="input_projection",
    )(*args)


ATT_BLOCK = 512


def _fox_kernel(q_ref, k_ref, v_ref, cum_ref, cumt_ref, o_ref, *, blk, ratio):
    f32 = jnp.float32
    hp = pl.program_id(1)
    i = pl.program_id(2)
    dh = ATT_HEAD_DIM
    q2 = q_ref[...]
    first = lax.broadcasted_iota(jnp.int32, (1, 2 * dh), 1) < dh
    qs = (jnp.where(first, q2, jnp.zeros_like(q2)), jnp.where(first, jnp.zeros_like(q2), q2))
    cum_blk = cum_ref[...]
    lane = lax.broadcasted_iota(jnp.int32, cum_blk.shape, 1)
    cqs = [jnp.sum(jnp.where(lane == 2 * hp + e, cum_blk, 0.0), axis=1, keepdims=True)
           for e in range(2)]

    def block(j, carry, masked):
        off = pl.multiple_of(j * blk, blk)
        k2 = k_ref[0, pl.ds(off, blk), :]
        v2 = v_ref[0, pl.ds(off, blk), :]
        out = []
        for e in range(2):
            m, l, acc = carry[e]
            ck = jnp.concatenate([cumt_ref[0, j * ratio + a, pl.ds(2 * hp + e, 1), :]
                                  for a in range(ratio)], axis=1)
            s = lax.dot_general(qs[e], k2, (((1,), (1,)), ((), ())), preferred_element_type=f32)
            s = s + (cqs[e] - ck)
            if masked:
                causal = (lax.broadcasted_iota(jnp.int32, (blk, blk), 0)
                          >= lax.broadcasted_iota(jnp.int32, (blk, blk), 1))
                s = jnp.where(causal, s, -jnp.inf)
            m_new = jnp.maximum(m, jnp.max(s, axis=1, keepdims=True))
            alpha = jnp.exp(m - m_new)
            p = jnp.exp(s - m_new)
            l = alpha * l + jnp.sum(p, axis=1, keepdims=True)
            acc = alpha * acc + jnp.dot(p.astype(v2.dtype), v2, preferred_element_type=f32)
            out.append((m_new, l, acc))
        return tuple(out)

    one = (jnp.full((blk, 1), -1e30, f32), jnp.zeros((blk, 1), f32), jnp.zeros((blk, 2 * dh), f32))
    carry = lax.fori_loop(0, i, lambda j, c: block(j, c, False), (one, one))
    (_, l0, acc0), (_, l1, acc1) = block(i, carry, True)
    o_ref[...] = jnp.where(first, acc0 / l0, acc1 / l1).astype(o_ref.dtype)


def _fox_attention(qkv, cum, cumt, batch, seq_len):
    n = qkv.shape[0]
    tm = cumt.shape[-1]
    blk = max(min(ATT_BLOCK, seq_len), tm)
    nb = seq_len // blk
    pair = 2 * ATT_HEAD_DIM
    n_pairs = ATT_WIDTH // pair
    qkv3 = qkv.reshape(batch, seq_len, 3 * ATT_WIDTH)
    return pl.pallas_call(
        functools.partial(_fox_kernel, blk=blk, ratio=blk // tm),
        grid=(batch, n_pairs, nb),
        in_specs=[pl.BlockSpec((blk, pair), lambda b, hp, i: (b * nb + i, hp)),
                  pl.BlockSpec((1, seq_len, pair), lambda b, hp, i: (b, 0, n_pairs + hp)),
                  pl.BlockSpec((1, seq_len, pair), lambda b, hp, i: (b, 0, 2 * n_pairs + hp)),
                  pl.BlockSpec((blk, LANES), lambda b, hp, i: (b * nb + i, 0)),
                  pl.BlockSpec((1,) + cumt.shape[1:], lambda b, hp, i: (b, 0, 0, 0))],
        out_specs=pl.BlockSpec((blk, pair), lambda b, hp, i: (b * nb + i, hp)),
        out_shape=jax.ShapeDtypeStruct((n, ATT_WIDTH), jnp.bfloat16),
        compiler_params=pltpu.CompilerParams(
            dimension_semantics=("arbitrary", "arbitrary", "arbitrary")),
        name="fox_attention",
    )(qkv, qkv3, qkv3, cum, cumt)


def _topk_rows(scores, k):
    rows, t = scores[0].shape
    iota = lax.broadcasted_iota(jnp.int32, (rows, t), 0).astype(jnp.float32)
    slot = lax.broadcasted_iota(jnp.int32, (k, t), 0)
    scores = list(scores)
    vals = [jnp.zeros((k, t), jnp.float32) for _ in scores]
    ids = [jnp.zeros((k, t), jnp.float32) for _ in scores]
    for j in range(k):
        for i, s in enumerate(scores):
            m = jnp.max(s, axis=0, keepdims=True)
            am = jnp.min(jnp.where(s == m, iota, float(rows)), axis=0, keepdims=True)
            vals[i] = jnp.where(slot == j, m, vals[i])
            ids[i] = jnp.where(slot == j, am, ids[i])
            scores[i] = jnp.where(iota == am, -jnp.inf, s)
    return [(v, i.astype(jnp.int32)) for v, i in zip(vals, ids)]


def _select_rows(table, pos):
    out = jnp.zeros(pos.shape, table.dtype)
    for r in range(table.shape[0]):
        out = jnp.where(pos == r, table[r:r + 1, :], out)
    return out


def _post_kernel(ya_ref, sga_ref, gb_ref, x_ref, mod_ref, n2g_ref, wa_ref, wo_ref, wq_ref, keys_ref,
                 x1_ref, h2_ref, idx_ref, gates_ref, q_scr):
    f32 = jnp.float32
    a = jnp.dot(ya_ref[...], wa_ref[...], preferred_element_type=f32)
    merged = sga_ref[...].astype(f32) * a + gb_ref[...].astype(f32)
    o = jnp.dot(merged.astype(jnp.bfloat16), wo_ref[...], preferred_element_type=f32)
    g1 = mod_ref[0, 2:3, :]
    sh2 = mod_ref[0, 3:4, :]
    sc2 = mod_ref[0, 4:5, :]
    x1 = x_ref[...] + g1 * o
    x1_ref[...] = x1
    h2 = x1 * lax.rsqrt(jnp.mean(x1 * x1, axis=-1, keepdims=True) + EPS) * n2g_ref[...]
    h2 = h2 * (1.0 + sc2) + sh2
    h2_ref[...] = h2
    qp = jnp.dot(h2.astype(jnp.bfloat16), wq_ref[...], preferred_element_type=f32)
    for j in range(2 * PEER_HEADS):
        q_scr[j] = qp[:, j * PEER_HALF:(j + 1) * PEER_HALF]

    half = PEER_TOPK // 2
    tail0 = PEER_TOPK + (half - 1) * half

    def heads(hg, carry):
        hs = [hg * ROUTE_HEADS + e for e in range(ROUTE_HEADS)]
        scores = []
        for h in hs:
            for p in range(2):
                q = q_scr[2 * h + p]
                keys = keys_ref[2 * h + p]
                scores.append(lax.dot_general(keys, q, (((1,), (1,)), ((), ())),
                                              precision=lax.Precision.HIGHEST,
                                              preferred_element_type=f32))
        tops = _topk_rows(scores, PEER_TOPK)
        cands = []
        for e in range(ROUTE_HEADS):
            (s1, _), (s2, _) = tops[2 * e], tops[2 * e + 1]
            blocks = [s1[0:1, :] + s2]
            blocks += [s1[a:a + 1, :] + s2[0:half, :] for a in range(1, half)]
            blocks += [s1[half:, :] + s2[0:1, :]]
            cands.append(jnp.concatenate(blocks, axis=0))
        best = _topk_rows(cands, PEER_TOPK)
        for e, h in enumerate(hs):
            (_, i1), (_, i2) = tops[2 * e], tops[2 * e + 1]
            vals, pos = best[e]
            mid = pos - PEER_TOPK
            ra = jnp.where(pos < PEER_TOPK, 0,
                           jnp.where(pos < tail0, 1 + (mid >> (half.bit_length() - 1)), pos - tail0 + half))
            rb = jnp.where(pos < PEER_TOPK, pos, jnp.where(pos < tail0, mid & (half - 1), 0))
            eid = _select_rows(i1, ra) * N_KEYS + _select_rows(i2, rb)
            ex = jnp.exp(vals - vals[0:1, :])
            g = ex / jnp.sum(ex, axis=0, keepdims=True)
            row = pl.multiple_of(h * PEER_TOPK, PEER_TOPK)
            idx_ref[pl.ds(row, PEER_TOPK), :] = eid
            gates_ref[pl.ds(row, PEER_TOPK), :] = g
        return carry

    lax.fori_loop(0, PEER_HEADS // ROUTE_HEADS, heads, 0)


POST_TOK_BLOCK = 256
ROUTE_HEADS = 2


def _post_attention(ya, sga, gb, x, mod, norm2_g, w_a, w_out, w_query, sub_keys, seq_len):
    n, d = x.shape
    tm = POST_TOK_BLOCK
    blocks_per_seq = seq_len // tm
    aw = ya.shape[1]
    keys = sub_keys.reshape(2 * PEER_HEADS, N_KEYS, PEER_HALF)
    tok = lambda w: pl.BlockSpec((tm, w), lambda i: (i, 0))
    full = lambda a: pl.BlockSpec(a.shape, lambda i: (0,) * a.ndim)
    return pl.pallas_call(
        _post_kernel,
        grid=(n // tm,),
        in_specs=[tok(aw), tok(d), tok(d), tok(d),
                  pl.BlockSpec((1, 6, d), lambda i: (i // blocks_per_seq, 0, 0)),
                  full(norm2_g), full(w_a), full(w_out), full(w_query), full(keys)],
        out_specs=[tok(d), tok(d),
                   pl.BlockSpec((PEER_SLOTS, tm), lambda i: (0, i)),
                   pl.BlockSpec((PEER_SLOTS, tm), lambda i: (0, i))],
        out_shape=[jax.ShapeDtypeStruct((n, d), jnp.float32),
                   jax.ShapeDtypeStruct((n, d), jnp.float32),
                   jax.ShapeDtypeStruct((PEER_SLOTS, n), jnp.int32),
                   jax.ShapeDtypeStruct((PEER_SLOTS, n), jnp.float32)],
        scratch_shapes=[pltpu.VMEM((2 * PEER_HEADS, tm, PEER_HALF), jnp.float32)],
        compiler_params=pltpu.CompilerParams(dimension_semantics=("arbitrary",),
                                             vmem_limit_bytes=48 * 1024 * 1024),
        name="post_attention",
    )(ya, sga, gb, x, mod, norm2_g, w_a, w_out, w_query, keys)


def _rowsum_bcast(p, ones_bf16):
    hi = p.astype(jnp.bfloat16)
    lo = (p - hi.astype(jnp.float32)).astype(jnp.bfloat16)
    return (jnp.dot(hi, ones_bf16, preferred_element_type=jnp.float32)
            + jnp.dot(lo, ones_bf16, preferred_element_type=jnp.float32))


def _word_halves(w):
    return (lax.bitcast_convert_type(w & jnp.uint32(0xFFFF0000), jnp.float32),
            lax.bitcast_convert_type(w << 16, jnp.float32))


def _eval_experts(chunk, hrow, grow):
    half = hrow.shape[1] // LANES // 2
    ones_bf16 = jnp.ones((LANES, LANES), jnp.bfloat16)
    eye = (lax.broadcasted_iota(jnp.int32, (PEER_SLOTS, LANES), 0)
           == lax.broadcasted_iota(jnp.int32, (PEER_SLOTS, LANES), 1))
    hpart = lambda c: hrow[:, c * LANES:(c + 1) * LANES]
    psum = None
    for c in range(half):
        hi, lo = _word_halves(chunk(c))
        p = hi * hpart(c) + lo * hpart(c + half)
        psum = p if psum is None else psum + p
    act = _gelu(_rowsum_bcast(psum, ones_bf16))
    gcol = _rowsum_bcast(jnp.where(eye, grow, 0.0), ones_bf16)
    coef = gcol * act
    outs_hi, outs_lo = [], []
    for c in range(half):
        hi, lo = _word_halves(chunk(c + half))
        outs_hi.append(jnp.sum(coef * hi, axis=0, keepdims=True))
        outs_lo.append(jnp.sum(coef * lo, axis=0, keepdims=True))
    return jnp.concatenate(outs_hi + outs_lo, axis=-1)


def _finish_block(x_ref, g2_ref, fg_ref, peer, out_ref):
    y = x_ref[...] + g2_ref[0] * peer[...]
    out_ref[...] = y * lax.rsqrt(jnp.mean(y * y, axis=-1, keepdims=True) + EPS) * fg_ref[...]


def _expert_kernel(idx_ref, gates_ref, h_ref, x_ref, g2_ref, fg_ref, tab_ref, out_ref,
                   *scratch, tok_block, n_slots):
    bufs = scratch[:n_slots]
    peer, sem = scratch[n_slots], scratch[n_slots + 1]
    d_model = h_ref.shape[-1]
    n_chunks = d_model // LANES
    rows_per_tok = PEER_SLOTS * n_chunks

    def issue(t, s):
        for r in range(PEER_SLOTS):
            row = pl.multiple_of(idx_ref[t, r] * n_chunks, n_chunks)
            pltpu.make_async_copy(tab_ref.at[pl.ds(row, n_chunks), :],
                                  bufs[s].at[pl.ds(r * n_chunks, n_chunks), :],
                                  sem.at[s]).start(priority=r % DMA_THREADS)

    def wait(s):
        pltpu.make_async_copy(tab_ref.at[pl.ds(0, rows_per_tok), :], bufs[s], sem.at[s]).wait()

    def compute(t, s):
        chunk = lambda c: bufs[s][pl.ds(c, PEER_SLOTS, stride=n_chunks), :]
        peer[pl.ds(t, 1), :] = _eval_experts(chunk, h_ref[pl.ds(t, 1), :], gates_ref[pl.ds(t, 1), :])

    def step(t, s, prefetch):
        wait(s)
        if prefetch:
            issue(t + n_slots - 1, (s - 1) % n_slots)
        compute(t, s)

    for t in range(n_slots - 1):
        issue(t, t)

    n_groups = tok_block // n_slots

    def group(g, carry):
        for s in range(n_slots):
            step(g * n_slots + s, s, True)
        return carry

    lax.fori_loop(0, n_groups - 1, group, 0)
    for s in range(n_slots):
        t = (n_groups - 1) * n_slots + s
        step(t, s, t + n_slots - 1 < tok_block)

    _finish_block(x_ref, g2_ref, fg_ref, peer, out_ref)


def _expert_cost(n_tok, d):
    pairs = n_tok * PEER_SLOTS
    return pl.CostEstimate(flops=4 * pairs * d, transcendentals=pairs,
                           bytes_accessed=4 * pairs * d + 12 * n_tok * d + 8 * pairs)


def _sc_tanh(y):
    return 1.0 - 2.0 / (jnp.exp(2.0 * y) + 1.0)


def _sc_peer_experts(table3, ids, gates, h3):
    n_tok, n_chunks, lanes = h3.shape
    info = plsc.get_sparse_core_info()
    sl = info.num_lanes
    n_workers = info.num_cores * info.num_subcores
    tok_per_worker = n_tok // n_workers
    assert tok_per_worker * n_workers == n_tok and tok_per_worker % 2 == 0
    rows = SC_GATHER_ROWS
    n_sub = PEER_SLOTS // rows
    assert n_sub % 2 == 0
    half = n_chunks // 2
    pieces = [(c, k * sl) for c in range(half) for k in range(lanes // sl)]
    mesh = plsc.VectorSubcoreMesh(core_axis_name="c", subcore_axis_name="s")
    buf = lambda dt: pltpu.VMEM((rows, n_chunks, lanes), dt)

    @functools.partial(
        pl.kernel, mesh=mesh,
        out_type=jax.ShapeDtypeStruct((n_tok, n_chunks, lanes), jnp.float32),
        scratch_types=[pltpu.VMEM((PEER_SLOTS,), jnp.int32), pltpu.VMEM((PEER_SLOTS,), jnp.int32),
                       pltpu.VMEM((PEER_SLOTS,), jnp.float32), pltpu.VMEM((PEER_SLOTS,), jnp.float32),
                       pltpu.VMEM((n_chunks, lanes), jnp.float32),
                       pltpu.VMEM((n_chunks, lanes), jnp.float32),
                       buf(jnp.uint32), buf(jnp.uint32), pltpu.VMEM((n_chunks, lanes), jnp.float32),
                       pltpu.SemaphoreType.DMA((2,)), pltpu.SemaphoreType.DMA((2,))],
        compiler_params=pltpu.CompilerParams(needs_layout_passes=False),
        cost_estimate=_expert_cost(n_tok, n_chunks * lanes),
        name="sc_peer_experts")
    def run(table_hbm, ids_hbm, gates_hbm, h_hbm, peer_hbm, ids0, ids1, g0, g1, h0, h1, rows0, rows1,
            out_v, gsem, isem):
        bufs = (rows0, rows1)
        ids_p, g_p, h_p = (ids0, ids1), (g0, g1), (h0, h1)
        t_first = (lax.axis_index("s") * info.num_cores + lax.axis_index("c")) * tok_per_worker

        def token_inputs(t, p):
            off = pl.multiple_of(t * PEER_SLOTS, PEER_SLOTS)
            return (pltpu.make_async_copy(ids_hbm.at[pl.ds(off, PEER_SLOTS)], ids_p[p], isem.at[p]),
                    pltpu.make_async_copy(gates_hbm.at[pl.ds(off, PEER_SLOTS)], g_p[p], isem.at[p]),
                    pltpu.make_async_copy(h_hbm.at[t], h_p[p], isem.at[p]))

        def gather(p, s):
            return pltpu.make_async_copy(table_hbm.at[ids_p[p].at[pl.ds(s * rows, rows)]],
                                         bufs[s % 2], gsem.at[s % 2])

        for cp in token_inputs(t_first, 0):
            cp.start()
        for cp in token_inputs(t_first, 0):
            cp.wait()
        gather(0, 0).start()

        def one_token(ti, p):
            t = t_first + ti
            g_v, h_v = g_p[p], h_p[p]
            has_next = ti + 1 < tok_per_worker

            @pl.when(has_next)
            def _():
                for cp in token_inputs(t + 1, 1 - p):
                    cp.start()

            for c, k in pieces:
                out_v[c, pl.ds(k, sl)] = jnp.zeros((sl,), jnp.float32)
                out_v[c + half, pl.ds(k, sl)] = jnp.zeros((sl,), jnp.float32)
            for s in range(n_sub):
                gather(p, s).wait()
                if s + 1 < n_sub:
                    gather(p, s + 1).start()
                else:
                    @pl.when(has_next)
                    def _():
                        for cp in token_inputs(t + 1, 1 - p):
                            cp.wait()
                        gather(1 - p, 0).start()
                rbuf = bufs[s % 2]

                @pl.loop(0, rows, step=SC_ROW_GROUP)
                def _(r0):
                    accs = [jnp.zeros((sl,), jnp.float32) for _ in range(SC_ROW_GROUP)]
                    for c, k in pieces:
                        h_hi, h_lo = h_v[c, pl.ds(k, sl)], h_v[c + half, pl.ds(k, sl)]
                        for j in range(SC_ROW_GROUP):
                            hi, lo = _word_halves(rbuf[r0 + j, c, pl.ds(k, sl)])
                            accs[j] = accs[j] + hi * h_hi + lo * h_lo
                    coefs = []
                    for j in range(SC_ROW_GROUP):
                        a = jnp.broadcast_to(jnp.sum(accs[j]), (sl,))
                        act = 0.5 * a * (1.0 + _sc_tanh(0.7978845608028654 * (a + 0.044715 * (a * a * a))))
                        gate = plsc.load_gather(g_v, [jnp.broadcast_to(s * rows + r0 + j, (sl,))])
                        coefs.append(gate * act)
                    for b0 in range(0, len(pieces), SC_STORE_BATCH):
                        tots = []
                        for c, k in pieces[b0:b0 + SC_STORE_BATCH]:
                            t_hi = t_lo = None
                            for j in range(SC_ROW_GROUP):
                                hi, lo = _word_halves(rbuf[r0 + j, c + half, pl.ds(k, sl)])
                                t_hi = coefs[j] * hi if t_hi is None else t_hi + coefs[j] * hi
                                t_lo = coefs[j] * lo if t_lo is None else t_lo + coefs[j] * lo
                            tots.append((t_hi, t_lo))
                        for (c, k), (t_hi, t_lo) in zip(pieces[b0:b0 + SC_STORE_BATCH], tots):
                            plsc.addupdate(out_v.at[c, pl.ds(k, sl)], t_hi)
                            plsc.addupdate(out_v.at[c + half, pl.ds(k, sl)], t_lo)

            pltpu.sync_copy(out_v, peer_hbm.at[t])

        @pl.loop(0, tok_per_worker, step=2)
        def _(ti):
            one_token(ti, 0)
            one_token(ti + 1, 1)

    return run(table3, ids, gates, h3)


def _finish_kernel(peer_ref, x_ref, g2_ref, fg_ref, prev_ref, out_ref):
    del prev_ref
    _finish_block(x_ref, g2_ref, fg_ref, peer_ref, out_ref)


def _finish_tokens(peer, x1, g2, fg, prev, seq_len):
    d = x1.shape[1]
    tm = FINISH_TOK_BLOCK
    return pl.pallas_call(
        _finish_kernel,
        grid=(peer.shape[0] // tm,),
        in_specs=[pl.BlockSpec((tm, d), lambda i: (i, 0)),
                  pl.BlockSpec((tm, d), lambda i: (i, 0)),
                  pl.BlockSpec((1, 1, d), lambda i: (i * tm // seq_len, 0, 0)),
                  pl.BlockSpec((1, d), lambda i: (0, 0)),
                  pl.BlockSpec(memory_space=pl.ANY)],
        out_specs=pl.BlockSpec((tm, d), lambda i: (i, 0)),
        out_shape=jax.ShapeDtypeStruct(prev.shape, prev.dtype),
        input_output_aliases={4: 0},
        compiler_params=pltpu.CompilerParams(dimension_semantics=("arbitrary",)),
        name="finish_tokens",
    )(peer, x1, g2, fg, prev)


def _pack_expert_table(expert_u, expert_v):
    n_experts, d = expert_u.shape
    bits = lambda a: lax.bitcast_convert_type(a.astype(jnp.bfloat16), jnp.uint16).astype(jnp.uint32)
    pair = lambda b: (b[:, :d // 2] << 16) | b[:, d // 2:]
    words = jnp.concatenate([pair(bits(expert_u)), pair(bits(expert_v))], axis=1)
    return words.reshape(n_experts, d // LANES, LANES)


def _tc_peer_experts(idx, gates, h2, x1, g2, final_g, table, seq_len, first_tok):
    n, d = h2.shape
    m = n - first_tok
    n_chunks = d // LANES
    tb = EXPERT_TOK_BLOCK
    assert tb % EXPERT_SLOTS == 0 and seq_len % tb == 0 and first_tok % tb == 0
    first = first_tok // tb
    return pl.pallas_call(
        functools.partial(_expert_kernel, tok_block=tb, n_slots=EXPERT_SLOTS),
        grid=(m // tb,),
        in_specs=[
            pl.BlockSpec((tb, PEER_SLOTS), lambda i: (i + first, 0), memory_space=pltpu.SMEM),
            pl.BlockSpec((tb, PEER_SLOTS), lambda i: (i + first, 0)),
            pl.BlockSpec((tb, d), lambda i: (i + first, 0)),
            pl.BlockSpec((tb, d), lambda i: (i + first, 0)),
            pl.BlockSpec((1, 1, d), lambda i: ((i + first) * tb // seq_len, 0, 0)),
            pl.BlockSpec((1, d), lambda i: (0, 0)),
            pl.BlockSpec(memory_space=pl.ANY),
        ],
        out_specs=pl.BlockSpec((tb, d), lambda i: (i + first, 0)),
        out_shape=jax.ShapeDtypeStruct((n, d), jnp.float32),
        scratch_shapes=(
            [pltpu.VMEM((PEER_SLOTS * n_chunks, LANES), jnp.uint32) for _ in range(EXPERT_SLOTS)]
            + [pltpu.VMEM((tb, d), jnp.float32), pltpu.SemaphoreType.DMA((EXPERT_SLOTS,))]),
        compiler_params=pltpu.CompilerParams(dimension_semantics=("arbitrary",)),
        cost_estimate=_expert_cost(m, d),
        name="peer_experts",
    )(idx, gates, h2, x1, g2, final_g.reshape(1, d), table.reshape(-1, LANES))


def _mix_and_route(x, mod, norm1_g, w_in, b_forget, ln_v_g, w_spatial, b_spatial, w_branch_a,
                   w_branch_b, w_out, norm2_g, w_query, sub_keys):
    B, S, D = x.shape
    n = B * S
    bf16 = jnp.bfloat16
    qkv, cum, cumt, sga, gb = _input_projection(x, mod, norm1_g, w_in, b_forget, ln_v_g,
                                                w_spatial, b_spatial, w_branch_b)
    y_a = _fox_attention(qkv, cum, cumt, B, S)
    x1, h2, idx_t, gates_t = _post_attention(
        y_a, sga, gb, x.reshape(n, D), mod, norm2_g.reshape(1, D),
        w_branch_a.astype(bf16), w_out.astype(bf16), w_query.astype(bf16), sub_keys, S)
    return x1, h2, idx_t.T, gates_t.T


def kernel(x, c, w_mod, b_mod, norm1_g, w_in, b_forget, ln_v_g, w_spatial, b_spatial, w_branch_a, w_branch_b, w_out, norm2_g, w_query, sub_keys, expert_u, expert_v, final_g):
    B, S, D = x.shape
    n = B * S
    assert w_mod.shape[0] == 1, "the final RMSNorm is fused into the single layer's expert kernels"
    l = 0
    mod = _modulation(c, w_mod[l], b_mod[l]).reshape(B, 6, D)
    table = _pack_expert_table(expert_u[l], expert_v[l])
    n_chunks = D // LANES
    weights = (norm1_g[l], w_in[l], b_forget[l], ln_v_g[l], w_spatial[l], b_spatial[l],
               w_branch_a[l], w_branch_b[l], w_out[l], norm2_g[l], w_query[l], sub_keys[l])

    x1, h2, idx, gates = _mix_and_route(x, mod, *weights)
    g2 = mod[:, 5:6, :]

    n_sc = n * SC_SHARE_PERCENT // 100 // SC_SHARE_ALIGN * SC_SHARE_ALIGN
    out = _tc_peer_experts(idx, gates, h2, x1, g2, final_g, table, S, n_sc)
    if n_sc > 0:
        peer_sc = _sc_peer_experts(table, idx[:n_sc].reshape(-1), gates[:n_sc].reshape(-1),
                                   h2[:n_sc].reshape(n_sc, n_chunks, LANES))
        out = _finish_tokens(peer_sc.reshape(n_sc, D), x1, g2, final_g.reshape(1, D), out, S)
    return out.reshape(B, S, D)
```

```python
import functools

import jax
import jax.numpy as jnp
from jax import lax
from jax.experimental import pallas as pl
from jax.experimental.pallas import tpu as pltpu
from jax.experimental.pallas import tpu_sc as plsc

D_MODEL = 1024
ATT_HEADS = 8
ATT_HEAD_DIM = 64
ATT_WIDTH = ATT_HEADS * ATT_HEAD_DIM
GM_GROUPS = 4
GM_GROUP_DIM = 128
GM_WIDTH = GM_GROUPS * GM_GROUP_DIM
GM_CHUNK = 128
PEER_HEADS = 8
PEER_KEY_DIM = 256
PEER_HALF = PEER_KEY_DIM // 2
N_KEYS = 128
PEER_TOPK = 16
PEER_SLOTS = PEER_HEADS * PEER_TOPK
SPLIT_POINTS = (ATT_WIDTH, 2 * ATT_WIDTH, 3 * ATT_WIDTH, 3 * ATT_WIDTH + ATT_HEADS,
                3 * ATT_WIDTH + ATT_HEADS + 2 * GM_WIDTH,
                3 * ATT_WIDTH + ATT_HEADS + 2 * GM_WIDTH + D_MODEL)
EPS = 1e-6

LANES = 128
EXPERT_TOK_BLOCK = 256
EXPERT_SLOTS = 8
DMA_THREADS = 2
FINISH_TOK_BLOCK = 512
SC_GATHER_ROWS = 32
SC_ROW_GROUP = 4
SC_STORE_BATCH = 8
SC_SHARE_PERCENT = 48
SC_SHARE_ALIGN = 512


def _gelu(x):
    return 0.5 * x * (1.0 + jnp.tanh(0.7978845608028654 * (x + 0.044715 * (x * x * x))))


def _mod_kernel(c_ref, w_ref, b_ref, o_ref):
    c = c_ref[...]
    sc = c * jax.nn.sigmoid(c)
    o_ref[...] = jnp.dot(sc, w_ref[...], precision=lax.Precision.HIGHEST,
                         preferred_element_type=jnp.float32) + b_ref[...]


def _modulation(c, w_mod, b_mod):
    b, d = c.shape
    cols = w_mod.shape[1]
    return pl.pallas_call(
        _mod_kernel,
        grid=(cols // d,),
        in_specs=[pl.BlockSpec((b, d), lambda j: (0, 0)),
                  pl.BlockSpec((d, d), lambda j: (0, j)),
                  pl.BlockSpec((1, d), lambda j: (0, j))],
        out_specs=pl.BlockSpec((b, d), lambda j: (0, j)),
        out_shape=jax.ShapeDtypeStruct((b, cols), jnp.float32),
        name="modulation",
    )(c, w_mod, b_mod.reshape(1, cols))


INPROJ_TOK_BLOCK = 256


def _inproj_kernel(x_ref, mod_ref, n1g_ref, wqkv_ref, wf_ref, bf_ref, wz_ref, wg_ref, lng_ref,
                   wsp_ref, bsp_ref, wb_ref, qkv_ref, cum_ref, cumt_ref, sga_ref, gb_ref, carry):
    f32, bf16 = jnp.float32, jnp.bfloat16
    tm, d = x_ref.shape[1], x_ref.shape[2]
    x = x_ref[0]
    sh1 = mod_ref[0, 0:1, :]
    sc1 = mod_ref[0, 1:2, :]
    h = x * lax.rsqrt(jnp.mean(x * x, axis=-1, keepdims=True) + EPS) * n1g_ref[...]
    hb = (h * (1.0 + sc1) + sh1).astype(bf16)

    qkv = jnp.dot(hb, wqkv_ref[...], preferred_element_type=f32)
    qkv_ref[:, 0:ATT_WIDTH] = (qkv[:, 0:ATT_WIDTH] * (ATT_HEAD_DIM ** -0.5)).astype(bf16)
    qkv_ref[:, ATT_WIDTH:] = qkv[:, ATT_WIDTH:].astype(bf16)

    f = jnp.dot(hb, wf_ref[...], preferred_element_type=f32) + bf_ref[...]
    logf = jnp.minimum(f, 0.0) - jnp.log1p(jnp.exp(-jnp.abs(f)))

    @pl.when(pl.program_id(1) == 0)
    def _():
        carry[...] = jnp.zeros_like(carry)

    tri = (lax.broadcasted_iota(jnp.int32, (tm, tm), 0)
           >= lax.broadcasted_iota(jnp.int32, (tm, tm), 1)).astype(f32)
    cum = jnp.dot(tri, logf, precision=lax.Precision.HIGHEST, preferred_element_type=f32) + carry[...]
    cum_ref[...] = cum
    cumt_ref[0, 0] = jnp.transpose(cum)[0:ATT_HEADS, :]
    carry[...] = cum[tm - 1:tm, :]

    gz = _gelu(jnp.dot(hb, wz_ref[...], preferred_element_type=f32))
    u = gz[:, 0:GM_WIDTH]
    v = gz[:, GM_WIDTH:]
    mu = jnp.mean(v, axis=-1, keepdims=True)
    var = jnp.mean(jnp.square(v - mu), axis=-1, keepdims=True)
    vn = ((v - mu) * lax.rsqrt(var + EPS) * lng_ref[...]).astype(bf16)
    tril = (lax.broadcasted_iota(jnp.int32, (GM_CHUNK, GM_CHUNK), 0)
            >= lax.broadcasted_iota(jnp.int32, (GM_CHUNK, GM_CHUNK), 1))
    w_sp = [jnp.where(tril, wsp_ref[g], 0.0).astype(bf16) for g in range(GM_GROUPS)]
    rows = []
    for ck in range(tm // GM_CHUNK):
        r0 = ck * GM_CHUNK
        cols = []
        for g in range(GM_GROUPS):
            c0 = g * GM_GROUP_DIM
            mixed = jnp.dot(w_sp[g], vn[r0:r0 + GM_CHUNK, c0:c0 + GM_GROUP_DIM],
                            preferred_element_type=f32) + bsp_ref[g]
            cols.append(u[r0:r0 + GM_CHUNK, c0:c0 + GM_GROUP_DIM] * mixed)
        rows.append(jnp.concatenate(cols, axis=1))
    yb = jnp.concatenate(rows, axis=0).astype(bf16)
    ybp = jnp.dot(yb, wb_ref[...], preferred_element_type=f32)

    sg = jax.nn.sigmoid(jnp.dot(hb, wg_ref[...], preferred_element_type=f32))
    sga_ref[...] = sg[:, 0:d].astype(bf16)
    gb_ref[...] = (sg[:, d:] * ybp).astype(bf16)


def _input_projection(x, mod, norm1_g, w_in, b_forget, ln_v_g, w_spatial, b_spatial, w_branch_b):
    B, S, d = x.shape
    n = B * S
    tm = INPROJ_TOK_BLOCK
    bf16 = jnp.bfloat16
    p0, p1, p2, p3, p4, p5 = SPLIT_POINTS
    w_qkv = w_in[:, 0:p2].astype(bf16)
    w_f = jnp.pad(w_in[:, p2:p3], ((0, 0), (0, LANES - ATT_HEADS))).astype(bf16)
    b_f = jnp.pad(b_forget, (0, LANES - ATT_HEADS)).reshape(1, LANES)
    w_z = w_in[:, p3:p4].astype(bf16)
    w_g = w_in[:, p4:].astype(bf16)
    nt = S // tm
    tok = lambda w: pl.BlockSpec((tm, w), lambda b, i: (b * nt + i, 0))
    full = lambda a: pl.BlockSpec(a.shape, lambda b, i: (0,) * a.ndim)
    args = (x, mod, norm1_g.reshape(1, d), w_qkv, w_f, b_f, w_z, w_g, ln_v_g.reshape(1, GM_WIDTH),
            w_spatial, b_spatial.reshape(GM_GROUPS, GM_CHUNK, 1), w_branch_b.astype(bf16))
    return pl.pallas_call(
        _inproj_kernel,
        grid=(B, nt),
        in_specs=[pl.BlockSpec((1, tm, d), lambda b, i: (b, i, 0)),
                  pl.BlockSpec((1, 6, d), lambda b, i: (b, 0, 0))] + [full(a) for a in args[2:]],
        out_specs=[tok(3 * ATT_WIDTH), tok(LANES),
                   pl.BlockSpec((1, 1, ATT_HEADS, tm), lambda b, i: (b, i, 0, 0)), tok(d), tok(d)],
        out_shape=[jax.ShapeDtypeStruct((n, 3 * ATT_WIDTH), bf16),
                   jax.ShapeDtypeStruct((n, LANES), jnp.float32),
                   jax.ShapeDtypeStruct((B, nt, ATT_HEADS, tm), jnp.float32),
                   jax.ShapeDtypeStruct((n, d), bf16),
                   jax.ShapeDtypeStruct((n, d), bf16)],
        scratch_shapes=[pltpu.VMEM((1, LANES), jnp.float32)],
        compiler_params=pltpu.CompilerParams(dimension_semantics=("arbitrary", "arbitrary"),
                                             vmem_limit_bytes=56 * 1024 * 1024),
        name="input_projection",
    )(*args)


ATT_BLOCK = 512


def _fox_kernel(q_ref, k_ref, v_ref, cum_ref, cumt_ref, o_ref, *, blk, ratio):
    f32 = jnp.float32
    hp = pl.program_id(1)
    i = pl.program_id(2)
    dh = ATT_HEAD_DIM
    q2 = q_ref[...]
    first = lax.broadcasted_iota(jnp.int32, (1, 2 * dh), 1) < dh
    qs = (jnp.where(first, q2, jnp.zeros_like(q2)), jnp.where(first, jnp.zeros_like(q2), q2))
    cum_blk = cum_ref[...]
    lane = lax.broadcasted_iota(jnp.int32, cum_blk.shape, 1)
    cqs = [jnp.sum(jnp.where(lane == 2 * hp + e, cum_blk, 0.0), axis=1, keepdims=True)
           for e in range(2)]

    def block(j, carry, masked):
        off = pl.multiple_of(j * blk, blk)
        k2 = k_ref[0, pl.ds(off, blk), :]
        v2 = v_ref[0, pl.ds(off, blk), :]
        out = []
        for e in range(2):
            m, l, acc = carry[e]
            ck = jnp.concatenate([cumt_ref[0, j * ratio + a, pl.ds(2 * hp + e, 1), :]
                                  for a in range(ratio)], axis=1)
            s = lax.dot_general(qs[e], k2, (((1,), (1,)), ((), ())), preferred_element_type=f32)
            s = s + (cqs[e] - ck)
            if masked:
                causal = (lax.broadcasted_iota(jnp.int32, (blk, blk), 0)
                          >= lax.broadcasted_iota(jnp.int32, (blk, blk), 1))
                s = jnp.where(causal, s, -jnp.inf)
            m_new = jnp.maximum(m, jnp.max(s, axis=1, keepdims=True))
            alpha = jnp.exp(m - m_new)
            p = jnp.exp(s - m_new)
            l = alpha * l + jnp.sum(p, axis=1, keepdims=True)
            acc = alpha * acc + jnp.dot(p.astype(v2.dtype), v2, preferred_element_type=f32)
            out.append((m_new, l, acc))
        return tuple(out)

    one = (jnp.full((blk, 1), -1e30, f32), jnp.zeros((blk, 1), f32), jnp.zeros((blk, 2 * dh), f32))
    carry = lax.fori_loop(0, i, lambda j, c: block(j, c, False), (one, one))
    (_, l0, acc0), (_, l1, acc1) = block(i, carry, True)
    o_ref[...] = jnp.where(first, acc0 / l0, acc1 / l1).astype(o_ref.dtype)


def _fox_attention(qkv, cum, cumt, batch, seq_len):
    n = qkv.shape[0]
    tm = cumt.shape[-1]
    blk = max(min(ATT_BLOCK, seq_len), tm)
    nb = seq_len // blk
    pair = 2 * ATT_HEAD_DIM
    n_pairs = ATT_WIDTH // pair
    qkv3 = qkv.reshape(batch, seq_len, 3 * ATT_WIDTH)
    return pl.pallas_call(
        functools.partial(_fox_kernel, blk=blk, ratio=blk // tm),
        grid=(batch, n_pairs, nb),
        in_specs=[pl.BlockSpec((blk, pair), lambda b, hp, i: (b * nb + i, hp)),
                  pl.BlockSpec((1, seq_len, pair), lambda b, hp, i: (b, 0, n_pairs + hp)),
                  pl.BlockSpec((1, seq_len, pair), lambda b, hp, i: (b, 0, 2 * n_pairs + hp)),
                  pl.BlockSpec((blk, LANES), lambda b, hp, i: (b * nb + i, 0)),
                  pl.BlockSpec((1,) + cumt.shape[1:], lambda b, hp, i: (b, 0, 0, 0))],
        out_specs=pl.BlockSpec((blk, pair), lambda b, hp, i: (b * nb + i, hp)),
        out_shape=jax.ShapeDtypeStruct((n, ATT_WIDTH), jnp.bfloat16),
        compiler_params=pltpu.CompilerParams(
            dimension_semantics=("arbitrary", "arbitrary", "arbitrary")),
        name="fox_attention",
    )(qkv, qkv3, qkv3, cum, cumt)


def _topk_rows(scores, k):
    rows, t = scores[0].shape
    iota = lax.broadcasted_iota(jnp.int32, (rows, t), 0).astype(jnp.float32)
    slot = lax.broadcasted_iota(jnp.int32, (k, t), 0)
    scores = list(scores)
    vals = [jnp.zeros((k, t), jnp.float32) for _ in scores]
    ids = [jnp.zeros((k, t), jnp.float32) for _ in scores]
    for j in range(k):
        for i, s in enumerate(scores):
            m = jnp.max(s, axis=0, keepdims=True)
            am = jnp.min(jnp.where(s == m, iota, float(rows)), axis=0, keepdims=True)
            vals[i] = jnp.where(slot == j, m, vals[i])
            ids[i] = jnp.where(slot == j, am, ids[i])
            scores[i] = jnp.where(iota == am, -jnp.inf, s)
    return [(v, i.astype(jnp.int32)) for v, i in zip(vals, ids)]


def _select_rows(table, pos):
    out = jnp.zeros(pos.shape, table.dtype)
    for r in range(table.shape[0]):
        out = jnp.where(pos == r, table[r:r + 1, :], out)
    return out


def _post_kernel(ya_ref, sga_ref, gb_ref, x_ref, mod_ref, n2g_ref, wa_ref, wo_ref, wq_ref, keys_ref,
                 x1_ref, h2_ref, idx_ref, gates_ref, q_scr):
    f32 = jnp.float32
    a = jnp.dot(ya_ref[...], wa_ref[...], preferred_element_type=f32)
    merged = sga_ref[...].astype(f32) * a + gb_ref[...].astype(f32)
    o = jnp.dot(merged.astype(jnp.bfloat16), wo_ref[...], preferred_element_type=f32)
    g1 = mod_ref[0, 2:3, :]
    sh2 = mod_ref[0, 3:4, :]
    sc2 = mod_ref[0, 4:5, :]
    x1 = x_ref[...] + g1 * o
    x1_ref[...] = x1
    h2 = x1 * lax.rsqrt(jnp.mean(x1 * x1, axis=-1, keepdims=True) + EPS) * n2g_ref[...]
    h2 = h2 * (1.0 + sc2) + sh2
    h2_ref[...] = h2
    qp = jnp.dot(h2.astype(jnp.bfloat16), wq_ref[...], preferred_element_type=f32)
    for j in range(2 * PEER_HEADS):
        q_scr[j] = qp[:, j * PEER_HALF:(j + 1) * PEER_HALF]

    half = PEER_TOPK // 2
    tail0 = PEER_TOPK + (half - 1) * half

    def heads(hg, carry):
        hs = [hg * ROUTE_HEADS + e for e in range(ROUTE_HEADS)]
        scores = []
        for h in hs:
            for p in range(2):
                q = q_scr[2 * h + p]
                keys = keys_ref[2 * h + p]
                scores.append(lax.dot_general(keys, q, (((1,), (1,)), ((), ())),
                                              precision=lax.Precision.HIGHEST,
                                              preferred_element_type=f32))
        tops = _topk_rows(scores, PEER_TOPK)
        cands = []
        for e in range(ROUTE_HEADS):
            (s1, _), (s2, _) = tops[2 * e], tops[2 * e + 1]
            blocks = [s1[0:1, :] + s2]
            blocks += [s1[a:a + 1, :] + s2[0:half, :] for a in range(1, half)]
            blocks += [s1[half:, :] + s2[0:1, :]]
            cands.append(jnp.concatenate(blocks, axis=0))
        best = _topk_rows(cands, PEER_TOPK)
        for e, h in enumerate(hs):
            (_, i1), (_, i2) = tops[2 * e], tops[2 * e + 1]
            vals, pos = best[e]
            mid = pos - PEER_TOPK
            ra = jnp.where(pos < PEER_TOPK, 0,
                           jnp.where(pos < tail0, 1 + (mid >> (half.bit_length() - 1)), pos - tail0 + half))
            rb = jnp.where(pos < PEER_TOPK, pos, jnp.where(pos < tail0, mid & (half - 1), 0))
            eid = _select_rows(i1, ra) * N_KEYS + _select_rows(i2, rb)
            ex = jnp.exp(vals - vals[0:1, :])
            g = ex / jnp.sum(ex, axis=0, keepdims=True)
            row = pl.multiple_of(h * PEER_TOPK, PEER_TOPK)
            idx_ref[pl.ds(row, PEER_TOPK), :] = eid
            gates_ref[pl.ds(row, PEER_TOPK), :] = g
        return carry

    lax.fori_loop(0, PEER_HEADS // ROUTE_HEADS, heads, 0)


POST_TOK_BLOCK = 256
ROUTE_HEADS = 4


def _post_attention(ya, sga, gb, x, mod, norm2_g, w_a, w_out, w_query, sub_keys, seq_len):
    n, d = x.shape
    tm = POST_TOK_BLOCK
    blocks_per_seq = seq_len // tm
    aw = ya.shape[1]
    keys = sub_keys.reshape(2 * PEER_HEADS, N_KEYS, PEER_HALF)
    tok = lambda w: pl.BlockSpec((tm, w), lambda i: (i, 0))
    full = lambda a: pl.BlockSpec(a.shape, lambda i: (0,) * a.ndim)
    return pl.pallas_call(
        _post_kernel,
        grid=(n // tm,),
        in_specs=[tok(aw), tok(d), tok(d), tok(d),
                  pl.BlockSpec((1, 6, d), lambda i: (i // blocks_per_seq, 0, 0)),
                  full(norm2_g), full(w_a), full(w_out), full(w_query), full(keys)],
        out_specs=[tok(d), tok(d),
                   pl.BlockSpec((PEER_SLOTS, tm), lambda i: (0, i)),
                   pl.BlockSpec((PEER_SLOTS, tm), lambda i: (0, i))],
        out_shape=[jax.ShapeDtypeStruct((n, d), jnp.float32),
                   jax.ShapeDtypeStruct((n, d), jnp.float32),
                   jax.ShapeDtypeStruct((PEER_SLOTS, n), jnp.int32),
                   jax.ShapeDtypeStruct((PEER_SLOTS, n), jnp.float32)],
        scratch_shapes=[pltpu.VMEM((2 * PEER_HEADS, tm, PEER_HALF), jnp.float32)],
        compiler_params=pltpu.CompilerParams(dimension_semantics=("arbitrary",),
                                             vmem_limit_bytes=48 * 1024 * 1024),
        name="post_attention",
    )(ya, sga, gb, x, mod, norm2_g, w_a, w_out, w_query, keys)


def _rowsum_bcast(p, ones_bf16):
    hi = p.astype(jnp.bfloat16)
    lo = (p - hi.astype(jnp.float32)).astype(jnp.bfloat16)
    return (jnp.dot(hi, ones_bf16, preferred_element_type=jnp.float32)
            + jnp.dot(lo, ones_bf16, preferred_element_type=jnp.float32))


def _word_halves(w):
    return (lax.bitcast_convert_type(w & jnp.uint32(0xFFFF0000), jnp.float32),
            lax.bitcast_convert_type(w << 16, jnp.float32))


def _eval_experts(chunk, hrow, grow):
    half = hrow.shape[1] // LANES // 2
    ones_bf16 = jnp.ones((LANES, LANES), jnp.bfloat16)
    eye = (lax.broadcasted_iota(jnp.int32, (PEER_SLOTS, LANES), 0)
           == lax.broadcasted_iota(jnp.int32, (PEER_SLOTS, LANES), 1))
    hpart = lambda c: hrow[:, c * LANES:(c + 1) * LANES]
    psum = None
    for c in range(half):
        hi, lo = _word_halves(chunk(c))
        p = hi * hpart(c) + lo * hpart(c + half)
        psum = p if psum is None else psum + p
    act = _gelu(_rowsum_bcast(psum, ones_bf16))
    gcol = _rowsum_bcast(jnp.where(eye, grow, 0.0), ones_bf16)
    coef = gcol * act
    outs_hi, outs_lo = [], []
    for c in range(half):
        hi, lo = _word_halves(chunk(c + half))
        outs_hi.append(jnp.sum(coef * hi, axis=0, keepdims=True))
        outs_lo.append(jnp.sum(coef * lo, axis=0, keepdims=True))
    return jnp.concatenate(outs_hi + outs_lo, axis=-1)


def _finish_block(x_ref, g2_ref, fg_ref, peer, out_ref):
    y = x_ref[...] + g2_ref[0] * peer[...]
    out_ref[...] = y * lax.rsqrt(jnp.mean(y * y, axis=-1, keepdims=True) + EPS) * fg_ref[...]


def _expert_kernel(idx_ref, gates_ref, h_ref, x_ref, g2_ref, fg_ref, tab_ref, out_ref,
                   *scratch, tok_block, n_slots):
    bufs = scratch[:n_slots]
    peer, sem = scratch[n_slots], scratch[n_slots + 1]
    d_model = h_ref.shape[-1]
    n_chunks = d_model // LANES
    rows_per_tok = PEER_SLOTS * n_chunks

    def issue(t, s):
        for r in range(PEER_SLOTS):
            row = pl.multiple_of(idx_ref[t, r] * n_chunks, n_chunks)
            pltpu.make_async_copy(tab_ref.at[pl.ds(row, n_chunks), :],
                                  bufs[s].at[pl.ds(r * n_chunks, n_chunks), :],
                                  sem.at[s]).start(priority=r % DMA_THREADS)

    def wait(s):
        pltpu.make_async_copy(tab_ref.at[pl.ds(0, rows_per_tok), :], bufs[s], sem.at[s]).wait()

    def compute(t, s):
        chunk = lambda c: bufs[s][pl.ds(c, PEER_SLOTS, stride=n_chunks), :]
        peer[pl.ds(t, 1), :] = _eval_experts(chunk, h_ref[pl.ds(t, 1), :], gates_ref[pl.ds(t, 1), :])

    def step(t, s, prefetch):
        wait(s)
        if prefetch:
            issue(t + n_slots - 1, (s - 1) % n_slots)
        compute(t, s)

    for t in range(n_slots - 1):
        issue(t, t)

    n_groups = tok_block // n_slots

    def group(g, carry):
        for s in range(n_slots):
            step(g * n_slots + s, s, True)
        return carry

    lax.fori_loop(0, n_groups - 1, group, 0)
    for s in range(n_slots):
        t = (n_groups - 1) * n_slots + s
        step(t, s, t + n_slots - 1 < tok_block)

    _finish_block(x_ref, g2_ref, fg_ref, peer, out_ref)


def _expert_cost(n_tok, d):
    pairs = n_tok * PEER_SLOTS
    return pl.CostEstimate(flops=4 * pairs * d, transcendentals=pairs,
                           bytes_accessed=4 * pairs * d + 12 * n_tok * d + 8 * pairs)


def _sc_tanh(y):
    return 1.0 - 2.0 / (jnp.exp(2.0 * y) + 1.0)


def _sc_peer_experts(table3, ids, gates, h3):
    n_tok, n_chunks, lanes = h3.shape
    info = plsc.get_sparse_core_info()
    sl = info.num_lanes
    n_workers = info.num_cores * info.num_subcores
    tok_per_worker = n_tok // n_workers
    assert tok_per_worker * n_workers == n_tok and tok_per_worker % 2 == 0
    rows = SC_GATHER_ROWS
    n_sub = PEER_SLOTS // rows
    assert n_sub % 2 == 0
    half = n_chunks // 2
    pieces = [(c, k * sl) for c in range(half) for k in range(lanes // sl)]
    mesh = plsc.VectorSubcoreMesh(core_axis_name="c", subcore_axis_name="s")
    buf = lambda dt: pltpu.VMEM((rows, n_chunks, lanes), dt)

    @functools.partial(
        pl.kernel, mesh=mesh,
        out_type=jax.ShapeDtypeStruct((n_tok, n_chunks, lanes), jnp.float32),
        scratch_types=[pltpu.VMEM((PEER_SLOTS,), jnp.int32), pltpu.VMEM((PEER_SLOTS,), jnp.int32),
                       pltpu.VMEM((PEER_SLOTS,), jnp.float32), pltpu.VMEM((PEER_SLOTS,), jnp.float32),
                       pltpu.VMEM((n_chunks, lanes), jnp.float32),
                       pltpu.VMEM((n_chunks, lanes), jnp.float32),
                       buf(jnp.uint32), buf(jnp.uint32), pltpu.VMEM((n_chunks, lanes), jnp.float32),
                       pltpu.SemaphoreType.DMA((2,)), pltpu.SemaphoreType.DMA((2,))],
        compiler_params=pltpu.CompilerParams(needs_layout_passes=False),
        cost_estimate=_expert_cost(n_tok, n_chunks * lanes),
        name="sc_peer_experts")
    def run(table_hbm, ids_hbm, gates_hbm, h_hbm, peer_hbm, ids0, ids1, g0, g1, h0, h1, rows0, rows1,
            out_v, gsem, isem):
        bufs = (rows0, rows1)
        ids_p, g_p, h_p = (ids0, ids1), (g0, g1), (h0, h1)
        t_first = (lax.axis_index("s") * info.num_cores + lax.axis_index("c")) * tok_per_worker

        def token_inputs(t, p):
            off = pl.multiple_of(t * PEER_SLOTS, PEER_SLOTS)
            return (pltpu.make_async_copy(ids_hbm.at[pl.ds(off, PEER_SLOTS)], ids_p[p], isem.at[p]),
                    pltpu.make_async_copy(gates_hbm.at[pl.ds(off, PEER_SLOTS)], g_p[p], isem.at[p]),
                    pltpu.make_async_copy(h_hbm.at[t], h_p[p], isem.at[p]))

        def gather(p, s):
            return pltpu.make_async_copy(table_hbm.at[ids_p[p].at[pl.ds(s * rows, rows)]],
                                         bufs[s % 2], gsem.at[s % 2])

        for cp in token_inputs(t_first, 0):
            cp.start()
        for cp in token_inputs(t_first, 0):
            cp.wait()
        gather(0, 0).start()

        def one_token(ti, p):
            t = t_first + ti
            g_v, h_v = g_p[p], h_p[p]
            has_next = ti + 1 < tok_per_worker

            @pl.when(has_next)
            def _():
                for cp in token_inputs(t + 1, 1 - p):
                    cp.start()

            for c, k in pieces:
                out_v[c, pl.ds(k, sl)] = jnp.zeros((sl,), jnp.float32)
                out_v[c + half, pl.ds(k, sl)] = jnp.zeros((sl,), jnp.float32)
            for s in range(n_sub):
                gather(p, s).wait()
                if s + 1 < n_sub:
                    gather(p, s + 1).start()
                else:
                    @pl.when(has_next)
                    def _():
                        for cp in token_inputs(t + 1, 1 - p):
                            cp.wait()
                        gather(1 - p, 0).start()
                rbuf = bufs[s % 2]

                @pl.loop(0, rows, step=SC_ROW_GROUP)
                def _(r0):
                    accs = [jnp.zeros((sl,), jnp.float32) for _ in range(SC_ROW_GROUP)]
                    for c, k in pieces:
                        h_hi, h_lo = h_v[c, pl.ds(k, sl)], h_v[c + half, pl.ds(k, sl)]
                        for j in range(SC_ROW_GROUP):
                            hi, lo = _word_halves(rbuf[r0 + j, c, pl.ds(k, sl)])
                            accs[j] = accs[j] + hi * h_hi + lo * h_lo
                    coefs = []
                    for j in range(SC_ROW_GROUP):
                        a = jnp.broadcast_to(jnp.sum(accs[j]), (sl,))
                        act = 0.5 * a * (1.0 + _sc_tanh(0.7978845608028654 * (a + 0.044715 * (a * a * a))))
                        gate = plsc.load_gather(g_v, [jnp.broadcast_to(s * rows + r0 + j, (sl,))])
                        coefs.append(gate * act)
                    for b0 in range(0, len(pieces), SC_STORE_BATCH):
                        tots = []
                        for c, k in pieces[b0:b0 + SC_STORE_BATCH]:
                            t_hi = t_lo = None
                            for j in range(SC_ROW_GROUP):
                                hi, lo = _word_halves(rbuf[r0 + j, c + half, pl.ds(k, sl)])
                                t_hi = coefs[j] * hi if t_hi is None else t_hi + coefs[j] * hi
                                t_lo = coefs[j] * lo if t_lo is None else t_lo + coefs[j] * lo
                            tots.append((t_hi, t_lo))
                        for (c, k), (t_hi, t_lo) in zip(pieces[b0:b0 + SC_STORE_BATCH], tots):
                            plsc.addupdate(out_v.at[c, pl.ds(k, sl)], t_hi)
                            plsc.addupdate(out_v.at[c + half, pl.ds(k, sl)], t_lo)

            pltpu.sync_copy(out_v, peer_hbm.at[t])

        @pl.loop(0, tok_per_worker, step=2)
        def _(ti):
            one_token(ti, 0)
            one_token(ti + 1, 1)

    return run(table3, ids, gates, h3)


def _finish_kernel(peer_ref, x_ref, g2_ref, fg_ref, prev_ref, out_ref):
    del prev_ref
    _finish_block(x_ref, g2_ref, fg_ref, peer_ref, out_ref)


def _finish_tokens(peer, x1, g2, fg, prev, seq_len):
    d = x1.shape[1]
    tm = FINISH_TOK_BLOCK
    return pl.pallas_call(
        _finish_kernel,
        grid=(peer.shape[0] // tm,),
        in_specs=[pl.BlockSpec((tm, d), lambda i: (i, 0)),
                  pl.BlockSpec((tm, d), lambda i: (i, 0)),
                  pl.BlockSpec((1, 1, d), lambda i: (i * tm // seq_len, 0, 0)),
                  pl.BlockSpec((1, d), lambda i: (0, 0)),
                  pl.BlockSpec(memory_space=pl.ANY)],
        out_specs=pl.BlockSpec((tm, d), lambda i: (i, 0)),
        out_shape=jax.ShapeDtypeStruct(prev.shape, prev.dtype),
        input_output_aliases={4: 0},
        compiler_params=pltpu.CompilerParams(dimension_semantics=("arbitrary",)),
        name="finish_tokens",
    )(peer, x1, g2, fg, prev)


def _pack_expert_table(expert_u, expert_v):
    n_experts, d = expert_u.shape
    bits = lambda a: lax.bitcast_convert_type(a.astype(jnp.bfloat16), jnp.uint16).astype(jnp.uint32)
    pair = lambda b: (b[:, :d // 2] << 16) | b[:, d // 2:]
    words = jnp.concatenate([pair(bits(expert_u)), pair(bits(expert_v))], axis=1)
    return words.reshape(n_experts, d // LANES, LANES)


def _tc_peer_experts(idx, gates, h2, x1, g2, final_g, table, seq_len, first_tok):
    n, d = h2.shape
    m = n - first_tok
    n_chunks = d // LANES
    tb = EXPERT_TOK_BLOCK
    assert tb % EXPERT_SLOTS == 0 and seq_len % tb == 0 and first_tok % tb == 0
    first = first_tok // tb
    return pl.pallas_call(
        functools.partial(_expert_kernel, tok_block=tb, n_slots=EXPERT_SLOTS),
        grid=(m // tb,),
        in_specs=[
            pl.BlockSpec((tb, PEER_SLOTS), lambda i: (i + first, 0), memory_space=pltpu.SMEM),
            pl.BlockSpec((tb, PEER_SLOTS), lambda i: (i + first, 0)),
            pl.BlockSpec((tb, d), lambda i: (i + first, 0)),
            pl.BlockSpec((tb, d), lambda i: (i + first, 0)),
            pl.BlockSpec((1, 1, d), lambda i: ((i + first) * tb // seq_len, 0, 0)),
            pl.BlockSpec((1, d), lambda i: (0, 0)),
            pl.BlockSpec(memory_space=pl.ANY),
        ],
        out_specs=pl.BlockSpec((tb, d), lambda i: (i + first, 0)),
        out_shape=jax.ShapeDtypeStruct((n, d), jnp.float32),
        scratch_shapes=(
            [pltpu.VMEM((PEER_SLOTS * n_chunks, LANES), jnp.uint32) for _ in range(EXPERT_SLOTS)]
            + [pltpu.VMEM((tb, d), jnp.float32), pltpu.SemaphoreType.DMA((EXPERT_SLOTS,))]),
        compiler_params=pltpu.CompilerParams(dimension_semantics=("arbitrary",)),
        cost_estimate=_expert_cost(m, d),
        name="peer_experts",
    )(idx, gates, h2, x1, g2, final_g.reshape(1, d), table.reshape(-1, LANES))


def _mix_and_route(x, mod, norm1_g, w_in, b_forget, ln_v_g, w_spatial, b_spatial, w_branch_a,
                   w_branch_b, w_out, norm2_g, w_query, sub_keys):
    B, S, D = x.shape
    n = B * S
    bf16 = jnp.bfloat16
    qkv, cum, cumt, sga, gb = _input_projection(x, mod, norm1_g, w_in, b_forget, ln_v_g,
                                                w_spatial, b_spatial, w_branch_b)
    y_a = _fox_attention(qkv, cum, cumt, B, S)
    x1, h2, idx_t, gates_t = _post_attention(
        y_a, sga, gb, x.reshape(n, D), mod, norm2_g.reshape(1, D),
        w_branch_a.astype(bf16), w_out.astype(bf16), w_query.astype(bf16), sub_keys, S)
    return x1, h2, idx_t.T, gates_t.T


def kernel(x, c, w_mod, b_mod, norm1_g, w_in, b_forget, ln_v_g, w_spatial, b_spatial, w_branch_a, w_branch_b, w_out, norm2_g, w_query, sub_keys, expert_u, expert_v, final_g):
    B, S, D = x.shape
    n = B * S
    assert w_mod.shape[0] == 1, "the final RMSNorm is fused into the single layer's expert kernels"
    l = 0
    mod = _modulation(c, w_mod[l], b_mod[l]).reshape(B, 6, D)
    table = _pack_expert_table(expert_u[l], expert_v[l])
    n_chunks = D // LANES
    weights = (norm1_g[l], w_in[l], b_forget[l], ln_v_g[l], w_spatial[l], b_spatial[l],
               w_branch_a[l], w_branch_b[l], w_out[l], norm2_g[l], w_query[l], sub_keys[l])

    x1, h2, idx, gates = _mix_and_route(x, mod, *weights)
    g2 = mod[:, 5:6, :]

    n_sc = n * SC_SHARE_PERCENT // 100 // SC_SHARE_ALIGN * SC_SHARE_ALIGN
    out = _tc_peer_experts(idx, gates, h2, x1, g2, final_g, table, S, n_sc)
    if n_sc > 0:
        peer_sc = _sc_peer_experts(table, idx[:n_sc].reshape(-1), gates[:n_sc].reshape(-1),
                                   h2[:n_sc].reshape(n_sc, n_chunks, LANES))
        out = _finish_tokens(peer_sc.reshape(n_sc, D), x1, g2, final_g.reshape(1, D), out, S)
    return out.reshape(B, S, D)
```

```python
import functools

import jax
import jax.numpy as jnp
from jax import lax
from jax.experimental import pallas as pl
from jax.experimental.pallas import tpu as pltpu
from jax.experimental.pallas import tpu_sc as plsc

D_MODEL = 1024
ATT_HEADS = 8
ATT_HEAD_DIM = 64
ATT_WIDTH = ATT_HEADS * ATT_HEAD_DIM
GM_GROUPS = 4
GM_GROUP_DIM = 128
GM_WIDTH = GM_GROUPS * GM_GROUP_DIM
GM_CHUNK = 128
PEER_HEADS = 8
PEER_KEY_DIM = 256
PEER_HALF = PEER_KEY_DIM // 2
N_KEYS = 128
PEER_TOPK = 16
PEER_SLOTS = PEER_HEADS * PEER_TOPK
SPLIT_POINTS = (ATT_WIDTH, 2 * ATT_WIDTH, 3 * ATT_WIDTH, 3 * ATT_WIDTH + ATT_HEADS,
                3 * ATT_WIDTH + ATT_HEADS + 2 * GM_WIDTH,
                3 * ATT_WIDTH + ATT_HEADS + 2 * GM_WIDTH + D_MODEL)
EPS = 1e-6

LANES = 128
EXPERT_TOK_BLOCK = 256
EXPERT_SLOTS = 8
DMA_THREADS = 2
FINISH_TOK_BLOCK = 512
PACK_EXPERT_BLOCK = 256
SC_GATHER_ROWS = 32
SC_ROW_GROUP = 4
SC_STORE_BATCH = 8
SC_SHARE_PERCENT = 48
SC_SHARE_ALIGN = 512


def _gelu(x):
    return 0.5 * x * (1.0 + jnp.tanh(0.7978845608028654 * (x + 0.044715 * (x * x * x))))


def _mod_kernel(c_ref, w_ref, b_ref, o_ref):
    c = c_ref[...]
    sc = c * jax.nn.sigmoid(c)
    o_ref[...] = jnp.dot(sc, w_ref[...], precision=lax.Precision.HIGHEST,
                         preferred_element_type=jnp.float32) + b_ref[...]


def _modulation(c, w_mod, b_mod):
    b, d = c.shape
    cols = w_mod.shape[1]
    return pl.pallas_call(
        _mod_kernel,
        grid=(cols // d,),
        in_specs=[pl.BlockSpec((b, d), lambda j: (0, 0)),
                  pl.BlockSpec((d, d), lambda j: (0, j)),
                  pl.BlockSpec((1, d), lambda j: (0, j))],
        out_specs=pl.BlockSpec((b, d), lambda j: (0, j)),
        out_shape=jax.ShapeDtypeStruct((b, cols), jnp.float32),
        name="modulation",
    )(c, w_mod, b_mod.reshape(1, cols))


INPROJ_TOK_BLOCK = 256


def _inproj_kernel(x_ref, mod_ref, n1g_ref, wqkv_ref, wf_ref, bf_ref, wz_ref, wg_ref, lng_ref,
                   wsp_ref, bsp_ref, wb_ref, qkv_ref, cum_ref, cumt_ref, sga_ref, gb_ref, carry):
    f32, bf16 = jnp.float32, jnp.bfloat16
    tm, d = x_ref.shape[1], x_ref.shape[2]
    x = x_ref[0]
    sh1 = mod_ref[0, 0:1, :]
    sc1 = mod_ref[0, 1:2, :]
    h = x * lax.rsqrt(jnp.mean(x * x, axis=-1, keepdims=True) + EPS) * n1g_ref[...]
    hb = (h * (1.0 + sc1) + sh1).astype(bf16)

    qkv = jnp.dot(hb, wqkv_ref[...], preferred_element_type=f32)
    qkv_ref[:, 0:ATT_WIDTH] = (qkv[:, 0:ATT_WIDTH] * (ATT_HEAD_DIM ** -0.5)).astype(bf16)
    qkv_ref[:, ATT_WIDTH:] = qkv[:, ATT_WIDTH:].astype(bf16)

    f = jnp.dot(hb, wf_ref[...], preferred_element_type=f32) + bf_ref[...]
    logf = jnp.minimum(f, 0.0) - jnp.log1p(jnp.exp(-jnp.abs(f)))

    @pl.when(pl.program_id(1) == 0)
    def _():
        carry[...] = jnp.zeros_like(carry)

    tri = (lax.broadcasted_iota(jnp.int32, (tm, tm), 0)
           >= lax.broadcasted_iota(jnp.int32, (tm, tm), 1)).astype(f32)
    cum = jnp.dot(tri, logf, precision=lax.Precision.HIGHEST, preferred_element_type=f32) + carry[...]
    cum_ref[...] = cum
    cumt_ref[0, 0] = jnp.transpose(cum)[0:ATT_HEADS, :]
    carry[...] = cum[tm - 1:tm, :]

    gz = _gelu(jnp.dot(hb, wz_ref[...], preferred_element_type=f32))
    u = gz[:, 0:GM_WIDTH]
    v = gz[:, GM_WIDTH:]
    mu = jnp.mean(v, axis=-1, keepdims=True)
    var = jnp.mean(jnp.square(v - mu), axis=-1, keepdims=True)
    vn = ((v - mu) * lax.rsqrt(var + EPS) * lng_ref[...]).astype(bf16)
    tril = (lax.broadcasted_iota(jnp.int32, (GM_CHUNK, GM_CHUNK), 0)
            >= lax.broadcasted_iota(jnp.int32, (GM_CHUNK, GM_CHUNK), 1))
    w_sp = [jnp.where(tril, wsp_ref[g], 0.0).astype(bf16) for g in range(GM_GROUPS)]
    rows = []
    for ck in range(tm // GM_CHUNK):
        r0 = ck * GM_CHUNK
        cols = []
        for g in range(GM_GROUPS):
            c0 = g * GM_GROUP_DIM
            mixed = jnp.dot(w_sp[g], vn[r0:r0 + GM_CHUNK, c0:c0 + GM_GROUP_DIM],
                            preferred_element_type=f32) + bsp_ref[g]
            cols.append(u[r0:r0 + GM_CHUNK, c0:c0 + GM_GROUP_DIM] * mixed)
        rows.append(jnp.concatenate(cols, axis=1))
    yb = jnp.concatenate(rows, axis=0).astype(bf16)
    ybp = jnp.dot(yb, wb_ref[...], preferred_element_type=f32)

    sg = jax.nn.sigmoid(jnp.dot(hb, wg_ref[...], preferred_element_type=f32))
    sga_ref[...] = sg[:, 0:d].astype(bf16)
    gb_ref[...] = (sg[:, d:] * ybp).astype(bf16)


def _input_projection(x, mod, norm1_g, w_in, b_forget, ln_v_g, w_spatial, b_spatial, w_branch_b):
    B, S, d = x.shape
    n = B * S
    tm = INPROJ_TOK_BLOCK
    bf16 = jnp.bfloat16
    p0, p1, p2, p3, p4, p5 = SPLIT_POINTS
    w_qkv = w_in[:, 0:p2].astype(bf16)
    w_f = jnp.pad(w_in[:, p2:p3], ((0, 0), (0, LANES - ATT_HEADS))).astype(bf16)
    b_f = jnp.pad(b_forget, (0, LANES - ATT_HEADS)).reshape(1, LANES)
    w_z = w_in[:, p3:p4].astype(bf16)
    w_g = w_in[:, p4:].astype(bf16)
    nt = S // tm
    tok = lambda w: pl.BlockSpec((tm, w), lambda b, i: (b * nt + i, 0))
    full = lambda a: pl.BlockSpec(a.shape, lambda b, i: (0,) * a.ndim)
    args = (x, mod, norm1_g.reshape(1, d), w_qkv, w_f, b_f, w_z, w_g, ln_v_g.reshape(1, GM_WIDTH),
            w_spatial, b_spatial.reshape(GM_GROUPS, GM_CHUNK, 1), w_branch_b.astype(bf16))
    return pl.pallas_call(
        _inproj_kernel,
        grid=(B, nt),
        in_specs=[pl.BlockSpec((1, tm, d), lambda b, i: (b, i, 0)),
                  pl.BlockSpec((1, 6, d), lambda b, i: (b, 0, 0))] + [full(a) for a in args[2:]],
        out_specs=[tok(3 * ATT_WIDTH), tok(LANES),
                   pl.BlockSpec((1, 1, ATT_HEADS, tm), lambda b, i: (b, i, 0, 0)), tok(d), tok(d)],
        out_shape=[jax.ShapeDtypeStruct((n, 3 * ATT_WIDTH), bf16),
                   jax.ShapeDtypeStruct((n, LANES), jnp.float32),
                   jax.ShapeDtypeStruct((B, nt, ATT_HEADS, tm), jnp.float32),
                   jax.ShapeDtypeStruct((n, d), bf16),
                   jax.ShapeDtypeStruct((n, d), bf16)],
        scratch_shapes=[pltpu.VMEM((1, LANES), jnp.float32)],
        compiler_params=pltpu.CompilerParams(dimension_semantics=("arbitrary", "arbitrary"),
                                             vmem_limit_bytes=56 * 1024 * 1024),
        name="input_projection",
    )(*args)


ATT_BLOCK = 512


def _fox_kernel(q_ref, k_ref, v_ref, cum_ref, cumt_ref, o_ref, *, blk, ratio):
    f32 = jnp.float32
    hp = pl.program_id(1)
    i = pl.program_id(2)
    dh = ATT_HEAD_DIM
    q2 = q_ref[...]
    first = lax.broadcasted_iota(jnp.int32, (1, 2 * dh), 1) < dh
    qs = (jnp.where(first, q2, jnp.zeros_like(q2)), jnp.where(first, jnp.zeros_like(q2), q2))
    cum_blk = cum_ref[...]
    lane = lax.broadcasted_iota(jnp.int32, cum_blk.shape, 1)
    cqs = [jnp.sum(jnp.where(lane == 2 * hp + e, cum_blk, 0.0), axis=1, keepdims=True)
           for e in range(2)]

    def block(j, carry, masked):
        off = pl.multiple_of(j * blk, blk)
        k2 = k_ref[0, pl.ds(off, blk), :]
        v2 = v_ref[0, pl.ds(off, blk), :]
        out = []
        for e in range(2):
            m, l, acc = carry[e]
            ck = jnp.concatenate([cumt_ref[0, j * ratio + a, pl.ds(2 * hp + e, 1), :]
                                  for a in range(ratio)], axis=1)
            s = lax.dot_general(qs[e], k2, (((1,), (1,)), ((), ())), preferred_element_type=f32)
            s = s + (cqs[e] - ck)
            if masked:
                causal = (lax.broadcasted_iota(jnp.int32, (blk, blk), 0)
                          >= lax.broadcasted_iota(jnp.int32, (blk, blk), 1))
                s = jnp.where(causal, s, -jnp.inf)
            m_new = jnp.maximum(m, jnp.max(s, axis=1, keepdims=True))
            alpha = jnp.exp(m - m_new)
            p = jnp.exp(s - m_new)
            l = alpha * l + jnp.sum(p, axis=1, keepdims=True)
            acc = alpha * acc + jnp.dot(p.astype(v2.dtype), v2, preferred_element_type=f32)
            out.append((m_new, l, acc))
        return tuple(out)

    one = (jnp.full((blk, 1), -1e30, f32), jnp.zeros((blk, 1), f32), jnp.zeros((blk, 2 * dh), f32))
    carry = lax.fori_loop(0, i, lambda j, c: block(j, c, False), (one, one))
    (_, l0, acc0), (_, l1, acc1) = block(i, carry, True)
    o_ref[...] = jnp.where(first, acc0 / l0, acc1 / l1).astype(o_ref.dtype)


def _fox_attention(qkv, cum, cumt, batch, seq_len):
    n = qkv.shape[0]
    tm = cumt.shape[-1]
    blk = max(min(ATT_BLOCK, seq_len), tm)
    nb = seq_len // blk
    pair = 2 * ATT_HEAD_DIM
    n_pairs = ATT_WIDTH // pair
    qkv3 = qkv.reshape(batch, seq_len, 3 * ATT_WIDTH)
    return pl.pallas_call(
        functools.partial(_fox_kernel, blk=blk, ratio=blk // tm),
        grid=(batch, n_pairs, nb),
        in_specs=[pl.BlockSpec((blk, pair), lambda b, hp, i: (b * nb + i, hp)),
                  pl.BlockSpec((1, seq_len, pair), lambda b, hp, i: (b, 0, n_pairs + hp)),
                  pl.BlockSpec((1, seq_len, pair), lambda b, hp, i: (b, 0, 2 * n_pairs + hp)),
                  pl.BlockSpec((blk, LANES), lambda b, hp, i: (b * nb + i, 0)),
                  pl.BlockSpec((1,) + cumt.shape[1:], lambda b, hp, i: (b, 0, 0, 0))],
        out_specs=pl.BlockSpec((blk, pair), lambda b, hp, i: (b * nb + i, hp)),
        out_shape=jax.ShapeDtypeStruct((n, ATT_WIDTH), jnp.bfloat16),
        compiler_params=pltpu.CompilerParams(
            dimension_semantics=("arbitrary", "arbitrary", "arbitrary")),
        name="fox_attention",
    )(qkv, qkv3, qkv3, cum, cumt)


def _topk_rows(scores, k):
    rows, t = scores[0].shape
    iota = lax.broadcasted_iota(jnp.int32, (rows, t), 0).astype(jnp.float32)
    slot = lax.broadcasted_iota(jnp.int32, (k, t), 0)
    scores = list(scores)
    vals = [jnp.zeros((k, t), jnp.float32) for _ in scores]
    ids = [jnp.zeros((k, t), jnp.float32) for _ in scores]
    for j in range(k):
        for i, s in enumerate(scores):
            m = jnp.max(s, axis=0, keepdims=True)
            am = jnp.min(jnp.where(s == m, iota, float(rows)), axis=0, keepdims=True)
            vals[i] = jnp.where(slot == j, m, vals[i])
            ids[i] = jnp.where(slot == j, am, ids[i])
            scores[i] = jnp.where(iota == am, -jnp.inf, s)
    return [(v, i.astype(jnp.int32)) for v, i in zip(vals, ids)]


def _select_rows(table, pos):
    out = jnp.zeros(pos.shape, table.dtype)
    for r in range(table.shape[0]):
        out = jnp.where(pos == r, table[r:r + 1, :], out)
    return out


def _post_kernel(ya_ref, sga_ref, gb_ref, x_ref, mod_ref, n2g_ref, wa_ref, wo_ref, wq_ref, keys_ref,
                 x1_ref, h2_ref, idx_ref, gates_ref, q_scr):
    f32 = jnp.float32
    a = jnp.dot(ya_ref[...], wa_ref[...], preferred_element_type=f32)
    merged = sga_ref[...].astype(f32) * a + gb_ref[...].astype(f32)
    o = jnp.dot(merged.astype(jnp.bfloat16), wo_ref[...], preferred_element_type=f32)
    g1 = mod_ref[0, 2:3, :]
    sh2 = mod_ref[0, 3:4, :]
    sc2 = mod_ref[0, 4:5, :]
    x1 = x_ref[...] + g1 * o
    x1_ref[...] = x1
    h2 = x1 * lax.rsqrt(jnp.mean(x1 * x1, axis=-1, keepdims=True) + EPS) * n2g_ref[...]
    h2 = h2 * (1.0 + sc2) + sh2
    h2_ref[...] = h2
    qp = jnp.dot(h2.astype(jnp.bfloat16), wq_ref[...], preferred_element_type=f32)
    for j in range(2 * PEER_HEADS):
        q_scr[j] = qp[:, j * PEER_HALF:(j + 1) * PEER_HALF]

    half = PEER_TOPK // 2
    tail0 = PEER_TOPK + (half - 1) * half

    def heads(hg, carry):
        hs = [hg * ROUTE_HEADS + e for e in range(ROUTE_HEADS)]
        scores = []
        for h in hs:
            for p in range(2):
                q = q_scr[2 * h + p]
                keys = keys_ref[2 * h + p]
                scores.append(lax.dot_general(keys, q, (((1,), (1,)), ((), ())),
                                              precision=lax.Precision.HIGHEST,
                                              preferred_element_type=f32))
        tops = _topk_rows(scores, PEER_TOPK)
        cands = []
        for e in range(ROUTE_HEADS):
            (s1, _), (s2, _) = tops[2 * e], tops[2 * e + 1]
            blocks = [s1[0:1, :] + s2]
            blocks += [s1[a:a + 1, :] + s2[0:half, :] for a in range(1, half)]
            blocks += [s1[half:, :] + s2[0:1, :]]
            cands.append(jnp.concatenate(blocks, axis=0))
        best = _topk_rows(cands, PEER_TOPK)
        for e, h in enumerate(hs):
            (_, i1), (_, i2) = tops[2 * e], tops[2 * e + 1]
            vals, pos = best[e]
            mid = pos - PEER_TOPK
            ra = jnp.where(pos < PEER_TOPK, 0,
                           jnp.where(pos < tail0, 1 + (mid >> (half.bit_length() - 1)), pos - tail0 + half))
            rb = jnp.where(pos < PEER_TOPK, pos, jnp.where(pos < tail0, mid & (half - 1), 0))
            eid = _select_rows(i1, ra) * N_KEYS + _select_rows(i2, rb)
            ex = jnp.exp(vals - vals[0:1, :])
            g = ex / jnp.sum(ex, axis=0, keepdims=True)
            row = pl.multiple_of(h * PEER_TOPK, PEER_TOPK)
            idx_ref[pl.ds(row, PEER_TOPK), :] = eid
            gates_ref[pl.ds(row, PEER_TOPK), :] = g
        return carry

    lax.fori_loop(0, PEER_HEADS // ROUTE_HEADS, heads, 0)


POST_TOK_BLOCK = 256
ROUTE_HEADS = 4


def _post_attention(ya, sga, gb, x, mod, norm2_g, w_a, w_out, w_query, sub_keys, seq_len):
    n, d = x.shape
    tm = POST_TOK_BLOCK
    blocks_per_seq = seq_len // tm
    aw = ya.shape[1]
    keys = sub_keys.reshape(2 * PEER_HEADS, N_KEYS, PEER_HALF)
    tok = lambda w: pl.BlockSpec((tm, w), lambda i: (i, 0))
    full = lambda a: pl.BlockSpec(a.shape, lambda i: (0,) * a.ndim)
    return pl.pallas_call(
        _post_kernel,
        grid=(n // tm,),
        in_specs=[tok(aw), tok(d), tok(d), tok(d),
                  pl.BlockSpec((1, 6, d), lambda i: (i // blocks_per_seq, 0, 0)),
                  full(norm2_g), full(w_a), full(w_out), full(w_query), full(keys)],
        out_specs=[tok(d), tok(d),
                   pl.BlockSpec((PEER_SLOTS, tm), lambda i: (0, i)),
                   pl.BlockSpec((PEER_SLOTS, tm), lambda i: (0, i))],
        out_shape=[jax.ShapeDtypeStruct((n, d), jnp.float32),
                   jax.ShapeDtypeStruct((n, d), jnp.float32),
                   jax.ShapeDtypeStruct((PEER_SLOTS, n), jnp.int32),
                   jax.ShapeDtypeStruct((PEER_SLOTS, n), jnp.float32)],
        scratch_shapes=[pltpu.VMEM((2 * PEER_HEADS, tm, PEER_HALF), jnp.float32)],
        compiler_params=pltpu.CompilerParams(dimension_semantics=("arbitrary",),
                                             vmem_limit_bytes=48 * 1024 * 1024),
        name="post_attention",
    )(ya, sga, gb, x, mod, norm2_g, w_a, w_out, w_query, keys)


def _rowsum_bcast(p, ones_bf16):
    hi = p.astype(jnp.bfloat16)
    lo = (p - hi.astype(jnp.float32)).astype(jnp.bfloat16)
    return (jnp.dot(hi, ones_bf16, preferred_element_type=jnp.float32)
            + jnp.dot(lo, ones_bf16, preferred_element_type=jnp.float32))


def _word_halves(w):
    return (lax.bitcast_convert_type(w & jnp.uint32(0xFFFF0000), jnp.float32),
            lax.bitcast_convert_type(w << 16, jnp.float32))


def _eval_experts(chunk, hrow, grow):
    half = hrow.shape[1] // LANES // 2
    ones_bf16 = jnp.ones((LANES, LANES), jnp.bfloat16)
    eye = (lax.broadcasted_iota(jnp.int32, (PEER_SLOTS, LANES), 0)
           == lax.broadcasted_iota(jnp.int32, (PEER_SLOTS, LANES), 1))
    hpart = lambda c: hrow[:, c * LANES:(c + 1) * LANES]
    psum = None
    for c in range(half):
        hi, lo = _word_halves(chunk(c))
        p = hi * hpart(c) + lo * hpart(c + half)
        psum = p if psum is None else psum + p
    act = _gelu(_rowsum_bcast(psum, ones_bf16))
    gcol = _rowsum_bcast(jnp.where(eye, grow, 0.0), ones_bf16)
    coef = gcol * act
    outs_hi, outs_lo = [], []
    for c in range(half):
        hi, lo = _word_halves(chunk(c + half))
        outs_hi.append(jnp.sum(coef * hi, axis=0, keepdims=True))
        outs_lo.append(jnp.sum(coef * lo, axis=0, keepdims=True))
    return jnp.concatenate(outs_hi + outs_lo, axis=-1)


def _finish_block(x_ref, g2_ref, fg_ref, peer, out_ref):
    y = x_ref[...] + g2_ref[0] * peer[...]
    out_ref[...] = y * lax.rsqrt(jnp.mean(y * y, axis=-1, keepdims=True) + EPS) * fg_ref[...]


def _expert_kernel(idx_ref, gates_ref, h_ref, x_ref, g2_ref, fg_ref, tab_ref, out_ref,
                   *scratch, tok_block, n_slots):
    bufs = scratch[:n_slots]
    peer, sem = scratch[n_slots], scratch[n_slots + 1]
    d_model = h_ref.shape[-1]
    n_chunks = d_model // LANES
    rows_per_tok = PEER_SLOTS * n_chunks

    def issue(t, s):
        for r in range(PEER_SLOTS):
            row = pl.multiple_of(idx_ref[t, r] * n_chunks, n_chunks)
            pltpu.make_async_copy(tab_ref.at[pl.ds(row, n_chunks), :],
                                  bufs[s].at[pl.ds(r * n_chunks, n_chunks), :],
                                  sem.at[s]).start(priority=r % DMA_THREADS)

    def wait(s):
        pltpu.make_async_copy(tab_ref.at[pl.ds(0, rows_per_tok), :], bufs[s], sem.at[s]).wait()

    def compute(t, s):
        chunk = lambda c: bufs[s][pl.ds(c, PEER_SLOTS, stride=n_chunks), :]
        peer[pl.ds(t, 1), :] = _eval_experts(chunk, h_ref[pl.ds(t, 1), :], gates_ref[pl.ds(t, 1), :])

    def step(t, s, prefetch):
        wait(s)
        if prefetch:
            issue(t + n_slots - 1, (s - 1) % n_slots)
        compute(t, s)

    for t in range(n_slots - 1):
        issue(t, t)

    n_groups = tok_block // n_slots

    def group(g, carry):
        for s in range(n_slots):
            step(g * n_slots + s, s, True)
        return carry

    lax.fori_loop(0, n_groups - 1, group, 0)
    for s in range(n_slots):
        t = (n_groups - 1) * n_slots + s
        step(t, s, t + n_slots - 1 < tok_block)

    _finish_block(x_ref, g2_ref, fg_ref, peer, out_ref)


def _expert_cost(n_tok, d):
    pairs = n_tok * PEER_SLOTS
    return pl.CostEstimate(flops=4 * pairs * d, transcendentals=pairs,
                           bytes_accessed=4 * pairs * d + 12 * n_tok * d + 8 * pairs)


def _sc_tanh(y):
    return 1.0 - 2.0 / (jnp.exp(2.0 * y) + 1.0)


def _sc_peer_experts(table3, ids, gates, h3):
    n_tok, n_chunks, lanes = h3.shape
    info = plsc.get_sparse_core_info()
    sl = info.num_lanes
    n_workers = info.num_cores * info.num_subcores
    tok_per_worker = n_tok // n_workers
    assert tok_per_worker * n_workers == n_tok and tok_per_worker % 2 == 0
    rows = SC_GATHER_ROWS
    n_sub = PEER_SLOTS // rows
    assert n_sub % 2 == 0
    half = n_chunks // 2
    pieces = [(c, k * sl) for c in range(half) for k in range(lanes // sl)]
    mesh = plsc.VectorSubcoreMesh(core_axis_name="c", subcore_axis_name="s")
    buf = lambda dt: pltpu.VMEM((rows, n_chunks, lanes), dt)

    @functools.partial(
        pl.kernel, mesh=mesh,
        out_type=jax.ShapeDtypeStruct((n_tok, n_chunks, lanes), jnp.float32),
        scratch_types=[pltpu.VMEM((PEER_SLOTS,), jnp.int32), pltpu.VMEM((PEER_SLOTS,), jnp.int32),
                       pltpu.VMEM((PEER_SLOTS,), jnp.float32), pltpu.VMEM((PEER_SLOTS,), jnp.float32),
                       pltpu.VMEM((n_chunks, lanes), jnp.float32),
                       pltpu.VMEM((n_chunks, lanes), jnp.float32),
                       buf(jnp.uint32), buf(jnp.uint32), pltpu.VMEM((n_chunks, lanes), jnp.float32),
                       pltpu.SemaphoreType.DMA((2,)), pltpu.SemaphoreType.DMA((2,))],
        compiler_params=pltpu.CompilerParams(needs_layout_passes=False),
        cost_estimate=_expert_cost(n_tok, n_chunks * lanes),
        name="sc_peer_experts")
    def run(table_hbm, ids_hbm, gates_hbm, h_hbm, peer_hbm, ids0, ids1, g0, g1, h0, h1, rows0, rows1,
            out_v, gsem, isem):
        bufs = (rows0, rows1)
        ids_p, g_p, h_p = (ids0, ids1), (g0, g1), (h0, h1)
        t_first = (lax.axis_index("s") * info.num_cores + lax.axis_index("c")) * tok_per_worker

        def token_inputs(t, p):
            off = pl.multiple_of(t * PEER_SLOTS, PEER_SLOTS)
            return (pltpu.make_async_copy(ids_hbm.at[pl.ds(off, PEER_SLOTS)], ids_p[p], isem.at[p]),
                    pltpu.make_async_copy(gates_hbm.at[pl.ds(off, PEER_SLOTS)], g_p[p], isem.at[p]),
                    pltpu.make_async_copy(h_hbm.at[t], h_p[p], isem.at[p]))

        def gather(p, s):
            return pltpu.make_async_copy(table_hbm.at[ids_p[p].at[pl.ds(s * rows, rows)]],
                                         bufs[s % 2], gsem.at[s % 2])

        for cp in token_inputs(t_first, 0):
            cp.start()
        for cp in token_inputs(t_first, 0):
            cp.wait()
        gather(0, 0).start()

        def one_token(ti, p):
            t = t_first + ti
            g_v, h_v = g_p[p], h_p[p]
            has_next = ti + 1 < tok_per_worker

            @pl.when(has_next)
            def _():
                for cp in token_inputs(t + 1, 1 - p):
                    cp.start()

            for c, k in pieces:
                out_v[c, pl.ds(k, sl)] = jnp.zeros((sl,), jnp.float32)
                out_v[c + half, pl.ds(k, sl)] = jnp.zeros((sl,), jnp.float32)
            for s in range(n_sub):
                gather(p, s).wait()
                if s + 1 < n_sub:
                    gather(p, s + 1).start()
                else:
                    @pl.when(has_next)
                    def _():
                        for cp in token_inputs(t + 1, 1 - p):
                            cp.wait()
                        gather(1 - p, 0).start()
                rbuf = bufs[s % 2]

                @pl.loop(0, rows, step=SC_ROW_GROUP)
                def _(r0):
                    accs = [jnp.zeros((sl,), jnp.float32) for _ in range(SC_ROW_GROUP)]
                    for c, k in pieces:
                        h_hi, h_lo = h_v[c, pl.ds(k, sl)], h_v[c + half, pl.ds(k, sl)]
                        for j in range(SC_ROW_GROUP):
                            hi, lo = _word_halves(rbuf[r0 + j, c, pl.ds(k, sl)])
                            accs[j] = accs[j] + hi * h_hi + lo * h_lo
                    coefs = []
                    for j in range(SC_ROW_GROUP):
                        a = jnp.broadcast_to(jnp.sum(accs[j]), (sl,))
                        act = 0.5 * a * (1.0 + _sc_tanh(0.7978845608028654 * (a + 0.044715 * (a * a * a))))
                        gate = plsc.load_gather(g_v, [jnp.broadcast_to(s * rows + r0 + j, (sl,))])
                        coefs.append(gate * act)
                    for b0 in range(0, len(pieces), SC_STORE_BATCH):
                        tots = []
                        for c, k in pieces[b0:b0 + SC_STORE_BATCH]:
                            t_hi = t_lo = None
                            for j in range(SC_ROW_GROUP):
                                hi, lo = _word_halves(rbuf[r0 + j, c + half, pl.ds(k, sl)])
                                t_hi = coefs[j] * hi if t_hi is None else t_hi + coefs[j] * hi
                                t_lo = coefs[j] * lo if t_lo is None else t_lo + coefs[j] * lo
                            tots.append((t_hi, t_lo))
                        for (c, k), (t_hi, t_lo) in zip(pieces[b0:b0 + SC_STORE_BATCH], tots):
                            plsc.addupdate(out_v.at[c, pl.ds(k, sl)], t_hi)
                            plsc.addupdate(out_v.at[c + half, pl.ds(k, sl)], t_lo)

            pltpu.sync_copy(out_v, peer_hbm.at[t])

        @pl.loop(0, tok_per_worker, step=2)
        def _(ti):
            one_token(ti, 0)
            one_token(ti + 1, 1)

    return run(table3, ids, gates, h3)


def _finish_kernel(peer_ref, x_ref, g2_ref, fg_ref, prev_ref, out_ref):
    del prev_ref
    _finish_block(x_ref, g2_ref, fg_ref, peer_ref, out_ref)


def _finish_tokens(peer, x1, g2, fg, prev, seq_len):
    d = x1.shape[1]
    tm = FINISH_TOK_BLOCK
    return pl.pallas_call(
        _finish_kernel,
        grid=(peer.shape[0] // tm,),
        in_specs=[pl.BlockSpec((tm, d), lambda i: (i, 0)),
                  pl.BlockSpec((tm, d), lambda i: (i, 0)),
                  pl.BlockSpec((1, 1, d), lambda i: (i * tm // seq_len, 0, 0)),
                  pl.BlockSpec((1, d), lambda i: (0, 0)),
                  pl.BlockSpec(memory_space=pl.ANY)],
        out_specs=pl.BlockSpec((tm, d), lambda i: (i, 0)),
        out_shape=jax.ShapeDtypeStruct(prev.shape, prev.dtype),
        input_output_aliases={4: 0},
        compiler_params=pltpu.CompilerParams(dimension_semantics=("arbitrary",)),
        name="finish_tokens",
    )(peer, x1, g2, fg, prev)


def _pack_kernel(u_ref, v_ref, o_ref):
    te, d = u_ref.shape
    n_chunks = d // LANES

    def pairs(x):
        r = lax.bitcast_convert_type(x.astype(jnp.bfloat16).astype(jnp.float32), jnp.uint32)
        return r[:, :d // 2] | (r[:, d // 2:] >> 16)

    words = jnp.concatenate([pairs(u_ref[...]), pairs(v_ref[...])], axis=1)
    for c in range(n_chunks):
        o_ref[pl.ds(c, te, stride=n_chunks), :] = words[:, c * LANES:(c + 1) * LANES]


def _pack_expert_table(expert_u, expert_v):
    n_experts, d = expert_u.shape
    te = PACK_EXPERT_BLOCK
    n_chunks = d // LANES
    packed = pl.pallas_call(
        _pack_kernel,
        grid=(n_experts // te,),
        in_specs=[pl.BlockSpec((te, d), lambda i: (i, 0)), pl.BlockSpec((te, d), lambda i: (i, 0))],
        out_specs=pl.BlockSpec((te * n_chunks, LANES), lambda i: (i, 0)),
        out_shape=jax.ShapeDtypeStruct((n_experts * n_chunks, LANES), jnp.uint32),
        name="pack_experts",
    )(expert_u, expert_v)
    return packed.reshape(n_experts, n_chunks, LANES)


def _tc_peer_experts(idx, gates, h2, x1, g2, final_g, table, seq_len, first_tok):
    n, d = h2.shape
    m = n - first_tok
    n_chunks = d // LANES
    tb = EXPERT_TOK_BLOCK
    assert tb % EXPERT_SLOTS == 0 and seq_len % tb == 0 and first_tok % tb == 0
    first = first_tok // tb
    return pl.pallas_call(
        functools.partial(_expert_kernel, tok_block=tb, n_slots=EXPERT_SLOTS),
        grid=(m // tb,),
        in_specs=[
            pl.BlockSpec((tb, PEER_SLOTS), lambda i: (i + first, 0), memory_space=pltpu.SMEM),
            pl.BlockSpec((tb, PEER_SLOTS), lambda i: (i + first, 0)),
            pl.BlockSpec((tb, d), lambda i: (i + first, 0)),
            pl.BlockSpec((tb, d), lambda i: (i + first, 0)),
            pl.BlockSpec((1, 1, d), lambda i: ((i + first) * tb // seq_len, 0, 0)),
            pl.BlockSpec((1, d), lambda i: (0, 0)),
            pl.BlockSpec(memory_space=pl.ANY),
        ],
        out_specs=pl.BlockSpec((tb, d), lambda i: (i + first, 0)),
        out_shape=jax.ShapeDtypeStruct((n, d), jnp.float32),
        scratch_shapes=(
            [pltpu.VMEM((PEER_SLOTS * n_chunks, LANES), jnp.uint32) for _ in range(EXPERT_SLOTS)]
            + [pltpu.VMEM((tb, d), jnp.float32), pltpu.SemaphoreType.DMA((EXPERT_SLOTS,))]),
        compiler_params=pltpu.CompilerParams(dimension_semantics=("arbitrary",)),
        cost_estimate=_expert_cost(m, d),
        name="peer_experts",
    )(idx, gates, h2, x1, g2, final_g.reshape(1, d), table.reshape(-1, LANES))


def _mix_and_route(x, mod, norm1_g, w_in, b_forget, ln_v_g, w_spatial, b_spatial, w_branch_a,
                   w_branch_b, w_out, norm2_g, w_query, sub_keys):
    B, S, D = x.shape
    n = B * S
    bf16 = jnp.bfloat16
    qkv, cum, cumt, sga, gb = _input_projection(x, mod, norm1_g, w_in, b_forget, ln_v_g,
                                                w_spatial, b_spatial, w_branch_b)
    y_a = _fox_attention(qkv, cum, cumt, B, S)
    x1, h2, idx_t, gates_t = _post_attention(
        y_a, sga, gb, x.reshape(n, D), mod, norm2_g.reshape(1, D),
        w_branch_a.astype(bf16), w_out.astype(bf16), w_query.astype(bf16), sub_keys, S)
    return x1, h2, idx_t.T, gates_t.T


def kernel(x, c, w_mod, b_mod, norm1_g, w_in, b_forget, ln_v_g, w_spatial, b_spatial, w_branch_a, w_branch_b, w_out, norm2_g, w_query, sub_keys, expert_u, expert_v, final_g):
    B, S, D = x.shape
    n = B * S
    assert w_mod.shape[0] == 1, "the final RMSNorm is fused into the single layer's expert kernels"
    l = 0
    mod = _modulation(c, w_mod[l], b_mod[l]).reshape(B, 6, D)
    table = _pack_expert_table(expert_u[l], expert_v[l])
    n_chunks = D // LANES
    weights = (norm1_g[l], w_in[l], b_forget[l], ln_v_g[l], w_spatial[l], b_spatial[l],
               w_branch_a[l], w_branch_b[l], w_out[l], norm2_g[l], w_query[l], sub_keys[l])

    x1, h2, idx, gates = _mix_and_route(x, mod, *weights)
    g2 = mod[:, 5:6, :]

    n_sc = n * SC_SHARE_PERCENT // 100 // SC_SHARE_ALIGN * SC_SHARE_ALIGN
    out = _tc_peer_experts(idx, gates, h2, x1, g2, final_g, table, S, n_sc)
    if n_sc > 0:
        peer_sc = _sc_peer_experts(table, idx[:n_sc].reshape(-1), gates[:n_sc].reshape(-1),
                                   h2[:n_sc].reshape(n_sc, n_chunks, LANES))
        out = _finish_tokens(peer_sc.reshape(n_sc, D), x1, g2, final_g.reshape(1, D), out, S)
    return out.reshape(B, S, D)
```

```python
import functools

import jax
import jax.numpy as jnp
from jax import lax
from jax.experimental import pallas as pl
from jax.experimental.pallas import tpu as pltpu
from jax.experimental.pallas import tpu_sc as plsc

D_MODEL = 1024
ATT_HEADS = 8
ATT_HEAD_DIM = 64
ATT_WIDTH = ATT_HEADS * ATT_HEAD_DIM
GM_GROUPS = 4
GM_GROUP_DIM = 128
GM_WIDTH = GM_GROUPS * GM_GROUP_DIM
GM_CHUNK = 128
PEER_HEADS = 8
PEER_KEY_DIM = 256
PEER_HALF = PEER_KEY_DIM // 2
N_KEYS = 128
PEER_TOPK = 16
PEER_SLOTS = PEER_HEADS * PEER_TOPK
SPLIT_POINTS = (ATT_WIDTH, 2 * ATT_WIDTH, 3 * ATT_WIDTH, 3 * ATT_WIDTH + ATT_HEADS,
                3 * ATT_WIDTH + ATT_HEADS + 2 * GM_WIDTH,
                3 * ATT_WIDTH + ATT_HEADS + 2 * GM_WIDTH + D_MODEL)
EPS = 1e-6

LANES = 128
EXPERT_TOK_BLOCK = 256
EXPERT_SLOTS = 8
DMA_THREADS = 2
FINISH_TOK_BLOCK = 512
PACK_EXPERT_BLOCK = 256
SC_GATHER_ROWS = 32
SC_ROW_GROUP = 4
SC_STORE_BATCH = 8
SC_SHARE_PERCENT = 48
SC_SHARE_ALIGN = 512


def _gelu(x):
    return 0.5 * x * (1.0 + jnp.tanh(0.7978845608028654 * (x + 0.044715 * (x * x * x))))


def _mod_kernel(c_ref, w_ref, b_ref, o_ref):
    c = c_ref[...]
    sc = c * jax.nn.sigmoid(c)
    o_ref[...] = jnp.dot(sc, w_ref[...], precision=lax.Precision.HIGHEST,
                         preferred_element_type=jnp.float32) + b_ref[...]


def _modulation(c, w_mod, b_mod):
    b, d = c.shape
    cols = w_mod.shape[1]
    return pl.pallas_call(
        _mod_kernel,
        grid=(cols // d,),
        in_specs=[pl.BlockSpec((b, d), lambda j: (0, 0)),
                  pl.BlockSpec((d, d), lambda j: (0, j)),
                  pl.BlockSpec((1, d), lambda j: (0, j))],
        out_specs=pl.BlockSpec((b, d), lambda j: (0, j)),
        out_shape=jax.ShapeDtypeStruct((b, cols), jnp.float32),
        name="modulation",
    )(c, w_mod, b_mod.reshape(1, cols))


INPROJ_TOK_BLOCK = 256


def _inproj_kernel(x_ref, mod_ref, n1g_ref, wqkv_ref, wf_ref, bf_ref, wz_ref, wg_ref, lng_ref,
                   wsp_ref, bsp_ref, wb_ref, qkv_ref, cum_ref, cumt_ref, sga_ref, gb_ref, carry):
    f32, bf16 = jnp.float32, jnp.bfloat16
    tm, d = x_ref.shape[1], x_ref.shape[2]
    x = x_ref[0]
    sh1 = mod_ref[0, 0:1, :]
    sc1 = mod_ref[0, 1:2, :]
    h = x * lax.rsqrt(jnp.mean(x * x, axis=-1, keepdims=True) + EPS) * n1g_ref[...]
    hb = (h * (1.0 + sc1) + sh1).astype(bf16)

    qkv = jnp.dot(hb, wqkv_ref[...], preferred_element_type=f32)
    qkv_ref[:, 0:ATT_WIDTH] = (qkv[:, 0:ATT_WIDTH] * (ATT_HEAD_DIM ** -0.5)).astype(bf16)
    qkv_ref[:, ATT_WIDTH:] = qkv[:, ATT_WIDTH:].astype(bf16)

    f = jnp.dot(hb, wf_ref[...], preferred_element_type=f32) + bf_ref[...]
    logf = jnp.minimum(f, 0.0) - jnp.log1p(jnp.exp(-jnp.abs(f)))

    @pl.when(pl.program_id(1) == 0)
    def _():
        carry[...] = jnp.zeros_like(carry)

    tri = (lax.broadcasted_iota(jnp.int32, (tm, tm), 0)
           >= lax.broadcasted_iota(jnp.int32, (tm, tm), 1)).astype(f32)
    cum = jnp.dot(tri, logf, precision=lax.Precision.HIGHEST, preferred_element_type=f32) + carry[...]
    cum_ref[...] = cum
    cumt_ref[0, 0] = jnp.transpose(cum)[0:ATT_HEADS, :]
    carry[...] = cum[tm - 1:tm, :]

    gz = _gelu(jnp.dot(hb, wz_ref[...], preferred_element_type=f32))
    u = gz[:, 0:GM_WIDTH]
    v = gz[:, GM_WIDTH:]
    mu = jnp.mean(v, axis=-1, keepdims=True)
    var = jnp.mean(jnp.square(v - mu), axis=-1, keepdims=True)
    vn = ((v - mu) * lax.rsqrt(var + EPS) * lng_ref[...]).astype(bf16)
    tril = (lax.broadcasted_iota(jnp.int32, (GM_CHUNK, GM_CHUNK), 0)
            >= lax.broadcasted_iota(jnp.int32, (GM_CHUNK, GM_CHUNK), 1))
    w_sp = [jnp.where(tril, wsp_ref[g], 0.0).astype(bf16) for g in range(GM_GROUPS)]
    rows = []
    for ck in range(tm // GM_CHUNK):
        r0 = ck * GM_CHUNK
        cols = []
        for g in range(GM_GROUPS):
            c0 = g * GM_GROUP_DIM
            mixed = jnp.dot(w_sp[g], vn[r0:r0 + GM_CHUNK, c0:c0 + GM_GROUP_DIM],
                            preferred_element_type=f32) + bsp_ref[g]
            cols.append(u[r0:r0 + GM_CHUNK, c0:c0 + GM_GROUP_DIM] * mixed)
        rows.append(jnp.concatenate(cols, axis=1))
    yb = jnp.concatenate(rows, axis=0).astype(bf16)
    ybp = jnp.dot(yb, wb_ref[...], preferred_element_type=f32)

    sg = jax.nn.sigmoid(jnp.dot(hb, wg_ref[...], preferred_element_type=f32))
    sga_ref[...] = sg[:, 0:d].astype(bf16)
    gb_ref[...] = (sg[:, d:] * ybp).astype(bf16)


def _input_projection(x, mod, norm1_g, w_in, b_forget, ln_v_g, w_spatial, b_spatial, w_branch_b):
    B, S, d = x.shape
    n = B * S
    tm = INPROJ_TOK_BLOCK
    bf16 = jnp.bfloat16
    p0, p1, p2, p3, p4, p5 = SPLIT_POINTS
    w_qkv = w_in[:, 0:p2].astype(bf16)
    w_f = jnp.pad(w_in[:, p2:p3], ((0, 0), (0, LANES - ATT_HEADS))).astype(bf16)
    b_f = jnp.pad(b_forget, (0, LANES - ATT_HEADS)).reshape(1, LANES)
    w_z = w_in[:, p3:p4].astype(bf16)
    w_g = w_in[:, p4:].astype(bf16)
    nt = S // tm
    tok = lambda w: pl.BlockSpec((tm, w), lambda b, i: (b * nt + i, 0))
    full = lambda a: pl.BlockSpec(a.shape, lambda b, i: (0,) * a.ndim)
    args = (x, mod, norm1_g.reshape(1, d), w_qkv, w_f, b_f, w_z, w_g, ln_v_g.reshape(1, GM_WIDTH),
            w_spatial, b_spatial.reshape(GM_GROUPS, GM_CHUNK, 1), w_branch_b.astype(bf16))
    return pl.pallas_call(
        _inproj_kernel,
        grid=(B, nt),
        in_specs=[pl.BlockSpec((1, tm, d), lambda b, i: (b, i, 0)),
                  pl.BlockSpec((1, 6, d), lambda b, i: (b, 0, 0))] + [full(a) for a in args[2:]],
        out_specs=[tok(3 * ATT_WIDTH), tok(LANES),
                   pl.BlockSpec((1, 1, ATT_HEADS, tm), lambda b, i: (b, i, 0, 0)), tok(d), tok(d)],
        out_shape=[jax.ShapeDtypeStruct((n, 3 * ATT_WIDTH), bf16),
                   jax.ShapeDtypeStruct((n, LANES), jnp.float32),
                   jax.ShapeDtypeStruct((B, nt, ATT_HEADS, tm), jnp.float32),
                   jax.ShapeDtypeStruct((n, d), bf16),
                   jax.ShapeDtypeStruct((n, d), bf16)],
        scratch_shapes=[pltpu.VMEM((1, LANES), jnp.float32)],
        compiler_params=pltpu.CompilerParams(dimension_semantics=("arbitrary", "arbitrary"),
                                             vmem_limit_bytes=56 * 1024 * 1024),
        name="input_projection",
    )(*args)


ATT_BLOCK = 512


def _fox_kernel(q_ref, k_ref, v_ref, cum_ref, cumt_ref, o_ref, *, blk, ratio):
    f32 = jnp.float32
    hp = pl.program_id(1)
    i = pl.program_id(2)
    dh = ATT_HEAD_DIM
    q2 = q_ref[...]
    first = lax.broadcasted_iota(jnp.int32, (1, 2 * dh), 1) < dh
    qs = (jnp.where(first, q2, jnp.zeros_like(q2)), jnp.where(first, jnp.zeros_like(q2), q2))
    cum_blk = cum_ref[...]
    lane = lax.broadcasted_iota(jnp.int32, cum_blk.shape, 1)
    cqs = [jnp.sum(jnp.where(lane == 2 * hp + e, cum_blk, 0.0), axis=1, keepdims=True)
           for e in range(2)]

    def block(j, carry, masked):
        off = pl.multiple_of(j * blk, blk)
        k2 = k_ref[0, pl.ds(off, blk), :]
        v2 = v_ref[0, pl.ds(off, blk), :]
        out = []
        for e in range(2):
            m, acc = carry[e]
            ck = jnp.concatenate([cumt_ref[0, j * ratio + a, pl.ds(2 * hp + e, 1), :]
                                  for a in range(ratio)], axis=1)
            s = lax.dot_general(qs[e], k2, (((1,), (1,)), ((), ())), preferred_element_type=f32)
            s = s + (cqs[e] - ck)
            if masked:
                causal = (lax.broadcasted_iota(jnp.int32, (blk, blk), 0)
                          >= lax.broadcasted_iota(jnp.int32, (blk, blk), 1))
                s = jnp.where(causal, s, -jnp.inf)
            m_new = jnp.maximum(m, jnp.max(s, axis=1, keepdims=True))
            alpha = jnp.exp(m - m_new)
            p = jnp.exp(s - m_new).astype(v2.dtype)
            own = first if e == 0 else jnp.logical_not(first)
            v_ones = jnp.where(own, v2, jnp.ones_like(v2))
            acc = alpha * acc + jnp.dot(p, v_ones, preferred_element_type=f32)
            out.append((m_new, acc))
        return tuple(out)

    one = (jnp.full((blk, 1), -1e30, f32), jnp.zeros((blk, 2 * dh), f32))
    carry = lax.fori_loop(0, i, lambda j, c: block(j, c, False), (one, one))
    (_, acc0), (_, acc1) = block(i, carry, True)
    ratio0 = acc0 / pltpu.roll(acc0, dh, axis=1)
    ratio1 = acc1 / pltpu.roll(acc1, dh, axis=1)
    o_ref[...] = jnp.where(first, ratio0, ratio1).astype(o_ref.dtype)


def _fox_attention(qkv, cum, cumt, batch, seq_len):
    n = qkv.shape[0]
    tm = cumt.shape[-1]
    blk = max(min(ATT_BLOCK, seq_len), tm)
    nb = seq_len // blk
    pair = 2 * ATT_HEAD_DIM
    n_pairs = ATT_WIDTH // pair
    qkv3 = qkv.reshape(batch, seq_len, 3 * ATT_WIDTH)
    return pl.pallas_call(
        functools.partial(_fox_kernel, blk=blk, ratio=blk // tm),
        grid=(batch, n_pairs, nb),
        in_specs=[pl.BlockSpec((blk, pair), lambda b, hp, i: (b * nb + i, hp)),
                  pl.BlockSpec((1, seq_len, pair), lambda b, hp, i: (b, 0, n_pairs + hp)),
                  pl.BlockSpec((1, seq_len, pair), lambda b, hp, i: (b, 0, 2 * n_pairs + hp)),
                  pl.BlockSpec((blk, LANES), lambda b, hp, i: (b * nb + i, 0)),
                  pl.BlockSpec((1,) + cumt.shape[1:], lambda b, hp, i: (b, 0, 0, 0))],
        out_specs=pl.BlockSpec((blk, pair), lambda b, hp, i: (b * nb + i, hp)),
        out_shape=jax.ShapeDtypeStruct((n, ATT_WIDTH), jnp.bfloat16),
        compiler_params=pltpu.CompilerParams(
            dimension_semantics=("arbitrary", "arbitrary", "arbitrary")),
        name="fox_attention",
    )(qkv, qkv3, qkv3, cum, cumt)


def _topk_rows(scores, k):
    rows, t = scores[0].shape
    iota = lax.broadcasted_iota(jnp.int32, (rows, t), 0).astype(jnp.float32)
    slot = lax.broadcasted_iota(jnp.int32, (k, t), 0)
    scores = list(scores)
    vals = [jnp.zeros((k, t), jnp.float32) for _ in scores]
    ids = [jnp.zeros((k, t), jnp.float32) for _ in scores]
    for j in range(k):
        for i, s in enumerate(scores):
            m = jnp.max(s, axis=0, keepdims=True)
            am = jnp.min(jnp.where(s == m, iota, float(rows)), axis=0, keepdims=True)
            vals[i] = jnp.where(slot == j, m, vals[i])
            ids[i] = jnp.where(slot == j, am, ids[i])
            scores[i] = jnp.where(iota == am, -jnp.inf, s)
    return [(v, i.astype(jnp.int32)) for v, i in zip(vals, ids)]


def _select_rows(table, pos):
    out = jnp.zeros(pos.shape, table.dtype)
    for r in range(table.shape[0]):
        out = jnp.where(pos == r, table[r:r + 1, :], out)
    return out


def _post_kernel(ya_ref, sga_ref, gb_ref, x_ref, mod_ref, n2g_ref, wa_ref, wo_ref, wq_ref, keys_ref,
                 x1_ref, h2_ref, idx_ref, gates_ref, q_scr):
    f32 = jnp.float32
    a = jnp.dot(ya_ref[...], wa_ref[...], preferred_element_type=f32)
    merged = sga_ref[...].astype(f32) * a + gb_ref[...].astype(f32)
    o = jnp.dot(merged.astype(jnp.bfloat16), wo_ref[...], preferred_element_type=f32)
    g1 = mod_ref[0, 2:3, :]
    sh2 = mod_ref[0, 3:4, :]
    sc2 = mod_ref[0, 4:5, :]
    x1 = x_ref[...] + g1 * o
    x1_ref[...] = x1
    h2 = x1 * lax.rsqrt(jnp.mean(x1 * x1, axis=-1, keepdims=True) + EPS) * n2g_ref[...]
    h2 = h2 * (1.0 + sc2) + sh2
    h2_ref[...] = h2
    qp = jnp.dot(h2.astype(jnp.bfloat16), wq_ref[...], preferred_element_type=f32)
    for j in range(2 * PEER_HEADS):
        q_scr[j] = qp[:, j * PEER_HALF:(j + 1) * PEER_HALF]

    half = PEER_TOPK // 2
    tail0 = PEER_TOPK + (half - 1) * half

    def heads(hg, carry):
        hs = [hg * ROUTE_HEADS + e for e in range(ROUTE_HEADS)]
        scores = []
        for h in hs:
            for p in range(2):
                q = q_scr[2 * h + p]
                keys = keys_ref[2 * h + p]
                scores.append(lax.dot_general(keys, q, (((1,), (1,)), ((), ())),
                                              precision=lax.Precision.HIGHEST,
                                              preferred_element_type=f32))
        tops = _topk_rows(scores, PEER_TOPK)
        cands = []
        for e in range(ROUTE_HEADS):
            (s1, _), (s2, _) = tops[2 * e], tops[2 * e + 1]
            blocks = [s1[0:1, :] + s2]
            blocks += [s1[a:a + 1, :] + s2[0:half, :] for a in range(1, half)]
            blocks += [s1[half:, :] + s2[0:1, :]]
            cands.append(jnp.concatenate(blocks, axis=0))
        best = _topk_rows(cands, PEER_TOPK)
        for e, h in enumerate(hs):
            (_, i1), (_, i2) = tops[2 * e], tops[2 * e + 1]
            vals, pos = best[e]
            mid = pos - PEER_TOPK
            ra = jnp.where(pos < PEER_TOPK, 0,
                           jnp.where(pos < tail0, 1 + (mid >> (half.bit_length() - 1)), pos - tail0 + half))
            rb = jnp.where(pos < PEER_TOPK, pos, jnp.where(pos < tail0, mid & (half - 1), 0))
            eid = _select_rows(i1, ra) * N_KEYS + _select_rows(i2, rb)
            ex = jnp.exp(vals - vals[0:1, :])
            g = ex / jnp.sum(ex, axis=0, keepdims=True)
            row = pl.multiple_of(h * PEER_TOPK, PEER_TOPK)
            idx_ref[pl.ds(row, PEER_TOPK), :] = eid
            gates_ref[pl.ds(row, PEER_TOPK), :] = g
        return carry

    lax.fori_loop(0, PEER_HEADS // ROUTE_HEADS, heads, 0)


POST_TOK_BLOCK = 256
ROUTE_HEADS = 4


def _post_attention(ya, sga, gb, x, mod, norm2_g, w_a, w_out, w_query, sub_keys, seq_len):
    n, d = x.shape
    tm = POST_TOK_BLOCK
    blocks_per_seq = seq_len // tm
    aw = ya.shape[1]
    keys = sub_keys.reshape(2 * PEER_HEADS, N_KEYS, PEER_HALF)
    tok = lambda w: pl.BlockSpec((tm, w), lambda i: (i, 0))
    full = lambda a: pl.BlockSpec(a.shape, lambda i: (0,) * a.ndim)
    return pl.pallas_call(
        _post_kernel,
        grid=(n // tm,),
        in_specs=[tok(aw), tok(d), tok(d), tok(d),
                  pl.BlockSpec((1, 6, d), lambda i: (i // blocks_per_seq, 0, 0)),
                  full(norm2_g), full(w_a), full(w_out), full(w_query), full(keys)],
        out_specs=[tok(d), tok(d),
                   pl.BlockSpec((PEER_SLOTS, tm), lambda i: (0, i)),
                   pl.BlockSpec((PEER_SLOTS, tm), lambda i: (0, i))],
        out_shape=[jax.ShapeDtypeStruct((n, d), jnp.float32),
                   jax.ShapeDtypeStruct((n, d), jnp.float32),
                   jax.ShapeDtypeStruct((PEER_SLOTS, n), jnp.int32),
                   jax.ShapeDtypeStruct((PEER_SLOTS, n), jnp.float32)],
        scratch_shapes=[pltpu.VMEM((2 * PEER_HEADS, tm, PEER_HALF), jnp.float32)],
        compiler_params=pltpu.CompilerParams(dimension_semantics=("arbitrary",),
                                             vmem_limit_bytes=48 * 1024 * 1024),
        name="post_attention",
    )(ya, sga, gb, x, mod, norm2_g, w_a, w_out, w_query, keys)


def _rowsum_bcast(p, ones_bf16):
    hi = p.astype(jnp.bfloat16)
    lo = (p - hi.astype(jnp.float32)).astype(jnp.bfloat16)
    return (jnp.dot(hi, ones_bf16, preferred_element_type=jnp.float32)
            + jnp.dot(lo, ones_bf16, preferred_element_type=jnp.float32))


def _word_halves(w):
    return (lax.bitcast_convert_type(w & jnp.uint32(0xFFFF0000), jnp.float32),
            lax.bitcast_convert_type(w << 16, jnp.float32))


def _eval_experts(chunk, hrow, grow):
    half = hrow.shape[1] // LANES // 2
    ones_bf16 = jnp.ones((LANES, LANES), jnp.bfloat16)
    eye = (lax.broadcasted_iota(jnp.int32, (PEER_SLOTS, LANES), 0)
           == lax.broadcasted_iota(jnp.int32, (PEER_SLOTS, LANES), 1))
    hpart = lambda c: hrow[:, c * LANES:(c + 1) * LANES]
    psum = None
    for c in range(half):
        hi, lo = _word_halves(chunk(c))
        p = hi * hpart(c) + lo * hpart(c + half)
        psum = p if psum is None else psum + p
    act = _gelu(_rowsum_bcast(psum, ones_bf16))
    gcol = _rowsum_bcast(jnp.where(eye, grow, 0.0), ones_bf16)
    coef = gcol * act
    outs_hi, outs_lo = [], []
    for c in range(half):
        hi, lo = _word_halves(chunk(c + half))
        outs_hi.append(jnp.sum(coef * hi, axis=0, keepdims=True))
        outs_lo.append(jnp.sum(coef * lo, axis=0, keepdims=True))
    return jnp.concatenate(outs_hi + outs_lo, axis=-1)


def _finish_block(x_ref, g2_ref, fg_ref, peer, out_ref):
    y = x_ref[...] + g2_ref[0] * peer[...]
    out_ref[...] = y * lax.rsqrt(jnp.mean(y * y, axis=-1, keepdims=True) + EPS) * fg_ref[...]


def _expert_kernel(idx_ref, gates_ref, h_ref, x_ref, g2_ref, fg_ref, tab_ref, out_ref,
                   *scratch, tok_block, n_slots):
    bufs = scratch[:n_slots]
    peer, sem = scratch[n_slots], scratch[n_slots + 1]
    d_model = h_ref.shape[-1]
    n_chunks = d_model // LANES
    rows_per_tok = PEER_SLOTS * n_chunks

    def issue(t, s):
        for r in range(PEER_SLOTS):
            row = pl.multiple_of(idx_ref[t, r] * n_chunks, n_chunks)
            pltpu.make_async_copy(tab_ref.at[pl.ds(row, n_chunks), :],
                                  bufs[s].at[pl.ds(r * n_chunks, n_chunks), :],
                                  sem.at[s]).start(priority=r % DMA_THREADS)

    def wait(s):
        pltpu.make_async_copy(tab_ref.at[pl.ds(0, rows_per_tok), :], bufs[s], sem.at[s]).wait()

    def compute(t, s):
        chunk = lambda c: bufs[s][pl.ds(c, PEER_SLOTS, stride=n_chunks), :]
        peer[pl.ds(t, 1), :] = _eval_experts(chunk, h_ref[pl.ds(t, 1), :], gates_ref[pl.ds(t, 1), :])

    def step(t, s, prefetch):
        wait(s)
        if prefetch:
            issue(t + n_slots - 1, (s - 1) % n_slots)
        compute(t, s)

    for t in range(n_slots - 1):
        issue(t, t)

    n_groups = tok_block // n_slots

    def group(g, carry):
        for s in range(n_slots):
            step(g * n_slots + s, s, True)
        return carry

    lax.fori_loop(0, n_groups - 1, group, 0)
    for s in range(n_slots):
        t = (n_groups - 1) * n_slots + s
        step(t, s, t + n_slots - 1 < tok_block)

    _finish_block(x_ref, g2_ref, fg_ref, peer, out_ref)


def _expert_cost(n_tok, d):
    pairs = n_tok * PEER_SLOTS
    return pl.CostEstimate(flops=4 * pairs * d, transcendentals=pairs,
                           bytes_accessed=4 * pairs * d + 12 * n_tok * d + 8 * pairs)


def _sc_tanh(y):
    return 1.0 - 2.0 / (jnp.exp(2.0 * y) + 1.0)


def _sc_peer_experts(table3, ids, gates, h3):
    n_tok, n_chunks, lanes = h3.shape
    info = plsc.get_sparse_core_info()
    sl = info.num_lanes
    n_workers = info.num_cores * info.num_subcores
    tok_per_worker = n_tok // n_workers
    assert tok_per_worker * n_workers == n_tok and tok_per_worker % 2 == 0
    rows = SC_GATHER_ROWS
    n_sub = PEER_SLOTS // rows
    assert n_sub % 2 == 0
    half = n_chunks // 2
    pieces = [(c, k * sl) for c in range(half) for k in range(lanes // sl)]
    mesh = plsc.VectorSubcoreMesh(core_axis_name="c", subcore_axis_name="s")
    buf = lambda dt: pltpu.VMEM((rows, n_chunks, lanes), dt)

    @functools.partial(
        pl.kernel, mesh=mesh,
        out_type=jax.ShapeDtypeStruct((n_tok, n_chunks, lanes), jnp.float32),
        scratch_types=[pltpu.VMEM((PEER_SLOTS,), jnp.int32), pltpu.VMEM((PEER_SLOTS,), jnp.int32),
                       pltpu.VMEM((PEER_SLOTS,), jnp.float32), pltpu.VMEM((PEER_SLOTS,), jnp.float32),
                       pltpu.VMEM((n_chunks, lanes), jnp.float32),
                       pltpu.VMEM((n_chunks, lanes), jnp.float32),
                       buf(jnp.uint32), buf(jnp.uint32), pltpu.VMEM((n_chunks, lanes), jnp.float32),
                       pltpu.SemaphoreType.DMA((2,)), pltpu.SemaphoreType.DMA((2,))],
        compiler_params=pltpu.CompilerParams(needs_layout_passes=False),
        cost_estimate=_expert_cost(n_tok, n_chunks * lanes),
        name="sc_peer_experts")
    def run(table_hbm, ids_hbm, gates_hbm, h_hbm, peer_hbm, ids0, ids1, g0, g1, h0, h1, rows0, rows1,
            out_v, gsem, isem):
        bufs = (rows0, rows1)
        ids_p, g_p, h_p = (ids0, ids1), (g0, g1), (h0, h1)
        t_first = (lax.axis_index("s") * info.num_cores + lax.axis_index("c")) * tok_per_worker

        def token_inputs(t, p):
            off = pl.multiple_of(t * PEER_SLOTS, PEER_SLOTS)
            return (pltpu.make_async_copy(ids_hbm.at[pl.ds(off, PEER_SLOTS)], ids_p[p], isem.at[p]),
                    pltpu.make_async_copy(gates_hbm.at[pl.ds(off, PEER_SLOTS)], g_p[p], isem.at[p]),
                    pltpu.make_async_copy(h_hbm.at[t], h_p[p], isem.at[p]))

        def gather(p, s):
            return pltpu.make_async_copy(table_hbm.at[ids_p[p].at[pl.ds(s * rows, rows)]],
                                         bufs[s % 2], gsem.at[s % 2])

        for cp in token_inputs(t_first, 0):
            cp.start()
        for cp in token_inputs(t_first, 0):
            cp.wait()
        gather(0, 0).start()

        def one_token(ti, p):
            t = t_first + ti
            g_v, h_v = g_p[p], h_p[p]
            has_next = ti + 1 < tok_per_worker

            @pl.when(has_next)
            def _():
                for cp in token_inputs(t + 1, 1 - p):
                    cp.start()

            for c, k in pieces:
                out_v[c, pl.ds(k, sl)] = jnp.zeros((sl,), jnp.float32)
                out_v[c + half, pl.ds(k, sl)] = jnp.zeros((sl,), jnp.float32)
            for s in range(n_sub):
                gather(p, s).wait()
                if s + 1 < n_sub:
                    gather(p, s + 1).start()
                else:
                    @pl.when(has_next)
                    def _():
                        for cp in token_inputs(t + 1, 1 - p):
                            cp.wait()
                        gather(1 - p, 0).start()
                rbuf = bufs[s % 2]

                @pl.loop(0, rows, step=SC_ROW_GROUP)
                def _(r0):
                    accs = [jnp.zeros((sl,), jnp.float32) for _ in range(SC_ROW_GROUP)]
                    for c, k in pieces:
                        h_hi, h_lo = h_v[c, pl.ds(k, sl)], h_v[c + half, pl.ds(k, sl)]
                        for j in range(SC_ROW_GROUP):
                            hi, lo = _word_halves(rbuf[r0 + j, c, pl.ds(k, sl)])
                            accs[j] = accs[j] + hi * h_hi + lo * h_lo
                    coefs = []
                    for j in range(SC_ROW_GROUP):
                        a = jnp.broadcast_to(jnp.sum(accs[j]), (sl,))
                        act = 0.5 * a * (1.0 + _sc_tanh(0.7978845608028654 * (a + 0.044715 * (a * a * a))))
                        gate = plsc.load_gather(g_v, [jnp.broadcast_to(s * rows + r0 + j, (sl,))])
                        coefs.append(gate * act)
                    for b0 in range(0, len(pieces), SC_STORE_BATCH):
                        tots = []
                        for c, k in pieces[b0:b0 + SC_STORE_BATCH]:
                            t_hi = t_lo = None
                            for j in range(SC_ROW_GROUP):
                                hi, lo = _word_halves(rbuf[r0 + j, c + half, pl.ds(k, sl)])
                                t_hi = coefs[j] * hi if t_hi is None else t_hi + coefs[j] * hi
                                t_lo = coefs[j] * lo if t_lo is None else t_lo + coefs[j] * lo
                            tots.append((t_hi, t_lo))
                        for (c, k), (t_hi, t_lo) in zip(pieces[b0:b0 + SC_STORE_BATCH], tots):
                            plsc.addupdate(out_v.at[c, pl.ds(k, sl)], t_hi)
                            plsc.addupdate(out_v.at[c + half, pl.ds(k, sl)], t_lo)

            pltpu.sync_copy(out_v, peer_hbm.at[t])

        @pl.loop(0, tok_per_worker, step=2)
        def _(ti):
            one_token(ti, 0)
            one_token(ti + 1, 1)

    return run(table3, ids, gates, h3)


def _finish_kernel(peer_ref, x_ref, g2_ref, fg_ref, prev_ref, out_ref):
    del prev_ref
    _finish_block(x_ref, g2_ref, fg_ref, peer_ref, out_ref)


def _finish_tokens(peer, x1, g2, fg, prev, seq_len):
    d = x1.shape[1]
    tm = FINISH_TOK_BLOCK
    return pl.pallas_call(
        _finish_kernel,
        grid=(peer.shape[0] // tm,),
        in_specs=[pl.BlockSpec((tm, d), lambda i: (i, 0)),
                  pl.BlockSpec((tm, d), lambda i: (i, 0)),
                  pl.BlockSpec((1, 1, d), lambda i: (i * tm // seq_len, 0, 0)),
                  pl.BlockSpec((1, d), lambda i: (0, 0)),
                  pl.BlockSpec(memory_space=pl.ANY)],
        out_specs=pl.BlockSpec((tm, d), lambda i: (i, 0)),
        out_shape=jax.ShapeDtypeStruct(prev.shape, prev.dtype),
        input_output_aliases={4: 0},
        compiler_params=pltpu.CompilerParams(dimension_semantics=("arbitrary",)),
        name="finish_tokens",
    )(peer, x1, g2, fg, prev)


def _pack_kernel(u_ref, v_ref, o_ref):
    te, d = u_ref.shape
    n_chunks = d // LANES

    def pairs(x):
        r = lax.bitcast_convert_type(x.astype(jnp.bfloat16).astype(jnp.float32), jnp.uint32)
        return r[:, :d // 2] | (r[:, d // 2:] >> 16)

    words = jnp.concatenate([pairs(u_ref[...]), pairs(v_ref[...])], axis=1)
    for c in range(n_chunks):
        o_ref[pl.ds(c, te, stride=n_chunks), :] = words[:, c * LANES:(c + 1) * LANES]


def _pack_expert_table(expert_u, expert_v):
    n_experts, d = expert_u.shape
    te = PACK_EXPERT_BLOCK
    n_chunks = d // LANES
    packed = pl.pallas_call(
        _pack_kernel,
        grid=(n_experts // te,),
        in_specs=[pl.BlockSpec((te, d), lambda i: (i, 0)), pl.BlockSpec((te, d), lambda i: (i, 0))],
        out_specs=pl.BlockSpec((te * n_chunks, LANES), lambda i: (i, 0)),
        out_shape=jax.ShapeDtypeStruct((n_experts * n_chunks, LANES), jnp.uint32),
        name="pack_experts",
    )(expert_u, expert_v)
    return packed.reshape(n_experts, n_chunks, LANES)


def _tc_peer_experts(idx, gates, h2, x1, g2, final_g, table, seq_len, first_tok):
    n, d = h2.shape
    m = n - first_tok
    n_chunks = d // LANES
    tb = EXPERT_TOK_BLOCK
    assert tb % EXPERT_SLOTS == 0 and seq_len % tb == 0 and first_tok % tb == 0
    first = first_tok // tb
    return pl.pallas_call(
        functools.partial(_expert_kernel, tok_block=tb, n_slots=EXPERT_SLOTS),
        grid=(m // tb,),
        in_specs=[
            pl.BlockSpec((tb, PEER_SLOTS), lambda i: (i + first, 0), memory_space=pltpu.SMEM),
            pl.BlockSpec((tb, PEER_SLOTS), lambda i: (i + first, 0)),
            pl.BlockSpec((tb, d), lambda i: (i + first, 0)),
            pl.BlockSpec((tb, d), lambda i: (i + first, 0)),
            pl.BlockSpec((1, 1, d), lambda i: ((i + first) * tb // seq_len, 0, 0)),
            pl.BlockSpec((1, d), lambda i: (0, 0)),
            pl.BlockSpec(memory_space=pl.ANY),
        ],
        out_specs=pl.BlockSpec((tb, d), lambda i: (i + first, 0)),
        out_shape=jax.ShapeDtypeStruct((n, d), jnp.float32),
        scratch_shapes=(
            [pltpu.VMEM((PEER_SLOTS * n_chunks, LANES), jnp.uint32) for _ in range(EXPERT_SLOTS)]
            + [pltpu.VMEM((tb, d), jnp.float32), pltpu.SemaphoreType.DMA((EXPERT_SLOTS,))]),
        compiler_params=pltpu.CompilerParams(dimension_semantics=("arbitrary",)),
        cost_estimate=_expert_cost(m, d),
        name="peer_experts",
    )(idx, gates, h2, x1, g2, final_g.reshape(1, d), table.reshape(-1, LANES))


def _mix_and_route(x, mod, norm1_g, w_in, b_forget, ln_v_g, w_spatial, b_spatial, w_branch_a,
                   w_branch_b, w_out, norm2_g, w_query, sub_keys):
    B, S, D = x.shape
    n = B * S
    bf16 = jnp.bfloat16
    qkv, cum, cumt, sga, gb = _input_projection(x, mod, norm1_g, w_in, b_forget, ln_v_g,
                                                w_spatial, b_spatial, w_branch_b)
    y_a = _fox_attention(qkv, cum, cumt, B, S)
    x1, h2, idx_t, gates_t = _post_attention(
        y_a, sga, gb, x.reshape(n, D), mod, norm2_g.reshape(1, D),
        w_branch_a.astype(bf16), w_out.astype(bf16), w_query.astype(bf16), sub_keys, S)
    return x1, h2, idx_t.T, gates_t.T


def kernel(x, c, w_mod, b_mod, norm1_g, w_in, b_forget, ln_v_g, w_spatial, b_spatial, w_branch_a, w_branch_b, w_out, norm2_g, w_query, sub_keys, expert_u, expert_v, final_g):
    B, S, D = x.shape
    n = B * S
    assert w_mod.shape[0] == 1, "the final RMSNorm is fused into the single layer's expert kernels"
    l = 0
    mod = _modulation(c, w_mod[l], b_mod[l]).reshape(B, 6, D)
    table = _pack_expert_table(expert_u[l], expert_v[l])
    n_chunks = D // LANES
    weights = (norm1_g[l], w_in[l], b_forget[l], ln_v_g[l], w_spatial[l], b_spatial[l],
               w_branch_a[l], w_branch_b[l], w_out[l], norm2_g[l], w_query[l], sub_keys[l])

    x1, h2, idx, gates = _mix_and_route(x, mod, *weights)
    g2 = mod[:, 5:6, :]

    n_sc = n * SC_SHARE_PERCENT // 100 // SC_SHARE_ALIGN * SC_SHARE_ALIGN
    out = _tc_peer_experts(idx, gates, h2, x1, g2, final_g, table, S, n_sc)
    if n_sc > 0:
        peer_sc = _sc_peer_experts(table, idx[:n_sc].reshape(-1), gates[:n_sc].reshape(-1),
                                   h2[:n_sc].reshape(n_sc, n_chunks, LANES))
        out = _finish_tokens(peer_sc.reshape(n_sc, D), x1, g2, final_g.reshape(1, D), out, S)
    return out.reshape(B, S, D)
```

```python
import functools

import jax
import jax.numpy as jnp
from jax import lax
from jax.experimental import pallas as pl
from jax.experimental.pallas import tpu as pltpu
from jax.experimental.pallas import tpu_sc as plsc

D_MODEL = 1024
ATT_HEADS = 8
ATT_HEAD_DIM = 64
ATT_WIDTH = ATT_HEADS * ATT_HEAD_DIM
GM_GROUPS = 4
GM_GROUP_DIM = 128
GM_WIDTH = GM_GROUPS * GM_GROUP_DIM
GM_CHUNK = 128
PEER_HEADS = 8
PEER_KEY_DIM = 256
PEER_HALF = PEER_KEY_DIM // 2
N_KEYS = 128
PEER_TOPK = 16
PEER_SLOTS = PEER_HEADS * PEER_TOPK
SPLIT_POINTS = (ATT_WIDTH, 2 * ATT_WIDTH, 3 * ATT_WIDTH, 3 * ATT_WIDTH + ATT_HEADS,
                3 * ATT_WIDTH + ATT_HEADS + 2 * GM_WIDTH,
                3 * ATT_WIDTH + ATT_HEADS + 2 * GM_WIDTH + D_MODEL)
EPS = 1e-6

LANES = 128
EXPERT_TOK_BLOCK = 256
EXPERT_SLOTS = 8
DMA_THREADS = 2
FINISH_TOK_BLOCK = 512
PACK_EXPERT_BLOCK = 256
SC_GATHER_ROWS = 32
SC_ROW_GROUP = 4
SC_STORE_BATCH = 8
SC_SHARE_PERCENT = 48
SC_SHARE_ALIGN = 512


def _gelu(x):
    return 0.5 * x * (1.0 + jnp.tanh(0.7978845608028654 * (x + 0.044715 * (x * x * x))))


def _mod_kernel(c_ref, w_ref, b_ref, o_ref):
    c = c_ref[...]
    sc = c * jax.nn.sigmoid(c)
    o_ref[...] = jnp.dot(sc, w_ref[...], precision=lax.Precision.HIGHEST,
                         preferred_element_type=jnp.float32) + b_ref[...]


def _modulation(c, w_mod, b_mod):
    b, d = c.shape
    cols = w_mod.shape[1]
    return pl.pallas_call(
        _mod_kernel,
        grid=(cols // d,),
        in_specs=[pl.BlockSpec((b, d), lambda j: (0, 0)),
                  pl.BlockSpec((d, d), lambda j: (0, j)),
                  pl.BlockSpec((1, d), lambda j: (0, j))],
        out_specs=pl.BlockSpec((b, d), lambda j: (0, j)),
        out_shape=jax.ShapeDtypeStruct((b, cols), jnp.float32),
        name="modulation",
    )(c, w_mod, b_mod.reshape(1, cols))


INPROJ_TOK_BLOCK = 256


def _inproj_kernel(x_ref, mod_ref, n1g_ref, wqkv_ref, wf_ref, bf_ref, wz_ref, wg_ref, lng_ref,
                   wsp_ref, bsp_ref, wb_ref, qkv_ref, cum_ref, sga_ref, gb_ref, carry):
    f32, bf16 = jnp.float32, jnp.bfloat16
    tm, d = x_ref.shape[1], x_ref.shape[2]
    x = x_ref[0]
    sh1 = mod_ref[0, 0:1, :]
    sc1 = mod_ref[0, 1:2, :]
    h = x * lax.rsqrt(jnp.mean(x * x, axis=-1, keepdims=True) + EPS) * n1g_ref[...]
    hb = (h * (1.0 + sc1) + sh1).astype(bf16)

    qkv = jnp.dot(hb, wqkv_ref[...], preferred_element_type=f32)
    qkv_ref[:, 0:ATT_WIDTH] = (qkv[:, 0:ATT_WIDTH] * (ATT_HEAD_DIM ** -0.5)).astype(bf16)
    qkv_ref[:, ATT_WIDTH:] = qkv[:, ATT_WIDTH:].astype(bf16)

    f = jnp.dot(hb, wf_ref[...], preferred_element_type=f32) + bf_ref[...]
    logf = jnp.minimum(f, 0.0) - jnp.log1p(jnp.exp(-jnp.abs(f)))

    @pl.when(pl.program_id(1) == 0)
    def _():
        carry[...] = jnp.zeros_like(carry)

    tri = (lax.broadcasted_iota(jnp.int32, (tm, tm), 0)
           >= lax.broadcasted_iota(jnp.int32, (tm, tm), 1)).astype(f32)
    cum = jnp.dot(tri, logf, precision=lax.Precision.HIGHEST, preferred_element_type=f32) + carry[...]
    cum_ref[...] = cum
    carry[...] = cum[tm - 1:tm, :]

    gz = _gelu(jnp.dot(hb, wz_ref[...], preferred_element_type=f32))
    u = gz[:, 0:GM_WIDTH]
    v = gz[:, GM_WIDTH:]
    mu = jnp.mean(v, axis=-1, keepdims=True)
    var = jnp.mean(jnp.square(v - mu), axis=-1, keepdims=True)
    vn = ((v - mu) * lax.rsqrt(var + EPS) * lng_ref[...]).astype(bf16)
    tril = (lax.broadcasted_iota(jnp.int32, (GM_CHUNK, GM_CHUNK), 0)
            >= lax.broadcasted_iota(jnp.int32, (GM_CHUNK, GM_CHUNK), 1))
    w_sp = [jnp.where(tril, wsp_ref[g], 0.0).astype(bf16) for g in range(GM_GROUPS)]
    rows = []
    for ck in range(tm // GM_CHUNK):
        r0 = ck * GM_CHUNK
        cols = []
        for g in range(GM_GROUPS):
            c0 = g * GM_GROUP_DIM
            mixed = jnp.dot(w_sp[g], vn[r0:r0 + GM_CHUNK, c0:c0 + GM_GROUP_DIM],
                            preferred_element_type=f32) + bsp_ref[g]
            cols.append(u[r0:r0 + GM_CHUNK, c0:c0 + GM_GROUP_DIM] * mixed)
        rows.append(jnp.concatenate(cols, axis=1))
    yb = jnp.concatenate(rows, axis=0).astype(bf16)
    ybp = jnp.dot(yb, wb_ref[...], preferred_element_type=f32)

    sg = jax.nn.sigmoid(jnp.dot(hb, wg_ref[...], preferred_element_type=f32))
    sga_ref[...] = sg[:, 0:d].astype(bf16)
    gb_ref[...] = (sg[:, d:] * ybp).astype(bf16)


def _input_projection(x, mod, norm1_g, w_in, b_forget, ln_v_g, w_spatial, b_spatial, w_branch_b):
    B, S, d = x.shape
    n = B * S
    tm = INPROJ_TOK_BLOCK
    bf16 = jnp.bfloat16
    p0, p1, p2, p3, p4, p5 = SPLIT_POINTS
    w_qkv = w_in[:, 0:p2].astype(bf16)
    w_f = jnp.pad(w_in[:, p2:p3], ((0, 0), (0, LANES - ATT_HEADS))).astype(bf16)
    b_f = jnp.pad(b_forget, (0, LANES - ATT_HEADS)).reshape(1, LANES)
    w_z = w_in[:, p3:p4].astype(bf16)
    w_g = w_in[:, p4:].astype(bf16)
    nt = S // tm
    tok = lambda w: pl.BlockSpec((tm, w), lambda b, i: (b * nt + i, 0))
    full = lambda a: pl.BlockSpec(a.shape, lambda b, i: (0,) * a.ndim)
    args = (x, mod, norm1_g.reshape(1, d), w_qkv, w_f, b_f, w_z, w_g, ln_v_g.reshape(1, GM_WIDTH),
            w_spatial, b_spatial.reshape(GM_GROUPS, GM_CHUNK, 1), w_branch_b.astype(bf16))
    return pl.pallas_call(
        _inproj_kernel,
        grid=(B, nt),
        in_specs=[pl.BlockSpec((1, tm, d), lambda b, i: (b, i, 0)),
                  pl.BlockSpec((1, 6, d), lambda b, i: (b, 0, 0))] + [full(a) for a in args[2:]],
        out_specs=[tok(3 * ATT_WIDTH), tok(LANES), tok(d), tok(d)],
        out_shape=[jax.ShapeDtypeStruct((n, 3 * ATT_WIDTH), bf16),
                   jax.ShapeDtypeStruct((n, LANES), jnp.float32),
                   jax.ShapeDtypeStruct((n, d), bf16),
                   jax.ShapeDtypeStruct((n, d), bf16)],
        scratch_shapes=[pltpu.VMEM((1, LANES), jnp.float32)],
        compiler_params=pltpu.CompilerParams(dimension_semantics=("arbitrary", "arbitrary"),
                                             vmem_limit_bytes=56 * 1024 * 1024),
        name="input_projection",
    )(*args)


ATT_BLOCK = 512


def _split3(x):
    hi = x.astype(jnp.bfloat16)
    r1 = x - hi.astype(jnp.float32)
    mid = r1.astype(jnp.bfloat16)
    lo = (r1 - mid.astype(jnp.float32)).astype(jnp.bfloat16)
    return hi, mid, lo


def _fox_kernel(q_ref, k_ref, v_ref, cumq_ref, cumk_ref, o_ref, kaug, *, blk):
    f32, bf16 = jnp.float32, jnp.bfloat16
    hp = pl.program_id(1)
    i = pl.program_id(2)
    dh = ATT_HEAD_DIM
    lane1 = lax.broadcasted_iota(jnp.int32, (1, 2 * dh), 1)
    first = lane1 < dh
    sel_r = lax.broadcasted_iota(jnp.int32, (LANES, 2 * dh), 0)
    sel_c = lax.broadcasted_iota(jnp.int32, (LANES, 2 * dh), 1)

    def augment(main, cum, e, first_lane, sign, ones_lane):
        own = first if e == 0 else jnp.logical_not(first)
        base = dh if e == 0 else 0
        aug = ((lane1 >= base + ones_lane) & (lane1 < base + ones_lane + 3)).astype(f32)
        for t, part in enumerate(_split3(cum)):
            sel = jnp.where((sel_r == 2 * hp + e) & (sel_c == base + first_lane + t), sign, 0.0)
            aug = aug + jnp.dot(part, sel.astype(bf16), preferred_element_type=f32)
        return jnp.where(own, main, aug.astype(bf16))

    @pl.when(i == 0)
    def _():
        for e in range(2):
            kaug[e] = augment(k_ref[0], cumk_ref[0], e, 3, -1.0, 0)

    qs = [augment(q_ref[...], cumq_ref[...], e, 0, 1.0, 3) for e in range(2)]

    def block(j, carry, masked):
        off = pl.multiple_of(j * blk, blk)
        v2 = v_ref[0, pl.ds(off, blk), :]
        out = []
        for e in range(2):
            m, acc = carry[e]
            s = lax.dot_general(qs[e], kaug[e, pl.ds(off, blk), :], (((1,), (1,)), ((), ())),
                                preferred_element_type=f32)
            if masked:
                causal = (lax.broadcasted_iota(jnp.int32, (blk, blk), 0)
                          >= lax.broadcasted_iota(jnp.int32, (blk, blk), 1))
                s = jnp.where(causal, s, -jnp.inf)
            m_new = jnp.maximum(m, jnp.max(s, axis=1, keepdims=True))
            alpha = jnp.exp(m - m_new)
            p = jnp.exp(s - m_new).astype(v2.dtype)
            own = first if e == 0 else jnp.logical_not(first)
            v_ones = jnp.where(own, v2, jnp.ones_like(v2))
            acc = alpha * acc + jnp.dot(p, v_ones, preferred_element_type=f32)
            out.append((m_new, acc))
        return tuple(out)

    one = (jnp.full((blk, 1), -1e30, f32), jnp.zeros((blk, 2 * dh), f32))
    carry = lax.fori_loop(0, i, lambda j, c: block(j, c, False), (one, one))
    (_, acc0), (_, acc1) = block(i, carry, True)
    ratio0 = acc0 / pltpu.roll(acc0, dh, axis=1)
    ratio1 = acc1 / pltpu.roll(acc1, dh, axis=1)
    o_ref[...] = jnp.where(first, ratio0, ratio1).astype(o_ref.dtype)


def _fox_attention(qkv, cum, batch, seq_len):
    n = qkv.shape[0]
    blk = min(ATT_BLOCK, seq_len)
    nb = seq_len // blk
    pair = 2 * ATT_HEAD_DIM
    n_pairs = ATT_WIDTH // pair
    qkv3 = qkv.reshape(batch, seq_len, 3 * ATT_WIDTH)
    return pl.pallas_call(
        functools.partial(_fox_kernel, blk=blk),
        grid=(batch, n_pairs, nb),
        in_specs=[pl.BlockSpec((blk, pair), lambda b, hp, i: (b * nb + i, hp)),
                  pl.BlockSpec((1, seq_len, pair), lambda b, hp, i: (b, 0, n_pairs + hp)),
                  pl.BlockSpec((1, seq_len, pair), lambda b, hp, i: (b, 0, 2 * n_pairs + hp)),
                  pl.BlockSpec((blk, LANES), lambda b, hp, i: (b * nb + i, 0)),
                  pl.BlockSpec((1, seq_len, LANES), lambda b, hp, i: (b, 0, 0))],
        out_specs=pl.BlockSpec((blk, pair), lambda b, hp, i: (b * nb + i, hp)),
        out_shape=jax.ShapeDtypeStruct((n, ATT_WIDTH), jnp.bfloat16),
        scratch_shapes=[pltpu.VMEM((2, seq_len, pair), jnp.bfloat16)],
        compiler_params=pltpu.CompilerParams(
            dimension_semantics=("arbitrary", "arbitrary", "arbitrary")),
        name="fox_attention",
    )(qkv, qkv3, qkv3, cum, cum.reshape(batch, seq_len, LANES))


def _topk_rows(scores, k):
    rows, t = scores[0].shape
    iota = lax.broadcasted_iota(jnp.int32, (rows, t), 0).astype(jnp.float32)
    slot = lax.broadcasted_iota(jnp.int32, (k, t), 0)
    scores = list(scores)
    vals = [jnp.zeros((k, t), jnp.float32) for _ in scores]
    ids = [jnp.zeros((k, t), jnp.float32) for _ in scores]
    for j in range(k):
        for i, s in enumerate(scores):
            m = jnp.max(s, axis=0, keepdims=True)
            am = jnp.min(jnp.where(s == m, iota, float(rows)), axis=0, keepdims=True)
            vals[i] = jnp.where(slot == j, m, vals[i])
            ids[i] = jnp.where(slot == j, am, ids[i])
            scores[i] = jnp.where(iota == am, -jnp.inf, s)
    return [(v, i.astype(jnp.int32)) for v, i in zip(vals, ids)]


def _select_rows(table, pos):
    out = jnp.zeros(pos.shape, table.dtype)
    for r in range(table.shape[0]):
        out = jnp.where(pos == r, table[r:r + 1, :], out)
    return out


def _post_kernel(ya_ref, sga_ref, gb_ref, x_ref, mod_ref, n2g_ref, wa_ref, wo_ref, wq_ref, keys_ref,
                 x1_ref, h2_ref, idx_ref, gates_ref, q_scr):
    f32 = jnp.float32
    a = jnp.dot(ya_ref[...], wa_ref[...], preferred_element_type=f32)
    merged = sga_ref[...].astype(f32) * a + gb_ref[...].astype(f32)
    o = jnp.dot(merged.astype(jnp.bfloat16), wo_ref[...], preferred_element_type=f32)
    g1 = mod_ref[0, 2:3, :]
    sh2 = mod_ref[0, 3:4, :]
    sc2 = mod_ref[0, 4:5, :]
    x1 = x_ref[...] + g1 * o
    x1_ref[...] = x1
    h2 = x1 * lax.rsqrt(jnp.mean(x1 * x1, axis=-1, keepdims=True) + EPS) * n2g_ref[...]
    h2 = h2 * (1.0 + sc2) + sh2
    h2_ref[...] = h2
    qp = jnp.dot(h2.astype(jnp.bfloat16), wq_ref[...], preferred_element_type=f32)
    for j in range(2 * PEER_HEADS):
        q_scr[j] = qp[:, j * PEER_HALF:(j + 1) * PEER_HALF]

    half = PEER_TOPK // 2
    tail0 = PEER_TOPK + (half - 1) * half

    def heads(hg, carry):
        hs = [hg * ROUTE_HEADS + e for e in range(ROUTE_HEADS)]
        scores = []
        for h in hs:
            for p in range(2):
                q = q_scr[2 * h + p]
                keys = keys_ref[2 * h + p]
                scores.append(lax.dot_general(keys, q, (((1,), (1,)), ((), ())),
                                              precision=lax.Precision.HIGHEST,
                                              preferred_element_type=f32))
        tops = _topk_rows(scores, PEER_TOPK)
        cands = []
        for e in range(ROUTE_HEADS):
            (s1, _), (s2, _) = tops[2 * e], tops[2 * e + 1]
            blocks = [s1[0:1, :] + s2]
            blocks += [s1[a:a + 1, :] + s2[0:half, :] for a in range(1, half)]
            blocks += [s1[half:, :] + s2[0:1, :]]
            cands.append(jnp.concatenate(blocks, axis=0))
        best = _topk_rows(cands, PEER_TOPK)
        for e, h in enumerate(hs):
            (_, i1), (_, i2) = tops[2 * e], tops[2 * e + 1]
            vals, pos = best[e]
            mid = pos - PEER_TOPK
            ra = jnp.where(pos < PEER_TOPK, 0,
                           jnp.where(pos < tail0, 1 + (mid >> (half.bit_length() - 1)), pos - tail0 + half))
            rb = jnp.where(pos < PEER_TOPK, pos, jnp.where(pos < tail0, mid & (half - 1), 0))
            eid = _select_rows(i1, ra) * N_KEYS + _select_rows(i2, rb)
            ex = jnp.exp(vals - vals[0:1, :])
            g = ex / jnp.sum(ex, axis=0, keepdims=True)
            row = pl.multiple_of(h * PEER_TOPK, PEER_TOPK)
            idx_ref[pl.ds(row, PEER_TOPK), :] = eid
            gates_ref[pl.ds(row, PEER_TOPK), :] = g
        return carry

    lax.fori_loop(0, PEER_HEADS // ROUTE_HEADS, heads, 0)


POST_TOK_BLOCK = 256
ROUTE_HEADS = 4


def _post_attention(ya, sga, gb, x, mod, norm2_g, w_a, w_out, w_query, sub_keys, seq_len):
    n, d = x.shape
    tm = POST_TOK_BLOCK
    blocks_per_seq = seq_len // tm
    aw = ya.shape[1]
    keys = sub_keys.reshape(2 * PEER_HEADS, N_KEYS, PEER_HALF)
    tok = lambda w: pl.BlockSpec((tm, w), lambda i: (i, 0))
    full = lambda a: pl.BlockSpec(a.shape, lambda i: (0,) * a.ndim)
    return pl.pallas_call(
        _post_kernel,
        grid=(n // tm,),
        in_specs=[tok(aw), tok(d), tok(d), tok(d),
                  pl.BlockSpec((1, 6, d), lambda i: (i // blocks_per_seq, 0, 0)),
                  full(norm2_g), full(w_a), full(w_out), full(w_query), full(keys)],
        out_specs=[tok(d), tok(d),
                   pl.BlockSpec((PEER_SLOTS, tm), lambda i: (0, i)),
                   pl.BlockSpec((PEER_SLOTS, tm), lambda i: (0, i))],
        out_shape=[jax.ShapeDtypeStruct((n, d), jnp.float32),
                   jax.ShapeDtypeStruct((n, d), jnp.float32),
                   jax.ShapeDtypeStruct((PEER_SLOTS, n), jnp.int32),
                   jax.ShapeDtypeStruct((PEER_SLOTS, n), jnp.float32)],
        scratch_shapes=[pltpu.VMEM((2 * PEER_HEADS, tm, PEER_HALF), jnp.float32)],
        compiler_params=pltpu.CompilerParams(dimension_semantics=("arbitrary",),
                                             vmem_limit_bytes=48 * 1024 * 1024),
        name="post_attention",
    )(ya, sga, gb, x, mod, norm2_g, w_a, w_out, w_query, keys)


def _rowsum_bcast(p, ones_bf16):
    hi = p.astype(jnp.bfloat16)
    lo = (p - hi.astype(jnp.float32)).astype(jnp.bfloat16)
    return (jnp.dot(hi, ones_bf16, preferred_element_type=jnp.float32)
            + jnp.dot(lo, ones_bf16, preferred_element_type=jnp.float32))


def _word_halves(w):
    return (lax.bitcast_convert_type(w & jnp.uint32(0xFFFF0000), jnp.float32),
            lax.bitcast_convert_type(w << 16, jnp.float32))


def _eval_experts(chunk, hrow, grow):
    half = hrow.shape[1] // LANES // 2
    ones_bf16 = jnp.ones((LANES, LANES), jnp.bfloat16)
    eye = (lax.broadcasted_iota(jnp.int32, (PEER_SLOTS, LANES), 0)
           == lax.broadcasted_iota(jnp.int32, (PEER_SLOTS, LANES), 1))
    hpart = lambda c: hrow[:, c * LANES:(c + 1) * LANES]
    psum = None
    for c in range(half):
        hi, lo = _word_halves(chunk(c))
        p = hi * hpart(c) + lo * hpart(c + half)
        psum = p if psum is None else psum + p
    act = _gelu(_rowsum_bcast(psum, ones_bf16))
    gcol = _rowsum_bcast(jnp.where(eye, grow, 0.0), ones_bf16)
    coef = gcol * act
    outs_hi, outs_lo = [], []
    for c in range(half):
        hi, lo = _word_halves(chunk(c + half))
        outs_hi.append(jnp.sum(coef * hi, axis=0, keepdims=True))
        outs_lo.append(jnp.sum(coef * lo, axis=0, keepdims=True))
    return jnp.concatenate(outs_hi + outs_lo, axis=-1)


def _finish_block(x_ref, g2_ref, fg_ref, peer, out_ref):
    y = x_ref[...] + g2_ref[0] * peer[...]
    out_ref[...] = y * lax.rsqrt(jnp.mean(y * y, axis=-1, keepdims=True) + EPS) * fg_ref[...]


def _expert_kernel(idx_ref, gates_ref, h_ref, x_ref, g2_ref, fg_ref, tab_ref, out_ref,
                   *scratch, tok_block, n_slots):
    bufs = scratch[:n_slots]
    peer, sem = scratch[n_slots], scratch[n_slots + 1]
    d_model = h_ref.shape[-1]
    n_chunks = d_model // LANES
    rows_per_tok = PEER_SLOTS * n_chunks

    def issue(t, s):
        for r in range(PEER_SLOTS):
            row = pl.multiple_of(idx_ref[t, r] * n_chunks, n_chunks)
            pltpu.make_async_copy(tab_ref.at[pl.ds(row, n_chunks), :],
                                  bufs[s].at[pl.ds(r * n_chunks, n_chunks), :],
                                  sem.at[s]).start(priority=r % DMA_THREADS)

    def wait(s):
        pltpu.make_async_copy(tab_ref.at[pl.ds(0, rows_per_tok), :], bufs[s], sem.at[s]).wait()

    def compute(t, s):
        chunk = lambda c: bufs[s][pl.ds(c, PEER_SLOTS, stride=n_chunks), :]
        peer[pl.ds(t, 1), :] = _eval_experts(chunk, h_ref[pl.ds(t, 1), :], gates_ref[pl.ds(t, 1), :])

    def step(t, s, prefetch):
        wait(s)
        if prefetch:
            issue(t + n_slots - 1, (s - 1) % n_slots)
        compute(t, s)

    for t in range(n_slots - 1):
        issue(t, t)

    n_groups = tok_block // n_slots

    def group(g, carry):
        for s in range(n_slots):
            step(g * n_slots + s, s, True)
        return carry

    lax.fori_loop(0, n_groups - 1, group, 0)
    for s in range(n_slots):
        t = (n_groups - 1) * n_slots + s
        step(t, s, t + n_slots - 1 < tok_block)

    _finish_block(x_ref, g2_ref, fg_ref, peer, out_ref)


def _expert_cost(n_tok, d):
    pairs = n_tok * PEER_SLOTS
    return pl.CostEstimate(flops=4 * pairs * d, transcendentals=pairs,
                           bytes_accessed=4 * pairs * d + 12 * n_tok * d + 8 * pairs)


def _sc_tanh(y):
    return 1.0 - 2.0 / (jnp.exp(2.0 * y) + 1.0)


def _sc_peer_experts(table3, ids, gates, h3):
    n_tok, n_chunks, lanes = h3.shape
    info = plsc.get_sparse_core_info()
    sl = info.num_lanes
    n_workers = info.num_cores * info.num_subcores
    tok_per_worker = n_tok // n_workers
    assert tok_per_worker * n_workers == n_tok and tok_per_worker % 2 == 0
    rows = SC_GATHER_ROWS
    n_sub = PEER_SLOTS // rows
    assert n_sub % 2 == 0
    half = n_chunks // 2
    pieces = [(c, k * sl) for c in range(half) for k in range(lanes // sl)]
    mesh = plsc.VectorSubcoreMesh(core_axis_name="c", subcore_axis_name="s")
    buf = lambda dt: pltpu.VMEM((rows, n_chunks, lanes), dt)

    @functools.partial(
        pl.kernel, mesh=mesh,
        out_type=jax.ShapeDtypeStruct((n_tok, n_chunks, lanes), jnp.float32),
        scratch_types=[pltpu.VMEM((PEER_SLOTS,), jnp.int32), pltpu.VMEM((PEER_SLOTS,), jnp.int32),
                       pltpu.VMEM((PEER_SLOTS,), jnp.float32), pltpu.VMEM((PEER_SLOTS,), jnp.float32),
                       pltpu.VMEM((n_chunks, lanes), jnp.float32),
                       pltpu.VMEM((n_chunks, lanes), jnp.float32),
                       buf(jnp.uint32), buf(jnp.uint32), pltpu.VMEM((n_chunks, lanes), jnp.float32),
                       pltpu.SemaphoreType.DMA((2,)), pltpu.SemaphoreType.DMA((2,))],
        compiler_params=pltpu.CompilerParams(needs_layout_passes=False),
        cost_estimate=_expert_cost(n_tok, n_chunks * lanes),
        name="sc_peer_experts")
    def run(table_hbm, ids_hbm, gates_hbm, h_hbm, peer_hbm, ids0, ids1, g0, g1, h0, h1, rows0, rows1,
            out_v, gsem, isem):
        bufs = (rows0, rows1)
        ids_p, g_p, h_p = (ids0, ids1), (g0, g1), (h0, h1)
        t_first = (lax.axis_index("s") * info.num_cores + lax.axis_index("c")) * tok_per_worker

        def token_inputs(t, p):
            off = pl.multiple_of(t * PEER_SLOTS, PEER_SLOTS)
            return (pltpu.make_async_copy(ids_hbm.at[pl.ds(off, PEER_SLOTS)], ids_p[p], isem.at[p]),
                    pltpu.make_async_copy(gates_hbm.at[pl.ds(off, PEER_SLOTS)], g_p[p], isem.at[p]),
                    pltpu.make_async_copy(h_hbm.at[t], h_p[p], isem.at[p]))

        def gather(p, s):
            return pltpu.make_async_copy(table_hbm.at[ids_p[p].at[pl.ds(s * rows, rows)]],
                                         bufs[s % 2], gsem.at[s % 2])

        for cp in token_inputs(t_first, 0):
            cp.start()
        for cp in token_inputs(t_first, 0):
            cp.wait()
        gather(0, 0).start()

        def one_token(ti, p):
            t = t_first + ti
            g_v, h_v = g_p[p], h_p[p]
            has_next = ti + 1 < tok_per_worker

            @pl.when(has_next)
            def _():
                for cp in token_inputs(t + 1, 1 - p):
                    cp.start()

            for c, k in pieces:
                out_v[c, pl.ds(k, sl)] = jnp.zeros((sl,), jnp.float32)
                out_v[c + half, pl.ds(k, sl)] = jnp.zeros((sl,), jnp.float32)
            for s in range(n_sub):
                gather(p, s).wait()
                if s + 1 < n_sub:
                    gather(p, s + 1).start()
                else:
                    @pl.when(has_next)
                    def _():
                        for cp in token_inputs(t + 1, 1 - p):
                            cp.wait()
                        gather(1 - p, 0).start()
                rbuf = bufs[s % 2]

                @pl.loop(0, rows, step=SC_ROW_GROUP)
                def _(r0):
                    accs = [jnp.zeros((sl,), jnp.float32) for _ in range(SC_ROW_GROUP)]
                    for c, k in pieces:
                        h_hi, h_lo = h_v[c, pl.ds(k, sl)], h_v[c + half, pl.ds(k, sl)]
                        for j in range(SC_ROW_GROUP):
                            hi, lo = _word_halves(rbuf[r0 + j, c, pl.ds(k, sl)])
                            accs[j] = accs[j] + hi * h_hi + lo * h_lo
                    coefs = []
                    for j in range(SC_ROW_GROUP):
                        a = jnp.broadcast_to(jnp.sum(accs[j]), (sl,))
                        act = 0.5 * a * (1.0 + _sc_tanh(0.7978845608028654 * (a + 0.044715 * (a * a * a))))
                        gate = plsc.load_gather(g_v, [jnp.broadcast_to(s * rows + r0 + j, (sl,))])
                        coefs.append(gate * act)
                    for b0 in range(0, len(pieces), SC_STORE_BATCH):
                        tots = []
                        for c, k in pieces[b0:b0 + SC_STORE_BATCH]:
                            t_hi = t_lo = None
                            for j in range(SC_ROW_GROUP):
                                hi, lo = _word_halves(rbuf[r0 + j, c + half, pl.ds(k, sl)])
                                t_hi = coefs[j] * hi if t_hi is None else t_hi + coefs[j] * hi
                                t_lo = coefs[j] * lo if t_lo is None else t_lo + coefs[j] * lo
                            tots.append((t_hi, t_lo))
                        for (c, k), (t_hi, t_lo) in zip(pieces[b0:b0 + SC_STORE_BATCH], tots):
                            plsc.addupdate(out_v.at[c, pl.ds(k, sl)], t_hi)
                            plsc.addupdate(out_v.at[c + half, pl.ds(k, sl)], t_lo)

            pltpu.sync_copy(out_v, peer_hbm.at[t])

        @pl.loop(0, tok_per_worker, step=2)
        def _(ti):
            one_token(ti, 0)
            one_token(ti + 1, 1)

    return run(table3, ids, gates, h3)


def _finish_kernel(peer_ref, x_ref, g2_ref, fg_ref, prev_ref, out_ref):
    del prev_ref
    _finish_block(x_ref, g2_ref, fg_ref, peer_ref, out_ref)


def _finish_tokens(peer, x1, g2, fg, prev, seq_len):
    d = x1.shape[1]
    tm = FINISH_TOK_BLOCK
    return pl.pallas_call(
        _finish_kernel,
        grid=(peer.shape[0] // tm,),
        in_specs=[pl.BlockSpec((tm, d), lambda i: (i, 0)),
                  pl.BlockSpec((tm, d), lambda i: (i, 0)),
                  pl.BlockSpec((1, 1, d), lambda i: (i * tm // seq_len, 0, 0)),
                  pl.BlockSpec((1, d), lambda i: (0, 0)),
                  pl.BlockSpec(memory_space=pl.ANY)],
        out_specs=pl.BlockSpec((tm, d), lambda i: (i, 0)),
        out_shape=jax.ShapeDtypeStruct(prev.shape, prev.dtype),
        input_output_aliases={4: 0},
        compiler_params=pltpu.CompilerParams(dimension_semantics=("arbitrary",)),
        name="finish_tokens",
    )(peer, x1, g2, fg, prev)


def _pack_kernel(u_ref, v_ref, o_ref):
    te, d = u_ref.shape
    n_chunks = d // LANES

    def pairs(x):
        r = lax.bitcast_convert_type(x.astype(jnp.bfloat16).astype(jnp.float32), jnp.uint32)
        return r[:, :d // 2] | (r[:, d // 2:] >> 16)

    words = jnp.concatenate([pairs(u_ref[...]), pairs(v_ref[...])], axis=1)
    for c in range(n_chunks):
        o_ref[pl.ds(c, te, stride=n_chunks), :] = words[:, c * LANES:(c + 1) * LANES]


def _pack_expert_table(expert_u, expert_v):
    n_experts, d = expert_u.shape
    te = PACK_EXPERT_BLOCK
    n_chunks = d // LANES
    packed = pl.pallas_call(
        _pack_kernel,
        grid=(n_experts // te,),
        in_specs=[pl.BlockSpec((te, d), lambda i: (i, 0)), pl.BlockSpec((te, d), lambda i: (i, 0))],
        out_specs=pl.BlockSpec((te * n_chunks, LANES), lambda i: (i, 0)),
        out_shape=jax.ShapeDtypeStruct((n_experts * n_chunks, LANES), jnp.uint32),
        name="pack_experts",
    )(expert_u, expert_v)
    return packed.reshape(n_experts, n_chunks, LANES)


def _tc_peer_experts(idx, gates, h2, x1, g2, final_g, table, seq_len, first_tok):
    n, d = h2.shape
    m = n - first_tok
    n_chunks = d // LANES
    tb = EXPERT_TOK_BLOCK
    assert tb % EXPERT_SLOTS == 0 and seq_len % tb == 0 and first_tok % tb == 0
    first = first_tok // tb
    return pl.pallas_call(
        functools.partial(_expert_kernel, tok_block=tb, n_slots=EXPERT_SLOTS),
        grid=(m // tb,),
        in_specs=[
            pl.BlockSpec((tb, PEER_SLOTS), lambda i: (i + first, 0), memory_space=pltpu.SMEM),
            pl.BlockSpec((tb, PEER_SLOTS), lambda i: (i + first, 0)),
            pl.BlockSpec((tb, d), lambda i: (i + first, 0)),
            pl.BlockSpec((tb, d), lambda i: (i + first, 0)),
            pl.BlockSpec((1, 1, d), lambda i: ((i + first) * tb // seq_len, 0, 0)),
            pl.BlockSpec((1, d), lambda i: (0, 0)),
            pl.BlockSpec(memory_space=pl.ANY),
        ],
        out_specs=pl.BlockSpec((tb, d), lambda i: (i + first, 0)),
        out_shape=jax.ShapeDtypeStruct((n, d), jnp.float32),
        scratch_shapes=(
            [pltpu.VMEM((PEER_SLOTS * n_chunks, LANES), jnp.uint32) for _ in range(EXPERT_SLOTS)]
            + [pltpu.VMEM((tb, d), jnp.float32), pltpu.SemaphoreType.DMA((EXPERT_SLOTS,))]),
        compiler_params=pltpu.CompilerParams(dimension_semantics=("arbitrary",)),
        cost_estimate=_expert_cost(m, d),
        name="peer_experts",
    )(idx, gates, h2, x1, g2, final_g.reshape(1, d), table.reshape(-1, LANES))


def _mix_and_route(x, mod, norm1_g, w_in, b_forget, ln_v_g, w_spatial, b_spatial, w_branch_a,
                   w_branch_b, w_out, norm2_g, w_query, sub_keys):
    B, S, D = x.shape
    n = B * S
    bf16 = jnp.bfloat16
    qkv, cum, sga, gb = _input_projection(x, mod, norm1_g, w_in, b_forget, ln_v_g,
                                          w_spatial, b_spatial, w_branch_b)
    y_a = _fox_attention(qkv, cum, B, S)
    x1, h2, idx_t, gates_t = _post_attention(
        y_a, sga, gb, x.reshape(n, D), mod, norm2_g.reshape(1, D),
        w_branch_a.astype(bf16), w_out.astype(bf16), w_query.astype(bf16), sub_keys, S)
    return x1, h2, idx_t.T, gates_t.T


def kernel(x, c, w_mod, b_mod, norm1_g, w_in, b_forget, ln_v_g, w_spatial, b_spatial, w_branch_a, w_branch_b, w_out, norm2_g, w_query, sub_keys, expert_u, expert_v, final_g):
    B, S, D = x.shape
    n = B * S
    assert w_mod.shape[0] == 1, "the final RMSNorm is fused into the single layer's expert kernels"
    l = 0
    mod = _modulation(c, w_mod[l], b_mod[l]).reshape(B, 6, D)
    table = _pack_expert_table(expert_u[l], expert_v[l])
    n_chunks = D // LANES
    weights = (norm1_g[l], w_in[l], b_forget[l], ln_v_g[l], w_spatial[l], b_spatial[l],
               w_branch_a[l], w_branch_b[l], w_out[l], norm2_g[l], w_query[l], sub_keys[l])

    x1, h2, idx, gates = _mix_and_route(x, mod, *weights)
    g2 = mod[:, 5:6, :]

    n_sc = n * SC_SHARE_PERCENT // 100 // SC_SHARE_ALIGN * SC_SHARE_ALIGN
    out = _tc_peer_experts(idx, gates, h2, x1, g2, final_g, table, S, n_sc)
    if n_sc > 0:
        peer_sc = _sc_peer_experts(table, idx[:n_sc].reshape(-1), gates[:n_sc].reshape(-1),
                                   h2[:n_sc].reshape(n_sc, n_chunks, LANES))
        out = _finish_tokens(peer_sc.reshape(n_sc, D), x1, g2, final_g.reshape(1, D), out, S)
    return out.reshape(B, S, D)
```

```python
import functools

import jax
import jax.numpy as jnp
from jax import lax
from jax.experimental import pallas as pl
from jax.experimental.pallas import tpu as pltpu
from jax.experimental.pallas import tpu_sc as plsc

D_MODEL = 1024
ATT_HEADS = 8
ATT_HEAD_DIM = 64
ATT_WIDTH = ATT_HEADS * ATT_HEAD_DIM
GM_GROUPS = 4
GM_GROUP_DIM = 128
GM_WIDTH = GM_GROUPS * GM_GROUP_DIM
GM_CHUNK = 128
PEER_HEADS = 8
PEER_KEY_DIM = 256
PEER_HALF = PEER_KEY_DIM // 2
N_KEYS = 128
PEER_TOPK = 16
PEER_SLOTS = PEER_HEADS * PEER_TOPK
SPLIT_POINTS = (ATT_WIDTH, 2 * ATT_WIDTH, 3 * ATT_WIDTH, 3 * ATT_WIDTH + ATT_HEADS,
                3 * ATT_WIDTH + ATT_HEADS + 2 * GM_WIDTH,
                3 * ATT_WIDTH + ATT_HEADS + 2 * GM_WIDTH + D_MODEL)
EPS = 1e-6

LANES = 128
EXPERT_TOK_BLOCK = 256
EXPERT_SLOTS = 8
DMA_THREADS = 2
FINISH_TOK_BLOCK = 512
PACK_EXPERT_BLOCK = 256
SC_GATHER_ROWS = 32
SC_ROW_GROUP = 4
SC_STORE_BATCH = 8
SC_SHARE_PERCENT = 52
SC_SHARE_ALIGN = 512


def _gelu(x):
    return 0.5 * x * (1.0 + jnp.tanh(0.7978845608028654 * (x + 0.044715 * (x * x * x))))


def _mod_kernel(c_ref, w_ref, b_ref, o_ref):
    c = c_ref[...]
    sc = c * jax.nn.sigmoid(c)
    o_ref[...] = jnp.dot(sc, w_ref[...], precision=lax.Precision.HIGHEST,
                         preferred_element_type=jnp.float32) + b_ref[...]


def _modulation(c, w_mod, b_mod):
    b, d = c.shape
    cols = w_mod.shape[1]
    return pl.pallas_call(
        _mod_kernel,
        grid=(cols // d,),
        in_specs=[pl.BlockSpec((b, d), lambda j: (0, 0)),
                  pl.BlockSpec((d, d), lambda j: (0, j)),
                  pl.BlockSpec((1, d), lambda j: (0, j))],
        out_specs=pl.BlockSpec((b, d), lambda j: (0, j)),
        out_shape=jax.ShapeDtypeStruct((b, cols), jnp.float32),
        name="modulation",
    )(c, w_mod, b_mod.reshape(1, cols))


INPROJ_TOK_BLOCK = 256


def _inproj_kernel(x_ref, mod_ref, n1g_ref, wqkv_ref, wf_ref, bf_ref, wz_ref, wg_ref, lng_ref,
                   wsp_ref, bsp_ref, wb_ref, qkv_ref, cum_ref, cumt_ref, sga_ref, gb_ref, carry):
    f32, bf16 = jnp.float32, jnp.bfloat16
    tm, d = x_ref.shape[1], x_ref.shape[2]
    x = x_ref[0]
    sh1 = mod_ref[0, 0:1, :]
    sc1 = mod_ref[0, 1:2, :]
    h = x * lax.rsqrt(jnp.mean(x * x, axis=-1, keepdims=True) + EPS) * n1g_ref[...]
    hb = (h * (1.0 + sc1) + sh1).astype(bf16)

    qkv = jnp.dot(hb, wqkv_ref[...], preferred_element_type=f32)
    qkv_ref[:, 0:ATT_WIDTH] = (qkv[:, 0:ATT_WIDTH] * (ATT_HEAD_DIM ** -0.5)).astype(bf16)
    qkv_ref[:, ATT_WIDTH:] = qkv[:, ATT_WIDTH:].astype(bf16)

    f = jnp.dot(hb, wf_ref[...], preferred_element_type=f32) + bf_ref[...]
    logf = jnp.minimum(f, 0.0) - jnp.log1p(jnp.exp(-jnp.abs(f)))

    @pl.when(pl.program_id(1) == 0)
    def _():
        carry[...] = jnp.zeros_like(carry)

    tri = (lax.broadcasted_iota(jnp.int32, (tm, tm), 0)
           >= lax.broadcasted_iota(jnp.int32, (tm, tm), 1)).astype(f32)
    cum = jnp.dot(tri, logf, precision=lax.Precision.HIGHEST, preferred_element_type=f32) + carry[...]
    cum_ref[...] = cum
    cumt_ref[0, 0] = jnp.transpose(cum)[0:ATT_HEADS, :]
    carry[...] = cum[tm - 1:tm, :]

    gz = _gelu(jnp.dot(hb, wz_ref[...], preferred_element_type=f32))
    u = gz[:, 0:GM_WIDTH]
    v = gz[:, GM_WIDTH:]
    mu = jnp.mean(v, axis=-1, keepdims=True)
    var = jnp.mean(jnp.square(v - mu), axis=-1, keepdims=True)
    vn = ((v - mu) * lax.rsqrt(var + EPS) * lng_ref[...]).astype(bf16)
    tril = (lax.broadcasted_iota(jnp.int32, (GM_CHUNK, GM_CHUNK), 0)
            >= lax.broadcasted_iota(jnp.int32, (GM_CHUNK, GM_CHUNK), 1))
    w_sp = [jnp.where(tril, wsp_ref[g], 0.0).astype(bf16) for g in range(GM_GROUPS)]
    rows = []
    for ck in range(tm // GM_CHUNK):
        r0 = ck * GM_CHUNK
        cols = []
        for g in range(GM_GROUPS):
            c0 = g * GM_GROUP_DIM
            mixed = jnp.dot(w_sp[g], vn[r0:r0 + GM_CHUNK, c0:c0 + GM_GROUP_DIM],
                            preferred_element_type=f32) + bsp_ref[g]
            cols.append(u[r0:r0 + GM_CHUNK, c0:c0 + GM_GROUP_DIM] * mixed)
        rows.append(jnp.concatenate(cols, axis=1))
    yb = jnp.concatenate(rows, axis=0).astype(bf16)
    ybp = jnp.dot(yb, wb_ref[...], preferred_element_type=f32)

    sg = jax.nn.sigmoid(jnp.dot(hb, wg_ref[...], preferred_element_type=f32))
    sga_ref[...] = sg[:, 0:d].astype(bf16)
    gb_ref[...] = (sg[:, d:] * ybp).astype(bf16)


def _input_projection(x, mod, norm1_g, w_in, b_forget, ln_v_g, w_spatial, b_spatial, w_branch_b):
    B, S, d = x.shape
    n = B * S
    tm = INPROJ_TOK_BLOCK
    bf16 = jnp.bfloat16
    p0, p1, p2, p3, p4, p5 = SPLIT_POINTS
    w_qkv = w_in[:, 0:p2].astype(bf16)
    w_f = jnp.pad(w_in[:, p2:p3], ((0, 0), (0, LANES - ATT_HEADS))).astype(bf16)
    b_f = jnp.pad(b_forget, (0, LANES - ATT_HEADS)).reshape(1, LANES)
    w_z = w_in[:, p3:p4].astype(bf16)
    w_g = w_in[:, p4:].astype(bf16)
    nt = S // tm
    tok = lambda w: pl.BlockSpec((tm, w), lambda b, i: (b * nt + i, 0))
    full = lambda a: pl.BlockSpec(a.shape, lambda b, i: (0,) * a.ndim)
    args = (x, mod, norm1_g.reshape(1, d), w_qkv, w_f, b_f, w_z, w_g, ln_v_g.reshape(1, GM_WIDTH),
            w_spatial, b_spatial.reshape(GM_GROUPS, GM_CHUNK, 1), w_branch_b.astype(bf16))
    return pl.pallas_call(
        _inproj_kernel,
        grid=(B, nt),
        in_specs=[pl.BlockSpec((1, tm, d), lambda b, i: (b, i, 0)),
                  pl.BlockSpec((1, 6, d), lambda b, i: (b, 0, 0))] + [full(a) for a in args[2:]],
        out_specs=[tok(3 * ATT_WIDTH), tok(LANES),
                   pl.BlockSpec((1, 1, ATT_HEADS, tm), lambda b, i: (b, i, 0, 0)), tok(d), tok(d)],
        out_shape=[jax.ShapeDtypeStruct((n, 3 * ATT_WIDTH), bf16),
                   jax.ShapeDtypeStruct((n, LANES), jnp.float32),
                   jax.ShapeDtypeStruct((B, nt, ATT_HEADS, tm), jnp.float32),
                   jax.ShapeDtypeStruct((n, d), bf16),
                   jax.ShapeDtypeStruct((n, d), bf16)],
        scratch_shapes=[pltpu.VMEM((1, LANES), jnp.float32)],
        compiler_params=pltpu.CompilerParams(dimension_semantics=("arbitrary", "arbitrary"),
                                             vmem_limit_bytes=56 * 1024 * 1024),
        name="input_projection",
    )(*args)


ATT_BLOCK = 512


def _fox_kernel(q_ref, k_ref, v_ref, cum_ref, cumt_ref, o_ref, *, blk, ratio):
    f32 = jnp.float32
    hp = pl.program_id(1)
    i = pl.program_id(2)
    dh = ATT_HEAD_DIM
    q2 = q_ref[...]
    first = lax.broadcasted_iota(jnp.int32, (1, 2 * dh), 1) < dh
    qs = (jnp.where(first, q2, jnp.zeros_like(q2)), jnp.where(first, jnp.zeros_like(q2), q2))
    cum_blk = cum_ref[...]
    lane = lax.broadcasted_iota(jnp.int32, cum_blk.shape, 1)
    cqs = [jnp.sum(jnp.where(lane == 2 * hp + e, cum_blk, 0.0), axis=1, keepdims=True)
           for e in range(2)]

    def block(j, carry, masked):
        off = pl.multiple_of(j * blk, blk)
        k2 = k_ref[0, pl.ds(off, blk), :]
        v2 = v_ref[0, pl.ds(off, blk), :]
        out = []
        for e in range(2):
            m, acc = carry[e]
            ck = jnp.concatenate([cumt_ref[0, j * ratio + a, pl.ds(2 * hp + e, 1), :]
                                  for a in range(ratio)], axis=1)
            s = lax.dot_general(qs[e], k2, (((1,), (1,)), ((), ())), preferred_element_type=f32)
            s = s + (cqs[e] - ck)
            if masked:
                causal = (lax.broadcasted_iota(jnp.int32, (blk, blk), 0)
                          >= lax.broadcasted_iota(jnp.int32, (blk, blk), 1))
                s = jnp.where(causal, s, -jnp.inf)
            m_new = jnp.maximum(m, jnp.max(s, axis=1, keepdims=True))
            alpha = jnp.exp(m - m_new)
            p = jnp.exp(s - m_new).astype(v2.dtype)
            own = first if e == 0 else jnp.logical_not(first)
            v_ones = jnp.where(own, v2, jnp.ones_like(v2))
            acc = alpha * acc + jnp.dot(p, v_ones, preferred_element_type=f32)
            out.append((m_new, acc))
        return tuple(out)

    one = (jnp.full((blk, 1), -1e30, f32), jnp.zeros((blk, 2 * dh), f32))
    carry = lax.fori_loop(0, i, lambda j, c: block(j, c, False), (one, one))
    (_, acc0), (_, acc1) = block(i, carry, True)
    ratio0 = acc0 / pltpu.roll(acc0, dh, axis=1)
    ratio1 = acc1 / pltpu.roll(acc1, dh, axis=1)
    o_ref[...] = jnp.where(first, ratio0, ratio1).astype(o_ref.dtype)


def _fox_attention(qkv, cum, cumt, batch, seq_len):
    n = qkv.shape[0]
    tm = cumt.shape[-1]
    blk = max(min(ATT_BLOCK, seq_len), tm)
    nb = seq_len // blk
    pair = 2 * ATT_HEAD_DIM
    n_pairs = ATT_WIDTH // pair
    qkv3 = qkv.reshape(batch, seq_len, 3 * ATT_WIDTH)
    return pl.pallas_call(
        functools.partial(_fox_kernel, blk=blk, ratio=blk // tm),
        grid=(batch, n_pairs, nb),
        in_specs=[pl.BlockSpec((blk, pair), lambda b, hp, i: (b * nb + i, hp)),
                  pl.BlockSpec((1, seq_len, pair), lambda b, hp, i: (b, 0, n_pairs + hp)),
                  pl.BlockSpec((1, seq_len, pair), lambda b, hp, i: (b, 0, 2 * n_pairs + hp)),
                  pl.BlockSpec((blk, LANES), lambda b, hp, i: (b * nb + i, 0)),
                  pl.BlockSpec((1,) + cumt.shape[1:], lambda b, hp, i: (b, 0, 0, 0))],
        out_specs=pl.BlockSpec((blk, pair), lambda b, hp, i: (b * nb + i, hp)),
        out_shape=jax.ShapeDtypeStruct((n, ATT_WIDTH), jnp.bfloat16),
        compiler_params=pltpu.CompilerParams(
            dimension_semantics=("arbitrary", "arbitrary", "arbitrary")),
        name="fox_attention",
    )(qkv, qkv3, qkv3, cum, cumt)


def _topk_rows(scores, k):
    rows, t = scores[0].shape
    iota = lax.broadcasted_iota(jnp.int32, (rows, t), 0).astype(jnp.float32)
    slot = lax.broadcasted_iota(jnp.int32, (k, t), 0)
    scores = list(scores)
    vals = [jnp.zeros((k, t), jnp.float32) for _ in scores]
    ids = [jnp.zeros((k, t), jnp.float32) for _ in scores]
    for j in range(k):
        for i, s in enumerate(scores):
            m = jnp.max(s, axis=0, keepdims=True)
            am = jnp.min(jnp.where(s == m, iota, float(rows)), axis=0, keepdims=True)
            vals[i] = jnp.where(slot == j, m, vals[i])
            ids[i] = jnp.where(slot == j, am, ids[i])
            scores[i] = jnp.where(iota == am, -jnp.inf, s)
    return [(v, i.astype(jnp.int32)) for v, i in zip(vals, ids)]


def _select_rows(table, pos):
    out = jnp.zeros(pos.shape, table.dtype)
    for r in range(table.shape[0]):
        out = jnp.where(pos == r, table[r:r + 1, :], out)
    return out


def _post_kernel(ya_ref, sga_ref, gb_ref, x_ref, mod_ref, n2g_ref, wa_ref, wo_ref, wq_ref, keys_ref,
                 x1_ref, h2_ref, idx_ref, gates_ref, q_scr):
    f32 = jnp.float32
    a = jnp.dot(ya_ref[...], wa_ref[...], preferred_element_type=f32)
    merged = sga_ref[...].astype(f32) * a + gb_ref[...].astype(f32)
    o = jnp.dot(merged.astype(jnp.bfloat16), wo_ref[...], preferred_element_type=f32)
    g1 = mod_ref[0, 2:3, :]
    sh2 = mod_ref[0, 3:4, :]
    sc2 = mod_ref[0, 4:5, :]
    x1 = x_ref[...] + g1 * o
    x1_ref[...] = x1
    h2 = x1 * lax.rsqrt(jnp.mean(x1 * x1, axis=-1, keepdims=True) + EPS) * n2g_ref[...]
    h2 = h2 * (1.0 + sc2) + sh2
    h2_ref[...] = h2
    qp = jnp.dot(h2.astype(jnp.bfloat16), wq_ref[...], preferred_element_type=f32)
    for j in range(2 * PEER_HEADS):
        q_scr[j] = qp[:, j * PEER_HALF:(j + 1) * PEER_HALF]

    half = PEER_TOPK // 2
    tail0 = PEER_TOPK + (half - 1) * half

    def heads(hg, carry):
        hs = [hg * ROUTE_HEADS + e for e in range(ROUTE_HEADS)]
        scores = []
        for h in hs:
            for p in range(2):
                q = q_scr[2 * h + p]
                keys = keys_ref[2 * h + p]
                scores.append(lax.dot_general(keys, q, (((1,), (1,)), ((), ())),
                                              precision=lax.Precision.HIGHEST,
                                              preferred_element_type=f32))
        tops = _topk_rows(scores, PEER_TOPK)
        cands = []
        for e in range(ROUTE_HEADS):
            (s1, _), (s2, _) = tops[2 * e], tops[2 * e + 1]
            blocks = [s1[0:1, :] + s2]
            blocks += [s1[a:a + 1, :] + s2[0:half, :] for a in range(1, half)]
            blocks += [s1[half:, :] + s2[0:1, :]]
            cands.append(jnp.concatenate(blocks, axis=0))
        best = _topk_rows(cands, PEER_TOPK)
        for e, h in enumerate(hs):
            (_, i1), (_, i2) = tops[2 * e], tops[2 * e + 1]
            vals, pos = best[e]
            mid = pos - PEER_TOPK
            ra = jnp.where(pos < PEER_TOPK, 0,
                           jnp.where(pos < tail0, 1 + (mid >> (half.bit_length() - 1)), pos - tail0 + half))
            rb = jnp.where(pos < PEER_TOPK, pos, jnp.where(pos < tail0, mid & (half - 1), 0))
            eid = _select_rows(i1, ra) * N_KEYS + _select_rows(i2, rb)
            ex = jnp.exp(vals - vals[0:1, :])
            g = ex / jnp.sum(ex, axis=0, keepdims=True)
            row = pl.multiple_of(h * PEER_TOPK, PEER_TOPK)
            idx_ref[pl.ds(row, PEER_TOPK), :] = eid
            gates_ref[pl.ds(row, PEER_TOPK), :] = g
        return carry

    lax.fori_loop(0, PEER_HEADS // ROUTE_HEADS, heads, 0)


POST_TOK_BLOCK = 256
ROUTE_HEADS = 4


def _post_attention(ya, sga, gb, x, mod, norm2_g, w_a, w_out, w_query, sub_keys, seq_len, n):
    d = x.shape[1]
    tm = POST_TOK_BLOCK
    assert n % tm == 0
    blocks_per_seq = seq_len // tm
    aw = ya.shape[1]
    keys = sub_keys.reshape(2 * PEER_HEADS, N_KEYS, PEER_HALF)
    tok = lambda w: pl.BlockSpec((tm, w), lambda i: (i, 0))
    full = lambda a: pl.BlockSpec(a.shape, lambda i: (0,) * a.ndim)
    return pl.pallas_call(
        _post_kernel,
        grid=(n // tm,),
        in_specs=[tok(aw), tok(d), tok(d), tok(d),
                  pl.BlockSpec((1, 6, d), lambda i: (i // blocks_per_seq, 0, 0)),
                  full(norm2_g), full(w_a), full(w_out), full(w_query), full(keys)],
        out_specs=[tok(d), tok(d),
                   pl.BlockSpec((PEER_SLOTS, tm), lambda i: (0, i)),
                   pl.BlockSpec((PEER_SLOTS, tm), lambda i: (0, i))],
        out_shape=[jax.ShapeDtypeStruct((n, d), jnp.float32),
                   jax.ShapeDtypeStruct((n, d), jnp.float32),
                   jax.ShapeDtypeStruct((PEER_SLOTS, n), jnp.int32),
                   jax.ShapeDtypeStruct((PEER_SLOTS, n), jnp.float32)],
        scratch_shapes=[pltpu.VMEM((2 * PEER_HEADS, tm, PEER_HALF), jnp.float32)],
        compiler_params=pltpu.CompilerParams(dimension_semantics=("arbitrary",),
                                             vmem_limit_bytes=48 * 1024 * 1024),
        name="post_attention",
    )(ya, sga, gb, x, mod, norm2_g, w_a, w_out, w_query, keys)


def _rowsum_bcast(p, ones_bf16):
    hi = p.astype(jnp.bfloat16)
    lo = (p - hi.astype(jnp.float32)).astype(jnp.bfloat16)
    return (jnp.dot(hi, ones_bf16, preferred_element_type=jnp.float32)
            + jnp.dot(lo, ones_bf16, preferred_element_type=jnp.float32))


def _word_halves(w):
    return (lax.bitcast_convert_type(w & jnp.uint32(0xFFFF0000), jnp.float32),
            lax.bitcast_convert_type(w << 16, jnp.float32))


def _eval_experts(chunk, hrow, grow):
    half = hrow.shape[1] // LANES // 2
    ones_bf16 = jnp.ones((LANES, LANES), jnp.bfloat16)
    eye = (lax.broadcasted_iota(jnp.int32, (PEER_SLOTS, LANES), 0)
           == lax.broadcasted_iota(jnp.int32, (PEER_SLOTS, LANES), 1))
    hpart = lambda c: hrow[:, c * LANES:(c + 1) * LANES]
    psum = None
    for c in range(half):
        hi, lo = _word_halves(chunk(c))
        p = hi * hpart(c) + lo * hpart(c + half)
        psum = p if psum is None else psum + p
    act = _gelu(_rowsum_bcast(psum, ones_bf16))
    gcol = _rowsum_bcast(jnp.where(eye, grow, 0.0), ones_bf16)
    coef = gcol * act
    outs_hi, outs_lo = [], []
    for c in range(half):
        hi, lo = _word_halves(chunk(c + half))
        outs_hi.append(jnp.sum(coef * hi, axis=0, keepdims=True))
        outs_lo.append(jnp.sum(coef * lo, axis=0, keepdims=True))
    return jnp.concatenate(outs_hi + outs_lo, axis=-1)


def _finish_block(x_ref, g2_ref, fg_ref, peer, out_ref):
    y = x_ref[...] + g2_ref[0] * peer[...]
    out_ref[...] = y * lax.rsqrt(jnp.mean(y * y, axis=-1, keepdims=True) + EPS) * fg_ref[...]


def _expert_kernel(idx_ref, gates_ref, h_ref, x_ref, g2_ref, fg_ref, tab_ref, out_ref,
                   *scratch, tok_block, n_slots):
    bufs = scratch[:n_slots]
    peer, sem = scratch[n_slots], scratch[n_slots + 1]
    d_model = h_ref.shape[-1]
    n_chunks = d_model // LANES
    rows_per_tok = PEER_SLOTS * n_chunks

    def issue(t, s):
        for r in range(PEER_SLOTS):
            row = pl.multiple_of(idx_ref[t, r] * n_chunks, n_chunks)
            pltpu.make_async_copy(tab_ref.at[pl.ds(row, n_chunks), :],
                                  bufs[s].at[pl.ds(r * n_chunks, n_chunks), :],
                                  sem.at[s]).start(priority=r % DMA_THREADS)

    def wait(s):
        pltpu.make_async_copy(tab_ref.at[pl.ds(0, rows_per_tok), :], bufs[s], sem.at[s]).wait()

    def compute(t, s):
        chunk = lambda c: bufs[s][pl.ds(c, PEER_SLOTS, stride=n_chunks), :]
        peer[pl.ds(t, 1), :] = _eval_experts(chunk, h_ref[pl.ds(t, 1), :], gates_ref[pl.ds(t, 1), :])

    def step(t, s, prefetch):
        wait(s)
        if prefetch:
            issue(t + n_slots - 1, (s - 1) % n_slots)
        compute(t, s)

    for t in range(n_slots - 1):
        issue(t, t)

    n_groups = tok_block // n_slots

    def group(g, carry):
        for s in range(n_slots):
            step(g * n_slots + s, s, True)
        return carry

    lax.fori_loop(0, n_groups - 1, group, 0)
    for s in range(n_slots):
        t = (n_groups - 1) * n_slots + s
        step(t, s, t + n_slots - 1 < tok_block)

    _finish_block(x_ref, g2_ref, fg_ref, peer, out_ref)


def _expert_cost(n_tok, d):
    pairs = n_tok * PEER_SLOTS
    return pl.CostEstimate(flops=4 * pairs * d, transcendentals=pairs,
                           bytes_accessed=4 * pairs * d + 12 * n_tok * d + 8 * pairs)


def _sc_tanh(y):
    return 1.0 - 2.0 / (jnp.exp(2.0 * y) + 1.0)


def _sc_peer_experts(table3, ids, gates, h2):
    n_tok = ids.shape[0] // PEER_SLOTS
    _, n_chunks, lanes = table3.shape
    assert h2.shape[1] == n_chunks * lanes
    info = plsc.get_sparse_core_info()
    sl = info.num_lanes
    n_workers = info.num_cores * info.num_subcores
    tok_per_worker = n_tok // n_workers
    assert tok_per_worker * n_workers == n_tok and tok_per_worker % 2 == 0
    rows = SC_GATHER_ROWS
    n_sub = PEER_SLOTS // rows
    assert n_sub % 2 == 0
    half = n_chunks // 2
    pieces = [(c, k * sl) for c in range(half) for k in range(lanes // sl)]
    mesh = plsc.VectorSubcoreMesh(core_axis_name="c", subcore_axis_name="s")
    buf = lambda dt: pltpu.VMEM((rows, n_chunks, lanes), dt)

    @functools.partial(
        pl.kernel, mesh=mesh,
        out_type=jax.ShapeDtypeStruct((n_tok, n_chunks, lanes), jnp.float32),
        scratch_types=[pltpu.VMEM((PEER_SLOTS,), jnp.int32), pltpu.VMEM((PEER_SLOTS,), jnp.int32),
                       pltpu.VMEM((PEER_SLOTS,), jnp.float32), pltpu.VMEM((PEER_SLOTS,), jnp.float32),
                       pltpu.VMEM((n_chunks * lanes,), jnp.float32),
                       pltpu.VMEM((n_chunks * lanes,), jnp.float32),
                       buf(jnp.uint32), buf(jnp.uint32), pltpu.VMEM((n_chunks, lanes), jnp.float32),
                       pltpu.SemaphoreType.DMA((2,)), pltpu.SemaphoreType.DMA((2,))],
        compiler_params=pltpu.CompilerParams(needs_layout_passes=False),
        cost_estimate=_expert_cost(n_tok, n_chunks * lanes),
        name="sc_peer_experts")
    def run(table_hbm, ids_hbm, gates_hbm, h_hbm, peer_hbm, ids0, ids1, g0, g1, h0, h1, rows0, rows1,
            out_v, gsem, isem):
        bufs = (rows0, rows1)
        ids_p, g_p, h_p = (ids0, ids1), (g0, g1), (h0, h1)
        t_first = (lax.axis_index("s") * info.num_cores + lax.axis_index("c")) * tok_per_worker

        def token_inputs(t, p):
            off = pl.multiple_of(t * PEER_SLOTS, PEER_SLOTS)
            return (pltpu.make_async_copy(ids_hbm.at[pl.ds(off, PEER_SLOTS)], ids_p[p], isem.at[p]),
                    pltpu.make_async_copy(gates_hbm.at[pl.ds(off, PEER_SLOTS)], g_p[p], isem.at[p]),
                    pltpu.make_async_copy(h_hbm.at[t], h_p[p], isem.at[p]))

        def gather(p, s):
            return pltpu.make_async_copy(table_hbm.at[ids_p[p].at[pl.ds(s * rows, rows)]],
                                         bufs[s % 2], gsem.at[s % 2])

        for cp in token_inputs(t_first, 0):
            cp.start()
        for cp in token_inputs(t_first, 0):
            cp.wait()
        gather(0, 0).start()

        def one_token(ti, p):
            t = t_first + ti
            g_v, h_v = g_p[p], h_p[p]
            has_next = ti + 1 < tok_per_worker

            @pl.when(has_next)
            def _():
                for cp in token_inputs(t + 1, 1 - p):
                    cp.start()

            for c, k in pieces:
                out_v[c, pl.ds(k, sl)] = jnp.zeros((sl,), jnp.float32)
                out_v[c + half, pl.ds(k, sl)] = jnp.zeros((sl,), jnp.float32)
            for s in range(n_sub):
                gather(p, s).wait()
                if s + 1 < n_sub:
                    gather(p, s + 1).start()
                else:
                    @pl.when(has_next)
                    def _():
                        for cp in token_inputs(t + 1, 1 - p):
                            cp.wait()
                        gather(1 - p, 0).start()
                rbuf = bufs[s % 2]

                @pl.loop(0, rows, step=SC_ROW_GROUP)
                def _(r0):
                    accs = [jnp.zeros((sl,), jnp.float32) for _ in range(SC_ROW_GROUP)]
                    for c, k in pieces:
                        h_hi = h_v[pl.ds(c * lanes + k, sl)]
                        h_lo = h_v[pl.ds((c + half) * lanes + k, sl)]
                        for j in range(SC_ROW_GROUP):
                            hi, lo = _word_halves(rbuf[r0 + j, c, pl.ds(k, sl)])
                            accs[j] = accs[j] + hi * h_hi + lo * h_lo
                    coefs = []
                    for j in range(SC_ROW_GROUP):
                        a = jnp.broadcast_to(jnp.sum(accs[j]), (sl,))
                        act = 0.5 * a * (1.0 + _sc_tanh(0.7978845608028654 * (a + 0.044715 * (a * a * a))))
                        gate = plsc.load_gather(g_v, [jnp.broadcast_to(s * rows + r0 + j, (sl,))])
                        coefs.append(gate * act)
                    for b0 in range(0, len(pieces), SC_STORE_BATCH):
                        tots = []
                        for c, k in pieces[b0:b0 + SC_STORE_BATCH]:
                            t_hi = t_lo = None
                            for j in range(SC_ROW_GROUP):
                                hi, lo = _word_halves(rbuf[r0 + j, c + half, pl.ds(k, sl)])
                                t_hi = coefs[j] * hi if t_hi is None else t_hi + coefs[j] * hi
                                t_lo = coefs[j] * lo if t_lo is None else t_lo + coefs[j] * lo
                            tots.append((t_hi, t_lo))
                        for (c, k), (t_hi, t_lo) in zip(pieces[b0:b0 + SC_STORE_BATCH], tots):
                            plsc.addupdate(out_v.at[c, pl.ds(k, sl)], t_hi)
                            plsc.addupdate(out_v.at[c + half, pl.ds(k, sl)], t_lo)

            pltpu.sync_copy(out_v, peer_hbm.at[t])

        @pl.loop(0, tok_per_worker, step=2)
        def _(ti):
            one_token(ti, 0)
            one_token(ti + 1, 1)

    return run(table3, ids, gates, h2)


def _finish_kernel(peer_ref, x_ref, g2_ref, fg_ref, prev_ref, out_ref):
    del prev_ref
    _finish_block(x_ref, g2_ref, fg_ref, peer_ref, out_ref)


def _finish_tokens(peer, x1, g2, fg, prev, seq_len):
    d = x1.shape[1]
    tm = FINISH_TOK_BLOCK
    return pl.pallas_call(
        _finish_kernel,
        grid=(peer.shape[0] // tm,),
        in_specs=[pl.BlockSpec((tm, d), lambda i: (i, 0)),
                  pl.BlockSpec((tm, d), lambda i: (i, 0)),
                  pl.BlockSpec((1, 1, d), lambda i: (i * tm // seq_len, 0, 0)),
                  pl.BlockSpec((1, d), lambda i: (0, 0)),
                  pl.BlockSpec(memory_space=pl.ANY)],
        out_specs=pl.BlockSpec((tm, d), lambda i: (i, 0)),
        out_shape=jax.ShapeDtypeStruct(prev.shape, prev.dtype),
        input_output_aliases={4: 0},
        compiler_params=pltpu.CompilerParams(dimension_semantics=("arbitrary",)),
        name="finish_tokens",
    )(peer, x1, g2, fg, prev)


def _pack_kernel(u_ref, v_ref, o_ref):
    te, d = u_ref.shape
    n_chunks = d // LANES

    def pairs(x):
        r = lax.bitcast_convert_type(x.astype(jnp.bfloat16).astype(jnp.float32), jnp.uint32)
        return r[:, :d // 2] | (r[:, d // 2:] >> 16)

    words = jnp.concatenate([pairs(u_ref[...]), pairs(v_ref[...])], axis=1)
    for c in range(n_chunks):
        o_ref[pl.ds(c, te, stride=n_chunks), :] = words[:, c * LANES:(c + 1) * LANES]


def _pack_expert_table(expert_u, expert_v):
    n_experts, d = expert_u.shape
    te = PACK_EXPERT_BLOCK
    n_chunks = d // LANES
    packed = pl.pallas_call(
        _pack_kernel,
        grid=(n_experts // te,),
        in_specs=[pl.BlockSpec((te, d), lambda i: (i, 0)), pl.BlockSpec((te, d), lambda i: (i, 0))],
        out_specs=pl.BlockSpec((te * n_chunks, LANES), lambda i: (i, 0)),
        out_shape=jax.ShapeDtypeStruct((n_experts * n_chunks, LANES), jnp.uint32),
        name="pack_experts",
    )(expert_u, expert_v)
    return packed.reshape(n_experts, n_chunks, LANES)


def _route_experts_kernel(ya_ref, sga_ref, gb_ref, x_ref, mod_ref, n2g_ref, wa_ref, wo_ref, wq_ref, keys_ref,
                          fg_ref, tab_ref, out_ref, q_scr, x1_v, h2_v, idx_v, gates_v, idx_tv, gates_tv,
                          idx_s, isem, *expert_scratch, tok_block, n_slots):
    _post_kernel(ya_ref, sga_ref, gb_ref, x_ref, mod_ref, n2g_ref, wa_ref, wo_ref, wq_ref, keys_ref,
                 x1_v, h2_v, idx_v, gates_v, q_scr)
    idx_tv[...] = idx_v[...].T
    gates_tv[...] = gates_v[...].T
    ids_to_smem = pltpu.make_async_copy(idx_tv, idx_s, isem)
    ids_to_smem.start()
    ids_to_smem.wait()
    _expert_kernel(idx_s, gates_tv, h2_v, x1_v, mod_ref.at[:, 5:6, :], fg_ref, tab_ref, out_ref,
                   *expert_scratch, tok_block=tok_block, n_slots=n_slots)


def _tc_route_experts(ya, sga, gb, x, mod, norm2_g, w_a, w_out, w_query, sub_keys, final_g, table,
                      seq_len, first_tok):
    n, d = x.shape
    tb = EXPERT_TOK_BLOCK
    assert tb == POST_TOK_BLOCK and tb % EXPERT_SLOTS == 0 and seq_len % tb == 0 and first_tok % tb == 0
    first = first_tok // tb
    blocks_per_seq = seq_len // tb
    n_chunks = d // LANES
    aw = ya.shape[1]
    keys = sub_keys.reshape(2 * PEER_HEADS, N_KEYS, PEER_HALF)
    tok = lambda w: pl.BlockSpec((tb, w), lambda i: (i + first, 0))
    full = lambda a: pl.BlockSpec(a.shape, lambda i: (0,) * a.ndim)
    fg = final_g.reshape(1, d)
    f32, i32 = jnp.float32, jnp.int32
    return pl.pallas_call(
        functools.partial(_route_experts_kernel, tok_block=tb, n_slots=EXPERT_SLOTS),
        grid=((n - first_tok) // tb,),
        in_specs=[tok(aw), tok(d), tok(d), tok(d),
                  pl.BlockSpec((1, 6, d), lambda i: ((i + first) // blocks_per_seq, 0, 0)),
                  full(norm2_g), full(w_a), full(w_out), full(w_query), full(keys), full(fg),
                  pl.BlockSpec(memory_space=pl.ANY)],
        out_specs=tok(d),
        out_shape=jax.ShapeDtypeStruct((n, d), f32),
        scratch_shapes=(
            [pltpu.VMEM((2 * PEER_HEADS, tb, PEER_HALF), f32), pltpu.VMEM((tb, d), f32),
             pltpu.VMEM((tb, d), f32), pltpu.VMEM((PEER_SLOTS, tb), i32), pltpu.VMEM((PEER_SLOTS, tb), f32),
             pltpu.VMEM((tb, PEER_SLOTS), i32), pltpu.VMEM((tb, PEER_SLOTS), f32),
             pltpu.SMEM((tb, PEER_SLOTS), i32), pltpu.SemaphoreType.DMA]
            + [pltpu.VMEM((PEER_SLOTS * n_chunks, LANES), jnp.uint32) for _ in range(EXPERT_SLOTS)]
            + [pltpu.VMEM((tb, d), f32), pltpu.SemaphoreType.DMA((EXPERT_SLOTS,))]),
        compiler_params=pltpu.CompilerParams(dimension_semantics=("arbitrary",),
                                             vmem_limit_bytes=56 * 1024 * 1024),
        cost_estimate=_expert_cost(n - first_tok, d),
        name="route_experts",
    )(ya, sga, gb, x, mod, norm2_g, w_a, w_out, w_query, keys, fg, table.reshape(-1, LANES))


def kernel(x, c, w_mod, b_mod, norm1_g, w_in, b_forget, ln_v_g, w_spatial, b_spatial, w_branch_a, w_branch_b, w_out, norm2_g, w_query, sub_keys, expert_u, expert_v, final_g):
    B, S, D = x.shape
    n = B * S
    bf16 = jnp.bfloat16
    assert w_mod.shape[0] == 1, "the final RMSNorm is fused into the single layer's expert kernels"
    l = 0
    mod = _modulation(c, w_mod[l], b_mod[l]).reshape(B, 6, D)
    table = _pack_expert_table(expert_u[l], expert_v[l])

    qkv, cum, cumt, sga, gb = _input_projection(x, mod, norm1_g[l], w_in[l], b_forget[l], ln_v_g[l],
                                                w_spatial[l], b_spatial[l], w_branch_b[l])
    y_a = _fox_attention(qkv, cum, cumt, B, S)
    post_args = (y_a, sga, gb, x.reshape(n, D), mod, norm2_g[l].reshape(1, D), w_branch_a[l].astype(bf16),
                 w_out[l].astype(bf16), w_query[l].astype(bf16), sub_keys[l])

    n_sc = n * SC_SHARE_PERCENT // 100 // SC_SHARE_ALIGN * SC_SHARE_ALIGN
    if n_sc > 0:
        x1_a, h2_a, idx_t, gates_t = _post_attention(*post_args, S, n_sc)
        peer_sc = _sc_peer_experts(table, idx_t.T.reshape(-1), gates_t.T.reshape(-1), h2_a)
    out = _tc_route_experts(*post_args, final_g, table, S, n_sc)
    if n_sc > 0:
        out = _finish_tokens(peer_sc.reshape(n_sc, D), x1_a, mod[:, 5:6, :], final_g.reshape(1, D), out, S)
    return out.reshape(B, S, D)
```

```python
import functools

import jax
import jax.numpy as jnp
from jax import lax
from jax.experimental import pallas as pl
from jax.experimental.pallas import tpu as pltpu
from jax.experimental.pallas import tpu_sc as plsc

D_MODEL = 1024
ATT_HEADS = 8
ATT_HEAD_DIM = 64
ATT_WIDTH = ATT_HEADS * ATT_HEAD_DIM
GM_GROUPS = 4
GM_GROUP_DIM = 128
GM_WIDTH = GM_GROUPS * GM_GROUP_DIM
GM_CHUNK = 128
PEER_HEADS = 8
PEER_KEY_DIM = 256
PEER_HALF = PEER_KEY_DIM // 2
N_KEYS = 128
PEER_TOPK = 16
PEER_SLOTS = PEER_HEADS * PEER_TOPK
SPLIT_POINTS = (ATT_WIDTH, 2 * ATT_WIDTH, 3 * ATT_WIDTH, 3 * ATT_WIDTH + ATT_HEADS,
                3 * ATT_WIDTH + ATT_HEADS + 2 * GM_WIDTH,
                3 * ATT_WIDTH + ATT_HEADS + 2 * GM_WIDTH + D_MODEL)
EPS = 1e-6

LANES = 128
EXPERT_TOK_BLOCK = 256
EXPERT_SLOTS = 8
DMA_THREADS = 2
FINISH_TOK_BLOCK = 256
PACK_EXPERT_BLOCK = 256
SC_GATHER_ROWS = 32
SC_ROW_GROUP = 4
SC_STORE_BATCH = 8
SC_SHARE_PERCENT = 53
SC_SHARE_ALIGN = 256


def _gelu(x):
    return 0.5 * x * (1.0 + jnp.tanh(0.7978845608028654 * (x + 0.044715 * (x * x * x))))


def _mod_kernel(c_ref, w_ref, b_ref, o_ref):
    c = c_ref[...]
    sc = c * jax.nn.sigmoid(c)
    o_ref[...] = jnp.dot(sc, w_ref[...], precision=lax.Precision.HIGHEST,
                         preferred_element_type=jnp.float32) + b_ref[...]


def _modulation(c, w_mod, b_mod):
    b, d = c.shape
    cols = w_mod.shape[1]
    return pl.pallas_call(
        _mod_kernel,
        grid=(cols // d,),
        in_specs=[pl.BlockSpec((b, d), lambda j: (0, 0)),
                  pl.BlockSpec((d, d), lambda j: (0, j)),
                  pl.BlockSpec((1, d), lambda j: (0, j))],
        out_specs=pl.BlockSpec((b, d), lambda j: (0, j)),
        out_shape=jax.ShapeDtypeStruct((b, cols), jnp.float32),
        name="modulation",
    )(c, w_mod, b_mod.reshape(1, cols))


INPROJ_TOK_BLOCK = 256


def _inproj_kernel(x_ref, mod_ref, n1g_ref, wqkv_ref, wf_ref, bf_ref, wz_ref, wg_ref, lng_ref,
                   wsp_ref, bsp_ref, wb_ref, qkv_ref, cum_ref, cumt_ref, sga_ref, gb_ref, carry):
    f32, bf16 = jnp.float32, jnp.bfloat16
    tm, d = x_ref.shape[1], x_ref.shape[2]
    x = x_ref[0]
    sh1 = mod_ref[0, 0:1, :]
    sc1 = mod_ref[0, 1:2, :]
    h = x * lax.rsqrt(jnp.mean(x * x, axis=-1, keepdims=True) + EPS) * n1g_ref[...]
    hb = (h * (1.0 + sc1) + sh1).astype(bf16)

    qkv = jnp.dot(hb, wqkv_ref[...], preferred_element_type=f32)
    qkv_ref[:, 0:ATT_WIDTH] = (qkv[:, 0:ATT_WIDTH] * (ATT_HEAD_DIM ** -0.5)).astype(bf16)
    qkv_ref[:, ATT_WIDTH:] = qkv[:, ATT_WIDTH:].astype(bf16)

    f = jnp.dot(hb, wf_ref[...], preferred_element_type=f32) + bf_ref[...]
    logf = jnp.minimum(f, 0.0) - jnp.log1p(jnp.exp(-jnp.abs(f)))

    @pl.when(pl.program_id(1) == 0)
    def _():
        carry[...] = jnp.zeros_like(carry)

    tri = (lax.broadcasted_iota(jnp.int32, (tm, tm), 0)
           >= lax.broadcasted_iota(jnp.int32, (tm, tm), 1)).astype(f32)
    cum = jnp.dot(tri, logf, precision=lax.Precision.HIGHEST, preferred_element_type=f32) + carry[...]
    cum_ref[...] = cum
    cumt_ref[0, 0] = jnp.transpose(cum)[0:ATT_HEADS, :]
    carry[...] = cum[tm - 1:tm, :]

    gz = _gelu(jnp.dot(hb, wz_ref[...], preferred_element_type=f32))
    u = gz[:, 0:GM_WIDTH]
    v = gz[:, GM_WIDTH:]
    mu = jnp.mean(v, axis=-1, keepdims=True)
    var = jnp.mean(jnp.square(v - mu), axis=-1, keepdims=True)
    vn = ((v - mu) * lax.rsqrt(var + EPS) * lng_ref[...]).astype(bf16)
    tril = (lax.broadcasted_iota(jnp.int32, (GM_CHUNK, GM_CHUNK), 0)
            >= lax.broadcasted_iota(jnp.int32, (GM_CHUNK, GM_CHUNK), 1))
    w_sp = [jnp.where(tril, wsp_ref[g], 0.0).astype(bf16) for g in range(GM_GROUPS)]
    rows = []
    for ck in range(tm // GM_CHUNK):
        r0 = ck * GM_CHUNK
        cols = []
        for g in range(GM_GROUPS):
            c0 = g * GM_GROUP_DIM
            mixed = jnp.dot(w_sp[g], vn[r0:r0 + GM_CHUNK, c0:c0 + GM_GROUP_DIM],
                            preferred_element_type=f32) + bsp_ref[g]
            cols.append(u[r0:r0 + GM_CHUNK, c0:c0 + GM_GROUP_DIM] * mixed)
        rows.append(jnp.concatenate(cols, axis=1))
    yb = jnp.concatenate(rows, axis=0).astype(bf16)
    ybp = jnp.dot(yb, wb_ref[...], preferred_element_type=f32)

    sg = jax.nn.sigmoid(jnp.dot(hb, wg_ref[...], preferred_element_type=f32))
    sga_ref[...] = sg[:, 0:d].astype(bf16)
    gb_ref[...] = (sg[:, d:] * ybp).astype(bf16)


def _input_projection(x, mod, norm1_g, w_in, b_forget, ln_v_g, w_spatial, b_spatial, w_branch_b):
    B, S, d = x.shape
    n = B * S
    tm = INPROJ_TOK_BLOCK
    bf16 = jnp.bfloat16
    p0, p1, p2, p3, p4, p5 = SPLIT_POINTS
    w_qkv = w_in[:, 0:p2].astype(bf16)
    w_f = jnp.pad(w_in[:, p2:p3], ((0, 0), (0, LANES - ATT_HEADS))).astype(bf16)
    b_f = jnp.pad(b_forget, (0, LANES - ATT_HEADS)).reshape(1, LANES)
    w_z = w_in[:, p3:p4].astype(bf16)
    w_g = w_in[:, p4:].astype(bf16)
    nt = S // tm
    tok = lambda w: pl.BlockSpec((tm, w), lambda b, i: (b * nt + i, 0))
    full = lambda a: pl.BlockSpec(a.shape, lambda b, i: (0,) * a.ndim)
    args = (x, mod, norm1_g.reshape(1, d), w_qkv, w_f, b_f, w_z, w_g, ln_v_g.reshape(1, GM_WIDTH),
            w_spatial, b_spatial.reshape(GM_GROUPS, GM_CHUNK, 1), w_branch_b.astype(bf16))
    return pl.pallas_call(
        _inproj_kernel,
        grid=(B, nt),
        in_specs=[pl.BlockSpec((1, tm, d), lambda b, i: (b, i, 0)),
                  pl.BlockSpec((1, 6, d), lambda b, i: (b, 0, 0))] + [full(a) for a in args[2:]],
        out_specs=[tok(3 * ATT_WIDTH), tok(LANES),
                   pl.BlockSpec((1, 1, ATT_HEADS, tm), lambda b, i: (b, i, 0, 0)), tok(d), tok(d)],
        out_shape=[jax.ShapeDtypeStruct((n, 3 * ATT_WIDTH), bf16),
                   jax.ShapeDtypeStruct((n, LANES), jnp.float32),
                   jax.ShapeDtypeStruct((B, nt, ATT_HEADS, tm), jnp.float32),
                   jax.ShapeDtypeStruct((n, d), bf16),
                   jax.ShapeDtypeStruct((n, d), bf16)],
        scratch_shapes=[pltpu.VMEM((1, LANES), jnp.float32)],
        compiler_params=pltpu.CompilerParams(dimension_semantics=("arbitrary", "arbitrary"),
                                             vmem_limit_bytes=56 * 1024 * 1024),
        name="input_projection",
    )(*args)


ATT_BLOCK = 1024


def _fox_kernel(q_ref, k_ref, v_ref, cum_ref, cumt_ref, o_ref, *, blk, ratio):
    f32 = jnp.float32
    hp = pl.program_id(1)
    i = pl.program_id(2)
    dh = ATT_HEAD_DIM
    q2 = q_ref[...]
    first = lax.broadcasted_iota(jnp.int32, (1, 2 * dh), 1) < dh
    qs = (jnp.where(first, q2, jnp.zeros_like(q2)), jnp.where(first, jnp.zeros_like(q2), q2))
    cum_blk = cum_ref[...]
    lane = lax.broadcasted_iota(jnp.int32, cum_blk.shape, 1)
    cqs = [jnp.sum(jnp.where(lane == 2 * hp + e, cum_blk, 0.0), axis=1, keepdims=True)
           for e in range(2)]

    def block(j, carry, masked):
        off = pl.multiple_of(j * blk, blk)
        k2 = k_ref[0, pl.ds(off, blk), :]
        v2 = v_ref[0, pl.ds(off, blk), :]
        out = []
        for e in range(2):
            m, acc = carry[e]
            ck = jnp.concatenate([cumt_ref[0, j * ratio + a, pl.ds(2 * hp + e, 1), :]
                                  for a in range(ratio)], axis=1)
            s = lax.dot_general(qs[e], k2, (((1,), (1,)), ((), ())), preferred_element_type=f32)
            s = s + (cqs[e] - ck)
            if masked:
                causal = (lax.broadcasted_iota(jnp.int32, (blk, blk), 0)
                          >= lax.broadcasted_iota(jnp.int32, (blk, blk), 1))
                s = jnp.where(causal, s, -jnp.inf)
            m_new = jnp.maximum(m, jnp.max(s, axis=1, keepdims=True))
            alpha = jnp.exp(m - m_new)
            p = jnp.exp(s - m_new).astype(v2.dtype)
            own = first if e == 0 else jnp.logical_not(first)
            v_ones = jnp.where(own, v2, jnp.ones_like(v2))
            acc = alpha * acc + jnp.dot(p, v_ones, preferred_element_type=f32)
            out.append((m_new, acc))
        return tuple(out)

    one = (jnp.full((blk, 1), -1e30, f32), jnp.zeros((blk, 2 * dh), f32))
    carry = lax.fori_loop(0, i, lambda j, c: block(j, c, False), (one, one))
    (_, acc0), (_, acc1) = block(i, carry, True)
    ratio0 = acc0 / pltpu.roll(acc0, dh, axis=1)
    ratio1 = acc1 / pltpu.roll(acc1, dh, axis=1)
    o_ref[...] = jnp.where(first, ratio0, ratio1).astype(o_ref.dtype)


def _fox_attention(qkv, cum, cumt, batch, seq_len):
    n = qkv.shape[0]
    tm = cumt.shape[-1]
    blk = max(min(ATT_BLOCK, seq_len), tm)
    nb = seq_len // blk
    pair = 2 * ATT_HEAD_DIM
    n_pairs = ATT_WIDTH // pair
    qkv3 = qkv.reshape(batch, seq_len, 3 * ATT_WIDTH)
    return pl.pallas_call(
        functools.partial(_fox_kernel, blk=blk, ratio=blk // tm),
        grid=(batch, n_pairs, nb),
        in_specs=[pl.BlockSpec((blk, pair), lambda b, hp, i: (b * nb + i, hp)),
                  pl.BlockSpec((1, seq_len, pair), lambda b, hp, i: (b, 0, n_pairs + hp)),
                  pl.BlockSpec((1, seq_len, pair), lambda b, hp, i: (b, 0, 2 * n_pairs + hp)),
                  pl.BlockSpec((blk, LANES), lambda b, hp, i: (b * nb + i, 0)),
                  pl.BlockSpec((1,) + cumt.shape[1:], lambda b, hp, i: (b, 0, 0, 0))],
        out_specs=pl.BlockSpec((blk, pair), lambda b, hp, i: (b * nb + i, hp)),
        out_shape=jax.ShapeDtypeStruct((n, ATT_WIDTH), jnp.bfloat16),
        compiler_params=pltpu.CompilerParams(
            dimension_semantics=("arbitrary", "arbitrary", "arbitrary")),
        name="fox_attention",
    )(qkv, qkv3, qkv3, cum, cumt)


def _topk_rows(scores, k):
    rows, t = scores[0].shape
    iota = lax.broadcasted_iota(jnp.int32, (rows, t), 0).astype(jnp.float32)
    slot = lax.broadcasted_iota(jnp.int32, (k, t), 0)
    scores = list(scores)
    vals = [jnp.zeros((k, t), jnp.float32) for _ in scores]
    ids = [jnp.zeros((k, t), jnp.float32) for _ in scores]
    for j in range(k):
        for i, s in enumerate(scores):
            m = jnp.max(s, axis=0, keepdims=True)
            am = jnp.min(jnp.where(s == m, iota, float(rows)), axis=0, keepdims=True)
            vals[i] = jnp.where(slot == j, m, vals[i])
            ids[i] = jnp.where(slot == j, am, ids[i])
            scores[i] = jnp.where(iota == am, -jnp.inf, s)
    return [(v, i.astype(jnp.int32)) for v, i in zip(vals, ids)]


def _select_rows(table, pos):
    out = jnp.zeros(pos.shape, table.dtype)
    for r in range(table.shape[0]):
        out = jnp.where(pos == r, table[r:r + 1, :], out)
    return out


def _post_kernel(ya_ref, sga_ref, gb_ref, x_ref, mod_ref, n2g_ref, wa_ref, wo_ref, wq_ref, keys_ref,
                 x1_ref, h2_ref, idx_ref, gates_ref, q_scr):
    f32 = jnp.float32
    a = jnp.dot(ya_ref[...], wa_ref[...], preferred_element_type=f32)
    merged = sga_ref[...].astype(f32) * a + gb_ref[...].astype(f32)
    o = jnp.dot(merged.astype(jnp.bfloat16), wo_ref[...], preferred_element_type=f32)
    g1 = mod_ref[0, 2:3, :]
    sh2 = mod_ref[0, 3:4, :]
    sc2 = mod_ref[0, 4:5, :]
    x1 = x_ref[...] + g1 * o
    x1_ref[...] = x1
    h2 = x1 * lax.rsqrt(jnp.mean(x1 * x1, axis=-1, keepdims=True) + EPS) * n2g_ref[...]
    h2 = h2 * (1.0 + sc2) + sh2
    h2_ref[...] = h2
    qp = jnp.dot(h2.astype(jnp.bfloat16), wq_ref[...], preferred_element_type=f32)
    for j in range(2 * PEER_HEADS):
        q_scr[j] = qp[:, j * PEER_HALF:(j + 1) * PEER_HALF]

    half = PEER_TOPK // 2
    tail0 = PEER_TOPK + (half - 1) * half

    def heads(hg, carry):
        hs = [hg * ROUTE_HEADS + e for e in range(ROUTE_HEADS)]
        scores = []
        for h in hs:
            for p in range(2):
                q = q_scr[2 * h + p]
                keys = keys_ref[2 * h + p]
                scores.append(lax.dot_general(keys, q, (((1,), (1,)), ((), ())),
                                              precision=lax.Precision.HIGHEST,
                                              preferred_element_type=f32))
        tops = _topk_rows(scores, PEER_TOPK)
        cands = []
        for e in range(ROUTE_HEADS):
            (s1, _), (s2, _) = tops[2 * e], tops[2 * e + 1]
            blocks = [s1[0:1, :] + s2]
            blocks += [s1[a:a + 1, :] + s2[0:half, :] for a in range(1, half)]
            blocks += [s1[half:, :] + s2[0:1, :]]
            cands.append(jnp.concatenate(blocks, axis=0))
        best = _topk_rows(cands, PEER_TOPK)
        for e, h in enumerate(hs):
            (_, i1), (_, i2) = tops[2 * e], tops[2 * e + 1]
            vals, pos = best[e]
            mid = pos - PEER_TOPK
            ra = jnp.where(pos < PEER_TOPK, 0,
                           jnp.where(pos < tail0, 1 + (mid >> (half.bit_length() - 1)), pos - tail0 + half))
            rb = jnp.where(pos < PEER_TOPK, pos, jnp.where(pos < tail0, mid & (half - 1), 0))
            eid = _select_rows(i1, ra) * N_KEYS + _select_rows(i2, rb)
            ex = jnp.exp(vals - vals[0:1, :])
            g = ex / jnp.sum(ex, axis=0, keepdims=True)
            row = pl.multiple_of(h * PEER_TOPK, PEER_TOPK)
            idx_ref[pl.ds(row, PEER_TOPK), :] = eid
            gates_ref[pl.ds(row, PEER_TOPK), :] = g
        return carry

    lax.fori_loop(0, PEER_HEADS // ROUTE_HEADS, heads, 0)


POST_TOK_BLOCK = 256
ROUTE_HEADS = 4


def _post_attention(ya, sga, gb, x, mod, norm2_g, w_a, w_out, w_query, sub_keys, seq_len, n):
    d = x.shape[1]
    tm = POST_TOK_BLOCK
    assert n % tm == 0
    blocks_per_seq = seq_len // tm
    aw = ya.shape[1]
    keys = sub_keys.reshape(2 * PEER_HEADS, N_KEYS, PEER_HALF)
    tok = lambda w: pl.BlockSpec((tm, w), lambda i: (i, 0))
    full = lambda a: pl.BlockSpec(a.shape, lambda i: (0,) * a.ndim)
    return pl.pallas_call(
        _post_kernel,
        grid=(n // tm,),
        in_specs=[tok(aw), tok(d), tok(d), tok(d),
                  pl.BlockSpec((1, 6, d), lambda i: (i // blocks_per_seq, 0, 0)),
                  full(norm2_g), full(w_a), full(w_out), full(w_query), full(keys)],
        out_specs=[tok(d), tok(d),
                   pl.BlockSpec((PEER_SLOTS, tm), lambda i: (0, i)),
                   pl.BlockSpec((PEER_SLOTS, tm), lambda i: (0, i))],
        out_shape=[jax.ShapeDtypeStruct((n, d), jnp.float32),
                   jax.ShapeDtypeStruct((n, d), jnp.float32),
                   jax.ShapeDtypeStruct((PEER_SLOTS, n), jnp.int32),
                   jax.ShapeDtypeStruct((PEER_SLOTS, n), jnp.float32)],
        scratch_shapes=[pltpu.VMEM((2 * PEER_HEADS, tm, PEER_HALF), jnp.float32)],
        compiler_params=pltpu.CompilerParams(dimension_semantics=("arbitrary",),
                                             vmem_limit_bytes=48 * 1024 * 1024),
        name="post_attention",
    )(ya, sga, gb, x, mod, norm2_g, w_a, w_out, w_query, keys)


def _rowsum_bcast(p, ones_bf16):
    hi = p.astype(jnp.bfloat16)
    lo = (p - hi.astype(jnp.float32)).astype(jnp.bfloat16)
    return (jnp.dot(hi, ones_bf16, preferred_element_type=jnp.float32)
            + jnp.dot(lo, ones_bf16, preferred_element_type=jnp.float32))


def _word_halves(w):
    return (lax.bitcast_convert_type(w & jnp.uint32(0xFFFF0000), jnp.float32),
            lax.bitcast_convert_type(w << 16, jnp.float32))


def _eval_experts(chunk, hrow, grow):
    half = hrow.shape[1] // LANES // 2
    ones_bf16 = jnp.ones((LANES, LANES), jnp.bfloat16)
    eye = (lax.broadcasted_iota(jnp.int32, (PEER_SLOTS, LANES), 0)
           == lax.broadcasted_iota(jnp.int32, (PEER_SLOTS, LANES), 1))
    hpart = lambda c: hrow[:, c * LANES:(c + 1) * LANES]
    psum = None
    for c in range(half):
        hi, lo = _word_halves(chunk(c))
        p = hi * hpart(c) + lo * hpart(c + half)
        psum = p if psum is None else psum + p
    act = _gelu(_rowsum_bcast(psum, ones_bf16))
    gcol = _rowsum_bcast(jnp.where(eye, grow, 0.0), ones_bf16)
    coef = gcol * act
    outs_hi, outs_lo = [], []
    for c in range(half):
        hi, lo = _word_halves(chunk(c + half))
        outs_hi.append(jnp.sum(coef * hi, axis=0, keepdims=True))
        outs_lo.append(jnp.sum(coef * lo, axis=0, keepdims=True))
    return jnp.concatenate(outs_hi + outs_lo, axis=-1)


def _finish_block(x_ref, g2_ref, fg_ref, peer, out_ref):
    y = x_ref[...] + g2_ref[0] * peer[...]
    out_ref[...] = y * lax.rsqrt(jnp.mean(y * y, axis=-1, keepdims=True) + EPS) * fg_ref[...]


def _expert_kernel(idx_ref, gates_ref, h_ref, x_ref, g2_ref, fg_ref, tab_ref, out_ref,
                   *scratch, tok_block, n_slots):
    bufs = scratch[:n_slots]
    peer, sem = scratch[n_slots], scratch[n_slots + 1]
    d_model = h_ref.shape[-1]
    n_chunks = d_model // LANES
    rows_per_tok = PEER_SLOTS * n_chunks

    def issue(t, s):
        for r in range(PEER_SLOTS):
            row = pl.multiple_of(idx_ref[t, r] * n_chunks, n_chunks)
            pltpu.make_async_copy(tab_ref.at[pl.ds(row, n_chunks), :],
                                  bufs[s].at[pl.ds(r * n_chunks, n_chunks), :],
                                  sem.at[s]).start(priority=r % DMA_THREADS)

    def wait(s):
        pltpu.make_async_copy(tab_ref.at[pl.ds(0, rows_per_tok), :], bufs[s], sem.at[s]).wait()

    def compute(t, s):
        chunk = lambda c: bufs[s][pl.ds(c, PEER_SLOTS, stride=n_chunks), :]
        peer[pl.ds(t, 1), :] = _eval_experts(chunk, h_ref[pl.ds(t, 1), :], gates_ref[pl.ds(t, 1), :])

    def step(t, s, prefetch):
        wait(s)
        if prefetch:
            issue(t + n_slots - 1, (s - 1) % n_slots)
        compute(t, s)

    for t in range(n_slots - 1):
        issue(t, t)

    n_groups = tok_block // n_slots

    def group(g, carry):
        for s in range(n_slots):
            step(g * n_slots + s, s, True)
        return carry

    lax.fori_loop(0, n_groups - 1, group, 0)
    for s in range(n_slots):
        t = (n_groups - 1) * n_slots + s
        step(t, s, t + n_slots - 1 < tok_block)

    _finish_block(x_ref, g2_ref, fg_ref, peer, out_ref)


def _expert_cost(n_tok, d):
    pairs = n_tok * PEER_SLOTS
    return pl.CostEstimate(flops=4 * pairs * d, transcendentals=pairs,
                           bytes_accessed=4 * pairs * d + 12 * n_tok * d + 8 * pairs)


def _sc_tanh(y):
    return 1.0 - 2.0 / (jnp.exp(2.0 * y) + 1.0)


def _sc_peer_experts(table3, ids, gates, h2):
    n_tok = ids.shape[0] // PEER_SLOTS
    _, n_chunks, lanes = table3.shape
    assert h2.shape[1] == n_chunks * lanes
    info = plsc.get_sparse_core_info()
    sl = info.num_lanes
    n_workers = info.num_cores * info.num_subcores
    tok_per_worker = n_tok // n_workers
    assert tok_per_worker * n_workers == n_tok and tok_per_worker % 2 == 0
    rows = SC_GATHER_ROWS
    n_sub = PEER_SLOTS // rows
    assert n_sub % 2 == 0
    half = n_chunks // 2
    pieces = [(c, k * sl) for c in range(half) for k in range(lanes // sl)]
    mesh = plsc.VectorSubcoreMesh(core_axis_name="c", subcore_axis_name="s")
    buf = lambda dt: pltpu.VMEM((rows, n_chunks, lanes), dt)

    @functools.partial(
        pl.kernel, mesh=mesh,
        out_type=jax.ShapeDtypeStruct((n_tok, n_chunks, lanes), jnp.float32),
        scratch_types=[pltpu.VMEM((PEER_SLOTS,), jnp.int32), pltpu.VMEM((PEER_SLOTS,), jnp.int32),
                       pltpu.VMEM((PEER_SLOTS,), jnp.float32), pltpu.VMEM((PEER_SLOTS,), jnp.float32),
                       pltpu.VMEM((n_chunks * lanes,), jnp.float32),
                       pltpu.VMEM((n_chunks * lanes,), jnp.float32),
                       buf(jnp.uint32), buf(jnp.uint32), pltpu.VMEM((n_chunks, lanes), jnp.float32),
                       pltpu.SemaphoreType.DMA((2,)), pltpu.SemaphoreType.DMA((2,))],
        compiler_params=pltpu.CompilerParams(needs_layout_passes=False),
        cost_estimate=_expert_cost(n_tok, n_chunks * lanes),
        name="sc_peer_experts")
    def run(table_hbm, ids_hbm, gates_hbm, h_hbm, peer_hbm, ids0, ids1, g0, g1, h0, h1, rows0, rows1,
            out_v, gsem, isem):
        bufs = (rows0, rows1)
        ids_p, g_p, h_p = (ids0, ids1), (g0, g1), (h0, h1)
        t_first = (lax.axis_index("s") * info.num_cores + lax.axis_index("c")) * tok_per_worker

        def token_inputs(t, p):
            off = pl.multiple_of(t * PEER_SLOTS, PEER_SLOTS)
            return (pltpu.make_async_copy(ids_hbm.at[pl.ds(off, PEER_SLOTS)], ids_p[p], isem.at[p]),
                    pltpu.make_async_copy(gates_hbm.at[pl.ds(off, PEER_SLOTS)], g_p[p], isem.at[p]),
                    pltpu.make_async_copy(h_hbm.at[t], h_p[p], isem.at[p]))

        def gather(p, s):
            return pltpu.make_async_copy(table_hbm.at[ids_p[p].at[pl.ds(s * rows, rows)]],
                                         bufs[s % 2], gsem.at[s % 2])

        for cp in token_inputs(t_first, 0):
            cp.start()
        for cp in token_inputs(t_first, 0):
            cp.wait()
        gather(0, 0).start()

        def one_token(ti, p):
            t = t_first + ti
            g_v, h_v = g_p[p], h_p[p]
            has_next = ti + 1 < tok_per_worker

            @pl.when(has_next)
            def _():
                for cp in token_inputs(t + 1, 1 - p):
                    cp.start()

            for c, k in pieces:
                out_v[c, pl.ds(k, sl)] = jnp.zeros((sl,), jnp.float32)
                out_v[c + half, pl.ds(k, sl)] = jnp.zeros((sl,), jnp.float32)
            for s in range(n_sub):
                gather(p, s).wait()
                if s + 1 < n_sub:
                    gather(p, s + 1).start()
                else:
                    @pl.when(has_next)
                    def _():
                        for cp in token_inputs(t + 1, 1 - p):
                            cp.wait()
                        gather(1 - p, 0).start()
                rbuf = bufs[s % 2]

                @pl.loop(0, rows, step=SC_ROW_GROUP)
                def _(r0):
                    accs = [jnp.zeros((sl,), jnp.float32) for _ in range(SC_ROW_GROUP)]
                    for c, k in pieces:
                        h_hi = h_v[pl.ds(c * lanes + k, sl)]
                        h_lo = h_v[pl.ds((c + half) * lanes + k, sl)]
                        for j in range(SC_ROW_GROUP):
                            hi, lo = _word_halves(rbuf[r0 + j, c, pl.ds(k, sl)])
                            accs[j] = accs[j] + hi * h_hi + lo * h_lo
                    coefs = []
                    for j in range(SC_ROW_GROUP):
                        a = jnp.broadcast_to(jnp.sum(accs[j]), (sl,))
                        act = 0.5 * a * (1.0 + _sc_tanh(0.7978845608028654 * (a + 0.044715 * (a * a * a))))
                        gate = plsc.load_gather(g_v, [jnp.broadcast_to(s * rows + r0 + j, (sl,))])
                        coefs.append(gate * act)
                    for b0 in range(0, len(pieces), SC_STORE_BATCH):
                        tots = []
                        for c, k in pieces[b0:b0 + SC_STORE_BATCH]:
                            t_hi = t_lo = None
                            for j in range(SC_ROW_GROUP):
                                hi, lo = _word_halves(rbuf[r0 + j, c + half, pl.ds(k, sl)])
                                t_hi = coefs[j] * hi if t_hi is None else t_hi + coefs[j] * hi
                                t_lo = coefs[j] * lo if t_lo is None else t_lo + coefs[j] * lo
                            tots.append((t_hi, t_lo))
                        for (c, k), (t_hi, t_lo) in zip(pieces[b0:b0 + SC_STORE_BATCH], tots):
                            plsc.addupdate(out_v.at[c, pl.ds(k, sl)], t_hi)
                            plsc.addupdate(out_v.at[c + half, pl.ds(k, sl)], t_lo)

            pltpu.sync_copy(out_v, peer_hbm.at[t])

        @pl.loop(0, tok_per_worker, step=2)
        def _(ti):
            one_token(ti, 0)
            one_token(ti + 1, 1)

    return run(table3, ids, gates, h2)


def _finish_kernel(peer_ref, x_ref, g2_ref, fg_ref, prev_ref, out_ref):
    del prev_ref
    _finish_block(x_ref, g2_ref, fg_ref, peer_ref, out_ref)


def _finish_tokens(peer, x1, g2, fg, prev, seq_len):
    d = x1.shape[1]
    tm = FINISH_TOK_BLOCK
    return pl.pallas_call(
        _finish_kernel,
        grid=(peer.shape[0] // tm,),
        in_specs=[pl.BlockSpec((tm, d), lambda i: (i, 0)),
                  pl.BlockSpec((tm, d), lambda i: (i, 0)),
                  pl.BlockSpec((1, 1, d), lambda i: (i * tm // seq_len, 0, 0)),
                  pl.BlockSpec((1, d), lambda i: (0, 0)),
                  pl.BlockSpec(memory_space=pl.ANY)],
        out_specs=pl.BlockSpec((tm, d), lambda i: (i, 0)),
        out_shape=jax.ShapeDtypeStruct(prev.shape, prev.dtype),
        input_output_aliases={4: 0},
        compiler_params=pltpu.CompilerParams(dimension_semantics=("arbitrary",)),
        name="finish_tokens",
    )(peer, x1, g2, fg, prev)


def _pack_kernel(u_ref, v_ref, o_ref):
    te, d = u_ref.shape
    n_chunks = d // LANES

    def pairs(x):
        r = lax.bitcast_convert_type(x.astype(jnp.bfloat16).astype(jnp.float32), jnp.uint32)
        return r[:, :d // 2] | (r[:, d // 2:] >> 16)

    words = jnp.concatenate([pairs(u_ref[...]), pairs(v_ref[...])], axis=1)
    for c in range(n_chunks):
        o_ref[pl.ds(c, te, stride=n_chunks), :] = words[:, c * LANES:(c + 1) * LANES]


def _pack_expert_table(expert_u, expert_v):
    n_experts, d = expert_u.shape
    te = PACK_EXPERT_BLOCK
    n_chunks = d // LANES
    packed = pl.pallas_call(
        _pack_kernel,
        grid=(n_experts // te,),
        in_specs=[pl.BlockSpec((te, d), lambda i: (i, 0)), pl.BlockSpec((te, d), lambda i: (i, 0))],
        out_specs=pl.BlockSpec((te * n_chunks, LANES), lambda i: (i, 0)),
        out_shape=jax.ShapeDtypeStruct((n_experts * n_chunks, LANES), jnp.uint32),
        name="pack_experts",
    )(expert_u, expert_v)
    return packed.reshape(n_experts, n_chunks, LANES)


def _route_experts_kernel(ya_ref, sga_ref, gb_ref, x_ref, mod_ref, n2g_ref, wa_ref, wo_ref, wq_ref, keys_ref,
                          fg_ref, tab_ref, out_ref, q_scr, x1_v, h2_v, idx_v, gates_v, idx_tv, gates_tv,
                          idx_s, isem, *expert_scratch, tok_block, n_slots):
    _post_kernel(ya_ref, sga_ref, gb_ref, x_ref, mod_ref, n2g_ref, wa_ref, wo_ref, wq_ref, keys_ref,
                 x1_v, h2_v, idx_v, gates_v, q_scr)
    idx_tv[...] = idx_v[...].T
    gates_tv[...] = gates_v[...].T
    ids_to_smem = pltpu.make_async_copy(idx_tv, idx_s, isem)
    ids_to_smem.start()
    ids_to_smem.wait()
    _expert_kernel(idx_s, gates_tv, h2_v, x1_v, mod_ref.at[:, 5:6, :], fg_ref, tab_ref, out_ref,
                   *expert_scratch, tok_block=tok_block, n_slots=n_slots)


def _tc_route_experts(ya, sga, gb, x, mod, norm2_g, w_a, w_out, w_query, sub_keys, final_g, table,
                      seq_len, first_tok):
    n, d = x.shape
    tb = EXPERT_TOK_BLOCK
    assert tb == POST_TOK_BLOCK and tb % EXPERT_SLOTS == 0 and seq_len % tb == 0 and first_tok % tb == 0
    first = first_tok // tb
    blocks_per_seq = seq_len // tb
    n_chunks = d // LANES
    aw = ya.shape[1]
    keys = sub_keys.reshape(2 * PEER_HEADS, N_KEYS, PEER_HALF)
    tok = lambda w: pl.BlockSpec((tb, w), lambda i: (i + first, 0))
    full = lambda a: pl.BlockSpec(a.shape, lambda i: (0,) * a.ndim)
    fg = final_g.reshape(1, d)
    f32, i32 = jnp.float32, jnp.int32
    return pl.pallas_call(
        functools.partial(_route_experts_kernel, tok_block=tb, n_slots=EXPERT_SLOTS),
        grid=((n - first_tok) // tb,),
        in_specs=[tok(aw), tok(d), tok(d), tok(d),
                  pl.BlockSpec((1, 6, d), lambda i: ((i + first) // blocks_per_seq, 0, 0)),
                  full(norm2_g), full(w_a), full(w_out), full(w_query), full(keys), full(fg),
                  pl.BlockSpec(memory_space=pl.ANY)],
        out_specs=tok(d),
        out_shape=jax.ShapeDtypeStruct((n, d), f32),
        scratch_shapes=(
            [pltpu.VMEM((2 * PEER_HEADS, tb, PEER_HALF), f32), pltpu.VMEM((tb, d), f32),
             pltpu.VMEM((tb, d), f32), pltpu.VMEM((PEER_SLOTS, tb), i32), pltpu.VMEM((PEER_SLOTS, tb), f32),
             pltpu.VMEM((tb, PEER_SLOTS), i32), pltpu.VMEM((tb, PEER_SLOTS), f32),
             pltpu.SMEM((tb, PEER_SLOTS), i32), pltpu.SemaphoreType.DMA]
            + [pltpu.VMEM((PEER_SLOTS * n_chunks, LANES), jnp.uint32) for _ in range(EXPERT_SLOTS)]
            + [pltpu.VMEM((tb, d), f32), pltpu.SemaphoreType.DMA((EXPERT_SLOTS,))]),
        compiler_params=pltpu.CompilerParams(dimension_semantics=("arbitrary",),
                                             vmem_limit_bytes=56 * 1024 * 1024),
        cost_estimate=_expert_cost(n - first_tok, d),
        name="route_experts",
    )(ya, sga, gb, x, mod, norm2_g, w_a, w_out, w_query, keys, fg, table.reshape(-1, LANES))


def kernel(x, c, w_mod, b_mod, norm1_g, w_in, b_forget, ln_v_g, w_spatial, b_spatial, w_branch_a, w_branch_b, w_out, norm2_g, w_query, sub_keys, expert_u, expert_v, final_g):
    B, S, D = x.shape
    n = B * S
    bf16 = jnp.bfloat16
    assert w_mod.shape[0] == 1, "the final RMSNorm is fused into the single layer's expert kernels"
    l = 0
    mod = _modulation(c, w_mod[l], b_mod[l]).reshape(B, 6, D)
    table = _pack_expert_table(expert_u[l], expert_v[l])

    qkv, cum, cumt, sga, gb = _input_projection(x, mod, norm1_g[l], w_in[l], b_forget[l], ln_v_g[l],
                                                w_spatial[l], b_spatial[l], w_branch_b[l])
    y_a = _fox_attention(qkv, cum, cumt, B, S)
    post_args = (y_a, sga, gb, x.reshape(n, D), mod, norm2_g[l].reshape(1, D), w_branch_a[l].astype(bf16),
                 w_out[l].astype(bf16), w_query[l].astype(bf16), sub_keys[l])

    n_sc = n * SC_SHARE_PERCENT // 100 // SC_SHARE_ALIGN * SC_SHARE_ALIGN
    if n_sc > 0:
        x1_a, h2_a, idx_t, gates_t = _post_attention(*post_args, S, n_sc)
        peer_sc = _sc_peer_experts(table, idx_t.T.reshape(-1), gates_t.T.reshape(-1), h2_a)
    out = _tc_route_experts(*post_args, final_g, table, S, n_sc)
    if n_sc > 0:
        out = _finish_tokens(peer_sc.reshape(n_sc, D), x1_a, mod[:, 5:6, :], final_g.reshape(1, D), out, S)
    return out.reshape(B, S, D)
```

```python
import functools

import jax
import jax.numpy as jnp
from jax import lax
from jax.experimental import pallas as pl
from jax.experimental.pallas import tpu as pltpu
from jax.experimental.pallas import tpu_sc as plsc

D_MODEL = 1024
ATT_HEADS = 8
ATT_HEAD_DIM = 64
ATT_WIDTH = ATT_HEADS * ATT_HEAD_DIM
GM_GROUPS = 4
GM_GROUP_DIM = 128
GM_WIDTH = GM_GROUPS * GM_GROUP_DIM
GM_CHUNK = 128
PEER_HEADS = 8
PEER_KEY_DIM = 256
PEER_HALF = PEER_KEY_DIM // 2
N_KEYS = 128
PEER_TOPK = 16
PEER_SLOTS = PEER_HEADS * PEER_TOPK
SPLIT_POINTS = (ATT_WIDTH, 2 * ATT_WIDTH, 3 * ATT_WIDTH, 3 * ATT_WIDTH + ATT_HEADS,
                3 * ATT_WIDTH + ATT_HEADS + 2 * GM_WIDTH,
                3 * ATT_WIDTH + ATT_HEADS + 2 * GM_WIDTH + D_MODEL)
EPS = 1e-6

LANES = 128
EXPERT_TOK_BLOCK = 256
EXPERT_SLOTS = 8
DMA_THREADS = 2
FINISH_TOK_BLOCK = 256
PACK_EXPERT_BLOCK = 256
SC_GATHER_ROWS = 32
SC_ROW_GROUP = 4
SC_STORE_BATCH = 8
SC_SHARE_PERCENT = 53
SC_SHARE_ALIGN = 256


def _gelu(x):
    return 0.5 * x * (1.0 + jnp.tanh(0.7978845608028654 * (x + 0.044715 * (x * x * x))))


def _mod_kernel(c_ref, w_ref, b_ref, o_ref):
    c = c_ref[...]
    sc = c * jax.nn.sigmoid(c)
    o_ref[...] = jnp.dot(sc, w_ref[...], precision=lax.Precision.HIGHEST,
                         preferred_element_type=jnp.float32) + b_ref[...]


def _modulation(c, w_mod, b_mod):
    b, d = c.shape
    cols = w_mod.shape[1]
    return pl.pallas_call(
        _mod_kernel,
        grid=(cols // d,),
        in_specs=[pl.BlockSpec((b, d), lambda j: (0, 0)),
                  pl.BlockSpec((d, d), lambda j: (0, j)),
                  pl.BlockSpec((1, d), lambda j: (0, j))],
        out_specs=pl.BlockSpec((b, d), lambda j: (0, j)),
        out_shape=jax.ShapeDtypeStruct((b, cols), jnp.float32),
        name="modulation",
    )(c, w_mod, b_mod.reshape(1, cols))


INPROJ_TOK_BLOCK = 256


def _inproj_kernel(x_ref, mod_ref, n1g_ref, wqkv_ref, wf_ref, bf_ref, wz_ref, wg_ref, lng_ref,
                   wsp_ref, bsp_ref, wb_ref, qkv_ref, cum_ref, cumt_ref, sga_ref, gb_ref, carry):
    f32, bf16 = jnp.float32, jnp.bfloat16
    tm, d = x_ref.shape[1], x_ref.shape[2]
    x = x_ref[0]
    sh1 = mod_ref[0, 0:1, :]
    sc1 = mod_ref[0, 1:2, :]
    h = x * lax.rsqrt(jnp.mean(x * x, axis=-1, keepdims=True) + EPS) * n1g_ref[...]
    hb = (h * (1.0 + sc1) + sh1).astype(bf16)

    qkv = jnp.dot(hb, wqkv_ref[...], preferred_element_type=f32)
    qkv_ref[:, 0:ATT_WIDTH] = (qkv[:, 0:ATT_WIDTH] * (ATT_HEAD_DIM ** -0.5)).astype(bf16)
    qkv_ref[:, ATT_WIDTH:] = qkv[:, ATT_WIDTH:].astype(bf16)

    f = jnp.dot(hb, wf_ref[...], preferred_element_type=f32) + bf_ref[...]
    logf = jnp.minimum(f, 0.0) - jnp.log1p(jnp.exp(-jnp.abs(f)))

    @pl.when(pl.program_id(1) == 0)
    def _():
        carry[...] = jnp.zeros_like(carry)

    tri = (lax.broadcasted_iota(jnp.int32, (tm, tm), 0)
           >= lax.broadcasted_iota(jnp.int32, (tm, tm), 1)).astype(f32)
    cum = jnp.dot(tri, logf, precision=lax.Precision.HIGHEST, preferred_element_type=f32) + carry[...]
    cum_ref[...] = cum
    cumt_ref[0, 0] = jnp.transpose(cum)[0:ATT_HEADS, :]
    carry[...] = cum[tm - 1:tm, :]

    gz = _gelu(jnp.dot(hb, wz_ref[...], preferred_element_type=f32))
    u = gz[:, 0:GM_WIDTH]
    v = gz[:, GM_WIDTH:]
    mu = jnp.mean(v, axis=-1, keepdims=True)
    var = jnp.mean(jnp.square(v - mu), axis=-1, keepdims=True)
    vn = ((v - mu) * lax.rsqrt(var + EPS) * lng_ref[...]).astype(bf16)
    tril = (lax.broadcasted_iota(jnp.int32, (GM_CHUNK, GM_CHUNK), 0)
            >= lax.broadcasted_iota(jnp.int32, (GM_CHUNK, GM_CHUNK), 1))
    w_sp = [jnp.where(tril, wsp_ref[g], 0.0).astype(bf16) for g in range(GM_GROUPS)]
    rows = []
    for ck in range(tm // GM_CHUNK):
        r0 = ck * GM_CHUNK
        cols = []
        for g in range(GM_GROUPS):
            c0 = g * GM_GROUP_DIM
            mixed = jnp.dot(w_sp[g], vn[r0:r0 + GM_CHUNK, c0:c0 + GM_GROUP_DIM],
                            preferred_element_type=f32) + bsp_ref[g]
            cols.append(u[r0:r0 + GM_CHUNK, c0:c0 + GM_GROUP_DIM] * mixed)
        rows.append(jnp.concatenate(cols, axis=1))
    yb = jnp.concatenate(rows, axis=0).astype(bf16)
    ybp = jnp.dot(yb, wb_ref[...], preferred_element_type=f32)

    sg = jax.nn.sigmoid(jnp.dot(hb, wg_ref[...], preferred_element_type=f32))
    sga_ref[...] = sg[:, 0:d].astype(bf16)
    gb_ref[...] = (sg[:, d:] * ybp).astype(bf16)


def _input_projection(x, mod, norm1_g, w_in, b_forget, ln_v_g, w_spatial, b_spatial, w_branch_b):
    B, S, d = x.shape
    n = B * S
    tm = INPROJ_TOK_BLOCK
    bf16 = jnp.bfloat16
    p0, p1, p2, p3, p4, p5 = SPLIT_POINTS
    w_qkv = w_in[:, 0:p2].astype(bf16)
    w_f = jnp.pad(w_in[:, p2:p3], ((0, 0), (0, LANES - ATT_HEADS))).astype(bf16)
    b_f = jnp.pad(b_forget, (0, LANES - ATT_HEADS)).reshape(1, LANES)
    w_z = w_in[:, p3:p4].astype(bf16)
    w_g = w_in[:, p4:].astype(bf16)
    nt = S // tm
    tok = lambda w: pl.BlockSpec((tm, w), lambda b, i: (b * nt + i, 0))
    full = lambda a: pl.BlockSpec(a.shape, lambda b, i: (0,) * a.ndim)
    args = (x, mod, norm1_g.reshape(1, d), w_qkv, w_f, b_f, w_z, w_g, ln_v_g.reshape(1, GM_WIDTH),
            w_spatial, b_spatial.reshape(GM_GROUPS, GM_CHUNK, 1), w_branch_b.astype(bf16))
    return pl.pallas_call(
        _inproj_kernel,
        grid=(B, nt),
        in_specs=[pl.BlockSpec((1, tm, d), lambda b, i: (b, i, 0)),
                  pl.BlockSpec((1, 6, d), lambda b, i: (b, 0, 0))] + [full(a) for a in args[2:]],
        out_specs=[tok(3 * ATT_WIDTH), tok(LANES),
                   pl.BlockSpec((1, 1, ATT_HEADS, tm), lambda b, i: (b, i, 0, 0)), tok(d), tok(d)],
        out_shape=[jax.ShapeDtypeStruct((n, 3 * ATT_WIDTH), bf16),
                   jax.ShapeDtypeStruct((n, LANES), jnp.float32),
                   jax.ShapeDtypeStruct((B, nt, ATT_HEADS, tm), jnp.float32),
                   jax.ShapeDtypeStruct((n, d), bf16),
                   jax.ShapeDtypeStruct((n, d), bf16)],
        scratch_shapes=[pltpu.VMEM((1, LANES), jnp.float32)],
        compiler_params=pltpu.CompilerParams(dimension_semantics=("arbitrary", "arbitrary"),
                                             vmem_limit_bytes=56 * 1024 * 1024),
        name="input_projection",
    )(*args)


ATT_BLOCK = 1024


def _fox_kernel(q_ref, k_ref, v_ref, cum_ref, cumt_ref, o_ref, *, blk, ratio):
    f32 = jnp.float32
    hp = pl.program_id(1)
    i = pl.program_id(2)
    dh = ATT_HEAD_DIM
    q2 = q_ref[...]
    first = lax.broadcasted_iota(jnp.int32, (1, 2 * dh), 1) < dh
    qs = (jnp.where(first, q2, jnp.zeros_like(q2)), jnp.where(first, jnp.zeros_like(q2), q2))
    cum_blk = cum_ref[...]
    lane = lax.broadcasted_iota(jnp.int32, cum_blk.shape, 1)
    cqs = [jnp.sum(jnp.where(lane == 2 * hp + e, cum_blk, 0.0), axis=1, keepdims=True)
           for e in range(2)]

    def block(j, carry, masked):
        off = pl.multiple_of(j * blk, blk)
        k2 = k_ref[0, pl.ds(off, blk), :]
        v2 = v_ref[0, pl.ds(off, blk), :]
        out = []
        for e in range(2):
            m, acc = carry[e]
            ck = jnp.concatenate([cumt_ref[0, j * ratio + a, pl.ds(2 * hp + e, 1), :]
                                  for a in range(ratio)], axis=1)
            s = lax.dot_general(qs[e], k2, (((1,), (1,)), ((), ())), preferred_element_type=f32)
            s = s + (cqs[e] - ck)
            if masked:
                causal = (lax.broadcasted_iota(jnp.int32, (blk, blk), 0)
                          >= lax.broadcasted_iota(jnp.int32, (blk, blk), 1))
                s = jnp.where(causal, s, -jnp.inf)
            m_new = jnp.maximum(m, jnp.max(s, axis=1, keepdims=True))
            alpha = jnp.exp(m - m_new)
            p = jnp.exp(s - m_new).astype(v2.dtype)
            own = first if e == 0 else jnp.logical_not(first)
            v_ones = jnp.where(own, v2, jnp.ones_like(v2))
            acc = alpha * acc + jnp.dot(p, v_ones, preferred_element_type=f32)
            out.append((m_new, acc))
        return tuple(out)

    one = (jnp.full((blk, 1), -1e30, f32), jnp.zeros((blk, 2 * dh), f32))
    carry = lax.fori_loop(0, i, lambda j, c: block(j, c, False), (one, one))
    (_, acc0), (_, acc1) = block(i, carry, True)
    ratio0 = acc0 / pltpu.roll(acc0, dh, axis=1)
    ratio1 = acc1 / pltpu.roll(acc1, dh, axis=1)
    o_ref[...] = jnp.where(first, ratio0, ratio1).astype(o_ref.dtype)


def _fox_attention(qkv, cum, cumt, batch, seq_len):
    n = qkv.shape[0]
    tm = cumt.shape[-1]
    blk = max(min(ATT_BLOCK, seq_len), tm)
    nb = seq_len // blk
    pair = 2 * ATT_HEAD_DIM
    n_pairs = ATT_WIDTH // pair
    qkv3 = qkv.reshape(batch, seq_len, 3 * ATT_WIDTH)
    return pl.pallas_call(
        functools.partial(_fox_kernel, blk=blk, ratio=blk // tm),
        grid=(batch, n_pairs, nb),
        in_specs=[pl.BlockSpec((blk, pair), lambda b, hp, i: (b * nb + i, hp)),
                  pl.BlockSpec((1, seq_len, pair), lambda b, hp, i: (b, 0, n_pairs + hp)),
                  pl.BlockSpec((1, seq_len, pair), lambda b, hp, i: (b, 0, 2 * n_pairs + hp)),
                  pl.BlockSpec((blk, LANES), lambda b, hp, i: (b * nb + i, 0)),
                  pl.BlockSpec((1,) + cumt.shape[1:], lambda b, hp, i: (b, 0, 0, 0))],
        out_specs=pl.BlockSpec((blk, pair), lambda b, hp, i: (b * nb + i, hp)),
        out_shape=jax.ShapeDtypeStruct((n, ATT_WIDTH), jnp.bfloat16),
        compiler_params=pltpu.CompilerParams(
            dimension_semantics=("arbitrary", "arbitrary", "arbitrary")),
        name="fox_attention",
    )(qkv, qkv3, qkv3, cum, cumt)


def _topk_rows(scores, k):
    rows, t = scores[0].shape
    iota = lax.broadcasted_iota(jnp.int32, (rows, t), 0).astype(jnp.float32)
    slot = lax.broadcasted_iota(jnp.int32, (k, t), 0)
    scores = list(scores)
    vals = [jnp.zeros((k, t), jnp.float32) for _ in scores]
    ids = [jnp.zeros((k, t), jnp.float32) for _ in scores]
    for j in range(k):
        for i, s in enumerate(scores):
            m = jnp.max(s, axis=0, keepdims=True)
            am = jnp.min(jnp.where(s == m, iota, float(rows)), axis=0, keepdims=True)
            vals[i] = jnp.where(slot == j, m, vals[i])
            ids[i] = jnp.where(slot == j, am, ids[i])
            scores[i] = jnp.where(iota == am, -jnp.inf, s)
    return [(v, i.astype(jnp.int32)) for v, i in zip(vals, ids)]


def _select_rows(table, pos):
    out = jnp.zeros(pos.shape, table.dtype)
    for r in range(table.shape[0]):
        out = jnp.where(pos == r, table[r:r + 1, :], out)
    return out


def _post_kernel(ya_ref, sga_ref, gb_ref, x_ref, mod_ref, n2g_ref, wa_ref, wo_ref, wq_ref, keys_ref,
                 x1_ref, h2_ref, idx_ref, gates_ref, q_scr):
    f32 = jnp.float32
    a = jnp.dot(ya_ref[...], wa_ref[...], preferred_element_type=f32)
    merged = sga_ref[...].astype(f32) * a + gb_ref[...].astype(f32)
    o = jnp.dot(merged.astype(jnp.bfloat16), wo_ref[...], preferred_element_type=f32)
    g1 = mod_ref[0, 2:3, :]
    sh2 = mod_ref[0, 3:4, :]
    sc2 = mod_ref[0, 4:5, :]
    x1 = x_ref[...] + g1 * o
    x1_ref[...] = x1
    h2 = x1 * lax.rsqrt(jnp.mean(x1 * x1, axis=-1, keepdims=True) + EPS) * n2g_ref[...]
    h2 = h2 * (1.0 + sc2) + sh2
    h2_ref[...] = h2
    qp = jnp.dot(h2.astype(jnp.bfloat16), wq_ref[...], preferred_element_type=f32)
    for j in range(2 * PEER_HEADS):
        q_scr[j] = qp[:, j * PEER_HALF:(j + 1) * PEER_HALF]

    half = PEER_TOPK // 2
    tail0 = PEER_TOPK + (half - 1) * half

    def heads(hg, carry):
        hs = [hg * ROUTE_HEADS + e for e in range(ROUTE_HEADS)]
        scores = []
        for h in hs:
            for p in range(2):
                q = q_scr[2 * h + p]
                keys = keys_ref[2 * h + p]
                scores.append(lax.dot_general(keys, q, (((1,), (1,)), ((), ())),
                                              precision=lax.Precision.HIGHEST,
                                              preferred_element_type=f32))
        tops = _topk_rows(scores, PEER_TOPK)
        cands = []
        for e in range(ROUTE_HEADS):
            (s1, _), (s2, _) = tops[2 * e], tops[2 * e + 1]
            blocks = [s1[0:1, :] + s2]
            blocks += [s1[a:a + 1, :] + s2[0:half, :] for a in range(1, half)]
            blocks += [s1[half:, :] + s2[0:1, :]]
            cands.append(jnp.concatenate(blocks, axis=0))
        best = _topk_rows(cands, PEER_TOPK)
        for e, h in enumerate(hs):
            (_, i1), (_, i2) = tops[2 * e], tops[2 * e + 1]
            vals, pos = best[e]
            mid = pos - PEER_TOPK
            ra = jnp.where(pos < PEER_TOPK, 0,
                           jnp.where(pos < tail0, 1 + (mid >> (half.bit_length() - 1)), pos - tail0 + half))
            rb = jnp.where(pos < PEER_TOPK, pos, jnp.where(pos < tail0, mid & (half - 1), 0))
            eid = _select_rows(i1, ra) * N_KEYS + _select_rows(i2, rb)
            ex = jnp.exp(vals - vals[0:1, :])
            g = ex / jnp.sum(ex, axis=0, keepdims=True)
            row = pl.multiple_of(h * PEER_TOPK, PEER_TOPK)
            idx_ref[pl.ds(row, PEER_TOPK), :] = eid
            gates_ref[pl.ds(row, PEER_TOPK), :] = g
        return carry

    lax.fori_loop(0, PEER_HEADS // ROUTE_HEADS, heads, 0)


POST_TOK_BLOCK = 256
ROUTE_HEADS = 4


def _post_attention(ya, sga, gb, x, mod, norm2_g, w_a, w_out, w_query, sub_keys, seq_len, n):
    d = x.shape[1]
    tm = POST_TOK_BLOCK
    assert n % tm == 0
    blocks_per_seq = seq_len // tm
    aw = ya.shape[1]
    keys = sub_keys.reshape(2 * PEER_HEADS, N_KEYS, PEER_HALF)
    tok = lambda w: pl.BlockSpec((tm, w), lambda i: (i, 0))
    full = lambda a: pl.BlockSpec(a.shape, lambda i: (0,) * a.ndim)
    return pl.pallas_call(
        _post_kernel,
        grid=(n // tm,),
        in_specs=[tok(aw), tok(d), tok(d), tok(d),
                  pl.BlockSpec((1, 6, d), lambda i: (i // blocks_per_seq, 0, 0)),
                  full(norm2_g), full(w_a), full(w_out), full(w_query), full(keys)],
        out_specs=[tok(d), tok(d),
                   pl.BlockSpec((PEER_SLOTS, tm), lambda i: (0, i)),
                   pl.BlockSpec((PEER_SLOTS, tm), lambda i: (0, i))],
        out_shape=[jax.ShapeDtypeStruct((n, d), jnp.float32),
                   jax.ShapeDtypeStruct((n, d), jnp.float32),
                   jax.ShapeDtypeStruct((PEER_SLOTS, n), jnp.int32),
                   jax.ShapeDtypeStruct((PEER_SLOTS, n), jnp.float32)],
        scratch_shapes=[pltpu.VMEM((2 * PEER_HEADS, tm, PEER_HALF), jnp.float32)],
        compiler_params=pltpu.CompilerParams(dimension_semantics=("arbitrary",),
                                             vmem_limit_bytes=48 * 1024 * 1024),
        name="post_attention",
    )(ya, sga, gb, x, mod, norm2_g, w_a, w_out, w_query, keys)


def _rowsum_bcast(p, ones_bf16):
    hi = p.astype(jnp.bfloat16)
    lo = (p - hi.astype(jnp.float32)).astype(jnp.bfloat16)
    return (jnp.dot(hi, ones_bf16, preferred_element_type=jnp.float32)
            + jnp.dot(lo, ones_bf16, preferred_element_type=jnp.float32))


def _word_halves(w):
    return (lax.bitcast_convert_type(w & jnp.uint32(0xFFFF0000), jnp.float32),
            lax.bitcast_convert_type(w << 16, jnp.float32))


def _eval_experts(chunk, hrow, grow):
    half = hrow.shape[1] // LANES // 2
    ones_bf16 = jnp.ones((LANES, LANES), jnp.bfloat16)
    eye = (lax.broadcasted_iota(jnp.int32, (PEER_SLOTS, LANES), 0)
           == lax.broadcasted_iota(jnp.int32, (PEER_SLOTS, LANES), 1))
    hpart = lambda c: hrow[:, c * LANES:(c + 1) * LANES]
    psum = None
    for c in range(half):
        hi, lo = _word_halves(chunk(c))
        p = hi * hpart(c) + lo * hpart(c + half)
        psum = p if psum is None else psum + p
    act = _gelu(_rowsum_bcast(psum, ones_bf16))
    gcol = _rowsum_bcast(jnp.where(eye, grow, 0.0), ones_bf16)
    coef = gcol * act
    outs_hi, outs_lo = [], []
    for c in range(half):
        hi, lo = _word_halves(chunk(c + half))
        outs_hi.append(jnp.sum(coef * hi, axis=0, keepdims=True))
        outs_lo.append(jnp.sum(coef * lo, axis=0, keepdims=True))
    return jnp.concatenate(outs_hi + outs_lo, axis=-1)


def _finish_block(x_ref, g2_ref, fg_ref, peer, out_ref):
    y = x_ref[...] + g2_ref[0] * peer[...]
    out_ref[...] = y * lax.rsqrt(jnp.mean(y * y, axis=-1, keepdims=True) + EPS) * fg_ref[...]


def _expert_kernel(idx_ref, gates_ref, h_ref, x_ref, g2_ref, fg_ref, tab_ref, out_ref,
                   *scratch, tok_block, n_slots):
    bufs = scratch[:n_slots]
    peer, sem = scratch[n_slots], scratch[n_slots + 1]
    d_model = h_ref.shape[-1]
    n_chunks = d_model // LANES
    rows_per_tok = PEER_SLOTS * n_chunks

    def issue(t, s):
        for r in range(PEER_SLOTS):
            row = pl.multiple_of(idx_ref[t, r] * n_chunks, n_chunks)
            pltpu.make_async_copy(tab_ref.at[pl.ds(row, n_chunks), :],
                                  bufs[s].at[pl.ds(r * n_chunks, n_chunks), :],
                                  sem.at[s]).start(priority=r % DMA_THREADS)

    def wait(s):
        pltpu.make_async_copy(tab_ref.at[pl.ds(0, rows_per_tok), :], bufs[s], sem.at[s]).wait()

    def compute(t, s):
        chunk = lambda c: bufs[s][pl.ds(c, PEER_SLOTS, stride=n_chunks), :]
        peer[pl.ds(t, 1), :] = _eval_experts(chunk, h_ref[pl.ds(t, 1), :], gates_ref[pl.ds(t, 1), :])

    def step(t, s, prefetch):
        wait(s)
        if prefetch:
            issue(t + n_slots - 1, (s - 1) % n_slots)
        compute(t, s)

    for t in range(n_slots - 1):
        issue(t, t)

    n_groups = tok_block // n_slots

    def group(g, carry):
        for s in range(n_slots):
            step(g * n_slots + s, s, True)
        return carry

    lax.fori_loop(0, n_groups - 1, group, 0)
    for s in range(n_slots):
        t = (n_groups - 1) * n_slots + s
        step(t, s, t + n_slots - 1 < tok_block)

    _finish_block(x_ref, g2_ref, fg_ref, peer, out_ref)


def _expert_cost(n_tok, d):
    pairs = n_tok * PEER_SLOTS
    return pl.CostEstimate(flops=4 * pairs * d, transcendentals=pairs,
                           bytes_accessed=4 * pairs * d + 12 * n_tok * d + 8 * pairs)


def _sc_tanh(y):
    return 1.0 - 2.0 / (jnp.exp(2.0 * y) + 1.0)


def _sc_peer_experts(table3, ids, gates, h2):
    n_tok = ids.shape[0] // PEER_SLOTS
    _, n_chunks, lanes = table3.shape
    assert h2.shape[1] == n_chunks * lanes
    info = plsc.get_sparse_core_info()
    sl = info.num_lanes
    n_workers = info.num_cores * info.num_subcores
    tok_per_worker = n_tok // n_workers
    assert tok_per_worker * n_workers == n_tok and tok_per_worker % 2 == 0
    rows = SC_GATHER_ROWS
    n_sub = PEER_SLOTS // rows
    assert n_sub % 2 == 0
    half = n_chunks // 2
    pieces = [(c, k * sl) for c in range(half) for k in range(lanes // sl)]
    mesh = plsc.VectorSubcoreMesh(core_axis_name="c", subcore_axis_name="s")
    buf = lambda dt: pltpu.VMEM((rows, n_chunks, lanes), dt)

    @functools.partial(
        pl.kernel, mesh=mesh,
        out_type=jax.ShapeDtypeStruct((n_tok, n_chunks, lanes), jnp.float32),
        scratch_types=[pltpu.VMEM((PEER_SLOTS,), jnp.int32), pltpu.VMEM((PEER_SLOTS,), jnp.int32),
                       pltpu.VMEM((PEER_SLOTS,), jnp.float32), pltpu.VMEM((PEER_SLOTS,), jnp.float32),
                       pltpu.VMEM((n_chunks * lanes,), jnp.float32),
                       pltpu.VMEM((n_chunks * lanes,), jnp.float32),
                       buf(jnp.uint32), buf(jnp.uint32), pltpu.VMEM((n_chunks, lanes), jnp.float32),
                       pltpu.SemaphoreType.DMA((2,)), pltpu.SemaphoreType.DMA((2,))],
        compiler_params=pltpu.CompilerParams(needs_layout_passes=False),
        cost_estimate=_expert_cost(n_tok, n_chunks * lanes),
        name="sc_peer_experts")
    def run(table_hbm, ids_hbm, gates_hbm, h_hbm, peer_hbm, ids0, ids1, g0, g1, h0, h1, rows0, rows1,
            out_v, gsem, isem):
        bufs = (rows0, rows1)
        ids_p, g_p, h_p = (ids0, ids1), (g0, g1), (h0, h1)
        t_first = (lax.axis_index("s") * info.num_cores + lax.axis_index("c")) * tok_per_worker

        def token_inputs(t, p):
            off = pl.multiple_of(t * PEER_SLOTS, PEER_SLOTS)
            return (pltpu.make_async_copy(ids_hbm.at[pl.ds(off, PEER_SLOTS)], ids_p[p], isem.at[p]),
                    pltpu.make_async_copy(gates_hbm.at[pl.ds(off, PEER_SLOTS)], g_p[p], isem.at[p]),
                    pltpu.make_async_copy(h_hbm.at[t], h_p[p], isem.at[p]))

        def gather(p, s):
            return pltpu.make_async_copy(table_hbm.at[ids_p[p].at[pl.ds(s * rows, rows)]],
                                         bufs[s % 2], gsem.at[s % 2])

        for cp in token_inputs(t_first, 0):
            cp.start()
        for cp in token_inputs(t_first, 0):
            cp.wait()
        gather(0, 0).start()

        def one_token(ti, p):
            t = t_first + ti
            g_v, h_v = g_p[p], h_p[p]
            has_next = ti + 1 < tok_per_worker

            @pl.when(has_next)
            def _():
                for cp in token_inputs(t + 1, 1 - p):
                    cp.start()

            for c, k in pieces:
                out_v[c, pl.ds(k, sl)] = jnp.zeros((sl,), jnp.float32)
                out_v[c + half, pl.ds(k, sl)] = jnp.zeros((sl,), jnp.float32)
            for s in range(n_sub):
                gather(p, s).wait()
                if s + 1 < n_sub:
                    gather(p, s + 1).start()
                else:
                    @pl.when(has_next)
                    def _():
                        for cp in token_inputs(t + 1, 1 - p):
                            cp.wait()
                        gather(1 - p, 0).start()
                rbuf = bufs[s % 2]

                @plsc.parallel_loop(0, rows, step=SC_ROW_GROUP)
                def _(r0):
                    accs = [jnp.zeros((sl,), jnp.float32) for _ in range(SC_ROW_GROUP)]
                    for c, k in pieces:
                        h_hi = h_v[pl.ds(c * lanes + k, sl)]
                        h_lo = h_v[pl.ds((c + half) * lanes + k, sl)]
                        for j in range(SC_ROW_GROUP):
                            hi, lo = _word_halves(rbuf[r0 + j, c, pl.ds(k, sl)])
                            accs[j] = accs[j] + hi * h_hi + lo * h_lo
                    coefs = []
                    for j in range(SC_ROW_GROUP):
                        a = jnp.broadcast_to(jnp.sum(accs[j]), (sl,))
                        act = 0.5 * a * (1.0 + _sc_tanh(0.7978845608028654 * (a + 0.044715 * (a * a * a))))
                        gate = plsc.load_gather(g_v, [jnp.broadcast_to(s * rows + r0 + j, (sl,))])
                        coefs.append(gate * act)
                    for b0 in range(0, len(pieces), SC_STORE_BATCH):
                        tots = []
                        for c, k in pieces[b0:b0 + SC_STORE_BATCH]:
                            t_hi = t_lo = None
                            for j in range(SC_ROW_GROUP):
                                hi, lo = _word_halves(rbuf[r0 + j, c + half, pl.ds(k, sl)])
                                t_hi = coefs[j] * hi if t_hi is None else t_hi + coefs[j] * hi
                                t_lo = coefs[j] * lo if t_lo is None else t_lo + coefs[j] * lo
                            tots.append((t_hi, t_lo))
                        for (c, k), (t_hi, t_lo) in zip(pieces[b0:b0 + SC_STORE_BATCH], tots):
                            plsc.addupdate(out_v.at[c, pl.ds(k, sl)], t_hi)
                            plsc.addupdate(out_v.at[c + half, pl.ds(k, sl)], t_lo)

            pltpu.sync_copy(out_v, peer_hbm.at[t])

        @pl.loop(0, tok_per_worker, step=2)
        def _(ti):
            one_token(ti, 0)
            one_token(ti + 1, 1)

    return run(table3, ids, gates, h2)


def _finish_kernel(peer_ref, x_ref, g2_ref, fg_ref, prev_ref, out_ref):
    del prev_ref
    _finish_block(x_ref, g2_ref, fg_ref, peer_ref, out_ref)


def _finish_tokens(peer, x1, g2, fg, prev, seq_len):
    d = x1.shape[1]
    tm = FINISH_TOK_BLOCK
    return pl.pallas_call(
        _finish_kernel,
        grid=(peer.shape[0] // tm,),
        in_specs=[pl.BlockSpec((tm, d), lambda i: (i, 0)),
                  pl.BlockSpec((tm, d), lambda i: (i, 0)),
                  pl.BlockSpec((1, 1, d), lambda i: (i * tm // seq_len, 0, 0)),
                  pl.BlockSpec((1, d), lambda i: (0, 0)),
                  pl.BlockSpec(memory_space=pl.ANY)],
        out_specs=pl.BlockSpec((tm, d), lambda i: (i, 0)),
        out_shape=jax.ShapeDtypeStruct(prev.shape, prev.dtype),
        input_output_aliases={4: 0},
        compiler_params=pltpu.CompilerParams(dimension_semantics=("arbitrary",)),
        name="finish_tokens",
    )(peer, x1, g2, fg, prev)


def _pack_kernel(u_ref, v_ref, o_ref):
    te, d = u_ref.shape
    n_chunks = d // LANES

    def pairs(x):
        r = lax.bitcast_convert_type(x.astype(jnp.bfloat16).astype(jnp.float32), jnp.uint32)
        return r[:, :d // 2] | (r[:, d // 2:] >> 16)

    words = jnp.concatenate([pairs(u_ref[...]), pairs(v_ref[...])], axis=1)
    for c in range(n_chunks):
        o_ref[pl.ds(c, te, stride=n_chunks), :] = words[:, c * LANES:(c + 1) * LANES]


def _pack_expert_table(expert_u, expert_v):
    n_experts, d = expert_u.shape
    te = PACK_EXPERT_BLOCK
    n_chunks = d // LANES
    packed = pl.pallas_call(
        _pack_kernel,
        grid=(n_experts // te,),
        in_specs=[pl.BlockSpec((te, d), lambda i: (i, 0)), pl.BlockSpec((te, d), lambda i: (i, 0))],
        out_specs=pl.BlockSpec((te * n_chunks, LANES), lambda i: (i, 0)),
        out_shape=jax.ShapeDtypeStruct((n_experts * n_chunks, LANES), jnp.uint32),
        name="pack_experts",
    )(expert_u, expert_v)
    return packed.reshape(n_experts, n_chunks, LANES)


def _route_experts_kernel(ya_ref, sga_ref, gb_ref, x_ref, mod_ref, n2g_ref, wa_ref, wo_ref, wq_ref, keys_ref,
                          fg_ref, tab_ref, out_ref, q_scr, x1_v, h2_v, idx_v, gates_v, idx_tv, gates_tv,
                          idx_s, isem, *expert_scratch, tok_block, n_slots):
    _post_kernel(ya_ref, sga_ref, gb_ref, x_ref, mod_ref, n2g_ref, wa_ref, wo_ref, wq_ref, keys_ref,
                 x1_v, h2_v, idx_v, gates_v, q_scr)
    idx_tv[...] = idx_v[...].T
    gates_tv[...] = gates_v[...].T
    ids_to_smem = pltpu.make_async_copy(idx_tv, idx_s, isem)
    ids_to_smem.start()
    ids_to_smem.wait()
    _expert_kernel(idx_s, gates_tv, h2_v, x1_v, mod_ref.at[:, 5:6, :], fg_ref, tab_ref, out_ref,
                   *expert_scratch, tok_block=tok_block, n_slots=n_slots)


def _tc_route_experts(ya, sga, gb, x, mod, norm2_g, w_a, w_out, w_query, sub_keys, final_g, table,
                      seq_len, first_tok):
    n, d = x.shape
    tb = EXPERT_TOK_BLOCK
    assert tb == POST_TOK_BLOCK and tb % EXPERT_SLOTS == 0 and seq_len % tb == 0 and first_tok % tb == 0
    first = first_tok // tb
    blocks_per_seq = seq_len // tb
    n_chunks = d // LANES
    aw = ya.shape[1]
    keys = sub_keys.reshape(2 * PEER_HEADS, N_KEYS, PEER_HALF)
    tok = lambda w: pl.BlockSpec((tb, w), lambda i: (i + first, 0))
    full = lambda a: pl.BlockSpec(a.shape, lambda i: (0,) * a.ndim)
    fg = final_g.reshape(1, d)
    f32, i32 = jnp.float32, jnp.int32
    return pl.pallas_call(
        functools.partial(_route_experts_kernel, tok_block=tb, n_slots=EXPERT_SLOTS),
        grid=((n - first_tok) // tb,),
        in_specs=[tok(aw), tok(d), tok(d), tok(d),
                  pl.BlockSpec((1, 6, d), lambda i: ((i + first) // blocks_per_seq, 0, 0)),
                  full(norm2_g), full(w_a), full(w_out), full(w_query), full(keys), full(fg),
                  pl.BlockSpec(memory_space=pl.ANY)],
        out_specs=tok(d),
        out_shape=jax.ShapeDtypeStruct((n, d), f32),
        scratch_shapes=(
            [pltpu.VMEM((2 * PEER_HEADS, tb, PEER_HALF), f32), pltpu.VMEM((tb, d), f32),
             pltpu.VMEM((tb, d), f32), pltpu.VMEM((PEER_SLOTS, tb), i32), pltpu.VMEM((PEER_SLOTS, tb), f32),
             pltpu.VMEM((tb, PEER_SLOTS), i32), pltpu.VMEM((tb, PEER_SLOTS), f32),
             pltpu.SMEM((tb, PEER_SLOTS), i32), pltpu.SemaphoreType.DMA]
            + [pltpu.VMEM((PEER_SLOTS * n_chunks, LANES), jnp.uint32) for _ in range(EXPERT_SLOTS)]
            + [pltpu.VMEM((tb, d), f32), pltpu.SemaphoreType.DMA((EXPERT_SLOTS,))]),
        compiler_params=pltpu.CompilerParams(dimension_semantics=("arbitrary",),
                                             vmem_limit_bytes=56 * 1024 * 1024),
        cost_estimate=_expert_cost(n - first_tok, d),
        name="route_experts",
    )(ya, sga, gb, x, mod, norm2_g, w_a, w_out, w_query, keys, fg, table.reshape(-1, LANES))


def kernel(x, c, w_mod, b_mod, norm1_g, w_in, b_forget, ln_v_g, w_spatial, b_spatial, w_branch_a, w_branch_b, w_out, norm2_g, w_query, sub_keys, expert_u, expert_v, final_g):
    B, S, D = x.shape
    n = B * S
    bf16 = jnp.bfloat16
    assert w_mod.shape[0] == 1, "the final RMSNorm is fused into the single layer's expert kernels"
    l = 0
    mod = _modulation(c, w_mod[l], b_mod[l]).reshape(B, 6, D)
    table = _pack_expert_table(expert_u[l], expert_v[l])

    qkv, cum, cumt, sga, gb = _input_projection(x, mod, norm1_g[l], w_in[l], b_forget[l], ln_v_g[l],
                                                w_spatial[l], b_spatial[l], w_branch_b[l])
    y_a = _fox_attention(qkv, cum, cumt, B, S)
    post_args = (y_a, sga, gb, x.reshape(n, D), mod, norm2_g[l].reshape(1, D), w_branch_a[l].astype(bf16),
                 w_out[l].astype(bf16), w_query[l].astype(bf16), sub_keys[l])

    n_sc = n * SC_SHARE_PERCENT // 100 // SC_SHARE_ALIGN * SC_SHARE_ALIGN
    if n_sc > 0:
        x1_a, h2_a, idx_t, gates_t = _post_attention(*post_args, S, n_sc)
        peer_sc = _sc_peer_experts(table, idx_t.T.reshape(-1), gates_t.T.reshape(-1), h2_a)
    out = _tc_route_experts(*post_args, final_g, table, S, n_sc)
    if n_sc > 0:
        out = _finish_tokens(peer_sc.reshape(n_sc, D), x1_a, mod[:, 5:6, :], final_g.reshape(1, D), out, S)
    return out.reshape(B, S, D)
```

```python
import functools

import jax
import jax.numpy as jnp
from jax import lax
from jax.experimental import pallas as pl
from jax.experimental.pallas import tpu as pltpu
from jax.experimental.pallas import tpu_sc as plsc

D_MODEL = 1024
ATT_HEADS = 8
ATT_HEAD_DIM = 64
ATT_WIDTH = ATT_HEADS * ATT_HEAD_DIM
GM_GROUPS = 4
GM_GROUP_DIM = 128
GM_WIDTH = GM_GROUPS * GM_GROUP_DIM
GM_CHUNK = 128
PEER_HEADS = 8
PEER_KEY_DIM = 256
PEER_HALF = PEER_KEY_DIM // 2
N_KEYS = 128
PEER_TOPK = 16
PEER_SLOTS = PEER_HEADS * PEER_TOPK
SPLIT_POINTS = (ATT_WIDTH, 2 * ATT_WIDTH, 3 * ATT_WIDTH, 3 * ATT_WIDTH + ATT_HEADS,
                3 * ATT_WIDTH + ATT_HEADS + 2 * GM_WIDTH,
                3 * ATT_WIDTH + ATT_HEADS + 2 * GM_WIDTH + D_MODEL)
EPS = 1e-6

LANES = 128
EXPERT_TOK_BLOCK = 256
EXPERT_SLOTS = 16
DMA_THREADS = 2
FINISH_TOK_BLOCK = 256
PACK_EXPERT_BLOCK = 256
SC_GATHER_ROWS = 32
SC_ROW_GROUP = 4
SC_STORE_BATCH = 8
SC_SHARE_PERCENT = 53
SC_SHARE_ALIGN = 256


def _gelu(x):
    return 0.5 * x * (1.0 + jnp.tanh(0.7978845608028654 * (x + 0.044715 * (x * x * x))))


def _mod_kernel(c_ref, w_ref, b_ref, o_ref):
    c = c_ref[...]
    sc = c * jax.nn.sigmoid(c)
    o_ref[...] = jnp.dot(sc, w_ref[...], precision=lax.Precision.HIGHEST,
                         preferred_element_type=jnp.float32) + b_ref[...]


def _modulation(c, w_mod, b_mod):
    b, d = c.shape
    cols = w_mod.shape[1]
    return pl.pallas_call(
        _mod_kernel,
        grid=(cols // d,),
        in_specs=[pl.BlockSpec((b, d), lambda j: (0, 0)),
                  pl.BlockSpec((d, d), lambda j: (0, j)),
                  pl.BlockSpec((1, d), lambda j: (0, j))],
        out_specs=pl.BlockSpec((b, d), lambda j: (0, j)),
        out_shape=jax.ShapeDtypeStruct((b, cols), jnp.float32),
        name="modulation",
    )(c, w_mod, b_mod.reshape(1, cols))


INPROJ_TOK_BLOCK = 256


def _inproj_kernel(x_ref, mod_ref, n1g_ref, wqkv_ref, wf_ref, bf_ref, wz_ref, wg_ref, lng_ref,
                   wsp_ref, bsp_ref, wb_ref, qkv_ref, cum_ref, cumt_ref, sga_ref, gb_ref, carry):
    f32, bf16 = jnp.float32, jnp.bfloat16
    tm, d = x_ref.shape[1], x_ref.shape[2]
    x = x_ref[0]
    sh1 = mod_ref[0, 0:1, :]
    sc1 = mod_ref[0, 1:2, :]
    h = x * lax.rsqrt(jnp.mean(x * x, axis=-1, keepdims=True) + EPS) * n1g_ref[...]
    hb = (h * (1.0 + sc1) + sh1).astype(bf16)

    qkv = jnp.dot(hb, wqkv_ref[...], preferred_element_type=f32)
    qkv_ref[:, 0:ATT_WIDTH] = (qkv[:, 0:ATT_WIDTH] * (ATT_HEAD_DIM ** -0.5)).astype(bf16)
    qkv_ref[:, ATT_WIDTH:] = qkv[:, ATT_WIDTH:].astype(bf16)

    f = jnp.dot(hb, wf_ref[...], preferred_element_type=f32) + bf_ref[...]
    logf = jnp.minimum(f, 0.0) - jnp.log1p(jnp.exp(-jnp.abs(f)))

    @pl.when(pl.program_id(1) == 0)
    def _():
        carry[...] = jnp.zeros_like(carry)

    tri = (lax.broadcasted_iota(jnp.int32, (tm, tm), 0)
           >= lax.broadcasted_iota(jnp.int32, (tm, tm), 1)).astype(f32)
    cum = jnp.dot(tri, logf, precision=lax.Precision.HIGHEST, preferred_element_type=f32) + carry[...]
    cum_ref[...] = cum
    cumt_ref[0, 0] = jnp.transpose(cum)[0:ATT_HEADS, :]
    carry[...] = cum[tm - 1:tm, :]

    gz = _gelu(jnp.dot(hb, wz_ref[...], preferred_element_type=f32))
    u = gz[:, 0:GM_WIDTH]
    v = gz[:, GM_WIDTH:]
    mu = jnp.mean(v, axis=-1, keepdims=True)
    var = jnp.mean(jnp.square(v - mu), axis=-1, keepdims=True)
    vn = ((v - mu) * lax.rsqrt(var + EPS) * lng_ref[...]).astype(bf16)
    tril = (lax.broadcasted_iota(jnp.int32, (GM_CHUNK, GM_CHUNK), 0)
            >= lax.broadcasted_iota(jnp.int32, (GM_CHUNK, GM_CHUNK), 1))
    w_sp = [jnp.where(tril, wsp_ref[g], 0.0).astype(bf16) for g in range(GM_GROUPS)]
    rows = []
    for ck in range(tm // GM_CHUNK):
        r0 = ck * GM_CHUNK
        cols = []
        for g in range(GM_GROUPS):
            c0 = g * GM_GROUP_DIM
            mixed = jnp.dot(w_sp[g], vn[r0:r0 + GM_CHUNK, c0:c0 + GM_GROUP_DIM],
                            preferred_element_type=f32) + bsp_ref[g]
            cols.append(u[r0:r0 + GM_CHUNK, c0:c0 + GM_GROUP_DIM] * mixed)
        rows.append(jnp.concatenate(cols, axis=1))
    yb = jnp.concatenate(rows, axis=0).astype(bf16)
    ybp = jnp.dot(yb, wb_ref[...], preferred_element_type=f32)

    sg = jax.nn.sigmoid(jnp.dot(hb, wg_ref[...], preferred_element_type=f32))
    sga_ref[...] = sg[:, 0:d].astype(bf16)
    gb_ref[...] = (sg[:, d:] * ybp).astype(bf16)


def _input_projection(x, mod, norm1_g, w_in, b_forget, ln_v_g, w_spatial, b_spatial, w_branch_b):
    B, S, d = x.shape
    n = B * S
    tm = INPROJ_TOK_BLOCK
    bf16 = jnp.bfloat16
    p0, p1, p2, p3, p4, p5 = SPLIT_POINTS
    w_qkv = w_in[:, 0:p2].astype(bf16)
    w_f = jnp.pad(w_in[:, p2:p3], ((0, 0), (0, LANES - ATT_HEADS))).astype(bf16)
    b_f = jnp.pad(b_forget, (0, LANES - ATT_HEADS)).reshape(1, LANES)
    w_z = w_in[:, p3:p4].astype(bf16)
    w_g = w_in[:, p4:].astype(bf16)
    nt = S // tm
    tok = lambda w: pl.BlockSpec((tm, w), lambda b, i: (b * nt + i, 0))
    full = lambda a: pl.BlockSpec(a.shape, lambda b, i: (0,) * a.ndim)
    args = (x, mod, norm1_g.reshape(1, d), w_qkv, w_f, b_f, w_z, w_g, ln_v_g.reshape(1, GM_WIDTH),
            w_spatial, b_spatial.reshape(GM_GROUPS, GM_CHUNK, 1), w_branch_b.astype(bf16))
    return pl.pallas_call(
        _inproj_kernel,
        grid=(B, nt),
        in_specs=[pl.BlockSpec((1, tm, d), lambda b, i: (b, i, 0)),
                  pl.BlockSpec((1, 6, d), lambda b, i: (b, 0, 0))] + [full(a) for a in args[2:]],
        out_specs=[tok(3 * ATT_WIDTH), tok(LANES),
                   pl.BlockSpec((1, 1, ATT_HEADS, tm), lambda b, i: (b, i, 0, 0)), tok(d), tok(d)],
        out_shape=[jax.ShapeDtypeStruct((n, 3 * ATT_WIDTH), bf16),
                   jax.ShapeDtypeStruct((n, LANES), jnp.float32),
                   jax.ShapeDtypeStruct((B, nt, ATT_HEADS, tm), jnp.float32),
                   jax.ShapeDtypeStruct((n, d), bf16),
                   jax.ShapeDtypeStruct((n, d), bf16)],
        scratch_shapes=[pltpu.VMEM((1, LANES), jnp.float32)],
        compiler_params=pltpu.CompilerParams(dimension_semantics=("arbitrary", "arbitrary"),
                                             vmem_limit_bytes=56 * 1024 * 1024),
        name="input_projection",
    )(*args)


ATT_BLOCK = 1024


def _fox_kernel(q_ref, k_ref, v_ref, cum_ref, cumt_ref, o_ref, *, blk, ratio):
    f32 = jnp.float32
    hp = pl.program_id(1)
    i = pl.program_id(2)
    dh = ATT_HEAD_DIM
    q2 = q_ref[...]
    first = lax.broadcasted_iota(jnp.int32, (1, 2 * dh), 1) < dh
    qs = (jnp.where(first, q2, jnp.zeros_like(q2)), jnp.where(first, jnp.zeros_like(q2), q2))
    cum_blk = cum_ref[...]
    lane = lax.broadcasted_iota(jnp.int32, cum_blk.shape, 1)
    cqs = [jnp.sum(jnp.where(lane == 2 * hp + e, cum_blk, 0.0), axis=1, keepdims=True)
           for e in range(2)]

    def block(j, carry, masked):
        off = pl.multiple_of(j * blk, blk)
        k2 = k_ref[0, pl.ds(off, blk), :]
        v2 = v_ref[0, pl.ds(off, blk), :]
        out = []
        for e in range(2):
            m, acc = carry[e]
            ck = jnp.concatenate([cumt_ref[0, j * ratio + a, pl.ds(2 * hp + e, 1), :]
                                  for a in range(ratio)], axis=1)
            s = lax.dot_general(qs[e], k2, (((1,), (1,)), ((), ())), preferred_element_type=f32)
            s = s + (cqs[e] - ck)
            if masked:
                causal = (lax.broadcasted_iota(jnp.int32, (blk, blk), 0)
                          >= lax.broadcasted_iota(jnp.int32, (blk, blk), 1))
                s = jnp.where(causal, s, -jnp.inf)
            m_new = jnp.maximum(m, jnp.max(s, axis=1, keepdims=True))
            alpha = jnp.exp(m - m_new)
            p = jnp.exp(s - m_new).astype(v2.dtype)
            own = first if e == 0 else jnp.logical_not(first)
            v_ones = jnp.where(own, v2, jnp.ones_like(v2))
            acc = alpha * acc + jnp.dot(p, v_ones, preferred_element_type=f32)
            out.append((m_new, acc))
        return tuple(out)

    one = (jnp.full((blk, 1), -1e30, f32), jnp.zeros((blk, 2 * dh), f32))
    carry = lax.fori_loop(0, i, lambda j, c: block(j, c, False), (one, one))
    (_, acc0), (_, acc1) = block(i, carry, True)
    ratio0 = acc0 / pltpu.roll(acc0, dh, axis=1)
    ratio1 = acc1 / pltpu.roll(acc1, dh, axis=1)
    o_ref[...] = jnp.where(first, ratio0, ratio1).astype(o_ref.dtype)


def _fox_attention(qkv, cum, cumt, batch, seq_len):
    n = qkv.shape[0]
    tm = cumt.shape[-1]
    blk = max(min(ATT_BLOCK, seq_len), tm)
    nb = seq_len // blk
    pair = 2 * ATT_HEAD_DIM
    n_pairs = ATT_WIDTH // pair
    qkv3 = qkv.reshape(batch, seq_len, 3 * ATT_WIDTH)
    return pl.pallas_call(
        functools.partial(_fox_kernel, blk=blk, ratio=blk // tm),
        grid=(batch, n_pairs, nb),
        in_specs=[pl.BlockSpec((blk, pair), lambda b, hp, i: (b * nb + i, hp)),
                  pl.BlockSpec((1, seq_len, pair), lambda b, hp, i: (b, 0, n_pairs + hp)),
                  pl.BlockSpec((1, seq_len, pair), lambda b, hp, i: (b, 0, 2 * n_pairs + hp)),
                  pl.BlockSpec((blk, LANES), lambda b, hp, i: (b * nb + i, 0)),
                  pl.BlockSpec((1,) + cumt.shape[1:], lambda b, hp, i: (b, 0, 0, 0))],
        out_specs=pl.BlockSpec((blk, pair), lambda b, hp, i: (b * nb + i, hp)),
        out_shape=jax.ShapeDtypeStruct((n, ATT_WIDTH), jnp.bfloat16),
        compiler_params=pltpu.CompilerParams(
            dimension_semantics=("arbitrary", "arbitrary", "arbitrary")),
        name="fox_attention",
    )(qkv, qkv3, qkv3, cum, cumt)


def _topk_rows(scores, k):
    rows, t = scores[0].shape
    iota = lax.broadcasted_iota(jnp.int32, (rows, t), 0).astype(jnp.float32)
    slot = lax.broadcasted_iota(jnp.int32, (k, t), 0)
    scores = list(scores)
    vals = [jnp.zeros((k, t), jnp.float32) for _ in scores]
    ids = [jnp.zeros((k, t), jnp.float32) for _ in scores]
    for j in range(k):
        for i, s in enumerate(scores):
            m = jnp.max(s, axis=0, keepdims=True)
            am = jnp.min(jnp.where(s == m, iota, float(rows)), axis=0, keepdims=True)
            vals[i] = jnp.where(slot == j, m, vals[i])
            ids[i] = jnp.where(slot == j, am, ids[i])
            scores[i] = jnp.where(iota == am, -jnp.inf, s)
    return [(v, i.astype(jnp.int32)) for v, i in zip(vals, ids)]


def _select_rows(table, pos):
    out = jnp.zeros(pos.shape, table.dtype)
    for r in range(table.shape[0]):
        out = jnp.where(pos == r, table[r:r + 1, :], out)
    return out


def _post_kernel(ya_ref, sga_ref, gb_ref, x_ref, mod_ref, n2g_ref, wa_ref, wo_ref, wq_ref, keys_ref,
                 x1_ref, h2_ref, idx_ref, gates_ref, q_scr):
    f32 = jnp.float32
    a = jnp.dot(ya_ref[...], wa_ref[...], preferred_element_type=f32)
    merged = sga_ref[...].astype(f32) * a + gb_ref[...].astype(f32)
    o = jnp.dot(merged.astype(jnp.bfloat16), wo_ref[...], preferred_element_type=f32)
    g1 = mod_ref[0, 2:3, :]
    sh2 = mod_ref[0, 3:4, :]
    sc2 = mod_ref[0, 4:5, :]
    x1 = x_ref[...] + g1 * o
    x1_ref[...] = x1
    h2 = x1 * lax.rsqrt(jnp.mean(x1 * x1, axis=-1, keepdims=True) + EPS) * n2g_ref[...]
    h2 = h2 * (1.0 + sc2) + sh2
    h2_ref[...] = h2
    qp = jnp.dot(h2.astype(jnp.bfloat16), wq_ref[...], preferred_element_type=f32)
    for j in range(2 * PEER_HEADS):
        q_scr[j] = qp[:, j * PEER_HALF:(j + 1) * PEER_HALF]

    half = PEER_TOPK // 2
    tail0 = PEER_TOPK + (half - 1) * half

    def heads(hg, carry):
        hs = [hg * ROUTE_HEADS + e for e in range(ROUTE_HEADS)]
        scores = []
        for h in hs:
            for p in range(2):
                q = q_scr[2 * h + p]
                keys = keys_ref[2 * h + p]
                scores.append(lax.dot_general(keys, q, (((1,), (1,)), ((), ())),
                                              precision=lax.Precision.HIGHEST,
                                              preferred_element_type=f32))
        tops = _topk_rows(scores, PEER_TOPK)
        cands = []
        for e in range(ROUTE_HEADS):
            (s1, _), (s2, _) = tops[2 * e], tops[2 * e + 1]
            blocks = [s1[0:1, :] + s2]
            blocks += [s1[a:a + 1, :] + s2[0:half, :] for a in range(1, half)]
            blocks += [s1[half:, :] + s2[0:1, :]]
            cands.append(jnp.concatenate(blocks, axis=0))
        best = _topk_rows(cands, PEER_TOPK)
        for e, h in enumerate(hs):
            (_, i1), (_, i2) = tops[2 * e], tops[2 * e + 1]
            vals, pos = best[e]
            mid = pos - PEER_TOPK
            ra = jnp.where(pos < PEER_TOPK, 0,
                           jnp.where(pos < tail0, 1 + (mid >> (half.bit_length() - 1)), pos - tail0 + half))
            rb = jnp.where(pos < PEER_TOPK, pos, jnp.where(pos < tail0, mid & (half - 1), 0))
            eid = _select_rows(i1, ra) * N_KEYS + _select_rows(i2, rb)
            ex = jnp.exp(vals - vals[0:1, :])
            g = ex / jnp.sum(ex, axis=0, keepdims=True)
            row = pl.multiple_of(h * PEER_TOPK, PEER_TOPK)
            idx_ref[pl.ds(row, PEER_TOPK), :] = eid
            gates_ref[pl.ds(row, PEER_TOPK), :] = g
        return carry

    lax.fori_loop(0, PEER_HEADS // ROUTE_HEADS, heads, 0)


POST_TOK_BLOCK = 256
ROUTE_HEADS = 4


def _post_attention(ya, sga, gb, x, mod, norm2_g, w_a, w_out, w_query, sub_keys, seq_len, n):
    d = x.shape[1]
    tm = POST_TOK_BLOCK
    assert n % tm == 0
    blocks_per_seq = seq_len // tm
    aw = ya.shape[1]
    keys = sub_keys.reshape(2 * PEER_HEADS, N_KEYS, PEER_HALF)
    tok = lambda w: pl.BlockSpec((tm, w), lambda i: (i, 0))
    full = lambda a: pl.BlockSpec(a.shape, lambda i: (0,) * a.ndim)
    return pl.pallas_call(
        _post_kernel,
        grid=(n // tm,),
        in_specs=[tok(aw), tok(d), tok(d), tok(d),
                  pl.BlockSpec((1, 6, d), lambda i: (i // blocks_per_seq, 0, 0)),
                  full(norm2_g), full(w_a), full(w_out), full(w_query), full(keys)],
        out_specs=[tok(d), tok(d),
                   pl.BlockSpec((PEER_SLOTS, tm), lambda i: (0, i)),
                   pl.BlockSpec((PEER_SLOTS, tm), lambda i: (0, i))],
        out_shape=[jax.ShapeDtypeStruct((n, d), jnp.float32),
                   jax.ShapeDtypeStruct((n, d), jnp.float32),
                   jax.ShapeDtypeStruct((PEER_SLOTS, n), jnp.int32),
                   jax.ShapeDtypeStruct((PEER_SLOTS, n), jnp.float32)],
        scratch_shapes=[pltpu.VMEM((2 * PEER_HEADS, tm, PEER_HALF), jnp.float32)],
        compiler_params=pltpu.CompilerParams(dimension_semantics=("arbitrary",),
                                             vmem_limit_bytes=48 * 1024 * 1024),
        name="post_attention",
    )(ya, sga, gb, x, mod, norm2_g, w_a, w_out, w_query, keys)


def _rowsum_bcast(p, ones_bf16):
    hi = p.astype(jnp.bfloat16)
    lo = (p - hi.astype(jnp.float32)).astype(jnp.bfloat16)
    return (jnp.dot(hi, ones_bf16, preferred_element_type=jnp.float32)
            + jnp.dot(lo, ones_bf16, preferred_element_type=jnp.float32))


def _word_halves(w):
    return (lax.bitcast_convert_type(w & jnp.uint32(0xFFFF0000), jnp.float32),
            lax.bitcast_convert_type(w << 16, jnp.float32))


def _eval_experts(chunk, hrow, grow):
    half = hrow.shape[1] // LANES // 2
    ones_bf16 = jnp.ones((LANES, LANES), jnp.bfloat16)
    eye = (lax.broadcasted_iota(jnp.int32, (PEER_SLOTS, LANES), 0)
           == lax.broadcasted_iota(jnp.int32, (PEER_SLOTS, LANES), 1))
    hpart = lambda c: hrow[:, c * LANES:(c + 1) * LANES]
    psum = None
    for c in range(half):
        hi, lo = _word_halves(chunk(c))
        p = hi * hpart(c) + lo * hpart(c + half)
        psum = p if psum is None else psum + p
    act = _gelu(_rowsum_bcast(psum, ones_bf16))
    gcol = _rowsum_bcast(jnp.where(eye, grow, 0.0), ones_bf16)
    coef = gcol * act
    outs_hi, outs_lo = [], []
    for c in range(half):
        hi, lo = _word_halves(chunk(c + half))
        outs_hi.append(jnp.sum(coef * hi, axis=0, keepdims=True))
        outs_lo.append(jnp.sum(coef * lo, axis=0, keepdims=True))
    return jnp.concatenate(outs_hi + outs_lo, axis=-1)


def _finish_block(x_ref, g2_ref, fg_ref, peer, out_ref):
    y = x_ref[...] + g2_ref[0] * peer[...]
    out_ref[...] = y * lax.rsqrt(jnp.mean(y * y, axis=-1, keepdims=True) + EPS) * fg_ref[...]


def _expert_kernel(idx_ref, gates_ref, h_ref, x_ref, g2_ref, fg_ref, tab_ref, out_ref,
                   *scratch, tok_block, n_slots):
    bufs = scratch[:n_slots]
    peer, sem = scratch[n_slots], scratch[n_slots + 1]
    d_model = h_ref.shape[-1]
    n_chunks = d_model // LANES
    rows_per_tok = PEER_SLOTS * n_chunks

    def issue(t, s):
        for r in range(PEER_SLOTS):
            row = pl.multiple_of(idx_ref[t, r] * n_chunks, n_chunks)
            pltpu.make_async_copy(tab_ref.at[pl.ds(row, n_chunks), :],
                                  bufs[s].at[pl.ds(r * n_chunks, n_chunks), :],
                                  sem.at[s]).start(priority=r % DMA_THREADS)

    def wait(s):
        pltpu.make_async_copy(tab_ref.at[pl.ds(0, rows_per_tok), :], bufs[s], sem.at[s]).wait()

    def compute(t, s):
        chunk = lambda c: bufs[s][pl.ds(c, PEER_SLOTS, stride=n_chunks), :]
        peer[pl.ds(t, 1), :] = _eval_experts(chunk, h_ref[pl.ds(t, 1), :], gates_ref[pl.ds(t, 1), :])

    def step(t, s, prefetch):
        wait(s)
        if prefetch:
            issue(t + n_slots - 1, (s - 1) % n_slots)
        compute(t, s)

    for t in range(n_slots - 1):
        issue(t, t)

    n_groups = tok_block // n_slots

    def group(g, carry):
        for s in range(n_slots):
            step(g * n_slots + s, s, True)
        return carry

    lax.fori_loop(0, n_groups - 1, group, 0)
    for s in range(n_slots):
        t = (n_groups - 1) * n_slots + s
        step(t, s, t + n_slots - 1 < tok_block)

    _finish_block(x_ref, g2_ref, fg_ref, peer, out_ref)


def _expert_cost(n_tok, d):
    pairs = n_tok * PEER_SLOTS
    return pl.CostEstimate(flops=4 * pairs * d, transcendentals=pairs,
                           bytes_accessed=4 * pairs * d + 12 * n_tok * d + 8 * pairs)


def _sc_tanh(y):
    return 1.0 - 2.0 / (jnp.exp(2.0 * y) + 1.0)


def _sc_peer_experts(table3, ids, gates, h2):
    n_tok = ids.shape[0] // PEER_SLOTS
    _, n_chunks, lanes = table3.shape
    assert h2.shape[1] == n_chunks * lanes
    info = plsc.get_sparse_core_info()
    sl = info.num_lanes
    n_workers = info.num_cores * info.num_subcores
    tok_per_worker = n_tok // n_workers
    assert tok_per_worker * n_workers == n_tok and tok_per_worker % 2 == 0
    rows = SC_GATHER_ROWS
    n_sub = PEER_SLOTS // rows
    assert n_sub % 2 == 0
    half = n_chunks // 2
    pieces = [(c, k * sl) for c in range(half) for k in range(lanes // sl)]
    mesh = plsc.VectorSubcoreMesh(core_axis_name="c", subcore_axis_name="s")
    buf = lambda dt: pltpu.VMEM((rows, n_chunks, lanes), dt)

    @functools.partial(
        pl.kernel, mesh=mesh,
        out_type=jax.ShapeDtypeStruct((n_tok, n_chunks, lanes), jnp.float32),
        scratch_types=[pltpu.VMEM((PEER_SLOTS,), jnp.int32), pltpu.VMEM((PEER_SLOTS,), jnp.int32),
                       pltpu.VMEM((PEER_SLOTS,), jnp.float32), pltpu.VMEM((PEER_SLOTS,), jnp.float32),
                       pltpu.VMEM((n_chunks * lanes,), jnp.float32),
                       pltpu.VMEM((n_chunks * lanes,), jnp.float32),
                       buf(jnp.uint32), buf(jnp.uint32), pltpu.VMEM((n_chunks, lanes), jnp.float32),
                       pltpu.SemaphoreType.DMA((2,)), pltpu.SemaphoreType.DMA((2,))],
        compiler_params=pltpu.CompilerParams(needs_layout_passes=False),
        cost_estimate=_expert_cost(n_tok, n_chunks * lanes),
        name="sc_peer_experts")
    def run(table_hbm, ids_hbm, gates_hbm, h_hbm, peer_hbm, ids0, ids1, g0, g1, h0, h1, rows0, rows1,
            out_v, gsem, isem):
        bufs = (rows0, rows1)
        ids_p, g_p, h_p = (ids0, ids1), (g0, g1), (h0, h1)
        t_first = (lax.axis_index("s") * info.num_cores + lax.axis_index("c")) * tok_per_worker

        def token_inputs(t, p):
            off = pl.multiple_of(t * PEER_SLOTS, PEER_SLOTS)
            return (pltpu.make_async_copy(ids_hbm.at[pl.ds(off, PEER_SLOTS)], ids_p[p], isem.at[p]),
                    pltpu.make_async_copy(gates_hbm.at[pl.ds(off, PEER_SLOTS)], g_p[p], isem.at[p]),
                    pltpu.make_async_copy(h_hbm.at[t], h_p[p], isem.at[p]))

        def gather(p, s):
            return pltpu.make_async_copy(table_hbm.at[ids_p[p].at[pl.ds(s * rows, rows)]],
                                         bufs[s % 2], gsem.at[s % 2])

        for cp in token_inputs(t_first, 0):
            cp.start()
        for cp in token_inputs(t_first, 0):
            cp.wait()
        gather(0, 0).start()

        def one_token(ti, p):
            t = t_first + ti
            g_v, h_v = g_p[p], h_p[p]
            has_next = ti + 1 < tok_per_worker

            @pl.when(has_next)
            def _():
                for cp in token_inputs(t + 1, 1 - p):
                    cp.start()

            for c, k in pieces:
                out_v[c, pl.ds(k, sl)] = jnp.zeros((sl,), jnp.float32)
                out_v[c + half, pl.ds(k, sl)] = jnp.zeros((sl,), jnp.float32)
            for s in range(n_sub):
                gather(p, s).wait()
                if s + 1 < n_sub:
                    gather(p, s + 1).start()
                else:
                    @pl.when(has_next)
                    def _():
                        for cp in token_inputs(t + 1, 1 - p):
                            cp.wait()
                        gather(1 - p, 0).start()
                rbuf = bufs[s % 2]

                @pl.loop(0, rows, step=SC_ROW_GROUP)
                def _(r0):
                    accs = [jnp.zeros((sl,), jnp.float32) for _ in range(SC_ROW_GROUP)]
                    for c, k in pieces:
                        h_hi = h_v[pl.ds(c * lanes + k, sl)]
                        h_lo = h_v[pl.ds((c + half) * lanes + k, sl)]
                        for j in range(SC_ROW_GROUP):
                            hi, lo = _word_halves(rbuf[r0 + j, c, pl.ds(k, sl)])
                            accs[j] = accs[j] + hi * h_hi + lo * h_lo
                    coefs = []
                    for j in range(SC_ROW_GROUP):
                        a = jnp.broadcast_to(jnp.sum(accs[j]), (sl,))
                        act = 0.5 * a * (1.0 + _sc_tanh(0.7978845608028654 * (a + 0.044715 * (a * a * a))))
                        gate = plsc.load_gather(g_v, [jnp.broadcast_to(s * rows + r0 + j, (sl,))])
                        coefs.append(gate * act)
                    for b0 in range(0, len(pieces), SC_STORE_BATCH):
                        tots = []
                        for c, k in pieces[b0:b0 + SC_STORE_BATCH]:
                            t_hi = t_lo = None
                            for j in range(SC_ROW_GROUP):
                                hi, lo = _word_halves(rbuf[r0 + j, c + half, pl.ds(k, sl)])
                                t_hi = coefs[j] * hi if t_hi is None else t_hi + coefs[j] * hi
                                t_lo = coefs[j] * lo if t_lo is None else t_lo + coefs[j] * lo
                            tots.append((t_hi, t_lo))
                        for (c, k), (t_hi, t_lo) in zip(pieces[b0:b0 + SC_STORE_BATCH], tots):
                            plsc.addupdate(out_v.at[c, pl.ds(k, sl)], t_hi)
                            plsc.addupdate(out_v.at[c + half, pl.ds(k, sl)], t_lo)

            pltpu.sync_copy(out_v, peer_hbm.at[t])

        @pl.loop(0, tok_per_worker, step=2)
        def _(ti):
            one_token(ti, 0)
            one_token(ti + 1, 1)

    return run(table3, ids, gates, h2)


def _finish_kernel(peer_ref, x_ref, g2_ref, fg_ref, prev_ref, out_ref):
    del prev_ref
    _finish_block(x_ref, g2_ref, fg_ref, peer_ref, out_ref)


def _finish_tokens(peer, x1, g2, fg, prev, seq_len):
    d = x1.shape[1]
    tm = FINISH_TOK_BLOCK
    return pl.pallas_call(
        _finish_kernel,
        grid=(peer.shape[0] // tm,),
        in_specs=[pl.BlockSpec((tm, d), lambda i: (i, 0)),
                  pl.BlockSpec((tm, d), lambda i: (i, 0)),
                  pl.BlockSpec((1, 1, d), lambda i: (i * tm // seq_len, 0, 0)),
                  pl.BlockSpec((1, d), lambda i: (0, 0)),
                  pl.BlockSpec(memory_space=pl.ANY)],
        out_specs=pl.BlockSpec((tm, d), lambda i: (i, 0)),
        out_shape=jax.ShapeDtypeStruct(prev.shape, prev.dtype),
        input_output_aliases={4: 0},
        compiler_params=pltpu.CompilerParams(dimension_semantics=("arbitrary",)),
        name="finish_tokens",
    )(peer, x1, g2, fg, prev)


def _pack_kernel(u_ref, v_ref, o_ref):
    te, d = u_ref.shape
    n_chunks = d // LANES

    def pairs(x):
        r = lax.bitcast_convert_type(x.astype(jnp.bfloat16).astype(jnp.float32), jnp.uint32)
        return r[:, :d // 2] | (r[:, d // 2:] >> 16)

    words = jnp.concatenate([pairs(u_ref[...]), pairs(v_ref[...])], axis=1)
    for c in range(n_chunks):
        o_ref[pl.ds(c, te, stride=n_chunks), :] = words[:, c * LANES:(c + 1) * LANES]


def _pack_expert_table(expert_u, expert_v):
    n_experts, d = expert_u.shape
    te = PACK_EXPERT_BLOCK
    n_chunks = d // LANES
    packed = pl.pallas_call(
        _pack_kernel,
        grid=(n_experts // te,),
        in_specs=[pl.BlockSpec((te, d), lambda i: (i, 0)), pl.BlockSpec((te, d), lambda i: (i, 0))],
        out_specs=pl.BlockSpec((te * n_chunks, LANES), lambda i: (i, 0)),
        out_shape=jax.ShapeDtypeStruct((n_experts * n_chunks, LANES), jnp.uint32),
        name="pack_experts",
    )(expert_u, expert_v)
    return packed.reshape(n_experts, n_chunks, LANES)


def _route_experts_kernel(ya_ref, sga_ref, gb_ref, x_ref, mod_ref, n2g_ref, wa_ref, wo_ref, wq_ref, keys_ref,
                          fg_ref, tab_ref, out_ref, q_scr, x1_v, h2_v, idx_v, gates_v, idx_tv, gates_tv,
                          idx_s, isem, *expert_scratch, tok_block, n_slots):
    _post_kernel(ya_ref, sga_ref, gb_ref, x_ref, mod_ref, n2g_ref, wa_ref, wo_ref, wq_ref, keys_ref,
                 x1_v, h2_v, idx_v, gates_v, q_scr)
    idx_tv[...] = idx_v[...].T
    gates_tv[...] = gates_v[...].T
    ids_to_smem = pltpu.make_async_copy(idx_tv, idx_s, isem)
    ids_to_smem.start()
    ids_to_smem.wait()
    _expert_kernel(idx_s, gates_tv, h2_v, x1_v, mod_ref.at[:, 5:6, :], fg_ref, tab_ref, out_ref,
                   *expert_scratch, tok_block=tok_block, n_slots=n_slots)


def _tc_route_experts(ya, sga, gb, x, mod, norm2_g, w_a, w_out, w_query, sub_keys, final_g, table,
                      seq_len, first_tok):
    n, d = x.shape
    tb = EXPERT_TOK_BLOCK
    assert tb == POST_TOK_BLOCK and tb % EXPERT_SLOTS == 0 and seq_len % tb == 0 and first_tok % tb == 0
    first = first_tok // tb
    blocks_per_seq = seq_len // tb
    n_chunks = d // LANES
    aw = ya.shape[1]
    keys = sub_keys.reshape(2 * PEER_HEADS, N_KEYS, PEER_HALF)
    tok = lambda w: pl.BlockSpec((tb, w), lambda i: (i + first, 0))
    full = lambda a: pl.BlockSpec(a.shape, lambda i: (0,) * a.ndim)
    fg = final_g.reshape(1, d)
    f32, i32 = jnp.float32, jnp.int32
    return pl.pallas_call(
        functools.partial(_route_experts_kernel, tok_block=tb, n_slots=EXPERT_SLOTS),
        grid=((n - first_tok) // tb,),
        in_specs=[tok(aw), tok(d), tok(d), tok(d),
                  pl.BlockSpec((1, 6, d), lambda i: ((i + first) // blocks_per_seq, 0, 0)),
                  full(norm2_g), full(w_a), full(w_out), full(w_query), full(keys), full(fg),
                  pl.BlockSpec(memory_space=pl.ANY)],
        out_specs=tok(d),
        out_shape=jax.ShapeDtypeStruct((n, d), f32),
        scratch_shapes=(
            [pltpu.VMEM((2 * PEER_HEADS, tb, PEER_HALF), f32), pltpu.VMEM((tb, d), f32),
             pltpu.VMEM((tb, d), f32), pltpu.VMEM((PEER_SLOTS, tb), i32), pltpu.VMEM((PEER_SLOTS, tb), f32),
             pltpu.VMEM((tb, PEER_SLOTS), i32), pltpu.VMEM((tb, PEER_SLOTS), f32),
             pltpu.SMEM((tb, PEER_SLOTS), i32), pltpu.SemaphoreType.DMA]
            + [pltpu.VMEM((PEER_SLOTS * n_chunks, LANES), jnp.uint32) for _ in range(EXPERT_SLOTS)]
            + [pltpu.VMEM((tb, d), f32), pltpu.SemaphoreType.DMA((EXPERT_SLOTS,))]),
        compiler_params=pltpu.CompilerParams(dimension_semantics=("arbitrary",),
                                             vmem_limit_bytes=56 * 1024 * 1024),
        cost_estimate=_expert_cost(n - first_tok, d),
        name="route_experts",
    )(ya, sga, gb, x, mod, norm2_g, w_a, w_out, w_query, keys, fg, table.reshape(-1, LANES))


def kernel(x, c, w_mod, b_mod, norm1_g, w_in, b_forget, ln_v_g, w_spatial, b_spatial, w_branch_a, w_branch_b, w_out, norm2_g, w_query, sub_keys, expert_u, expert_v, final_g):
    B, S, D = x.shape
    n = B * S
    bf16 = jnp.bfloat16
    assert w_mod.shape[0] == 1, "the final RMSNorm is fused into the single layer's expert kernels"
    l = 0
    mod = _modulation(c, w_mod[l], b_mod[l]).reshape(B, 6, D)
    table = _pack_expert_table(expert_u[l], expert_v[l])

    qkv, cum, cumt, sga, gb = _input_projection(x, mod, norm1_g[l], w_in[l], b_forget[l], ln_v_g[l],
                                                w_spatial[l], b_spatial[l], w_branch_b[l])
    y_a = _fox_attention(qkv, cum, cumt, B, S)
    post_args = (y_a, sga, gb, x.reshape(n, D), mod, norm2_g[l].reshape(1, D), w_branch_a[l].astype(bf16),
                 w_out[l].astype(bf16), w_query[l].astype(bf16), sub_keys[l])

    n_sc = n * SC_SHARE_PERCENT // 100 // SC_SHARE_ALIGN * SC_SHARE_ALIGN
    if n_sc > 0:
        x1_a, h2_a, idx_t, gates_t = _post_attention(*post_args, S, n_sc)
        peer_sc = _sc_peer_experts(table, idx_t.T.reshape(-1), gates_t.T.reshape(-1), h2_a)
    out = _tc_route_experts(*post_args, final_g, table, S, n_sc)
    if n_sc > 0:
        out = _finish_tokens(peer_sc.reshape(n_sc, D), x1_a, mod[:, 5:6, :], final_g.reshape(1, D), out, S)
    return out.reshape(B, S, D)
```

```python
import functools

import jax
import jax.numpy as jnp
from jax import lax
from jax.experimental import pallas as pl
from jax.experimental.pallas import tpu as pltpu
from jax.experimental.pallas import tpu_sc as plsc

D_MODEL = 1024
ATT_HEADS = 8
ATT_HEAD_DIM = 64
ATT_WIDTH = ATT_HEADS * ATT_HEAD_DIM
GM_GROUPS = 4
GM_GROUP_DIM = 128
GM_WIDTH = GM_GROUPS * GM_GROUP_DIM
GM_CHUNK = 128
PEER_HEADS = 8
PEER_KEY_DIM = 256
PEER_HALF = PEER_KEY_DIM // 2
N_KEYS = 128
PEER_TOPK = 16
PEER_SLOTS = PEER_HEADS * PEER_TOPK
SPLIT_POINTS = (ATT_WIDTH, 2 * ATT_WIDTH, 3 * ATT_WIDTH, 3 * ATT_WIDTH + ATT_HEADS,
                3 * ATT_WIDTH + ATT_HEADS + 2 * GM_WIDTH,
                3 * ATT_WIDTH + ATT_HEADS + 2 * GM_WIDTH + D_MODEL)
EPS = 1e-6

LANES = 128
EXPERT_TOK_BLOCK = 256
EXPERT_SLOTS = 8
DMA_THREADS = 2
FINISH_TOK_BLOCK = 256
PACK_EXPERT_BLOCK = 256
SC_GATHER_ROWS = 32
SC_ROW_GROUP = 4
SC_STORE_BATCH = 8
SC_SHARE_PERCENT = 53
SC_SHARE_ALIGN = 256


def _gelu(x):
    return 0.5 * x * (1.0 + jnp.tanh(0.7978845608028654 * (x + 0.044715 * (x * x * x))))


def _mod_kernel(c_ref, w_ref, b_ref, o_ref):
    c = c_ref[...]
    sc = c * jax.nn.sigmoid(c)
    o_ref[...] = jnp.dot(sc, w_ref[...], precision=lax.Precision.HIGHEST,
                         preferred_element_type=jnp.float32) + b_ref[...]


def _modulation(c, w_mod, b_mod):
    b, d = c.shape
    cols = w_mod.shape[1]
    return pl.pallas_call(
        _mod_kernel,
        grid=(cols // d,),
        in_specs=[pl.BlockSpec((b, d), lambda j: (0, 0)),
                  pl.BlockSpec((d, d), lambda j: (0, j)),
                  pl.BlockSpec((1, d), lambda j: (0, j))],
        out_specs=pl.BlockSpec((b, d), lambda j: (0, j)),
        out_shape=jax.ShapeDtypeStruct((b, cols), jnp.float32),
        name="modulation",
    )(c, w_mod, b_mod.reshape(1, cols))


INPROJ_TOK_BLOCK = 256


def _inproj_kernel(x_ref, mod_ref, n1g_ref, wqkv_ref, wf_ref, bf_ref, wz_ref, wg_ref, lng_ref,
                   wsp_ref, bsp_ref, wb_ref, qkv_ref, cum_ref, cumt_ref, sga_ref, gb_ref, carry):
    f32, bf16 = jnp.float32, jnp.bfloat16
    tm, d = x_ref.shape[1], x_ref.shape[2]
    x = x_ref[0]
    sh1 = mod_ref[0, 0:1, :]
    sc1 = mod_ref[0, 1:2, :]
    h = x * lax.rsqrt(jnp.mean(x * x, axis=-1, keepdims=True) + EPS) * n1g_ref[...]
    hb = (h * (1.0 + sc1) + sh1).astype(bf16)

    qkv = jnp.dot(hb, wqkv_ref[...], preferred_element_type=f32)
    qkv_ref[:, 0:ATT_WIDTH] = (qkv[:, 0:ATT_WIDTH] * (ATT_HEAD_DIM ** -0.5)).astype(bf16)
    qkv_ref[:, ATT_WIDTH:] = qkv[:, ATT_WIDTH:].astype(bf16)

    f = jnp.dot(hb, wf_ref[...], preferred_element_type=f32) + bf_ref[...]
    logf = jnp.minimum(f, 0.0) - jnp.log1p(jnp.exp(-jnp.abs(f)))

    @pl.when(pl.program_id(1) == 0)
    def _():
        carry[...] = jnp.zeros_like(carry)

    tri = (lax.broadcasted_iota(jnp.int32, (tm, tm), 0)
           >= lax.broadcasted_iota(jnp.int32, (tm, tm), 1)).astype(f32)
    cum = jnp.dot(tri, logf, precision=lax.Precision.HIGHEST, preferred_element_type=f32) + carry[...]
    cum_ref[...] = cum
    cumt_ref[0, 0] = jnp.transpose(cum)[0:ATT_HEADS, :]
    carry[...] = cum[tm - 1:tm, :]

    gz = _gelu(jnp.dot(hb, wz_ref[...], preferred_element_type=f32))
    u = gz[:, 0:GM_WIDTH]
    v = gz[:, GM_WIDTH:]
    mu = jnp.mean(v, axis=-1, keepdims=True)
    var = jnp.mean(jnp.square(v - mu), axis=-1, keepdims=True)
    vn = ((v - mu) * lax.rsqrt(var + EPS) * lng_ref[...]).astype(bf16)
    tril = (lax.broadcasted_iota(jnp.int32, (GM_CHUNK, GM_CHUNK), 0)
            >= lax.broadcasted_iota(jnp.int32, (GM_CHUNK, GM_CHUNK), 1))
    w_sp = [jnp.where(tril, wsp_ref[g], 0.0).astype(bf16) for g in range(GM_GROUPS)]
    rows = []
    for ck in range(tm // GM_CHUNK):
        r0 = ck * GM_CHUNK
        cols = []
        for g in range(GM_GROUPS):
            c0 = g * GM_GROUP_DIM
            mixed = jnp.dot(w_sp[g], vn[r0:r0 + GM_CHUNK, c0:c0 + GM_GROUP_DIM],
                            preferred_element_type=f32) + bsp_ref[g]
            cols.append(u[r0:r0 + GM_CHUNK, c0:c0 + GM_GROUP_DIM] * mixed)
        rows.append(jnp.concatenate(cols, axis=1))
    yb = jnp.concatenate(rows, axis=0).astype(bf16)
    ybp = jnp.dot(yb, wb_ref[...], preferred_element_type=f32)

    sg = jax.nn.sigmoid(jnp.dot(hb, wg_ref[...], preferred_element_type=f32))
    sga_ref[...] = sg[:, 0:d].astype(bf16)
    gb_ref[...] = (sg[:, d:] * ybp).astype(bf16)


def _input_projection(x, mod, norm1_g, w_in, b_forget, ln_v_g, w_spatial, b_spatial, w_branch_b):
    B, S, d = x.shape
    n = B * S
    tm = INPROJ_TOK_BLOCK
    bf16 = jnp.bfloat16
    p0, p1, p2, p3, p4, p5 = SPLIT_POINTS
    w_qkv = w_in[:, 0:p2].astype(bf16)
    w_f = jnp.pad(w_in[:, p2:p3], ((0, 0), (0, LANES - ATT_HEADS))).astype(bf16)
    b_f = jnp.pad(b_forget, (0, LANES - ATT_HEADS)).reshape(1, LANES)
    w_z = w_in[:, p3:p4].astype(bf16)
    w_g = w_in[:, p4:].astype(bf16)
    nt = S // tm
    tok = lambda w: pl.BlockSpec((tm, w), lambda b, i: (b * nt + i, 0))
    full = lambda a: pl.BlockSpec(a.shape, lambda b, i: (0,) * a.ndim)
    args = (x, mod, norm1_g.reshape(1, d), w_qkv, w_f, b_f, w_z, w_g, ln_v_g.reshape(1, GM_WIDTH),
            w_spatial, b_spatial.reshape(GM_GROUPS, GM_CHUNK, 1), w_branch_b.astype(bf16))
    return pl.pallas_call(
        _inproj_kernel,
        grid=(B, nt),
        in_specs=[pl.BlockSpec((1, tm, d), lambda b, i: (b, i, 0)),
                  pl.BlockSpec((1, 6, d), lambda b, i: (b, 0, 0))] + [full(a) for a in args[2:]],
        out_specs=[tok(3 * ATT_WIDTH), tok(LANES),
                   pl.BlockSpec((1, 1, ATT_HEADS, tm), lambda b, i: (b, i, 0, 0)), tok(d), tok(d)],
        out_shape=[jax.ShapeDtypeStruct((n, 3 * ATT_WIDTH), bf16),
                   jax.ShapeDtypeStruct((n, LANES), jnp.float32),
                   jax.ShapeDtypeStruct((B, nt, ATT_HEADS, tm), jnp.float32),
                   jax.ShapeDtypeStruct((n, d), bf16),
                   jax.ShapeDtypeStruct((n, d), bf16)],
        scratch_shapes=[pltpu.VMEM((1, LANES), jnp.float32)],
        compiler_params=pltpu.CompilerParams(dimension_semantics=("arbitrary", "arbitrary"),
                                             vmem_limit_bytes=56 * 1024 * 1024),
        name="input_projection",
    )(*args)


ATT_BLOCK = 1024


def _fox_kernel(q_ref, k_ref, v_ref, cum_ref, cumt_ref, o_ref, *, blk, ratio):
    f32 = jnp.float32
    hp = pl.program_id(1)
    i = pl.program_id(2)
    dh = ATT_HEAD_DIM
    q2 = q_ref[...]
    first = lax.broadcasted_iota(jnp.int32, (1, 2 * dh), 1) < dh
    qs = (jnp.where(first, q2, jnp.zeros_like(q2)), jnp.where(first, jnp.zeros_like(q2), q2))
    cum_blk = cum_ref[...]
    lane = lax.broadcasted_iota(jnp.int32, cum_blk.shape, 1)
    cqs = [jnp.sum(jnp.where(lane == 2 * hp + e, cum_blk, 0.0), axis=1, keepdims=True)
           for e in range(2)]

    def block(j, carry, masked):
        off = pl.multiple_of(j * blk, blk)
        k2 = k_ref[0, pl.ds(off, blk), :]
        v2 = v_ref[0, pl.ds(off, blk), :]
        out = []
        for e in range(2):
            m, acc = carry[e]
            ck = jnp.concatenate([cumt_ref[0, j * ratio + a, pl.ds(2 * hp + e, 1), :]
                                  for a in range(ratio)], axis=1)
            s = lax.dot_general(qs[e], k2, (((1,), (1,)), ((), ())), preferred_element_type=f32)
            s = s + (cqs[e] - ck)
            if masked:
                causal = (lax.broadcasted_iota(jnp.int32, (blk, blk), 0)
                          >= lax.broadcasted_iota(jnp.int32, (blk, blk), 1))
                s = jnp.where(causal, s, -jnp.inf)
            m_new = jnp.maximum(m, jnp.max(s, axis=1, keepdims=True))
            alpha = jnp.exp(m - m_new)
            p = jnp.exp(s - m_new).astype(v2.dtype)
            own = first if e == 0 else jnp.logical_not(first)
            v_ones = jnp.where(own, v2, jnp.ones_like(v2))
            acc = alpha * acc + jnp.dot(p, v_ones, preferred_element_type=f32)
            out.append((m_new, acc))
        return tuple(out)

    one = (jnp.full((blk, 1), -1e30, f32), jnp.zeros((blk, 2 * dh), f32))
    carry = lax.fori_loop(0, i, lambda j, c: block(j, c, False), (one, one))
    (_, acc0), (_, acc1) = block(i, carry, True)
    ratio0 = acc0 / pltpu.roll(acc0, dh, axis=1)
    ratio1 = acc1 / pltpu.roll(acc1, dh, axis=1)
    o_ref[...] = jnp.where(first, ratio0, ratio1).astype(o_ref.dtype)


def _fox_attention(qkv, cum, cumt, batch, seq_len):
    n = qkv.shape[0]
    tm = cumt.shape[-1]
    blk = max(min(ATT_BLOCK, seq_len), tm)
    nb = seq_len // blk
    pair = 2 * ATT_HEAD_DIM
    n_pairs = ATT_WIDTH // pair
    qkv3 = qkv.reshape(batch, seq_len, 3 * ATT_WIDTH)
    return pl.pallas_call(
        functools.partial(_fox_kernel, blk=blk, ratio=blk // tm),
        grid=(batch, n_pairs, nb),
        in_specs=[pl.BlockSpec((blk, pair), lambda b, hp, i: (b * nb + i, hp)),
                  pl.BlockSpec((1, seq_len, pair), lambda b, hp, i: (b, 0, n_pairs + hp)),
                  pl.BlockSpec((1, seq_len, pair), lambda b, hp, i: (b, 0, 2 * n_pairs + hp)),
                  pl.BlockSpec((blk, LANES), lambda b, hp, i: (b * nb + i, 0)),
                  pl.BlockSpec((1,) + cumt.shape[1:], lambda b, hp, i: (b, 0, 0, 0))],
        out_specs=pl.BlockSpec((blk, pair), lambda b, hp, i: (b * nb + i, hp)),
        out_shape=jax.ShapeDtypeStruct((n, ATT_WIDTH), jnp.bfloat16),
        compiler_params=pltpu.CompilerParams(
            dimension_semantics=("arbitrary", "arbitrary", "arbitrary")),
        name="fox_attention",
    )(qkv, qkv3, qkv3, cum, cumt)


def _topk_rows(scores, k):
    rows, t = scores[0].shape
    iota = lax.broadcasted_iota(jnp.int32, (rows, t), 0).astype(jnp.float32)
    slot = lax.broadcasted_iota(jnp.int32, (k, t), 0)
    scores = list(scores)
    vals = [jnp.zeros((k, t), jnp.float32) for _ in scores]
    ids = [jnp.zeros((k, t), jnp.float32) for _ in scores]
    for j in range(k):
        for i, s in enumerate(scores):
            m = jnp.max(s, axis=0, keepdims=True)
            am = jnp.min(jnp.where(s == m, iota, float(rows)), axis=0, keepdims=True)
            vals[i] = jnp.where(slot == j, m, vals[i])
            ids[i] = jnp.where(slot == j, am, ids[i])
            scores[i] = jnp.where(iota == am, -jnp.inf, s)
    return [(v, i.astype(jnp.int32)) for v, i in zip(vals, ids)]


def _select_rows(table, pos):
    out = jnp.zeros(pos.shape, table.dtype)
    for r in range(table.shape[0]):
        out = jnp.where(pos == r, table[r:r + 1, :], out)
    return out


def _post_dense(ya_ref, sga_ref, gb_ref, x_ref, mod_ref, n2g_ref, wa_ref, wo_ref, wq_ref,
                x1_ref, h2_ref, q_scr):
    f32 = jnp.float32
    a = jnp.dot(ya_ref[...], wa_ref[...], preferred_element_type=f32)
    merged = sga_ref[...].astype(f32) * a + gb_ref[...].astype(f32)
    o = jnp.dot(merged.astype(jnp.bfloat16), wo_ref[...], preferred_element_type=f32)
    g1 = mod_ref[0, 2:3, :]
    sh2 = mod_ref[0, 3:4, :]
    sc2 = mod_ref[0, 4:5, :]
    x1 = x_ref[...] + g1 * o
    x1_ref[...] = x1
    h2 = x1 * lax.rsqrt(jnp.mean(x1 * x1, axis=-1, keepdims=True) + EPS) * n2g_ref[...]
    h2 = h2 * (1.0 + sc2) + sh2
    h2_ref[...] = h2
    qp = jnp.dot(h2.astype(jnp.bfloat16), wq_ref[...], preferred_element_type=f32)
    for j in range(2 * PEER_HEADS):
        q_scr[j] = qp[:, j * PEER_HALF:(j + 1) * PEER_HALF]


def _route_heads(hs, q_scr, keys_ref, idx_ref, gates_ref):
    f32 = jnp.float32
    half = PEER_TOPK // 2
    tail0 = PEER_TOPK + (half - 1) * half
    scores = []
    for h in hs:
        for p in range(2):
            q = q_scr[2 * h + p]
            keys = keys_ref[2 * h + p]
            scores.append(lax.dot_general(keys, q, (((1,), (1,)), ((), ())),
                                          precision=lax.Precision.HIGHEST,
                                          preferred_element_type=f32))
    tops = _topk_rows(scores, PEER_TOPK)
    cands = []
    for e in range(len(hs)):
        (s1, _), (s2, _) = tops[2 * e], tops[2 * e + 1]
        blocks = [s1[0:1, :] + s2]
        blocks += [s1[a:a + 1, :] + s2[0:half, :] for a in range(1, half)]
        blocks += [s1[half:, :] + s2[0:1, :]]
        cands.append(jnp.concatenate(blocks, axis=0))
    best = _topk_rows(cands, PEER_TOPK)
    for e, h in enumerate(hs):
        (_, i1), (_, i2) = tops[2 * e], tops[2 * e + 1]
        vals, pos = best[e]
        mid = pos - PEER_TOPK
        ra = jnp.where(pos < PEER_TOPK, 0,
                       jnp.where(pos < tail0, 1 + (mid >> (half.bit_length() - 1)), pos - tail0 + half))
        rb = jnp.where(pos < PEER_TOPK, pos, jnp.where(pos < tail0, mid & (half - 1), 0))
        eid = _select_rows(i1, ra) * N_KEYS + _select_rows(i2, rb)
        ex = jnp.exp(vals - vals[0:1, :])
        g = ex / jnp.sum(ex, axis=0, keepdims=True)
        row = pl.multiple_of(h * PEER_TOPK, PEER_TOPK)
        idx_ref[pl.ds(row, PEER_TOPK), :] = eid
        gates_ref[pl.ds(row, PEER_TOPK), :] = g


def _post_kernel(ya_ref, sga_ref, gb_ref, x_ref, mod_ref, n2g_ref, wa_ref, wo_ref, wq_ref, keys_ref,
                 x1_ref, h2_ref, idx_ref, gates_ref, q_scr):
    _post_dense(ya_ref, sga_ref, gb_ref, x_ref, mod_ref, n2g_ref, wa_ref, wo_ref, wq_ref,
                x1_ref, h2_ref, q_scr)

    def heads(hg, carry):
        _route_heads([hg * ROUTE_HEADS + e for e in range(ROUTE_HEADS)], q_scr, keys_ref, idx_ref, gates_ref)
        return carry

    lax.fori_loop(0, PEER_HEADS // ROUTE_HEADS, heads, 0)


POST_TOK_BLOCK = 256
ROUTE_HEADS = 4


def _post_attention(ya, sga, gb, x, mod, norm2_g, w_a, w_out, w_query, sub_keys, seq_len, n):
    d = x.shape[1]
    tm = POST_TOK_BLOCK
    assert n % tm == 0
    blocks_per_seq = seq_len // tm
    aw = ya.shape[1]
    keys = sub_keys.reshape(2 * PEER_HEADS, N_KEYS, PEER_HALF)
    tok = lambda w: pl.BlockSpec((tm, w), lambda i: (i, 0))
    full = lambda a: pl.BlockSpec(a.shape, lambda i: (0,) * a.ndim)
    return pl.pallas_call(
        _post_kernel,
        grid=(n // tm,),
        in_specs=[tok(aw), tok(d), tok(d), tok(d),
                  pl.BlockSpec((1, 6, d), lambda i: (i // blocks_per_seq, 0, 0)),
                  full(norm2_g), full(w_a), full(w_out), full(w_query), full(keys)],
        out_specs=[tok(d), tok(d),
                   pl.BlockSpec((PEER_SLOTS, tm), lambda i: (0, i)),
                   pl.BlockSpec((PEER_SLOTS, tm), lambda i: (0, i))],
        out_shape=[jax.ShapeDtypeStruct((n, d), jnp.float32),
                   jax.ShapeDtypeStruct((n, d), jnp.float32),
                   jax.ShapeDtypeStruct((PEER_SLOTS, n), jnp.int32),
                   jax.ShapeDtypeStruct((PEER_SLOTS, n), jnp.float32)],
        scratch_shapes=[pltpu.VMEM((2 * PEER_HEADS, tm, PEER_HALF), jnp.float32)],
        compiler_params=pltpu.CompilerParams(dimension_semantics=("arbitrary",),
                                             vmem_limit_bytes=48 * 1024 * 1024),
        name="post_attention",
    )(ya, sga, gb, x, mod, norm2_g, w_a, w_out, w_query, keys)


def _rowsum_bcast(p, ones_bf16):
    hi = p.astype(jnp.bfloat16)
    lo = (p - hi.astype(jnp.float32)).astype(jnp.bfloat16)
    return (jnp.dot(hi, ones_bf16, preferred_element_type=jnp.float32)
            + jnp.dot(lo, ones_bf16, preferred_element_type=jnp.float32))


def _word_halves(w):
    return (lax.bitcast_convert_type(w & jnp.uint32(0xFFFF0000), jnp.float32),
            lax.bitcast_convert_type(w << 16, jnp.float32))


def _eval_experts(chunk, hrow, grow):
    half = hrow.shape[1] // LANES // 2
    ones_bf16 = jnp.ones((LANES, LANES), jnp.bfloat16)
    eye = (lax.broadcasted_iota(jnp.int32, (PEER_SLOTS, LANES), 0)
           == lax.broadcasted_iota(jnp.int32, (PEER_SLOTS, LANES), 1))
    hpart = lambda c: hrow[:, c * LANES:(c + 1) * LANES]
    psum = None
    for c in range(half):
        hi, lo = _word_halves(chunk(c))
        p = hi * hpart(c) + lo * hpart(c + half)
        psum = p if psum is None else psum + p
    act = _gelu(_rowsum_bcast(psum, ones_bf16))
    gcol = _rowsum_bcast(jnp.where(eye, grow, 0.0), ones_bf16)
    coef = gcol * act
    outs_hi, outs_lo = [], []
    for c in range(half):
        hi, lo = _word_halves(chunk(c + half))
        outs_hi.append(jnp.sum(coef * hi, axis=0, keepdims=True))
        outs_lo.append(jnp.sum(coef * lo, axis=0, keepdims=True))
    return jnp.concatenate(outs_hi + outs_lo, axis=-1)


def _finish_block(x_ref, g2_ref, fg_ref, peer, out_ref):
    y = x_ref[...] + g2_ref[0] * peer[...]
    out_ref[...] = y * lax.rsqrt(jnp.mean(y * y, axis=-1, keepdims=True) + EPS) * fg_ref[...]


def _expert_cost(n_tok, d):
    pairs = n_tok * PEER_SLOTS
    return pl.CostEstimate(flops=4 * pairs * d, transcendentals=pairs,
                           bytes_accessed=4 * pairs * d + 12 * n_tok * d + 8 * pairs)


def _sc_tanh(y):
    return 1.0 - 2.0 / (jnp.exp(2.0 * y) + 1.0)


def _sc_peer_experts(table3, ids, gates, h2):
    n_tok = ids.shape[0] // PEER_SLOTS
    _, n_chunks, lanes = table3.shape
    assert h2.shape[1] == n_chunks * lanes
    info = plsc.get_sparse_core_info()
    sl = info.num_lanes
    n_workers = info.num_cores * info.num_subcores
    tok_per_worker = n_tok // n_workers
    assert tok_per_worker * n_workers == n_tok and tok_per_worker % 2 == 0
    rows = SC_GATHER_ROWS
    n_sub = PEER_SLOTS // rows
    assert n_sub % 2 == 0
    half = n_chunks // 2
    pieces = [(c, k * sl) for c in range(half) for k in range(lanes // sl)]
    mesh = plsc.VectorSubcoreMesh(core_axis_name="c", subcore_axis_name="s")
    buf = lambda dt: pltpu.VMEM((rows, n_chunks, lanes), dt)

    @functools.partial(
        pl.kernel, mesh=mesh,
        out_type=jax.ShapeDtypeStruct((n_tok, n_chunks, lanes), jnp.float32),
        scratch_types=[pltpu.VMEM((PEER_SLOTS,), jnp.int32), pltpu.VMEM((PEER_SLOTS,), jnp.int32),
                       pltpu.VMEM((PEER_SLOTS,), jnp.float32), pltpu.VMEM((PEER_SLOTS,), jnp.float32),
                       pltpu.VMEM((n_chunks * lanes,), jnp.float32),
                       pltpu.VMEM((n_chunks * lanes,), jnp.float32),
                       buf(jnp.uint32), buf(jnp.uint32), pltpu.VMEM((n_chunks, lanes), jnp.float32),
                       pltpu.SemaphoreType.DMA((2,)), pltpu.SemaphoreType.DMA((2,))],
        compiler_params=pltpu.CompilerParams(needs_layout_passes=False),
        cost_estimate=_expert_cost(n_tok, n_chunks * lanes),
        name="sc_peer_experts")
    def run(table_hbm, ids_hbm, gates_hbm, h_hbm, peer_hbm, ids0, ids1, g0, g1, h0, h1, rows0, rows1,
            out_v, gsem, isem):
        bufs = (rows0, rows1)
        ids_p, g_p, h_p = (ids0, ids1), (g0, g1), (h0, h1)
        t_first = (lax.axis_index("s") * info.num_cores + lax.axis_index("c")) * tok_per_worker

        def token_inputs(t, p):
            off = pl.multiple_of(t * PEER_SLOTS, PEER_SLOTS)
            return (pltpu.make_async_copy(ids_hbm.at[pl.ds(off, PEER_SLOTS)], ids_p[p], isem.at[p]),
                    pltpu.make_async_copy(gates_hbm.at[pl.ds(off, PEER_SLOTS)], g_p[p], isem.at[p]),
                    pltpu.make_async_copy(h_hbm.at[t], h_p[p], isem.at[p]))

        def gather(p, s):
            return pltpu.make_async_copy(table_hbm.at[ids_p[p].at[pl.ds(s * rows, rows)]],
                                         bufs[s % 2], gsem.at[s % 2])

        for cp in token_inputs(t_first, 0):
            cp.start()
        for cp in token_inputs(t_first, 0):
            cp.wait()
        gather(0, 0).start()

        def one_token(ti, p):
            t = t_first + ti
            g_v, h_v = g_p[p], h_p[p]
            has_next = ti + 1 < tok_per_worker

            @pl.when(has_next)
            def _():
                for cp in token_inputs(t + 1, 1 - p):
                    cp.start()

            for c, k in pieces:
                out_v[c, pl.ds(k, sl)] = jnp.zeros((sl,), jnp.float32)
                out_v[c + half, pl.ds(k, sl)] = jnp.zeros((sl,), jnp.float32)
            for s in range(n_sub):
                gather(p, s).wait()
                if s + 1 < n_sub:
                    gather(p, s + 1).start()
                else:
                    @pl.when(has_next)
                    def _():
                        for cp in token_inputs(t + 1, 1 - p):
                            cp.wait()
                        gather(1 - p, 0).start()
                rbuf = bufs[s % 2]

                @pl.loop(0, rows, step=SC_ROW_GROUP)
                def _(r0):
                    accs = [jnp.zeros((sl,), jnp.float32) for _ in range(SC_ROW_GROUP)]
                    for c, k in pieces:
                        h_hi = h_v[pl.ds(c * lanes + k, sl)]
                        h_lo = h_v[pl.ds((c + half) * lanes + k, sl)]
                        for j in range(SC_ROW_GROUP):
                            hi, lo = _word_halves(rbuf[r0 + j, c, pl.ds(k, sl)])
                            accs[j] = accs[j] + hi * h_hi + lo * h_lo
                    coefs = []
                    for j in range(SC_ROW_GROUP):
                        a = jnp.broadcast_to(jnp.sum(accs[j]), (sl,))
                        act = 0.5 * a * (1.0 + _sc_tanh(0.7978845608028654 * (a + 0.044715 * (a * a * a))))
                        gate = plsc.load_gather(g_v, [jnp.broadcast_to(s * rows + r0 + j, (sl,))])
                        coefs.append(gate * act)
                    for b0 in range(0, len(pieces), SC_STORE_BATCH):
                        tots = []
                        for c, k in pieces[b0:b0 + SC_STORE_BATCH]:
                            t_hi = t_lo = None
                            for j in range(SC_ROW_GROUP):
                                hi, lo = _word_halves(rbuf[r0 + j, c + half, pl.ds(k, sl)])
                                t_hi = coefs[j] * hi if t_hi is None else t_hi + coefs[j] * hi
                                t_lo = coefs[j] * lo if t_lo is None else t_lo + coefs[j] * lo
                            tots.append((t_hi, t_lo))
                        for (c, k), (t_hi, t_lo) in zip(pieces[b0:b0 + SC_STORE_BATCH], tots):
                            plsc.addupdate(out_v.at[c, pl.ds(k, sl)], t_hi)
                            plsc.addupdate(out_v.at[c + half, pl.ds(k, sl)], t_lo)

            pltpu.sync_copy(out_v, peer_hbm.at[t])

        @pl.loop(0, tok_per_worker, step=2)
        def _(ti):
            one_token(ti, 0)
            one_token(ti + 1, 1)

    return run(table3, ids, gates, h2)


def _finish_kernel(peer_ref, x_ref, g2_ref, fg_ref, prev_ref, out_ref):
    del prev_ref
    _finish_block(x_ref, g2_ref, fg_ref, peer_ref, out_ref)


def _finish_tokens(peer, x1, g2, fg, prev, seq_len):
    d = x1.shape[1]
    tm = FINISH_TOK_BLOCK
    return pl.pallas_call(
        _finish_kernel,
        grid=(peer.shape[0] // tm,),
        in_specs=[pl.BlockSpec((tm, d), lambda i: (i, 0)),
                  pl.BlockSpec((tm, d), lambda i: (i, 0)),
                  pl.BlockSpec((1, 1, d), lambda i: (i * tm // seq_len, 0, 0)),
                  pl.BlockSpec((1, d), lambda i: (0, 0)),
                  pl.BlockSpec(memory_space=pl.ANY)],
        out_specs=pl.BlockSpec((tm, d), lambda i: (i, 0)),
        out_shape=jax.ShapeDtypeStruct(prev.shape, prev.dtype),
        input_output_aliases={4: 0},
        compiler_params=pltpu.CompilerParams(dimension_semantics=("arbitrary",)),
        name="finish_tokens",
    )(peer, x1, g2, fg, prev)


def _pack_kernel(u_ref, v_ref, o_ref):
    te, d = u_ref.shape
    n_chunks = d // LANES

    def pairs(x):
        r = lax.bitcast_convert_type(x.astype(jnp.bfloat16).astype(jnp.float32), jnp.uint32)
        return r[:, :d // 2] | (r[:, d // 2:] >> 16)

    words = jnp.concatenate([pairs(u_ref[...]), pairs(v_ref[...])], axis=1)
    for c in range(n_chunks):
        o_ref[pl.ds(c, te, stride=n_chunks), :] = words[:, c * LANES:(c + 1) * LANES]


def _pack_expert_table(expert_u, expert_v):
    n_experts, d = expert_u.shape
    te = PACK_EXPERT_BLOCK
    n_chunks = d // LANES
    packed = pl.pallas_call(
        _pack_kernel,
        grid=(n_experts // te,),
        in_specs=[pl.BlockSpec((te, d), lambda i: (i, 0)), pl.BlockSpec((te, d), lambda i: (i, 0))],
        out_specs=pl.BlockSpec((te * n_chunks, LANES), lambda i: (i, 0)),
        out_shape=jax.ShapeDtypeStruct((n_experts * n_chunks, LANES), jnp.uint32),
        name="pack_experts",
    )(expert_u, expert_v)
    return packed.reshape(n_experts, n_chunks, LANES)


def _route_experts_kernel(ya_ref, sga_ref, gb_ref, x_ref, mod_ref, ya_nx, sga_nx, gb_nx, x_nx, mod_nx,
                          n2g_ref, wa_ref, wo_ref, wq_ref, keys_ref, fg_ref, tab_ref, out_ref,
                          q_scr, x1_v, h2_v, idx_v, gates_v, idx_tv, gates_tv, idx_s, isem,
                          *expert_scratch, tok_block, n_slots):
    bufs = expert_scratch[:n_slots]
    peer, sem = expert_scratch[n_slots], expert_scratch[n_slots + 1]
    i = pl.program_id(0)
    par = lax.rem(i, 2)
    n_chunks = x_ref.shape[-1] // LANES
    rows_per_tok = PEER_SLOTS * n_chunks
    group = tok_block // PEER_HEADS
    assert group % n_slots == 0 and n_slots - 1 < group

    def publish(p):
        idx_tv[...] = idx_v[...].T
        gates_tv[p] = gates_v[...].T
        cp = pltpu.make_async_copy(idx_tv, idx_s, isem)
        cp.start()
        cp.wait()

    @pl.when(i == 0)
    def _():
        _post_dense(ya_ref, sga_ref, gb_ref, x_ref, mod_ref, n2g_ref, wa_ref, wo_ref, wq_ref,
                    x1_v.at[0], h2_v.at[0], q_scr)

        def heads(hg, carry):
            _route_heads([hg * ROUTE_HEADS + e for e in range(ROUTE_HEADS)], q_scr, keys_ref, idx_v, gates_v)
            return carry

        lax.fori_loop(0, PEER_HEADS // ROUTE_HEADS, heads, 0)
        publish(0)

    x1_cur, h2_cur, gates_cur = x1_v.at[par], h2_v.at[par], gates_tv.at[par]
    _post_dense(ya_nx, sga_nx, gb_nx, x_nx, mod_nx, n2g_ref, wa_ref, wo_ref, wq_ref,
                x1_v.at[1 - par], h2_v.at[1 - par], q_scr)

    def issue(t, s):
        for r in range(PEER_SLOTS):
            row = pl.multiple_of(idx_s[t, r] * n_chunks, n_chunks)
            pltpu.make_async_copy(tab_ref.at[pl.ds(row, n_chunks), :],
                                  bufs[s].at[pl.ds(r * n_chunks, n_chunks), :],
                                  sem.at[s]).start(priority=r % DMA_THREADS)

    def step(t, s, prefetch):
        pltpu.make_async_copy(tab_ref.at[pl.ds(0, rows_per_tok), :], bufs[s], sem.at[s]).wait()
        if prefetch:
            issue(t + n_slots - 1, (s - 1) % n_slots)
        chunk = lambda c: bufs[s][pl.ds(c, PEER_SLOTS, stride=n_chunks), :]
        peer[pl.ds(t, 1), :] = _eval_experts(chunk, h2_cur[pl.ds(t, 1), :], gates_cur[pl.ds(t, 1), :])

    for t in range(n_slots - 1):
        issue(t, t)

    def super_group(h, last):
        _route_heads([h], q_scr, keys_ref, idx_v, gates_v)
        for k in range(group):
            t = h * group + k
            step(t, k % n_slots, (not last) or k + n_slots - 1 < group)

    def body(h, carry):
        super_group(h, False)
        return carry

    lax.fori_loop(0, PEER_HEADS - 1, body, 0)
    super_group(PEER_HEADS - 1, True)

    _finish_block(x1_cur, mod_ref.at[:, 5:6, :], fg_ref, peer, out_ref)
    publish(1 - par)


def _tc_route_experts(ya, sga, gb, x, mod, norm2_g, w_a, w_out, w_query, sub_keys, final_g, table,
                      seq_len, first_tok):
    n, d = x.shape
    tb = EXPERT_TOK_BLOCK
    assert tb == POST_TOK_BLOCK and tb % EXPERT_SLOTS == 0 and seq_len % tb == 0 and first_tok % tb == 0
    first = first_tok // tb
    blocks_per_seq = seq_len // tb
    n_chunks = d // LANES
    aw = ya.shape[1]
    keys = sub_keys.reshape(2 * PEER_HEADS, N_KEYS, PEER_HALF)
    n_blocks = (n - first_tok) // tb
    nxt = lambda i: jnp.minimum(i + 1, n_blocks - 1) + first
    tok = lambda w: pl.BlockSpec((tb, w), lambda i: (i + first, 0))
    tok_nx = lambda w: pl.BlockSpec((tb, w), lambda i: (nxt(i), 0))
    mod_spec = lambda blk: pl.BlockSpec((1, 6, d), lambda i: (blk(i) // blocks_per_seq, 0, 0))
    full = lambda a: pl.BlockSpec(a.shape, lambda i: (0,) * a.ndim)
    fg = final_g.reshape(1, d)
    f32, i32 = jnp.float32, jnp.int32
    return pl.pallas_call(
        functools.partial(_route_experts_kernel, tok_block=tb, n_slots=EXPERT_SLOTS),
        grid=(n_blocks,),
        in_specs=[tok(aw), tok(d), tok(d), tok(d), mod_spec(lambda i: i + first),
                  tok_nx(aw), tok_nx(d), tok_nx(d), tok_nx(d), mod_spec(nxt),
                  full(norm2_g), full(w_a), full(w_out), full(w_query), full(keys), full(fg),
                  pl.BlockSpec(memory_space=pl.ANY)],
        out_specs=tok(d),
        out_shape=jax.ShapeDtypeStruct((n, d), f32),
        scratch_shapes=(
            [pltpu.VMEM((2 * PEER_HEADS, tb, PEER_HALF), f32), pltpu.VMEM((2, tb, d), f32),
             pltpu.VMEM((2, tb, d), f32), pltpu.VMEM((PEER_SLOTS, tb), i32), pltpu.VMEM((PEER_SLOTS, tb), f32),
             pltpu.VMEM((tb, PEER_SLOTS), i32), pltpu.VMEM((2, tb, PEER_SLOTS), f32),
             pltpu.SMEM((tb, PEER_SLOTS), i32), pltpu.SemaphoreType.DMA]
            + [pltpu.VMEM((PEER_SLOTS * n_chunks, LANES), jnp.uint32) for _ in range(EXPERT_SLOTS)]
            + [pltpu.VMEM((tb, d), f32), pltpu.SemaphoreType.DMA((EXPERT_SLOTS,))]),
        compiler_params=pltpu.CompilerParams(dimension_semantics=("arbitrary",),
                                             vmem_limit_bytes=56 * 1024 * 1024),
        cost_estimate=_expert_cost(n - first_tok, d),
        name="route_experts",
    )(ya, sga, gb, x, mod, ya, sga, gb, x, mod, norm2_g, w_a, w_out, w_query, keys, fg,
      table.reshape(-1, LANES))


def kernel(x, c, w_mod, b_mod, norm1_g, w_in, b_forget, ln_v_g, w_spatial, b_spatial, w_branch_a, w_branch_b, w_out, norm2_g, w_query, sub_keys, expert_u, expert_v, final_g):
    B, S, D = x.shape
    n = B * S
    bf16 = jnp.bfloat16
    assert w_mod.shape[0] == 1, "the final RMSNorm is fused into the single layer's expert kernels"
    l = 0
    mod = _modulation(c, w_mod[l], b_mod[l]).reshape(B, 6, D)
    table = _pack_expert_table(expert_u[l], expert_v[l])

    qkv, cum, cumt, sga, gb = _input_projection(x, mod, norm1_g[l], w_in[l], b_forget[l], ln_v_g[l],
                                                w_spatial[l], b_spatial[l], w_branch_b[l])
    y_a = _fox_attention(qkv, cum, cumt, B, S)
    post_args = (y_a, sga, gb, x.reshape(n, D), mod, norm2_g[l].reshape(1, D), w_branch_a[l].astype(bf16),
                 w_out[l].astype(bf16), w_query[l].astype(bf16), sub_keys[l])

    n_sc = n * SC_SHARE_PERCENT // 100 // SC_SHARE_ALIGN * SC_SHARE_ALIGN
    if n_sc > 0:
        x1_a, h2_a, idx_t, gates_t = _post_attention(*post_args, S, n_sc)
        peer_sc = _sc_peer_experts(table, idx_t.T.reshape(-1), gates_t.T.reshape(-1), h2_a)
    out = _tc_route_experts(*post_args, final_g, table, S, n_sc)
    if n_sc > 0:
        out = _finish_tokens(peer_sc.reshape(n_sc, D), x1_a, mod[:, 5:6, :], final_g.reshape(1, D), out, S)
    return out.reshape(B, S, D)
```

```python
import functools

import jax
import jax.numpy as jnp
from jax import lax
from jax.experimental import pallas as pl
from jax.experimental.pallas import tpu as pltpu
from jax.experimental.pallas import tpu_sc as plsc

D_MODEL = 1024
ATT_HEADS = 8
ATT_HEAD_DIM = 64
ATT_WIDTH = ATT_HEADS * ATT_HEAD_DIM
GM_GROUPS = 4
GM_GROUP_DIM = 128
GM_WIDTH = GM_GROUPS * GM_GROUP_DIM
GM_CHUNK = 128
PEER_HEADS = 8
PEER_KEY_DIM = 256
PEER_HALF = PEER_KEY_DIM // 2
N_KEYS = 128
PEER_TOPK = 16
PEER_SLOTS = PEER_HEADS * PEER_TOPK
SPLIT_POINTS = (ATT_WIDTH, 2 * ATT_WIDTH, 3 * ATT_WIDTH, 3 * ATT_WIDTH + ATT_HEADS,
                3 * ATT_WIDTH + ATT_HEADS + 2 * GM_WIDTH,
                3 * ATT_WIDTH + ATT_HEADS + 2 * GM_WIDTH + D_MODEL)
EPS = 1e-6

LANES = 128
EXPERT_TOK_BLOCK = 256
EXPERT_SLOTS = 8
DMA_THREADS = 2
FINISH_TOK_BLOCK = 256
PACK_EXPERT_BLOCK = 256
SC_GATHER_ROWS = 32
SC_ROW_GROUP = 4
SC_STORE_BATCH = 8
SC_SHARE_PERCENT = 53
SC_SHARE_ALIGN = 256


def _gelu(x):
    return 0.5 * x * (1.0 + jnp.tanh(0.7978845608028654 * (x + 0.044715 * (x * x * x))))


def _mod_kernel(c_ref, w_ref, b_ref, o_ref):
    c = c_ref[...]
    sc = c * jax.nn.sigmoid(c)
    o_ref[...] = jnp.dot(sc, w_ref[...], precision=lax.Precision.HIGHEST,
                         preferred_element_type=jnp.float32) + b_ref[...]


def _modulation(c, w_mod, b_mod):
    b, d = c.shape
    cols = w_mod.shape[1]
    return pl.pallas_call(
        _mod_kernel,
        grid=(cols // d,),
        in_specs=[pl.BlockSpec((b, d), lambda j: (0, 0)),
                  pl.BlockSpec((d, d), lambda j: (0, j)),
                  pl.BlockSpec((1, d), lambda j: (0, j))],
        out_specs=pl.BlockSpec((b, d), lambda j: (0, j)),
        out_shape=jax.ShapeDtypeStruct((b, cols), jnp.float32),
        name="modulation",
    )(c, w_mod, b_mod.reshape(1, cols))


INPROJ_TOK_BLOCK = 256


def _inproj_kernel(x_ref, mod_ref, n1g_ref, wqkv_ref, wf_ref, bf_ref, wz_ref, wg_ref, lng_ref,
                   wsp_ref, bsp_ref, wb_ref, qkv_ref, cum_ref, cumt_ref, sga_ref, gb_ref, carry):
    f32, bf16 = jnp.float32, jnp.bfloat16
    tm, d = x_ref.shape[1], x_ref.shape[2]
    x = x_ref[0]
    sh1 = mod_ref[0, 0:1, :]
    sc1 = mod_ref[0, 1:2, :]
    h = x * lax.rsqrt(jnp.mean(x * x, axis=-1, keepdims=True) + EPS) * n1g_ref[...]
    hb = (h * (1.0 + sc1) + sh1).astype(bf16)

    qkv = jnp.dot(hb, wqkv_ref[...], preferred_element_type=f32)
    qkv_ref[:, 0:ATT_WIDTH] = (qkv[:, 0:ATT_WIDTH] * (ATT_HEAD_DIM ** -0.5)).astype(bf16)
    qkv_ref[:, ATT_WIDTH:] = qkv[:, ATT_WIDTH:].astype(bf16)

    f = jnp.dot(hb, wf_ref[...], preferred_element_type=f32) + bf_ref[...]
    logf = jnp.minimum(f, 0.0) - jnp.log1p(jnp.exp(-jnp.abs(f)))

    @pl.when(pl.program_id(1) == 0)
    def _():
        carry[...] = jnp.zeros_like(carry)

    tri = (lax.broadcasted_iota(jnp.int32, (tm, tm), 0)
           >= lax.broadcasted_iota(jnp.int32, (tm, tm), 1)).astype(f32)
    cum = jnp.dot(tri, logf, precision=lax.Precision.HIGHEST, preferred_element_type=f32) + carry[...]
    cum_ref[...] = cum
    cumt_ref[0, 0] = jnp.transpose(cum)[0:ATT_HEADS, :]
    carry[...] = cum[tm - 1:tm, :]

    gz = _gelu(jnp.dot(hb, wz_ref[...], preferred_element_type=f32))
    u = gz[:, 0:GM_WIDTH]
    v = gz[:, GM_WIDTH:]
    mu = jnp.mean(v, axis=-1, keepdims=True)
    var = jnp.mean(jnp.square(v - mu), axis=-1, keepdims=True)
    vn = ((v - mu) * lax.rsqrt(var + EPS) * lng_ref[...]).astype(bf16)
    tril = (lax.broadcasted_iota(jnp.int32, (GM_CHUNK, GM_CHUNK), 0)
            >= lax.broadcasted_iota(jnp.int32, (GM_CHUNK, GM_CHUNK), 1))
    w_sp = [jnp.where(tril, wsp_ref[g], 0.0).astype(bf16) for g in range(GM_GROUPS)]
    rows = []
    for ck in range(tm // GM_CHUNK):
        r0 = ck * GM_CHUNK
        cols = []
        for g in range(GM_GROUPS):
            c0 = g * GM_GROUP_DIM
            mixed = jnp.dot(w_sp[g], vn[r0:r0 + GM_CHUNK, c0:c0 + GM_GROUP_DIM],
                            preferred_element_type=f32) + bsp_ref[g]
            cols.append(u[r0:r0 + GM_CHUNK, c0:c0 + GM_GROUP_DIM] * mixed)
        rows.append(jnp.concatenate(cols, axis=1))
    yb = jnp.concatenate(rows, axis=0).astype(bf16)
    ybp = jnp.dot(yb, wb_ref[...], preferred_element_type=f32)

    sg = jax.nn.sigmoid(jnp.dot(hb, wg_ref[...], preferred_element_type=f32))
    sga_ref[...] = sg[:, 0:d].astype(bf16)
    gb_ref[...] = (sg[:, d:] * ybp).astype(bf16)


def _input_projection(x, mod, norm1_g, w_in, b_forget, ln_v_g, w_spatial, b_spatial, w_branch_b):
    B, S, d = x.shape
    n = B * S
    tm = INPROJ_TOK_BLOCK
    bf16 = jnp.bfloat16
    p0, p1, p2, p3, p4, p5 = SPLIT_POINTS
    w_qkv = w_in[:, 0:p2].astype(bf16)
    w_f = jnp.pad(w_in[:, p2:p3], ((0, 0), (0, LANES - ATT_HEADS))).astype(bf16)
    b_f = jnp.pad(b_forget, (0, LANES - ATT_HEADS)).reshape(1, LANES)
    w_z = w_in[:, p3:p4].astype(bf16)
    w_g = w_in[:, p4:].astype(bf16)
    nt = S // tm
    tok = lambda w: pl.BlockSpec((tm, w), lambda b, i: (b * nt + i, 0))
    full = lambda a: pl.BlockSpec(a.shape, lambda b, i: (0,) * a.ndim)
    args = (x, mod, norm1_g.reshape(1, d), w_qkv, w_f, b_f, w_z, w_g, ln_v_g.reshape(1, GM_WIDTH),
            w_spatial, b_spatial.reshape(GM_GROUPS, GM_CHUNK, 1), w_branch_b.astype(bf16))
    return pl.pallas_call(
        _inproj_kernel,
        grid=(B, nt),
        in_specs=[pl.BlockSpec((1, tm, d), lambda b, i: (b, i, 0)),
                  pl.BlockSpec((1, 6, d), lambda b, i: (b, 0, 0))] + [full(a) for a in args[2:]],
        out_specs=[tok(3 * ATT_WIDTH), tok(LANES),
                   pl.BlockSpec((1, 1, ATT_HEADS, tm), lambda b, i: (b, i, 0, 0)), tok(d), tok(d)],
        out_shape=[jax.ShapeDtypeStruct((n, 3 * ATT_WIDTH), bf16),
                   jax.ShapeDtypeStruct((n, LANES), jnp.float32),
                   jax.ShapeDtypeStruct((B, nt, ATT_HEADS, tm), jnp.float32),
                   jax.ShapeDtypeStruct((n, d), bf16),
                   jax.ShapeDtypeStruct((n, d), bf16)],
        scratch_shapes=[pltpu.VMEM((1, LANES), jnp.float32)],
        compiler_params=pltpu.CompilerParams(dimension_semantics=("arbitrary", "arbitrary"),
                                             vmem_limit_bytes=56 * 1024 * 1024),
        name="input_projection",
    )(*args)


ATT_BLOCK = 1024


def _fox_kernel(q_ref, k_ref, v_ref, cum_ref, cumt_ref, o_ref, *, blk, ratio):
    f32 = jnp.float32
    hp = pl.program_id(1)
    i = pl.program_id(2)
    dh = ATT_HEAD_DIM
    q2 = q_ref[...]
    first = lax.broadcasted_iota(jnp.int32, (1, 2 * dh), 1) < dh
    qs = (jnp.where(first, q2, jnp.zeros_like(q2)), jnp.where(first, jnp.zeros_like(q2), q2))
    cum_blk = cum_ref[...]
    lane = lax.broadcasted_iota(jnp.int32, cum_blk.shape, 1)
    cqs = [jnp.sum(jnp.where(lane == 2 * hp + e, cum_blk, 0.0), axis=1, keepdims=True)
           for e in range(2)]

    def block(j, carry, masked):
        off = pl.multiple_of(j * blk, blk)
        k2 = k_ref[0, pl.ds(off, blk), :]
        v2 = v_ref[0, pl.ds(off, blk), :]
        out = []
        for e in range(2):
            m, acc = carry[e]
            ck = jnp.concatenate([cumt_ref[0, j * ratio + a, pl.ds(2 * hp + e, 1), :]
                                  for a in range(ratio)], axis=1)
            s = lax.dot_general(qs[e], k2, (((1,), (1,)), ((), ())), preferred_element_type=f32)
            s = s + (cqs[e] - ck)
            if masked:
                causal = (lax.broadcasted_iota(jnp.int32, (blk, blk), 0)
                          >= lax.broadcasted_iota(jnp.int32, (blk, blk), 1))
                s = jnp.where(causal, s, -jnp.inf)
            m_new = jnp.maximum(m, jnp.max(s, axis=1, keepdims=True))
            alpha = jnp.exp(m - m_new)
            p = jnp.exp(s - m_new).astype(v2.dtype)
            own = first if e == 0 else jnp.logical_not(first)
            v_ones = jnp.where(own, v2, jnp.ones_like(v2))
            acc = alpha * acc + jnp.dot(p, v_ones, preferred_element_type=f32)
            out.append((m_new, acc))
        return tuple(out)

    one = (jnp.full((blk, 1), -1e30, f32), jnp.zeros((blk, 2 * dh), f32))
    carry = lax.fori_loop(0, i, lambda j, c: block(j, c, False), (one, one))
    (_, acc0), (_, acc1) = block(i, carry, True)
    ratio0 = acc0 / pltpu.roll(acc0, dh, axis=1)
    ratio1 = acc1 / pltpu.roll(acc1, dh, axis=1)
    o_ref[...] = jnp.where(first, ratio0, ratio1).astype(o_ref.dtype)


def _fox_attention(qkv, cum, cumt, batch, seq_len):
    n = qkv.shape[0]
    tm = cumt.shape[-1]
    blk = max(min(ATT_BLOCK, seq_len), tm)
    nb = seq_len // blk
    pair = 2 * ATT_HEAD_DIM
    n_pairs = ATT_WIDTH // pair
    qkv3 = qkv.reshape(batch, seq_len, 3 * ATT_WIDTH)
    return pl.pallas_call(
        functools.partial(_fox_kernel, blk=blk, ratio=blk // tm),
        grid=(batch, n_pairs, nb),
        in_specs=[pl.BlockSpec((blk, pair), lambda b, hp, i: (b * nb + i, hp)),
                  pl.BlockSpec((1, seq_len, pair), lambda b, hp, i: (b, 0, n_pairs + hp)),
                  pl.BlockSpec((1, seq_len, pair), lambda b, hp, i: (b, 0, 2 * n_pairs + hp)),
                  pl.BlockSpec((blk, LANES), lambda b, hp, i: (b * nb + i, 0)),
                  pl.BlockSpec((1,) + cumt.shape[1:], lambda b, hp, i: (b, 0, 0, 0))],
        out_specs=pl.BlockSpec((blk, pair), lambda b, hp, i: (b * nb + i, hp)),
        out_shape=jax.ShapeDtypeStruct((n, ATT_WIDTH), jnp.bfloat16),
        compiler_params=pltpu.CompilerParams(
            dimension_semantics=("arbitrary", "arbitrary", "arbitrary")),
        name="fox_attention",
    )(qkv, qkv3, qkv3, cum, cumt)


def _topk_rows(scores, k):
    rows, t = scores[0].shape
    iota = lax.broadcasted_iota(jnp.int32, (rows, t), 0).astype(jnp.float32)
    slot = lax.broadcasted_iota(jnp.int32, (k, t), 0)
    scores = list(scores)
    vals = [jnp.zeros((k, t), jnp.float32) for _ in scores]
    ids = [jnp.zeros((k, t), jnp.float32) for _ in scores]
    for j in range(k):
        for i, s in enumerate(scores):
            m = jnp.max(s, axis=0, keepdims=True)
            am = jnp.min(jnp.where(s == m, iota, float(rows)), axis=0, keepdims=True)
            vals[i] = jnp.where(slot == j, m, vals[i])
            ids[i] = jnp.where(slot == j, am, ids[i])
            scores[i] = jnp.where(iota == am, -jnp.inf, s)
    return [(v, i.astype(jnp.int32)) for v, i in zip(vals, ids)]


def _select_rows(table, pos):
    out = jnp.zeros(pos.shape, table.dtype)
    for r in range(table.shape[0]):
        out = jnp.where(pos == r, table[r:r + 1, :], out)
    return out


def _post_dense(ya_ref, sga_ref, gb_ref, x_ref, mod_ref, n2g_ref, wa_ref, wo_ref, wq_ref,
                x1_ref, h2_ref, q_scr):
    f32 = jnp.float32
    a = jnp.dot(ya_ref[...], wa_ref[...], preferred_element_type=f32)
    merged = sga_ref[...].astype(f32) * a + gb_ref[...].astype(f32)
    o = jnp.dot(merged.astype(jnp.bfloat16), wo_ref[...], preferred_element_type=f32)
    g1 = mod_ref[0, 2:3, :]
    sh2 = mod_ref[0, 3:4, :]
    sc2 = mod_ref[0, 4:5, :]
    x1 = x_ref[...] + g1 * o
    x1_ref[...] = x1
    h2 = x1 * lax.rsqrt(jnp.mean(x1 * x1, axis=-1, keepdims=True) + EPS) * n2g_ref[...]
    h2 = h2 * (1.0 + sc2) + sh2
    h2_ref[...] = h2
    qp = jnp.dot(h2.astype(jnp.bfloat16), wq_ref[...], preferred_element_type=f32)
    for j in range(2 * PEER_HEADS):
        q_scr[j] = qp[:, j * PEER_HALF:(j + 1) * PEER_HALF]


def _route_heads(hs, q_scr, keys_ref, idx_ref, gates_ref):
    f32 = jnp.float32
    half = PEER_TOPK // 2
    tail0 = PEER_TOPK + (half - 1) * half
    scores = []
    for h in hs:
        for p in range(2):
            q = q_scr[2 * h + p]
            keys = keys_ref[2 * h + p]
            scores.append(lax.dot_general(keys, q, (((1,), (1,)), ((), ())),
                                          precision=lax.Precision.HIGHEST,
                                          preferred_element_type=f32))
    tops = _topk_rows(scores, PEER_TOPK)
    cands = []
    for e in range(len(hs)):
        (s1, _), (s2, _) = tops[2 * e], tops[2 * e + 1]
        blocks = [s1[0:1, :] + s2]
        blocks += [s1[a:a + 1, :] + s2[0:half, :] for a in range(1, half)]
        blocks += [s1[half:, :] + s2[0:1, :]]
        cands.append(jnp.concatenate(blocks, axis=0))
    best = _topk_rows(cands, PEER_TOPK)
    for e, h in enumerate(hs):
        (_, i1), (_, i2) = tops[2 * e], tops[2 * e + 1]
        vals, pos = best[e]
        mid = pos - PEER_TOPK
        ra = jnp.where(pos < PEER_TOPK, 0,
                       jnp.where(pos < tail0, 1 + (mid >> (half.bit_length() - 1)), pos - tail0 + half))
        rb = jnp.where(pos < PEER_TOPK, pos, jnp.where(pos < tail0, mid & (half - 1), 0))
        eid = _select_rows(i1, ra) * N_KEYS + _select_rows(i2, rb)
        ex = jnp.exp(vals - vals[0:1, :])
        g = ex / jnp.sum(ex, axis=0, keepdims=True)
        row = pl.multiple_of(h * PEER_TOPK, PEER_TOPK)
        idx_ref[pl.ds(row, PEER_TOPK), :] = eid
        gates_ref[pl.ds(row, PEER_TOPK), :] = g


ROUTE_STAGES = 27


def _route_stage(k, h, q_scr, keys_ref, idx_ref, gates_ref, rs_s, rs_v, rs_i, rs_c, rs_b):
    f32 = jnp.float32
    half = PEER_TOPK // 2
    tail0 = PEER_TOPK + (half - 1) * half

    def extract(s_ref, v_ref, i_ref, j):
        s = s_ref[...]
        rows = s.shape[0]
        iota = lax.broadcasted_iota(jnp.int32, s.shape, 0).astype(f32)
        m = jnp.max(s, axis=0, keepdims=True)
        am = jnp.min(jnp.where(s == m, iota, float(rows)), axis=0, keepdims=True)
        v_ref[pl.ds(j, 1), :] = m
        i_ref[pl.ds(j, 1), :] = am
        s_ref[...] = jnp.where(iota == am, -jnp.inf, s)

    if k == 0:
        for p in range(2):
            rs_s[p] = lax.dot_general(keys_ref[2 * h + p], q_scr[2 * h + p], (((1,), (1,)), ((), ())),
                                      precision=lax.Precision.HIGHEST, preferred_element_type=f32)
    elif k <= PEER_TOPK:
        for p in range(2):
            extract(rs_s.at[p], rs_v.at[p], rs_i.at[p], k - 1)
    elif k == PEER_TOPK + 1:
        s1, s2 = rs_v[0], rs_v[1]
        blocks = [s1[0:1, :] + s2]
        blocks += [s1[a:a + 1, :] + s2[0:half, :] for a in range(1, half)]
        blocks += [s1[half:, :] + s2[0:1, :]]
        rs_c[...] = jnp.concatenate(blocks, axis=0)
    elif k <= PEER_TOPK + 1 + half:
        j = 2 * (k - PEER_TOPK - 2)
        extract(rs_c, rs_b.at[0], rs_b.at[1], j)
        extract(rs_c, rs_b.at[0], rs_b.at[1], j + 1)
    elif k == ROUTE_STAGES - 1:
        i1, i2 = rs_i[0].astype(jnp.int32), rs_i[1].astype(jnp.int32)
        vals, pos = rs_b[0], rs_b[1].astype(jnp.int32)
        mid = pos - PEER_TOPK
        ra = jnp.where(pos < PEER_TOPK, 0,
                       jnp.where(pos < tail0, 1 + (mid >> (half.bit_length() - 1)), pos - tail0 + half))
        rb = jnp.where(pos < PEER_TOPK, pos, jnp.where(pos < tail0, mid & (half - 1), 0))
        eid = _select_rows(i1, ra) * N_KEYS + _select_rows(i2, rb)
        ex = jnp.exp(vals - vals[0:1, :])
        g = ex / jnp.sum(ex, axis=0, keepdims=True)
        row = pl.multiple_of(h * PEER_TOPK, PEER_TOPK)
        idx_ref[pl.ds(row, PEER_TOPK), :] = eid
        gates_ref[pl.ds(row, PEER_TOPK), :] = g


def _post_kernel(ya_ref, sga_ref, gb_ref, x_ref, mod_ref, n2g_ref, wa_ref, wo_ref, wq_ref, keys_ref,
                 x1_ref, h2_ref, idx_ref, gates_ref, q_scr):
    _post_dense(ya_ref, sga_ref, gb_ref, x_ref, mod_ref, n2g_ref, wa_ref, wo_ref, wq_ref,
                x1_ref, h2_ref, q_scr)

    def heads(hg, carry):
        _route_heads([hg * ROUTE_HEADS + e for e in range(ROUTE_HEADS)], q_scr, keys_ref, idx_ref, gates_ref)
        return carry

    lax.fori_loop(0, PEER_HEADS // ROUTE_HEADS, heads, 0)


POST_TOK_BLOCK = 256
ROUTE_HEADS = 4


def _post_attention(ya, sga, gb, x, mod, norm2_g, w_a, w_out, w_query, sub_keys, seq_len, n):
    d = x.shape[1]
    tm = POST_TOK_BLOCK
    assert n % tm == 0
    blocks_per_seq = seq_len // tm
    aw = ya.shape[1]
    keys = sub_keys.reshape(2 * PEER_HEADS, N_KEYS, PEER_HALF)
    tok = lambda w: pl.BlockSpec((tm, w), lambda i: (i, 0))
    full = lambda a: pl.BlockSpec(a.shape, lambda i: (0,) * a.ndim)
    return pl.pallas_call(
        _post_kernel,
        grid=(n // tm,),
        in_specs=[tok(aw), tok(d), tok(d), tok(d),
                  pl.BlockSpec((1, 6, d), lambda i: (i // blocks_per_seq, 0, 0)),
                  full(norm2_g), full(w_a), full(w_out), full(w_query), full(keys)],
        out_specs=[tok(d), tok(d),
                   pl.BlockSpec((PEER_SLOTS, tm), lambda i: (0, i)),
                   pl.BlockSpec((PEER_SLOTS, tm), lambda i: (0, i))],
        out_shape=[jax.ShapeDtypeStruct((n, d), jnp.float32),
                   jax.ShapeDtypeStruct((n, d), jnp.float32),
                   jax.ShapeDtypeStruct((PEER_SLOTS, n), jnp.int32),
                   jax.ShapeDtypeStruct((PEER_SLOTS, n), jnp.float32)],
        scratch_shapes=[pltpu.VMEM((2 * PEER_HEADS, tm, PEER_HALF), jnp.float32)],
        compiler_params=pltpu.CompilerParams(dimension_semantics=("arbitrary",),
                                             vmem_limit_bytes=48 * 1024 * 1024),
        name="post_attention",
    )(ya, sga, gb, x, mod, norm2_g, w_a, w_out, w_query, keys)


def _rowsum_bcast(p, ones_bf16):
    hi = p.astype(jnp.bfloat16)
    lo = (p - hi.astype(jnp.float32)).astype(jnp.bfloat16)
    return (jnp.dot(hi, ones_bf16, preferred_element_type=jnp.float32)
            + jnp.dot(lo, ones_bf16, preferred_element_type=jnp.float32))


def _word_halves(w):
    return (lax.bitcast_convert_type(w & jnp.uint32(0xFFFF0000), jnp.float32),
            lax.bitcast_convert_type(w << 16, jnp.float32))


def _expert_coef(chunk, hrow, grow):
    half = hrow.shape[1] // LANES // 2
    ones_bf16 = jnp.ones((LANES, LANES), jnp.bfloat16)
    eye = (lax.broadcasted_iota(jnp.int32, (PEER_SLOTS, LANES), 0)
           == lax.broadcasted_iota(jnp.int32, (PEER_SLOTS, LANES), 1))
    hpart = lambda c: hrow[:, c * LANES:(c + 1) * LANES]
    psum = None
    for c in range(half):
        hi, lo = _word_halves(chunk(c))
        p = hi * hpart(c) + lo * hpart(c + half)
        psum = p if psum is None else psum + p
    act = _gelu(_rowsum_bcast(psum, ones_bf16))
    gcol = _rowsum_bcast(jnp.where(eye, grow, 0.0), ones_bf16)
    return gcol * act


def _expert_mix(chunk, coef, half):
    outs_hi, outs_lo = [], []
    for c in range(half):
        hi, lo = _word_halves(chunk(c + half))
        outs_hi.append(jnp.sum(coef * hi, axis=0, keepdims=True))
        outs_lo.append(jnp.sum(coef * lo, axis=0, keepdims=True))
    return jnp.concatenate(outs_hi + outs_lo, axis=-1)


def _finish_block(x_ref, g2_ref, fg_ref, peer, out_ref):
    y = x_ref[...] + g2_ref[0] * peer[...]
    out_ref[...] = y * lax.rsqrt(jnp.mean(y * y, axis=-1, keepdims=True) + EPS) * fg_ref[...]


def _expert_cost(n_tok, d):
    pairs = n_tok * PEER_SLOTS
    return pl.CostEstimate(flops=4 * pairs * d, transcendentals=pairs,
                           bytes_accessed=4 * pairs * d + 12 * n_tok * d + 8 * pairs)


def _sc_tanh(y):
    return 1.0 - 2.0 / (jnp.exp(2.0 * y) + 1.0)


def _sc_peer_experts(table3, ids, gates, h2):
    n_tok = ids.shape[0] // PEER_SLOTS
    _, n_chunks, lanes = table3.shape
    assert h2.shape[1] == n_chunks * lanes
    info = plsc.get_sparse_core_info()
    sl = info.num_lanes
    n_workers = info.num_cores * info.num_subcores
    tok_per_worker = n_tok // n_workers
    assert tok_per_worker * n_workers == n_tok and tok_per_worker % 2 == 0
    rows = SC_GATHER_ROWS
    n_sub = PEER_SLOTS // rows
    assert n_sub % 2 == 0
    half = n_chunks // 2
    pieces = [(c, k * sl) for c in range(half) for k in range(lanes // sl)]
    mesh = plsc.VectorSubcoreMesh(core_axis_name="c", subcore_axis_name="s")
    buf = lambda dt: pltpu.VMEM((rows, n_chunks, lanes), dt)

    @functools.partial(
        pl.kernel, mesh=mesh,
        out_type=jax.ShapeDtypeStruct((n_tok, n_chunks, lanes), jnp.float32),
        scratch_types=[pltpu.VMEM((PEER_SLOTS,), jnp.int32), pltpu.VMEM((PEER_SLOTS,), jnp.int32),
                       pltpu.VMEM((PEER_SLOTS,), jnp.float32), pltpu.VMEM((PEER_SLOTS,), jnp.float32),
                       pltpu.VMEM((n_chunks * lanes,), jnp.float32),
                       pltpu.VMEM((n_chunks * lanes,), jnp.float32),
                       buf(jnp.uint32), buf(jnp.uint32), pltpu.VMEM((n_chunks, lanes), jnp.float32),
                       pltpu.SemaphoreType.DMA((2,)), pltpu.SemaphoreType.DMA((2,))],
        compiler_params=pltpu.CompilerParams(needs_layout_passes=False),
        cost_estimate=_expert_cost(n_tok, n_chunks * lanes),
        name="sc_peer_experts")
    def run(table_hbm, ids_hbm, gates_hbm, h_hbm, peer_hbm, ids0, ids1, g0, g1, h0, h1, rows0, rows1,
            out_v, gsem, isem):
        bufs = (rows0, rows1)
        ids_p, g_p, h_p = (ids0, ids1), (g0, g1), (h0, h1)
        t_first = (lax.axis_index("s") * info.num_cores + lax.axis_index("c")) * tok_per_worker

        def token_inputs(t, p):
            off = pl.multiple_of(t * PEER_SLOTS, PEER_SLOTS)
            return (pltpu.make_async_copy(ids_hbm.at[pl.ds(off, PEER_SLOTS)], ids_p[p], isem.at[p]),
                    pltpu.make_async_copy(gates_hbm.at[pl.ds(off, PEER_SLOTS)], g_p[p], isem.at[p]),
                    pltpu.make_async_copy(h_hbm.at[t], h_p[p], isem.at[p]))

        def gather(p, s):
            return pltpu.make_async_copy(table_hbm.at[ids_p[p].at[pl.ds(s * rows, rows)]],
                                         bufs[s % 2], gsem.at[s % 2])

        for cp in token_inputs(t_first, 0):
            cp.start()
        for cp in token_inputs(t_first, 0):
            cp.wait()
        gather(0, 0).start()

        def one_token(ti, p):
            t = t_first + ti
            g_v, h_v = g_p[p], h_p[p]
            has_next = ti + 1 < tok_per_worker

            @pl.when(has_next)
            def _():
                for cp in token_inputs(t + 1, 1 - p):
                    cp.start()

            for c, k in pieces:
                out_v[c, pl.ds(k, sl)] = jnp.zeros((sl,), jnp.float32)
                out_v[c + half, pl.ds(k, sl)] = jnp.zeros((sl,), jnp.float32)
            for s in range(n_sub):
                gather(p, s).wait()
                if s + 1 < n_sub:
                    gather(p, s + 1).start()
                else:
                    @pl.when(has_next)
                    def _():
                        for cp in token_inputs(t + 1, 1 - p):
                            cp.wait()
                        gather(1 - p, 0).start()
                rbuf = bufs[s % 2]

                @pl.loop(0, rows, step=SC_ROW_GROUP)
                def _(r0):
                    accs = [jnp.zeros((sl,), jnp.float32) for _ in range(SC_ROW_GROUP)]
                    for c, k in pieces:
                        h_hi = h_v[pl.ds(c * lanes + k, sl)]
                        h_lo = h_v[pl.ds((c + half) * lanes + k, sl)]
                        for j in range(SC_ROW_GROUP):
                            hi, lo = _word_halves(rbuf[r0 + j, c, pl.ds(k, sl)])
                            accs[j] = accs[j] + hi * h_hi + lo * h_lo
                    coefs = []
                    for j in range(SC_ROW_GROUP):
                        a = jnp.broadcast_to(jnp.sum(accs[j]), (sl,))
                        act = 0.5 * a * (1.0 + _sc_tanh(0.7978845608028654 * (a + 0.044715 * (a * a * a))))
                        gate = plsc.load_gather(g_v, [jnp.broadcast_to(s * rows + r0 + j, (sl,))])
                        coefs.append(gate * act)
                    for b0 in range(0, len(pieces), SC_STORE_BATCH):
                        tots = []
                        for c, k in pieces[b0:b0 + SC_STORE_BATCH]:
                            t_hi = t_lo = None
                            for j in range(SC_ROW_GROUP):
                                hi, lo = _word_halves(rbuf[r0 + j, c + half, pl.ds(k, sl)])
                                t_hi = coefs[j] * hi if t_hi is None else t_hi + coefs[j] * hi
                                t_lo = coefs[j] * lo if t_lo is None else t_lo + coefs[j] * lo
                            tots.append((t_hi, t_lo))
                        for (c, k), (t_hi, t_lo) in zip(pieces[b0:b0 + SC_STORE_BATCH], tots):
                            plsc.addupdate(out_v.at[c, pl.ds(k, sl)], t_hi)
                            plsc.addupdate(out_v.at[c + half, pl.ds(k, sl)], t_lo)

            pltpu.sync_copy(out_v, peer_hbm.at[t])

        @pl.loop(0, tok_per_worker, step=2)
        def _(ti):
            one_token(ti, 0)
            one_token(ti + 1, 1)

    return run(table3, ids, gates, h2)


def _finish_kernel(peer_ref, x_ref, g2_ref, fg_ref, prev_ref, out_ref):
    del prev_ref
    _finish_block(x_ref, g2_ref, fg_ref, peer_ref, out_ref)


def _finish_tokens(peer, x1, g2, fg, prev, seq_len):
    d = x1.shape[1]
    tm = FINISH_TOK_BLOCK
    return pl.pallas_call(
        _finish_kernel,
        grid=(peer.shape[0] // tm,),
        in_specs=[pl.BlockSpec((tm, d), lambda i: (i, 0)),
                  pl.BlockSpec((tm, d), lambda i: (i, 0)),
                  pl.BlockSpec((1, 1, d), lambda i: (i * tm // seq_len, 0, 0)),
                  pl.BlockSpec((1, d), lambda i: (0, 0)),
                  pl.BlockSpec(memory_space=pl.ANY)],
        out_specs=pl.BlockSpec((tm, d), lambda i: (i, 0)),
        out_shape=jax.ShapeDtypeStruct(prev.shape, prev.dtype),
        input_output_aliases={4: 0},
        compiler_params=pltpu.CompilerParams(dimension_semantics=("arbitrary",)),
        name="finish_tokens",
    )(peer, x1, g2, fg, prev)


def _pack_kernel(u_ref, v_ref, o_ref):
    te, d = u_ref.shape
    n_chunks = d // LANES

    def pairs(x):
        r = lax.bitcast_convert_type(x.astype(jnp.bfloat16).astype(jnp.float32), jnp.uint32)
        return r[:, :d // 2] | (r[:, d // 2:] >> 16)

    words = jnp.concatenate([pairs(u_ref[...]), pairs(v_ref[...])], axis=1)
    for c in range(n_chunks):
        o_ref[pl.ds(c, te, stride=n_chunks), :] = words[:, c * LANES:(c + 1) * LANES]


def _pack_expert_table(expert_u, expert_v):
    n_experts, d = expert_u.shape
    te = PACK_EXPERT_BLOCK
    n_chunks = d // LANES
    packed = pl.pallas_call(
        _pack_kernel,
        grid=(n_experts // te,),
        in_specs=[pl.BlockSpec((te, d), lambda i: (i, 0)), pl.BlockSpec((te, d), lambda i: (i, 0))],
        out_specs=pl.BlockSpec((te * n_chunks, LANES), lambda i: (i, 0)),
        out_shape=jax.ShapeDtypeStruct((n_experts * n_chunks, LANES), jnp.uint32),
        name="pack_experts",
    )(expert_u, expert_v)
    return packed.reshape(n_experts, n_chunks, LANES)


def _route_experts_kernel(ya_ref, sga_ref, gb_ref, x_ref, mod_ref, ya_nx, sga_nx, gb_nx, x_nx, mod_nx,
                          n2g_ref, wa_ref, wo_ref, wq_ref, keys_ref, fg_ref, tab_ref, out_ref,
                          q_scr, x1_v, h2_v, idx_v, gates_v, idx_tv, gates_tv, idx_s, isem,
                          *expert_scratch, tok_block, n_slots):
    bufs = expert_scratch[:n_slots]
    peer, sem = expert_scratch[n_slots], expert_scratch[n_slots + 1]
    route_state = expert_scratch[n_slots + 2:]
    i = pl.program_id(0)
    par = lax.rem(i, 2)
    n_chunks = x_ref.shape[-1] // LANES
    rows_per_tok = PEER_SLOTS * n_chunks
    group = tok_block // PEER_HEADS
    assert group % n_slots == 0 and n_slots - 1 < group and ROUTE_STAGES <= group

    def publish(p):
        idx_tv[...] = idx_v[...].T
        gates_tv[p] = gates_v[...].T
        cp = pltpu.make_async_copy(idx_tv, idx_s, isem)
        cp.start()
        cp.wait()

    @pl.when(i == 0)
    def _():
        _post_dense(ya_ref, sga_ref, gb_ref, x_ref, mod_ref, n2g_ref, wa_ref, wo_ref, wq_ref,
                    x1_v.at[0], h2_v.at[0], q_scr)

        def heads(hg, carry):
            _route_heads([hg * ROUTE_HEADS + e for e in range(ROUTE_HEADS)], q_scr, keys_ref, idx_v, gates_v)
            return carry

        lax.fori_loop(0, PEER_HEADS // ROUTE_HEADS, heads, 0)
        publish(0)

    x1_cur, h2_cur, gates_cur = x1_v.at[par], h2_v.at[par], gates_tv.at[par]
    _post_dense(ya_nx, sga_nx, gb_nx, x_nx, mod_nx, n2g_ref, wa_ref, wo_ref, wq_ref,
                x1_v.at[1 - par], h2_v.at[1 - par], q_scr)

    def issue(t, s):
        for r in range(PEER_SLOTS):
            row = pl.multiple_of(idx_s[t, r] * n_chunks, n_chunks)
            pltpu.make_async_copy(tab_ref.at[pl.ds(row, n_chunks), :],
                                  bufs[s].at[pl.ds(r * n_chunks, n_chunks), :],
                                  sem.at[s]).start(priority=r % DMA_THREADS)

    chunk = lambda s: (lambda c: bufs[s][pl.ds(c, PEER_SLOTS, stride=n_chunks), :])
    ahead = n_slots - 1

    def step(t, k, prefetch):
        s = k % n_slots
        pltpu.make_async_copy(tab_ref.at[pl.ds(0, rows_per_tok), :], bufs[s], sem.at[s]).wait()
        if prefetch:
            issue(t + ahead, (s - 1) % n_slots)
        coef = _expert_coef(chunk(s), h2_cur[pl.ds(t, 1), :], gates_cur[pl.ds(t, 1), :])
        peer[pl.ds(t, 1), :] = _expert_mix(chunk(s), coef, n_chunks // 2)

    for t in range(ahead):
        issue(t, t)

    def super_group(h, last):
        for k in range(group):
            t = h * group + k
            step(t, k, (not last) or k + ahead < group)
            if k < ROUTE_STAGES:
                _route_stage(k, h, q_scr, keys_ref, idx_v, gates_v, *route_state)

    def body(h, carry):
        super_group(h, False)
        return carry

    lax.fori_loop(0, PEER_HEADS - 1, body, 0)
    super_group(PEER_HEADS - 1, True)

    _finish_block(x1_cur, mod_ref.at[:, 5:6, :], fg_ref, peer, out_ref)
    publish(1 - par)


def _tc_route_experts(ya, sga, gb, x, mod, norm2_g, w_a, w_out, w_query, sub_keys, final_g, table,
                      seq_len, first_tok):
    n, d = x.shape
    tb = EXPERT_TOK_BLOCK
    assert tb == POST_TOK_BLOCK and tb % EXPERT_SLOTS == 0 and seq_len % tb == 0 and first_tok % tb == 0
    first = first_tok // tb
    blocks_per_seq = seq_len // tb
    n_chunks = d // LANES
    aw = ya.shape[1]
    keys = sub_keys.reshape(2 * PEER_HEADS, N_KEYS, PEER_HALF)
    n_blocks = (n - first_tok) // tb
    nxt = lambda i: jnp.minimum(i + 1, n_blocks - 1) + first
    tok = lambda w: pl.BlockSpec((tb, w), lambda i: (i + first, 0))
    tok_nx = lambda w: pl.BlockSpec((tb, w), lambda i: (nxt(i), 0))
    mod_spec = lambda blk: pl.BlockSpec((1, 6, d), lambda i: (blk(i) // blocks_per_seq, 0, 0))
    full = lambda a: pl.BlockSpec(a.shape, lambda i: (0,) * a.ndim)
    fg = final_g.reshape(1, d)
    f32, i32 = jnp.float32, jnp.int32
    return pl.pallas_call(
        functools.partial(_route_experts_kernel, tok_block=tb, n_slots=EXPERT_SLOTS),
        grid=(n_blocks,),
        in_specs=[tok(aw), tok(d), tok(d), tok(d), mod_spec(lambda i: i + first),
                  tok_nx(aw), tok_nx(d), tok_nx(d), tok_nx(d), mod_spec(nxt),
                  full(norm2_g), full(w_a), full(w_out), full(w_query), full(keys), full(fg),
                  pl.BlockSpec(memory_space=pl.ANY)],
        out_specs=tok(d),
        out_shape=jax.ShapeDtypeStruct((n, d), f32),
        scratch_shapes=(
            [pltpu.VMEM((2 * PEER_HEADS, tb, PEER_HALF), f32), pltpu.VMEM((2, tb, d), f32),
             pltpu.VMEM((2, tb, d), f32), pltpu.VMEM((PEER_SLOTS, tb), i32), pltpu.VMEM((PEER_SLOTS, tb), f32),
             pltpu.VMEM((tb, PEER_SLOTS), i32), pltpu.VMEM((2, tb, PEER_SLOTS), f32),
             pltpu.SMEM((tb, PEER_SLOTS), i32), pltpu.SemaphoreType.DMA]
            + [pltpu.VMEM((PEER_SLOTS * n_chunks, LANES), jnp.uint32) for _ in range(EXPERT_SLOTS)]
            + [pltpu.VMEM((tb, d), f32), pltpu.SemaphoreType.DMA((EXPERT_SLOTS,))]
            + [pltpu.VMEM((2, N_KEYS, tb), f32), pltpu.VMEM((2, PEER_TOPK, tb), f32),
               pltpu.VMEM((2, PEER_TOPK, tb), f32),
               pltpu.VMEM((PEER_TOPK + (PEER_TOPK // 2 - 1) * (PEER_TOPK // 2) + PEER_TOPK // 2, tb), f32),
               pltpu.VMEM((2, PEER_TOPK, tb), f32)]),
        compiler_params=pltpu.CompilerParams(dimension_semantics=("arbitrary",),
                                             vmem_limit_bytes=56 * 1024 * 1024),
        cost_estimate=_expert_cost(n - first_tok, d),
        name="route_experts",
    )(ya, sga, gb, x, mod, ya, sga, gb, x, mod, norm2_g, w_a, w_out, w_query, keys, fg,
      table.reshape(-1, LANES))


def kernel(x, c, w_mod, b_mod, norm1_g, w_in, b_forget, ln_v_g, w_spatial, b_spatial, w_branch_a, w_branch_b, w_out, norm2_g, w_query, sub_keys, expert_u, expert_v, final_g):
    B, S, D = x.shape
    n = B * S
    bf16 = jnp.bfloat16
    assert w_mod.shape[0] == 1, "the final RMSNorm is fused into the single layer's expert kernels"
    l = 0
    mod = _modulation(c, w_mod[l], b_mod[l]).reshape(B, 6, D)
    table = _pack_expert_table(expert_u[l], expert_v[l])

    qkv, cum, cumt, sga, gb = _input_projection(x, mod, norm1_g[l], w_in[l], b_forget[l], ln_v_g[l],
                                                w_spatial[l], b_spatial[l], w_branch_b[l])
    y_a = _fox_attention(qkv, cum, cumt, B, S)
    post_args = (y_a, sga, gb, x.reshape(n, D), mod, norm2_g[l].reshape(1, D), w_branch_a[l].astype(bf16),
                 w_out[l].astype(bf16), w_query[l].astype(bf16), sub_keys[l])

    n_sc = n * SC_SHARE_PERCENT // 100 // SC_SHARE_ALIGN * SC_SHARE_ALIGN
    if n_sc > 0:
        x1_a, h2_a, idx_t, gates_t = _post_attention(*post_args, S, n_sc)
        peer_sc = _sc_peer_experts(table, idx_t.T.reshape(-1), gates_t.T.reshape(-1), h2_a)
    out = _tc_route_experts(*post_args, final_g, table, S, n_sc)
    if n_sc > 0:
        out = _finish_tokens(peer_sc.reshape(n_sc, D), x1_a, mod[:, 5:6, :], final_g.reshape(1, D), out, S)
    return out.reshape(B, S, D)
```

```python
import functools

import jax
import jax.numpy as jnp
from jax import lax
from jax.experimental import pallas as pl
from jax.experimental.pallas import tpu as pltpu
from jax.experimental.pallas import tpu_sc as plsc

D_MODEL = 1024
ATT_HEADS = 8
ATT_HEAD_DIM = 64
ATT_WIDTH = ATT_HEADS * ATT_HEAD_DIM
GM_GROUPS = 4
GM_GROUP_DIM = 128
GM_WIDTH = GM_GROUPS * GM_GROUP_DIM
GM_CHUNK = 128
PEER_HEADS = 8
PEER_KEY_DIM = 256
PEER_HALF = PEER_KEY_DIM // 2
N_KEYS = 128
PEER_TOPK = 16
PEER_SLOTS = PEER_HEADS * PEER_TOPK
SPLIT_POINTS = (ATT_WIDTH, 2 * ATT_WIDTH, 3 * ATT_WIDTH, 3 * ATT_WIDTH + ATT_HEADS,
                3 * ATT_WIDTH + ATT_HEADS + 2 * GM_WIDTH,
                3 * ATT_WIDTH + ATT_HEADS + 2 * GM_WIDTH + D_MODEL)
EPS = 1e-6

LANES = 128
EXPERT_TOK_BLOCK = 256
EXPERT_SLOTS = 8
DMA_THREADS = 2
FINISH_TOK_BLOCK = 256
PACK_EXPERT_BLOCK = 256
SC_GATHER_ROWS = 32
SC_ROW_GROUP = 4
SC_STORE_BATCH = 8
SC_SHARE_PERCENT = 52
SC_SHARE_ALIGN = 256


def _gelu(x):
    return 0.5 * x * (1.0 + jnp.tanh(0.7978845608028654 * (x + 0.044715 * (x * x * x))))


def _mod_kernel(c_ref, w_ref, b_ref, o_ref):
    c = c_ref[...]
    sc = c * jax.nn.sigmoid(c)
    o_ref[...] = jnp.dot(sc, w_ref[...], precision=lax.Precision.HIGHEST,
                         preferred_element_type=jnp.float32) + b_ref[...]


def _modulation(c, w_mod, b_mod):
    b, d = c.shape
    cols = w_mod.shape[1]
    return pl.pallas_call(
        _mod_kernel,
        grid=(cols // d,),
        in_specs=[pl.BlockSpec((b, d), lambda j: (0, 0)),
                  pl.BlockSpec((d, d), lambda j: (0, j)),
                  pl.BlockSpec((1, d), lambda j: (0, j))],
        out_specs=pl.BlockSpec((b, d), lambda j: (0, j)),
        out_shape=jax.ShapeDtypeStruct((b, cols), jnp.float32),
        name="modulation",
    )(c, w_mod, b_mod.reshape(1, cols))


INPROJ_TOK_BLOCK = 256


def _inproj_kernel(x_ref, mod_ref, n1g_ref, wqkv_ref, wf_ref, bf_ref, wz_ref, wg_ref, lng_ref,
                   wsp_ref, bsp_ref, wb_ref, qkv_ref, cum_ref, cumt_ref, sga_ref, gb_ref, carry):
    f32, bf16 = jnp.float32, jnp.bfloat16
    tm, d = x_ref.shape[1], x_ref.shape[2]
    x = x_ref[0]
    sh1 = mod_ref[0, 0:1, :]
    sc1 = mod_ref[0, 1:2, :]
    h = x * lax.rsqrt(jnp.mean(x * x, axis=-1, keepdims=True) + EPS) * n1g_ref[...]
    hb = (h * (1.0 + sc1) + sh1).astype(bf16)

    qkv = jnp.dot(hb, wqkv_ref[...], preferred_element_type=f32)
    qkv_ref[:, 0:ATT_WIDTH] = (qkv[:, 0:ATT_WIDTH] * (ATT_HEAD_DIM ** -0.5)).astype(bf16)
    qkv_ref[:, ATT_WIDTH:] = qkv[:, ATT_WIDTH:].astype(bf16)

    f = jnp.dot(hb, wf_ref[...], preferred_element_type=f32) + bf_ref[...]
    logf = jnp.minimum(f, 0.0) - jnp.log1p(jnp.exp(-jnp.abs(f)))

    @pl.when(pl.program_id(1) == 0)
    def _():
        carry[...] = jnp.zeros_like(carry)

    tri = (lax.broadcasted_iota(jnp.int32, (tm, tm), 0)
           >= lax.broadcasted_iota(jnp.int32, (tm, tm), 1)).astype(f32)
    cum = jnp.dot(tri, logf, precision=lax.Precision.HIGHEST, preferred_element_type=f32) + carry[...]
    cum_ref[...] = cum
    cumt_ref[0, 0] = jnp.transpose(cum)[0:ATT_HEADS, :]
    carry[...] = cum[tm - 1:tm, :]

    gz = _gelu(jnp.dot(hb, wz_ref[...], preferred_element_type=f32))
    u = gz[:, 0:GM_WIDTH]
    v = gz[:, GM_WIDTH:]
    mu = jnp.mean(v, axis=-1, keepdims=True)
    var = jnp.mean(jnp.square(v - mu), axis=-1, keepdims=True)
    vn = ((v - mu) * lax.rsqrt(var + EPS) * lng_ref[...]).astype(bf16)
    tril = (lax.broadcasted_iota(jnp.int32, (GM_CHUNK, GM_CHUNK), 0)
            >= lax.broadcasted_iota(jnp.int32, (GM_CHUNK, GM_CHUNK), 1))
    w_sp = [jnp.where(tril, wsp_ref[g], 0.0).astype(bf16) for g in range(GM_GROUPS)]
    rows = []
    for ck in range(tm // GM_CHUNK):
        r0 = ck * GM_CHUNK
        cols = []
        for g in range(GM_GROUPS):
            c0 = g * GM_GROUP_DIM
            mixed = jnp.dot(w_sp[g], vn[r0:r0 + GM_CHUNK, c0:c0 + GM_GROUP_DIM],
                            preferred_element_type=f32) + bsp_ref[g]
            cols.append(u[r0:r0 + GM_CHUNK, c0:c0 + GM_GROUP_DIM] * mixed)
        rows.append(jnp.concatenate(cols, axis=1))
    yb = jnp.concatenate(rows, axis=0).astype(bf16)
    ybp = jnp.dot(yb, wb_ref[...], preferred_element_type=f32)

    sg = jax.nn.sigmoid(jnp.dot(hb, wg_ref[...], preferred_element_type=f32))
    sga_ref[...] = sg[:, 0:d].astype(bf16)
    gb_ref[...] = (sg[:, d:] * ybp).astype(bf16)


def _input_projection(x, mod, norm1_g, w_in, b_forget, ln_v_g, w_spatial, b_spatial, w_branch_b):
    B, S, d = x.shape
    n = B * S
    tm = INPROJ_TOK_BLOCK
    bf16 = jnp.bfloat16
    p0, p1, p2, p3, p4, p5 = SPLIT_POINTS
    w_qkv = w_in[:, 0:p2].astype(bf16)
    w_f = jnp.pad(w_in[:, p2:p3], ((0, 0), (0, LANES - ATT_HEADS))).astype(bf16)
    b_f = jnp.pad(b_forget, (0, LANES - ATT_HEADS)).reshape(1, LANES)
    w_z = w_in[:, p3:p4].astype(bf16)
    w_g = w_in[:, p4:].astype(bf16)
    nt = S // tm
    tok = lambda w: pl.BlockSpec((tm, w), lambda b, i: (b * nt + i, 0))
    full = lambda a: pl.BlockSpec(a.shape, lambda b, i: (0,) * a.ndim)
    args = (x, mod, norm1_g.reshape(1, d), w_qkv, w_f, b_f, w_z, w_g, ln_v_g.reshape(1, GM_WIDTH),
            w_spatial, b_spatial.reshape(GM_GROUPS, GM_CHUNK, 1), w_branch_b.astype(bf16))
    return pl.pallas_call(
        _inproj_kernel,
        grid=(B, nt),
        in_specs=[pl.BlockSpec((1, tm, d), lambda b, i: (b, i, 0)),
                  pl.BlockSpec((1, 6, d), lambda b, i: (b, 0, 0))] + [full(a) for a in args[2:]],
        out_specs=[tok(3 * ATT_WIDTH), tok(LANES),
                   pl.BlockSpec((1, 1, ATT_HEADS, tm), lambda b, i: (b, i, 0, 0)), tok(d), tok(d)],
        out_shape=[jax.ShapeDtypeStruct((n, 3 * ATT_WIDTH), bf16),
                   jax.ShapeDtypeStruct((n, LANES), jnp.float32),
                   jax.ShapeDtypeStruct((B, nt, ATT_HEADS, tm), jnp.float32),
                   jax.ShapeDtypeStruct((n, d), bf16),
                   jax.ShapeDtypeStruct((n, d), bf16)],
        scratch_shapes=[pltpu.VMEM((1, LANES), jnp.float32)],
        compiler_params=pltpu.CompilerParams(dimension_semantics=("arbitrary", "arbitrary"),
                                             vmem_limit_bytes=56 * 1024 * 1024),
        name="input_projection",
    )(*args)


ATT_BLOCK = 1024


def _fox_kernel(q_ref, k_ref, v_ref, cum_ref, cumt_ref, o_ref, *, blk, ratio):
    f32 = jnp.float32
    hp = pl.program_id(1)
    i = pl.program_id(2)
    dh = ATT_HEAD_DIM
    q2 = q_ref[...]
    first = lax.broadcasted_iota(jnp.int32, (1, 2 * dh), 1) < dh
    qs = (jnp.where(first, q2, jnp.zeros_like(q2)), jnp.where(first, jnp.zeros_like(q2), q2))
    cum_blk = cum_ref[...]
    lane = lax.broadcasted_iota(jnp.int32, cum_blk.shape, 1)
    cqs = [jnp.sum(jnp.where(lane == 2 * hp + e, cum_blk, 0.0), axis=1, keepdims=True)
           for e in range(2)]

    def block(j, carry, masked):
        off = pl.multiple_of(j * blk, blk)
        k2 = k_ref[0, pl.ds(off, blk), :]
        v2 = v_ref[0, pl.ds(off, blk), :]
        out = []
        for e in range(2):
            m, acc = carry[e]
            ck = jnp.concatenate([cumt_ref[0, j * ratio + a, pl.ds(2 * hp + e, 1), :]
                                  for a in range(ratio)], axis=1)
            s = lax.dot_general(qs[e], k2, (((1,), (1,)), ((), ())), preferred_element_type=f32)
            s = s + (cqs[e] - ck)
            if masked:
                causal = (lax.broadcasted_iota(jnp.int32, (blk, blk), 0)
                          >= lax.broadcasted_iota(jnp.int32, (blk, blk), 1))
                s = jnp.where(causal, s, -jnp.inf)
            m_new = jnp.maximum(m, jnp.max(s, axis=1, keepdims=True))
            alpha = jnp.exp(m - m_new)
            p = jnp.exp(s - m_new).astype(v2.dtype)
            own = first if e == 0 else jnp.logical_not(first)
            v_ones = jnp.where(own, v2, jnp.ones_like(v2))
            acc = alpha * acc + jnp.dot(p, v_ones, preferred_element_type=f32)
            out.append((m_new, acc))
        return tuple(out)

    one = (jnp.full((blk, 1), -1e30, f32), jnp.zeros((blk, 2 * dh), f32))
    carry = lax.fori_loop(0, i, lambda j, c: block(j, c, False), (one, one))
    (_, acc0), (_, acc1) = block(i, carry, True)
    ratio0 = acc0 / pltpu.roll(acc0, dh, axis=1)
    ratio1 = acc1 / pltpu.roll(acc1, dh, axis=1)
    o_ref[...] = jnp.where(first, ratio0, ratio1).astype(o_ref.dtype)


def _fox_attention(qkv, cum, cumt, batch, seq_len):
    n = qkv.shape[0]
    tm = cumt.shape[-1]
    blk = max(min(ATT_BLOCK, seq_len), tm)
    nb = seq_len // blk
    pair = 2 * ATT_HEAD_DIM
    n_pairs = ATT_WIDTH // pair
    qkv3 = qkv.reshape(batch, seq_len, 3 * ATT_WIDTH)
    return pl.pallas_call(
        functools.partial(_fox_kernel, blk=blk, ratio=blk // tm),
        grid=(batch, n_pairs, nb),
        in_specs=[pl.BlockSpec((blk, pair), lambda b, hp, i: (b * nb + i, hp)),
                  pl.BlockSpec((1, seq_len, pair), lambda b, hp, i: (b, 0, n_pairs + hp)),
                  pl.BlockSpec((1, seq_len, pair), lambda b, hp, i: (b, 0, 2 * n_pairs + hp)),
                  pl.BlockSpec((blk, LANES), lambda b, hp, i: (b * nb + i, 0)),
                  pl.BlockSpec((1,) + cumt.shape[1:], lambda b, hp, i: (b, 0, 0, 0))],
        out_specs=pl.BlockSpec((blk, pair), lambda b, hp, i: (b * nb + i, hp)),
        out_shape=jax.ShapeDtypeStruct((n, ATT_WIDTH), jnp.bfloat16),
        compiler_params=pltpu.CompilerParams(
            dimension_semantics=("arbitrary", "arbitrary", "arbitrary")),
        name="fox_attention",
    )(qkv, qkv3, qkv3, cum, cumt)


def _topk_rows(scores, k):
    rows, t = scores[0].shape
    iota = lax.broadcasted_iota(jnp.int32, (rows, t), 0).astype(jnp.float32)
    slot = lax.broadcasted_iota(jnp.int32, (k, t), 0)
    scores = list(scores)
    vals = [jnp.zeros((k, t), jnp.float32) for _ in scores]
    ids = [jnp.zeros((k, t), jnp.float32) for _ in scores]
    for j in range(k):
        for i, s in enumerate(scores):
            m = jnp.max(s, axis=0, keepdims=True)
            am = jnp.min(jnp.where(s == m, iota, float(rows)), axis=0, keepdims=True)
            vals[i] = jnp.where(slot == j, m, vals[i])
            ids[i] = jnp.where(slot == j, am, ids[i])
            scores[i] = jnp.where(iota == am, -jnp.inf, s)
    return [(v, i.astype(jnp.int32)) for v, i in zip(vals, ids)]


def _select_rows(table, pos):
    out = jnp.zeros(pos.shape, table.dtype)
    for r in range(table.shape[0]):
        out = jnp.where(pos == r, table[r:r + 1, :], out)
    return out


def _post_dense(ya_ref, sga_ref, gb_ref, x_ref, mod_ref, n2g_ref, wa_ref, wo_ref, wq_ref,
                x1_ref, h2_ref, q_scr):
    f32 = jnp.float32
    a = jnp.dot(ya_ref[...], wa_ref[...], preferred_element_type=f32)
    merged = sga_ref[...].astype(f32) * a + gb_ref[...].astype(f32)
    o = jnp.dot(merged.astype(jnp.bfloat16), wo_ref[...], preferred_element_type=f32)
    g1 = mod_ref[0, 2:3, :]
    sh2 = mod_ref[0, 3:4, :]
    sc2 = mod_ref[0, 4:5, :]
    x1 = x_ref[...] + g1 * o
    x1_ref[...] = x1
    h2 = x1 * lax.rsqrt(jnp.mean(x1 * x1, axis=-1, keepdims=True) + EPS) * n2g_ref[...]
    h2 = h2 * (1.0 + sc2) + sh2
    h2_ref[...] = h2
    qp = jnp.dot(h2.astype(jnp.bfloat16), wq_ref[...], preferred_element_type=f32)
    for j in range(2 * PEER_HEADS):
        q_scr[j] = qp[:, j * PEER_HALF:(j + 1) * PEER_HALF]


def _route_heads(hs, q_scr, keys_ref, idx_ref, gates_ref):
    f32 = jnp.float32
    half = PEER_TOPK // 2
    tail0 = PEER_TOPK + (half - 1) * half
    scores = []
    for h in hs:
        for p in range(2):
            q = q_scr[2 * h + p]
            keys = keys_ref[2 * h + p]
            scores.append(lax.dot_general(keys, q, (((1,), (1,)), ((), ())),
                                          precision=lax.Precision.HIGHEST,
                                          preferred_element_type=f32))
    tops = _topk_rows(scores, PEER_TOPK)
    cands = []
    for e in range(len(hs)):
        (s1, _), (s2, _) = tops[2 * e], tops[2 * e + 1]
        blocks = [s1[0:1, :] + s2]
        blocks += [s1[a:a + 1, :] + s2[0:half, :] for a in range(1, half)]
        blocks += [s1[half:, :] + s2[0:1, :]]
        cands.append(jnp.concatenate(blocks, axis=0))
    best = _topk_rows(cands, PEER_TOPK)
    for e, h in enumerate(hs):
        (_, i1), (_, i2) = tops[2 * e], tops[2 * e + 1]
        vals, pos = best[e]
        mid = pos - PEER_TOPK
        ra = jnp.where(pos < PEER_TOPK, 0,
                       jnp.where(pos < tail0, 1 + (mid >> (half.bit_length() - 1)), pos - tail0 + half))
        rb = jnp.where(pos < PEER_TOPK, pos, jnp.where(pos < tail0, mid & (half - 1), 0))
        eid = _select_rows(i1, ra) * N_KEYS + _select_rows(i2, rb)
        ex = jnp.exp(vals - vals[0:1, :])
        g = ex / jnp.sum(ex, axis=0, keepdims=True)
        row = pl.multiple_of(h * PEER_TOPK, PEER_TOPK)
        idx_ref[pl.ds(row, PEER_TOPK), :] = eid
        gates_ref[pl.ds(row, PEER_TOPK), :] = g


ROUTE_STAGES = 27


def _route_stage(k, h, q_scr, keys_ref, idx_ref, gates_ref, rs_s, rs_v, rs_i, rs_c, rs_b):
    f32 = jnp.float32
    half = PEER_TOPK // 2
    tail0 = PEER_TOPK + (half - 1) * half

    def extract(s_ref, v_ref, i_ref, j):
        s = s_ref[...]
        rows = s.shape[0]
        iota = lax.broadcasted_iota(jnp.int32, s.shape, 0).astype(f32)
        m = jnp.max(s, axis=0, keepdims=True)
        am = jnp.min(jnp.where(s == m, iota, float(rows)), axis=0, keepdims=True)
        v_ref[pl.ds(j, 1), :] = m
        i_ref[pl.ds(j, 1), :] = am
        s_ref[...] = jnp.where(iota == am, -jnp.inf, s)

    if k == 0:
        for p in range(2):
            rs_s[p] = lax.dot_general(keys_ref[2 * h + p], q_scr[2 * h + p], (((1,), (1,)), ((), ())),
                                      precision=lax.Precision.HIGHEST, preferred_element_type=f32)
    elif k <= PEER_TOPK:
        for p in range(2):
            extract(rs_s.at[p], rs_v.at[p], rs_i.at[p], k - 1)
    elif k == PEER_TOPK + 1:
        s1, s2 = rs_v[0], rs_v[1]
        blocks = [s1[0:1, :] + s2]
        blocks += [s1[a:a + 1, :] + s2[0:half, :] for a in range(1, half)]
        blocks += [s1[half:, :] + s2[0:1, :]]
        rs_c[...] = jnp.concatenate(blocks, axis=0)
    elif k <= PEER_TOPK + 1 + half:
        j = 2 * (k - PEER_TOPK - 2)
        extract(rs_c, rs_b.at[0], rs_b.at[1], j)
        extract(rs_c, rs_b.at[0], rs_b.at[1], j + 1)
    elif k == ROUTE_STAGES - 1:
        i1, i2 = rs_i[0].astype(jnp.int32), rs_i[1].astype(jnp.int32)
        vals, pos = rs_b[0], rs_b[1].astype(jnp.int32)
        mid = pos - PEER_TOPK
        ra = jnp.where(pos < PEER_TOPK, 0,
                       jnp.where(pos < tail0, 1 + (mid >> (half.bit_length() - 1)), pos - tail0 + half))
        rb = jnp.where(pos < PEER_TOPK, pos, jnp.where(pos < tail0, mid & (half - 1), 0))
        eid = _select_rows(i1, ra) * N_KEYS + _select_rows(i2, rb)
        ex = jnp.exp(vals - vals[0:1, :])
        g = ex / jnp.sum(ex, axis=0, keepdims=True)
        row = pl.multiple_of(h * PEER_TOPK, PEER_TOPK)
        idx_ref[pl.ds(row, PEER_TOPK), :] = eid
        gates_ref[pl.ds(row, PEER_TOPK), :] = g


def _post_kernel(ya_ref, sga_ref, gb_ref, x_ref, mod_ref, n2g_ref, wa_ref, wo_ref, wq_ref, keys_ref,
                 x1_ref, h2_ref, idx_ref, gates_ref, q_scr):
    _post_dense(ya_ref, sga_ref, gb_ref, x_ref, mod_ref, n2g_ref, wa_ref, wo_ref, wq_ref,
                x1_ref, h2_ref, q_scr)

    def heads(hg, carry):
        _route_heads([hg * ROUTE_HEADS + e for e in range(ROUTE_HEADS)], q_scr, keys_ref, idx_ref, gates_ref)
        return carry

    lax.fori_loop(0, PEER_HEADS // ROUTE_HEADS, heads, 0)


POST_TOK_BLOCK = 256
ROUTE_HEADS = 4


def _post_attention(ya, sga, gb, x, mod, norm2_g, w_a, w_out, w_query, sub_keys, seq_len, n):
    d = x.shape[1]
    tm = POST_TOK_BLOCK
    assert n % tm == 0
    blocks_per_seq = seq_len // tm
    aw = ya.shape[1]
    keys = sub_keys.reshape(2 * PEER_HEADS, N_KEYS, PEER_HALF)
    tok = lambda w: pl.BlockSpec((tm, w), lambda i: (i, 0))
    full = lambda a: pl.BlockSpec(a.shape, lambda i: (0,) * a.ndim)
    return pl.pallas_call(
        _post_kernel,
        grid=(n // tm,),
        in_specs=[tok(aw), tok(d), tok(d), tok(d),
                  pl.BlockSpec((1, 6, d), lambda i: (i // blocks_per_seq, 0, 0)),
                  full(norm2_g), full(w_a), full(w_out), full(w_query), full(keys)],
        out_specs=[tok(d), tok(d),
                   pl.BlockSpec((PEER_SLOTS, tm), lambda i: (0, i)),
                   pl.BlockSpec((PEER_SLOTS, tm), lambda i: (0, i))],
        out_shape=[jax.ShapeDtypeStruct((n, d), jnp.float32),
                   jax.ShapeDtypeStruct((n, d), jnp.float32),
                   jax.ShapeDtypeStruct((PEER_SLOTS, n), jnp.int32),
                   jax.ShapeDtypeStruct((PEER_SLOTS, n), jnp.float32)],
        scratch_shapes=[pltpu.VMEM((2 * PEER_HEADS, tm, PEER_HALF), jnp.float32)],
        compiler_params=pltpu.CompilerParams(dimension_semantics=("arbitrary",),
                                             vmem_limit_bytes=48 * 1024 * 1024),
        name="post_attention",
    )(ya, sga, gb, x, mod, norm2_g, w_a, w_out, w_query, keys)


def _rowsum_bcast(p, ones_bf16):
    hi = p.astype(jnp.bfloat16)
    lo = (p - hi.astype(jnp.float32)).astype(jnp.bfloat16)
    return (jnp.dot(hi, ones_bf16, preferred_element_type=jnp.float32)
            + jnp.dot(lo, ones_bf16, preferred_element_type=jnp.float32))


def _word_halves(w):
    return (lax.bitcast_convert_type(w & jnp.uint32(0xFFFF0000), jnp.float32),
            lax.bitcast_convert_type(w << 16, jnp.float32))


def _expert_coef(chunk, hrow, grow):
    half = hrow.shape[1] // LANES // 2
    ones_bf16 = jnp.ones((LANES, LANES), jnp.bfloat16)
    eye = (lax.broadcasted_iota(jnp.int32, (PEER_SLOTS, LANES), 0)
           == lax.broadcasted_iota(jnp.int32, (PEER_SLOTS, LANES), 1))
    hpart = lambda c: hrow[:, c * LANES:(c + 1) * LANES]
    psum = None
    for c in range(half):
        hi, lo = _word_halves(chunk(c))
        p = hi * hpart(c) + lo * hpart(c + half)
        psum = p if psum is None else psum + p
    act = _gelu(_rowsum_bcast(psum, ones_bf16))
    gcol = _rowsum_bcast(jnp.where(eye, grow, 0.0), ones_bf16)
    return gcol * act


def _expert_mix(chunk, coef, half):
    outs_hi, outs_lo = [], []
    for c in range(half):
        hi, lo = _word_halves(chunk(c + half))
        outs_hi.append(jnp.sum(coef * hi, axis=0, keepdims=True))
        outs_lo.append(jnp.sum(coef * lo, axis=0, keepdims=True))
    return jnp.concatenate(outs_hi + outs_lo, axis=-1)


def _finish_block(x_ref, g2_ref, fg_ref, peer, out_ref):
    y = x_ref[...] + g2_ref[0] * peer[...]
    out_ref[...] = y * lax.rsqrt(jnp.mean(y * y, axis=-1, keepdims=True) + EPS) * fg_ref[...]


def _expert_cost(n_tok, d):
    pairs = n_tok * PEER_SLOTS
    return pl.CostEstimate(flops=4 * pairs * d, transcendentals=pairs,
                           bytes_accessed=4 * pairs * d + 12 * n_tok * d + 8 * pairs)


def _sc_tanh(y):
    return 1.0 - 2.0 / (jnp.exp(2.0 * y) + 1.0)


def _sc_peer_experts(table3, ids, gates, h2):
    n_tok = ids.shape[0] // PEER_SLOTS
    _, n_chunks, lanes = table3.shape
    assert h2.shape[1] == n_chunks * lanes
    info = plsc.get_sparse_core_info()
    sl = info.num_lanes
    n_workers = info.num_cores * info.num_subcores
    tok_per_worker = n_tok // n_workers
    assert tok_per_worker * n_workers == n_tok and tok_per_worker % 2 == 0
    rows = SC_GATHER_ROWS
    n_sub = PEER_SLOTS // rows
    assert n_sub % 2 == 0
    half = n_chunks // 2
    pieces = [(c, k * sl) for c in range(half) for k in range(lanes // sl)]
    mesh = plsc.VectorSubcoreMesh(core_axis_name="c", subcore_axis_name="s")
    buf = lambda dt: pltpu.VMEM((rows, n_chunks, lanes), dt)

    @functools.partial(
        pl.kernel, mesh=mesh,
        out_type=jax.ShapeDtypeStruct((n_tok, n_chunks, lanes), jnp.float32),
        scratch_types=[pltpu.VMEM((PEER_SLOTS,), jnp.int32), pltpu.VMEM((PEER_SLOTS,), jnp.int32),
                       pltpu.VMEM((PEER_SLOTS,), jnp.float32), pltpu.VMEM((PEER_SLOTS,), jnp.float32),
                       pltpu.VMEM((n_chunks * lanes,), jnp.float32),
                       pltpu.VMEM((n_chunks * lanes,), jnp.float32),
                       buf(jnp.uint32), buf(jnp.uint32), pltpu.VMEM((n_chunks, lanes), jnp.float32),
                       pltpu.SemaphoreType.DMA((2,)), pltpu.SemaphoreType.DMA((2,))],
        compiler_params=pltpu.CompilerParams(needs_layout_passes=False),
        cost_estimate=_expert_cost(n_tok, n_chunks * lanes),
        name="sc_peer_experts")
    def run(table_hbm, ids_hbm, gates_hbm, h_hbm, peer_hbm, ids0, ids1, g0, g1, h0, h1, rows0, rows1,
            out_v, gsem, isem):
        bufs = (rows0, rows1)
        ids_p, g_p, h_p = (ids0, ids1), (g0, g1), (h0, h1)
        t_first = (lax.axis_index("s") * info.num_cores + lax.axis_index("c")) * tok_per_worker

        def token_inputs(t, p):
            off = pl.multiple_of(t * PEER_SLOTS, PEER_SLOTS)
            return (pltpu.make_async_copy(ids_hbm.at[pl.ds(off, PEER_SLOTS)], ids_p[p], isem.at[p]),
                    pltpu.make_async_copy(gates_hbm.at[pl.ds(off, PEER_SLOTS)], g_p[p], isem.at[p]),
                    pltpu.make_async_copy(h_hbm.at[t], h_p[p], isem.at[p]))

        def gather(p, s):
            return pltpu.make_async_copy(table_hbm.at[ids_p[p].at[pl.ds(s * rows, rows)]],
                                         bufs[s % 2], gsem.at[s % 2])

        for cp in token_inputs(t_first, 0):
            cp.start()
        for cp in token_inputs(t_first, 0):
            cp.wait()
        gather(0, 0).start()

        def one_token(ti, p):
            t = t_first + ti
            g_v, h_v = g_p[p], h_p[p]
            has_next = ti + 1 < tok_per_worker

            @pl.when(has_next)
            def _():
                for cp in token_inputs(t + 1, 1 - p):
                    cp.start()

            for c, k in pieces:
                out_v[c, pl.ds(k, sl)] = jnp.zeros((sl,), jnp.float32)
                out_v[c + half, pl.ds(k, sl)] = jnp.zeros((sl,), jnp.float32)
            for s in range(n_sub):
                gather(p, s).wait()
                if s + 1 < n_sub:
                    gather(p, s + 1).start()
                else:
                    @pl.when(has_next)
                    def _():
                        for cp in token_inputs(t + 1, 1 - p):
                            cp.wait()
                        gather(1 - p, 0).start()
                rbuf = bufs[s % 2]

                @pl.loop(0, rows, step=SC_ROW_GROUP)
                def _(r0):
                    accs = [jnp.zeros((sl,), jnp.float32) for _ in range(SC_ROW_GROUP)]
                    for c, k in pieces:
                        h_hi = h_v[pl.ds(c * lanes + k, sl)]
                        h_lo = h_v[pl.ds((c + half) * lanes + k, sl)]
                        for j in range(SC_ROW_GROUP):
                            hi, lo = _word_halves(rbuf[r0 + j, c, pl.ds(k, sl)])
                            accs[j] = accs[j] + hi * h_hi + lo * h_lo
                    coefs = []
                    for j in range(SC_ROW_GROUP):
                        a = jnp.broadcast_to(jnp.sum(accs[j]), (sl,))
                        act = 0.5 * a * (1.0 + _sc_tanh(0.7978845608028654 * (a + 0.044715 * (a * a * a))))
                        gate = plsc.load_gather(g_v, [jnp.broadcast_to(s * rows + r0 + j, (sl,))])
                        coefs.append(gate * act)
                    for b0 in range(0, len(pieces), SC_STORE_BATCH):
                        tots = []
                        for c, k in pieces[b0:b0 + SC_STORE_BATCH]:
                            t_hi = t_lo = None
                            for j in range(SC_ROW_GROUP):
                                hi, lo = _word_halves(rbuf[r0 + j, c + half, pl.ds(k, sl)])
                                t_hi = coefs[j] * hi if t_hi is None else t_hi + coefs[j] * hi
                                t_lo = coefs[j] * lo if t_lo is None else t_lo + coefs[j] * lo
                            tots.append((t_hi, t_lo))
                        for (c, k), (t_hi, t_lo) in zip(pieces[b0:b0 + SC_STORE_BATCH], tots):
                            plsc.addupdate(out_v.at[c, pl.ds(k, sl)], t_hi)
                            plsc.addupdate(out_v.at[c + half, pl.ds(k, sl)], t_lo)

            pltpu.sync_copy(out_v, peer_hbm.at[t])

        @pl.loop(0, tok_per_worker, step=2)
        def _(ti):
            one_token(ti, 0)
            one_token(ti + 1, 1)

    return run(table3, ids, gates, h2)


def _finish_kernel(peer_ref, x_ref, g2_ref, fg_ref, prev_ref, out_ref):
    del prev_ref
    _finish_block(x_ref, g2_ref, fg_ref, peer_ref, out_ref)


def _finish_tokens(peer, x1, g2, fg, prev, seq_len):
    d = x1.shape[1]
    tm = FINISH_TOK_BLOCK
    return pl.pallas_call(
        _finish_kernel,
        grid=(peer.shape[0] // tm,),
        in_specs=[pl.BlockSpec((tm, d), lambda i: (i, 0)),
                  pl.BlockSpec((tm, d), lambda i: (i, 0)),
                  pl.BlockSpec((1, 1, d), lambda i: (i * tm // seq_len, 0, 0)),
                  pl.BlockSpec((1, d), lambda i: (0, 0)),
                  pl.BlockSpec(memory_space=pl.ANY)],
        out_specs=pl.BlockSpec((tm, d), lambda i: (i, 0)),
        out_shape=jax.ShapeDtypeStruct(prev.shape, prev.dtype),
        input_output_aliases={4: 0},
        compiler_params=pltpu.CompilerParams(dimension_semantics=("arbitrary",)),
        name="finish_tokens",
    )(peer, x1, g2, fg, prev)


def _pack_kernel(u_ref, v_ref, o_ref):
    te, d = u_ref.shape
    n_chunks = d // LANES

    def pairs(x):
        r = lax.bitcast_convert_type(x.astype(jnp.bfloat16).astype(jnp.float32), jnp.uint32)
        return r[:, :d // 2] | (r[:, d // 2:] >> 16)

    words = jnp.concatenate([pairs(u_ref[...]), pairs(v_ref[...])], axis=1)
    for c in range(n_chunks):
        o_ref[pl.ds(c, te, stride=n_chunks), :] = words[:, c * LANES:(c + 1) * LANES]


def _pack_expert_table(expert_u, expert_v):
    n_experts, d = expert_u.shape
    te = PACK_EXPERT_BLOCK
    n_chunks = d // LANES
    packed = pl.pallas_call(
        _pack_kernel,
        grid=(n_experts // te,),
        in_specs=[pl.BlockSpec((te, d), lambda i: (i, 0)), pl.BlockSpec((te, d), lambda i: (i, 0))],
        out_specs=pl.BlockSpec((te * n_chunks, LANES), lambda i: (i, 0)),
        out_shape=jax.ShapeDtypeStruct((n_experts * n_chunks, LANES), jnp.uint32),
        name="pack_experts",
    )(expert_u, expert_v)
    return packed.reshape(n_experts, n_chunks, LANES)


def _route_experts_kernel(ya_ref, sga_ref, gb_ref, x_ref, mod_ref, ya_nx, sga_nx, gb_nx, x_nx, mod_nx,
                          n2g_ref, wa_ref, wo_ref, wq_ref, keys_ref, fg_ref, tab_ref, out_ref,
                          q_scr, x1_v, h2_v, idx_v, gates_v, idx_tv, gates_tv, idx_s, isem,
                          *expert_scratch, tok_block, n_slots):
    bufs = expert_scratch[:n_slots]
    peer, sem = expert_scratch[n_slots], expert_scratch[n_slots + 1]
    route_state = expert_scratch[n_slots + 2:]
    i = pl.program_id(0)
    par = lax.rem(i, 2)
    n_chunks = x_ref.shape[-1] // LANES
    rows_per_tok = PEER_SLOTS * n_chunks
    group = tok_block // PEER_HEADS
    assert group % n_slots == 0 and n_slots - 1 < group and ROUTE_STAGES <= group

    def publish(p):
        idx_tv[...] = idx_v[...].T
        gates_tv[p] = gates_v[...].T
        cp = pltpu.make_async_copy(idx_tv, idx_s, isem)
        cp.start()
        cp.wait()

    @pl.when(i == 0)
    def _():
        _post_dense(ya_ref, sga_ref, gb_ref, x_ref, mod_ref, n2g_ref, wa_ref, wo_ref, wq_ref,
                    x1_v.at[0], h2_v.at[0], q_scr)

        def heads(hg, carry):
            _route_heads([hg * ROUTE_HEADS + e for e in range(ROUTE_HEADS)], q_scr, keys_ref, idx_v, gates_v)
            return carry

        lax.fori_loop(0, PEER_HEADS // ROUTE_HEADS, heads, 0)
        publish(0)

    x1_cur, h2_cur, gates_cur = x1_v.at[par], h2_v.at[par], gates_tv.at[par]
    _post_dense(ya_nx, sga_nx, gb_nx, x_nx, mod_nx, n2g_ref, wa_ref, wo_ref, wq_ref,
                x1_v.at[1 - par], h2_v.at[1 - par], q_scr)

    def issue(t, s):
        for r in range(PEER_SLOTS):
            row = pl.multiple_of(idx_s[t, r] * n_chunks, n_chunks)
            pltpu.make_async_copy(tab_ref.at[pl.ds(row, n_chunks), :],
                                  bufs[s].at[pl.ds(r * n_chunks, n_chunks), :],
                                  sem.at[s]).start(priority=r % DMA_THREADS)

    chunk = lambda s: (lambda c: bufs[s][pl.ds(c, PEER_SLOTS, stride=n_chunks), :])
    ahead = n_slots - 1

    def step(t, k, prefetch):
        s = k % n_slots
        pltpu.make_async_copy(tab_ref.at[pl.ds(0, rows_per_tok), :], bufs[s], sem.at[s]).wait()
        if prefetch:
            issue(t + ahead, (s - 1) % n_slots)
        coef = _expert_coef(chunk(s), h2_cur[pl.ds(t, 1), :], gates_cur[pl.ds(t, 1), :])
        peer[pl.ds(t, 1), :] = _expert_mix(chunk(s), coef, n_chunks // 2)

    for t in range(ahead):
        issue(t, t)

    def super_group(h, last):
        for k in range(group):
            t = h * group + k
            step(t, k, (not last) or k + ahead < group)
            if k < ROUTE_STAGES:
                _route_stage(k, h, q_scr, keys_ref, idx_v, gates_v, *route_state)

    def body(h, carry):
        super_group(h, False)
        return carry

    lax.fori_loop(0, PEER_HEADS - 1, body, 0)
    super_group(PEER_HEADS - 1, True)

    _finish_block(x1_cur, mod_ref.at[:, 5:6, :], fg_ref, peer, out_ref)
    publish(1 - par)


def _tc_route_experts(ya, sga, gb, x, mod, norm2_g, w_a, w_out, w_query, sub_keys, final_g, table,
                      seq_len, first_tok):
    n, d = x.shape
    tb = EXPERT_TOK_BLOCK
    assert tb == POST_TOK_BLOCK and tb % EXPERT_SLOTS == 0 and seq_len % tb == 0 and first_tok % tb == 0
    first = first_tok // tb
    blocks_per_seq = seq_len // tb
    n_chunks = d // LANES
    aw = ya.shape[1]
    keys = sub_keys.reshape(2 * PEER_HEADS, N_KEYS, PEER_HALF)
    n_blocks = (n - first_tok) // tb
    nxt = lambda i: jnp.minimum(i + 1, n_blocks - 1) + first
    tok = lambda w: pl.BlockSpec((tb, w), lambda i: (i + first, 0))
    tok_nx = lambda w: pl.BlockSpec((tb, w), lambda i: (nxt(i), 0))
    mod_spec = lambda blk: pl.BlockSpec((1, 6, d), lambda i: (blk(i) // blocks_per_seq, 0, 0))
    full = lambda a: pl.BlockSpec(a.shape, lambda i: (0,) * a.ndim)
    fg = final_g.reshape(1, d)
    f32, i32 = jnp.float32, jnp.int32
    return pl.pallas_call(
        functools.partial(_route_experts_kernel, tok_block=tb, n_slots=EXPERT_SLOTS),
        grid=(n_blocks,),
        in_specs=[tok(aw), tok(d), tok(d), tok(d), mod_spec(lambda i: i + first),
                  tok_nx(aw), tok_nx(d), tok_nx(d), tok_nx(d), mod_spec(nxt),
                  full(norm2_g), full(w_a), full(w_out), full(w_query), full(keys), full(fg),
                  pl.BlockSpec(memory_space=pl.ANY)],
        out_specs=tok(d),
        out_shape=jax.ShapeDtypeStruct((n, d), f32),
        scratch_shapes=(
            [pltpu.VMEM((2 * PEER_HEADS, tb, PEER_HALF), f32), pltpu.VMEM((2, tb, d), f32),
             pltpu.VMEM((2, tb, d), f32), pltpu.VMEM((PEER_SLOTS, tb), i32), pltpu.VMEM((PEER_SLOTS, tb), f32),
             pltpu.VMEM((tb, PEER_SLOTS), i32), pltpu.VMEM((2, tb, PEER_SLOTS), f32),
             pltpu.SMEM((tb, PEER_SLOTS), i32), pltpu.SemaphoreType.DMA]
            + [pltpu.VMEM((PEER_SLOTS * n_chunks, LANES), jnp.uint32) for _ in range(EXPERT_SLOTS)]
            + [pltpu.VMEM((tb, d), f32), pltpu.SemaphoreType.DMA((EXPERT_SLOTS,))]
            + [pltpu.VMEM((2, N_KEYS, tb), f32), pltpu.VMEM((2, PEER_TOPK, tb), f32),
               pltpu.VMEM((2, PEER_TOPK, tb), f32),
               pltpu.VMEM((PEER_TOPK + (PEER_TOPK // 2 - 1) * (PEER_TOPK // 2) + PEER_TOPK // 2, tb), f32),
               pltpu.VMEM((2, PEER_TOPK, tb), f32)]),
        compiler_params=pltpu.CompilerParams(dimension_semantics=("arbitrary",),
                                             vmem_limit_bytes=56 * 1024 * 1024),
        cost_estimate=_expert_cost(n - first_tok, d),
        name="route_experts",
    )(ya, sga, gb, x, mod, ya, sga, gb, x, mod, norm2_g, w_a, w_out, w_query, keys, fg,
      table.reshape(-1, LANES))


def kernel(x, c, w_mod, b_mod, norm1_g, w_in, b_forget, ln_v_g, w_spatial, b_spatial, w_branch_a, w_branch_b, w_out, norm2_g, w_query, sub_keys, expert_u, expert_v, final_g):
    B, S, D = x.shape
    n = B * S
    bf16 = jnp.bfloat16
    assert w_mod.shape[0] == 1, "the final RMSNorm is fused into the single layer's expert kernels"
    l = 0
    mod = _modulation(c, w_mod[l], b_mod[l]).reshape(B, 6, D)
    table = _pack_expert_table(expert_u[l], expert_v[l])

    qkv, cum, cumt, sga, gb = _input_projection(x, mod, norm1_g[l], w_in[l], b_forget[l], ln_v_g[l],
                                                w_spatial[l], b_spatial[l], w_branch_b[l])
    y_a = _fox_attention(qkv, cum, cumt, B, S)
    post_args = (y_a, sga, gb, x.reshape(n, D), mod, norm2_g[l].reshape(1, D), w_branch_a[l].astype(bf16),
                 w_out[l].astype(bf16), w_query[l].astype(bf16), sub_keys[l])

    n_sc = n * SC_SHARE_PERCENT // 100 // SC_SHARE_ALIGN * SC_SHARE_ALIGN
    if n_sc > 0:
        x1_a, h2_a, idx_t, gates_t = _post_attention(*post_args, S, n_sc)
        peer_sc = _sc_peer_experts(table, idx_t.T.reshape(-1), gates_t.T.reshape(-1), h2_a)
    out = _tc_route_experts(*post_args, final_g, table, S, n_sc)
    if n_sc > 0:
        out = _finish_tokens(peer_sc.reshape(n_sc, D), x1_a, mod[:, 5:6, :], final_g.reshape(1, D), out, S)
    return out.reshape(B, S, D)
```

```python
import functools

import jax
import jax.numpy as jnp
from jax import lax
from jax.experimental import pallas as pl
from jax.experimental.pallas import tpu as pltpu
from jax.experimental.pallas import tpu_sc as plsc

D_MODEL = 1024
ATT_HEADS = 8
ATT_HEAD_DIM = 64
ATT_WIDTH = ATT_HEADS * ATT_HEAD_DIM
GM_GROUPS = 4
GM_GROUP_DIM = 128
GM_WIDTH = GM_GROUPS * GM_GROUP_DIM
GM_CHUNK = 128
PEER_HEADS = 8
PEER_KEY_DIM = 256
PEER_HALF = PEER_KEY_DIM // 2
N_KEYS = 128
PEER_TOPK = 16
PEER_SLOTS = PEER_HEADS * PEER_TOPK
SPLIT_POINTS = (ATT_WIDTH, 2 * ATT_WIDTH, 3 * ATT_WIDTH, 3 * ATT_WIDTH + ATT_HEADS,
                3 * ATT_WIDTH + ATT_HEADS + 2 * GM_WIDTH,
                3 * ATT_WIDTH + ATT_HEADS + 2 * GM_WIDTH + D_MODEL)
EPS = 1e-6

LANES = 128
EXPERT_TOK_BLOCK = 256
EXPERT_SLOTS = 8
DMA_THREADS = 2
FINISH_TOK_BLOCK = 256
PACK_EXPERT_BLOCK = 256
SC_GATHER_ROWS = 32
SC_ROW_GROUP = 4
SC_STORE_BATCH = 8
SC_SHARE_PERCENT = 52
SC_SHARE_ALIGN = 256


def _gelu(x):
    return 0.5 * x * (1.0 + jnp.tanh(0.7978845608028654 * (x + 0.044715 * (x * x * x))))


def _mod_kernel(c_ref, w_ref, b_ref, o_ref):
    c = c_ref[...]
    sc = c * jax.nn.sigmoid(c)
    o_ref[...] = jnp.dot(sc, w_ref[...], precision=lax.Precision.HIGHEST,
                         preferred_element_type=jnp.float32) + b_ref[...]


def _modulation(c, w_mod, b_mod):
    b, d = c.shape
    cols = w_mod.shape[1]
    return pl.pallas_call(
        _mod_kernel,
        grid=(cols // d,),
        in_specs=[pl.BlockSpec((b, d), lambda j: (0, 0)),
                  pl.BlockSpec((d, d), lambda j: (0, j)),
                  pl.BlockSpec((1, d), lambda j: (0, j))],
        out_specs=pl.BlockSpec((b, d), lambda j: (0, j)),
        out_shape=jax.ShapeDtypeStruct((b, cols), jnp.float32),
        name="modulation",
    )(c, w_mod, b_mod.reshape(1, cols))


INPROJ_TOK_BLOCK = 256


def _inproj_kernel(x_ref, mod_ref, n1g_ref, wqkv_ref, wf_ref, bf_ref, wz_ref, wg_ref, lng_ref,
                   wsp_ref, bsp_ref, wb_ref, qkv_ref, cum_ref, cumt_ref, sga_ref, gb_ref, carry):
    f32, bf16 = jnp.float32, jnp.bfloat16
    tm, d = x_ref.shape[1], x_ref.shape[2]
    x = x_ref[0]
    sh1 = mod_ref[0, 0:1, :]
    sc1 = mod_ref[0, 1:2, :]
    h = x * lax.rsqrt(jnp.mean(x * x, axis=-1, keepdims=True) + EPS) * n1g_ref[...]
    hb = (h * (1.0 + sc1) + sh1).astype(bf16)

    qkv = jnp.dot(hb, wqkv_ref[...], preferred_element_type=f32)
    qkv_ref[:, 0:ATT_WIDTH] = (qkv[:, 0:ATT_WIDTH] * (ATT_HEAD_DIM ** -0.5)).astype(bf16)
    qkv_ref[:, ATT_WIDTH:] = qkv[:, ATT_WIDTH:].astype(bf16)

    f = jnp.dot(hb, wf_ref[...], preferred_element_type=f32) + bf_ref[...]
    logf = jnp.minimum(f, 0.0) - jnp.log1p(jnp.exp(-jnp.abs(f)))

    @pl.when(pl.program_id(1) == 0)
    def _():
        carry[...] = jnp.zeros_like(carry)

    tri = (lax.broadcasted_iota(jnp.int32, (tm, tm), 0)
           >= lax.broadcasted_iota(jnp.int32, (tm, tm), 1)).astype(f32)
    cum = jnp.dot(tri, logf, precision=lax.Precision.HIGHEST, preferred_element_type=f32) + carry[...]
    cum_ref[...] = cum
    cumt_ref[0, 0] = jnp.transpose(cum)[0:ATT_HEADS, :]
    carry[...] = cum[tm - 1:tm, :]

    gz = _gelu(jnp.dot(hb, wz_ref[...], preferred_element_type=f32))
    u = gz[:, 0:GM_WIDTH]
    v = gz[:, GM_WIDTH:]
    mu = jnp.mean(v, axis=-1, keepdims=True)
    var = jnp.mean(jnp.square(v - mu), axis=-1, keepdims=True)
    vn = ((v - mu) * lax.rsqrt(var + EPS) * lng_ref[...]).astype(bf16)
    tril = (lax.broadcasted_iota(jnp.int32, (GM_CHUNK, GM_CHUNK), 0)
            >= lax.broadcasted_iota(jnp.int32, (GM_CHUNK, GM_CHUNK), 1))
    w_sp = [jnp.where(tril, wsp_ref[g], 0.0).astype(bf16) for g in range(GM_GROUPS)]
    rows = []
    for ck in range(tm // GM_CHUNK):
        r0 = ck * GM_CHUNK
        cols = []
        for g in range(GM_GROUPS):
            c0 = g * GM_GROUP_DIM
            mixed = jnp.dot(w_sp[g], vn[r0:r0 + GM_CHUNK, c0:c0 + GM_GROUP_DIM],
                            preferred_element_type=f32) + bsp_ref[g]
            cols.append(u[r0:r0 + GM_CHUNK, c0:c0 + GM_GROUP_DIM] * mixed)
        rows.append(jnp.concatenate(cols, axis=1))
    yb = jnp.concatenate(rows, axis=0).astype(bf16)
    ybp = jnp.dot(yb, wb_ref[...], preferred_element_type=f32)

    sg = jax.nn.sigmoid(jnp.dot(hb, wg_ref[...], preferred_element_type=f32))
    sga_ref[...] = sg[:, 0:d].astype(bf16)
    gb_ref[...] = (sg[:, d:] * ybp).astype(bf16)


def _input_projection(x, mod, norm1_g, w_in, b_forget, ln_v_g, w_spatial, b_spatial, w_branch_b):
    B, S, d = x.shape
    n = B * S
    tm = INPROJ_TOK_BLOCK
    bf16 = jnp.bfloat16
    p0, p1, p2, p3, p4, p5 = SPLIT_POINTS
    w_qkv = w_in[:, 0:p2].astype(bf16)
    w_f = jnp.pad(w_in[:, p2:p3], ((0, 0), (0, LANES - ATT_HEADS))).astype(bf16)
    b_f = jnp.pad(b_forget, (0, LANES - ATT_HEADS)).reshape(1, LANES)
    w_z = w_in[:, p3:p4].astype(bf16)
    w_g = w_in[:, p4:].astype(bf16)
    nt = S // tm
    tok = lambda w: pl.BlockSpec((tm, w), lambda b, i: (b * nt + i, 0))
    full = lambda a: pl.BlockSpec(a.shape, lambda b, i: (0,) * a.ndim)
    args = (x, mod, norm1_g.reshape(1, d), w_qkv, w_f, b_f, w_z, w_g, ln_v_g.reshape(1, GM_WIDTH),
            w_spatial, b_spatial.reshape(GM_GROUPS, GM_CHUNK, 1), w_branch_b.astype(bf16))
    return pl.pallas_call(
        _inproj_kernel,
        grid=(B, nt),
        in_specs=[pl.BlockSpec((1, tm, d), lambda b, i: (b, i, 0)),
                  pl.BlockSpec((1, 6, d), lambda b, i: (b, 0, 0))] + [full(a) for a in args[2:]],
        out_specs=[tok(3 * ATT_WIDTH), tok(LANES),
                   pl.BlockSpec((1, 1, ATT_HEADS, tm), lambda b, i: (b, i, 0, 0)), tok(d), tok(d)],
        out_shape=[jax.ShapeDtypeStruct((n, 3 * ATT_WIDTH), bf16),
                   jax.ShapeDtypeStruct((n, LANES), jnp.float32),
                   jax.ShapeDtypeStruct((B, nt, ATT_HEADS, tm), jnp.float32),
                   jax.ShapeDtypeStruct((n, d), bf16),
                   jax.ShapeDtypeStruct((n, d), bf16)],
        scratch_shapes=[pltpu.VMEM((1, LANES), jnp.float32)],
        compiler_params=pltpu.CompilerParams(dimension_semantics=("arbitrary", "arbitrary"),
                                             vmem_limit_bytes=56 * 1024 * 1024),
        name="input_projection",
    )(*args)


ATT_BLOCK = 1024


def _fox_kernel(q_ref, k_ref, v_ref, cum_ref, cumt_ref, o_ref, *, blk, ratio):
    f32 = jnp.float32
    hp = pl.program_id(1)
    i = pl.program_id(2)
    dh = ATT_HEAD_DIM
    q2 = q_ref[...]
    first = lax.broadcasted_iota(jnp.int32, (1, 2 * dh), 1) < dh
    qs = (jnp.where(first, q2, jnp.zeros_like(q2)), jnp.where(first, jnp.zeros_like(q2), q2))
    cum_blk = cum_ref[...]
    lane = lax.broadcasted_iota(jnp.int32, cum_blk.shape, 1)
    cqs = [jnp.sum(jnp.where(lane == 2 * hp + e, cum_blk, 0.0), axis=1, keepdims=True)
           for e in range(2)]

    def block(j, carry, masked):
        off = pl.multiple_of(j * blk, blk)
        k2 = k_ref[0, pl.ds(off, blk), :]
        v2 = v_ref[0, pl.ds(off, blk), :]
        out = []
        for e in range(2):
            m, acc = carry[e]
            ck = jnp.concatenate([cumt_ref[0, j * ratio + a, pl.ds(2 * hp + e, 1), :]
                                  for a in range(ratio)], axis=1)
            s = lax.dot_general(qs[e], k2, (((1,), (1,)), ((), ())), preferred_element_type=f32)
            s = s + (cqs[e] - ck)
            if masked:
                causal = (lax.broadcasted_iota(jnp.int32, (blk, blk), 0)
                          >= lax.broadcasted_iota(jnp.int32, (blk, blk), 1))
                s = jnp.where(causal, s, -jnp.inf)
            m_new = jnp.maximum(m, jnp.max(s, axis=1, keepdims=True))
            alpha = jnp.exp(m - m_new)
            p = jnp.exp(s - m_new).astype(v2.dtype)
            own = first if e == 0 else jnp.logical_not(first)
            v_ones = jnp.where(own, v2, jnp.ones_like(v2))
            acc = alpha * acc + jnp.dot(p, v_ones, preferred_element_type=f32)
            out.append((m_new, acc))
        return tuple(out)

    one = (jnp.full((blk, 1), -1e30, f32), jnp.zeros((blk, 2 * dh), f32))
    carry = lax.fori_loop(0, i, lambda j, c: block(j, c, False), (one, one))
    (_, acc0), (_, acc1) = block(i, carry, True)
    ratio0 = acc0 / pltpu.roll(acc0, dh, axis=1)
    ratio1 = acc1 / pltpu.roll(acc1, dh, axis=1)
    o_ref[...] = jnp.where(first, ratio0, ratio1).astype(o_ref.dtype)


def _fox_attention(qkv, cum, cumt, batch, seq_len):
    n = qkv.shape[0]
    tm = cumt.shape[-1]
    blk = max(min(ATT_BLOCK, seq_len), tm)
    nb = seq_len // blk
    pair = 2 * ATT_HEAD_DIM
    n_pairs = ATT_WIDTH // pair
    qkv3 = qkv.reshape(batch, seq_len, 3 * ATT_WIDTH)
    return pl.pallas_call(
        functools.partial(_fox_kernel, blk=blk, ratio=blk // tm),
        grid=(batch, n_pairs, nb),
        in_specs=[pl.BlockSpec((blk, pair), lambda b, hp, i: (b * nb + i, hp)),
                  pl.BlockSpec((1, seq_len, pair), lambda b, hp, i: (b, 0, n_pairs + hp)),
                  pl.BlockSpec((1, seq_len, pair), lambda b, hp, i: (b, 0, 2 * n_pairs + hp)),
                  pl.BlockSpec((blk, LANES), lambda b, hp, i: (b * nb + i, 0)),
                  pl.BlockSpec((1,) + cumt.shape[1:], lambda b, hp, i: (b, 0, 0, 0))],
        out_specs=pl.BlockSpec((blk, pair), lambda b, hp, i: (b * nb + i, hp)),
        out_shape=jax.ShapeDtypeStruct((n, ATT_WIDTH), jnp.bfloat16),
        compiler_params=pltpu.CompilerParams(
            dimension_semantics=("arbitrary", "arbitrary", "arbitrary")),
        name="fox_attention",
    )(qkv, qkv3, qkv3, cum, cumt)


def _topk_rows(scores, k):
    rows, t = scores[0].shape
    iota = lax.broadcasted_iota(jnp.int32, (rows, t), 0).astype(jnp.float32)
    slot = lax.broadcasted_iota(jnp.int32, (k, t), 0)
    scores = list(scores)
    vals = [jnp.zeros((k, t), jnp.float32) for _ in scores]
    ids = [jnp.zeros((k, t), jnp.float32) for _ in scores]
    for j in range(k):
        for i, s in enumerate(scores):
            m = jnp.max(s, axis=0, keepdims=True)
            am = jnp.min(jnp.where(s == m, iota, float(rows)), axis=0, keepdims=True)
            vals[i] = jnp.where(slot == j, m, vals[i])
            ids[i] = jnp.where(slot == j, am, ids[i])
            scores[i] = jnp.where(iota == am, -jnp.inf, s)
    return [(v, i.astype(jnp.int32)) for v, i in zip(vals, ids)]


def _select_rows(table, pos):
    out = jnp.zeros(pos.shape, table.dtype)
    for r in range(table.shape[0]):
        out = jnp.where(pos == r, table[r:r + 1, :], out)
    return out


def _post_dense(ya_ref, sga_ref, gb_ref, x_ref, mod_ref, n2g_ref, wa_ref, wo_ref, wq_ref,
                x1_ref, h2_ref, q_scr):
    f32 = jnp.float32
    a = jnp.dot(ya_ref[...], wa_ref[...], preferred_element_type=f32)
    merged = sga_ref[...].astype(f32) * a + gb_ref[...].astype(f32)
    o = jnp.dot(merged.astype(jnp.bfloat16), wo_ref[...], preferred_element_type=f32)
    g1 = mod_ref[0, 2:3, :]
    sh2 = mod_ref[0, 3:4, :]
    sc2 = mod_ref[0, 4:5, :]
    x1 = x_ref[...] + g1 * o
    x1_ref[...] = x1
    h2 = x1 * lax.rsqrt(jnp.mean(x1 * x1, axis=-1, keepdims=True) + EPS) * n2g_ref[...]
    h2 = h2 * (1.0 + sc2) + sh2
    h2_ref[...] = h2
    qp = jnp.dot(h2.astype(jnp.bfloat16), wq_ref[...], preferred_element_type=f32)
    for j in range(2 * PEER_HEADS):
        q_scr[j] = qp[:, j * PEER_HALF:(j + 1) * PEER_HALF]


def _route_heads(hs, q_scr, keys_ref, idx_ref, gates_ref):
    f32 = jnp.float32
    half = PEER_TOPK // 2
    tail0 = PEER_TOPK + (half - 1) * half
    scores = []
    for h in hs:
        for p in range(2):
            q = q_scr[2 * h + p]
            keys = keys_ref[2 * h + p]
            scores.append(lax.dot_general(keys, q, (((1,), (1,)), ((), ())),
                                          precision=lax.Precision.HIGHEST,
                                          preferred_element_type=f32))
    tops = _topk_rows(scores, PEER_TOPK)
    cands = []
    for e in range(len(hs)):
        (s1, _), (s2, _) = tops[2 * e], tops[2 * e + 1]
        blocks = [s1[0:1, :] + s2]
        blocks += [s1[a:a + 1, :] + s2[0:half, :] for a in range(1, half)]
        blocks += [s1[half:, :] + s2[0:1, :]]
        cands.append(jnp.concatenate(blocks, axis=0))
    best = _topk_rows(cands, PEER_TOPK)
    for e, h in enumerate(hs):
        (_, i1), (_, i2) = tops[2 * e], tops[2 * e + 1]
        vals, pos = best[e]
        mid = pos - PEER_TOPK
        ra = jnp.where(pos < PEER_TOPK, 0,
                       jnp.where(pos < tail0, 1 + (mid >> (half.bit_length() - 1)), pos - tail0 + half))
        rb = jnp.where(pos < PEER_TOPK, pos, jnp.where(pos < tail0, mid & (half - 1), 0))
        eid = _select_rows(i1, ra) * N_KEYS + _select_rows(i2, rb)
        ex = jnp.exp(vals - vals[0:1, :])
        g = ex / jnp.sum(ex, axis=0, keepdims=True)
        row = pl.multiple_of(h * PEER_TOPK, PEER_TOPK)
        idx_ref[pl.ds(row, PEER_TOPK), :] = eid
        gates_ref[pl.ds(row, PEER_TOPK), :] = g


ROUTE_STAGES = 27


def _route_stage(k, h, q_scr, keys_ref, idx_ref, gates_ref, rs_s, rs_v, rs_i, rs_c, rs_b):
    f32 = jnp.float32
    half = PEER_TOPK // 2
    tail0 = PEER_TOPK + (half - 1) * half

    def extract(s_ref, v_ref, i_ref, j):
        s = s_ref[...]
        rows = s.shape[0]
        iota = lax.broadcasted_iota(jnp.int32, s.shape, 0).astype(f32)
        m = jnp.max(s, axis=0, keepdims=True)
        am = jnp.min(jnp.where(s == m, iota, float(rows)), axis=0, keepdims=True)
        v_ref[pl.ds(j, 1), :] = m
        i_ref[pl.ds(j, 1), :] = am
        s_ref[...] = jnp.where(iota == am, -jnp.inf, s)

    if k == 0:
        for p in range(2):
            rs_s[p] = lax.dot_general(keys_ref[2 * h + p], q_scr[2 * h + p], (((1,), (1,)), ((), ())),
                                      precision=lax.Precision.HIGHEST, preferred_element_type=f32)
    elif k <= PEER_TOPK:
        for p in range(2):
            extract(rs_s.at[p], rs_v.at[p], rs_i.at[p], k - 1)
    elif k == PEER_TOPK + 1:
        s1, s2 = rs_v[0], rs_v[1]
        blocks = [s1[0:1, :] + s2]
        blocks += [s1[a:a + 1, :] + s2[0:half, :] for a in range(1, half)]
        blocks += [s1[half:, :] + s2[0:1, :]]
        rs_c[...] = jnp.concatenate(blocks, axis=0)
    elif k <= PEER_TOPK + 1 + half:
        j = 2 * (k - PEER_TOPK - 2)
        extract(rs_c, rs_b.at[0], rs_b.at[1], j)
        extract(rs_c, rs_b.at[0], rs_b.at[1], j + 1)
    elif k == ROUTE_STAGES - 1:
        i1, i2 = rs_i[0].astype(jnp.int32), rs_i[1].astype(jnp.int32)
        vals, pos = rs_b[0], rs_b[1].astype(jnp.int32)
        mid = pos - PEER_TOPK
        ra = jnp.where(pos < PEER_TOPK, 0,
                       jnp.where(pos < tail0, 1 + (mid >> (half.bit_length() - 1)), pos - tail0 + half))
        rb = jnp.where(pos < PEER_TOPK, pos, jnp.where(pos < tail0, mid & (half - 1), 0))
        eid = _select_rows(i1, ra) * N_KEYS + _select_rows(i2, rb)
        ex = jnp.exp(vals - vals[0:1, :])
        g = ex / jnp.sum(ex, axis=0, keepdims=True)
        row = pl.multiple_of(h * PEER_TOPK, PEER_TOPK)
        idx_ref[pl.ds(row, PEER_TOPK), :] = eid
        gates_ref[pl.ds(row, PEER_TOPK), :] = g


def _post_kernel(ya_ref, sga_ref, gb_ref, x_ref, mod_ref, n2g_ref, wa_ref, wo_ref, wq_ref, keys_ref,
                 x1_ref, h2_ref, idx_ref, gates_ref, q_scr):
    _post_dense(ya_ref, sga_ref, gb_ref, x_ref, mod_ref, n2g_ref, wa_ref, wo_ref, wq_ref,
                x1_ref, h2_ref, q_scr)

    def heads(hg, carry):
        _route_heads([hg * ROUTE_HEADS + e for e in range(ROUTE_HEADS)], q_scr, keys_ref, idx_ref, gates_ref)
        return carry

    lax.fori_loop(0, PEER_HEADS // ROUTE_HEADS, heads, 0)


POST_TOK_BLOCK = 256
ROUTE_HEADS = 4


def _post_attention(ya, sga, gb, x, mod, norm2_g, w_a, w_out, w_query, sub_keys, seq_len, n):
    d = x.shape[1]
    tm = POST_TOK_BLOCK
    assert n % tm == 0
    blocks_per_seq = seq_len // tm
    aw = ya.shape[1]
    keys = sub_keys.reshape(2 * PEER_HEADS, N_KEYS, PEER_HALF)
    tok = lambda w: pl.BlockSpec((tm, w), lambda i: (i, 0))
    full = lambda a: pl.BlockSpec(a.shape, lambda i: (0,) * a.ndim)
    return pl.pallas_call(
        _post_kernel,
        grid=(n // tm,),
        in_specs=[tok(aw), tok(d), tok(d), tok(d),
                  pl.BlockSpec((1, 6, d), lambda i: (i // blocks_per_seq, 0, 0)),
                  full(norm2_g), full(w_a), full(w_out), full(w_query), full(keys)],
        out_specs=[tok(d), tok(d),
                   pl.BlockSpec((PEER_SLOTS, tm), lambda i: (0, i)),
                   pl.BlockSpec((PEER_SLOTS, tm), lambda i: (0, i))],
        out_shape=[jax.ShapeDtypeStruct((n, d), jnp.float32),
                   jax.ShapeDtypeStruct((n, d), jnp.float32),
                   jax.ShapeDtypeStruct((PEER_SLOTS, n), jnp.int32),
                   jax.ShapeDtypeStruct((PEER_SLOTS, n), jnp.float32)],
        scratch_shapes=[pltpu.VMEM((2 * PEER_HEADS, tm, PEER_HALF), jnp.float32)],
        compiler_params=pltpu.CompilerParams(dimension_semantics=("arbitrary",),
                                             vmem_limit_bytes=48 * 1024 * 1024),
        name="post_attention",
    )(ya, sga, gb, x, mod, norm2_g, w_a, w_out, w_query, keys)


def _rowsum_bcast(p, ones_bf16):
    hi = p.astype(jnp.bfloat16)
    lo = (p - hi.astype(jnp.float32)).astype(jnp.bfloat16)
    return (jnp.dot(hi, ones_bf16, preferred_element_type=jnp.float32)
            + jnp.dot(lo, ones_bf16, preferred_element_type=jnp.float32))


def _word_halves(w):
    return (lax.bitcast_convert_type(w & jnp.uint32(0xFFFF0000), jnp.float32),
            lax.bitcast_convert_type(w << 16, jnp.float32))


def _expert_coef(chunk, hrow, grow):
    half = hrow.shape[1] // LANES // 2
    ones_bf16 = jnp.ones((LANES, LANES), jnp.bfloat16)
    eye = (lax.broadcasted_iota(jnp.int32, (PEER_SLOTS, LANES), 0)
           == lax.broadcasted_iota(jnp.int32, (PEER_SLOTS, LANES), 1))
    hpart = lambda c: hrow[:, c * LANES:(c + 1) * LANES]
    psum = None
    for c in range(half):
        hi, lo = _word_halves(chunk(c))
        p = hi * hpart(c) + lo * hpart(c + half)
        psum = p if psum is None else psum + p
    act = _gelu(_rowsum_bcast(psum, ones_bf16))
    gcol = _rowsum_bcast(jnp.where(eye, grow, 0.0), ones_bf16)
    return gcol * act


def _expert_mix(chunk, coef, half):
    outs_hi, outs_lo = [], []
    for c in range(half):
        hi, lo = _word_halves(chunk(c + half))
        outs_hi.append(jnp.sum(coef * hi, axis=0, keepdims=True))
        outs_lo.append(jnp.sum(coef * lo, axis=0, keepdims=True))
    return jnp.concatenate(outs_hi + outs_lo, axis=-1)


def _finish_block(x_ref, g2_ref, fg_ref, peer, out_ref):
    y = x_ref[...] + g2_ref[0] * peer[...]
    out_ref[...] = y * lax.rsqrt(jnp.mean(y * y, axis=-1, keepdims=True) + EPS) * fg_ref[...]


def _expert_cost(n_tok, d):
    pairs = n_tok * PEER_SLOTS
    return pl.CostEstimate(flops=4 * pairs * d, transcendentals=pairs,
                           bytes_accessed=4 * pairs * d + 12 * n_tok * d + 8 * pairs)


def _sc_tanh(y):
    return 1.0 - 2.0 / (jnp.exp(2.0 * y) + 1.0)


def _sc_peer_experts(table3, ids, gates, h2):
    n_tok = ids.shape[0] // PEER_SLOTS
    _, n_chunks, lanes = table3.shape
    assert h2.shape[1] == n_chunks * lanes
    info = plsc.get_sparse_core_info()
    sl = info.num_lanes
    n_workers = info.num_cores * info.num_subcores
    tok_per_worker = n_tok // n_workers
    assert tok_per_worker * n_workers == n_tok and tok_per_worker % 2 == 0
    rows = SC_GATHER_ROWS
    n_sub = PEER_SLOTS // rows
    assert n_sub % 2 == 0
    half = n_chunks // 2
    pieces = [(c, k * sl) for c in range(half) for k in range(lanes // sl)]
    mesh = plsc.VectorSubcoreMesh(core_axis_name="c", subcore_axis_name="s")
    buf = lambda dt: pltpu.VMEM((rows, n_chunks, lanes), dt)

    @functools.partial(
        pl.kernel, mesh=mesh,
        out_type=jax.ShapeDtypeStruct((n_tok, n_chunks, lanes), jnp.float32),
        scratch_types=[pltpu.VMEM((PEER_SLOTS,), jnp.int32), pltpu.VMEM((PEER_SLOTS,), jnp.int32),
                       pltpu.VMEM((PEER_SLOTS,), jnp.float32), pltpu.VMEM((PEER_SLOTS,), jnp.float32),
                       pltpu.VMEM((n_chunks * lanes,), jnp.float32),
                       pltpu.VMEM((n_chunks * lanes,), jnp.float32),
                       buf(jnp.uint32), buf(jnp.uint32), pltpu.VMEM((n_chunks, lanes), jnp.float32),
                       pltpu.SemaphoreType.DMA((2,)), pltpu.SemaphoreType.DMA((2,))],
        compiler_params=pltpu.CompilerParams(needs_layout_passes=False),
        cost_estimate=_expert_cost(n_tok, n_chunks * lanes),
        name="sc_peer_experts")
    def run(table_hbm, ids_hbm, gates_hbm, h_hbm, peer_hbm, ids0, ids1, g0, g1, h0, h1, rows0, rows1,
            out_v, gsem, isem):
        bufs = (rows0, rows1)
        ids_p, g_p, h_p = (ids0, ids1), (g0, g1), (h0, h1)
        t_first = (lax.axis_index("s") * info.num_cores + lax.axis_index("c")) * tok_per_worker

        def token_inputs(t, p):
            off = pl.multiple_of(t * PEER_SLOTS, PEER_SLOTS)
            return (pltpu.make_async_copy(ids_hbm.at[pl.ds(off, PEER_SLOTS)], ids_p[p], isem.at[p]),
                    pltpu.make_async_copy(gates_hbm.at[pl.ds(off, PEER_SLOTS)], g_p[p], isem.at[p]),
                    pltpu.make_async_copy(h_hbm.at[t], h_p[p], isem.at[p]))

        def gather(p, s):
            return pltpu.make_async_copy(table_hbm.at[ids_p[p].at[pl.ds(s * rows, rows)]],
                                         bufs[s % 2], gsem.at[s % 2])

        for cp in token_inputs(t_first, 0):
            cp.start()
        for cp in token_inputs(t_first, 0):
            cp.wait()
        gather(0, 0).start()

        def one_token(ti, p):
            t = t_first + ti
            g_v, h_v = g_p[p], h_p[p]
            has_next = ti + 1 < tok_per_worker

            @pl.when(has_next)
            def _():
                for cp in token_inputs(t + 1, 1 - p):
                    cp.start()

            for c, k in pieces:
                out_v[c, pl.ds(k, sl)] = jnp.zeros((sl,), jnp.float32)
                out_v[c + half, pl.ds(k, sl)] = jnp.zeros((sl,), jnp.float32)
            for s in range(n_sub):
                gather(p, s).wait()
                if s + 1 < n_sub:
                    gather(p, s + 1).start()
                else:
                    @pl.when(has_next)
                    def _():
                        for cp in token_inputs(t + 1, 1 - p):
                            cp.wait()
                        gather(1 - p, 0).start()
                rbuf = bufs[s % 2]

                @pl.loop(0, rows, step=SC_ROW_GROUP)
                def _(r0):
                    accs = [jnp.zeros((sl,), jnp.float32) for _ in range(SC_ROW_GROUP)]
                    for c, k in pieces:
                        h_hi = h_v[pl.ds(c * lanes + k, sl)]
                        h_lo = h_v[pl.ds((c + half) * lanes + k, sl)]
                        for j in range(SC_ROW_GROUP):
                            hi, lo = _word_halves(rbuf[r0 + j, c, pl.ds(k, sl)])
                            accs[j] = accs[j] + hi * h_hi + lo * h_lo
                    coefs = []
                    for j in range(SC_ROW_GROUP):
                        a = jnp.broadcast_to(jnp.sum(accs[j]), (sl,))
                        act = 0.5 * a * (1.0 + _sc_tanh(0.7978845608028654 * (a + 0.044715 * (a * a * a))))
                        gate = plsc.load_gather(g_v, [jnp.broadcast_to(s * rows + r0 + j, (sl,))])
                        coefs.append(gate * act)
                    for b0 in range(0, len(pieces), SC_STORE_BATCH):
                        tots = []
                        for c, k in pieces[b0:b0 + SC_STORE_BATCH]:
                            t_hi = t_lo = None
                            for j in range(SC_ROW_GROUP):
                                hi, lo = _word_halves(rbuf[r0 + j, c + half, pl.ds(k, sl)])
                                t_hi = coefs[j] * hi if t_hi is None else t_hi + coefs[j] * hi
                                t_lo = coefs[j] * lo if t_lo is None else t_lo + coefs[j] * lo
                            tots.append((t_hi, t_lo))
                        for (c, k), (t_hi, t_lo) in zip(pieces[b0:b0 + SC_STORE_BATCH], tots):
                            plsc.addupdate(out_v.at[c, pl.ds(k, sl)], t_hi)
                            plsc.addupdate(out_v.at[c + half, pl.ds(k, sl)], t_lo)

            pltpu.sync_copy(out_v, peer_hbm.at[t])

        @pl.loop(0, tok_per_worker, step=2)
        def _(ti):
            one_token(ti, 0)
            one_token(ti + 1, 1)

    return run(table3, ids, gates, h2)


def _finish_kernel(peer_ref, x_ref, g2_ref, fg_ref, prev_ref, out_ref):
    del prev_ref
    _finish_block(x_ref, g2_ref, fg_ref, peer_ref, out_ref)


def _finish_tokens(peer, x1, g2, fg, prev, seq_len):
    d = x1.shape[1]
    tm = FINISH_TOK_BLOCK
    return pl.pallas_call(
        _finish_kernel,
        grid=(peer.shape[0] // tm,),
        in_specs=[pl.BlockSpec((tm, d), lambda i: (i, 0)),
                  pl.BlockSpec((tm, d), lambda i: (i, 0)),
                  pl.BlockSpec((1, 1, d), lambda i: (i * tm // seq_len, 0, 0)),
                  pl.BlockSpec((1, d), lambda i: (0, 0)),
                  pl.BlockSpec(memory_space=pl.ANY)],
        out_specs=pl.BlockSpec((tm, d), lambda i: (i, 0)),
        out_shape=jax.ShapeDtypeStruct(prev.shape, prev.dtype),
        input_output_aliases={4: 0},
        compiler_params=pltpu.CompilerParams(dimension_semantics=("arbitrary",)),
        name="finish_tokens",
    )(peer, x1, g2, fg, prev)


def _pack_kernel(u_ref, v_ref, o_ref):
    te, d = u_ref.shape
    n_chunks = d // LANES

    def pairs(x):
        r = lax.bitcast_convert_type(x.astype(jnp.bfloat16).astype(jnp.float32), jnp.uint32)
        return r[:, :d // 2] | (r[:, d // 2:] >> 16)

    words = jnp.concatenate([pairs(u_ref[...]), pairs(v_ref[...])], axis=1)
    for c in range(n_chunks):
        o_ref[pl.ds(c, te, stride=n_chunks), :] = words[:, c * LANES:(c + 1) * LANES]


def _pack_expert_table(expert_u, expert_v):
    n_experts, d = expert_u.shape
    te = PACK_EXPERT_BLOCK
    n_chunks = d // LANES
    packed = pl.pallas_call(
        _pack_kernel,
        grid=(n_experts // te,),
        in_specs=[pl.BlockSpec((te, d), lambda i: (i, 0)), pl.BlockSpec((te, d), lambda i: (i, 0))],
        out_specs=pl.BlockSpec((te * n_chunks, LANES), lambda i: (i, 0)),
        out_shape=jax.ShapeDtypeStruct((n_experts * n_chunks, LANES), jnp.uint32),
        name="pack_experts",
    )(expert_u, expert_v)
    return packed.reshape(n_experts, n_chunks, LANES)


def _route_experts_kernel(ya_ref, sga_ref, gb_ref, x_ref, mod_ref, ya_nx, sga_nx, gb_nx, x_nx, mod_nx,
                          n2g_ref, wa_ref, wo_ref, wq_ref, keys_ref, fg_ref, tab_ref, out_ref,
                          q_scr, x1_v, h2_v, idx_v, gates_v, idx_tv, gates_tv, idx_s, isem,
                          *expert_scratch, tok_block, n_slots):
    bufs = expert_scratch[:n_slots]
    peer, sem = expert_scratch[n_slots], expert_scratch[n_slots + 1]
    route_state = expert_scratch[n_slots + 2:]
    i = pl.program_id(0)
    par = lax.rem(i, 2)
    n_chunks = x_ref.shape[-1] // LANES
    rows_per_tok = PEER_SLOTS * n_chunks
    group = tok_block // PEER_HEADS
    assert group % n_slots == 0 and n_slots - 1 < group and ROUTE_STAGES <= group

    def publish(p):
        idx_tv[...] = idx_v[...].T
        gates_tv[p] = gates_v[...].T
        cp = pltpu.make_async_copy(idx_tv, idx_s, isem)
        cp.start()
        cp.wait()

    def issue(t, s):
        for r in range(PEER_SLOTS):
            row = pl.multiple_of(idx_s[t, r] * n_chunks, n_chunks)
            pltpu.make_async_copy(tab_ref.at[pl.ds(row, n_chunks), :],
                                  bufs[s].at[pl.ds(r * n_chunks, n_chunks), :],
                                  sem.at[s]).start(priority=r % DMA_THREADS)

    ahead = n_slots - 1

    def prime():
        for t in range(ahead):
            issue(t, t)

    @pl.when(i == 0)
    def _():
        _post_dense(ya_ref, sga_ref, gb_ref, x_ref, mod_ref, n2g_ref, wa_ref, wo_ref, wq_ref,
                    x1_v.at[0], h2_v.at[0], q_scr)

        def heads(hg, carry):
            _route_heads([hg * ROUTE_HEADS + e for e in range(ROUTE_HEADS)], q_scr, keys_ref, idx_v, gates_v)
            return carry

        lax.fori_loop(0, PEER_HEADS // ROUTE_HEADS, heads, 0)
        publish(0)
        prime()

    x1_cur, h2_cur, gates_cur = x1_v.at[par], h2_v.at[par], gates_tv.at[par]
    _post_dense(ya_nx, sga_nx, gb_nx, x_nx, mod_nx, n2g_ref, wa_ref, wo_ref, wq_ref,
                x1_v.at[1 - par], h2_v.at[1 - par], q_scr)

    chunk = lambda s: (lambda c: bufs[s][pl.ds(c, PEER_SLOTS, stride=n_chunks), :])

    def step(t, k, prefetch):
        s = k % n_slots
        pltpu.make_async_copy(tab_ref.at[pl.ds(0, rows_per_tok), :], bufs[s], sem.at[s]).wait()
        if prefetch:
            issue(t + ahead, (s - 1) % n_slots)
        coef = _expert_coef(chunk(s), h2_cur[pl.ds(t, 1), :], gates_cur[pl.ds(t, 1), :])
        peer[pl.ds(t, 1), :] = _expert_mix(chunk(s), coef, n_chunks // 2)

    def super_group(h, last):
        for k in range(group):
            t = h * group + k
            step(t, k, (not last) or k + ahead < group)
            if k < ROUTE_STAGES:
                _route_stage(k, h, q_scr, keys_ref, idx_v, gates_v, *route_state)

    def body(h, carry):
        super_group(h, False)
        return carry

    lax.fori_loop(0, PEER_HEADS - 1, body, 0)
    super_group(PEER_HEADS - 1, True)

    publish(1 - par)
    pl.when(i < pl.num_programs(0) - 1)(prime)
    _finish_block(x1_cur, mod_ref.at[:, 5:6, :], fg_ref, peer, out_ref)


def _tc_route_experts(ya, sga, gb, x, mod, norm2_g, w_a, w_out, w_query, sub_keys, final_g, table,
                      seq_len, first_tok):
    n, d = x.shape
    tb = EXPERT_TOK_BLOCK
    assert tb == POST_TOK_BLOCK and tb % EXPERT_SLOTS == 0 and seq_len % tb == 0 and first_tok % tb == 0
    first = first_tok // tb
    blocks_per_seq = seq_len // tb
    n_chunks = d // LANES
    aw = ya.shape[1]
    keys = sub_keys.reshape(2 * PEER_HEADS, N_KEYS, PEER_HALF)
    n_blocks = (n - first_tok) // tb
    nxt = lambda i: jnp.minimum(i + 1, n_blocks - 1) + first
    tok = lambda w: pl.BlockSpec((tb, w), lambda i: (i + first, 0))
    tok_nx = lambda w: pl.BlockSpec((tb, w), lambda i: (nxt(i), 0))
    mod_spec = lambda blk: pl.BlockSpec((1, 6, d), lambda i: (blk(i) // blocks_per_seq, 0, 0))
    full = lambda a: pl.BlockSpec(a.shape, lambda i: (0,) * a.ndim)
    fg = final_g.reshape(1, d)
    f32, i32 = jnp.float32, jnp.int32
    return pl.pallas_call(
        functools.partial(_route_experts_kernel, tok_block=tb, n_slots=EXPERT_SLOTS),
        grid=(n_blocks,),
        in_specs=[tok(aw), tok(d), tok(d), tok(d), mod_spec(lambda i: i + first),
                  tok_nx(aw), tok_nx(d), tok_nx(d), tok_nx(d), mod_spec(nxt),
                  full(norm2_g), full(w_a), full(w_out), full(w_query), full(keys), full(fg),
                  pl.BlockSpec(memory_space=pl.ANY)],
        out_specs=tok(d),
        out_shape=jax.ShapeDtypeStruct((n, d), f32),
        scratch_shapes=(
            [pltpu.VMEM((2 * PEER_HEADS, tb, PEER_HALF), f32), pltpu.VMEM((2, tb, d), f32),
             pltpu.VMEM((2, tb, d), f32), pltpu.VMEM((PEER_SLOTS, tb), i32), pltpu.VMEM((PEER_SLOTS, tb), f32),
             pltpu.VMEM((tb, PEER_SLOTS), i32), pltpu.VMEM((2, tb, PEER_SLOTS), f32),
             pltpu.SMEM((tb, PEER_SLOTS), i32), pltpu.SemaphoreType.DMA]
            + [pltpu.VMEM((PEER_SLOTS * n_chunks, LANES), jnp.uint32) for _ in range(EXPERT_SLOTS)]
            + [pltpu.VMEM((tb, d), f32), pltpu.SemaphoreType.DMA((EXPERT_SLOTS,))]
            + [pltpu.VMEM((2, N_KEYS, tb), f32), pltpu.VMEM((2, PEER_TOPK, tb), f32),
               pltpu.VMEM((2, PEER_TOPK, tb), f32),
               pltpu.VMEM((PEER_TOPK + (PEER_TOPK // 2 - 1) * (PEER_TOPK // 2) + PEER_TOPK // 2, tb), f32),
               pltpu.VMEM((2, PEER_TOPK, tb), f32)]),
        compiler_params=pltpu.CompilerParams(dimension_semantics=("arbitrary",),
                                             vmem_limit_bytes=56 * 1024 * 1024),
        cost_estimate=_expert_cost(n - first_tok, d),
        name="route_experts",
    )(ya, sga, gb, x, mod, ya, sga, gb, x, mod, norm2_g, w_a, w_out, w_query, keys, fg,
      table.reshape(-1, LANES))


def kernel(x, c, w_mod, b_mod, norm1_g, w_in, b_forget, ln_v_g, w_spatial, b_spatial, w_branch_a, w_branch_b, w_out, norm2_g, w_query, sub_keys, expert_u, expert_v, final_g):
    B, S, D = x.shape
    n = B * S
    bf16 = jnp.bfloat16
    assert w_mod.shape[0] == 1, "the final RMSNorm is fused into the single layer's expert kernels"
    l = 0
    mod = _modulation(c, w_mod[l], b_mod[l]).reshape(B, 6, D)
    table = _pack_expert_table(expert_u[l], expert_v[l])

    qkv, cum, cumt, sga, gb = _input_projection(x, mod, norm1_g[l], w_in[l], b_forget[l], ln_v_g[l],
                                                w_spatial[l], b_spatial[l], w_branch_b[l])
    y_a = _fox_attention(qkv, cum, cumt, B, S)
    post_args = (y_a, sga, gb, x.reshape(n, D), mod, norm2_g[l].reshape(1, D), w_branch_a[l].astype(bf16),
                 w_out[l].astype(bf16), w_query[l].astype(bf16), sub_keys[l])

    n_sc = n * SC_SHARE_PERCENT // 100 // SC_SHARE_ALIGN * SC_SHARE_ALIGN
    if n_sc > 0:
        x1_a, h2_a, idx_t, gates_t = _post_attention(*post_args, S, n_sc)
        peer_sc = _sc_peer_experts(table, idx_t.T.reshape(-1), gates_t.T.reshape(-1), h2_a)
    out = _tc_route_experts(*post_args, final_g, table, S, n_sc)
    if n_sc > 0:
        out = _finish_tokens(peer_sc.reshape(n_sc, D), x1_a, mod[:, 5:6, :], final_g.reshape(1, D), out, S)
    return out.reshape(B, S, D)
```

```python
import functools

import jax
import jax.numpy as jnp
from jax import lax
from jax.experimental import pallas as pl
from jax.experimental.pallas import tpu as pltpu
from jax.experimental.pallas import tpu_sc as plsc

D_MODEL = 1024
ATT_HEADS = 8
ATT_HEAD_DIM = 64
ATT_WIDTH = ATT_HEADS * ATT_HEAD_DIM
GM_GROUPS = 4
GM_GROUP_DIM = 128
GM_WIDTH = GM_GROUPS * GM_GROUP_DIM
GM_CHUNK = 128
PEER_HEADS = 8
PEER_KEY_DIM = 256
PEER_HALF = PEER_KEY_DIM // 2
N_KEYS = 128
PEER_TOPK = 16
PEER_SLOTS = PEER_HEADS * PEER_TOPK
SPLIT_POINTS = (ATT_WIDTH, 2 * ATT_WIDTH, 3 * ATT_WIDTH, 3 * ATT_WIDTH + ATT_HEADS,
                3 * ATT_WIDTH + ATT_HEADS + 2 * GM_WIDTH,
                3 * ATT_WIDTH + ATT_HEADS + 2 * GM_WIDTH + D_MODEL)
EPS = 1e-6

LANES = 128
EXPERT_TOK_BLOCK = 256
EXPERT_SLOTS = 8
DMA_THREADS = 2
FINISH_TOK_BLOCK = 256
PACK_EXPERT_BLOCK = 256
SC_GATHER_ROWS = 32
SC_ROW_GROUP = 4
SC_STORE_BATCH = 8
SC_SHARE_PERCENT = 52
SC_SHARE_ALIGN = 256


def _gelu(x):
    return 0.5 * x * (1.0 + jnp.tanh(0.7978845608028654 * (x + 0.044715 * (x * x * x))))


def _mod_kernel(c_ref, w_ref, b_ref, o_ref):
    c = c_ref[...]
    sc = c * jax.nn.sigmoid(c)
    o_ref[...] = jnp.dot(sc, w_ref[...], precision=lax.Precision.HIGHEST,
                         preferred_element_type=jnp.float32) + b_ref[...]


def _modulation(c, w_mod, b_mod):
    b, d = c.shape
    cols = w_mod.shape[1]
    return pl.pallas_call(
        _mod_kernel,
        grid=(cols // d,),
        in_specs=[pl.BlockSpec((b, d), lambda j: (0, 0)),
                  pl.BlockSpec((d, d), lambda j: (0, j)),
                  pl.BlockSpec((1, d), lambda j: (0, j))],
        out_specs=pl.BlockSpec((b, d), lambda j: (0, j)),
        out_shape=jax.ShapeDtypeStruct((b, cols), jnp.float32),
        name="modulation",
    )(c, w_mod, b_mod.reshape(1, cols))


INPROJ_TOK_BLOCK = 256


def _inproj_kernel(x_ref, mod_ref, n1g_ref, wqkv_ref, wf_ref, bf_ref, wz_ref, wg_ref, lng_ref,
                   wsp_ref, bsp_ref, wb_ref, qkv_ref, cum_ref, cumt_ref, sga_ref, gb_ref, carry):
    f32, bf16 = jnp.float32, jnp.bfloat16
    tm, d = x_ref.shape[1], x_ref.shape[2]
    x = x_ref[0]
    sh1 = mod_ref[0, 0:1, :]
    sc1 = mod_ref[0, 1:2, :]
    h = x * lax.rsqrt(jnp.mean(x * x, axis=-1, keepdims=True) + EPS) * n1g_ref[...]
    hb = (h * (1.0 + sc1) + sh1).astype(bf16)

    qkv = jnp.dot(hb, wqkv_ref[...], preferred_element_type=f32)
    qkv_ref[:, 0:ATT_WIDTH] = (qkv[:, 0:ATT_WIDTH] * (ATT_HEAD_DIM ** -0.5)).astype(bf16)
    qkv_ref[:, ATT_WIDTH:] = qkv[:, ATT_WIDTH:].astype(bf16)

    f = jnp.dot(hb, wf_ref[...], preferred_element_type=f32) + bf_ref[...]
    logf = jnp.minimum(f, 0.0) - jnp.log1p(jnp.exp(-jnp.abs(f)))

    @pl.when(pl.program_id(1) == 0)
    def _():
        carry[...] = jnp.zeros_like(carry)

    tri = (lax.broadcasted_iota(jnp.int32, (tm, tm), 0)
           >= lax.broadcasted_iota(jnp.int32, (tm, tm), 1)).astype(f32)
    cum = jnp.dot(tri, logf, precision=lax.Precision.HIGHEST, preferred_element_type=f32) + carry[...]
    cum_ref[...] = cum
    cumt_ref[0, 0] = jnp.transpose(cum)[0:ATT_HEADS, :]
    carry[...] = cum[tm - 1:tm, :]

    gz = _gelu(jnp.dot(hb, wz_ref[...], preferred_element_type=f32))
    u = gz[:, 0:GM_WIDTH]
    v = gz[:, GM_WIDTH:]
    mu = jnp.mean(v, axis=-1, keepdims=True)
    var = jnp.mean(jnp.square(v - mu), axis=-1, keepdims=True)
    vn = ((v - mu) * lax.rsqrt(var + EPS) * lng_ref[...]).astype(bf16)
    tril = (lax.broadcasted_iota(jnp.int32, (GM_CHUNK, GM_CHUNK), 0)
            >= lax.broadcasted_iota(jnp.int32, (GM_CHUNK, GM_CHUNK), 1))
    w_sp = [jnp.where(tril, wsp_ref[g], 0.0).astype(bf16) for g in range(GM_GROUPS)]
    rows = []
    for ck in range(tm // GM_CHUNK):
        r0 = ck * GM_CHUNK
        cols = []
        for g in range(GM_GROUPS):
            c0 = g * GM_GROUP_DIM
            mixed = jnp.dot(w_sp[g], vn[r0:r0 + GM_CHUNK, c0:c0 + GM_GROUP_DIM],
                            preferred_element_type=f32) + bsp_ref[g]
            cols.append(u[r0:r0 + GM_CHUNK, c0:c0 + GM_GROUP_DIM] * mixed)
        rows.append(jnp.concatenate(cols, axis=1))
    yb = jnp.concatenate(rows, axis=0).astype(bf16)
    ybp = jnp.dot(yb, wb_ref[...], preferred_element_type=f32)

    sg = jax.nn.sigmoid(jnp.dot(hb, wg_ref[...], preferred_element_type=f32))
    sga_ref[...] = sg[:, 0:d].astype(bf16)
    gb_ref[...] = (sg[:, d:] * ybp).astype(bf16)


def _input_projection(x, mod, norm1_g, w_in, b_forget, ln_v_g, w_spatial, b_spatial, w_branch_b):
    B, S, d = x.shape
    n = B * S
    tm = INPROJ_TOK_BLOCK
    bf16 = jnp.bfloat16
    p0, p1, p2, p3, p4, p5 = SPLIT_POINTS
    w_qkv = w_in[:, 0:p2].astype(bf16)
    w_f = jnp.pad(w_in[:, p2:p3], ((0, 0), (0, LANES - ATT_HEADS))).astype(bf16)
    b_f = jnp.pad(b_forget, (0, LANES - ATT_HEADS)).reshape(1, LANES)
    w_z = w_in[:, p3:p4].astype(bf16)
    w_g = w_in[:, p4:].astype(bf16)
    nt = S // tm
    tok = lambda w: pl.BlockSpec((tm, w), lambda b, i: (b * nt + i, 0))
    full = lambda a: pl.BlockSpec(a.shape, lambda b, i: (0,) * a.ndim)
    args = (x, mod, norm1_g.reshape(1, d), w_qkv, w_f, b_f, w_z, w_g, ln_v_g.reshape(1, GM_WIDTH),
            w_spatial, b_spatial.reshape(GM_GROUPS, GM_CHUNK, 1), w_branch_b.astype(bf16))
    return pl.pallas_call(
        _inproj_kernel,
        grid=(B, nt),
        in_specs=[pl.BlockSpec((1, tm, d), lambda b, i: (b, i, 0)),
                  pl.BlockSpec((1, 6, d), lambda b, i: (b, 0, 0))] + [full(a) for a in args[2:]],
        out_specs=[tok(3 * ATT_WIDTH), tok(LANES),
                   pl.BlockSpec((1, 1, ATT_HEADS, tm), lambda b, i: (b, i, 0, 0)), tok(d), tok(d)],
        out_shape=[jax.ShapeDtypeStruct((n, 3 * ATT_WIDTH), bf16),
                   jax.ShapeDtypeStruct((n, LANES), jnp.float32),
                   jax.ShapeDtypeStruct((B, nt, ATT_HEADS, tm), jnp.float32),
                   jax.ShapeDtypeStruct((n, d), bf16),
                   jax.ShapeDtypeStruct((n, d), bf16)],
        scratch_shapes=[pltpu.VMEM((1, LANES), jnp.float32)],
        compiler_params=pltpu.CompilerParams(dimension_semantics=("arbitrary", "arbitrary"),
                                             vmem_limit_bytes=56 * 1024 * 1024),
        name="input_projection",
    )(*args)


ATT_BLOCK = 1024


def _fox_kernel(q_ref, k_ref, v_ref, cum_ref, cumt_ref, o_ref, *, blk, ratio):
    f32 = jnp.float32
    hp = pl.program_id(1)
    i = pl.program_id(2)
    dh = ATT_HEAD_DIM
    q2 = q_ref[...]
    first = lax.broadcasted_iota(jnp.int32, (1, 2 * dh), 1) < dh
    qs = (jnp.where(first, q2, jnp.zeros_like(q2)), jnp.where(first, jnp.zeros_like(q2), q2))
    cum_blk = cum_ref[...]
    lane = lax.broadcasted_iota(jnp.int32, cum_blk.shape, 1)
    cqs = [jnp.sum(jnp.where(lane == 2 * hp + e, cum_blk, 0.0), axis=1, keepdims=True)
           for e in range(2)]

    def block(j, carry, masked):
        off = pl.multiple_of(j * blk, blk)
        k2 = k_ref[0, pl.ds(off, blk), :]
        v2 = v_ref[0, pl.ds(off, blk), :]
        out = []
        for e in range(2):
            m, acc = carry[e]
            ck = jnp.concatenate([cumt_ref[0, j * ratio + a, pl.ds(2 * hp + e, 1), :]
                                  for a in range(ratio)], axis=1)
            s = lax.dot_general(qs[e], k2, (((1,), (1,)), ((), ())), preferred_element_type=f32)
            s = s + (cqs[e] - ck)
            if masked:
                causal = (lax.broadcasted_iota(jnp.int32, (blk, blk), 0)
                          >= lax.broadcasted_iota(jnp.int32, (blk, blk), 1))
                s = jnp.where(causal, s, -jnp.inf)
            m_new = jnp.maximum(m, jnp.max(s, axis=1, keepdims=True))
            alpha = jnp.exp(m - m_new)
            p = jnp.exp(s - m_new).astype(v2.dtype)
            own = first if e == 0 else jnp.logical_not(first)
            v_ones = jnp.where(own, v2, jnp.ones_like(v2))
            acc = alpha * acc + jnp.dot(p, v_ones, preferred_element_type=f32)
            out.append((m_new, acc))
        return tuple(out)

    one = (jnp.full((blk, 1), -1e30, f32), jnp.zeros((blk, 2 * dh), f32))
    carry = lax.fori_loop(0, i, lambda j, c: block(j, c, False), (one, one))
    (_, acc0), (_, acc1) = block(i, carry, True)
    ratio0 = acc0 / pltpu.roll(acc0, dh, axis=1)
    ratio1 = acc1 / pltpu.roll(acc1, dh, axis=1)
    o_ref[...] = jnp.where(first, ratio0, ratio1).astype(o_ref.dtype)


def _fox_attention(qkv, cum, cumt, batch, seq_len):
    n = qkv.shape[0]
    tm = cumt.shape[-1]
    blk = max(min(ATT_BLOCK, seq_len), tm)
    nb = seq_len // blk
    pair = 2 * ATT_HEAD_DIM
    n_pairs = ATT_WIDTH // pair
    qkv3 = qkv.reshape(batch, seq_len, 3 * ATT_WIDTH)
    return pl.pallas_call(
        functools.partial(_fox_kernel, blk=blk, ratio=blk // tm),
        grid=(batch, n_pairs, nb),
        in_specs=[pl.BlockSpec((blk, pair), lambda b, hp, i: (b * nb + i, hp)),
                  pl.BlockSpec((1, seq_len, pair), lambda b, hp, i: (b, 0, n_pairs + hp)),
                  pl.BlockSpec((1, seq_len, pair), lambda b, hp, i: (b, 0, 2 * n_pairs + hp)),
                  pl.BlockSpec((blk, LANES), lambda b, hp, i: (b * nb + i, 0)),
                  pl.BlockSpec((1,) + cumt.shape[1:], lambda b, hp, i: (b, 0, 0, 0))],
        out_specs=pl.BlockSpec((blk, pair), lambda b, hp, i: (b * nb + i, hp)),
        out_shape=jax.ShapeDtypeStruct((n, ATT_WIDTH), jnp.bfloat16),
        compiler_params=pltpu.CompilerParams(
            dimension_semantics=("arbitrary", "arbitrary", "arbitrary")),
        name="fox_attention",
    )(qkv, qkv3, qkv3, cum, cumt)


def _topk_rows(scores, k):
    rows, t = scores[0].shape
    iota = lax.broadcasted_iota(jnp.int32, (rows, t), 0).astype(jnp.float32)
    slot = lax.broadcasted_iota(jnp.int32, (k, t), 0)
    scores = list(scores)
    vals = [jnp.zeros((k, t), jnp.float32) for _ in scores]
    ids = [jnp.zeros((k, t), jnp.float32) for _ in scores]
    for j in range(k):
        for i, s in enumerate(scores):
            m = jnp.max(s, axis=0, keepdims=True)
            am = jnp.min(jnp.where(s == m, iota, float(rows)), axis=0, keepdims=True)
            vals[i] = jnp.where(slot == j, m, vals[i])
            ids[i] = jnp.where(slot == j, am, ids[i])
            scores[i] = jnp.where(iota == am, -jnp.inf, s)
    return [(v, i.astype(jnp.int32)) for v, i in zip(vals, ids)]


def _select_rows(table, pos):
    out = jnp.zeros(pos.shape, table.dtype)
    for r in range(table.shape[0]):
        out = jnp.where(pos == r, table[r:r + 1, :], out)
    return out


def _post_dense(ya_ref, sga_ref, gb_ref, x_ref, mod_ref, n2g_ref, wa_ref, wo_ref, wq_ref,
                x1_ref, h2_ref, q_scr):
    f32 = jnp.float32
    a = jnp.dot(ya_ref[...], wa_ref[...], preferred_element_type=f32)
    merged = sga_ref[...].astype(f32) * a + gb_ref[...].astype(f32)
    o = jnp.dot(merged.astype(jnp.bfloat16), wo_ref[...], preferred_element_type=f32)
    g1 = mod_ref[0, 2:3, :]
    sh2 = mod_ref[0, 3:4, :]
    sc2 = mod_ref[0, 4:5, :]
    x1 = x_ref[...] + g1 * o
    x1_ref[...] = x1
    h2 = x1 * lax.rsqrt(jnp.mean(x1 * x1, axis=-1, keepdims=True) + EPS) * n2g_ref[...]
    h2 = h2 * (1.0 + sc2) + sh2
    h2_ref[...] = h2
    qp = jnp.dot(h2.astype(jnp.bfloat16), wq_ref[...], preferred_element_type=f32)
    for j in range(2 * PEER_HEADS):
        q_scr[j] = qp[:, j * PEER_HALF:(j + 1) * PEER_HALF]


def _route_heads(hs, q_scr, keys_ref, idx_ref, gates_ref):
    f32 = jnp.float32
    half = PEER_TOPK // 2
    tail0 = PEER_TOPK + (half - 1) * half
    scores = []
    for h in hs:
        for p in range(2):
            q = q_scr[2 * h + p]
            keys = keys_ref[2 * h + p]
            scores.append(lax.dot_general(keys, q, (((1,), (1,)), ((), ())),
                                          precision=lax.Precision.HIGHEST,
                                          preferred_element_type=f32))
    tops = _topk_rows(scores, PEER_TOPK)
    cands = []
    for e in range(len(hs)):
        (s1, _), (s2, _) = tops[2 * e], tops[2 * e + 1]
        blocks = [s1[0:1, :] + s2]
        blocks += [s1[a:a + 1, :] + s2[0:half, :] for a in range(1, half)]
        blocks += [s1[half:, :] + s2[0:1, :]]
        cands.append(jnp.concatenate(blocks, axis=0))
    best = _topk_rows(cands, PEER_TOPK)
    for e, h in enumerate(hs):
        (_, i1), (_, i2) = tops[2 * e], tops[2 * e + 1]
        vals, pos = best[e]
        mid = pos - PEER_TOPK
        ra = jnp.where(pos < PEER_TOPK, 0,
                       jnp.where(pos < tail0, 1 + (mid >> (half.bit_length() - 1)), pos - tail0 + half))
        rb = jnp.where(pos < PEER_TOPK, pos, jnp.where(pos < tail0, mid & (half - 1), 0))
        eid = _select_rows(i1, ra) * N_KEYS + _select_rows(i2, rb)
        ex = jnp.exp(vals - vals[0:1, :])
        g = ex / jnp.sum(ex, axis=0, keepdims=True)
        row = pl.multiple_of(h * PEER_TOPK, PEER_TOPK)
        idx_ref[pl.ds(row, PEER_TOPK), :] = eid
        gates_ref[pl.ds(row, PEER_TOPK), :] = g


ROUTE_STAGES = 27


def _stage_pick(s_ref):
    s = s_ref[...]
    rows = s.shape[0]
    iota = lax.broadcasted_iota(jnp.int32, s.shape, 0).astype(jnp.float32)
    m = jnp.max(s, axis=0, keepdims=True)
    am = jnp.min(jnp.where(s == m, iota, float(rows)), axis=0, keepdims=True)
    return m, am


def _route_stage_picks(k, rs_s, rs_c):
    if 1 <= k <= PEER_TOPK:
        return [_stage_pick(rs_s.at[p]) for p in range(2)]
    if PEER_TOPK + 2 <= k <= PEER_TOPK + 1 + PEER_TOPK // 2:
        return [_stage_pick(rs_c)]
    return None


def _route_stage(k, h, q_scr, keys_ref, idx_ref, gates_ref, rs_s, rs_v, rs_i, rs_c, rs_b, picks):
    f32 = jnp.float32
    half = PEER_TOPK // 2
    tail0 = PEER_TOPK + (half - 1) * half

    def drop(s_ref, v_ref, i_ref, j, pick):
        m, am = pick
        v_ref[pl.ds(j, 1), :] = m
        i_ref[pl.ds(j, 1), :] = am
        s = s_ref[...]
        iota = lax.broadcasted_iota(jnp.int32, s.shape, 0).astype(f32)
        s_ref[...] = jnp.where(iota == am, -jnp.inf, s)

    def extract(s_ref, v_ref, i_ref, j):
        drop(s_ref, v_ref, i_ref, j, _stage_pick(s_ref))

    if k == 0:
        for p in range(2):
            rs_s[p] = lax.dot_general(keys_ref[2 * h + p], q_scr[2 * h + p], (((1,), (1,)), ((), ())),
                                      precision=lax.Precision.HIGHEST, preferred_element_type=f32)
    elif k <= PEER_TOPK:
        for p in range(2):
            drop(rs_s.at[p], rs_v.at[p], rs_i.at[p], k - 1, picks[p])
    elif k == PEER_TOPK + 1:
        s1, s2 = rs_v[0], rs_v[1]
        blocks = [s1[0:1, :] + s2]
        blocks += [s1[a:a + 1, :] + s2[0:half, :] for a in range(1, half)]
        blocks += [s1[half:, :] + s2[0:1, :]]
        rs_c[...] = jnp.concatenate(blocks, axis=0)
    elif k <= PEER_TOPK + 1 + half:
        j = 2 * (k - PEER_TOPK - 2)
        drop(rs_c, rs_b.at[0], rs_b.at[1], j, picks[0])
        extract(rs_c, rs_b.at[0], rs_b.at[1], j + 1)
    elif k == ROUTE_STAGES - 1:
        i1, i2 = rs_i[0].astype(jnp.int32), rs_i[1].astype(jnp.int32)
        vals, pos = rs_b[0], rs_b[1].astype(jnp.int32)
        mid = pos - PEER_TOPK
        ra = jnp.where(pos < PEER_TOPK, 0,
                       jnp.where(pos < tail0, 1 + (mid >> (half.bit_length() - 1)), pos - tail0 + half))
        rb = jnp.where(pos < PEER_TOPK, pos, jnp.where(pos < tail0, mid & (half - 1), 0))
        eid = _select_rows(i1, ra) * N_KEYS + _select_rows(i2, rb)
        ex = jnp.exp(vals - vals[0:1, :])
        g = ex / jnp.sum(ex, axis=0, keepdims=True)
        row = pl.multiple_of(h * PEER_TOPK, PEER_TOPK)
        idx_ref[pl.ds(row, PEER_TOPK), :] = eid
        gates_ref[pl.ds(row, PEER_TOPK), :] = g


def _post_kernel(ya_ref, sga_ref, gb_ref, x_ref, mod_ref, n2g_ref, wa_ref, wo_ref, wq_ref, keys_ref,
                 x1_ref, h2_ref, idx_ref, gates_ref, q_scr):
    _post_dense(ya_ref, sga_ref, gb_ref, x_ref, mod_ref, n2g_ref, wa_ref, wo_ref, wq_ref,
                x1_ref, h2_ref, q_scr)

    def heads(hg, carry):
        _route_heads([hg * ROUTE_HEADS + e for e in range(ROUTE_HEADS)], q_scr, keys_ref, idx_ref, gates_ref)
        return carry

    lax.fori_loop(0, PEER_HEADS // ROUTE_HEADS, heads, 0)


POST_TOK_BLOCK = 256
ROUTE_HEADS = 4


def _post_attention(ya, sga, gb, x, mod, norm2_g, w_a, w_out, w_query, sub_keys, seq_len, n):
    d = x.shape[1]
    tm = POST_TOK_BLOCK
    assert n % tm == 0
    blocks_per_seq = seq_len // tm
    aw = ya.shape[1]
    keys = sub_keys.reshape(2 * PEER_HEADS, N_KEYS, PEER_HALF)
    tok = lambda w: pl.BlockSpec((tm, w), lambda i: (i, 0))
    full = lambda a: pl.BlockSpec(a.shape, lambda i: (0,) * a.ndim)
    return pl.pallas_call(
        _post_kernel,
        grid=(n // tm,),
        in_specs=[tok(aw), tok(d), tok(d), tok(d),
                  pl.BlockSpec((1, 6, d), lambda i: (i // blocks_per_seq, 0, 0)),
                  full(norm2_g), full(w_a), full(w_out), full(w_query), full(keys)],
        out_specs=[tok(d), tok(d),
                   pl.BlockSpec((PEER_SLOTS, tm), lambda i: (0, i)),
                   pl.BlockSpec((PEER_SLOTS, tm), lambda i: (0, i))],
        out_shape=[jax.ShapeDtypeStruct((n, d), jnp.float32),
                   jax.ShapeDtypeStruct((n, d), jnp.float32),
                   jax.ShapeDtypeStruct((PEER_SLOTS, n), jnp.int32),
                   jax.ShapeDtypeStruct((PEER_SLOTS, n), jnp.float32)],
        scratch_shapes=[pltpu.VMEM((2 * PEER_HEADS, tm, PEER_HALF), jnp.float32)],
        compiler_params=pltpu.CompilerParams(dimension_semantics=("arbitrary",),
                                             vmem_limit_bytes=48 * 1024 * 1024),
        name="post_attention",
    )(ya, sga, gb, x, mod, norm2_g, w_a, w_out, w_query, keys)


def _rowsum_bcast(p, ones_bf16):
    hi = p.astype(jnp.bfloat16)
    lo = (p - hi.astype(jnp.float32)).astype(jnp.bfloat16)
    return (jnp.dot(hi, ones_bf16, preferred_element_type=jnp.float32)
            + jnp.dot(lo, ones_bf16, preferred_element_type=jnp.float32))


def _word_halves(w):
    return (lax.bitcast_convert_type(w & jnp.uint32(0xFFFF0000), jnp.float32),
            lax.bitcast_convert_type(w << 16, jnp.float32))


def _expert_coef(chunk, hrow, grow):
    half = hrow.shape[1] // LANES // 2
    ones_bf16 = jnp.ones((LANES, LANES), jnp.bfloat16)
    eye = (lax.broadcasted_iota(jnp.int32, (PEER_SLOTS, LANES), 0)
           == lax.broadcasted_iota(jnp.int32, (PEER_SLOTS, LANES), 1))
    hpart = lambda c: hrow[:, c * LANES:(c + 1) * LANES]
    psum = None
    for c in range(half):
        hi, lo = _word_halves(chunk(c))
        p = hi * hpart(c) + lo * hpart(c + half)
        psum = p if psum is None else psum + p
    act = _gelu(_rowsum_bcast(psum, ones_bf16))
    gcol = _rowsum_bcast(jnp.where(eye, grow, 0.0), ones_bf16)
    return gcol * act


def _expert_mix(chunk, coef, half):
    outs_hi, outs_lo = [], []
    for c in range(half):
        hi, lo = _word_halves(chunk(c + half))
        outs_hi.append(jnp.sum(coef * hi, axis=0, keepdims=True))
        outs_lo.append(jnp.sum(coef * lo, axis=0, keepdims=True))
    return jnp.concatenate(outs_hi + outs_lo, axis=-1)


def _finish_block(x_ref, g2_ref, fg_ref, peer, out_ref):
    y = x_ref[...] + g2_ref[0] * peer[...]
    out_ref[...] = y * lax.rsqrt(jnp.mean(y * y, axis=-1, keepdims=True) + EPS) * fg_ref[...]


def _expert_cost(n_tok, d):
    pairs = n_tok * PEER_SLOTS
    return pl.CostEstimate(flops=4 * pairs * d, transcendentals=pairs,
                           bytes_accessed=4 * pairs * d + 12 * n_tok * d + 8 * pairs)


def _sc_tanh(y):
    return 1.0 - 2.0 / (jnp.exp(2.0 * y) + 1.0)


def _sc_peer_experts(table3, ids, gates, h2):
    n_tok = ids.shape[0] // PEER_SLOTS
    _, n_chunks, lanes = table3.shape
    assert h2.shape[1] == n_chunks * lanes
    info = plsc.get_sparse_core_info()
    sl = info.num_lanes
    n_workers = info.num_cores * info.num_subcores
    tok_per_worker = n_tok // n_workers
    assert tok_per_worker * n_workers == n_tok and tok_per_worker % 2 == 0
    rows = SC_GATHER_ROWS
    n_sub = PEER_SLOTS // rows
    assert n_sub % 2 == 0
    half = n_chunks // 2
    pieces = [(c, k * sl) for c in range(half) for k in range(lanes // sl)]
    mesh = plsc.VectorSubcoreMesh(core_axis_name="c", subcore_axis_name="s")
    buf = lambda dt: pltpu.VMEM((rows, n_chunks, lanes), dt)

    @functools.partial(
        pl.kernel, mesh=mesh,
        out_type=jax.ShapeDtypeStruct((n_tok, n_chunks, lanes), jnp.float32),
        scratch_types=[pltpu.VMEM((PEER_SLOTS,), jnp.int32), pltpu.VMEM((PEER_SLOTS,), jnp.int32),
                       pltpu.VMEM((PEER_SLOTS,), jnp.float32), pltpu.VMEM((PEER_SLOTS,), jnp.float32),
                       pltpu.VMEM((n_chunks * lanes,), jnp.float32),
                       pltpu.VMEM((n_chunks * lanes,), jnp.float32),
                       buf(jnp.uint32), buf(jnp.uint32), pltpu.VMEM((n_chunks, lanes), jnp.float32),
                       pltpu.SemaphoreType.DMA((2,)), pltpu.SemaphoreType.DMA((2,))],
        compiler_params=pltpu.CompilerParams(needs_layout_passes=False),
        cost_estimate=_expert_cost(n_tok, n_chunks * lanes),
        name="sc_peer_experts")
    def run(table_hbm, ids_hbm, gates_hbm, h_hbm, peer_hbm, ids0, ids1, g0, g1, h0, h1, rows0, rows1,
            out_v, gsem, isem):
        bufs = (rows0, rows1)
        ids_p, g_p, h_p = (ids0, ids1), (g0, g1), (h0, h1)
        t_first = (lax.axis_index("s") * info.num_cores + lax.axis_index("c")) * tok_per_worker

        def token_inputs(t, p):
            off = pl.multiple_of(t * PEER_SLOTS, PEER_SLOTS)
            return (pltpu.make_async_copy(ids_hbm.at[pl.ds(off, PEER_SLOTS)], ids_p[p], isem.at[p]),
                    pltpu.make_async_copy(gates_hbm.at[pl.ds(off, PEER_SLOTS)], g_p[p], isem.at[p]),
                    pltpu.make_async_copy(h_hbm.at[t], h_p[p], isem.at[p]))

        def gather(p, s):
            return pltpu.make_async_copy(table_hbm.at[ids_p[p].at[pl.ds(s * rows, rows)]],
                                         bufs[s % 2], gsem.at[s % 2])

        for cp in token_inputs(t_first, 0):
            cp.start()
        for cp in token_inputs(t_first, 0):
            cp.wait()
        gather(0, 0).start()

        def one_token(ti, p):
            t = t_first + ti
            g_v, h_v = g_p[p], h_p[p]
            has_next = ti + 1 < tok_per_worker

            @pl.when(has_next)
            def _():
                for cp in token_inputs(t + 1, 1 - p):
                    cp.start()

            for c, k in pieces:
                out_v[c, pl.ds(k, sl)] = jnp.zeros((sl,), jnp.float32)
                out_v[c + half, pl.ds(k, sl)] = jnp.zeros((sl,), jnp.float32)
            for s in range(n_sub):
                gather(p, s).wait()
                if s + 1 < n_sub:
                    gather(p, s + 1).start()
                else:
                    @pl.when(has_next)
                    def _():
                        for cp in token_inputs(t + 1, 1 - p):
                            cp.wait()
                        gather(1 - p, 0).start()
                rbuf = bufs[s % 2]

                @pl.loop(0, rows, step=SC_ROW_GROUP)
                def _(r0):
                    accs = [jnp.zeros((sl,), jnp.float32) for _ in range(SC_ROW_GROUP)]
                    for c, k in pieces:
                        h_hi = h_v[pl.ds(c * lanes + k, sl)]
                        h_lo = h_v[pl.ds((c + half) * lanes + k, sl)]
                        for j in range(SC_ROW_GROUP):
                            hi, lo = _word_halves(rbuf[r0 + j, c, pl.ds(k, sl)])
                            accs[j] = accs[j] + hi * h_hi + lo * h_lo
                    coefs = []
                    for j in range(SC_ROW_GROUP):
                        a = jnp.broadcast_to(jnp.sum(accs[j]), (sl,))
                        act = 0.5 * a * (1.0 + _sc_tanh(0.7978845608028654 * (a + 0.044715 * (a * a * a))))
                        gate = plsc.load_gather(g_v, [jnp.broadcast_to(s * rows + r0 + j, (sl,))])
                        coefs.append(gate * act)
                    for b0 in range(0, len(pieces), SC_STORE_BATCH):
                        tots = []
                        for c, k in pieces[b0:b0 + SC_STORE_BATCH]:
                            t_hi = t_lo = None
                            for j in range(SC_ROW_GROUP):
                                hi, lo = _word_halves(rbuf[r0 + j, c + half, pl.ds(k, sl)])
                                t_hi = coefs[j] * hi if t_hi is None else t_hi + coefs[j] * hi
                                t_lo = coefs[j] * lo if t_lo is None else t_lo + coefs[j] * lo
                            tots.append((t_hi, t_lo))
                        for (c, k), (t_hi, t_lo) in zip(pieces[b0:b0 + SC_STORE_BATCH], tots):
                            plsc.addupdate(out_v.at[c, pl.ds(k, sl)], t_hi)
                            plsc.addupdate(out_v.at[c + half, pl.ds(k, sl)], t_lo)

            pltpu.sync_copy(out_v, peer_hbm.at[t])

        @pl.loop(0, tok_per_worker, step=2)
        def _(ti):
            one_token(ti, 0)
            one_token(ti + 1, 1)

    return run(table3, ids, gates, h2)


def _finish_kernel(peer_ref, x_ref, g2_ref, fg_ref, prev_ref, out_ref):
    del prev_ref
    _finish_block(x_ref, g2_ref, fg_ref, peer_ref, out_ref)


def _finish_tokens(peer, x1, g2, fg, prev, seq_len):
    d = x1.shape[1]
    tm = FINISH_TOK_BLOCK
    return pl.pallas_call(
        _finish_kernel,
        grid=(peer.shape[0] // tm,),
        in_specs=[pl.BlockSpec((tm, d), lambda i: (i, 0)),
                  pl.BlockSpec((tm, d), lambda i: (i, 0)),
                  pl.BlockSpec((1, 1, d), lambda i: (i * tm // seq_len, 0, 0)),
                  pl.BlockSpec((1, d), lambda i: (0, 0)),
                  pl.BlockSpec(memory_space=pl.ANY)],
        out_specs=pl.BlockSpec((tm, d), lambda i: (i, 0)),
        out_shape=jax.ShapeDtypeStruct(prev.shape, prev.dtype),
        input_output_aliases={4: 0},
        compiler_params=pltpu.CompilerParams(dimension_semantics=("arbitrary",)),
        name="finish_tokens",
    )(peer, x1, g2, fg, prev)


def _pack_kernel(u_ref, v_ref, o_ref):
    te, d = u_ref.shape
    n_chunks = d // LANES

    def pairs(x):
        r = lax.bitcast_convert_type(x.astype(jnp.bfloat16).astype(jnp.float32), jnp.uint32)
        return r[:, :d // 2] | (r[:, d // 2:] >> 16)

    words = jnp.concatenate([pairs(u_ref[...]), pairs(v_ref[...])], axis=1)
    for c in range(n_chunks):
        o_ref[pl.ds(c, te, stride=n_chunks), :] = words[:, c * LANES:(c + 1) * LANES]


def _pack_expert_table(expert_u, expert_v):
    n_experts, d = expert_u.shape
    te = PACK_EXPERT_BLOCK
    n_chunks = d // LANES
    packed = pl.pallas_call(
        _pack_kernel,
        grid=(n_experts // te,),
        in_specs=[pl.BlockSpec((te, d), lambda i: (i, 0)), pl.BlockSpec((te, d), lambda i: (i, 0))],
        out_specs=pl.BlockSpec((te * n_chunks, LANES), lambda i: (i, 0)),
        out_shape=jax.ShapeDtypeStruct((n_experts * n_chunks, LANES), jnp.uint32),
        name="pack_experts",
    )(expert_u, expert_v)
    return packed.reshape(n_experts, n_chunks, LANES)


def _route_experts_kernel(ya_ref, sga_ref, gb_ref, x_ref, mod_ref, ya_nx, sga_nx, gb_nx, x_nx, mod_nx,
                          n2g_ref, wa_ref, wo_ref, wq_ref, keys_ref, fg_ref, tab_ref, out_ref,
                          q_scr, x1_v, h2_v, idx_v, gates_v, idx_tv, gates_tv, idx_s, isem,
                          *expert_scratch, tok_block, n_slots):
    bufs = expert_scratch[:n_slots]
    peer, sem = expert_scratch[n_slots], expert_scratch[n_slots + 1]
    route_state = expert_scratch[n_slots + 2:]
    i = pl.program_id(0)
    par = lax.rem(i, 2)
    n_chunks = x_ref.shape[-1] // LANES
    rows_per_tok = PEER_SLOTS * n_chunks
    group = tok_block // PEER_HEADS
    assert group % n_slots == 0 and n_slots - 1 < group and ROUTE_STAGES <= group

    def publish(p):
        idx_tv[...] = idx_v[...].T
        gates_tv[p] = gates_v[...].T
        cp = pltpu.make_async_copy(idx_tv, idx_s, isem)
        cp.start()
        cp.wait()

    @pl.when(i == 0)
    def _():
        _post_dense(ya_ref, sga_ref, gb_ref, x_ref, mod_ref, n2g_ref, wa_ref, wo_ref, wq_ref,
                    x1_v.at[0], h2_v.at[0], q_scr)

        def heads(hg, carry):
            _route_heads([hg * ROUTE_HEADS + e for e in range(ROUTE_HEADS)], q_scr, keys_ref, idx_v, gates_v)
            return carry

        lax.fori_loop(0, PEER_HEADS // ROUTE_HEADS, heads, 0)
        publish(0)

    x1_cur, h2_cur, gates_cur = x1_v.at[par], h2_v.at[par], gates_tv.at[par]
    _post_dense(ya_nx, sga_nx, gb_nx, x_nx, mod_nx, n2g_ref, wa_ref, wo_ref, wq_ref,
                x1_v.at[1 - par], h2_v.at[1 - par], q_scr)

    def issue(t, s):
        for r in range(PEER_SLOTS):
            row = pl.multiple_of(idx_s[t, r] * n_chunks, n_chunks)
            pltpu.make_async_copy(tab_ref.at[pl.ds(row, n_chunks), :],
                                  bufs[s].at[pl.ds(r * n_chunks, n_chunks), :],
                                  sem.at[s]).start(priority=r % DMA_THREADS)

    chunk = lambda s: (lambda c: bufs[s][pl.ds(c, PEER_SLOTS, stride=n_chunks), :])
    ahead = n_slots - 1

    def step(t, k, prefetch):
        s = k % n_slots
        pltpu.make_async_copy(tab_ref.at[pl.ds(0, rows_per_tok), :], bufs[s], sem.at[s]).wait()
        if prefetch:
            issue(t + ahead, (s - 1) % n_slots)
        coef = _expert_coef(chunk(s), h2_cur[pl.ds(t, 1), :], gates_cur[pl.ds(t, 1), :])
        peer[pl.ds(t, 1), :] = _expert_mix(chunk(s), coef, n_chunks // 2)

    for t in range(ahead):
        issue(t, t)

    def super_group(h, last):
        for k in range(group):
            t = h * group + k
            picks = _route_stage_picks(k, route_state[0], route_state[3])
            step(t, k, (not last) or k + ahead < group)
            if k < ROUTE_STAGES:
                _route_stage(k, h, q_scr, keys_ref, idx_v, gates_v, *route_state, picks)

    def body(h, carry):
        super_group(h, False)
        return carry

    lax.fori_loop(0, PEER_HEADS - 1, body, 0)
    super_group(PEER_HEADS - 1, True)

    _finish_block(x1_cur, mod_ref.at[:, 5:6, :], fg_ref, peer, out_ref)
    publish(1 - par)


def _tc_route_experts(ya, sga, gb, x, mod, norm2_g, w_a, w_out, w_query, sub_keys, final_g, table,
                      seq_len, first_tok):
    n, d = x.shape
    tb = EXPERT_TOK_BLOCK
    assert tb == POST_TOK_BLOCK and tb % EXPERT_SLOTS == 0 and seq_len % tb == 0 and first_tok % tb == 0
    first = first_tok // tb
    blocks_per_seq = seq_len // tb
    n_chunks = d // LANES
    aw = ya.shape[1]
    keys = sub_keys.reshape(2 * PEER_HEADS, N_KEYS, PEER_HALF)
    n_blocks = (n - first_tok) // tb
    nxt = lambda i: jnp.minimum(i + 1, n_blocks - 1) + first
    tok = lambda w: pl.BlockSpec((tb, w), lambda i: (i + first, 0))
    tok_nx = lambda w: pl.BlockSpec((tb, w), lambda i: (nxt(i), 0))
    mod_spec = lambda blk: pl.BlockSpec((1, 6, d), lambda i: (blk(i) // blocks_per_seq, 0, 0))
    full = lambda a: pl.BlockSpec(a.shape, lambda i: (0,) * a.ndim)
    fg = final_g.reshape(1, d)
    f32, i32 = jnp.float32, jnp.int32
    return pl.pallas_call(
        functools.partial(_route_experts_kernel, tok_block=tb, n_slots=EXPERT_SLOTS),
        grid=(n_blocks,),
        in_specs=[tok(aw), tok(d), tok(d), tok(d), mod_spec(lambda i: i + first),
                  tok_nx(aw), tok_nx(d), tok_nx(d), tok_nx(d), mod_spec(nxt),
                  full(norm2_g), full(w_a), full(w_out), full(w_query), full(keys), full(fg),
                  pl.BlockSpec(memory_space=pl.ANY)],
        out_specs=tok(d),
        out_shape=jax.ShapeDtypeStruct((n, d), f32),
        scratch_shapes=(
            [pltpu.VMEM((2 * PEER_HEADS, tb, PEER_HALF), f32), pltpu.VMEM((2, tb, d), f32),
             pltpu.VMEM((2, tb, d), f32), pltpu.VMEM((PEER_SLOTS, tb), i32), pltpu.VMEM((PEER_SLOTS, tb), f32),
             pltpu.VMEM((tb, PEER_SLOTS), i32), pltpu.VMEM((2, tb, PEER_SLOTS), f32),
             pltpu.SMEM((tb, PEER_SLOTS), i32), pltpu.SemaphoreType.DMA]
            + [pltpu.VMEM((PEER_SLOTS * n_chunks, LANES), jnp.uint32) for _ in range(EXPERT_SLOTS)]
            + [pltpu.VMEM((tb, d), f32), pltpu.SemaphoreType.DMA((EXPERT_SLOTS,))]
            + [pltpu.VMEM((2, N_KEYS, tb), f32), pltpu.VMEM((2, PEER_TOPK, tb), f32),
               pltpu.VMEM((2, PEER_TOPK, tb), f32),
               pltpu.VMEM((PEER_TOPK + (PEER_TOPK // 2 - 1) * (PEER_TOPK // 2) + PEER_TOPK // 2, tb), f32),
               pltpu.VMEM((2, PEER_TOPK, tb), f32)]),
        compiler_params=pltpu.CompilerParams(dimension_semantics=("arbitrary",),
                                             vmem_limit_bytes=56 * 1024 * 1024),
        cost_estimate=_expert_cost(n - first_tok, d),
        name="route_experts",
    )(ya, sga, gb, x, mod, ya, sga, gb, x, mod, norm2_g, w_a, w_out, w_query, keys, fg,
      table.reshape(-1, LANES))


def kernel(x, c, w_mod, b_mod, norm1_g, w_in, b_forget, ln_v_g, w_spatial, b_spatial, w_branch_a, w_branch_b, w_out, norm2_g, w_query, sub_keys, expert_u, expert_v, final_g):
    B, S, D = x.shape
    n = B * S
    bf16 = jnp.bfloat16
    assert w_mod.shape[0] == 1, "the final RMSNorm is fused into the single layer's expert kernels"
    l = 0
    mod = _modulation(c, w_mod[l], b_mod[l]).reshape(B, 6, D)
    table = _pack_expert_table(expert_u[l], expert_v[l])

    qkv, cum, cumt, sga, gb = _input_projection(x, mod, norm1_g[l], w_in[l], b_forget[l], ln_v_g[l],
                                                w_spatial[l], b_spatial[l], w_branch_b[l])
    y_a = _fox_attention(qkv, cum, cumt, B, S)
    post_args = (y_a, sga, gb, x.reshape(n, D), mod, norm2_g[l].reshape(1, D), w_branch_a[l].astype(bf16),
                 w_out[l].astype(bf16), w_query[l].astype(bf16), sub_keys[l])

    n_sc = n * SC_SHARE_PERCENT // 100 // SC_SHARE_ALIGN * SC_SHARE_ALIGN
    if n_sc > 0:
        x1_a, h2_a, idx_t, gates_t = _post_attention(*post_args, S, n_sc)
        peer_sc = _sc_peer_experts(table, idx_t.T.reshape(-1), gates_t.T.reshape(-1), h2_a)
    out = _tc_route_experts(*post_args, final_g, table, S, n_sc)
    if n_sc > 0:
        out = _finish_tokens(peer_sc.reshape(n_sc, D), x1_a, mod[:, 5:6, :], final_g.reshape(1, D), out, S)
    return out.reshape(B, S, D)
```
